```python
import math
import jax, jax.numpy as jnp
from jax import lax
import numpy as np

D_MODEL = 1024
BATCH = 8
SEQ = 2048
DEPTH = 2

N_MEM = 256
NORM_EPS = 1e-6
NEG_INF = -1e30
ATTN_BLOCK = 128

LRU_WIDTH = 1024
LRU_BLOCKS = 16
LRU_BLOCK_DIM = LRU_WIDTH // LRU_BLOCKS
CONV_WIDTH = 4
LRU_C = 8.0
LRU_A_MIN = 0.9
LRU_A_MAX = 0.999

DIL_GROUPS = ((128, 1), (512, 4), (2048, 16))
DIL_HEADS = 4
DIL_HEAD_DIM = 128
DIL_WIDTH = DIL_HEADS * DIL_HEAD_DIM
DIL_QKV = len(DIL_GROUPS) * DIL_WIDTH

MEM_HEADS = 4
MEM_HEAD_DIM = 64
MEM_WIDTH = MEM_HEADS * MEM_HEAD_DIM

NSA_HEADS = 16
NSA_KV_GROUPS = 2
NSA_HEADS_PER_GROUP = NSA_HEADS // NSA_KV_GROUPS
NSA_HEAD_DIM = 64
NSA_WIDTH = NSA_HEADS * NSA_HEAD_DIM
NSA_KV = NSA_KV_GROUPS * NSA_HEAD_DIM
CMP_BLOCK = 32
CMP_STRIDE = 16
SLC_BLOCK = 64
SLC_TOP_N = 8
WIN_SIZE = 512
PHI_HIDDEN = 256
SEL_FORCE = 1e6

N_EVEN = (DEPTH + 1) // 2
N_ODD = DEPTH // 2

HAWK_IN_SPLITS = (LRU_WIDTH, LRU_WIDTH, DIL_QKV, DIL_QKV, DIL_QKV, DIL_WIDTH, MEM_WIDTH, MEM_WIDTH)
HAWK_IN = sum(HAWK_IN_SPLITS)
HAWK_OUT = LRU_WIDTH + DIL_WIDTH + MEM_WIDTH
NSA_IN_SPLITS = (NSA_WIDTH, 6 * NSA_KV, 3 * NSA_HEADS, NSA_WIDTH, MEM_WIDTH, MEM_WIDTH)
NSA_IN = sum(NSA_IN_SPLITS)
NSA_OUT = NSA_WIDTH + MEM_WIDTH

kernel_name = "hybrid_rglru_dilated_nsa_memory"


def rms_norm(x, g):
    xf = x.astype(jnp.float32)
    y = xf * lax.rsqrt(jnp.mean(xf * xf, axis=-1, keepdims=True) + NORM_EPS)
    return (y * g.astype(jnp.float32)).astype(x.dtype)


def split_cols(h, sizes):
    return jnp.split(h, np.cumsum(sizes)[:-1].tolist(), axis=-1)


def alibi_slopes(n):
    return np.exp2(-8.0 * np.arange(1, n + 1) / n).astype(np.float32)


def banded_attention(q, k, v, slopes, max_dist, pos_scale):
    n, g, r, length, hd = q.shape
    blk = ATTN_BLOCK
    nb = -(-length // blk)
    pad_end = nb * blk - length
    p = -(-max_dist // blk)
    width = (p + 1) * blk
    q = jnp.pad(q, ((0, 0), (0, 0), (0, 0), (0, pad_end), (0, 0)))
    k = jnp.pad(k, ((0, 0), (0, 0), (p * blk, pad_end), (0, 0))).reshape(n, g, nb + p, blk, hd)
    v = jnp.pad(v, ((0, 0), (0, 0), (p * blk, pad_end), (0, 0))).reshape(n, g, nb + p, blk, hd)
    k_win = jnp.concatenate([k[:, :, i:i + nb] for i in range(p + 1)], axis=3).transpose(2, 0, 1, 3, 4)
    v_win = jnp.concatenate([v[:, :, i:i + nb] for i in range(p + 1)], axis=3).transpose(2, 0, 1, 3, 4)
    q_blk = q.reshape(n, g, r, nb, blk, hd).transpose(3, 0, 1, 2, 4, 5)
    dist = p * blk + np.arange(blk)[:, None] - np.arange(width)[None, :]
    key_idx = (np.arange(nb)[:, None] - p) * blk + np.arange(width)[None, :]
    valid = (dist >= 0)[None] & (dist <= max_dist)[None] & (key_idx >= 0)[:, None, :]
    bias = -(slopes[:, :, None, None] * (dist * pos_scale).astype(np.float32)[None, None])
    scale = hd ** -0.5

    def one_block(args):
        qb, kb, vb, ok = args
        s = jnp.einsum('ngrqd,ngkd->ngrqk', qb, kb, preferred_element_type=jnp.float32) * scale + bias
        s = jnp.where(ok, s, NEG_INF)
        m = jnp.max(s, axis=-1, keepdims=True)
        e = jnp.exp(s - m)
        den = jnp.sum(e, axis=-1, keepdims=True)
        o = jnp.einsum('ngrqk,ngkd->ngrqd', (e / den).astype(vb.dtype), vb,
                       preferred_element_type=jnp.float32)
        return o.astype(qb.dtype), (m + jnp.log(den))[..., 0]

    o, lse = lax.map(one_block, (q_blk, k_win, v_win, jnp.asarray(valid)))
    o = o.transpose(1, 2, 3, 0, 4, 5).reshape(n, g, r, nb * blk, hd)[:, :, :, :length]
    lse = lse.transpose(1, 2, 3, 0, 4).reshape(n, g, r, nb * blk)[..., :length]
    return o, lse


def dilated_attention(q, k, v):
    b, s, _, hd = q.shape
    slopes_all = alibi_slopes(len(DIL_GROUPS) * DIL_HEADS).reshape(len(DIL_GROUPS), DIL_HEADS)
    outs, lses = [], []
    for gi, (window, dil) in enumerate(DIL_GROUPS):
        sub = s // dil

        def to_sub(t):
            t = t[:, :, gi * DIL_HEADS:(gi + 1) * DIL_HEADS]
            return t.reshape(b, sub, dil, DIL_HEADS, hd).transpose(0, 2, 3, 1, 4).reshape(
                b * dil, DIL_HEADS, sub, hd)

        o, lse = banded_attention(to_sub(q)[:, :, None], to_sub(k), to_sub(v),
                                  slopes_all[gi][:, None], window // dil, dil)
        outs.append(o[:, :, 0].reshape(b, dil, DIL_HEADS, sub, hd).transpose(0, 3, 1, 2, 4).reshape(
            b, s, DIL_HEADS, hd))
        lses.append(lse[:, :, 0].reshape(b, dil, DIL_HEADS, sub).transpose(0, 3, 1, 2).reshape(
            b, s, DIL_HEADS))
    w = jax.nn.softmax(jnp.stack(lses, axis=0), axis=0)
    o = jnp.einsum('gbsh,gbshd->bshd', w, jnp.stack(outs, axis=0).astype(jnp.float32))
    return o.astype(q.dtype)


def causal_depthwise_conv(x, w, bias):
    y = lax.conv_general_dilated(x, w[:, None, :], window_strides=(1,),
                                 padding=((CONV_WIDTH - 1, 0),),
                                 dimension_numbers=('NWC', 'WIO', 'NWC'),
                                 feature_group_count=x.shape[-1])
    return y + bias


def rg_lru(x, gate_a_w, gate_a_b, gate_x_w, gate_x_b, lam):
    b, s, w = x.shape
    xb = x.reshape(b, s, LRU_BLOCKS, LRU_BLOCK_DIM)
    r = jax.nn.sigmoid((jnp.einsum('bshi,hij->bshj', xb, gate_a_w) + gate_a_b).astype(jnp.float32)).reshape(b, s, w)
    i = jax.nn.sigmoid((jnp.einsum('bshi,hij->bshj', xb, gate_x_w) + gate_x_b).astype(jnp.float32)).reshape(b, s, w)
    log_a = -LRU_C * r * jax.nn.softplus(-lam.astype(jnp.float32))
    a = jnp.exp(log_a)
    mult = jnp.sqrt(-jnp.expm1(2.0 * log_a))
    mult = jnp.where((jnp.arange(s) == 0)[None, :, None], 1.0, mult)
    u = x.astype(jnp.float32) * i * mult

    def combine(c1, c2):
        a1, b1 = c1
        a2, b2 = c2
        return a1 * a2, a2 * b1 + b2

    _, h = lax.associative_scan(combine, (a, u), axis=1)
    return h.astype(x.dtype)


def memory_attention(q, mem_n, w_mem_kv):
    b, s, _ = q.shape
    k, v = jnp.split(mem_n @ w_mem_kv, 2, axis=-1)
    q = q.reshape(b, s, MEM_HEADS, MEM_HEAD_DIM)
    k = k.reshape(b, -1, MEM_HEADS, MEM_HEAD_DIM)
    v = v.reshape(b, -1, MEM_HEADS, MEM_HEAD_DIM)
    sc = jnp.einsum('bshd,bmhd->bhsm', q, k, preferred_element_type=jnp.float32) * MEM_HEAD_DIM ** -0.5
    p = jax.nn.softmax(sc, axis=-1)
    o = jnp.einsum('bhsm,bmhd->bshd', p.astype(v.dtype), v)
    return o.reshape(b, s, MEM_WIDTH)


def hawk_dilated_layer(x, mem, norm_g, w_in, conv_w, conv_b, ga_w, ga_b, gx_w, gx_b, lam,
                       mem_norm_g, w_mem_kv, w_out):
    b, s, _ = x.shape
    h = rms_norm(x, norm_g) @ w_in
    xa, za, q, k, v, zb, qm, zm = split_cols(h, HAWK_IN_SPLITS)
    ya = rg_lru(causal_depthwise_conv(xa, conv_w, conv_b), ga_w, ga_b, gx_w, gx_b, lam)
    nh = len(DIL_GROUPS) * DIL_HEADS
    shp = (b, s, nh, DIL_HEAD_DIM)
    yb = dilated_attention(q.reshape(shp), k.reshape(shp), v.reshape(shp)).reshape(b, s, DIL_WIDTH)
    ym = memory_attention(qm, rms_norm(mem, mem_norm_g), w_mem_kv)
    y = jnp.concatenate([ya * jax.nn.silu(za), yb * jax.nn.silu(zb), ym * jax.nn.silu(zm)], axis=-1)
    return x + (y.astype(x.dtype) @ w_out)


def compress_tokens(t, pe, w1, w2):
    s, hd = t.shape[2], t.shape[3]
    n_cmp = (s - CMP_BLOCK) // CMP_STRIDE + 1
    idx = np.arange(n_cmp)[:, None] * CMP_STRIDE + np.arange(CMP_BLOCK)[None, :]
    blocks = t[:, :, idx] + pe
    flat = blocks.reshape(blocks.shape[0], blocks.shape[1], n_cmp, CMP_BLOCK * hd)
    return jax.nn.silu(flat @ w1) @ w2


def compressed_attention(q, k, v, slopes):
    s, hd = q.shape[3], q.shape[4]
    n_cmp = k.shape[2]
    block_end = np.arange(n_cmp) * CMP_STRIDE + CMP_BLOCK - 1
    dist = (np.arange(s)[:, None] - block_end[None, :]).astype(np.float32)
    visible = dist >= 0
    bias = -jnp.asarray(slopes)[:, :, None, None] * jnp.asarray(dist)
    sc = jnp.einsum('bgrsd,bgnd->bgrsn', q, k, preferred_element_type=jnp.float32) * hd ** -0.5 + bias
    sc = jnp.where(visible, sc, NEG_INF)
    p = jax.nn.softmax(sc, axis=-1) * visible.any(axis=-1, keepdims=True).astype(np.float32)
    o = jnp.einsum('bgrsn,bgnd->bgrsd', p.astype(v.dtype), v)
    return o, p


def selected_attention(q, k, v, p_cmp, slopes):
    b, g, r, s, hd = q.shape
    n_slc = s // SLC_BLOCK
    top_n = min(SLC_TOP_N, n_slc)
    n_cmp = p_cmp.shape[-1]
    c_start = np.arange(n_cmp)[:, None] * CMP_STRIDE
    s_start = np.arange(n_slc)[None, :] * SLC_BLOCK
    overlap = ((c_start < s_start + SLC_BLOCK) & (c_start + CMP_BLOCK > s_start)).astype(np.float32)
    imp = jnp.einsum('bgrsn,nj->bgsj', p_cmp, overlap)
    cur = (np.arange(s) // SLC_BLOCK)[:, None]
    j = np.arange(n_slc)[None, :]
    forced = (j == 0) | (j == cur) | (j == cur - 1)
    imp = jnp.where(forced, SEL_FORCE, jnp.where(j > cur, -SEL_FORCE, imp))
    _, sel = lax.top_k(imp, top_n)
    nqb = s // ATTN_BLOCK
    k_blocks = k.reshape(b, g, n_slc, SLC_BLOCK, hd)
    v_blocks = v.reshape(b, g, n_slc, SLC_BLOCK, hd)
    q_blk = q.reshape(b, g, r, nqb, ATTN_BLOCK, hd).transpose(3, 0, 1, 2, 4, 5)
    sel_blk = sel.reshape(b, g, nqb, ATTN_BLOCK, top_n).transpose(2, 0, 1, 3, 4)
    qpos_blk = jnp.arange(s, dtype=jnp.int32).reshape(nqb, ATTN_BLOCK)
    bi = jnp.arange(b)[:, None, None]
    gi = jnp.arange(g)[None, :, None]
    slope_b = jnp.asarray(slopes)[None, :, :, None, None]
    n_keys = top_n * SLC_BLOCK

    def one_block(args):
        qb, ib, qpos = args
        flat = ib.reshape(b, g, ATTN_BLOCK * top_n)
        kg = k_blocks[bi, gi, flat].reshape(b, g, ATTN_BLOCK, n_keys, hd)
        vg = v_blocks[bi, gi, flat].reshape(b, g, ATTN_BLOCK, n_keys, hd)
        kpos = (ib[..., None] * SLC_BLOCK + jnp.arange(SLC_BLOCK)).reshape(b, g, ATTN_BLOCK, n_keys)
        dist = (qpos[None, None, :, None] - kpos)[:, :, None]
        sc = jnp.einsum('bgrqd,bgqkd->bgrqk', qb, kg, preferred_element_type=jnp.float32) * hd ** -0.5
        sc = jnp.where(dist >= 0, sc - slope_b * dist.astype(jnp.float32), NEG_INF)
        p = jax.nn.softmax(sc, axis=-1)
        return jnp.einsum('bgrqk,bgqkd->bgrqd', p.astype(vg.dtype), vg)

    o = lax.map(one_block, (q_blk, sel_blk, qpos_blk))
    return o.transpose(1, 2, 3, 0, 4, 5).reshape(b, g, r, s, hd)


def nsa_layer(x, mem, norm_g, w_in, pe_k, pe_v, phik_w1, phik_w2, phiv_w1, phiv_w2,
              mem_norm_g, w_mem_kv, w_out):
    b, s, _ = x.shape
    g, r, hd = NSA_KV_GROUPS, NSA_HEADS_PER_GROUP, NSA_HEAD_DIM
    h = rms_norm(x, norm_g) @ w_in
    q, kv, gate_logits, z, qm, zm = split_cols(h, NSA_IN_SPLITS)
    q = q.reshape(b, s, g, r, hd).transpose(0, 2, 3, 1, 4)
    kc, vc, ks, vs, kw, vw = [t.reshape(b, s, g, hd).transpose(0, 2, 1, 3)
                              for t in jnp.split(kv, 6, axis=-1)]
    slopes = alibi_slopes(NSA_HEADS).reshape(g, r)
    k_cmp = compress_tokens(kc, pe_k, phik_w1, phik_w2)
    v_cmp = compress_tokens(vc, pe_v, phiv_w1, phiv_w2)
    o_cmp, p_cmp = compressed_attention(q, k_cmp, v_cmp, slopes)
    o_slc = selected_attention(q, ks, vs, p_cmp, slopes)
    o_win, _ = banded_attention(q, kw, vw, slopes, WIN_SIZE - 1, 1)
    gates = jax.nn.sigmoid(gate_logits.astype(jnp.float32)).reshape(b, s, g, r, 3).transpose(0, 2, 3, 1, 4)
    o = gates[..., 0:1] * o_cmp + gates[..., 1:2] * o_slc + gates[..., 2:3] * o_win
    o = o.transpose(0, 3, 1, 2, 4).reshape(b, s, NSA_WIDTH).astype(x.dtype)
    ym = memory_attention(qm, rms_norm(mem, mem_norm_g), w_mem_kv)
    y = jnp.concatenate([o * jax.nn.silu(z), ym * jax.nn.silu(zm)], axis=-1)
    return x + (y.astype(x.dtype) @ w_out)


def setup_inputs(seed: int = 0) -> dict:
    key = jax.random.key(seed)
    keys = iter(jax.random.split(key, 40))

    def nrm(shape, scale):
        return jax.random.normal(next(keys), shape, jnp.float32) * scale

    def gain(shape):
        return 1.0 + 0.02 * jax.random.normal(next(keys), shape, jnp.float32)

    x = nrm((BATCH, SEQ, D_MODEL), 1.0)
    mem = nrm((BATCH, N_MEM, D_MODEL), 1.0)
    u = jax.random.uniform(next(keys), (N_EVEN, LRU_WIDTH), jnp.float32, LRU_A_MIN, LRU_A_MAX)
    a_base = u ** (1.0 / LRU_C)
    hawk_lambda = jnp.log(a_base) - jnp.log1p(-a_base)
    phi_in = CMP_BLOCK * NSA_HEAD_DIM
    return {
        "x": x,
        "mem": mem,
        "hawk_norm": gain((N_EVEN, D_MODEL)),
        "hawk_w_in": nrm((N_EVEN, D_MODEL, HAWK_IN), D_MODEL ** -0.5),
        "hawk_conv_w": nrm((N_EVEN, CONV_WIDTH, LRU_WIDTH), CONV_WIDTH ** -0.5),
        "hawk_conv_b": nrm((N_EVEN, LRU_WIDTH), 0.02),
        "hawk_gate_a_w": nrm((N_EVEN, LRU_BLOCKS, LRU_BLOCK_DIM, LRU_BLOCK_DIM), LRU_BLOCK_DIM ** -0.5),
        "hawk_gate_a_b": nrm((N_EVEN, LRU_BLOCKS, LRU_BLOCK_DIM), 0.02),
        "hawk_gate_x_w": nrm((N_EVEN, LRU_BLOCKS, LRU_BLOCK_DIM, LRU_BLOCK_DIM), LRU_BLOCK_DIM ** -0.5),
        "hawk_gate_x_b": nrm((N_EVEN, LRU_BLOCKS, LRU_BLOCK_DIM), 0.02),
        "hawk_lambda": hawk_lambda,
        "hawk_mem_norm": gain((N_EVEN, D_MODEL)),
        "hawk_w_mem_kv": nrm((N_EVEN, D_MODEL, 2 * MEM_WIDTH), D_MODEL ** -0.5),
        "hawk_w_out": nrm((N_EVEN, HAWK_OUT, D_MODEL), HAWK_OUT ** -0.5),
        "nsa_norm": gain((N_ODD, D_MODEL)),
        "nsa_w_in": nrm((N_ODD, D_MODEL, NSA_IN), D_MODEL ** -0.5),
        "nsa_pe_k": nrm((N_ODD, CMP_BLOCK, NSA_HEAD_DIM), 0.1),
        "nsa_pe_v": nrm((N_ODD, CMP_BLOCK, NSA_HEAD_DIM), 0.1),
        "nsa_phi_k_w1": nrm((N_ODD, phi_in, PHI_HIDDEN), phi_in ** -0.5),
        "nsa_phi_k_w2": nrm((N_ODD, PHI_HIDDEN, NSA_HEAD_DIM), PHI_HIDDEN ** -0.5),
        "nsa_phi_v_w1": nrm((N_ODD, phi_in, PHI_HIDDEN), phi_in ** -0.5),
        "nsa_phi_v_w2": nrm((N_ODD, PHI_HIDDEN, NSA_HEAD_DIM), PHI_HIDDEN ** -0.5),
        "nsa_mem_norm": gain((N_ODD, D_MODEL)),
        "nsa_w_mem_kv": nrm((N_ODD, D_MODEL, 2 * MEM_WIDTH), D_MODEL ** -0.5),
        "nsa_w_out": nrm((N_ODD, NSA_OUT, D_MODEL), NSA_OUT ** -0.5),
        "final_norm": gain((D_MODEL,)),
    }


def reference(x, mem, hawk_norm, hawk_w_in, hawk_conv_w, hawk_conv_b, hawk_gate_a_w, hawk_gate_a_b,
              hawk_gate_x_w, hawk_gate_x_b, hawk_lambda, hawk_mem_norm, hawk_w_mem_kv, hawk_w_out,
              nsa_norm, nsa_w_in, nsa_pe_k, nsa_pe_v, nsa_phi_k_w1, nsa_phi_k_w2, nsa_phi_v_w1,
              nsa_phi_v_w2, nsa_mem_norm, nsa_w_mem_kv, nsa_w_out, final_norm):
    for layer in range(DEPTH):
        i = layer // 2
        if layer % 2 == 0:
            x = hawk_dilated_layer(x, mem, hawk_norm[i], hawk_w_in[i], hawk_conv_w[i], hawk_conv_b[i],
                                   hawk_gate_a_w[i], hawk_gate_a_b[i], hawk_gate_x_w[i], hawk_gate_x_b[i],
                                   hawk_lambda[i], hawk_mem_norm[i], hawk_w_mem_kv[i], hawk_w_out[i])
        else:
            x = nsa_layer(x, mem, nsa_norm[i], nsa_w_in[i], nsa_pe_k[i], nsa_pe_v[i], nsa_phi_k_w1[i],
                          nsa_phi_k_w2[i], nsa_phi_v_w1[i], nsa_phi_v_w2[i], nsa_mem_norm[i],
                          nsa_w_mem_kv[i], nsa_w_out[i])
    return rms_norm(x, final_norm)
```

```python
import functools

import numpy as np
import jax
import jax.numpy as jnp
from jax import lax
from jax.experimental import pallas as pl
from jax.experimental.pallas import tpu as pltpu

F32 = jnp.float32
BF16 = jnp.bfloat16

NORM_EPS = 1e-6
NEG_INF = -1e30
LANES = 128
ATTN_BLOCK = 128
VMEM_LIMIT = 56 * 1024 * 1024

LRU_WIDTH = 1024
LRU_BLOCKS = 16
LRU_BLOCK_DIM = LRU_WIDTH // LRU_BLOCKS
LRU_PACK = 256
CONV_WIDTH = 4
LRU_C = 8.0

DIL_GROUPS = ((128, 1), (512, 4), (2048, 16))
DIL_HEADS = 4
DIL_HEAD_DIM = 128
DIL_WIDTH = DIL_HEADS * DIL_HEAD_DIM
DIL_QKV = len(DIL_GROUPS) * DIL_WIDTH
LSE_LANES = LANES // DIL_HEADS

MEM_HEADS = 4
MEM_HEAD_DIM = 64
MEM_WIDTH = MEM_HEADS * MEM_HEAD_DIM

NSA_HEADS = 16
NSA_KV_GROUPS = 2
NSA_R = NSA_HEADS // NSA_KV_GROUPS
NSA_HEAD_DIM = 64
NSA_WIDTH = NSA_HEADS * NSA_HEAD_DIM
NSA_KV = NSA_KV_GROUPS * NSA_HEAD_DIM
CMP_BLOCK = 32
CMP_STRIDE = 16
SLC_BLOCK = 64
SLC_TOP_N = 8
WIN_SIZE = 512
PHI_HIDDEN = 256
SEL_FORCE = 1e6
CMP_PER_SLC = SLC_BLOCK // CMP_STRIDE


def _alibi_slopes(n):
    return [float(v) for v in np.exp2(-8.0 * np.arange(1, n + 1) / n).astype(np.float32)]


def _params(*semantics):
    return pltpu.CompilerParams(dimension_semantics=semantics, vmem_limit_bytes=VMEM_LIMIT)


def _silu(z):
    return z * jax.nn.sigmoid(z)


def _dot_t(a, b):
    return lax.dot_general(a, b, (((1,), (1,)), ((), ())), preferred_element_type=F32)


def _norm_matmul_kernel(x_ref, g_ref, w_ref, o_ref):
    x = x_ref[...]
    ms = jnp.mean(x * x, axis=-1, keepdims=True)
    xn = (x * lax.rsqrt(ms + NORM_EPS) * g_ref[...]).astype(BF16)
    o_ref[...] = jnp.dot(xn, w_ref[...], preferred_element_type=F32).astype(o_ref.dtype)


def _norm_matmul(x, g, w, out_dtype, dil=1, tm=256):
    m, k = x.shape
    n = w.shape[1]
    rows = m // dil
    tm = min(tm, rows)
    return pl.pallas_call(
        _norm_matmul_kernel,
        grid=(dil, rows // tm),
        in_specs=[pl.BlockSpec((tm, k), lambda c, i: (i, c)),
                  pl.BlockSpec((1, k), lambda c, i: (0, 0)),
                  pl.BlockSpec((k, n), lambda c, i: (0, 0))],
        out_specs=pl.BlockSpec((None, tm, n), lambda c, i: (c, i, 0)),
        out_shape=jax.ShapeDtypeStruct((dil, rows, n), out_dtype),
        compiler_params=_params("arbitrary", "arbitrary"),
        name="norm_matmul",
    )(x.reshape(rows, dil * k), g.reshape(1, k), w)


def _shift_rows(x, k, fill, row):
    return jnp.where(row < k, fill, pltpu.roll(x, k, axis=0))


def _rglru_kernel(xa_ref, za_ref, cw_ref, cb_ref, wa_ref, ba_ref, wx_ref, bx_ref, lam_ref,
                  o_ref, xpad_ref, h_ref):
    t = pl.program_id(1)
    tt, width = xa_ref.shape
    halo = 8

    @pl.when(t == 0)
    def _():
        xpad_ref[0:halo, :] = jnp.zeros((halo, width), F32)
        h_ref[...] = jnp.zeros_like(h_ref)

    x = xa_ref[...]
    xpad_ref[halo:halo + tt, :] = x
    cw = cw_ref[...]
    y = cw[CONV_WIDTH - 1:CONV_WIDTH] * x
    for k in range(1, CONV_WIDTH):
        y = y + cw[CONV_WIDTH - 1 - k:CONV_WIDTH - k] * xpad_ref[halo - k:halo - k + tt, :]
    y = y + cb_ref[...]
    xpad_ref[0:halo, :] = x[tt - halo:tt, :]

    yb = y.astype(BF16)
    r_parts, i_parts = [], []
    for p in range(width // LRU_PACK):
        ys = yb[:, p * LRU_PACK:(p + 1) * LRU_PACK]
        r_parts.append(jnp.dot(ys, wa_ref[p], preferred_element_type=F32))
        i_parts.append(jnp.dot(ys, wx_ref[p], preferred_element_type=F32))
    r = jax.nn.sigmoid(jnp.concatenate(r_parts, axis=1) + ba_ref[...])
    gi = jax.nn.sigmoid(jnp.concatenate(i_parts, axis=1) + bx_ref[...])

    nl = -lam_ref[...]
    softplus = jnp.maximum(nl, 0.0) + jnp.log1p(jnp.exp(-jnp.abs(nl)))
    log_a = (-LRU_C) * r * softplus
    a = jnp.exp(log_a)
    z2 = 2.0 * log_a
    u = a * a
    one_minus = jnp.where(u == 1.0, -z2, jnp.where(u < 0.5, 1.0 - u, (1.0 - u) * z2 / jnp.log(u)))
    mult = jnp.sqrt(one_minus)
    row = lax.broadcasted_iota(jnp.int32, (tt, width), 0)
    mult = jnp.where((row + t * tt) == 0, 1.0, mult)
    b = y * gi * mult

    k = 1
    while k < tt:
        a_sh = _shift_rows(a, k, 1.0, row)
        b_sh = _shift_rows(b, k, 0.0, row)
        b = a * b_sh + b
        a = a * a_sh
        k *= 2
    h = a * h_ref[...] + b
    h_ref[...] = h[tt - 1:tt, :]
    o_ref[...] = (h * _silu(za_ref[...])).astype(o_ref.dtype)


def _rglru(hf, conv_w, conv_b, wa, ba, wx, bx, lam, batch, seq, tt=256):
    width = LRU_WIDTH
    nt = seq // tt
    packs = width // LRU_PACK
    vec = pl.BlockSpec((1, width), lambda b, t: (0, 0))
    gate_w = pl.BlockSpec((packs, LRU_PACK, LRU_PACK), lambda b, t: (0, 0, 0))
    return pl.pallas_call(
        _rglru_kernel,
        grid=(batch, nt),
        in_specs=[pl.BlockSpec((tt, width), lambda b, t: (b * nt + t, 0)),
                  pl.BlockSpec((tt, width), lambda b, t: (b * nt + t, 1)),
                  pl.BlockSpec((CONV_WIDTH, width), lambda b, t: (0, 0)),
                  vec, gate_w, vec, gate_w, vec, vec],
        out_specs=pl.BlockSpec((tt, width), lambda b, t: (b * nt + t, 0)),
        out_shape=jax.ShapeDtypeStruct((batch * seq, width), BF16),
        scratch_shapes=[pltpu.VMEM((tt + 8, width), F32), pltpu.VMEM((1, width), F32)],
        compiler_params=_params("arbitrary", "arbitrary"),
        name="rglru",
    )(hf, hf, conv_w, conv_b.reshape(1, width), wa, ba.reshape(1, width), wx, bx.reshape(1, width),
      lam.reshape(1, width))


def _pack_block_diag(w):
    per = LRU_PACK // LRU_BLOCK_DIM
    w = w.reshape(LRU_BLOCKS // per, per, LRU_BLOCK_DIM, LRU_BLOCK_DIM)
    eye = jnp.eye(per, dtype=w.dtype)
    packed = w[:, :, :, None, :] * eye[None, :, None, :, None]
    return packed.reshape(LRU_BLOCKS // per, LRU_PACK, LRU_PACK).astype(BF16)


def _dil_attn_kernel(*refs, slopes, pos_scale, max_dist, use_prev):
    if use_prev:
        q_ref, kp_ref, kc_ref, vp_ref, vc_ref, o_ref, l_ref = refs
    else:
        q_ref, kc_ref, vc_ref, o_ref, l_ref = refs
    i = pl.program_id(2)
    blk = q_ref.shape[0]
    width = 2 * blk if use_prev else blk
    row = lax.broadcasted_iota(jnp.int32, (blk, width), 0)
    col = lax.broadcasted_iota(jnp.int32, (blk, width), 1)
    dist = (width - blk) + row - col
    valid = (dist >= 0) & (dist <= max_dist)
    if use_prev:
        valid = valid & ((col >= blk) | (i > 0))
    distf = (dist * pos_scale).astype(F32)
    scale = DIL_HEAD_DIM ** -0.5
    for h in range(DIL_HEADS):
        hs = slice(h * DIL_HEAD_DIM, (h + 1) * DIL_HEAD_DIM)
        q = q_ref[:, hs]
        if use_prev:
            k = jnp.concatenate([kp_ref[:, hs], kc_ref[:, hs]], axis=0)
            v = jnp.concatenate([vp_ref[:, hs], vc_ref[:, hs]], axis=0)
        else:
            k, v = kc_ref[:, hs], vc_ref[:, hs]
        s = _dot_t(q, k) * scale - slopes[h] * distf
        s = jnp.where(valid, s, NEG_INF)
        m = jnp.max(s, axis=-1, keepdims=True)
        e = jnp.exp(s - m)
        den = jnp.sum(e, axis=-1, keepdims=True)
        o_ref[:, hs] = jnp.dot(e.astype(BF16), v, preferred_element_type=F32) / den
        l_ref[:, h * LSE_LANES:(h + 1) * LSE_LANES] = jnp.broadcast_to(m + jnp.log(den), (blk, LSE_LANES))


def _dil_attn(qkv, gi, batch, seq):
    window, dil = DIL_GROUPS[gi]
    sub = seq // dil
    nb = sub // ATTN_BLOCK
    use_prev = nb > 1
    slopes = _alibi_slopes(len(DIL_GROUPS) * DIL_HEADS)[gi * DIL_HEADS:(gi + 1) * DIL_HEADS]
    blk = (None, ATTN_BLOCK, DIL_WIDTH)
    cur = lambda col: pl.BlockSpec(blk, lambda c, b, i: (c, b * nb + i, col))
    prev = lambda col: pl.BlockSpec(blk, lambda c, b, i: (c, b * nb + jnp.maximum(i - 1, 0), col))
    if use_prev:
        in_specs = [cur(0), prev(1), cur(1), prev(2), cur(2)]
        args = (qkv,) * 5
    else:
        in_specs = [cur(0), cur(1), cur(2)]
        args = (qkv,) * 3
    o, lse = pl.pallas_call(
        functools.partial(_dil_attn_kernel, slopes=slopes, pos_scale=dil, max_dist=window // dil,
                          use_prev=use_prev),
        grid=(dil, batch, nb),
        in_specs=in_specs,
        out_specs=[pl.BlockSpec((ATTN_BLOCK, DIL_WIDTH), lambda c, b, i: (b * nb + i, c)),
                   pl.BlockSpec((ATTN_BLOCK, LANES), lambda c, b, i: (b * nb + i, c))],
        out_shape=[jax.ShapeDtypeStruct((batch * sub, dil * DIL_WIDTH), F32),
                   jax.ShapeDtypeStruct((batch * sub, dil * LANES), F32)],
        compiler_params=_params("arbitrary", "arbitrary", "arbitrary"),
        name=f"dil_attn_d{dil}",
    )(*args)
    return o.reshape(batch * seq, DIL_WIDTH), lse.reshape(batch * seq, LANES)


def _mem_attn_kernel(q_ref, k_ref, v_ref, z_ref, o_ref):
    scale = MEM_HEAD_DIM ** -0.5
    outs = []
    for h in range(MEM_HEADS):
        hs = slice(h * MEM_HEAD_DIM, (h + 1) * MEM_HEAD_DIM)
        s = _dot_t(q_ref[:, hs], k_ref[:, hs]) * scale
        m = jnp.max(s, axis=-1, keepdims=True)
        e = jnp.exp(s - m)
        den = jnp.sum(e, axis=-1, keepdims=True)
        outs.append(jnp.dot(e.astype(BF16), v_ref[:, hs], preferred_element_type=F32) / den)
    o_ref[...] = (jnp.concatenate(outs, axis=1) * _silu(z_ref[...])).astype(o_ref.dtype)


def _mem_attn(qsrc, q_col, kv, zsrc, z_col, batch, seq, n_mem, tq=512):
    nq = seq // tq
    return pl.pallas_call(
        _mem_attn_kernel,
        grid=(batch, nq),
        in_specs=[pl.BlockSpec((tq, MEM_WIDTH), lambda b, i: (b * nq + i, q_col)),
                  pl.BlockSpec((n_mem, MEM_WIDTH), lambda b, i: (b, 0)),
                  pl.BlockSpec((n_mem, MEM_WIDTH), lambda b, i: (b, 1)),
                  pl.BlockSpec((tq, MEM_WIDTH), lambda b, i: (b * nq + i, z_col))],
        out_specs=pl.BlockSpec((tq, MEM_WIDTH), lambda b, i: (b * nq + i, 0)),
        out_shape=jax.ShapeDtypeStruct((batch * seq, MEM_WIDTH), BF16),
        compiler_params=_params("arbitrary", "arbitrary"),
        name="mem_attn",
    )(qsrc, kv, kv, zsrc)


def _hawk_out_kernel(ya_ref, o0_ref, o1_ref, o2_ref, l0_ref, l1_ref, l2_ref, zb_ref, ym_ref, w_ref,
                     x_ref, out_ref):
    o_refs = (o0_ref, o1_ref, o2_ref)
    l_refs = (l0_ref, l1_ref, l2_ref)
    parts = []
    for h in range(DIL_HEADS):
        hs = slice(h * DIL_HEAD_DIM, (h + 1) * DIL_HEAD_DIM)
        ls = [l[:, h * LSE_LANES:h * LSE_LANES + 1] for l in l_refs]
        m = jnp.maximum(jnp.maximum(ls[0], ls[1]), ls[2])
        ws = [jnp.exp(l - m) for l in ls]
        num = ws[0] * o_refs[0][:, hs] + ws[1] * o_refs[1][:, hs] + ws[2] * o_refs[2][:, hs]
        parts.append(num / (ws[0] + ws[1] + ws[2]))
    yb = (jnp.concatenate(parts, axis=1) * _silu(zb_ref[...])).astype(BF16)
    a_end = LRU_WIDTH
    b_end = a_end + DIL_WIDTH
    y = jnp.dot(ya_ref[...], w_ref[0:a_end, :], preferred_element_type=F32)
    y = y + jnp.dot(yb, w_ref[a_end:b_end, :], preferred_element_type=F32)
    y = y + jnp.dot(ym_ref[...], w_ref[b_end:b_end + MEM_WIDTH, :], preferred_element_type=F32)
    out_ref[...] = x_ref[...] + y


def _hawk_out(ya, os_, ls_, hf, zb_col, ym, w, x, tm=256):
    m, d = x.shape
    row = lambda width, col=0: pl.BlockSpec((tm, width), lambda i: (i, col))
    return pl.pallas_call(
        _hawk_out_kernel,
        grid=(m // tm,),
        in_specs=[row(LRU_WIDTH)] + [row(DIL_WIDTH)] * 3 + [row(LANES)] * 3
                 + [row(DIL_WIDTH, zb_col), row(MEM_WIDTH),
                    pl.BlockSpec(w.shape, lambda i: (0, 0)), row(d)],
        out_specs=row(d),
        out_shape=jax.ShapeDtypeStruct((m, d), F32),
        compiler_params=_params("arbitrary"),
        name="hawk_out",
    )(ya, *os_, *ls_, hf, ym, w, x)


def _compress_kernel(x_ref, pe_ref, w1_ref, w2_ref, o_ref):
    x = x_ref[...]
    n_blk = x.shape[0]
    first = jnp.dot(x, w1_ref[0], preferred_element_type=F32)
    second = jnp.dot(x, w1_ref[1], preferred_element_type=F32)
    pe = (jnp.dot(pe_ref[0], w1_ref[0], preferred_element_type=F32)
          + jnp.dot(pe_ref[1], w1_ref[1], preferred_element_type=F32))
    hid = first + pltpu.roll(second, n_blk - 1, axis=0) + pe[0:1, :]
    act = _silu(hid).astype(BF16)
    outs = []
    for which in range(2):
        for g in range(NSA_KV_GROUPS):
            c = which * NSA_KV_GROUPS + g
            outs.append(jnp.dot(act[:, c * PHI_HIDDEN:(c + 1) * PHI_HIDDEN], w2_ref[which],
                                preferred_element_type=F32))
    o_ref[...] = jnp.concatenate(outs, axis=1).astype(o_ref.dtype)


def _compress(kcv, pe_k, pe_v, k_w1, k_w2, v_w1, v_w2, batch, seq):
    half = CMP_BLOCK // 2
    assert half == CMP_STRIDE
    n_blk = seq // CMP_STRIDE
    cols = 2 * NSA_KV
    x = kcv.reshape(batch, n_blk, half * cols)
    w1 = jnp.stack([k_w1, v_w1]).reshape(2, 2, half, NSA_HEAD_DIM, PHI_HIDDEN)
    eye = jnp.eye(2, dtype=w1.dtype)
    sel = eye[:, None, :, None] * eye[None, :, None, :]
    w1e = w1.transpose(1, 2, 0, 3, 4)[:, :, :, None, :, None, None, :] * sel[None, None, :, :, None, :, :, None]
    w1e = w1e.reshape(2, half * cols, 2 * NSA_KV_GROUPS * PHI_HIDDEN).astype(BF16)
    pe = jnp.stack([pe_k, pe_v]).reshape(2, 2, half, 1, NSA_HEAD_DIM)
    pe = jnp.broadcast_to(pe.transpose(1, 2, 0, 3, 4), (2, half, 2, NSA_KV_GROUPS, NSA_HEAD_DIM))
    pe = jnp.broadcast_to(pe.reshape(2, 1, half * cols), (2, 8, half * cols)).astype(BF16)
    w2 = jnp.stack([k_w2, v_w2]).astype(BF16)
    return pl.pallas_call(
        _compress_kernel,
        grid=(batch,),
        in_specs=[pl.BlockSpec((None, n_blk, half * cols), lambda b: (b, 0, 0)),
                  pl.BlockSpec(pe.shape, lambda b: (0, 0, 0)),
                  pl.BlockSpec(w1e.shape, lambda b: (0, 0, 0)),
                  pl.BlockSpec(w2.shape, lambda b: (0, 0, 0))],
        out_specs=pl.BlockSpec((None, n_blk, cols), lambda b: (b, 0, 0)),
        out_shape=jax.ShapeDtypeStruct((batch, n_blk, cols), BF16),
        compiler_params=_params("arbitrary"),
        name="compress",
    )(x, pe, w1e, w2)


def _per_head_rows(fn, n_heads, blk):
    return jnp.concatenate([fn(r, slice(r * blk, (r + 1) * blk)) for r in range(n_heads)], axis=0)


def _nsa_kernel(q_ref, cmp_ref, ks_ref, vs_ref, kw_ref, vw_ref, gl_ref, z_ref, o_ref,
                s_ref, m_ref, acc_ref):
    i = pl.program_id(1)
    tq = q_ref.shape[0]
    hd = NSA_HEAD_DIM
    scale = hd ** -0.5
    slopes_all = _alibi_slopes(NSA_HEADS)
    gates = jax.nn.sigmoid(gl_ref[...])
    lane = lax.broadcasted_iota(jnp.int32, (tq, LANES), 1)
    t_pos = i * tq + lax.broadcasted_iota(jnp.int32, (tq, LANES), 0)
    lane_f = lane.astype(F32)
    ones_v = jnp.ones((LANES, LANES - hd), BF16)

    def biased(s, ok, distf, slopes):
        return _per_head_rows(lambda r, rows: jnp.where(ok, s[rows] - slopes[r] * distf, NEG_INF),
                              NSA_R, tq)

    for g in range(NSA_KV_GROUPS):
        slopes = slopes_all[g * NSA_R:(g + 1) * NSA_R]
        gs = slice(g * hd, (g + 1) * hd)
        q = jnp.concatenate([q_ref[:, (g * NSA_R + r) * hd:(g * NSA_R + r + 1) * hd]
                             for r in range(NSA_R)], axis=0)

        k_cmp = cmp_ref[:, gs]
        v_cmp = cmp_ref[:, NSA_KV + g * hd:NSA_KV + (g + 1) * hd]
        dist_c = t_pos - (lane * CMP_STRIDE + (CMP_BLOCK - 1))
        s = biased(_dot_t(q, k_cmp) * scale, dist_c >= 0, dist_c.astype(F32), slopes)
        m = jnp.max(s, axis=-1, keepdims=True)
        e = jnp.exp(s - m)
        p = e / jnp.sum(e, axis=-1, keepdims=True)
        any_visible = jnp.concatenate([t_pos[:, 0:1]] * NSA_R, axis=0) >= (CMP_BLOCK - 1)
        p = jnp.where(any_visible, p, 0.0)
        o_cmp = jnp.dot(p.astype(BF16), v_cmp, preferred_element_type=F32)
        p_sum = p[0:tq]
        for r in range(1, NSA_R):
            p_sum = p_sum + p[r * tq:(r + 1) * tq]

        imp = p_sum + pltpu.roll(p_sum, 1, axis=1)
        for k in range(1, CMP_PER_SLC):
            imp = imp + pltpu.roll(p_sum, LANES - k, axis=1)
        blk_j = lane // CMP_PER_SLC
        cur = t_pos // SLC_BLOCK
        forced = (blk_j == 0) | (blk_j == cur) | (blk_j == cur - 1)
        imp = jnp.where(forced, SEL_FORCE, jnp.where(blk_j > cur, -SEL_FORCE, imp))
        dead = -3.0e38
        v_imp = jnp.where(lane % CMP_PER_SLC == 0, imp, dead)
        sel = jnp.zeros((tq, LANES), jnp.bool_)
        for _ in range(SLC_TOP_N):
            mx = jnp.max(v_imp, axis=-1, keepdims=True)
            first = jnp.min(jnp.where(v_imp == mx, lane_f, 2.0 * LANES), axis=-1, keepdims=True)
            pick = lane_f == first
            sel = sel | pick
            v_imp = jnp.where(pick, dead, v_imp)
        sel_n = jnp.where(sel, 1.0, 0.0).astype(BF16)

        def attend(k_ref, v_ref, lo, hi, mask_fn):
            m_ref[...] = jnp.full(m_ref.shape, NEG_INF, F32)

            def scores(c, carry):
                start = pl.multiple_of(c * ATTN_BLOCK, ATTN_BLOCK)
                k = k_ref[pl.ds(start, ATTN_BLOCK), gs]
                dist = t_pos - (start + lane)
                sc = biased(_dot_t(q, k) * scale, mask_fn(start, dist), dist.astype(F32), slopes)
                s_ref[c] = sc
                m_ref[...] = jnp.maximum(m_ref[...], sc)
                return carry

            lax.fori_loop(lo, hi, scores, 0)
            m_ref[...] = jnp.broadcast_to(jnp.max(m_ref[...], axis=-1, keepdims=True), m_ref.shape)
            acc_ref[...] = jnp.zeros(acc_ref.shape, F32)

            def weighted(c, carry):
                start = pl.multiple_of(c * ATTN_BLOCK, ATTN_BLOCK)
                e = jnp.exp(s_ref[c] - m_ref[...]).astype(BF16)
                v = jnp.concatenate([v_ref[pl.ds(start, ATTN_BLOCK), gs], ones_v], axis=1)
                acc_ref[...] += jnp.dot(e, v, preferred_element_type=F32)
                return carry

            lax.fori_loop(lo, hi, weighted, 0)
            acc = acc_ref[...]
            return (acc / pltpu.roll(acc, LANES - hd, axis=1))[:, 0:hd]

        def slc_mask(start, dist):
            key_blk = (start + lax.broadcasted_iota(jnp.int32, (LANES, ATTN_BLOCK), 1)) // SLC_BLOCK
            lane_n = lax.broadcasted_iota(jnp.int32, (LANES, ATTN_BLOCK), 0)
            expand = jnp.where(lane_n == key_blk * CMP_PER_SLC, 1.0, 0.0).astype(BF16)
            chosen = jnp.dot(sel_n, expand, preferred_element_type=F32) > 0.5
            return chosen & (dist >= 0)

        def win_mask(start, dist):
            return (dist >= 0) & (dist <= WIN_SIZE - 1)

        o_slc = attend(ks_ref, vs_ref, 0, i + 1, slc_mask)
        win_blocks = -(-(WIN_SIZE - 1) // ATTN_BLOCK)
        o_win = attend(kw_ref, vw_ref, jnp.maximum(i - win_blocks, 0), i + 1, win_mask)

        def gate(kind):
            base = g * NSA_R * 3 + kind
            return jnp.concatenate([gates[:, base + 3 * r:base + 3 * r + 1] for r in range(NSA_R)], axis=0)

        o = gate(0) * o_cmp + gate(1) * o_slc + gate(2) * o_win
        o = jnp.concatenate([o[r * tq:(r + 1) * tq] for r in range(NSA_R)], axis=1)
        cs = slice(g * NSA_R * hd, (g + 1) * NSA_R * hd)
        o_ref[:, cs] = (o * _silu(z_ref[:, cs])).astype(o_ref.dtype)


def _nsa_attn(nb, cmp, nf, gl_col, batch, seq):
    tq = ATTN_BLOCK
    nq = seq // tq
    kv0 = NSA_WIDTH // NSA_KV
    full = lambda col: pl.BlockSpec((seq, NSA_KV), lambda b, i: (b, col))
    return pl.pallas_call(
        _nsa_kernel,
        grid=(batch, nq),
        in_specs=[pl.BlockSpec((tq, NSA_WIDTH), lambda b, i: (b * nq + i, 0)),
                  pl.BlockSpec((None,) + cmp.shape[1:], lambda b, i: (b, 0, 0)),
                  full(kv0 + 2), full(kv0 + 3), full(kv0 + 4), full(kv0 + 5),
                  pl.BlockSpec((tq, LANES), lambda b, i: (b * nq + i, gl_col)),
                  pl.BlockSpec((tq, NSA_WIDTH), lambda b, i: (b * nq + i, 0))],
        out_specs=pl.BlockSpec((tq, NSA_WIDTH), lambda b, i: (b * nq + i, 0)),
        out_shape=jax.ShapeDtypeStruct((batch * seq, NSA_WIDTH), BF16),
        scratch_shapes=[pltpu.VMEM((nq, NSA_R * tq, ATTN_BLOCK), F32),
                        pltpu.VMEM((NSA_R * tq, ATTN_BLOCK), F32),
                        pltpu.VMEM((NSA_R * tq, LANES), F32)],
        compiler_params=_params("arbitrary", "arbitrary"),
        name="nsa_attn",
    )(nb, cmp, nb, nb, nb, nb, nf, nf)


def _nsa_out_kernel(yo_ref, ym_ref, w_ref, x_ref, g_ref, out_ref):
    y = jnp.dot(yo_ref[...], w_ref[0:NSA_WIDTH, :], preferred_element_type=F32)
    y = y + jnp.dot(ym_ref[...], w_ref[NSA_WIDTH:NSA_WIDTH + MEM_WIDTH, :], preferred_element_type=F32)
    x = x_ref[...] + y
    ms = jnp.mean(x * x, axis=-1, keepdims=True)
    out_ref[...] = x * lax.rsqrt(ms + NORM_EPS) * g_ref[...]


def _nsa_out(yo, ym, w, x, g, tm=256):
    m, d = x.shape
    row = lambda width: pl.BlockSpec((tm, width), lambda i: (i, 0))
    return pl.pallas_call(
        _nsa_out_kernel,
        grid=(m // tm,),
        in_specs=[row(NSA_WIDTH), row(MEM_WIDTH), pl.BlockSpec(w.shape, lambda i: (0, 0)), row(d),
                  pl.BlockSpec((1, d), lambda i: (0, 0))],
        out_specs=row(d),
        out_shape=jax.ShapeDtypeStruct((m, d), F32),
        compiler_params=_params("arbitrary"),
        name="nsa_out",
    )(yo, ym, w, x, g.reshape(1, d))


def _cols(w, *ranges):
    return jnp.concatenate([w[:, a:b] for a, b in ranges], axis=1).astype(BF16)


def _hawk_layer(x, mem, batch, seq, norm_g, w_in, conv_w, conv_b, ga_w, ga_b, gx_w, gx_b, lam,
                mem_norm_g, w_mem_kv, w_out):
    xa0, za0 = 0, LRU_WIDTH
    q0 = 2 * LRU_WIDTH
    k0, v0 = q0 + DIL_QKV, q0 + 2 * DIL_QKV
    zb0 = q0 + 3 * DIL_QKV
    qm0 = zb0 + DIL_WIDTH
    zm0 = qm0 + MEM_WIDTH

    def qkv_cols(gi):
        return [(base + gi * DIL_WIDTH, base + (gi + 1) * DIL_WIDTH) for base in (q0, k0, v0)]

    hf = _norm_matmul(x, norm_g, _cols(w_in, (xa0, q0), (zb0, qm0), (zm0, zm0 + MEM_WIDTH)), F32)[0]
    zb_col = 2 * LRU_WIDTH // DIL_WIDTH
    zm_col = (2 * LRU_WIDTH + DIL_WIDTH) // MEM_WIDTH
    qkv = [_norm_matmul(x, norm_g, _cols(w_in, *qkv_cols(0), (qm0, zm0)), BF16)]
    for gi in range(1, len(DIL_GROUPS)):
        qkv.append(_norm_matmul(x, norm_g, _cols(w_in, *qkv_cols(gi)), BF16, dil=DIL_GROUPS[gi][1]))
    qm_col = 3 * DIL_WIDTH // MEM_WIDTH
    n_mem = mem.shape[0] // batch
    mem_kv = _norm_matmul(mem, mem_norm_g, w_mem_kv.astype(BF16), BF16)[0]

    ya = _rglru(hf, conv_w, conv_b, _pack_block_diag(ga_w), ga_b, _pack_block_diag(gx_w), gx_b, lam,
                batch, seq)
    attn = [_dil_attn(qkv[gi], gi, batch, seq) for gi in range(len(DIL_GROUPS))]
    ym = _mem_attn(qkv[0][0], qm_col, mem_kv, hf, zm_col, batch, seq, n_mem)
    return _hawk_out(ya, [o for o, _ in attn], [l for _, l in attn], hf, zb_col, ym,
                     w_out.astype(BF16), x)


def _nsa_layer(x, mem, batch, seq, norm_g, w_in, pe_k, pe_v, phik_w1, phik_w2, phiv_w1, phiv_w2,
               mem_norm_g, w_mem_kv, w_out, final_g):
    kv0 = NSA_WIDTH
    gl0 = kv0 + 6 * NSA_KV
    z0 = gl0 + 3 * NSA_HEADS
    qm0 = z0 + NSA_WIDTH
    zm0 = qm0 + MEM_WIDTH
    nb = _norm_matmul(x, norm_g, _cols(w_in, (0, gl0), (qm0, zm0)), BF16)[0]
    gl_w = jnp.pad(w_in[:, gl0:z0], ((0, 0), (0, LANES - 3 * NSA_HEADS)))
    nf_w = jnp.concatenate([w_in[:, z0:qm0], w_in[:, zm0:zm0 + MEM_WIDTH], gl_w], axis=1).astype(BF16)
    nf = _norm_matmul(x, norm_g, nf_w, F32)[0]
    qm_col = (NSA_WIDTH + 6 * NSA_KV) // MEM_WIDTH
    zm_col = NSA_WIDTH // MEM_WIDTH
    gl_col = (NSA_WIDTH + MEM_WIDTH) // LANES
    n_mem = mem.shape[0] // batch
    mem_kv = _norm_matmul(mem, mem_norm_g, w_mem_kv.astype(BF16), BF16)[0]

    cmp = _compress(nb[:, kv0:kv0 + 2 * NSA_KV], pe_k, pe_v, phik_w1, phik_w2, phiv_w1, phiv_w2,
                    batch, seq)
    yo = _nsa_attn(nb, cmp, nf, gl_col, batch, seq)
    ym = _mem_attn(nb, qm_col, mem_kv, nf, zm_col, batch, seq, n_mem)
    return _nsa_out(yo, ym, w_out.astype(BF16), x, final_g)


def kernel(x, mem, hawk_norm, hawk_w_in, hawk_conv_w, hawk_conv_b, hawk_gate_a_w, hawk_gate_a_b,
           hawk_gate_x_w, hawk_gate_x_b, hawk_lambda, hawk_mem_norm, hawk_w_mem_kv, hawk_w_out,
           nsa_norm, nsa_w_in, nsa_pe_k, nsa_pe_v, nsa_phi_k_w1, nsa_phi_k_w2, nsa_phi_v_w1,
           nsa_phi_v_w2, nsa_mem_norm, nsa_w_mem_kv, nsa_w_out, final_norm):
    batch, seq, d = x.shape
    assert hawk_norm.shape[0] == 1 and nsa_norm.shape[0] == 1, "one layer of each kind"
    assert seq % (ATTN_BLOCK * DIL_GROUPS[-1][1]) == 0
    x2 = x.reshape(batch * seq, d)
    mem2 = mem.reshape(batch * mem.shape[1], d)
    x2 = _hawk_layer(x2, mem2, batch, seq, hawk_norm[0], hawk_w_in[0], hawk_conv_w[0], hawk_conv_b[0],
                     hawk_gate_a_w[0], hawk_gate_a_b[0].reshape(-1), hawk_gate_x_w[0],
                     hawk_gate_x_b[0].reshape(-1), hawk_lambda[0], hawk_mem_norm[0], hawk_w_mem_kv[0],
                     hawk_w_out[0])
    out = _nsa_layer(x2, mem2, batch, seq, nsa_norm[0], nsa_w_in[0], nsa_pe_k[0], nsa_pe_v[0],
                     nsa_phi_k_w1[0], nsa_phi_k_w2[0], nsa_phi_v_w1[0], nsa_phi_v_w2[0],
                     nsa_mem_norm[0], nsa_w_mem_kv[0], nsa_w_out[0], final_norm)
    return out.reshape(batch, seq, d)
```

```python
import functools

import numpy as np
import jax
import jax.numpy as jnp
from jax import lax
from jax.experimental import pallas as pl
from jax.experimental.pallas import tpu as pltpu

F32 = jnp.float32
BF16 = jnp.bfloat16

NORM_EPS = 1e-6
NEG_INF = -1e30
LANES = 128
ATTN_BLOCK = 128
VMEM_LIMIT = 56 * 1024 * 1024

LRU_WIDTH = 1024
LRU_BLOCKS = 16
LRU_BLOCK_DIM = LRU_WIDTH // LRU_BLOCKS
LRU_PACK = 256
CONV_WIDTH = 4
LRU_C = 8.0

DIL_GROUPS = ((128, 1), (512, 4), (2048, 16))
DIL_HEADS = 4
DIL_HEAD_DIM = 128
DIL_WIDTH = DIL_HEADS * DIL_HEAD_DIM
DIL_QKV = len(DIL_GROUPS) * DIL_WIDTH
LSE_LANES = LANES // DIL_HEADS

MEM_HEADS = 4
MEM_HEAD_DIM = 64
MEM_WIDTH = MEM_HEADS * MEM_HEAD_DIM

NSA_HEADS = 16
NSA_KV_GROUPS = 2
NSA_R = NSA_HEADS // NSA_KV_GROUPS
NSA_HEAD_DIM = 64
NSA_WIDTH = NSA_HEADS * NSA_HEAD_DIM
NSA_KV = NSA_KV_GROUPS * NSA_HEAD_DIM
CMP_BLOCK = 32
CMP_STRIDE = 16
SLC_BLOCK = 64
SLC_TOP_N = 8
WIN_SIZE = 512
PHI_HIDDEN = 256
SEL_FORCE = 1e6
CMP_PER_SLC = SLC_BLOCK // CMP_STRIDE


def _alibi_slopes(n):
    return [float(v) for v in np.exp2(-8.0 * np.arange(1, n + 1) / n).astype(np.float32)]


def _params(*semantics):
    return pltpu.CompilerParams(dimension_semantics=semantics, vmem_limit_bytes=VMEM_LIMIT)


def _silu(z):
    return z * jax.nn.sigmoid(z)


def _dot_t(a, b):
    return lax.dot_general(a, b, (((1,), (1,)), ((), ())), preferred_element_type=F32)


def _norm_matmul_kernel(x_ref, g_ref, w_ref, o_ref):
    x = x_ref[...]
    ms = jnp.mean(x * x, axis=-1, keepdims=True)
    xn = (x * lax.rsqrt(ms + NORM_EPS) * g_ref[...]).astype(BF16)
    o_ref[...] = jnp.dot(xn, w_ref[...], preferred_element_type=F32).astype(o_ref.dtype)


def _norm_matmul(x, g, w, out_dtype, dil=1, tm=256):
    m, k = x.shape
    n = w.shape[1]
    rows = m // dil
    tm = min(tm, rows)
    return pl.pallas_call(
        _norm_matmul_kernel,
        grid=(dil, rows // tm),
        in_specs=[pl.BlockSpec((tm, k), lambda c, i: (i, c)),
                  pl.BlockSpec((1, k), lambda c, i: (0, 0)),
                  pl.BlockSpec((k, n), lambda c, i: (0, 0))],
        out_specs=pl.BlockSpec((None, tm, n), lambda c, i: (c, i, 0)),
        out_shape=jax.ShapeDtypeStruct((dil, rows, n), out_dtype),
        compiler_params=_params("arbitrary", "arbitrary"),
        name="norm_matmul",
    )(x.reshape(rows, dil * k), g.reshape(1, k), w)


def _shift_rows(x, k, fill, row):
    return jnp.where(row < k, fill, pltpu.roll(x, k, axis=0))


def _rglru_kernel(xa_ref, za_ref, cw_ref, cb_ref, wa_ref, ba_ref, wx_ref, bx_ref, lam_ref,
                  o_ref, xpad_ref, h_ref):
    t = pl.program_id(1)
    tt, width = xa_ref.shape
    halo = 8

    @pl.when(t == 0)
    def _():
        xpad_ref[0:halo, :] = jnp.zeros((halo, width), F32)
        h_ref[...] = jnp.zeros_like(h_ref)

    x = xa_ref[...]
    xpad_ref[halo:halo + tt, :] = x
    cw = cw_ref[...]
    y = cw[CONV_WIDTH - 1:CONV_WIDTH] * x
    for k in range(1, CONV_WIDTH):
        y = y + cw[CONV_WIDTH - 1 - k:CONV_WIDTH - k] * xpad_ref[halo - k:halo - k + tt, :]
    y = y + cb_ref[...]
    xpad_ref[0:halo, :] = x[tt - halo:tt, :]

    yb = y.astype(BF16)
    r_parts, i_parts = [], []
    for p in range(width // LRU_PACK):
        ys = yb[:, p * LRU_PACK:(p + 1) * LRU_PACK]
        r_parts.append(jnp.dot(ys, wa_ref[p], preferred_element_type=F32))
        i_parts.append(jnp.dot(ys, wx_ref[p], preferred_element_type=F32))
    r = jax.nn.sigmoid(jnp.concatenate(r_parts, axis=1) + ba_ref[...])
    gi = jax.nn.sigmoid(jnp.concatenate(i_parts, axis=1) + bx_ref[...])

    nl = -lam_ref[...]
    softplus = jnp.maximum(nl, 0.0) + jnp.log1p(jnp.exp(-jnp.abs(nl)))
    log_a = (-LRU_C) * r * softplus
    a = jnp.exp(log_a)
    z2 = 2.0 * log_a
    u = a * a
    one_minus = jnp.where(u == 1.0, -z2, jnp.where(u < 0.5, 1.0 - u, (1.0 - u) * z2 / jnp.log(u)))
    mult = jnp.sqrt(one_minus)
    row = lax.broadcasted_iota(jnp.int32, (tt, width), 0)
    mult = jnp.where((row + t * tt) == 0, 1.0, mult)
    b = y * gi * mult

    k = 1
    while k < tt:
        a_sh = _shift_rows(a, k, 1.0, row)
        b_sh = _shift_rows(b, k, 0.0, row)
        b = a * b_sh + b
        a = a * a_sh
        k *= 2
    h = a * h_ref[...] + b
    h_ref[...] = h[tt - 1:tt, :]
    o_ref[...] = (h * _silu(za_ref[...])).astype(o_ref.dtype)


def _rglru(hf, conv_w, conv_b, wa, ba, wx, bx, lam, batch, seq, tt=256):
    width = LRU_WIDTH
    nt = seq // tt
    packs = width // LRU_PACK
    vec = pl.BlockSpec((1, width), lambda b, t: (0, 0))
    gate_w = pl.BlockSpec((packs, LRU_PACK, LRU_PACK), lambda b, t: (0, 0, 0))
    return pl.pallas_call(
        _rglru_kernel,
        grid=(batch, nt),
        in_specs=[pl.BlockSpec((tt, width), lambda b, t: (b * nt + t, 0)),
                  pl.BlockSpec((tt, width), lambda b, t: (b * nt + t, 1)),
                  pl.BlockSpec((CONV_WIDTH, width), lambda b, t: (0, 0)),
                  vec, gate_w, vec, gate_w, vec, vec],
        out_specs=pl.BlockSpec((tt, width), lambda b, t: (b * nt + t, 0)),
        out_shape=jax.ShapeDtypeStruct((batch * seq, width), BF16),
        scratch_shapes=[pltpu.VMEM((tt + 8, width), F32), pltpu.VMEM((1, width), F32)],
        compiler_params=_params("arbitrary", "arbitrary"),
        name="rglru",
    )(hf, hf, conv_w, conv_b.reshape(1, width), wa, ba.reshape(1, width), wx, bx.reshape(1, width),
      lam.reshape(1, width))


def _pack_block_diag(w):
    per = LRU_PACK // LRU_BLOCK_DIM
    w = w.reshape(LRU_BLOCKS // per, per, LRU_BLOCK_DIM, LRU_BLOCK_DIM)
    eye = jnp.eye(per, dtype=w.dtype)
    packed = w[:, :, :, None, :] * eye[None, :, None, :, None]
    return packed.reshape(LRU_BLOCKS // per, LRU_PACK, LRU_PACK).astype(BF16)


def _dil_attn_kernel(*refs, slopes, pos_scale, max_dist, use_prev):
    if use_prev:
        q_ref, kp_ref, kc_ref, vp_ref, vc_ref, o_ref, l_ref = refs
    else:
        q_ref, kc_ref, vc_ref, o_ref, l_ref = refs
    i = pl.program_id(2)
    blk = q_ref.shape[0]
    width = 2 * blk if use_prev else blk
    row = lax.broadcasted_iota(jnp.int32, (blk, width), 0)
    col = lax.broadcasted_iota(jnp.int32, (blk, width), 1)
    dist = (width - blk) + row - col
    valid = (dist >= 0) & (dist <= max_dist)
    if use_prev:
        valid = valid & ((col >= blk) | (i > 0))
    distf = (dist * pos_scale).astype(F32)
    scale = DIL_HEAD_DIM ** -0.5
    for h in range(DIL_HEADS):
        hs = slice(h * DIL_HEAD_DIM, (h + 1) * DIL_HEAD_DIM)
        q = q_ref[:, hs]
        if use_prev:
            k = jnp.concatenate([kp_ref[:, hs], kc_ref[:, hs]], axis=0)
            v = jnp.concatenate([vp_ref[:, hs], vc_ref[:, hs]], axis=0)
        else:
            k, v = kc_ref[:, hs], vc_ref[:, hs]
        s = _dot_t(q, k) * scale - slopes[h] * distf
        s = jnp.where(valid, s, NEG_INF)
        m = jnp.max(s, axis=-1, keepdims=True)
        e = jnp.exp(s - m)
        den = jnp.sum(e, axis=-1, keepdims=True)
        o_ref[:, hs] = jnp.dot(e.astype(BF16), v, preferred_element_type=F32) / den
        l_ref[:, h * LSE_LANES:(h + 1) * LSE_LANES] = jnp.broadcast_to(m + jnp.log(den), (blk, LSE_LANES))


def _dil_attn(qkv, gi, batch, seq):
    window, dil = DIL_GROUPS[gi]
    sub = seq // dil
    nb = sub // ATTN_BLOCK
    use_prev = nb > 1
    slopes = _alibi_slopes(len(DIL_GROUPS) * DIL_HEADS)[gi * DIL_HEADS:(gi + 1) * DIL_HEADS]
    blk = (None, ATTN_BLOCK, DIL_WIDTH)
    cur = lambda col: pl.BlockSpec(blk, lambda c, b, i: (c, b * nb + i, col))
    prev = lambda col: pl.BlockSpec(blk, lambda c, b, i: (c, b * nb + jnp.maximum(i - 1, 0), col))
    if use_prev:
        in_specs = [cur(0), prev(1), cur(1), prev(2), cur(2)]
        args = (qkv,) * 5
    else:
        in_specs = [cur(0), cur(1), cur(2)]
        args = (qkv,) * 3
    o, lse = pl.pallas_call(
        functools.partial(_dil_attn_kernel, slopes=slopes, pos_scale=dil, max_dist=window // dil,
                          use_prev=use_prev),
        grid=(dil, batch, nb),
        in_specs=in_specs,
        out_specs=[pl.BlockSpec((ATTN_BLOCK, DIL_WIDTH), lambda c, b, i: (b * nb + i, c)),
                   pl.BlockSpec((ATTN_BLOCK, LANES), lambda c, b, i: (b * nb + i, c))],
        out_shape=[jax.ShapeDtypeStruct((batch * sub, dil * DIL_WIDTH), F32),
                   jax.ShapeDtypeStruct((batch * sub, dil * LANES), F32)],
        compiler_params=_params("arbitrary", "arbitrary", "arbitrary"),
        name=f"dil_attn_d{dil}",
    )(*args)
    return o.reshape(batch * seq, DIL_WIDTH), lse.reshape(batch * seq, LANES)


def _mem_attn_kernel(q_ref, k_ref, v_ref, z_ref, o_ref):
    scale = MEM_HEAD_DIM ** -0.5
    outs = []
    for h in range(MEM_HEADS):
        hs = slice(h * MEM_HEAD_DIM, (h + 1) * MEM_HEAD_DIM)
        s = _dot_t(q_ref[:, hs], k_ref[:, hs]) * scale
        m = jnp.max(s, axis=-1, keepdims=True)
        e = jnp.exp(s - m)
        den = jnp.sum(e, axis=-1, keepdims=True)
        outs.append(jnp.dot(e.astype(BF16), v_ref[:, hs], preferred_element_type=F32) / den)
    o_ref[...] = (jnp.concatenate(outs, axis=1) * _silu(z_ref[...])).astype(o_ref.dtype)


def _mem_attn(qsrc, q_col, kv, zsrc, z_col, batch, seq, n_mem, tq=512):
    nq = seq // tq
    return pl.pallas_call(
        _mem_attn_kernel,
        grid=(batch, nq),
        in_specs=[pl.BlockSpec((tq, MEM_WIDTH), lambda b, i: (b * nq + i, q_col)),
                  pl.BlockSpec((n_mem, MEM_WIDTH), lambda b, i: (b, 0)),
                  pl.BlockSpec((n_mem, MEM_WIDTH), lambda b, i: (b, 1)),
                  pl.BlockSpec((tq, MEM_WIDTH), lambda b, i: (b * nq + i, z_col))],
        out_specs=pl.BlockSpec((tq, MEM_WIDTH), lambda b, i: (b * nq + i, 0)),
        out_shape=jax.ShapeDtypeStruct((batch * seq, MEM_WIDTH), BF16),
        compiler_params=_params("arbitrary", "arbitrary"),
        name="mem_attn",
    )(qsrc, kv, kv, zsrc)


def _hawk_out_kernel(ya_ref, o0_ref, o1_ref, o2_ref, l0_ref, l1_ref, l2_ref, zb_ref, ym_ref, w_ref,
                     x_ref, out_ref):
    o_refs = (o0_ref, o1_ref, o2_ref)
    l_refs = (l0_ref, l1_ref, l2_ref)
    parts = []
    for h in range(DIL_HEADS):
        hs = slice(h * DIL_HEAD_DIM, (h + 1) * DIL_HEAD_DIM)
        ls = [l[:, h * LSE_LANES:h * LSE_LANES + 1] for l in l_refs]
        m = jnp.maximum(jnp.maximum(ls[0], ls[1]), ls[2])
        ws = [jnp.exp(l - m) for l in ls]
        num = ws[0] * o_refs[0][:, hs] + ws[1] * o_refs[1][:, hs] + ws[2] * o_refs[2][:, hs]
        parts.append(num / (ws[0] + ws[1] + ws[2]))
    yb = (jnp.concatenate(parts, axis=1) * _silu(zb_ref[...])).astype(BF16)
    a_end = LRU_WIDTH
    b_end = a_end + DIL_WIDTH
    y = jnp.dot(ya_ref[...], w_ref[0:a_end, :], preferred_element_type=F32)
    y = y + jnp.dot(yb, w_ref[a_end:b_end, :], preferred_element_type=F32)
    y = y + jnp.dot(ym_ref[...], w_ref[b_end:b_end + MEM_WIDTH, :], preferred_element_type=F32)
    out_ref[...] = x_ref[...] + y


def _hawk_out(ya, os_, ls_, hf, zb_col, ym, w, x, tm=256):
    m, d = x.shape
    row = lambda width, col=0: pl.BlockSpec((tm, width), lambda i: (i, col))
    return pl.pallas_call(
        _hawk_out_kernel,
        grid=(m // tm,),
        in_specs=[row(LRU_WIDTH)] + [row(DIL_WIDTH)] * 3 + [row(LANES)] * 3
                 + [row(DIL_WIDTH, zb_col), row(MEM_WIDTH),
                    pl.BlockSpec(w.shape, lambda i: (0, 0)), row(d)],
        out_specs=row(d),
        out_shape=jax.ShapeDtypeStruct((m, d), F32),
        compiler_params=_params("arbitrary"),
        name="hawk_out",
    )(ya, *os_, *ls_, hf, ym, w, x)


def _compress_kernel(x_ref, pe_ref, w1_ref, w2k_ref, w2vt_ref, k_ref, vt_ref):
    x = x_ref[...]
    n_blk = x.shape[0]
    first = jnp.dot(x, w1_ref[0], preferred_element_type=F32)
    second = jnp.dot(x, w1_ref[1], preferred_element_type=F32)
    pe = (jnp.dot(pe_ref[0], w1_ref[0], preferred_element_type=F32)
          + jnp.dot(pe_ref[1], w1_ref[1], preferred_element_type=F32))
    hid = first + pltpu.roll(second, n_blk - 1, axis=0) + pe[0:1, :]
    act = _silu(hid).astype(BF16)
    hidden = lambda c: act[:, c * PHI_HIDDEN:(c + 1) * PHI_HIDDEN]
    ks = [jnp.dot(hidden(g), w2k_ref[...], preferred_element_type=F32) for g in range(NSA_KV_GROUPS)]
    vts = [_dot_t(w2vt_ref[...], hidden(NSA_KV_GROUPS + g)) for g in range(NSA_KV_GROUPS)]
    k_ref[...] = jnp.concatenate(ks, axis=1).astype(k_ref.dtype)
    vt_ref[...] = jnp.concatenate(vts, axis=0).astype(vt_ref.dtype)


def _compress(kcv, pe_k, pe_v, k_w1, k_w2, v_w1, v_w2, batch, seq):
    half = CMP_BLOCK // 2
    assert half == CMP_STRIDE
    n_blk = seq // CMP_STRIDE
    cols = 2 * NSA_KV
    x = kcv.reshape(batch, n_blk, half * cols)
    w1 = jnp.stack([k_w1, v_w1]).reshape(2, 2, half, NSA_HEAD_DIM, PHI_HIDDEN)
    eye = jnp.eye(2, dtype=w1.dtype)
    sel = eye[:, None, :, None] * eye[None, :, None, :]
    w1e = w1.transpose(1, 2, 0, 3, 4)[:, :, :, None, :, None, None, :] * sel[None, None, :, :, None, :, :, None]
    w1e = w1e.reshape(2, half * cols, 2 * NSA_KV_GROUPS * PHI_HIDDEN).astype(BF16)
    pe = jnp.stack([pe_k, pe_v]).reshape(2, 2, half, 1, NSA_HEAD_DIM)
    pe = jnp.broadcast_to(pe.transpose(1, 2, 0, 3, 4), (2, half, 2, NSA_KV_GROUPS, NSA_HEAD_DIM))
    pe = jnp.broadcast_to(pe.reshape(2, 1, half * cols), (2, 8, half * cols)).astype(BF16)
    w2k = k_w2.astype(BF16)
    w2vt = v_w2.T.astype(BF16)
    whole = lambda a: pl.BlockSpec(a.shape, lambda b: (0,) * a.ndim)
    return pl.pallas_call(
        _compress_kernel,
        grid=(batch,),
        in_specs=[pl.BlockSpec((None, n_blk, half * cols), lambda b: (b, 0, 0)),
                  whole(pe), whole(w1e), whole(w2k), whole(w2vt)],
        out_specs=[pl.BlockSpec((None, n_blk, NSA_KV), lambda b: (b, 0, 0)),
                   pl.BlockSpec((None, NSA_KV, n_blk), lambda b: (b, 0, 0))],
        out_shape=[jax.ShapeDtypeStruct((batch, n_blk, NSA_KV), BF16),
                   jax.ShapeDtypeStruct((batch, NSA_KV, n_blk), BF16)],
        compiler_params=_params("arbitrary"),
        name="compress",
    )(x, pe, w1e, w2k, w2vt)


KEY_CHUNK = 256
NSA_TQ = 256
SLOPE_PIECES = 3
N_FEATS = 2 * SLOPE_PIECES
WIN_KEY_COLS = 2 * NSA_HEAD_DIM
SLC_KEY_COLS = WIN_KEY_COLS + LANES


def _slope_pieces(slope):
    rest = np.float32(slope)
    pieces = []
    for _ in range(SLOPE_PIECES):
        p = np.float32(np.asarray(rest).astype(BF16))
        pieces.append(float(p))
        rest = np.float32(rest - p)
    return pieces


def _lane_table(lane, values):
    out = jnp.zeros(lane.shape, F32)
    for idx, v in enumerate(values):
        out = jnp.where(lane == idx, v, out)
    return out


def _key_feats(pos_hi, pos_lo, lane):
    return jnp.where(lane < SLOPE_PIECES, pos_hi, jnp.where(lane < N_FEATS, pos_lo, 0)).astype(F32)


def _tile_heads(x):
    return jnp.concatenate([x] * NSA_R, axis=1)


def _nsa_kernel(q_ref, kc_ref, vct_ref, ks_ref, vst_ref, kw_ref, vwt_ref, gl_ref, z_ref, o_ref,
                s_ref, acc_ref):
    i = pl.program_id(1)
    tq = q_ref.shape[0]
    hd = NSA_HEAD_DIM
    n_cmp = kc_ref.shape[0]
    slopes_all = _alibi_slopes(NSA_HEADS)
    gates_t = jax.nn.sigmoid(gl_ref[...]).T
    feat_lane = lax.broadcasted_iota(jnp.int32, (tq, hd), 1)
    key_row = lax.broadcasted_iota(jnp.int32, (KEY_CHUNK, tq), 0)
    t_pos = i * tq + lax.broadcasted_iota(jnp.int32, (KEY_CHUNK, tq), 1)
    ones_rows = jnp.ones((16, KEY_CHUNK), BF16)
    win_lo = jnp.maximum(i * tq - (WIN_SIZE - 1), 0) // KEY_CHUNK
    chunks_hi = (i * tq + tq - 1) // KEY_CHUNK + 1

    for g in range(NSA_KV_GROUPS):
        slopes = slopes_all[g * NSA_R:(g + 1) * NSA_R]
        gs = slice(g * hd, (g + 1) * hd)
        q_parts = []
        for r in range(NSA_R):
            qr = q_ref[:, (g * NSA_R + r) * hd:(g * NSA_R + r + 1) * hd].astype(F32) * (hd ** -0.5)
            feats = _lane_table(feat_lane, _slope_pieces(slopes[r]) * 2)
            q_parts.append(jnp.concatenate([qr, feats], axis=1).astype(BF16))
        q_aug = jnp.concatenate(q_parts, axis=0)

        n_row = lax.broadcasted_iota(jnp.int32, (n_cmp, tq), 0)
        t_cmp = i * tq + lax.broadcasted_iota(jnp.int32, (n_cmp, tq), 1)
        visible = t_cmp >= n_row * CMP_STRIDE + (CMP_BLOCK - 1)
        cfeat_row = lax.broadcasted_iota(jnp.int32, (n_cmp, hd), 0)
        cfeat_lane = lax.broadcasted_iota(jnp.int32, (n_cmp, hd), 1)
        kc_feats = _key_feats(cfeat_row * CMP_STRIDE, 0, cfeat_lane)
        kc_aug = jnp.concatenate([kc_ref[:, gs], kc_feats.astype(BF16)], axis=1)
        s = _dot_t(kc_aug, q_aug) + _tile_heads(jnp.where(visible, 0.0, NEG_INF))
        m = jnp.max(s, axis=0, keepdims=True)
        e = jnp.exp(s - m)
        t_one = i * tq + lax.broadcasted_iota(jnp.int32, (1, NSA_R * tq), 1) % tq
        any_visible = t_one >= (CMP_BLOCK - 1)
        p = jnp.where(any_visible, e / jnp.sum(e, axis=0, keepdims=True), 0.0)
        o_cmp = jnp.dot(vct_ref[gs, :], p.astype(BF16), preferred_element_type=F32)
        p_sum = p[:, 0:tq]
        for r in range(1, NSA_R):
            p_sum = p_sum + p[:, r * tq:(r + 1) * tq]

        imp = p_sum + pltpu.roll(p_sum, 1, axis=0)
        for k in range(1, CMP_PER_SLC):
            imp = imp + pltpu.roll(p_sum, n_cmp - k, axis=0)
        blk_j = n_row // CMP_PER_SLC
        cur = t_cmp // SLC_BLOCK
        forced = (blk_j == 0) | (blk_j == cur) | (blk_j == cur - 1)
        imp = jnp.where(forced, SEL_FORCE, jnp.where(blk_j > cur, -SEL_FORCE, imp))
        dead = -3.0e38
        v_imp = jnp.where(n_row % CMP_PER_SLC == 0, imp, dead)
        n_row_f = n_row.astype(F32)
        sel = jnp.zeros((n_cmp, tq), F32)
        for _ in range(SLC_TOP_N):
            mx = jnp.max(v_imp, axis=0, keepdims=True)
            first = jnp.min(jnp.where(v_imp == mx, n_row_f, 2.0 * n_cmp), axis=0, keepdims=True)
            pick = n_row_f == first
            sel = jnp.where(pick, 1.0, sel)
            v_imp = jnp.where(pick, dead, v_imp)

        sel_bias = jnp.where(sel > 0.5, 0.0, NEG_INF).T.astype(BF16)
        q_slc = jnp.concatenate([q_aug, jnp.concatenate([sel_bias] * NSA_R, axis=0)], axis=1)

        def attend(q_br, k_ref, k_cols, vt_ref, lo, hi, masked_from, mask_fn):
            def scores(c, m_run, masked):
                start = pl.multiple_of(c * KEY_CHUNK, KEY_CHUNK)
                sc = _dot_t(k_ref[pl.ds(start, KEY_CHUNK), k_cols], q_br)
                if masked:
                    ok = mask_fn(t_pos - (start + key_row))
                    sc = sc + _tile_heads(jnp.where(ok, 0.0, NEG_INF))
                s_ref[c] = sc
                return jnp.maximum(m_run, jnp.max(sc, axis=0, keepdims=True))

            m_row = jnp.full((1, NSA_R * tq), NEG_INF, F32)
            m_row = lax.fori_loop(lo, masked_from, functools.partial(scores, masked=False), m_row)
            m_row = lax.fori_loop(masked_from, hi, functools.partial(scores, masked=True), m_row)
            acc_ref[...] = jnp.zeros(acc_ref.shape, F32)

            def weighted(c, carry):
                e = jnp.exp(s_ref[c] - m_row).astype(BF16)
                v_ext = jnp.concatenate([vt_ref[c, gs, :], ones_rows], axis=0)
                acc_ref[...] += jnp.dot(v_ext, e, preferred_element_type=F32)
                return carry

            lax.fori_loop(lo, hi, weighted, 0)
            return acc_ref[0:hd, :] / acc_ref[hd:hd + 1, :]

        slc_cols = slice(g * SLC_KEY_COLS, (g + 1) * SLC_KEY_COLS)
        win_cols = slice(g * WIN_KEY_COLS, (g + 1) * WIN_KEY_COLS)
        o_slc = attend(q_slc, ks_ref, slc_cols, vst_ref, 0, chunks_hi, chunks_hi - 1, lambda dist: dist >= 0)
        o_win = attend(q_aug, kw_ref, win_cols, vwt_ref, win_lo, chunks_hi, win_lo,
                       lambda dist: (dist >= 0) & (dist <= WIN_SIZE - 1))

        def gate(kind):
            base = g * NSA_R * 3 + kind
            return jnp.concatenate([gates_t[base + 3 * r:base + 3 * r + 1, :] for r in range(NSA_R)], axis=1)

        o = gate(0) * o_cmp + gate(1) * o_slc + gate(2) * o_win
        pairs = []
        for r in range(0, NSA_R, 2):
            two = jnp.concatenate([o[:, r * tq:(r + 1) * tq], o[:, (r + 1) * tq:(r + 2) * tq]], axis=0)
            pairs.append(two.T)
        cs = slice(g * NSA_R * hd, (g + 1) * NSA_R * hd)
        o_ref[:, cs] = (jnp.concatenate(pairs, axis=1) * _silu(z_ref[:, cs])).astype(o_ref.dtype)


def _key_pos_feats(seq):
    pos = np.arange(seq)
    feats = np.zeros((seq, NSA_HEAD_DIM), np.float32)
    feats[:, 0:SLOPE_PIECES] = ((pos // SLC_BLOCK) * SLC_BLOCK)[:, None]
    feats[:, SLOPE_PIECES:N_FEATS] = (pos % SLC_BLOCK)[:, None]
    onehot = (np.arange(LANES)[None, :] == (pos // SLC_BLOCK * CMP_PER_SLC)[:, None]).astype(np.float32)
    return jnp.asarray(feats, BF16), jnp.asarray(onehot, BF16)


def _nsa_attn(nb, k_cmp, v_cmp_t, nf, gl_col, batch, seq):
    tq = NSA_TQ
    nq = seq // tq
    hd = NSA_HEAD_DIM
    kv0 = NSA_WIDTH

    def branch(idx):
        return nb[:, kv0 + idx * NSA_KV:kv0 + (idx + 1) * NSA_KV].reshape(batch, seq, NSA_KV)

    def chunked_t(v):
        return v.reshape(batch, seq // KEY_CHUNK, KEY_CHUNK, NSA_KV).transpose(0, 1, 3, 2)

    feats, onehot = _key_pos_feats(seq)
    bcast = lambda a: jnp.broadcast_to(a[None], (batch,) + a.shape)
    k_slc, k_win = branch(2), branch(4)
    ks_aug = jnp.concatenate([t for g in range(NSA_KV_GROUPS)
                              for t in (k_slc[..., g * hd:(g + 1) * hd], bcast(feats), bcast(onehot))], axis=-1)
    kw_aug = jnp.concatenate([t for g in range(NSA_KV_GROUPS)
                              for t in (k_win[..., g * hd:(g + 1) * hd], bcast(feats))], axis=-1)
    vs_t, vw_t = chunked_t(branch(3)), chunked_t(branch(5))
    per_batch = lambda a: pl.BlockSpec((None,) + a.shape[1:], lambda b, i: (b,) + (0,) * (a.ndim - 1))
    return pl.pallas_call(
        _nsa_kernel,
        grid=(batch, nq),
        in_specs=[pl.BlockSpec((tq, NSA_WIDTH), lambda b, i: (b * nq + i, 0)),
                  per_batch(k_cmp), per_batch(v_cmp_t),
                  per_batch(ks_aug), per_batch(vs_t), per_batch(kw_aug), per_batch(vw_t),
                  pl.BlockSpec((tq, LANES), lambda b, i: (b * nq + i, gl_col)),
                  pl.BlockSpec((tq, NSA_WIDTH), lambda b, i: (b * nq + i, 0))],
        out_specs=pl.BlockSpec((tq, NSA_WIDTH), lambda b, i: (b * nq + i, 0)),
        out_shape=jax.ShapeDtypeStruct((batch * seq, NSA_WIDTH), BF16),
        scratch_shapes=[pltpu.VMEM((seq // KEY_CHUNK, KEY_CHUNK, NSA_R * tq), F32),
                        pltpu.VMEM((NSA_HEAD_DIM + 16, NSA_R * tq), F32)],
        compiler_params=_params("arbitrary", "arbitrary"),
        name="nsa_attn",
    )(nb, k_cmp, v_cmp_t, ks_aug, vs_t, kw_aug, vw_t, nf, nf)


def _nsa_out_kernel(yo_ref, ym_ref, w_ref, x_ref, g_ref, out_ref):
    y = jnp.dot(yo_ref[...], w_ref[0:NSA_WIDTH, :], preferred_element_type=F32)
    y = y + jnp.dot(ym_ref[...], w_ref[NSA_WIDTH:NSA_WIDTH + MEM_WIDTH, :], preferred_element_type=F32)
    x = x_ref[...] + y
    ms = jnp.mean(x * x, axis=-1, keepdims=True)
    out_ref[...] = x * lax.rsqrt(ms + NORM_EPS) * g_ref[...]


def _nsa_out(yo, ym, w, x, g, tm=256):
    m, d = x.shape
    row = lambda width: pl.BlockSpec((tm, width), lambda i: (i, 0))
    return pl.pallas_call(
        _nsa_out_kernel,
        grid=(m // tm,),
        in_specs=[row(NSA_WIDTH), row(MEM_WIDTH), pl.BlockSpec(w.shape, lambda i: (0, 0)), row(d),
                  pl.BlockSpec((1, d), lambda i: (0, 0))],
        out_specs=row(d),
        out_shape=jax.ShapeDtypeStruct((m, d), F32),
        compiler_params=_params("arbitrary"),
        name="nsa_out",
    )(yo, ym, w, x, g.reshape(1, d))


def _cols(w, *ranges):
    return jnp.concatenate([w[:, a:b] for a, b in ranges], axis=1).astype(BF16)


def _hawk_layer(x, mem, batch, seq, norm_g, w_in, conv_w, conv_b, ga_w, ga_b, gx_w, gx_b, lam,
                mem_norm_g, w_mem_kv, w_out):
    xa0, za0 = 0, LRU_WIDTH
    q0 = 2 * LRU_WIDTH
    k0, v0 = q0 + DIL_QKV, q0 + 2 * DIL_QKV
    zb0 = q0 + 3 * DIL_QKV
    qm0 = zb0 + DIL_WIDTH
    zm0 = qm0 + MEM_WIDTH

    def qkv_cols(gi):
        return [(base + gi * DIL_WIDTH, base + (gi + 1) * DIL_WIDTH) for base in (q0, k0, v0)]

    hf = _norm_matmul(x, norm_g, _cols(w_in, (xa0, q0), (zb0, qm0), (zm0, zm0 + MEM_WIDTH)), F32)[0]
    zb_col = 2 * LRU_WIDTH // DIL_WIDTH
    zm_col = (2 * LRU_WIDTH + DIL_WIDTH) // MEM_WIDTH
    qkv = [_norm_matmul(x, norm_g, _cols(w_in, *qkv_cols(0), (qm0, zm0)), BF16)]
    for gi in range(1, len(DIL_GROUPS)):
        qkv.append(_norm_matmul(x, norm_g, _cols(w_in, *qkv_cols(gi)), BF16, dil=DIL_GROUPS[gi][1]))
    qm_col = 3 * DIL_WIDTH // MEM_WIDTH
    n_mem = mem.shape[0] // batch
    mem_kv = _norm_matmul(mem, mem_norm_g, w_mem_kv.astype(BF16), BF16)[0]

    ya = _rglru(hf, conv_w, conv_b, _pack_block_diag(ga_w), ga_b, _pack_block_diag(gx_w), gx_b, lam,
                batch, seq)
    attn = [_dil_attn(qkv[gi], gi, batch, seq) for gi in range(len(DIL_GROUPS))]
    ym = _mem_attn(qkv[0][0], qm_col, mem_kv, hf, zm_col, batch, seq, n_mem)
    return _hawk_out(ya, [o for o, _ in attn], [l for _, l in attn], hf, zb_col, ym,
                     w_out.astype(BF16), x)


def _nsa_layer(x, mem, batch, seq, norm_g, w_in, pe_k, pe_v, phik_w1, phik_w2, phiv_w1, phiv_w2,
               mem_norm_g, w_mem_kv, w_out, final_g):
    kv0 = NSA_WIDTH
    gl0 = kv0 + 6 * NSA_KV
    z0 = gl0 + 3 * NSA_HEADS
    qm0 = z0 + NSA_WIDTH
    zm0 = qm0 + MEM_WIDTH
    nb = _norm_matmul(x, norm_g, _cols(w_in, (0, gl0), (qm0, zm0)), BF16)[0]
    gl_w = jnp.pad(w_in[:, gl0:z0], ((0, 0), (0, LANES - 3 * NSA_HEADS)))
    nf_w = jnp.concatenate([w_in[:, z0:qm0], w_in[:, zm0:zm0 + MEM_WIDTH], gl_w], axis=1).astype(BF16)
    nf = _norm_matmul(x, norm_g, nf_w, F32)[0]
    qm_col = (NSA_WIDTH + 6 * NSA_KV) // MEM_WIDTH
    zm_col = NSA_WIDTH // MEM_WIDTH
    gl_col = (NSA_WIDTH + MEM_WIDTH) // LANES
    n_mem = mem.shape[0] // batch
    mem_kv = _norm_matmul(mem, mem_norm_g, w_mem_kv.astype(BF16), BF16)[0]

    k_cmp, v_cmp_t = _compress(nb[:, kv0:kv0 + 2 * NSA_KV], pe_k, pe_v, phik_w1, phik_w2, phiv_w1,
                               phiv_w2, batch, seq)

    yo = _nsa_attn(nb, k_cmp, v_cmp_t, nf, gl_col, batch, seq)
    ym = _mem_attn(nb, qm_col, mem_kv, nf, zm_col, batch, seq, n_mem)
    return _nsa_out(yo, ym, w_out.astype(BF16), x, final_g)


def kernel(x, mem, hawk_norm, hawk_w_in, hawk_conv_w, hawk_conv_b, hawk_gate_a_w, hawk_gate_a_b,
           hawk_gate_x_w, hawk_gate_x_b, hawk_lambda, hawk_mem_norm, hawk_w_mem_kv, hawk_w_out,
           nsa_norm, nsa_w_in, nsa_pe_k, nsa_pe_v, nsa_phi_k_w1, nsa_phi_k_w2, nsa_phi_v_w1,
           nsa_phi_v_w2, nsa_mem_norm, nsa_w_mem_kv, nsa_w_out, final_norm):
    batch, seq, d = x.shape
    assert hawk_norm.shape[0] == 1 and nsa_norm.shape[0] == 1, "one layer of each kind"
    assert seq % (ATTN_BLOCK * DIL_GROUPS[-1][1]) == 0
    x2 = x.reshape(batch * seq, d)
    mem2 = mem.reshape(batch * mem.shape[1], d)
    x2 = _hawk_layer(x2, mem2, batch, seq, hawk_norm[0], hawk_w_in[0], hawk_conv_w[0], hawk_conv_b[0],
                     hawk_gate_a_w[0], hawk_gate_a_b[0].reshape(-1), hawk_gate_x_w[0],
                     hawk_gate_x_b[0].reshape(-1), hawk_lambda[0], hawk_mem_norm[0], hawk_w_mem_kv[0],
                     hawk_w_out[0])
    out = _nsa_layer(x2, mem2, batch, seq, nsa_norm[0], nsa_w_in[0], nsa_pe_k[0], nsa_pe_v[0],
                     nsa_phi_k_w1[0], nsa_phi_k_w2[0], nsa_phi_v_w1[0], nsa_phi_v_w2[0],
                     nsa_mem_norm[0], nsa_w_mem_kv[0], nsa_w_out[0], final_norm)
    return out.reshape(batch, seq, d)
```

```python
import functools

import numpy as np
import jax
import jax.numpy as jnp
from jax import lax
from jax.experimental import pallas as pl
from jax.experimental.pallas import tpu as pltpu

F32 = jnp.float32
BF16 = jnp.bfloat16

NORM_EPS = 1e-6
NEG_INF = -1e30
LANES = 128
SUBLANES = 8
ATTN_BLOCK = 128
VMEM_LIMIT = 56 * 1024 * 1024

LRU_WIDTH = 1024
LRU_BLOCKS = 16
LRU_BLOCK_DIM = LRU_WIDTH // LRU_BLOCKS
LRU_PACK = 256
CONV_WIDTH = 4
LRU_C = 8.0

DIL_GROUPS = ((128, 1), (512, 4), (2048, 16))
DIL_HEADS = 4
DIL_HEAD_DIM = 128
DIL_WIDTH = DIL_HEADS * DIL_HEAD_DIM
DIL_QKV = len(DIL_GROUPS) * DIL_WIDTH
LSE_LANES = LANES // DIL_HEADS

MEM_HEADS = 4
MEM_HEAD_DIM = 64
MEM_WIDTH = MEM_HEADS * MEM_HEAD_DIM

NSA_HEADS = 16
NSA_KV_GROUPS = 2
NSA_R = NSA_HEADS // NSA_KV_GROUPS
NSA_HEAD_DIM = 64
NSA_WIDTH = NSA_HEADS * NSA_HEAD_DIM
NSA_KV = NSA_KV_GROUPS * NSA_HEAD_DIM
CMP_BLOCK = 32
CMP_STRIDE = 16
SLC_BLOCK = 64
SLC_TOP_N = 8
WIN_SIZE = 512
PHI_HIDDEN = 256
SEL_FORCE = 1e6
CMP_PER_SLC = SLC_BLOCK // CMP_STRIDE


def _alibi_slopes(n):
    return [float(v) for v in np.exp2(-8.0 * np.arange(1, n + 1) / n).astype(np.float32)]


def _params(*semantics):
    return pltpu.CompilerParams(dimension_semantics=semantics, vmem_limit_bytes=VMEM_LIMIT)


def _silu(z):
    return z * jax.nn.sigmoid(z)


def _dot_t(a, b):
    return lax.dot_general(a, b, (((1,), (1,)), ((), ())), preferred_element_type=F32)


def _rms_norm_rows(x, g):
    ms = jnp.mean(x * x, axis=-1, keepdims=True)
    return (x * lax.rsqrt(ms + NORM_EPS) * g).astype(BF16)


def _norm_matmul_kernel(*refs, dil):
    if dil == 1:
        x_ref, g_ref, w_ref, o_ref = refs
        xn = _rms_norm_rows(x_ref[...], g_ref[...])
        o_ref[0] = jnp.dot(xn, w_ref[...], preferred_element_type=F32).astype(o_ref.dtype)
        return
    *x_refs, g_ref, w_ref, o_ref, xn_ref = refs
    per = x_refs[0].shape[0] // dil
    inv_k = 1.0 / (len(x_refs) * LANES)
    for c in range(dil):
        xs = [x_ref[pl.ds(c, per, stride=dil), :] for x_ref in x_refs]
        ss = xs[0] * xs[0]
        for xj in xs[1:]:
            ss = ss + xj * xj
        r = lax.rsqrt(jnp.sum(ss, axis=-1, keepdims=True) * inv_k + NORM_EPS)
        for j, xj in enumerate(xs):
            cols = slice(j * LANES, (j + 1) * LANES)
            xn_ref[c * per:(c + 1) * per, cols] = (xj * r * g_ref[:, cols]).astype(BF16)
    res = jnp.dot(xn_ref[...], w_ref[...], preferred_element_type=F32).astype(o_ref.dtype)
    for c in range(dil):
        o_ref[c] = res[c * per:(c + 1) * per]


def _norm_matmul(x, g, w, out_dtype, dil=1, tm=256):
    m, k = x.shape
    n = w.shape[1]
    per = tm // dil
    assert m % tm == 0 and tm % dil == 0 and per % 16 == 0 and k % LANES == 0
    if dil == 1:
        x_specs = [pl.BlockSpec((tm, k), lambda i: (i, 0))]
    else:
        x_specs = [pl.BlockSpec((tm, LANES), functools.partial(lambda i, j: (i, j), j=j))
                   for j in range(k // LANES)]
    return pl.pallas_call(
        functools.partial(_norm_matmul_kernel, dil=dil),
        grid=(m // tm,),
        in_specs=x_specs + [pl.BlockSpec((1, k), lambda i: (0, 0)),
                            pl.BlockSpec((k, n), lambda i: (0, 0))],
        out_specs=pl.BlockSpec((dil, per, n), lambda i: (0, i, 0)),
        out_shape=jax.ShapeDtypeStruct((dil, m // dil, n), out_dtype),
        scratch_shapes=[pltpu.VMEM((tm, k), BF16)] if dil > 1 else [],
        compiler_params=_params("arbitrary"),
        name="norm_matmul",
    )(*([x] * len(x_specs)), g.reshape(1, k), w)


def _rglru_kernel(xa_ref, za_ref, cw_ref, cb_ref, wa_ref, ba_ref, wx_ref, bx_ref, lam_ref,
                  o_ref, xpad_ref, h_ref):
    t = pl.program_id(1)
    tt, width = xa_ref.shape
    halo = 8

    @pl.when(t == 0)
    def _():
        xpad_ref[0:halo, :] = jnp.zeros((halo, width), F32)
        h_ref[...] = jnp.zeros_like(h_ref)

    x = xa_ref[...]
    xpad_ref[halo:halo + tt, :] = x
    cw = cw_ref[...]
    y = cw[CONV_WIDTH - 1:CONV_WIDTH] * x
    for k in range(1, CONV_WIDTH):
        y = y + cw[CONV_WIDTH - 1 - k:CONV_WIDTH - k] * xpad_ref[halo - k:halo - k + tt, :]
    y = y + cb_ref[...]
    xpad_ref[0:halo, :] = x[tt - halo:tt, :]

    yb = y.astype(BF16)
    r_parts, i_parts = [], []
    for p in range(width // LRU_PACK):
        ys = yb[:, p * LRU_PACK:(p + 1) * LRU_PACK]
        r_parts.append(jnp.dot(ys, wa_ref[p], preferred_element_type=F32))
        i_parts.append(jnp.dot(ys, wx_ref[p], preferred_element_type=F32))
    r = jax.nn.sigmoid(jnp.concatenate(r_parts, axis=1) + ba_ref[...])
    gi = jax.nn.sigmoid(jnp.concatenate(i_parts, axis=1) + bx_ref[...])

    nl = -lam_ref[...]
    softplus = jnp.maximum(nl, 0.0) + jnp.log1p(jnp.exp(-jnp.abs(nl)))
    log_a = (-LRU_C) * r * softplus
    a = jnp.exp(log_a)
    z2 = 2.0 * log_a
    u = a * a
    one_minus = jnp.where(u == 1.0, -z2, jnp.where(u < 0.5, 1.0 - u, (1.0 - u) * z2 / jnp.log(u)))
    mult = jnp.sqrt(one_minus)
    b = y * gi * mult
    first = (lax.broadcasted_iota(jnp.int32, (SUBLANES, width), 0) == 0) & (t == 0)
    b = jnp.concatenate([jnp.where(first, (y * gi)[0:SUBLANES], b[0:SUBLANES]), b[SUBLANES:]], axis=0)

    groups = tt // SUBLANES
    a3 = a.reshape(groups, SUBLANES, width)
    b3 = b.reshape(groups, SUBLANES, width)
    sub = lax.broadcasted_iota(jnp.int32, (groups, SUBLANES, width), 1)
    k = 1
    while k < SUBLANES:
        keep = sub >= k
        a_sh = jnp.where(keep, pltpu.roll(a3, k, axis=1), 1.0)
        b_sh = jnp.where(keep, pltpu.roll(b3, k, axis=1), 0.0)
        b3 = a3 * b_sh + b3
        a3 = a3 * a_sh
        k *= 2
    carry = jnp.broadcast_to(h_ref[...], (SUBLANES, width))
    hs = []
    for gidx in range(groups):
        hg = a3[gidx] * carry + b3[gidx]
        hs.append(hg)
        carry = jnp.broadcast_to(hg[SUBLANES - 1:SUBLANES], (SUBLANES, width))
    h = jnp.concatenate(hs, axis=0)
    h_ref[...] = carry[0:1]
    o_ref[...] = (h * _silu(za_ref[...])).astype(o_ref.dtype)


def _rglru(hf, conv_w, conv_b, wa, ba, wx, bx, lam, batch, seq, tt=256):
    width = LRU_WIDTH
    nt = seq // tt
    packs = width // LRU_PACK
    vec = pl.BlockSpec((1, width), lambda b, t: (0, 0))
    gate_w = pl.BlockSpec((packs, LRU_PACK, LRU_PACK), lambda b, t: (0, 0, 0))
    return pl.pallas_call(
        _rglru_kernel,
        grid=(batch, nt),
        in_specs=[pl.BlockSpec((tt, width), lambda b, t: (b * nt + t, 0)),
                  pl.BlockSpec((tt, width), lambda b, t: (b * nt + t, 1)),
                  pl.BlockSpec((CONV_WIDTH, width), lambda b, t: (0, 0)),
                  vec, gate_w, vec, gate_w, vec, vec],
        out_specs=pl.BlockSpec((tt, width), lambda b, t: (b * nt + t, 0)),
        out_shape=jax.ShapeDtypeStruct((batch * seq, width), BF16),
        scratch_shapes=[pltpu.VMEM((tt + 8, width), F32), pltpu.VMEM((1, width), F32)],
        compiler_params=_params("arbitrary", "arbitrary"),
        name="rglru",
    )(hf, hf, conv_w, conv_b.reshape(1, width), wa, ba.reshape(1, width), wx, bx.reshape(1, width),
      lam.reshape(1, width))


def _pack_block_diag(w):
    per = LRU_PACK // LRU_BLOCK_DIM
    w = w.reshape(LRU_BLOCKS // per, per, LRU_BLOCK_DIM, LRU_BLOCK_DIM)
    eye = jnp.eye(per, dtype=w.dtype)
    packed = w[:, :, :, None, :] * eye[None, :, None, :, None]
    return packed.reshape(LRU_BLOCKS // per, LRU_PACK, LRU_PACK).astype(BF16)


def _dil_attn_kernel(*refs, slopes, pos_scale, max_dist, has_halo, dil, n_cls, n_blk):
    if has_halo:
        q_ref, kh_ref, k_ref, vh_ref, v_ref = refs[:5]
        out_refs = refs[5:]
    else:
        q_ref, k_ref, v_ref = refs[:3]
        out_refs = refs[3:]
    n_out = DIL_HEADS + 1
    dst_refs = out_refs[:n_out]
    stage_refs = out_refs[n_out:] if dil > 1 else dst_refs
    first_super = pl.program_id(1) == 0
    cls0 = pl.program_id(2) * n_cls
    blk = ATTN_BLOCK
    scale = DIL_HEAD_DIM ** -0.5

    def band(width, halo_live):
        row = lax.broadcasted_iota(jnp.int32, (blk, width), 0)
        col = lax.broadcasted_iota(jnp.int32, (blk, width), 1)
        dist = (width - blk) + row - col
        valid = (dist >= 0) & (dist <= max_dist)
        if halo_live is not None:
            valid = valid & ((col >= blk) | halo_live)
        return valid, (dist * pos_scale).astype(F32)

    def scores(cc, jb):
        cur = slice(jb * blk, (jb + 1) * blk)
        if jb > 0:
            valid, distf = band(2 * blk, None)
        elif has_halo:
            valid, distf = band(2 * blk, jnp.logical_not(first_super))
        else:
            valid, distf = band(blk, None)
        out = []
        for h in range(DIL_HEADS):
            hs = slice(h * DIL_HEAD_DIM, (h + 1) * DIL_HEAD_DIM)
            q = q_ref[cc, cur, hs]
            if jb > 0:
                k = k_ref[cc, (jb - 1) * blk:(jb + 1) * blk, hs]
                v = v_ref[cc, (jb - 1) * blk:(jb + 1) * blk, hs]
            elif has_halo:
                k = jnp.concatenate([kh_ref[cc, :, hs], k_ref[cc, cur, hs]], axis=0)
                v = jnp.concatenate([vh_ref[cc, :, hs], v_ref[cc, cur, hs]], axis=0)
            else:
                k, v = k_ref[cc, cur, hs], v_ref[cc, cur, hs]
            s = _dot_t(q, k) * scale - slopes[h] * distf
            out.append((jnp.where(valid, s, NEG_INF), v))
        return out

    def finish(cc, jb, pairs):
        where = (cls0 + cc, slice(jb * blk, (jb + 1) * blk)) if dil > 1 else (slice(jb * blk, (jb + 1) * blk),)
        lses = []
        for h, (s, v) in enumerate(pairs):
            m = jnp.max(s, axis=-1, keepdims=True)
            e = jnp.exp(s - m)
            den = jnp.sum(e, axis=-1, keepdims=True)
            stage_refs[h][where] = jnp.dot(e.astype(BF16), v, preferred_element_type=F32) / den
            lses.append(jnp.broadcast_to(m + jnp.log(den), (blk, LSE_LANES)))
        stage_refs[DIL_HEADS][where] = jnp.concatenate(lses, axis=1)

    pending = None
    for cc in range(n_cls):
        for jb in range(n_blk):
            pairs = scores(cc, jb)
            if pending is not None:
                finish(*pending)
            pending = (cc, jb, pairs)
    finish(*pending)

    if dil > 1:
        @pl.when(pl.program_id(2) == pl.num_programs(2) - 1)
        def _():
            for stage, dst in zip(stage_refs, dst_refs):
                dst[...] = jnp.swapaxes(stage[...], 0, 1).reshape(dst.shape)


def _dil_attn(qkv, gi, batch, seq, work=4):
    window, dil = DIL_GROUPS[gi]
    sub = seq // dil
    nb = sub // ATTN_BLOCK
    n_blk = min(work, nb)
    n_cls = min(work // n_blk, dil)
    n_super = nb // n_blk
    has_halo = n_super > 1
    span = n_blk * ATTN_BLOCK
    slopes = _alibi_slopes(len(DIL_GROUPS) * DIL_HEADS)[gi * DIL_HEADS:(gi + 1) * DIL_HEADS]
    cur = lambda col: pl.BlockSpec((n_cls, span, DIL_WIDTH), lambda b, i, c: (c, b * n_super + i, col))
    halo = lambda col: pl.BlockSpec(
        (n_cls, ATTN_BLOCK, DIL_WIDTH),
        lambda b, i, c: (c, jnp.maximum((b * n_super + i) * n_blk - 1, 0), col))
    if has_halo:
        in_specs = [cur(0), halo(1), cur(1), halo(2), cur(2)]
    else:
        in_specs = [cur(0), cur(1), cur(2)]
    n_out = DIL_HEADS + 1
    *o, lse = pl.pallas_call(
        functools.partial(_dil_attn_kernel, slopes=slopes, pos_scale=dil, max_dist=window // dil,
                          has_halo=has_halo, dil=dil, n_cls=n_cls, n_blk=n_blk),
        grid=(batch, n_super, dil // n_cls),
        in_specs=in_specs,
        out_specs=[pl.BlockSpec((span * dil, LANES), lambda b, i, c: (b * n_super + i, 0))] * n_out,
        out_shape=[jax.ShapeDtypeStruct((batch * seq, LANES), F32)] * n_out,
        scratch_shapes=[pltpu.VMEM((dil, span, LANES), F32)] * (n_out if dil > 1 else 0),
        compiler_params=_params("arbitrary", "arbitrary", "arbitrary"),
        name=f"dil_attn_d{dil}",
    )(*([qkv] * len(in_specs)))
    return o, lse


def _mem_attn_kernel(q_ref, k_ref, v_ref, z_ref, o_ref):
    scale = MEM_HEAD_DIM ** -0.5
    heads = [slice(h * MEM_HEAD_DIM, (h + 1) * MEM_HEAD_DIM) for h in range(MEM_HEADS)]
    scores = [_dot_t(q_ref[:, hs], k_ref[:, hs]) * scale for hs in heads]
    outs = []
    for hs, s in zip(heads, scores):
        m = jnp.max(s, axis=-1, keepdims=True)
        e = jnp.exp(s - m)
        den = jnp.sum(e, axis=-1, keepdims=True)
        outs.append(jnp.dot(e.astype(BF16), v_ref[:, hs], preferred_element_type=F32) / den)
    o_ref[...] = (jnp.concatenate(outs, axis=1) * _silu(z_ref[...])).astype(o_ref.dtype)


def _mem_attn(qsrc, q_col, kv, zsrc, z_col, batch, seq, n_mem, tq=512):
    nq = seq // tq
    return pl.pallas_call(
        _mem_attn_kernel,
        grid=(batch, nq),
        in_specs=[pl.BlockSpec((tq, MEM_WIDTH), lambda b, i: (b * nq + i, q_col)),
                  pl.BlockSpec((n_mem, MEM_WIDTH), lambda b, i: (b, 0)),
                  pl.BlockSpec((n_mem, MEM_WIDTH), lambda b, i: (b, 1)),
                  pl.BlockSpec((tq, MEM_WIDTH), lambda b, i: (b * nq + i, z_col))],
        out_specs=pl.BlockSpec((tq, MEM_WIDTH), lambda b, i: (b * nq + i, 0)),
        out_shape=jax.ShapeDtypeStruct((batch * seq, MEM_WIDTH), BF16),
        compiler_params=_params("arbitrary", "arbitrary"),
        name="mem_attn",
    )(qsrc, kv, kv, zsrc)


def _hawk_out_kernel(*refs):
    n_groups = len(DIL_GROUPS)
    ya_ref = refs[0]
    o_refs = refs[1:1 + n_groups * DIL_HEADS]
    l_refs = refs[1 + n_groups * DIL_HEADS:1 + n_groups * (DIL_HEADS + 1)]
    zb_ref, ym_ref, w_ref, x_ref, out_ref = refs[1 + n_groups * (DIL_HEADS + 1):]
    parts = []
    for h in range(DIL_HEADS):
        ls = [l[:, h * LSE_LANES:h * LSE_LANES + 1] for l in l_refs]
        m = functools.reduce(jnp.maximum, ls)
        ws = [jnp.exp(l - m) for l in ls]
        num = sum(w * o_refs[gi * DIL_HEADS + h][...] for gi, w in enumerate(ws))
        parts.append(num / sum(ws))
    yb = (jnp.concatenate(parts, axis=1) * _silu(zb_ref[...])).astype(BF16)
    a_end = LRU_WIDTH
    b_end = a_end + DIL_WIDTH
    y = jnp.dot(ya_ref[...], w_ref[0:a_end, :], preferred_element_type=F32)
    y = y + jnp.dot(yb, w_ref[a_end:b_end, :], preferred_element_type=F32)
    y = y + jnp.dot(ym_ref[...], w_ref[b_end:b_end + MEM_WIDTH, :], preferred_element_type=F32)
    out_ref[...] = x_ref[...] + y


def _hawk_out(ya, os_, ls_, hf, zb_col, ym, w, x, tm=256):
    m, d = x.shape
    row = lambda width, col=0: pl.BlockSpec((tm, width), lambda i: (i, col))
    heads = [o for group in os_ for o in group]
    return pl.pallas_call(
        _hawk_out_kernel,
        grid=(m // tm,),
        in_specs=[row(LRU_WIDTH)] + [row(DIL_HEAD_DIM)] * len(heads) + [row(LANES)] * len(ls_)
                 + [row(DIL_WIDTH, zb_col), row(MEM_WIDTH),
                    pl.BlockSpec(w.shape, lambda i: (0, 0)), row(d)],
        out_specs=row(d),
        out_shape=jax.ShapeDtypeStruct((m, d), F32),
        compiler_params=_params("arbitrary"),
        name="hawk_out",
    )(ya, *heads, *ls_, hf, ym, w, x)


def _compress_kernel(x_ref, pe_ref, w1_ref, w2k_ref, w2vt_ref, k_ref, vt_ref):
    x = x_ref[...]
    n_blk = x.shape[0]
    first = jnp.dot(x, w1_ref[0], preferred_element_type=F32)
    second = jnp.dot(x, w1_ref[1], preferred_element_type=F32)
    pe = (jnp.dot(pe_ref[0], w1_ref[0], preferred_element_type=F32)
          + jnp.dot(pe_ref[1], w1_ref[1], preferred_element_type=F32))
    hid = first + pltpu.roll(second, n_blk - 1, axis=0) + pe[0:1, :]
    act = _silu(hid).astype(BF16)
    hidden = lambda c: act[:, c * PHI_HIDDEN:(c + 1) * PHI_HIDDEN]
    ks = [jnp.dot(hidden(g), w2k_ref[...], preferred_element_type=F32) for g in range(NSA_KV_GROUPS)]
    vts = [_dot_t(w2vt_ref[...], hidden(NSA_KV_GROUPS + g)) for g in range(NSA_KV_GROUPS)]
    k_ref[...] = jnp.concatenate(ks, axis=1).astype(k_ref.dtype)
    vt_ref[...] = jnp.concatenate(vts, axis=0).astype(vt_ref.dtype)


def _compress(kcv, pe_k, pe_v, k_w1, k_w2, v_w1, v_w2, batch, seq):
    half = CMP_BLOCK // 2
    assert half == CMP_STRIDE
    n_blk = seq // CMP_STRIDE
    cols = 2 * NSA_KV
    x = kcv.reshape(batch, n_blk, half * cols)
    w1 = jnp.stack([k_w1, v_w1]).reshape(2, 2, half, NSA_HEAD_DIM, PHI_HIDDEN)
    eye = jnp.eye(2, dtype=w1.dtype)
    sel = eye[:, None, :, None] * eye[None, :, None, :]
    w1e = w1.transpose(1, 2, 0, 3, 4)[:, :, :, None, :, None, None, :] * sel[None, None, :, :, None, :, :, None]
    w1e = w1e.reshape(2, half * cols, 2 * NSA_KV_GROUPS * PHI_HIDDEN).astype(BF16)
    pe = jnp.stack([pe_k, pe_v]).reshape(2, 2, half, 1, NSA_HEAD_DIM)
    pe = jnp.broadcast_to(pe.transpose(1, 2, 0, 3, 4), (2, half, 2, NSA_KV_GROUPS, NSA_HEAD_DIM))
    pe = jnp.broadcast_to(pe.reshape(2, 1, half * cols), (2, 8, half * cols)).astype(BF16)
    w2k = k_w2.astype(BF16)
    w2vt = v_w2.T.astype(BF16)
    whole = lambda a: pl.BlockSpec(a.shape, lambda b: (0,) * a.ndim)
    return pl.pallas_call(
        _compress_kernel,
        grid=(batch,),
        in_specs=[pl.BlockSpec((None, n_blk, half * cols), lambda b: (b, 0, 0)),
                  whole(pe), whole(w1e), whole(w2k), whole(w2vt)],
        out_specs=[pl.BlockSpec((None, n_blk, NSA_KV), lambda b: (b, 0, 0)),
                   pl.BlockSpec((None, NSA_KV, n_blk), lambda b: (b, 0, 0))],
        out_shape=[jax.ShapeDtypeStruct((batch, n_blk, NSA_KV), BF16),
                   jax.ShapeDtypeStruct((batch, NSA_KV, n_blk), BF16)],
        compiler_params=_params("arbitrary"),
        name="compress",
    )(x, pe, w1e, w2k, w2vt)


KEY_CHUNK = 256
NSA_TQ = 256
SLOPE_PIECES = 3
N_FEATS = 2 * SLOPE_PIECES
WIN_KEY_COLS = 2 * NSA_HEAD_DIM
SLC_KEY_COLS = WIN_KEY_COLS + LANES


def _slope_pieces(slope):
    rest = np.float32(slope)
    pieces = []
    for _ in range(SLOPE_PIECES):
        p = np.float32(np.asarray(rest).astype(BF16))
        pieces.append(float(p))
        rest = np.float32(rest - p)
    return pieces


def _lane_table(lane, values):
    out = jnp.zeros(lane.shape, F32)
    for idx, v in enumerate(values):
        out = jnp.where(lane == idx, v, out)
    return out


def _key_feats(pos_hi, pos_lo, lane):
    return jnp.where(lane < SLOPE_PIECES, pos_hi, jnp.where(lane < N_FEATS, pos_lo, 0)).astype(F32)


def _tile_heads(x):
    return jnp.concatenate([x] * NSA_R, axis=1)


def _nsa_kernel(q_ref, kc_ref, vct_ref, ks_ref, vst_ref, kw_ref, vwt_ref, gl_ref, z_ref, o_ref,
                s_ref, acc_ref):
    i = pl.program_id(1)
    tq = q_ref.shape[0]
    hd = NSA_HEAD_DIM
    n_cmp = kc_ref.shape[0]
    slopes_all = _alibi_slopes(NSA_HEADS)
    gates_t = jax.nn.sigmoid(gl_ref[...]).T
    feat_lane = lax.broadcasted_iota(jnp.int32, (tq, hd), 1)
    key_row = lax.broadcasted_iota(jnp.int32, (KEY_CHUNK, tq), 0)
    t_pos = i * tq + lax.broadcasted_iota(jnp.int32, (KEY_CHUNK, tq), 1)
    ones_rows = jnp.ones((16, KEY_CHUNK), BF16)
    win_lo = jnp.maximum(i * tq - (WIN_SIZE - 1), 0) // KEY_CHUNK
    chunks_hi = (i * tq + tq - 1) // KEY_CHUNK + 1

    for g in range(NSA_KV_GROUPS):
        slopes = slopes_all[g * NSA_R:(g + 1) * NSA_R]
        gs = slice(g * hd, (g + 1) * hd)
        q_parts = []
        for r in range(NSA_R):
            qr = q_ref[:, (g * NSA_R + r) * hd:(g * NSA_R + r + 1) * hd].astype(F32) * (hd ** -0.5)
            feats = _lane_table(feat_lane, _slope_pieces(slopes[r]) * 2)
            q_parts.append(jnp.concatenate([qr, feats], axis=1).astype(BF16))
        q_aug = jnp.concatenate(q_parts, axis=0)

        n_row = lax.broadcasted_iota(jnp.int32, (n_cmp, tq), 0)
        t_cmp = i * tq + lax.broadcasted_iota(jnp.int32, (n_cmp, tq), 1)
        visible = t_cmp >= n_row * CMP_STRIDE + (CMP_BLOCK - 1)
        cfeat_row = lax.broadcasted_iota(jnp.int32, (n_cmp, hd), 0)
        cfeat_lane = lax.broadcasted_iota(jnp.int32, (n_cmp, hd), 1)
        kc_feats = _key_feats(cfeat_row * CMP_STRIDE, 0, cfeat_lane)
        kc_aug = jnp.concatenate([kc_ref[:, gs], kc_feats.astype(BF16)], axis=1)
        s = _dot_t(kc_aug, q_aug) + _tile_heads(jnp.where(visible, 0.0, NEG_INF))
        m = jnp.max(s, axis=0, keepdims=True)
        e = jnp.exp(s - m)
        t_one = i * tq + lax.broadcasted_iota(jnp.int32, (1, NSA_R * tq), 1) % tq
        any_visible = t_one >= (CMP_BLOCK - 1)
        p = jnp.where(any_visible, e / jnp.sum(e, axis=0, keepdims=True), 0.0)
        o_cmp = jnp.dot(vct_ref[gs, :], p.astype(BF16), preferred_element_type=F32)
        p_sum = p[:, 0:tq]
        for r in range(1, NSA_R):
            p_sum = p_sum + p[:, r * tq:(r + 1) * tq]

        imp = p_sum + pltpu.roll(p_sum, 1, axis=0)
        for k in range(1, CMP_PER_SLC):
            imp = imp + pltpu.roll(p_sum, n_cmp - k, axis=0)
        blk_j = n_row // CMP_PER_SLC
        cur = t_cmp // SLC_BLOCK
        forced = (blk_j == 0) | (blk_j == cur) | (blk_j == cur - 1)
        imp = jnp.where(forced, SEL_FORCE, jnp.where(blk_j > cur, -SEL_FORCE, imp))
        dead = -3.0e38
        v_imp = jnp.where(n_row % CMP_PER_SLC == 0, imp, dead)
        n_row_f = n_row.astype(F32)
        sel = jnp.zeros((n_cmp, tq), F32)
        for _ in range(SLC_TOP_N):
            mx = jnp.max(v_imp, axis=0, keepdims=True)
            first = jnp.min(jnp.where(v_imp == mx, n_row_f, 2.0 * n_cmp), axis=0, keepdims=True)
            pick = n_row_f == first
            sel = jnp.where(pick, 1.0, sel)
            v_imp = jnp.where(pick, dead, v_imp)

        sel_bias = jnp.where(sel > 0.5, 0.0, NEG_INF).T.astype(BF16)
        q_slc = jnp.concatenate([q_aug, jnp.concatenate([sel_bias] * NSA_R, axis=0)], axis=1)

        def attend(q_br, k_ref, k_cols, vt_ref, lo, hi, masked_from, mask_fn):
            def scores(c, m_run, masked):
                start = pl.multiple_of(c * KEY_CHUNK, KEY_CHUNK)
                sc = _dot_t(k_ref[pl.ds(start, KEY_CHUNK), k_cols], q_br)
                if masked:
                    ok = mask_fn(t_pos - (start + key_row))
                    sc = sc + _tile_heads(jnp.where(ok, 0.0, NEG_INF))
                s_ref[c] = sc
                return jnp.maximum(m_run, jnp.max(sc, axis=0, keepdims=True))

            m_row = jnp.full((1, NSA_R * tq), NEG_INF, F32)
            m_row = lax.fori_loop(lo, masked_from, functools.partial(scores, masked=False), m_row)
            m_row = lax.fori_loop(masked_from, hi, functools.partial(scores, masked=True), m_row)
            acc_ref[...] = jnp.zeros(acc_ref.shape, F32)

            def weighted(c, carry):
                e = jnp.exp(s_ref[c] - m_row).astype(BF16)
                v_ext = jnp.concatenate([vt_ref[c, gs, :], ones_rows], axis=0)
                acc_ref[...] += jnp.dot(v_ext, e, preferred_element_type=F32)
                return carry

            lax.fori_loop(lo, hi, weighted, 0)
            return acc_ref[0:hd, :] / acc_ref[hd:hd + 1, :]

        slc_cols = slice(g * SLC_KEY_COLS, (g + 1) * SLC_KEY_COLS)
        win_cols = slice(g * WIN_KEY_COLS, (g + 1) * WIN_KEY_COLS)
        o_slc = attend(q_slc, ks_ref, slc_cols, vst_ref, 0, chunks_hi, (i * tq) // KEY_CHUNK, lambda dist: dist >= 0)
        o_win = attend(q_aug, kw_ref, win_cols, vwt_ref, win_lo, chunks_hi, win_lo,
                       lambda dist: (dist >= 0) & (dist <= WIN_SIZE - 1))

        def gate(kind):
            base = g * NSA_R * 3 + kind
            return jnp.concatenate([gates_t[base + 3 * r:base + 3 * r + 1, :] for r in range(NSA_R)], axis=1)

        o = gate(0) * o_cmp + gate(1) * o_slc + gate(2) * o_win
        pairs = []
        for r in range(0, NSA_R, 2):
            two = jnp.concatenate([o[:, r * tq:(r + 1) * tq], o[:, (r + 1) * tq:(r + 2) * tq]], axis=0)
            pairs.append(two.T)
        cs = slice(g * NSA_R * hd, (g + 1) * NSA_R * hd)
        o_ref[:, cs] = (jnp.concatenate(pairs, axis=1) * _silu(z_ref[:, cs])).astype(o_ref.dtype)


def _key_pos_feats(seq):
    pos = np.arange(seq)
    feats = np.zeros((seq, NSA_HEAD_DIM), np.float32)
    feats[:, 0:SLOPE_PIECES] = ((pos // SLC_BLOCK) * SLC_BLOCK)[:, None]
    feats[:, SLOPE_PIECES:N_FEATS] = (pos % SLC_BLOCK)[:, None]
    onehot = (np.arange(LANES)[None, :] == (pos // SLC_BLOCK * CMP_PER_SLC)[:, None]).astype(np.float32)
    return jnp.asarray(feats, BF16), jnp.asarray(onehot, BF16)


def _nsa_attn(nb, k_cmp, v_cmp_t, nf, gl_col, batch, seq):
    tq = NSA_TQ
    nq = seq // tq
    hd = NSA_HEAD_DIM
    kv0 = NSA_WIDTH

    def branch(idx):
        return nb[:, kv0 + idx * NSA_KV:kv0 + (idx + 1) * NSA_KV].reshape(batch, seq, NSA_KV)

    def chunked_t(v):
        return v.reshape(batch, seq // KEY_CHUNK, KEY_CHUNK, NSA_KV).transpose(0, 1, 3, 2)

    feats, onehot = _key_pos_feats(seq)
    bcast = lambda a: jnp.broadcast_to(a[None], (batch,) + a.shape)
    k_slc, k_win = branch(2), branch(4)
    ks_aug = jnp.concatenate([t for g in range(NSA_KV_GROUPS)
                              for t in (k_slc[..., g * hd:(g + 1) * hd], bcast(feats), bcast(onehot))], axis=-1)
    kw_aug = jnp.concatenate([t for g in range(NSA_KV_GROUPS)
                              for t in (k_win[..., g * hd:(g + 1) * hd], bcast(feats))], axis=-1)
    vs_t, vw_t = chunked_t(branch(3)), chunked_t(branch(5))
    per_batch = lambda a: pl.BlockSpec((None,) + a.shape[1:], lambda b, i: (b,) + (0,) * (a.ndim - 1))
    return pl.pallas_call(
        _nsa_kernel,
        grid=(batch, nq),
        in_specs=[pl.BlockSpec((tq, NSA_WIDTH), lambda b, i: (b * nq + i, 0)),
                  per_batch(k_cmp), per_batch(v_cmp_t),
                  per_batch(ks_aug), per_batch(vs_t), per_batch(kw_aug), per_batch(vw_t),
                  pl.BlockSpec((tq, LANES), lambda b, i: (b * nq + i, gl_col)),
                  pl.BlockSpec((tq, NSA_WIDTH), lambda b, i: (b * nq + i, 0))],
        out_specs=pl.BlockSpec((tq, NSA_WIDTH), lambda b, i: (b * nq + i, 0)),
        out_shape=jax.ShapeDtypeStruct((batch * seq, NSA_WIDTH), BF16),
        scratch_shapes=[pltpu.VMEM((seq // KEY_CHUNK, KEY_CHUNK, NSA_R * tq), F32),
                        pltpu.VMEM((NSA_HEAD_DIM + 16, NSA_R * tq), F32)],
        compiler_params=_params("arbitrary", "arbitrary"),
        name="nsa_attn",
    )(nb, k_cmp, v_cmp_t, ks_aug, vs_t, kw_aug, vw_t, nf, nf)


def _nsa_out_kernel(yo_ref, ym_ref, w_ref, x_ref, g_ref, out_ref):
    y = jnp.dot(yo_ref[...], w_ref[0:NSA_WIDTH, :], preferred_element_type=F32)
    y = y + jnp.dot(ym_ref[...], w_ref[NSA_WIDTH:NSA_WIDTH + MEM_WIDTH, :], preferred_element_type=F32)
    x = x_ref[...] + y
    ms = jnp.mean(x * x, axis=-1, keepdims=True)
    out_ref[...] = x * lax.rsqrt(ms + NORM_EPS) * g_ref[...]


def _nsa_out(yo, ym, w, x, g, tm=256):
    m, d = x.shape
    row = lambda width: pl.BlockSpec((tm, width), lambda i: (i, 0))
    return pl.pallas_call(
        _nsa_out_kernel,
        grid=(m // tm,),
        in_specs=[row(NSA_WIDTH), row(MEM_WIDTH), pl.BlockSpec(w.shape, lambda i: (0, 0)), row(d),
                  pl.BlockSpec((1, d), lambda i: (0, 0))],
        out_specs=row(d),
        out_shape=jax.ShapeDtypeStruct((m, d), F32),
        compiler_params=_params("arbitrary"),
        name="nsa_out",
    )(yo, ym, w, x, g.reshape(1, d))


def _cols(w, *ranges):
    return jnp.concatenate([w[:, a:b] for a, b in ranges], axis=1).astype(BF16)


def _hawk_layer(x, mem, batch, seq, norm_g, w_in, conv_w, conv_b, ga_w, ga_b, gx_w, gx_b, lam,
                mem_norm_g, w_mem_kv, w_out):
    xa0, za0 = 0, LRU_WIDTH
    q0 = 2 * LRU_WIDTH
    k0, v0 = q0 + DIL_QKV, q0 + 2 * DIL_QKV
    zb0 = q0 + 3 * DIL_QKV
    qm0 = zb0 + DIL_WIDTH
    zm0 = qm0 + MEM_WIDTH

    def qkv_cols(gi):
        return [(base + gi * DIL_WIDTH, base + (gi + 1) * DIL_WIDTH) for base in (q0, k0, v0)]

    hf = _norm_matmul(x, norm_g, _cols(w_in, (xa0, q0), (zb0, qm0), (zm0, zm0 + MEM_WIDTH)), F32)[0]
    zb_col = 2 * LRU_WIDTH // DIL_WIDTH
    zm_col = (2 * LRU_WIDTH + DIL_WIDTH) // MEM_WIDTH
    qkv = [_norm_matmul(x, norm_g, _cols(w_in, *qkv_cols(0), (qm0, zm0)), BF16)]
    for gi in range(1, len(DIL_GROUPS)):
        qkv.append(_norm_matmul(x, norm_g, _cols(w_in, *qkv_cols(gi)), BF16, dil=DIL_GROUPS[gi][1], tm=512))
    qm_col = 3 * DIL_WIDTH // MEM_WIDTH
    n_mem = mem.shape[0] // batch
    mem_kv = _norm_matmul(mem, mem_norm_g, w_mem_kv.astype(BF16), BF16)[0]

    ya = _rglru(hf, conv_w, conv_b, _pack_block_diag(ga_w), ga_b, _pack_block_diag(gx_w), gx_b, lam,
                batch, seq)
    attn = [_dil_attn(qkv[gi], gi, batch, seq) for gi in range(len(DIL_GROUPS))]
    ym = _mem_attn(qkv[0][0], qm_col, mem_kv, hf, zm_col, batch, seq, n_mem)
    return _hawk_out(ya, [o for o, _ in attn], [l for _, l in attn], hf, zb_col, ym,
                     w_out.astype(BF16), x)


def _nsa_layer(x, mem, batch, seq, norm_g, w_in, pe_k, pe_v, phik_w1, phik_w2, phiv_w1, phiv_w2,
               mem_norm_g, w_mem_kv, w_out, final_g):
    kv0 = NSA_WIDTH
    gl0 = kv0 + 6 * NSA_KV
    z0 = gl0 + 3 * NSA_HEADS
    qm0 = z0 + NSA_WIDTH
    zm0 = qm0 + MEM_WIDTH
    nb = _norm_matmul(x, norm_g, _cols(w_in, (0, gl0), (qm0, zm0)), BF16)[0]
    gl_w = jnp.pad(w_in[:, gl0:z0], ((0, 0), (0, LANES - 3 * NSA_HEADS)))
    nf_w = jnp.concatenate([w_in[:, z0:qm0], w_in[:, zm0:zm0 + MEM_WIDTH], gl_w], axis=1).astype(BF16)
    nf = _norm_matmul(x, norm_g, nf_w, F32)[0]
    qm_col = (NSA_WIDTH + 6 * NSA_KV) // MEM_WIDTH
    zm_col = NSA_WIDTH // MEM_WIDTH
    gl_col = (NSA_WIDTH + MEM_WIDTH) // LANES
    n_mem = mem.shape[0] // batch
    mem_kv = _norm_matmul(mem, mem_norm_g, w_mem_kv.astype(BF16), BF16)[0]

    k_cmp, v_cmp_t = _compress(nb[:, kv0:kv0 + 2 * NSA_KV], pe_k, pe_v, phik_w1, phik_w2, phiv_w1,
                               phiv_w2, batch, seq)

    yo = _nsa_attn(nb, k_cmp, v_cmp_t, nf, gl_col, batch, seq)
    ym = _mem_attn(nb, qm_col, mem_kv, nf, zm_col, batch, seq, n_mem)
    return _nsa_out(yo, ym, w_out.astype(BF16), x, final_g)


def kernel(x, mem, hawk_norm, hawk_w_in, hawk_conv_w, hawk_conv_b, hawk_gate_a_w, hawk_gate_a_b,
           hawk_gate_x_w, hawk_gate_x_b, hawk_lambda, hawk_mem_norm, hawk_w_mem_kv, hawk_w_out,
           nsa_norm, nsa_w_in, nsa_pe_k, nsa_pe_v, nsa_phi_k_w1, nsa_phi_k_w2, nsa_phi_v_w1,
           nsa_phi_v_w2, nsa_mem_norm, nsa_w_mem_kv, nsa_w_out, final_norm):
    batch, seq, d = x.shape
    assert hawk_norm.shape[0] == 1 and nsa_norm.shape[0] == 1, "one layer of each kind"
    assert seq % (ATTN_BLOCK * DIL_GROUPS[-1][1]) == 0
    x2 = x.reshape(batch * seq, d)
    mem2 = mem.reshape(batch * mem.shape[1], d)
    x2 = _hawk_layer(x2, mem2, batch, seq, hawk_norm[0], hawk_w_in[0], hawk_conv_w[0], hawk_conv_b[0],
                     hawk_gate_a_w[0], hawk_gate_a_b[0].reshape(-1), hawk_gate_x_w[0],
                     hawk_gate_x_b[0].reshape(-1), hawk_lambda[0], hawk_mem_norm[0], hawk_w_mem_kv[0],
                     hawk_w_out[0])
    out = _nsa_layer(x2, mem2, batch, seq, nsa_norm[0], nsa_w_in[0], nsa_pe_k[0], nsa_pe_v[0],
                     nsa_phi_k_w1[0], nsa_phi_k_w2[0], nsa_phi_v_w1[0], nsa_phi_v_w2[0],
                     nsa_mem_norm[0], nsa_w_mem_kv[0], nsa_w_out[0], final_norm)
    return out.reshape(batch, seq, d)
```

```python
import functools

import numpy as np
import jax
import jax.numpy as jnp
from jax import lax
from jax.experimental import pallas as pl
from jax.experimental.pallas import tpu as pltpu

F32 = jnp.float32
BF16 = jnp.bfloat16

NORM_EPS = 1e-6
NEG_INF = -1e30
LANES = 128
SUBLANES = 8
ATTN_BLOCK = 128
VMEM_LIMIT = 56 * 1024 * 1024

LRU_WIDTH = 1024
LRU_BLOCKS = 16
LRU_BLOCK_DIM = LRU_WIDTH // LRU_BLOCKS
LRU_PACK = 256
CONV_WIDTH = 4
LRU_C = 8.0

DIL_GROUPS = ((128, 1), (512, 4), (2048, 16))
DIL_HEADS = 4
DIL_HEAD_DIM = 128
DIL_WIDTH = DIL_HEADS * DIL_HEAD_DIM
DIL_QKV = len(DIL_GROUPS) * DIL_WIDTH
LSE_LANES = LANES // DIL_HEADS

MEM_HEADS = 4
MEM_HEAD_DIM = 64
MEM_WIDTH = MEM_HEADS * MEM_HEAD_DIM

NSA_HEADS = 16
NSA_KV_GROUPS = 2
NSA_R = NSA_HEADS // NSA_KV_GROUPS
NSA_HEAD_DIM = 64
NSA_WIDTH = NSA_HEADS * NSA_HEAD_DIM
NSA_KV = NSA_KV_GROUPS * NSA_HEAD_DIM
CMP_BLOCK = 32
CMP_STRIDE = 16
SLC_BLOCK = 64
SLC_TOP_N = 8
WIN_SIZE = 512
PHI_HIDDEN = 256
SEL_FORCE = 1e6
CMP_PER_SLC = SLC_BLOCK // CMP_STRIDE


def _alibi_slopes(n):
    return [float(v) for v in np.exp2(-8.0 * np.arange(1, n + 1) / n).astype(np.float32)]


def _params(*semantics):
    return pltpu.CompilerParams(dimension_semantics=semantics, vmem_limit_bytes=VMEM_LIMIT)


def _silu(z):
    return z * jax.nn.sigmoid(z)


def _dot_t(a, b):
    return lax.dot_general(a, b, (((1,), (1,)), ((), ())), preferred_element_type=F32)


def _rms_norm_rows(x, g):
    ms = jnp.mean(x * x, axis=-1, keepdims=True)
    return (x * lax.rsqrt(ms + NORM_EPS) * g).astype(BF16)


def _norm_matmul_kernel(*refs, dil):
    if dil == 1:
        x_ref, g_ref, w_ref, *o_refs = refs
        xn = _rms_norm_rows(x_ref[...], g_ref[...])
        col = 0
        for o_ref in o_refs:
            width = o_ref.shape[1]
            o_ref[...] = jnp.dot(xn, w_ref[:, col:col + width], preferred_element_type=F32).astype(o_ref.dtype)
            col += width
        return
    *x_refs, g_ref, w_ref, o_ref, xn_ref = refs
    per = x_refs[0].shape[0] // dil
    inv_k = 1.0 / (len(x_refs) * LANES)
    for c in range(dil):
        xs = [x_ref[pl.ds(c, per, stride=dil), :] for x_ref in x_refs]
        ss = xs[0] * xs[0]
        for xj in xs[1:]:
            ss = ss + xj * xj
        r = lax.rsqrt(jnp.sum(ss, axis=-1, keepdims=True) * inv_k + NORM_EPS)
        for j, xj in enumerate(xs):
            cols = slice(j * LANES, (j + 1) * LANES)
            xn_ref[c * per:(c + 1) * per, cols] = (xj * r * g_ref[:, cols]).astype(BF16)
    res = jnp.dot(xn_ref[...], w_ref[...], preferred_element_type=F32).astype(o_ref.dtype)
    for c in range(dil):
        o_ref[c] = res[c * per:(c + 1) * per]


def _norm_matmul(x, g, w, outs, dil=1, tm=512):
    m, k = x.shape
    n = w.shape[1]
    tm = min(tm, m)
    assert m % tm == 0 and k % LANES == 0 and sum(width for width, _ in outs) == n
    resident = [pl.BlockSpec((1, k), lambda i: (0, 0)), pl.BlockSpec((k, n), lambda i: (0, 0))]
    if dil == 1:
        return pl.pallas_call(
            functools.partial(_norm_matmul_kernel, dil=1),
            grid=(m // tm,),
            in_specs=[pl.BlockSpec((tm, k), lambda i: (i, 0))] + resident,
            out_specs=[pl.BlockSpec((tm, width), lambda i: (i, 0)) for width, _ in outs],
            out_shape=[jax.ShapeDtypeStruct((m, width), dtype) for width, dtype in outs],
            compiler_params=_params("arbitrary"),
            name="norm_matmul",
        )(x, g.reshape(1, k), w)
    per = tm // dil
    (_, out_dtype), = outs
    assert tm % dil == 0 and per % 16 == 0
    x_specs = [pl.BlockSpec((tm, LANES), functools.partial(lambda i, j: (i, j), j=j)) for j in range(k // LANES)]
    return pl.pallas_call(
        functools.partial(_norm_matmul_kernel, dil=dil),
        grid=(m // tm,),
        in_specs=x_specs + resident,
        out_specs=pl.BlockSpec((dil, per, n), lambda i: (0, i, 0)),
        out_shape=jax.ShapeDtypeStruct((dil, m // dil, n), out_dtype),
        scratch_shapes=[pltpu.VMEM((tm, k), BF16)],
        compiler_params=_params("arbitrary"),
        name="norm_matmul_dil",
    )(*([x] * len(x_specs)), g.reshape(1, k), w)


def _rglru_kernel(xa_ref, za_ref, cw_ref, cb_ref, wa_ref, ba_ref, wx_ref, bx_ref, lam_ref,
                  o_ref, xpad_ref, h_ref):
    t = pl.program_id(1)
    tt, width = xa_ref.shape
    halo = 8

    @pl.when(t == 0)
    def _():
        xpad_ref[0:halo, :] = jnp.zeros((halo, width), F32)
        h_ref[...] = jnp.zeros_like(h_ref)

    x = xa_ref[...]
    xpad_ref[halo:halo + tt, :] = x
    cw = cw_ref[...]
    y = cw[CONV_WIDTH - 1:CONV_WIDTH] * x
    for k in range(1, CONV_WIDTH):
        y = y + cw[CONV_WIDTH - 1 - k:CONV_WIDTH - k] * xpad_ref[halo - k:halo - k + tt, :]
    y = y + cb_ref[...]
    xpad_ref[0:halo, :] = x[tt - halo:tt, :]

    yb = y.astype(BF16)
    r_parts, i_parts = [], []
    for p in range(width // LRU_PACK):
        ys = yb[:, p * LRU_PACK:(p + 1) * LRU_PACK]
        r_parts.append(jnp.dot(ys, wa_ref[p], preferred_element_type=F32))
        i_parts.append(jnp.dot(ys, wx_ref[p], preferred_element_type=F32))
    r = jax.nn.sigmoid(jnp.concatenate(r_parts, axis=1) + ba_ref[...])
    gi = jax.nn.sigmoid(jnp.concatenate(i_parts, axis=1) + bx_ref[...])

    nl = -lam_ref[...]
    softplus = jnp.maximum(nl, 0.0) + jnp.log1p(jnp.exp(-jnp.abs(nl)))
    log_a = (-LRU_C) * r * softplus
    a = jnp.exp(log_a)
    z2 = 2.0 * log_a
    u = a * a
    one_minus = jnp.where(u == 1.0, -z2, jnp.where(u < 0.5, 1.0 - u, (1.0 - u) * z2 / jnp.log(u)))
    mult = jnp.sqrt(one_minus)
    b = y * gi * mult
    first = (lax.broadcasted_iota(jnp.int32, (SUBLANES, width), 0) == 0) & (t == 0)
    b = jnp.concatenate([jnp.where(first, (y * gi)[0:SUBLANES], b[0:SUBLANES]), b[SUBLANES:]], axis=0)

    groups = tt // SUBLANES
    a3 = a.reshape(groups, SUBLANES, width)
    b3 = b.reshape(groups, SUBLANES, width)
    sub = lax.broadcasted_iota(jnp.int32, (groups, SUBLANES, width), 1)
    k = 1
    while k < SUBLANES:
        keep = sub >= k
        a_sh = jnp.where(keep, pltpu.roll(a3, k, axis=1), 1.0)
        b_sh = jnp.where(keep, pltpu.roll(b3, k, axis=1), 0.0)
        b3 = a3 * b_sh + b3
        a3 = a3 * a_sh
        k *= 2
    carry = jnp.broadcast_to(h_ref[...], (SUBLANES, width))
    hs = []
    for gidx in range(groups):
        hg = a3[gidx] * carry + b3[gidx]
        hs.append(hg)
        carry = jnp.broadcast_to(hg[SUBLANES - 1:SUBLANES], (SUBLANES, width))
    h = jnp.concatenate(hs, axis=0)
    h_ref[...] = carry[0:1]
    o_ref[...] = (h * _silu(za_ref[...].astype(F32))).astype(o_ref.dtype)


def _rglru(xa, za_src, za_col, conv_w, conv_b, wa, ba, wx, bx, lam, batch, seq, tt=256):
    width = LRU_WIDTH
    nt = seq // tt
    packs = width // LRU_PACK
    vec = pl.BlockSpec((1, width), lambda b, t: (0, 0))
    gate_w = pl.BlockSpec((packs, LRU_PACK, LRU_PACK), lambda b, t: (0, 0, 0))
    return pl.pallas_call(
        _rglru_kernel,
        grid=(batch, nt),
        in_specs=[pl.BlockSpec((tt, width), lambda b, t: (b * nt + t, 0)),
                  pl.BlockSpec((tt, width), lambda b, t: (b * nt + t, za_col)),
                  pl.BlockSpec((CONV_WIDTH, width), lambda b, t: (0, 0)),
                  vec, gate_w, vec, gate_w, vec, vec],
        out_specs=pl.BlockSpec((tt, width), lambda b, t: (b * nt + t, 0)),
        out_shape=jax.ShapeDtypeStruct((batch * seq, width), BF16),
        scratch_shapes=[pltpu.VMEM((tt + 8, width), F32), pltpu.VMEM((1, width), F32)],
        compiler_params=_params("arbitrary", "arbitrary"),
        name="rglru",
    )(xa, za_src, conv_w, conv_b.reshape(1, width), wa, ba.reshape(1, width), wx, bx.reshape(1, width),
      lam.reshape(1, width))


def _pack_block_diag(w):
    per = LRU_PACK // LRU_BLOCK_DIM
    w = w.reshape(LRU_BLOCKS // per, per, LRU_BLOCK_DIM, LRU_BLOCK_DIM)
    eye = jnp.eye(per, dtype=w.dtype)
    packed = w[:, :, :, None, :] * eye[None, :, None, :, None]
    return packed.reshape(LRU_BLOCKS // per, LRU_PACK, LRU_PACK).astype(BF16)


def _dil_attn_kernel(*refs, slopes, pos_scale, max_dist, has_halo, dil, n_cls, n_blk):
    if has_halo:
        q_ref, kh_ref, k_ref, vh_ref, v_ref = refs[:5]
        out_refs = refs[5:]
    else:
        q_ref, k_ref, v_ref = refs[:3]
        out_refs = refs[3:]
    n_out = DIL_HEADS + 1
    dst_refs = out_refs[:n_out]
    stage_refs = out_refs[n_out:] if dil > 1 else dst_refs
    first_super = pl.program_id(1) == 0
    cls0 = pl.program_id(2) * n_cls
    blk = ATTN_BLOCK
    scale = DIL_HEAD_DIM ** -0.5

    def band(width, halo_live):
        row = lax.broadcasted_iota(jnp.int32, (blk, width), 0)
        col = lax.broadcasted_iota(jnp.int32, (blk, width), 1)
        dist = (width - blk) + row - col
        valid = (dist >= 0) & (dist <= max_dist)
        if halo_live is not None:
            valid = valid & ((col >= blk) | halo_live)
        return valid, (dist * pos_scale).astype(F32)

    def scores(cc, jb):
        cur = slice(jb * blk, (jb + 1) * blk)
        if jb > 0:
            valid, distf = band(2 * blk, None)
        elif has_halo:
            valid, distf = band(2 * blk, jnp.logical_not(first_super))
        else:
            valid, distf = band(blk, None)
        out = []
        for h in range(DIL_HEADS):
            hs = slice(h * DIL_HEAD_DIM, (h + 1) * DIL_HEAD_DIM)
            q = q_ref[cc, cur, hs]
            if jb > 0:
                k = k_ref[cc, (jb - 1) * blk:(jb + 1) * blk, hs]
                v = v_ref[cc, (jb - 1) * blk:(jb + 1) * blk, hs]
            elif has_halo:
                k = jnp.concatenate([kh_ref[cc, :, hs], k_ref[cc, cur, hs]], axis=0)
                v = jnp.concatenate([vh_ref[cc, :, hs], v_ref[cc, cur, hs]], axis=0)
            else:
                k, v = k_ref[cc, cur, hs], v_ref[cc, cur, hs]
            s = _dot_t(q, k) * scale - slopes[h] * distf
            out.append((jnp.where(valid, s, NEG_INF), v))
        return out

    def finish(cc, jb, pairs):
        where = (cls0 + cc, slice(jb * blk, (jb + 1) * blk)) if dil > 1 else (slice(jb * blk, (jb + 1) * blk),)
        lses = []
        for h, (s, v) in enumerate(pairs):
            m = jnp.max(s, axis=-1, keepdims=True)
            e = jnp.exp(s - m)
            den = jnp.sum(e, axis=-1, keepdims=True)
            stage_refs[h][where] = jnp.dot(e.astype(BF16), v, preferred_element_type=F32) / den
            lses.append(jnp.broadcast_to(m + jnp.log(den), (blk, LSE_LANES)))
        stage_refs[DIL_HEADS][where] = jnp.concatenate(lses, axis=1)

    pending = None
    for cc in range(n_cls):
        for jb in range(n_blk):
            pairs = scores(cc, jb)
            if pending is not None:
                finish(*pending)
            pending = (cc, jb, pairs)
    finish(*pending)

    if dil > 1:
        @pl.when(pl.program_id(2) == pl.num_programs(2) - 1)
        def _():
            for stage, dst in zip(stage_refs, dst_refs):
                dst[...] = jnp.swapaxes(stage[...], 0, 1).reshape(dst.shape)


def _dil_attn(qkv, col0, gi, batch, seq, work=4):
    window, dil = DIL_GROUPS[gi]
    sub = seq // dil
    nb = sub // ATTN_BLOCK
    n_blk = min(work, nb)
    n_cls = min(work // n_blk, dil)
    n_super = nb // n_blk
    has_halo = n_super > 1
    span = n_blk * ATTN_BLOCK
    slopes = _alibi_slopes(len(DIL_GROUPS) * DIL_HEADS)[gi * DIL_HEADS:(gi + 1) * DIL_HEADS]
    cur = lambda col: pl.BlockSpec((n_cls, span, DIL_WIDTH), lambda b, i, c: (c, b * n_super + i, col0 + col))
    halo = lambda col: pl.BlockSpec(
        (n_cls, ATTN_BLOCK, DIL_WIDTH),
        lambda b, i, c: (c, jnp.maximum((b * n_super + i) * n_blk - 1, 0), col0 + col))
    if has_halo:
        in_specs = [cur(0), halo(1), cur(1), halo(2), cur(2)]
    else:
        in_specs = [cur(0), cur(1), cur(2)]
    n_out = DIL_HEADS + 1
    *o, lse = pl.pallas_call(
        functools.partial(_dil_attn_kernel, slopes=slopes, pos_scale=dil, max_dist=window // dil,
                          has_halo=has_halo, dil=dil, n_cls=n_cls, n_blk=n_blk),
        grid=(batch, n_super, dil // n_cls),
        in_specs=in_specs,
        out_specs=[pl.BlockSpec((span * dil, LANES), lambda b, i, c: (b * n_super + i, 0))] * n_out,
        out_shape=[jax.ShapeDtypeStruct((batch * seq, LANES), F32)] * n_out,
        scratch_shapes=[pltpu.VMEM((dil, span, LANES), F32)] * (n_out if dil > 1 else 0),
        compiler_params=_params("arbitrary", "arbitrary", "arbitrary"),
        name=f"dil_attn_d{dil}",
    )(*([qkv] * len(in_specs)))
    return o, lse


def _mem_attn_kernel(q_ref, k_ref, v_ref, z_ref, o_ref):
    scale = MEM_HEAD_DIM ** -0.5
    heads = [slice(h * MEM_HEAD_DIM, (h + 1) * MEM_HEAD_DIM) for h in range(MEM_HEADS)]
    scores = [_dot_t(q_ref[:, hs], k_ref[:, hs]) * scale for hs in heads]
    outs = []
    for hs, s in zip(heads, scores):
        m = jnp.max(s, axis=-1, keepdims=True)
        e = jnp.exp(s - m)
        den = jnp.sum(e, axis=-1, keepdims=True)
        outs.append(jnp.dot(e.astype(BF16), v_ref[:, hs], preferred_element_type=F32) / den)
    o_ref[...] = (jnp.concatenate(outs, axis=1) * _silu(z_ref[...].astype(F32))).astype(o_ref.dtype)


def _mem_attn(qsrc, q_col, kv, zsrc, z_col, batch, seq, n_mem, tq=512):
    nq = seq // tq
    return pl.pallas_call(
        _mem_attn_kernel,
        grid=(batch, nq),
        in_specs=[pl.BlockSpec((tq, MEM_WIDTH), lambda b, i: (b * nq + i, q_col)),
                  pl.BlockSpec((n_mem, MEM_WIDTH), lambda b, i: (b, 0)),
                  pl.BlockSpec((n_mem, MEM_WIDTH), lambda b, i: (b, 1)),
                  pl.BlockSpec((tq, MEM_WIDTH), lambda b, i: (b * nq + i, z_col))],
        out_specs=pl.BlockSpec((tq, MEM_WIDTH), lambda b, i: (b * nq + i, 0)),
        out_shape=jax.ShapeDtypeStruct((batch * seq, MEM_WIDTH), BF16),
        compiler_params=_params("arbitrary", "arbitrary"),
        name="mem_attn",
    )(qsrc, kv, kv, zsrc)


def _hawk_out_kernel(*refs):
    n_groups = len(DIL_GROUPS)
    ya_ref = refs[0]
    o_refs = refs[1:1 + n_groups * DIL_HEADS]
    l_refs = refs[1 + n_groups * DIL_HEADS:1 + n_groups * (DIL_HEADS + 1)]
    zb_ref, ym_ref, w_ref, x_ref, out_ref = refs[1 + n_groups * (DIL_HEADS + 1):]
    parts = []
    for h in range(DIL_HEADS):
        ls = [l[:, h * LSE_LANES:h * LSE_LANES + 1] for l in l_refs]
        m = functools.reduce(jnp.maximum, ls)
        ws = [jnp.exp(l - m) for l in ls]
        num = sum(w * o_refs[gi * DIL_HEADS + h][...] for gi, w in enumerate(ws))
        parts.append(num / sum(ws))
    yb = (jnp.concatenate(parts, axis=1) * _silu(zb_ref[...].astype(F32))).astype(BF16)
    a_end = LRU_WIDTH
    b_end = a_end + DIL_WIDTH
    y = jnp.dot(ya_ref[...], w_ref[0:a_end, :], preferred_element_type=F32)
    y = y + jnp.dot(yb, w_ref[a_end:b_end, :], preferred_element_type=F32)
    y = y + jnp.dot(ym_ref[...], w_ref[b_end:b_end + MEM_WIDTH, :], preferred_element_type=F32)
    out_ref[...] = x_ref[...] + y


def _hawk_out(ya, os_, ls_, zb_src, zb_col, ym, w, x, tm=512):
    m, d = x.shape
    row = lambda width, col=0: pl.BlockSpec((tm, width), lambda i: (i, col))
    heads = [o for group in os_ for o in group]
    return pl.pallas_call(
        _hawk_out_kernel,
        grid=(m // tm,),
        in_specs=[row(LRU_WIDTH)] + [row(DIL_HEAD_DIM)] * len(heads) + [row(LANES)] * len(ls_)
                 + [row(DIL_WIDTH, zb_col), row(MEM_WIDTH),
                    pl.BlockSpec(w.shape, lambda i: (0, 0)), row(d)],
        out_specs=row(d),
        out_shape=jax.ShapeDtypeStruct((m, d), F32),
        compiler_params=_params("arbitrary"),
        name="hawk_out",
    )(ya, *heads, *ls_, zb_src, ym, w, x)


def _compress_kernel(k_ref, v_ref, pe_ref, w1_ref, w2k_ref, w2vt_ref, ko_ref, vto_ref):
    n_blk = k_ref.shape[0] // CMP_STRIDE

    def hidden(which, src_ref):
        x = jnp.concatenate([src_ref[pl.ds(p, n_blk, stride=CMP_STRIDE), :] for p in range(CMP_STRIDE)],
                            axis=1).astype(BF16)
        first = jnp.dot(x, w1_ref[which, 0], preferred_element_type=F32)
        second = jnp.dot(x, w1_ref[which, 1], preferred_element_type=F32)
        pe = (jnp.dot(pe_ref[which, 0], w1_ref[which, 0], preferred_element_type=F32)
              + jnp.dot(pe_ref[which, 1], w1_ref[which, 1], preferred_element_type=F32))
        return _silu(first + pltpu.roll(second, n_blk - 1, axis=0) + pe[0:1, :]).astype(BF16)

    act_k, act_v = hidden(0, k_ref), hidden(1, v_ref)
    part = lambda a, g: a[:, g * PHI_HIDDEN:(g + 1) * PHI_HIDDEN]
    ks = [jnp.dot(part(act_k, g), w2k_ref[...], preferred_element_type=F32) for g in range(NSA_KV_GROUPS)]
    vts = [_dot_t(w2vt_ref[...], part(act_v, g)) for g in range(NSA_KV_GROUPS)]
    ko_ref[...] = jnp.concatenate(ks, axis=1).astype(ko_ref.dtype)
    vto_ref[...] = jnp.concatenate(vts, axis=0).astype(vto_ref.dtype)


def _compress(src, k_col, v_col, pe_k, pe_v, k_w1, k_w2, v_w1, v_w2, batch, seq):
    half = CMP_BLOCK // 2
    assert half == CMP_STRIDE and NSA_KV == LANES
    n_blk = seq // CMP_STRIDE
    hd = NSA_HEAD_DIM
    w1 = jnp.stack([k_w1, v_w1]).reshape(2, 2, half, 1, hd, 1, PHI_HIDDEN)
    eye = jnp.eye(NSA_KV_GROUPS, dtype=w1.dtype)[None, None, None, :, None, :, None]
    w1e = (w1 * eye).reshape(2, 2, half * NSA_KV, NSA_KV_GROUPS * PHI_HIDDEN).astype(BF16)
    pe = jnp.stack([pe_k, pe_v]).reshape(2, 2, half, 1, hd)
    pe = jnp.broadcast_to(pe, (2, 2, half, NSA_KV_GROUPS, hd)).reshape(2, 2, 1, half * NSA_KV)
    pe = jnp.broadcast_to(pe, (2, 2, SUBLANES, half * NSA_KV)).astype(BF16)
    w2k = k_w2.astype(BF16)
    w2vt = v_w2.T.astype(BF16)
    whole = lambda a: pl.BlockSpec(a.shape, lambda b: (0,) * a.ndim)
    return pl.pallas_call(
        _compress_kernel,
        grid=(batch,),
        in_specs=[pl.BlockSpec((seq, LANES), lambda b: (b, k_col)),
                  pl.BlockSpec((seq, LANES), lambda b: (b, v_col)),
                  whole(pe), whole(w1e), whole(w2k), whole(w2vt)],
        out_specs=[pl.BlockSpec((None, n_blk, NSA_KV), lambda b: (b, 0, 0)),
                   pl.BlockSpec((None, NSA_KV, n_blk), lambda b: (b, 0, 0))],
        out_shape=[jax.ShapeDtypeStruct((batch, n_blk, NSA_KV), BF16),
                   jax.ShapeDtypeStruct((batch, NSA_KV, n_blk), BF16)],
        compiler_params=_params("arbitrary"),
        name="compress",
    )(src, src, pe, w1e, w2k, w2vt)


KEY_CHUNK = 256
NSA_TQ = 256
SLOPE_PIECES = 3
N_FEATS = 2 * SLOPE_PIECES
WIN_KEY_COLS = 2 * NSA_HEAD_DIM
SLC_KEY_COLS = WIN_KEY_COLS + LANES


def _slope_pieces(slope):
    rest = np.float32(slope)
    pieces = []
    for _ in range(SLOPE_PIECES):
        p = np.float32(np.asarray(rest).astype(BF16))
        pieces.append(float(p))
        rest = np.float32(rest - p)
    return pieces


def _lane_table(lane, values):
    out = jnp.zeros(lane.shape, F32)
    for idx, v in enumerate(values):
        out = jnp.where(lane == idx, v, out)
    return out


def _key_feats(pos_hi, pos_lo, lane):
    return jnp.where(lane < SLOPE_PIECES, pos_hi, jnp.where(lane < N_FEATS, pos_lo, 0)).astype(F32)


def _tile_heads(x):
    return jnp.concatenate([x] * NSA_R, axis=1)


def _nsa_kernel(q_ref, kc_ref, vct_ref, ksrc_ref, vsrc_ref, kwsrc_ref, vwsrc_ref, feat_ref, hot_ref,
                gl_ref, z_ref, o_ref, ks_ref, vst_ref, kw_ref, vwt_ref, s_ref, acc_ref):
    i = pl.program_id(1)
    tq = q_ref.shape[0]
    hd = NSA_HEAD_DIM
    n_cmp = kc_ref.shape[0]

    @pl.when(i == 0)
    def _():
        for g in range(NSA_KV_GROUPS):
            gs = slice(g * hd, (g + 1) * hd)
            for dst, src, cols in ((ks_ref, ksrc_ref, SLC_KEY_COLS), (kw_ref, kwsrc_ref, WIN_KEY_COLS)):
                dst[:, g * cols:g * cols + hd] = src[:, gs]
                dst[:, g * cols + hd:g * cols + 2 * hd] = feat_ref[...]
            ks_ref[:, g * SLC_KEY_COLS + 2 * hd:(g + 1) * SLC_KEY_COLS] = hot_ref[...]
        for c in range(vst_ref.shape[0]):
            rows = slice(c * KEY_CHUNK, (c + 1) * KEY_CHUNK)
            vst_ref[c] = vsrc_ref[rows, :].astype(F32).T.astype(BF16)
            vwt_ref[c] = vwsrc_ref[rows, :].astype(F32).T.astype(BF16)

    slopes_all = _alibi_slopes(NSA_HEADS)
    gates_t = jax.nn.sigmoid(gl_ref[...]).T
    feat_lane = lax.broadcasted_iota(jnp.int32, (tq, hd), 1)
    key_row = lax.broadcasted_iota(jnp.int32, (KEY_CHUNK, tq), 0)
    t_pos = i * tq + lax.broadcasted_iota(jnp.int32, (KEY_CHUNK, tq), 1)
    ones_rows = jnp.ones((16, KEY_CHUNK), BF16)
    win_lo = jnp.maximum(i * tq - (WIN_SIZE - 1), 0) // KEY_CHUNK
    chunks_hi = (i * tq + tq - 1) // KEY_CHUNK + 1

    for g in range(NSA_KV_GROUPS):
        slopes = slopes_all[g * NSA_R:(g + 1) * NSA_R]
        gs = slice(g * hd, (g + 1) * hd)
        q_parts = []
        for r in range(NSA_R):
            qr = q_ref[:, (g * NSA_R + r) * hd:(g * NSA_R + r + 1) * hd].astype(F32) * (hd ** -0.5)
            feats = _lane_table(feat_lane, _slope_pieces(slopes[r]) * 2)
            q_parts.append(jnp.concatenate([qr, feats], axis=1).astype(BF16))
        q_aug = jnp.concatenate(q_parts, axis=0)

        n_row = lax.broadcasted_iota(jnp.int32, (n_cmp, tq), 0)
        t_cmp = i * tq + lax.broadcasted_iota(jnp.int32, (n_cmp, tq), 1)
        visible = t_cmp >= n_row * CMP_STRIDE + (CMP_BLOCK - 1)
        cfeat_row = lax.broadcasted_iota(jnp.int32, (n_cmp, hd), 0)
        cfeat_lane = lax.broadcasted_iota(jnp.int32, (n_cmp, hd), 1)
        kc_feats = _key_feats(cfeat_row * CMP_STRIDE, 0, cfeat_lane)
        kc_aug = jnp.concatenate([kc_ref[:, gs], kc_feats.astype(BF16)], axis=1)
        s = _dot_t(kc_aug, q_aug) + _tile_heads(jnp.where(visible, 0.0, NEG_INF))
        m = jnp.max(s, axis=0, keepdims=True)
        e = jnp.exp(s - m)
        t_one = i * tq + lax.broadcasted_iota(jnp.int32, (1, NSA_R * tq), 1) % tq
        any_visible = t_one >= (CMP_BLOCK - 1)
        p = jnp.where(any_visible, e / jnp.sum(e, axis=0, keepdims=True), 0.0)
        o_cmp = jnp.dot(vct_ref[gs, :], p.astype(BF16), preferred_element_type=F32)
        p_sum = p[:, 0:tq]
        for r in range(1, NSA_R):
            p_sum = p_sum + p[:, r * tq:(r + 1) * tq]

        imp = p_sum + pltpu.roll(p_sum, 1, axis=0)
        for k in range(1, CMP_PER_SLC):
            imp = imp + pltpu.roll(p_sum, n_cmp - k, axis=0)
        blk_j = n_row // CMP_PER_SLC
        cur = t_cmp // SLC_BLOCK
        forced = (blk_j == 0) | (blk_j == cur) | (blk_j == cur - 1)
        imp = jnp.where(forced, SEL_FORCE, jnp.where(blk_j > cur, -SEL_FORCE, imp))
        dead = -3.0e38
        v_imp = jnp.where(n_row % CMP_PER_SLC == 0, imp, dead)
        n_row_f = n_row.astype(F32)
        sel = jnp.zeros((n_cmp, tq), F32)
        for _ in range(SLC_TOP_N):
            mx = jnp.max(v_imp, axis=0, keepdims=True)
            first = jnp.min(jnp.where(v_imp == mx, n_row_f, 2.0 * n_cmp), axis=0, keepdims=True)
            pick = n_row_f == first
            sel = jnp.where(pick, 1.0, sel)
            v_imp = jnp.where(pick, dead, v_imp)

        sel_bias = jnp.where(sel > 0.5, 0.0, NEG_INF).T.astype(BF16)
        q_slc = jnp.concatenate([q_aug, jnp.concatenate([sel_bias] * NSA_R, axis=0)], axis=1)

        def attend(q_br, k_ref, k_cols, vt_ref, lo, hi, masked_from, mask_fn):
            def scores(c, m_run, masked):
                start = pl.multiple_of(c * KEY_CHUNK, KEY_CHUNK)
                sc = _dot_t(k_ref[pl.ds(start, KEY_CHUNK), k_cols], q_br)
                if masked:
                    ok = mask_fn(t_pos - (start + key_row))
                    sc = sc + _tile_heads(jnp.where(ok, 0.0, NEG_INF))
                s_ref[c] = sc
                return jnp.maximum(m_run, jnp.max(sc, axis=0, keepdims=True))

            m_row = jnp.full((1, NSA_R * tq), NEG_INF, F32)
            m_row = lax.fori_loop(lo, masked_from, functools.partial(scores, masked=False), m_row)
            m_row = lax.fori_loop(masked_from, hi, functools.partial(scores, masked=True), m_row)
            acc_ref[...] = jnp.zeros(acc_ref.shape, F32)

            def weighted(c, carry):
                e = jnp.exp(s_ref[c] - m_row).astype(BF16)
                v_ext = jnp.concatenate([vt_ref[c, gs, :], ones_rows], axis=0)
                acc_ref[...] += jnp.dot(v_ext, e, preferred_element_type=F32)
                return carry

            lax.fori_loop(lo, hi, weighted, 0)
            return acc_ref[0:hd, :] / acc_ref[hd:hd + 1, :]

        slc_cols = slice(g * SLC_KEY_COLS, (g + 1) * SLC_KEY_COLS)
        win_cols = slice(g * WIN_KEY_COLS, (g + 1) * WIN_KEY_COLS)
        o_slc = attend(q_slc, ks_ref, slc_cols, vst_ref, 0, chunks_hi, (i * tq) // KEY_CHUNK, lambda dist: dist >= 0)
        o_win = attend(q_aug, kw_ref, win_cols, vwt_ref, win_lo, chunks_hi, win_lo,
                       lambda dist: (dist >= 0) & (dist <= WIN_SIZE - 1))

        def gate(kind):
            base = g * NSA_R * 3 + kind
            return jnp.concatenate([gates_t[base + 3 * r:base + 3 * r + 1, :] for r in range(NSA_R)], axis=1)

        o = gate(0) * o_cmp + gate(1) * o_slc + gate(2) * o_win
        pairs = []
        for r in range(0, NSA_R, 2):
            two = jnp.concatenate([o[:, r * tq:(r + 1) * tq], o[:, (r + 1) * tq:(r + 2) * tq]], axis=0)
            pairs.append(two.T)
        cs = slice(g * NSA_R * hd, (g + 1) * NSA_R * hd)
        o_ref[:, cs] = (jnp.concatenate(pairs, axis=1) * _silu(z_ref[:, cs].astype(F32))).astype(o_ref.dtype)


def _key_pos_feats(seq):
    pos = np.arange(seq)
    feats = np.zeros((seq, NSA_HEAD_DIM), np.float32)
    feats[:, 0:SLOPE_PIECES] = ((pos // SLC_BLOCK) * SLC_BLOCK)[:, None]
    feats[:, SLOPE_PIECES:N_FEATS] = (pos % SLC_BLOCK)[:, None]
    onehot = (np.arange(LANES)[None, :] == (pos // SLC_BLOCK * CMP_PER_SLC)[:, None]).astype(np.float32)
    return jnp.asarray(feats, BF16), jnp.asarray(onehot, BF16)


def _nsa_attn(nb, kv_col0, z_col, k_cmp, v_cmp_t, nf, gl_col, batch, seq):
    tq = NSA_TQ
    nq = seq // tq
    feats, onehot = _key_pos_feats(seq)
    kv_blk = kv_col0 // NSA_KV
    seq_cols = lambda col: pl.BlockSpec((seq, NSA_KV), lambda b, i: (b, kv_blk + col))
    const = lambda a: pl.BlockSpec(a.shape, lambda b, i: (0,) * a.ndim)
    per_batch = lambda a: pl.BlockSpec((None,) + a.shape[1:], lambda b, i: (b,) + (0,) * (a.ndim - 1))
    return pl.pallas_call(
        _nsa_kernel,
        grid=(batch, nq),
        in_specs=[pl.BlockSpec((tq, NSA_WIDTH), lambda b, i: (b * nq + i, 0)),
                  per_batch(k_cmp), per_batch(v_cmp_t),
                  seq_cols(0), seq_cols(1), seq_cols(2), seq_cols(3), const(feats), const(onehot),
                  pl.BlockSpec((tq, LANES), lambda b, i: (b * nq + i, gl_col)),
                  pl.BlockSpec((tq, NSA_WIDTH), lambda b, i: (b * nq + i, z_col))],
        out_specs=pl.BlockSpec((tq, NSA_WIDTH), lambda b, i: (b * nq + i, 0)),
        out_shape=jax.ShapeDtypeStruct((batch * seq, NSA_WIDTH), BF16),
        scratch_shapes=[pltpu.VMEM((seq, NSA_KV_GROUPS * SLC_KEY_COLS), BF16),
                        pltpu.VMEM((seq // KEY_CHUNK, NSA_KV, KEY_CHUNK), BF16),
                        pltpu.VMEM((seq, NSA_KV_GROUPS * WIN_KEY_COLS), BF16),
                        pltpu.VMEM((seq // KEY_CHUNK, NSA_KV, KEY_CHUNK), BF16),
                        pltpu.VMEM((seq // KEY_CHUNK, KEY_CHUNK, NSA_R * tq), F32),
                        pltpu.VMEM((NSA_HEAD_DIM + 16, NSA_R * tq), F32)],
        compiler_params=_params("arbitrary", "arbitrary"),
        name="nsa_attn",
    )(nb, k_cmp, v_cmp_t, nb, nb, nb, nb, feats, onehot, nf, nb)


def _nsa_out_kernel(yo_ref, ym_ref, w_ref, x_ref, g_ref, out_ref):
    y = jnp.dot(yo_ref[...], w_ref[0:NSA_WIDTH, :], preferred_element_type=F32)
    y = y + jnp.dot(ym_ref[...], w_ref[NSA_WIDTH:NSA_WIDTH + MEM_WIDTH, :], preferred_element_type=F32)
    x = x_ref[...] + y
    ms = jnp.mean(x * x, axis=-1, keepdims=True)
    out_ref[...] = x * lax.rsqrt(ms + NORM_EPS) * g_ref[...]


def _nsa_out(yo, ym, w, x, g, tm=512):
    m, d = x.shape
    row = lambda width: pl.BlockSpec((tm, width), lambda i: (i, 0))
    return pl.pallas_call(
        _nsa_out_kernel,
        grid=(m // tm,),
        in_specs=[row(NSA_WIDTH), row(MEM_WIDTH), pl.BlockSpec(w.shape, lambda i: (0, 0)), row(d),
                  pl.BlockSpec((1, d), lambda i: (0, 0))],
        out_specs=row(d),
        out_shape=jax.ShapeDtypeStruct((m, d), F32),
        compiler_params=_params("arbitrary"),
        name="nsa_out",
    )(yo, ym, w, x, g.reshape(1, d))


def _cols(w, *ranges):
    return jnp.concatenate([w[:, a:b] for a, b in ranges], axis=1).astype(BF16)


def _hawk_layer(x, mem, batch, seq, norm_g, w_in, conv_w, conv_b, ga_w, ga_b, gx_w, gx_b, lam,
                mem_norm_g, w_mem_kv, w_out):
    xa0, za0 = 0, LRU_WIDTH
    q0 = 2 * LRU_WIDTH
    k0, v0 = q0 + DIL_QKV, q0 + 2 * DIL_QKV
    zb0 = q0 + 3 * DIL_QKV
    qm0 = zb0 + DIL_WIDTH
    zm0 = qm0 + MEM_WIDTH

    def qkv_cols(gi):
        return [(base + gi * DIL_WIDTH, base + (gi + 1) * DIL_WIDTH) for base in (q0, k0, v0)]

    w_nat = _cols(w_in, (xa0, za0), (za0, q0), *qkv_cols(0), (zb0, qm0), (qm0, zm0), (zm0, zm0 + MEM_WIDTH))
    nat_width = w_nat.shape[1] - LRU_WIDTH
    xa, hb = _norm_matmul(x, norm_g, w_nat, [(LRU_WIDTH, F32), (nat_width, BF16)])
    za_col = 0
    qkv0_col = LRU_WIDTH // DIL_WIDTH
    zb_col = (LRU_WIDTH + 3 * DIL_WIDTH) // DIL_WIDTH
    qm_col = (LRU_WIDTH + 4 * DIL_WIDTH) // MEM_WIDTH
    zm_col = qm_col + 1
    qkv = [(hb[None], qkv0_col)]
    for gi in range(1, len(DIL_GROUPS)):
        qkv.append((_norm_matmul(x, norm_g, _cols(w_in, *qkv_cols(gi)), [(3 * DIL_WIDTH, BF16)],
                                 dil=DIL_GROUPS[gi][1]), 0))
    n_mem = mem.shape[0] // batch
    mem_kv, = _norm_matmul(mem, mem_norm_g, w_mem_kv.astype(BF16), [(2 * MEM_WIDTH, BF16)])

    ya = _rglru(xa, hb, za_col, conv_w, conv_b, _pack_block_diag(ga_w), ga_b, _pack_block_diag(gx_w), gx_b,
                lam, batch, seq)
    attn = [_dil_attn(arr, col0, gi, batch, seq) for gi, (arr, col0) in enumerate(qkv)]
    ym = _mem_attn(hb, qm_col, mem_kv, hb, zm_col, batch, seq, n_mem)
    return _hawk_out(ya, [o for o, _ in attn], [l for _, l in attn], hb, zb_col, ym,
                     w_out.astype(BF16), x)


def _nsa_layer(x, mem, batch, seq, norm_g, w_in, pe_k, pe_v, phik_w1, phik_w2, phiv_w1, phiv_w2,
               mem_norm_g, w_mem_kv, w_out, final_g):
    kv0 = NSA_WIDTH
    gl0 = kv0 + 6 * NSA_KV
    z0 = gl0 + 3 * NSA_HEADS
    qm0 = z0 + NSA_WIDTH
    zm0 = qm0 + MEM_WIDTH
    gl_w = jnp.pad(w_in[:, gl0:z0], ((0, 0), (0, LANES - 3 * NSA_HEADS)))
    w_all = jnp.concatenate([gl_w, w_in[:, kv0:kv0 + 2 * NSA_KV], w_in[:, 0:kv0], w_in[:, z0:qm0],
                             w_in[:, kv0 + 2 * NSA_KV:gl0], w_in[:, qm0:zm0 + MEM_WIDTH]], axis=1).astype(BF16)
    f32_width = LANES + 2 * NSA_KV
    nf, nb = _norm_matmul(x, norm_g, w_all, [(f32_width, F32), (w_all.shape[1] - f32_width, BF16)])
    gl_col, kc_col, vc_col = 0, 1, 2
    z_col = 1
    kv_col0 = 2 * NSA_WIDTH
    qm_col = (kv_col0 + 4 * NSA_KV) // MEM_WIDTH
    zm_col = qm_col + 1
    n_mem = mem.shape[0] // batch
    mem_kv, = _norm_matmul(mem, mem_norm_g, w_mem_kv.astype(BF16), [(2 * MEM_WIDTH, BF16)])

    k_cmp, v_cmp_t = _compress(nf, kc_col, vc_col, pe_k, pe_v, phik_w1, phik_w2, phiv_w1, phiv_w2,
                               batch, seq)
    yo = _nsa_attn(nb, kv_col0, z_col, k_cmp, v_cmp_t, nf, gl_col, batch, seq)
    ym = _mem_attn(nb, qm_col, mem_kv, nb, zm_col, batch, seq, n_mem)
    return _nsa_out(yo, ym, w_out.astype(BF16), x, final_g)


def kernel(x, mem, hawk_norm, hawk_w_in, hawk_conv_w, hawk_conv_b, hawk_gate_a_w, hawk_gate_a_b,
           hawk_gate_x_w, hawk_gate_x_b, hawk_lambda, hawk_mem_norm, hawk_w_mem_kv, hawk_w_out,
           nsa_norm, nsa_w_in, nsa_pe_k, nsa_pe_v, nsa_phi_k_w1, nsa_phi_k_w2, nsa_phi_v_w1,
           nsa_phi_v_w2, nsa_mem_norm, nsa_w_mem_kv, nsa_w_out, final_norm):
    batch, seq, d = x.shape
    assert hawk_norm.shape[0] == 1 and nsa_norm.shape[0] == 1, "one layer of each kind"
    assert seq % (ATTN_BLOCK * DIL_GROUPS[-1][1]) == 0
    x2 = x.reshape(batch * seq, d)
    mem2 = mem.reshape(batch * mem.shape[1], d)
    x2 = _hawk_layer(x2, mem2, batch, seq, hawk_norm[0], hawk_w_in[0], hawk_conv_w[0], hawk_conv_b[0],
                     hawk_gate_a_w[0], hawk_gate_a_b[0].reshape(-1), hawk_gate_x_w[0],
                     hawk_gate_x_b[0].reshape(-1), hawk_lambda[0], hawk_mem_norm[0], hawk_w_mem_kv[0],
                     hawk_w_out[0])
    out = _nsa_layer(x2, mem2, batch, seq, nsa_norm[0], nsa_w_in[0], nsa_pe_k[0], nsa_pe_v[0],
                     nsa_phi_k_w1[0], nsa_phi_k_w2[0], nsa_phi_v_w1[0], nsa_phi_v_w2[0],
                     nsa_mem_norm[0], nsa_w_mem_kv[0], nsa_w_out[0], final_norm)
    return out.reshape(batch, seq, d)
```

```python
import functools

import numpy as np
import jax
import jax.numpy as jnp
from jax import lax
from jax.experimental import pallas as pl
from jax.experimental.pallas import tpu as pltpu

F32 = jnp.float32
BF16 = jnp.bfloat16

NORM_EPS = 1e-6
NEG_INF = -1e30
LANES = 128
SUBLANES = 8
ATTN_BLOCK = 128
VMEM_LIMIT = 56 * 1024 * 1024

LRU_WIDTH = 1024
LRU_BLOCKS = 16
LRU_BLOCK_DIM = LRU_WIDTH // LRU_BLOCKS
LRU_PACK = 256
CONV_WIDTH = 4
LRU_C = 8.0

DIL_GROUPS = ((128, 1), (512, 4), (2048, 16))
DIL_HEADS = 4
DIL_HEAD_DIM = 128
DIL_WIDTH = DIL_HEADS * DIL_HEAD_DIM
DIL_QKV = len(DIL_GROUPS) * DIL_WIDTH
LSE_LANES = LANES // DIL_HEADS

MEM_HEADS = 4
MEM_HEAD_DIM = 64
MEM_WIDTH = MEM_HEADS * MEM_HEAD_DIM

NSA_HEADS = 16
NSA_KV_GROUPS = 2
NSA_R = NSA_HEADS // NSA_KV_GROUPS
NSA_HEAD_DIM = 64
NSA_WIDTH = NSA_HEADS * NSA_HEAD_DIM
NSA_KV = NSA_KV_GROUPS * NSA_HEAD_DIM
CMP_BLOCK = 32
CMP_STRIDE = 16
SLC_BLOCK = 64
SLC_TOP_N = 8
WIN_SIZE = 512
PHI_HIDDEN = 256
SEL_FORCE = 1e6
CMP_PER_SLC = SLC_BLOCK // CMP_STRIDE


def _alibi_slopes(n):
    return [float(v) for v in np.exp2(-8.0 * np.arange(1, n + 1) / n).astype(np.float32)]


def _params(*semantics):
    return pltpu.CompilerParams(dimension_semantics=semantics, vmem_limit_bytes=VMEM_LIMIT)


def _silu(z):
    return z * jax.nn.sigmoid(z)


def _dot_t(a, b):
    return lax.dot_general(a, b, (((1,), (1,)), ((), ())), preferred_element_type=F32)


def _rms_norm_rows(x, g):
    ms = jnp.mean(x * x, axis=-1, keepdims=True)
    return (x * lax.rsqrt(ms + NORM_EPS) * g).astype(BF16)


def _norm_matmul_kernel(*refs, dil):
    if dil == 1:
        x_ref, g_ref, w_ref, *o_refs = refs
        xn = _rms_norm_rows(x_ref[...], g_ref[...])
        col = 0
        for o_ref in o_refs:
            width = o_ref.shape[1]
            o_ref[...] = jnp.dot(xn, w_ref[:, col:col + width], preferred_element_type=F32).astype(o_ref.dtype)
            col += width
        return
    *x_refs, g_ref, w_ref, o_ref, xn_ref = refs
    per = x_refs[0].shape[0] // dil
    inv_k = 1.0 / (len(x_refs) * LANES)
    for c in range(dil):
        xs = [x_ref[pl.ds(c, per, stride=dil), :] for x_ref in x_refs]
        ss = xs[0] * xs[0]
        for xj in xs[1:]:
            ss = ss + xj * xj
        r = lax.rsqrt(jnp.sum(ss, axis=-1, keepdims=True) * inv_k + NORM_EPS)
        for j, xj in enumerate(xs):
            cols = slice(j * LANES, (j + 1) * LANES)
            xn_ref[c * per:(c + 1) * per, cols] = (xj * r * g_ref[:, cols]).astype(BF16)
    res = jnp.dot(xn_ref[...], w_ref[...], preferred_element_type=F32).astype(o_ref.dtype)
    for c in range(dil):
        o_ref[c] = res[c * per:(c + 1) * per]


def _norm_matmul(x, g, w, outs, dil=1, tm=512):
    m, k = x.shape
    n = w.shape[1]
    tm = min(tm, m)
    assert m % tm == 0 and k % LANES == 0 and sum(width for width, _ in outs) == n
    resident = [pl.BlockSpec((1, k), lambda i: (0, 0)), pl.BlockSpec((k, n), lambda i: (0, 0))]
    if dil == 1:
        return pl.pallas_call(
            functools.partial(_norm_matmul_kernel, dil=1),
            grid=(m // tm,),
            in_specs=[pl.BlockSpec((tm, k), lambda i: (i, 0))] + resident,
            out_specs=[pl.BlockSpec((tm, width), lambda i: (i, 0)) for width, _ in outs],
            out_shape=[jax.ShapeDtypeStruct((m, width), dtype) for width, dtype in outs],
            compiler_params=_params("arbitrary"),
            name="norm_matmul",
        )(x, g.reshape(1, k), w)
    per = tm // dil
    (_, out_dtype), = outs
    assert tm % dil == 0 and per % 16 == 0
    x_specs = [pl.BlockSpec((tm, LANES), functools.partial(lambda i, j: (i, j), j=j)) for j in range(k // LANES)]
    return pl.pallas_call(
        functools.partial(_norm_matmul_kernel, dil=dil),
        grid=(m // tm,),
        in_specs=x_specs + resident,
        out_specs=pl.BlockSpec((dil, per, n), lambda i: (0, i, 0)),
        out_shape=jax.ShapeDtypeStruct((dil, m // dil, n), out_dtype),
        scratch_shapes=[pltpu.VMEM((tm, k), BF16)],
        compiler_params=_params("arbitrary"),
        name="norm_matmul_dil",
    )(*([x] * len(x_specs)), g.reshape(1, k), w)


def _rglru_kernel(xa_ref, za_ref, cw_ref, cb_ref, wa_ref, ba_ref, wx_ref, bx_ref, lam_ref,
                  o_ref, xpad_ref, h_ref):
    t = pl.program_id(1)
    tt, width = xa_ref.shape
    halo = 8

    @pl.when(t == 0)
    def _():
        xpad_ref[0:halo, :] = jnp.zeros((halo, width), F32)
        h_ref[...] = jnp.zeros_like(h_ref)

    x = xa_ref[...]
    xpad_ref[halo:halo + tt, :] = x
    cw = cw_ref[...]
    y = cw[CONV_WIDTH - 1:CONV_WIDTH] * x
    for k in range(1, CONV_WIDTH):
        y = y + cw[CONV_WIDTH - 1 - k:CONV_WIDTH - k] * xpad_ref[halo - k:halo - k + tt, :]
    y = y + cb_ref[...]
    xpad_ref[0:halo, :] = x[tt - halo:tt, :]

    yb = y.astype(BF16)
    r_parts, i_parts = [], []
    for p in range(width // LRU_PACK):
        ys = yb[:, p * LRU_PACK:(p + 1) * LRU_PACK]
        r_parts.append(jnp.dot(ys, wa_ref[p], preferred_element_type=F32))
        i_parts.append(jnp.dot(ys, wx_ref[p], preferred_element_type=F32))
    r = jax.nn.sigmoid(jnp.concatenate(r_parts, axis=1) + ba_ref[...])
    gi = jax.nn.sigmoid(jnp.concatenate(i_parts, axis=1) + bx_ref[...])

    nl = -lam_ref[...]
    softplus = jnp.maximum(nl, 0.0) + jnp.log1p(jnp.exp(-jnp.abs(nl)))
    log_a = (-LRU_C) * r * softplus
    a = jnp.exp(log_a)
    z2 = 2.0 * log_a
    u = a * a
    one_minus = jnp.where(u == 1.0, -z2, jnp.where(u < 0.5, 1.0 - u, (1.0 - u) * z2 / jnp.log(u)))
    mult = jnp.sqrt(one_minus)
    b = y * gi * mult
    first = (lax.broadcasted_iota(jnp.int32, (SUBLANES, width), 0) == 0) & (t == 0)
    b = jnp.concatenate([jnp.where(first, (y * gi)[0:SUBLANES], b[0:SUBLANES]), b[SUBLANES:]], axis=0)

    groups = tt // SUBLANES
    a3 = a.reshape(groups, SUBLANES, width)
    b3 = b.reshape(groups, SUBLANES, width)
    sub = lax.broadcasted_iota(jnp.int32, (groups, SUBLANES, width), 1)
    k = 1
    while k < SUBLANES:
        keep = sub >= k
        a_sh = jnp.where(keep, pltpu.roll(a3, k, axis=1), 1.0)
        b_sh = jnp.where(keep, pltpu.roll(b3, k, axis=1), 0.0)
        b3 = a3 * b_sh + b3
        a3 = a3 * a_sh
        k *= 2
    carry = jnp.broadcast_to(h_ref[...], (SUBLANES, width))
    hs = []
    for gidx in range(groups):
        hg = a3[gidx] * carry + b3[gidx]
        hs.append(hg)
        carry = jnp.broadcast_to(hg[SUBLANES - 1:SUBLANES], (SUBLANES, width))
    h = jnp.concatenate(hs, axis=0)
    h_ref[...] = carry[0:1]
    o_ref[...] = (h * _silu(za_ref[...].astype(F32))).astype(o_ref.dtype)


def _rglru(xa, za_src, za_col, conv_w, conv_b, wa, ba, wx, bx, lam, batch, seq, tt=256):
    width = LRU_WIDTH
    nt = seq // tt
    packs = width // LRU_PACK
    vec = pl.BlockSpec((1, width), lambda b, t: (0, 0))
    gate_w = pl.BlockSpec((packs, LRU_PACK, LRU_PACK), lambda b, t: (0, 0, 0))
    return pl.pallas_call(
        _rglru_kernel,
        grid=(batch, nt),
        in_specs=[pl.BlockSpec((tt, width), lambda b, t: (b * nt + t, 0)),
                  pl.BlockSpec((tt, width), lambda b, t: (b * nt + t, za_col)),
                  pl.BlockSpec((CONV_WIDTH, width), lambda b, t: (0, 0)),
                  vec, gate_w, vec, gate_w, vec, vec],
        out_specs=pl.BlockSpec((tt, width), lambda b, t: (b * nt + t, 0)),
        out_shape=jax.ShapeDtypeStruct((batch * seq, width), BF16),
        scratch_shapes=[pltpu.VMEM((tt + 8, width), F32), pltpu.VMEM((1, width), F32)],
        compiler_params=_params("arbitrary", "arbitrary"),
        name="rglru",
    )(xa, za_src, conv_w, conv_b.reshape(1, width), wa, ba.reshape(1, width), wx, bx.reshape(1, width),
      lam.reshape(1, width))


def _pack_block_diag(w):
    per = LRU_PACK // LRU_BLOCK_DIM
    w = w.reshape(LRU_BLOCKS // per, per, LRU_BLOCK_DIM, LRU_BLOCK_DIM)
    eye = jnp.eye(per, dtype=w.dtype)
    packed = w[:, :, :, None, :] * eye[None, :, None, :, None]
    return packed.reshape(LRU_BLOCKS // per, LRU_PACK, LRU_PACK).astype(BF16)


def _dil_attn_kernel(*refs, slopes, pos_scale, max_dist, has_halo, dil, n_cls, n_blk):
    if has_halo:
        q_ref, kh_ref, k_ref, vh_ref, v_ref = refs[:5]
        out_refs = refs[5:]
    else:
        q_ref, k_ref, v_ref = refs[:3]
        out_refs = refs[3:]
    n_out = DIL_HEADS + 1
    dst_refs = out_refs[:n_out]
    stage_refs = out_refs[n_out:] if dil > 1 else dst_refs
    first_super = pl.program_id(1) == 0
    cls0 = pl.program_id(2) * n_cls
    blk = ATTN_BLOCK
    scale = DIL_HEAD_DIM ** -0.5

    def band(width, halo_live):
        row = lax.broadcasted_iota(jnp.int32, (blk, width), 0)
        col = lax.broadcasted_iota(jnp.int32, (blk, width), 1)
        dist = (width - blk) + row - col
        valid = (dist >= 0) & (dist <= max_dist)
        if halo_live is not None:
            valid = valid & ((col >= blk) | halo_live)
        return valid, (dist * pos_scale).astype(F32)

    def scores(cc, jb):
        cur = slice(jb * blk, (jb + 1) * blk)
        if jb > 0:
            valid, distf = band(2 * blk, None)
        elif has_halo:
            valid, distf = band(2 * blk, jnp.logical_not(first_super))
        else:
            valid, distf = band(blk, None)
        out = []
        for h in range(DIL_HEADS):
            hs = slice(h * DIL_HEAD_DIM, (h + 1) * DIL_HEAD_DIM)
            q = q_ref[cc, cur, hs]
            if jb > 0:
                k = k_ref[cc, (jb - 1) * blk:(jb + 1) * blk, hs]
                v = v_ref[cc, (jb - 1) * blk:(jb + 1) * blk, hs]
            elif has_halo:
                k = jnp.concatenate([kh_ref[cc, :, hs], k_ref[cc, cur, hs]], axis=0)
                v = jnp.concatenate([vh_ref[cc, :, hs], v_ref[cc, cur, hs]], axis=0)
            else:
                k, v = k_ref[cc, cur, hs], v_ref[cc, cur, hs]
            s = _dot_t(q, k) * scale - slopes[h] * distf
            out.append((jnp.where(valid, s, NEG_INF), v))
        return out

    def finish(cc, jb, pairs):
        where = (cls0 + cc, slice(jb * blk, (jb + 1) * blk)) if dil > 1 else (slice(jb * blk, (jb + 1) * blk),)
        lses = []
        for h, (s, v) in enumerate(pairs):
            m = jnp.max(s, axis=-1, keepdims=True)
            e = jnp.exp(s - m)
            den = jnp.sum(e, axis=-1, keepdims=True)
            stage_refs[h][where] = jnp.dot(e.astype(BF16), v, preferred_element_type=F32) / den
            lses.append(jnp.broadcast_to(m + jnp.log(den), (blk, LSE_LANES)))
        stage_refs[DIL_HEADS][where] = jnp.concatenate(lses, axis=1)

    pending = None
    for cc in range(n_cls):
        for jb in range(n_blk):
            pairs = scores(cc, jb)
            if pending is not None:
                finish(*pending)
            pending = (cc, jb, pairs)
    finish(*pending)

    if dil > 1:
        @pl.when(pl.program_id(2) == pl.num_programs(2) - 1)
        def _():
            for stage, dst in zip(stage_refs, dst_refs):
                dst[...] = jnp.swapaxes(stage[...], 0, 1).reshape(dst.shape)


def _dil_attn(qkv, col0, gi, batch, seq, work=4):
    window, dil = DIL_GROUPS[gi]
    sub = seq // dil
    nb = sub // ATTN_BLOCK
    n_blk = min(work, nb)
    n_cls = min(work // n_blk, dil)
    n_super = nb // n_blk
    has_halo = n_super > 1
    span = n_blk * ATTN_BLOCK
    slopes = _alibi_slopes(len(DIL_GROUPS) * DIL_HEADS)[gi * DIL_HEADS:(gi + 1) * DIL_HEADS]
    cur = lambda col: pl.BlockSpec((n_cls, span, DIL_WIDTH), lambda b, i, c: (c, b * n_super + i, col0 + col))
    halo = lambda col: pl.BlockSpec(
        (n_cls, ATTN_BLOCK, DIL_WIDTH),
        lambda b, i, c: (c, jnp.maximum((b * n_super + i) * n_blk - 1, 0), col0 + col))
    if has_halo:
        in_specs = [cur(0), halo(1), cur(1), halo(2), cur(2)]
    else:
        in_specs = [cur(0), cur(1), cur(2)]
    n_out = DIL_HEADS + 1
    *o, lse = pl.pallas_call(
        functools.partial(_dil_attn_kernel, slopes=slopes, pos_scale=dil, max_dist=window // dil,
                          has_halo=has_halo, dil=dil, n_cls=n_cls, n_blk=n_blk),
        grid=(batch, n_super, dil // n_cls),
        in_specs=in_specs,
        out_specs=[pl.BlockSpec((span * dil, LANES), lambda b, i, c: (b * n_super + i, 0))] * n_out,
        out_shape=[jax.ShapeDtypeStruct((batch * seq, LANES), F32)] * n_out,
        scratch_shapes=[pltpu.VMEM((dil, span, LANES), F32)] * (n_out if dil > 1 else 0),
        compiler_params=_params("arbitrary", "arbitrary", "arbitrary"),
        name=f"dil_attn_d{dil}",
    )(*([qkv] * len(in_specs)))
    return o, lse


def _mem_attn_kernel(q_ref, k_ref, v_ref, z_ref, o_ref):
    scale = MEM_HEAD_DIM ** -0.5
    heads = [slice(h * MEM_HEAD_DIM, (h + 1) * MEM_HEAD_DIM) for h in range(MEM_HEADS)]
    scores = [_dot_t(q_ref[:, hs], k_ref[:, hs]) * scale for hs in heads]
    outs = []
    for hs, s in zip(heads, scores):
        m = jnp.max(s, axis=-1, keepdims=True)
        e = jnp.exp(s - m)
        den = jnp.sum(e, axis=-1, keepdims=True)
        outs.append(jnp.dot(e.astype(BF16), v_ref[:, hs], preferred_element_type=F32) / den)
    o_ref[...] = (jnp.concatenate(outs, axis=1) * _silu(z_ref[...].astype(F32))).astype(o_ref.dtype)


def _mem_attn(qsrc, q_col, kv, zsrc, z_col, batch, seq, n_mem, tq=512):
    nq = seq // tq
    return pl.pallas_call(
        _mem_attn_kernel,
        grid=(batch, nq),
        in_specs=[pl.BlockSpec((tq, MEM_WIDTH), lambda b, i: (b * nq + i, q_col)),
                  pl.BlockSpec((n_mem, MEM_WIDTH), lambda b, i: (b, 0)),
                  pl.BlockSpec((n_mem, MEM_WIDTH), lambda b, i: (b, 1)),
                  pl.BlockSpec((tq, MEM_WIDTH), lambda b, i: (b * nq + i, z_col))],
        out_specs=pl.BlockSpec((tq, MEM_WIDTH), lambda b, i: (b * nq + i, 0)),
        out_shape=jax.ShapeDtypeStruct((batch * seq, MEM_WIDTH), BF16),
        compiler_params=_params("arbitrary", "arbitrary"),
        name="mem_attn",
    )(qsrc, kv, kv, zsrc)


def _hawk_out_kernel(*refs):
    n_groups = len(DIL_GROUPS)
    ya_ref = refs[0]
    o_refs = refs[1:1 + n_groups * DIL_HEADS]
    l_refs = refs[1 + n_groups * DIL_HEADS:1 + n_groups * (DIL_HEADS + 1)]
    zb_ref, ym_ref, w_ref, x_ref, out_ref = refs[1 + n_groups * (DIL_HEADS + 1):]
    parts = []
    for h in range(DIL_HEADS):
        ls = [l[:, h * LSE_LANES:h * LSE_LANES + 1] for l in l_refs]
        m = functools.reduce(jnp.maximum, ls)
        ws = [jnp.exp(l - m) for l in ls]
        num = sum(w * o_refs[gi * DIL_HEADS + h][...] for gi, w in enumerate(ws))
        parts.append(num / sum(ws))
    yb = (jnp.concatenate(parts, axis=1) * _silu(zb_ref[...].astype(F32))).astype(BF16)
    a_end = LRU_WIDTH
    b_end = a_end + DIL_WIDTH
    y = jnp.dot(ya_ref[...], w_ref[0:a_end, :], preferred_element_type=F32)
    y = y + jnp.dot(yb, w_ref[a_end:b_end, :], preferred_element_type=F32)
    y = y + jnp.dot(ym_ref[...], w_ref[b_end:b_end + MEM_WIDTH, :], preferred_element_type=F32)
    out_ref[...] = x_ref[...] + y


def _hawk_out(ya, os_, ls_, zb_src, zb_col, ym, w, x, tm=512):
    m, d = x.shape
    row = lambda width, col=0: pl.BlockSpec((tm, width), lambda i: (i, col))
    heads = [o for group in os_ for o in group]
    return pl.pallas_call(
        _hawk_out_kernel,
        grid=(m // tm,),
        in_specs=[row(LRU_WIDTH)] + [row(DIL_HEAD_DIM)] * len(heads) + [row(LANES)] * len(ls_)
                 + [row(DIL_WIDTH, zb_col), row(MEM_WIDTH),
                    pl.BlockSpec(w.shape, lambda i: (0, 0)), row(d)],
        out_specs=row(d),
        out_shape=jax.ShapeDtypeStruct((m, d), F32),
        compiler_params=_params("arbitrary"),
        name="hawk_out",
    )(ya, *heads, *ls_, zb_src, ym, w, x)


def _compress_kernel(k_ref, v_ref, pe_ref, w1_ref, w2k_ref, w2vt_ref, ko_ref, vto_ref):
    n_blk = k_ref.shape[0] // CMP_STRIDE

    def hidden(which, src_ref):
        x = jnp.concatenate([src_ref[pl.ds(p, n_blk, stride=CMP_STRIDE), :] for p in range(CMP_STRIDE)],
                            axis=1).astype(BF16)
        first = jnp.dot(x, w1_ref[which, 0], preferred_element_type=F32)
        second = jnp.dot(x, w1_ref[which, 1], preferred_element_type=F32)
        pe = (jnp.dot(pe_ref[which, 0], w1_ref[which, 0], preferred_element_type=F32)
              + jnp.dot(pe_ref[which, 1], w1_ref[which, 1], preferred_element_type=F32))
        return _silu(first + pltpu.roll(second, n_blk - 1, axis=0) + pe[0:1, :]).astype(BF16)

    act_k, act_v = hidden(0, k_ref), hidden(1, v_ref)
    part = lambda a, g: a[:, g * PHI_HIDDEN:(g + 1) * PHI_HIDDEN]
    ks = [jnp.dot(part(act_k, g), w2k_ref[...], preferred_element_type=F32) for g in range(NSA_KV_GROUPS)]
    vts = [_dot_t(w2vt_ref[...], part(act_v, g)) for g in range(NSA_KV_GROUPS)]
    ko_ref[...] = jnp.concatenate(ks, axis=1).astype(ko_ref.dtype)
    vto_ref[...] = jnp.concatenate(vts, axis=0).astype(vto_ref.dtype)


def _compress(src, k_col, v_col, pe_k, pe_v, k_w1, k_w2, v_w1, v_w2, batch, seq):
    half = CMP_BLOCK // 2
    assert half == CMP_STRIDE and NSA_KV == LANES
    n_blk = seq // CMP_STRIDE
    hd = NSA_HEAD_DIM
    w1 = jnp.stack([k_w1, v_w1]).reshape(2, 2, half, 1, hd, 1, PHI_HIDDEN)
    eye = jnp.eye(NSA_KV_GROUPS, dtype=w1.dtype)[None, None, None, :, None, :, None]
    w1e = (w1 * eye).reshape(2, 2, half * NSA_KV, NSA_KV_GROUPS * PHI_HIDDEN).astype(BF16)
    pe = jnp.stack([pe_k, pe_v]).reshape(2, 2, half, 1, hd)
    pe = jnp.broadcast_to(pe, (2, 2, half, NSA_KV_GROUPS, hd)).reshape(2, 2, 1, half * NSA_KV)
    pe = jnp.broadcast_to(pe, (2, 2, SUBLANES, half * NSA_KV)).astype(BF16)
    w2k = k_w2.astype(BF16)
    w2vt = v_w2.T.astype(BF16)
    whole = lambda a: pl.BlockSpec(a.shape, lambda b: (0,) * a.ndim)
    return pl.pallas_call(
        _compress_kernel,
        grid=(batch,),
        in_specs=[pl.BlockSpec((seq, LANES), lambda b: (b, k_col)),
                  pl.BlockSpec((seq, LANES), lambda b: (b, v_col)),
                  whole(pe), whole(w1e), whole(w2k), whole(w2vt)],
        out_specs=[pl.BlockSpec((None, n_blk, NSA_KV), lambda b: (b, 0, 0)),
                   pl.BlockSpec((None, NSA_KV, n_blk), lambda b: (b, 0, 0))],
        out_shape=[jax.ShapeDtypeStruct((batch, n_blk, NSA_KV), BF16),
                   jax.ShapeDtypeStruct((batch, NSA_KV, n_blk), BF16)],
        compiler_params=_params("arbitrary"),
        name="compress",
    )(src, src, pe, w1e, w2k, w2vt)


KEY_CHUNK = 256
NSA_TQ = 256
SLOPE_PIECES = 3
N_FEATS = 2 * SLOPE_PIECES
WIN_KEY_COLS = 2 * NSA_HEAD_DIM
SLC_KEY_COLS = WIN_KEY_COLS + LANES


def _slope_pieces(slope):
    rest = np.float32(slope)
    pieces = []
    for _ in range(SLOPE_PIECES):
        p = np.float32(np.asarray(rest).astype(BF16))
        pieces.append(float(p))
        rest = np.float32(rest - p)
    return pieces


def _lane_table(lane, values):
    out = jnp.zeros(lane.shape, F32)
    for idx, v in enumerate(values):
        out = jnp.where(lane == idx, v, out)
    return out


def _key_feats(pos_hi, pos_lo, lane):
    return jnp.where(lane < SLOPE_PIECES, pos_hi, jnp.where(lane < N_FEATS, pos_lo, 0)).astype(F32)


def _tile_heads(x):
    return jnp.concatenate([x] * NSA_R, axis=1)


def _nsa_kernel(q_ref, kc_ref, vct_ref, ksrc_ref, vsrc_ref, kwsrc_ref, vwsrc_ref, feat_ref, hot_ref,
                gl_ref, z_ref, o_ref, ks_ref, vst_ref, kw_ref, vwt_ref, s_ref, acc_ref):
    i = pl.program_id(1)
    tq = q_ref.shape[0]
    hd = NSA_HEAD_DIM
    n_cmp = kc_ref.shape[0]

    @pl.when(i == 0)
    def _():
        for g in range(NSA_KV_GROUPS):
            gs = slice(g * hd, (g + 1) * hd)
            for dst, src, cols in ((ks_ref, ksrc_ref, SLC_KEY_COLS), (kw_ref, kwsrc_ref, WIN_KEY_COLS)):
                dst[:, g * cols:g * cols + hd] = src[:, gs]
                dst[:, g * cols + hd:g * cols + 2 * hd] = feat_ref[...]
            ks_ref[:, g * SLC_KEY_COLS + 2 * hd:(g + 1) * SLC_KEY_COLS] = hot_ref[...]
        for c in range(vst_ref.shape[0]):
            rows = slice(c * KEY_CHUNK, (c + 1) * KEY_CHUNK)
            vst_ref[c] = vsrc_ref[rows, :].astype(F32).T.astype(BF16)
            vwt_ref[c] = vwsrc_ref[rows, :].astype(F32).T.astype(BF16)

    slopes_all = _alibi_slopes(NSA_HEADS)
    gates_t = jax.nn.sigmoid(gl_ref[...]).T
    feat_lane = lax.broadcasted_iota(jnp.int32, (tq, hd), 1)
    key_row = lax.broadcasted_iota(jnp.int32, (KEY_CHUNK, tq), 0)
    t_pos = i * tq + lax.broadcasted_iota(jnp.int32, (KEY_CHUNK, tq), 1)
    ones_rows = jnp.ones((16, KEY_CHUNK), BF16)
    win_lo = jnp.maximum(i * tq - (WIN_SIZE - 1), 0) // KEY_CHUNK
    chunks_hi = (i * tq + tq - 1) // KEY_CHUNK + 1

    groups = range(NSA_KV_GROUPS)
    q_win, q_slc, o_cmp = [], [], []
    for g in groups:
        slopes = slopes_all[g * NSA_R:(g + 1) * NSA_R]
        gs = slice(g * hd, (g + 1) * hd)
        q_parts = []
        for r in range(NSA_R):
            qr = q_ref[:, (g * NSA_R + r) * hd:(g * NSA_R + r + 1) * hd].astype(F32) * (hd ** -0.5)
            feats = _lane_table(feat_lane, _slope_pieces(slopes[r]) * 2)
            q_parts.append(jnp.concatenate([qr, feats], axis=1).astype(BF16))
        q_aug = jnp.concatenate(q_parts, axis=0)

        n_row = lax.broadcasted_iota(jnp.int32, (n_cmp, tq), 0)
        t_cmp = i * tq + lax.broadcasted_iota(jnp.int32, (n_cmp, tq), 1)
        visible = t_cmp >= n_row * CMP_STRIDE + (CMP_BLOCK - 1)
        cfeat_row = lax.broadcasted_iota(jnp.int32, (n_cmp, hd), 0)
        cfeat_lane = lax.broadcasted_iota(jnp.int32, (n_cmp, hd), 1)
        kc_feats = _key_feats(cfeat_row * CMP_STRIDE, 0, cfeat_lane)
        kc_aug = jnp.concatenate([kc_ref[:, gs], kc_feats.astype(BF16)], axis=1)
        s = _dot_t(kc_aug, q_aug) + _tile_heads(jnp.where(visible, 0.0, NEG_INF))
        m = jnp.max(s, axis=0, keepdims=True)
        e = jnp.exp(s - m)
        t_one = i * tq + lax.broadcasted_iota(jnp.int32, (1, NSA_R * tq), 1) % tq
        any_visible = t_one >= (CMP_BLOCK - 1)
        p = jnp.where(any_visible, e / jnp.sum(e, axis=0, keepdims=True), 0.0)
        o_cmp.append(jnp.dot(vct_ref[gs, :], p.astype(BF16), preferred_element_type=F32))
        p_sum = p[:, 0:tq]
        for r in range(1, NSA_R):
            p_sum = p_sum + p[:, r * tq:(r + 1) * tq]

        imp = p_sum + pltpu.roll(p_sum, 1, axis=0)
        for k in range(1, CMP_PER_SLC):
            imp = imp + pltpu.roll(p_sum, n_cmp - k, axis=0)
        blk_j = n_row // CMP_PER_SLC
        cur = t_cmp // SLC_BLOCK
        forced = (blk_j == 0) | (blk_j == cur) | (blk_j == cur - 1)
        imp = jnp.where(forced, SEL_FORCE, jnp.where(blk_j > cur, -SEL_FORCE, imp))
        dead = -3.0e38
        v_imp = jnp.where(n_row % CMP_PER_SLC == 0, imp, dead)
        n_row_f = n_row.astype(F32)
        sel = jnp.zeros((n_cmp, tq), F32)
        for _ in range(SLC_TOP_N):
            mx = jnp.max(v_imp, axis=0, keepdims=True)
            first = jnp.min(jnp.where(v_imp == mx, n_row_f, 2.0 * n_cmp), axis=0, keepdims=True)
            pick = n_row_f == first
            sel = jnp.where(pick, 1.0, sel)
            v_imp = jnp.where(pick, dead, v_imp)

        sel_bias = jnp.where(sel > 0.5, 0.0, NEG_INF).T.astype(BF16)
        q_win.append(q_aug)
        q_slc.append(jnp.concatenate([q_aug, jnp.concatenate([sel_bias] * NSA_R, axis=0)], axis=1))

    def attend(q_brs, k_ref, key_cols, vt_ref, lo, hi, masked_from, mask_fn):
        def scores(c, m_run, masked):
            start = pl.multiple_of(c * KEY_CHUNK, KEY_CHUNK)
            scs = [_dot_t(k_ref[pl.ds(start, KEY_CHUNK), g * key_cols:(g + 1) * key_cols], q_brs[g])
                   for g in groups]
            if masked:
                bias = _tile_heads(jnp.where(mask_fn(t_pos - (start + key_row)), 0.0, NEG_INF))
                scs = [sc + bias for sc in scs]
            for g in groups:
                s_ref[g, c] = scs[g]
            return tuple(jnp.maximum(m_run[g], jnp.max(scs[g], axis=0, keepdims=True)) for g in groups)

        m_rows = tuple(jnp.full((1, NSA_R * tq), NEG_INF, F32) for _ in groups)
        m_rows = lax.fori_loop(lo, masked_from, functools.partial(scores, masked=False), m_rows)
        m_rows = lax.fori_loop(masked_from, hi, functools.partial(scores, masked=True), m_rows)
        acc_ref[...] = jnp.zeros(acc_ref.shape, F32)

        def weighted(c, carry):
            for g in groups:
                e = jnp.exp(s_ref[g, c] - m_rows[g]).astype(BF16)
                v_ext = jnp.concatenate([vt_ref[c, g * hd:(g + 1) * hd, :], ones_rows], axis=0)
                acc_ref[g] += jnp.dot(v_ext, e, preferred_element_type=F32)
            return carry

        lax.fori_loop(lo, hi, weighted, 0)
        return [acc_ref[g, 0:hd, :] / acc_ref[g, hd:hd + 1, :] for g in groups]

    o_slc = attend(q_slc, ks_ref, SLC_KEY_COLS, vst_ref, 0, chunks_hi, (i * tq) // KEY_CHUNK,
                   lambda dist: dist >= 0)
    o_win = attend(q_win, kw_ref, WIN_KEY_COLS, vwt_ref, win_lo, chunks_hi, win_lo,
                   lambda dist: (dist >= 0) & (dist <= WIN_SIZE - 1))

    for g in groups:
        def gate(kind):
            base = g * NSA_R * 3 + kind
            return jnp.concatenate([gates_t[base + 3 * r:base + 3 * r + 1, :] for r in range(NSA_R)], axis=1)

        o = gate(0) * o_cmp[g] + gate(1) * o_slc[g] + gate(2) * o_win[g]
        pairs = []
        for r in range(0, NSA_R, 2):
            two = jnp.concatenate([o[:, r * tq:(r + 1) * tq], o[:, (r + 1) * tq:(r + 2) * tq]], axis=0)
            pairs.append(two.T)
        cs = slice(g * NSA_R * hd, (g + 1) * NSA_R * hd)
        o_ref[:, cs] = (jnp.concatenate(pairs, axis=1) * _silu(z_ref[:, cs].astype(F32))).astype(o_ref.dtype)


def _key_pos_feats(seq):
    pos = np.arange(seq)
    feats = np.zeros((seq, NSA_HEAD_DIM), np.float32)
    feats[:, 0:SLOPE_PIECES] = ((pos // SLC_BLOCK) * SLC_BLOCK)[:, None]
    feats[:, SLOPE_PIECES:N_FEATS] = (pos % SLC_BLOCK)[:, None]
    onehot = (np.arange(LANES)[None, :] == (pos // SLC_BLOCK * CMP_PER_SLC)[:, None]).astype(np.float32)
    return jnp.asarray(feats, BF16), jnp.asarray(onehot, BF16)


def _nsa_attn(nb, kv_col0, z_col, k_cmp, v_cmp_t, nf, gl_col, batch, seq):
    tq = NSA_TQ
    nq = seq // tq
    feats, onehot = _key_pos_feats(seq)
    kv_blk = kv_col0 // NSA_KV
    seq_cols = lambda col: pl.BlockSpec((seq, NSA_KV), lambda b, i: (b, kv_blk + col))
    const = lambda a: pl.BlockSpec(a.shape, lambda b, i: (0,) * a.ndim)
    per_batch = lambda a: pl.BlockSpec((None,) + a.shape[1:], lambda b, i: (b,) + (0,) * (a.ndim - 1))
    return pl.pallas_call(
        _nsa_kernel,
        grid=(batch, nq),
        in_specs=[pl.BlockSpec((tq, NSA_WIDTH), lambda b, i: (b * nq + i, 0)),
                  per_batch(k_cmp), per_batch(v_cmp_t),
                  seq_cols(0), seq_cols(1), seq_cols(2), seq_cols(3), const(feats), const(onehot),
                  pl.BlockSpec((tq, LANES), lambda b, i: (b * nq + i, gl_col)),
                  pl.BlockSpec((tq, NSA_WIDTH), lambda b, i: (b * nq + i, z_col))],
        out_specs=pl.BlockSpec((tq, NSA_WIDTH), lambda b, i: (b * nq + i, 0)),
        out_shape=jax.ShapeDtypeStruct((batch * seq, NSA_WIDTH), BF16),
        scratch_shapes=[pltpu.VMEM((seq, NSA_KV_GROUPS * SLC_KEY_COLS), BF16),
                        pltpu.VMEM((seq // KEY_CHUNK, NSA_KV, KEY_CHUNK), BF16),
                        pltpu.VMEM((seq, NSA_KV_GROUPS * WIN_KEY_COLS), BF16),
                        pltpu.VMEM((seq // KEY_CHUNK, NSA_KV, KEY_CHUNK), BF16),
                        pltpu.VMEM((NSA_KV_GROUPS, seq // KEY_CHUNK, KEY_CHUNK, NSA_R * tq), F32),
                        pltpu.VMEM((NSA_KV_GROUPS, NSA_HEAD_DIM + 16, NSA_R * tq), F32)],
        compiler_params=_params("arbitrary", "arbitrary"),
        name="nsa_attn",
    )(nb, k_cmp, v_cmp_t, nb, nb, nb, nb, feats, onehot, nf, nb)


def _nsa_out_kernel(yo_ref, ym_ref, w_ref, x_ref, g_ref, out_ref):
    y = jnp.dot(yo_ref[...], w_ref[0:NSA_WIDTH, :], preferred_element_type=F32)
    y = y + jnp.dot(ym_ref[...], w_ref[NSA_WIDTH:NSA_WIDTH + MEM_WIDTH, :], preferred_element_type=F32)
    x = x_ref[...] + y
    ms = jnp.mean(x * x, axis=-1, keepdims=True)
    out_ref[...] = x * lax.rsqrt(ms + NORM_EPS) * g_ref[...]


def _nsa_out(yo, ym, w, x, g, tm=512):
    m, d = x.shape
    row = lambda width: pl.BlockSpec((tm, width), lambda i: (i, 0))
    return pl.pallas_call(
        _nsa_out_kernel,
        grid=(m // tm,),
        in_specs=[row(NSA_WIDTH), row(MEM_WIDTH), pl.BlockSpec(w.shape, lambda i: (0, 0)), row(d),
                  pl.BlockSpec((1, d), lambda i: (0, 0))],
        out_specs=row(d),
        out_shape=jax.ShapeDtypeStruct((m, d), F32),
        compiler_params=_params("arbitrary"),
        name="nsa_out",
    )(yo, ym, w, x, g.reshape(1, d))


def _cols(w, *ranges):
    return jnp.concatenate([w[:, a:b] for a, b in ranges], axis=1).astype(BF16)


def _hawk_layer(x, mem, batch, seq, norm_g, w_in, conv_w, conv_b, ga_w, ga_b, gx_w, gx_b, lam,
                mem_norm_g, w_mem_kv, w_out):
    xa0, za0 = 0, LRU_WIDTH
    q0 = 2 * LRU_WIDTH
    k0, v0 = q0 + DIL_QKV, q0 + 2 * DIL_QKV
    zb0 = q0 + 3 * DIL_QKV
    qm0 = zb0 + DIL_WIDTH
    zm0 = qm0 + MEM_WIDTH

    def qkv_cols(gi):
        return [(base + gi * DIL_WIDTH, base + (gi + 1) * DIL_WIDTH) for base in (q0, k0, v0)]

    w_nat = _cols(w_in, (xa0, za0), (za0, q0), *qkv_cols(0), (zb0, qm0), (qm0, zm0), (zm0, zm0 + MEM_WIDTH))
    nat_width = w_nat.shape[1] - LRU_WIDTH
    xa, hb = _norm_matmul(x, norm_g, w_nat, [(LRU_WIDTH, F32), (nat_width, BF16)])
    za_col = 0
    qkv0_col = LRU_WIDTH // DIL_WIDTH
    zb_col = (LRU_WIDTH + 3 * DIL_WIDTH) // DIL_WIDTH
    qm_col = (LRU_WIDTH + 4 * DIL_WIDTH) // MEM_WIDTH
    zm_col = qm_col + 1
    qkv = [(hb[None], qkv0_col)]
    for gi in range(1, len(DIL_GROUPS)):
        qkv.append((_norm_matmul(x, norm_g, _cols(w_in, *qkv_cols(gi)), [(3 * DIL_WIDTH, BF16)],
                                 dil=DIL_GROUPS[gi][1]), 0))
    n_mem = mem.shape[0] // batch
    mem_kv, = _norm_matmul(mem, mem_norm_g, w_mem_kv.astype(BF16), [(2 * MEM_WIDTH, BF16)])

    ya = _rglru(xa, hb, za_col, conv_w, conv_b, _pack_block_diag(ga_w), ga_b, _pack_block_diag(gx_w), gx_b,
                lam, batch, seq)
    attn = [_dil_attn(arr, col0, gi, batch, seq) for gi, (arr, col0) in enumerate(qkv)]
    ym = _mem_attn(hb, qm_col, mem_kv, hb, zm_col, batch, seq, n_mem)
    return _hawk_out(ya, [o for o, _ in attn], [l for _, l in attn], hb, zb_col, ym,
                     w_out.astype(BF16), x)


def _nsa_layer(x, mem, batch, seq, norm_g, w_in, pe_k, pe_v, phik_w1, phik_w2, phiv_w1, phiv_w2,
               mem_norm_g, w_mem_kv, w_out, final_g):
    kv0 = NSA_WIDTH
    gl0 = kv0 + 6 * NSA_KV
    z0 = gl0 + 3 * NSA_HEADS
    qm0 = z0 + NSA_WIDTH
    zm0 = qm0 + MEM_WIDTH
    gl_w = jnp.pad(w_in[:, gl0:z0], ((0, 0), (0, LANES - 3 * NSA_HEADS)))
    w_all = jnp.concatenate([gl_w, w_in[:, kv0:kv0 + 2 * NSA_KV], w_in[:, 0:kv0], w_in[:, z0:qm0],
                             w_in[:, kv0 + 2 * NSA_KV:gl0], w_in[:, qm0:zm0 + MEM_WIDTH]], axis=1).astype(BF16)
    f32_width = LANES + 2 * NSA_KV
    nf, nb = _norm_matmul(x, norm_g, w_all, [(f32_width, F32), (w_all.shape[1] - f32_width, BF16)])
    gl_col, kc_col, vc_col = 0, 1, 2
    z_col = 1
    kv_col0 = 2 * NSA_WIDTH
    qm_col = (kv_col0 + 4 * NSA_KV) // MEM_WIDTH
    zm_col = qm_col + 1
    n_mem = mem.shape[0] // batch
    mem_kv, = _norm_matmul(mem, mem_norm_g, w_mem_kv.astype(BF16), [(2 * MEM_WIDTH, BF16)])

    k_cmp, v_cmp_t = _compress(nf, kc_col, vc_col, pe_k, pe_v, phik_w1, phik_w2, phiv_w1, phiv_w2,
                               batch, seq)
    yo = _nsa_attn(nb, kv_col0, z_col, k_cmp, v_cmp_t, nf, gl_col, batch, seq)
    ym = _mem_attn(nb, qm_col, mem_kv, nb, zm_col, batch, seq, n_mem)
    return _nsa_out(yo, ym, w_out.astype(BF16), x, final_g)


def kernel(x, mem, hawk_norm, hawk_w_in, hawk_conv_w, hawk_conv_b, hawk_gate_a_w, hawk_gate_a_b,
           hawk_gate_x_w, hawk_gate_x_b, hawk_lambda, hawk_mem_norm, hawk_w_mem_kv, hawk_w_out,
           nsa_norm, nsa_w_in, nsa_pe_k, nsa_pe_v, nsa_phi_k_w1, nsa_phi_k_w2, nsa_phi_v_w1,
           nsa_phi_v_w2, nsa_mem_norm, nsa_w_mem_kv, nsa_w_out, final_norm):
    batch, seq, d = x.shape
    assert hawk_norm.shape[0] == 1 and nsa_norm.shape[0] == 1, "one layer of each kind"
    assert seq % (ATTN_BLOCK * DIL_GROUPS[-1][1]) == 0
    x2 = x.reshape(batch * seq, d)
    mem2 = mem.reshape(batch * mem.shape[1], d)
    x2 = _hawk_layer(x2, mem2, batch, seq, hawk_norm[0], hawk_w_in[0], hawk_conv_w[0], hawk_conv_b[0],
                     hawk_gate_a_w[0], hawk_gate_a_b[0].reshape(-1), hawk_gate_x_w[0],
                     hawk_gate_x_b[0].reshape(-1), hawk_lambda[0], hawk_mem_norm[0], hawk_w_mem_kv[0],
                     hawk_w_out[0])
    out = _nsa_layer(x2, mem2, batch, seq, nsa_norm[0], nsa_w_in[0], nsa_pe_k[0], nsa_pe_v[0],
                     nsa_phi_k_w1[0], nsa_phi_k_w2[0], nsa_phi_v_w1[0], nsa_phi_v_w2[0],
                     nsa_mem_norm[0], nsa_w_mem_kv[0], nsa_w_out[0], final_norm)
    return out.reshape(batch, seq, d)
```

```python
import functools

import numpy as np
import jax
import jax.numpy as jnp
from jax import lax
from jax.experimental import pallas as pl
from jax.experimental.pallas import tpu as pltpu

F32 = jnp.float32
BF16 = jnp.bfloat16

NORM_EPS = 1e-6
NEG_INF = -1e30
LANES = 128
SUBLANES = 8
ATTN_BLOCK = 128
VMEM_LIMIT = 56 * 1024 * 1024

LRU_WIDTH = 1024
LRU_BLOCKS = 16
LRU_BLOCK_DIM = LRU_WIDTH // LRU_BLOCKS
LRU_PACK = 256
CONV_WIDTH = 4
LRU_C = 8.0

DIL_GROUPS = ((128, 1), (512, 4), (2048, 16))
DIL_HEADS = 4
DIL_HEAD_DIM = 128
DIL_WIDTH = DIL_HEADS * DIL_HEAD_DIM
DIL_QKV = len(DIL_GROUPS) * DIL_WIDTH
LSE_LANES = LANES // DIL_HEADS

MEM_HEADS = 4
MEM_HEAD_DIM = 64
MEM_WIDTH = MEM_HEADS * MEM_HEAD_DIM

NSA_HEADS = 16
NSA_KV_GROUPS = 2
NSA_R = NSA_HEADS // NSA_KV_GROUPS
NSA_HEAD_DIM = 64
NSA_WIDTH = NSA_HEADS * NSA_HEAD_DIM
NSA_KV = NSA_KV_GROUPS * NSA_HEAD_DIM
CMP_BLOCK = 32
CMP_STRIDE = 16
SLC_BLOCK = 64
SLC_TOP_N = 8
WIN_SIZE = 512
PHI_HIDDEN = 256
SEL_FORCE = 1e6
CMP_PER_SLC = SLC_BLOCK // CMP_STRIDE


def _alibi_slopes(n):
    return [float(v) for v in np.exp2(-8.0 * np.arange(1, n + 1) / n).astype(np.float32)]


def _params(*semantics):
    return pltpu.CompilerParams(dimension_semantics=semantics, vmem_limit_bytes=VMEM_LIMIT)


def _silu(z):
    return z * jax.nn.sigmoid(z)


def _dot_t(a, b):
    return lax.dot_general(a, b, (((1,), (1,)), ((), ())), preferred_element_type=F32)


def _rms_norm_rows(x, g):
    ms = jnp.mean(x * x, axis=-1, keepdims=True)
    return (x * lax.rsqrt(ms + NORM_EPS) * g).astype(BF16)


def _norm_matmul_kernel(*refs, dil):
    if dil == 1:
        x_ref, g_ref, w_ref, *o_refs = refs
        xn = _rms_norm_rows(x_ref[...], g_ref[...])
        col = 0
        for o_ref in o_refs:
            width = o_ref.shape[1]
            o_ref[...] = jnp.dot(xn, w_ref[:, col:col + width], preferred_element_type=F32).astype(o_ref.dtype)
            col += width
        return
    if dil % SUBLANES == 0:
        x_ref, g_ref, w_ref, o_ref = refs
        tm, k = x_ref.shape
        x = x_ref[...]
        xn = x * lax.rsqrt(jnp.mean(x * x, axis=-1, keepdims=True) + NORM_EPS) * g_ref[...]
        xn = jnp.swapaxes(xn.reshape(tm // dil, dil, k), 0, 1).reshape(tm, k).astype(BF16)
        res = jnp.dot(xn, w_ref[...], preferred_element_type=F32).astype(o_ref.dtype)
        for c in range(dil):
            o_ref[c] = res[c * (tm // dil):(c + 1) * (tm // dil)]
        return
    *x_refs, g_ref, w_ref, o_ref, xn_ref = refs
    per = x_refs[0].shape[0] // dil
    inv_k = 1.0 / (len(x_refs) * LANES)
    for c in range(dil):
        xs = [x_ref[pl.ds(c, per, stride=dil), :] for x_ref in x_refs]
        ss = xs[0] * xs[0]
        for xj in xs[1:]:
            ss = ss + xj * xj
        r = lax.rsqrt(jnp.sum(ss, axis=-1, keepdims=True) * inv_k + NORM_EPS)
        for j, xj in enumerate(xs):
            cols = slice(j * LANES, (j + 1) * LANES)
            xn_ref[c * per:(c + 1) * per, cols] = (xj * r * g_ref[:, cols]).astype(BF16)
    res = jnp.dot(xn_ref[...], w_ref[...], preferred_element_type=F32).astype(o_ref.dtype)
    for c in range(dil):
        o_ref[c] = res[c * per:(c + 1) * per]


def _norm_matmul(x, g, w, outs, dil=1, tm=512):
    m, k = x.shape
    n = w.shape[1]
    tm = min(tm, m)
    assert m % tm == 0 and k % LANES == 0 and sum(width for width, _ in outs) == n
    resident = [pl.BlockSpec((1, k), lambda i: (0, 0)), pl.BlockSpec((k, n), lambda i: (0, 0))]
    if dil == 1:
        return pl.pallas_call(
            functools.partial(_norm_matmul_kernel, dil=1),
            grid=(m // tm,),
            in_specs=[pl.BlockSpec((tm, k), lambda i: (i, 0))] + resident,
            out_specs=[pl.BlockSpec((tm, width), lambda i: (i, 0)) for width, _ in outs],
            out_shape=[jax.ShapeDtypeStruct((m, width), dtype) for width, dtype in outs],
            compiler_params=_params("arbitrary"),
            name="norm_matmul",
        )(x, g.reshape(1, k), w)
    per = tm // dil
    (_, out_dtype), = outs
    assert tm % dil == 0 and per % 16 == 0
    if dil % SUBLANES == 0:
        x_specs, scratch = [pl.BlockSpec((tm, k), lambda i: (i, 0))], []
    else:
        x_specs = [pl.BlockSpec((tm, LANES), functools.partial(lambda i, j: (i, j), j=j)) for j in range(k // LANES)]
        scratch = [pltpu.VMEM((tm, k), BF16)]
    return pl.pallas_call(
        functools.partial(_norm_matmul_kernel, dil=dil),
        grid=(m // tm,),
        in_specs=x_specs + resident,
        out_specs=pl.BlockSpec((dil, per, n), lambda i: (0, i, 0)),
        out_shape=jax.ShapeDtypeStruct((dil, m // dil, n), out_dtype),
        scratch_shapes=scratch,
        compiler_params=_params("arbitrary"),
        name="norm_matmul_dil",
    )(*([x] * len(x_specs)), g.reshape(1, k), w)


def _rglru_kernel(xa_ref, za_ref, cw_ref, cb_ref, wa_ref, ba_ref, wx_ref, bx_ref, lam_ref,
                  o_ref, xpad_ref, h_ref):
    t = pl.program_id(1)
    tt, width = xa_ref.shape
    halo = 8

    @pl.when(t == 0)
    def _():
        xpad_ref[0:halo, :] = jnp.zeros((halo, width), F32)
        h_ref[...] = jnp.zeros_like(h_ref)

    x = xa_ref[...]
    xpad_ref[halo:halo + tt, :] = x
    cw = cw_ref[...]
    y = cw[CONV_WIDTH - 1:CONV_WIDTH] * x
    for k in range(1, CONV_WIDTH):
        y = y + cw[CONV_WIDTH - 1 - k:CONV_WIDTH - k] * xpad_ref[halo - k:halo - k + tt, :]
    y = y + cb_ref[...]
    xpad_ref[0:halo, :] = x[tt - halo:tt, :]

    yb = y.astype(BF16)
    r_parts, i_parts = [], []
    for p in range(width // LRU_PACK):
        ys = yb[:, p * LRU_PACK:(p + 1) * LRU_PACK]
        r_parts.append(jnp.dot(ys, wa_ref[p], preferred_element_type=F32))
        i_parts.append(jnp.dot(ys, wx_ref[p], preferred_element_type=F32))
    r = jax.nn.sigmoid(jnp.concatenate(r_parts, axis=1) + ba_ref[...])
    gi = jax.nn.sigmoid(jnp.concatenate(i_parts, axis=1) + bx_ref[...])

    nl = -lam_ref[...]
    softplus = jnp.maximum(nl, 0.0) + jnp.log1p(jnp.exp(-jnp.abs(nl)))
    log_a = (-LRU_C) * r * softplus
    a = jnp.exp(log_a)
    z2 = 2.0 * log_a
    u = a * a
    one_minus = jnp.where(u == 1.0, -z2, jnp.where(u < 0.5, 1.0 - u, (1.0 - u) * z2 / jnp.log(u)))
    mult = jnp.sqrt(one_minus)
    b = y * gi * mult
    first = (lax.broadcasted_iota(jnp.int32, (SUBLANES, width), 0) == 0) & (t == 0)
    b = jnp.concatenate([jnp.where(first, (y * gi)[0:SUBLANES], b[0:SUBLANES]), b[SUBLANES:]], axis=0)

    groups = tt // SUBLANES
    a3 = a.reshape(groups, SUBLANES, width)
    b3 = b.reshape(groups, SUBLANES, width)
    sub = lax.broadcasted_iota(jnp.int32, (groups, SUBLANES, width), 1)
    k = 1
    while k < SUBLANES:
        keep = sub >= k
        a_sh = jnp.where(keep, pltpu.roll(a3, k, axis=1), 1.0)
        b_sh = jnp.where(keep, pltpu.roll(b3, k, axis=1), 0.0)
        b3 = a3 * b_sh + b3
        a3 = a3 * a_sh
        k *= 2
    carry = jnp.broadcast_to(h_ref[...], (SUBLANES, width))
    hs = []
    for gidx in range(groups):
        hg = a3[gidx] * carry + b3[gidx]
        hs.append(hg)
        carry = jnp.broadcast_to(hg[SUBLANES - 1:SUBLANES], (SUBLANES, width))
    h = jnp.concatenate(hs, axis=0)
    h_ref[...] = carry[0:1]
    o_ref[...] = (h * _silu(za_ref[...].astype(F32))).astype(o_ref.dtype)


def _rglru(xa, za_src, za_col, conv_w, conv_b, wa, ba, wx, bx, lam, batch, seq, tt=512):
    width = LRU_WIDTH
    nt = seq // tt
    packs = width // LRU_PACK
    vec = pl.BlockSpec((1, width), lambda b, t: (0, 0))
    gate_w = pl.BlockSpec((packs, LRU_PACK, LRU_PACK), lambda b, t: (0, 0, 0))
    return pl.pallas_call(
        _rglru_kernel,
        grid=(batch, nt),
        in_specs=[pl.BlockSpec((tt, width), lambda b, t: (b * nt + t, 0)),
                  pl.BlockSpec((tt, width), lambda b, t: (b * nt + t, za_col)),
                  pl.BlockSpec((CONV_WIDTH, width), lambda b, t: (0, 0)),
                  vec, gate_w, vec, gate_w, vec, vec],
        out_specs=pl.BlockSpec((tt, width), lambda b, t: (b * nt + t, 0)),
        out_shape=jax.ShapeDtypeStruct((batch * seq, width), BF16),
        scratch_shapes=[pltpu.VMEM((tt + 8, width), F32), pltpu.VMEM((1, width), F32)],
        compiler_params=_params("arbitrary", "arbitrary"),
        name="rglru",
    )(xa, za_src, conv_w, conv_b.reshape(1, width), wa, ba.reshape(1, width), wx, bx.reshape(1, width),
      lam.reshape(1, width))


def _pack_block_diag(w):
    per = LRU_PACK // LRU_BLOCK_DIM
    w = w.reshape(LRU_BLOCKS // per, per, LRU_BLOCK_DIM, LRU_BLOCK_DIM)
    eye = jnp.eye(per, dtype=w.dtype)
    packed = w[:, :, :, None, :] * eye[None, :, None, :, None]
    return packed.reshape(LRU_BLOCKS // per, LRU_PACK, LRU_PACK).astype(BF16)


def _dil_attn_kernel(*refs, slopes, pos_scale, max_dist, has_halo, dil, n_cls, n_blk):
    if has_halo:
        q_ref, kh_ref, k_ref, vh_ref, v_ref = refs[:5]
        out_refs = refs[5:]
    else:
        q_ref, k_ref, v_ref = refs[:3]
        out_refs = refs[3:]
    n_out = DIL_HEADS + 1
    dst_refs = out_refs[:n_out]
    stage_refs = out_refs[n_out:] if dil > 1 else dst_refs
    first_super = pl.program_id(1) == 0
    cls0 = pl.program_id(2) * n_cls
    blk = ATTN_BLOCK
    scale = DIL_HEAD_DIM ** -0.5

    def band(width, halo_live):
        row = lax.broadcasted_iota(jnp.int32, (blk, width), 0)
        col = lax.broadcasted_iota(jnp.int32, (blk, width), 1)
        dist = (width - blk) + row - col
        valid = (dist >= 0) & (dist <= max_dist)
        if halo_live is not None:
            valid = valid & ((col >= blk) | halo_live)
        return valid, (dist * pos_scale).astype(F32)

    def scores(cc, jb):
        cur = slice(jb * blk, (jb + 1) * blk)
        if jb > 0:
            valid, distf = band(2 * blk, None)
        elif has_halo:
            valid, distf = band(2 * blk, jnp.logical_not(first_super))
        else:
            valid, distf = band(blk, None)
        out = []
        for h in range(DIL_HEADS):
            hs = slice(h * DIL_HEAD_DIM, (h + 1) * DIL_HEAD_DIM)
            q = q_ref[cc, cur, hs]
            if jb > 0:
                k = k_ref[cc, (jb - 1) * blk:(jb + 1) * blk, hs]
                v = v_ref[cc, (jb - 1) * blk:(jb + 1) * blk, hs]
            elif has_halo:
                k = jnp.concatenate([kh_ref[cc, :, hs], k_ref[cc, cur, hs]], axis=0)
                v = jnp.concatenate([vh_ref[cc, :, hs], v_ref[cc, cur, hs]], axis=0)
            else:
                k, v = k_ref[cc, cur, hs], v_ref[cc, cur, hs]
            s = _dot_t(q, k) * scale - slopes[h] * distf
            out.append((jnp.where(valid, s, NEG_INF), v))
        return out

    def finish(cc, jb, pairs):
        where = (cls0 + cc, slice(jb * blk, (jb + 1) * blk)) if dil > 1 else (slice(jb * blk, (jb + 1) * blk),)
        lses = []
        for h, (s, v) in enumerate(pairs):
            m = jnp.max(s, axis=-1, keepdims=True)
            e = jnp.exp(s - m)
            den = jnp.sum(e, axis=-1, keepdims=True)
            o = jnp.dot(e.astype(BF16), v, preferred_element_type=F32) / den
            stage_refs[h][where] = o.astype(stage_refs[h].dtype)
            lses.append(jnp.broadcast_to(m + jnp.log(den), (blk, LSE_LANES)))
        stage_refs[DIL_HEADS][where] = jnp.concatenate(lses, axis=1)

    pending = None
    for cc in range(n_cls):
        for jb in range(n_blk):
            pairs = scores(cc, jb)
            if pending is not None:
                finish(*pending)
            pending = (cc, jb, pairs)
    finish(*pending)

    if dil > 1:
        @pl.when(pl.program_id(2) == pl.num_programs(2) - 1)
        def _():
            for stage, dst in zip(stage_refs, dst_refs):
                dst[...] = jnp.swapaxes(stage[...], 0, 1).reshape(dst.shape).astype(dst.dtype)


def _dil_attn(qkv, col0, gi, batch, seq, work=4):
    window, dil = DIL_GROUPS[gi]
    sub = seq // dil
    nb = sub // ATTN_BLOCK
    n_blk = min(work, nb)
    n_cls = min(work // n_blk, dil)
    n_super = nb // n_blk
    has_halo = n_super > 1
    span = n_blk * ATTN_BLOCK
    slopes = _alibi_slopes(len(DIL_GROUPS) * DIL_HEADS)[gi * DIL_HEADS:(gi + 1) * DIL_HEADS]
    cur = lambda col: pl.BlockSpec((n_cls, span, DIL_WIDTH), lambda b, i, c: (c, b * n_super + i, col0 + col))
    halo = lambda col: pl.BlockSpec(
        (n_cls, ATTN_BLOCK, DIL_WIDTH),
        lambda b, i, c: (c, jnp.maximum((b * n_super + i) * n_blk - 1, 0), col0 + col))
    if has_halo:
        in_specs = [cur(0), halo(1), cur(1), halo(2), cur(2)]
    else:
        in_specs = [cur(0), cur(1), cur(2)]
    n_out = DIL_HEADS + 1
    *o, lse = pl.pallas_call(
        functools.partial(_dil_attn_kernel, slopes=slopes, pos_scale=dil, max_dist=window // dil,
                          has_halo=has_halo, dil=dil, n_cls=n_cls, n_blk=n_blk),
        grid=(batch, n_super, dil // n_cls),
        in_specs=in_specs,
        out_specs=[pl.BlockSpec((span * dil, LANES), lambda b, i, c: (b * n_super + i, 0))] * n_out,
        out_shape=[jax.ShapeDtypeStruct((batch * seq, LANES), BF16)] * DIL_HEADS
                  + [jax.ShapeDtypeStruct((batch * seq, LANES), F32)],
        scratch_shapes=[pltpu.VMEM((dil, span, LANES), F32)] * (n_out if dil > 1 else 0),
        compiler_params=_params("arbitrary", "arbitrary", "arbitrary"),
        name=f"dil_attn_d{dil}",
    )(*([qkv] * len(in_specs)))
    return o, lse


def _mem_attn_kernel(q_ref, k_ref, v_ref, z_ref, o_ref):
    scale = MEM_HEAD_DIM ** -0.5
    heads = [slice(h * MEM_HEAD_DIM, (h + 1) * MEM_HEAD_DIM) for h in range(MEM_HEADS)]
    scores = [_dot_t(q_ref[:, hs], k_ref[:, hs]) * scale for hs in heads]
    outs = []
    for hs, s in zip(heads, scores):
        m = jnp.max(s, axis=-1, keepdims=True)
        e = jnp.exp(s - m)
        den = jnp.sum(e, axis=-1, keepdims=True)
        outs.append(jnp.dot(e.astype(BF16), v_ref[:, hs], preferred_element_type=F32) / den)
    o_ref[...] = (jnp.concatenate(outs, axis=1) * _silu(z_ref[...].astype(F32))).astype(o_ref.dtype)


def _mem_attn(qsrc, q_col, kv, zsrc, z_col, batch, seq, n_mem, tq=512):
    nq = seq // tq
    return pl.pallas_call(
        _mem_attn_kernel,
        grid=(batch, nq),
        in_specs=[pl.BlockSpec((tq, MEM_WIDTH), lambda b, i: (b * nq + i, q_col)),
                  pl.BlockSpec((n_mem, MEM_WIDTH), lambda b, i: (b, 0)),
                  pl.BlockSpec((n_mem, MEM_WIDTH), lambda b, i: (b, 1)),
                  pl.BlockSpec((tq, MEM_WIDTH), lambda b, i: (b * nq + i, z_col))],
        out_specs=pl.BlockSpec((tq, MEM_WIDTH), lambda b, i: (b * nq + i, 0)),
        out_shape=jax.ShapeDtypeStruct((batch * seq, MEM_WIDTH), BF16),
        compiler_params=_params("arbitrary", "arbitrary"),
        name="mem_attn",
    )(qsrc, kv, kv, zsrc)


def _hawk_out_kernel(*refs):
    n_groups = len(DIL_GROUPS)
    ya_ref = refs[0]
    o_refs = refs[1:1 + n_groups * DIL_HEADS]
    l_refs = refs[1 + n_groups * DIL_HEADS:1 + n_groups * (DIL_HEADS + 1)]
    zb_ref, ym_ref, w_ref, x_ref, out_ref = refs[1 + n_groups * (DIL_HEADS + 1):]
    parts = []
    for h in range(DIL_HEADS):
        ls = [l[:, h * LSE_LANES:h * LSE_LANES + 1] for l in l_refs]
        m = functools.reduce(jnp.maximum, ls)
        ws = [jnp.exp(l - m) for l in ls]
        num = sum(w * o_refs[gi * DIL_HEADS + h][...].astype(F32) for gi, w in enumerate(ws))
        parts.append(num / sum(ws))
    yb = (jnp.concatenate(parts, axis=1) * _silu(zb_ref[...].astype(F32))).astype(BF16)
    a_end = LRU_WIDTH
    b_end = a_end + DIL_WIDTH
    y = jnp.dot(ya_ref[...], w_ref[0:a_end, :], preferred_element_type=F32)
    y = y + jnp.dot(yb, w_ref[a_end:b_end, :], preferred_element_type=F32)
    y = y + jnp.dot(ym_ref[...], w_ref[b_end:b_end + MEM_WIDTH, :], preferred_element_type=F32)
    out_ref[...] = x_ref[...] + y


def _hawk_out(ya, os_, ls_, zb_src, zb_col, ym, w, x, tm=512):
    m, d = x.shape
    row = lambda width, col=0: pl.BlockSpec((tm, width), lambda i: (i, col))
    heads = [o for group in os_ for o in group]
    return pl.pallas_call(
        _hawk_out_kernel,
        grid=(m // tm,),
        in_specs=[row(LRU_WIDTH)] + [row(DIL_HEAD_DIM)] * len(heads) + [row(LANES)] * len(ls_)
                 + [row(DIL_WIDTH, zb_col), row(MEM_WIDTH),
                    pl.BlockSpec(w.shape, lambda i: (0, 0)), row(d)],
        out_specs=row(d),
        out_shape=jax.ShapeDtypeStruct((m, d), F32),
        compiler_params=_params("arbitrary"),
        name="hawk_out",
    )(ya, *heads, *ls_, zb_src, ym, w, x)


def _compress_kernel(k_ref, v_ref, pe_ref, w1_ref, w2k_ref, w2vt_ref, ko_ref, vto_ref):
    n_blk = k_ref.shape[0] // CMP_STRIDE

    def hidden(which, src_ref):
        x = jnp.concatenate([src_ref[pl.ds(p, n_blk, stride=CMP_STRIDE), :] for p in range(CMP_STRIDE)],
                            axis=1).astype(BF16)
        first = jnp.dot(x, w1_ref[which, 0], preferred_element_type=F32)
        second = jnp.dot(x, w1_ref[which, 1], preferred_element_type=F32)
        pe = (jnp.dot(pe_ref[which, 0], w1_ref[which, 0], preferred_element_type=F32)
              + jnp.dot(pe_ref[which, 1], w1_ref[which, 1], preferred_element_type=F32))
        return _silu(first + pltpu.roll(second, n_blk - 1, axis=0) + pe[0:1, :]).astype(BF16)

    act_k, act_v = hidden(0, k_ref), hidden(1, v_ref)
    part = lambda a, g: a[:, g * PHI_HIDDEN:(g + 1) * PHI_HIDDEN]
    ks = [jnp.dot(part(act_k, g), w2k_ref[...], preferred_element_type=F32) for g in range(NSA_KV_GROUPS)]
    vts = [_dot_t(w2vt_ref[...], part(act_v, g)) for g in range(NSA_KV_GROUPS)]
    ko_ref[...] = jnp.concatenate(ks, axis=1).astype(ko_ref.dtype)
    vto_ref[...] = jnp.concatenate(vts, axis=0).astype(vto_ref.dtype)


def _compress(src, k_col, v_col, pe_k, pe_v, k_w1, k_w2, v_w1, v_w2, batch, seq):
    half = CMP_BLOCK // 2
    assert half == CMP_STRIDE and NSA_KV == LANES
    n_blk = seq // CMP_STRIDE
    hd = NSA_HEAD_DIM
    w1 = jnp.stack([k_w1, v_w1]).reshape(2, 2, half, 1, hd, 1, PHI_HIDDEN)
    eye = jnp.eye(NSA_KV_GROUPS, dtype=w1.dtype)[None, None, None, :, None, :, None]
    w1e = (w1 * eye).reshape(2, 2, half * NSA_KV, NSA_KV_GROUPS * PHI_HIDDEN).astype(BF16)
    pe = jnp.stack([pe_k, pe_v]).reshape(2, 2, half, 1, hd)
    pe = jnp.broadcast_to(pe, (2, 2, half, NSA_KV_GROUPS, hd)).reshape(2, 2, 1, half * NSA_KV)
    pe = jnp.broadcast_to(pe, (2, 2, SUBLANES, half * NSA_KV)).astype(BF16)
    w2k = k_w2.astype(BF16)
    w2vt = v_w2.T.astype(BF16)
    whole = lambda a: pl.BlockSpec(a.shape, lambda b: (0,) * a.ndim)
    return pl.pallas_call(
        _compress_kernel,
        grid=(batch,),
        in_specs=[pl.BlockSpec((seq, LANES), lambda b: (b, k_col)),
                  pl.BlockSpec((seq, LANES), lambda b: (b, v_col)),
                  whole(pe), whole(w1e), whole(w2k), whole(w2vt)],
        out_specs=[pl.BlockSpec((None, n_blk, NSA_KV), lambda b: (b, 0, 0)),
                   pl.BlockSpec((None, NSA_KV, n_blk), lambda b: (b, 0, 0))],
        out_shape=[jax.ShapeDtypeStruct((batch, n_blk, NSA_KV), BF16),
                   jax.ShapeDtypeStruct((batch, NSA_KV, n_blk), BF16)],
        compiler_params=_params("arbitrary"),
        name="compress",
    )(src, src, pe, w1e, w2k, w2vt)


KEY_CHUNK = 256
NSA_TQ = 256
SLOPE_PIECES = 3
N_FEATS = 2 * SLOPE_PIECES
FEAT_LANES = 32
KEY_COLS = NSA_HEAD_DIM + 2 * FEAT_LANES
N_SLC = 32


def _slope_pieces(slope):
    rest = np.float32(slope)
    pieces = []
    for _ in range(SLOPE_PIECES):
        p = np.float32(np.asarray(rest).astype(BF16))
        pieces.append(float(p))
        rest = np.float32(rest - p)
    return pieces


def _lane_table(lane, values):
    out = jnp.zeros(lane.shape, F32)
    for idx, v in enumerate(values):
        out = jnp.where(lane == idx, v, out)
    return out


def _key_feats(pos_hi, pos_lo, lane):
    return jnp.where(lane < SLOPE_PIECES, pos_hi, jnp.where(lane < N_FEATS, pos_lo, 0)).astype(F32)


def _tile_heads(x):
    return jnp.concatenate([x] * NSA_R, axis=1)


def _nsa_kernel(q_ref, kc_ref, vct_ref, ksrc_ref, vsrc_ref, kwsrc_ref, vwsrc_ref, feat_ref, hot_ref,
                gl_ref, z_ref, o_ref, ks_ref, vst_ref, kw_ref, vwt_ref, s_ref, acc_ref, imp_ref):
    i = pl.program_id(1)
    tq = q_ref.shape[0]
    hd = NSA_HEAD_DIM
    n_cmp = kc_ref.shape[0]

    @pl.when(i == 0)
    def _():
        for g in range(NSA_KV_GROUPS):
            gs = slice(g * hd, (g + 1) * hd)
            for dst, src, tail in ((ks_ref, ksrc_ref, hot_ref[...]),
                                   (kw_ref, kwsrc_ref, jnp.zeros(hot_ref.shape, BF16))):
                dst[:, g * KEY_COLS:g * KEY_COLS + hd] = src[:, gs]
                dst[:, g * KEY_COLS + hd:g * KEY_COLS + hd + FEAT_LANES] = feat_ref[...]
                dst[:, g * KEY_COLS + hd + FEAT_LANES:(g + 1) * KEY_COLS] = tail
        for c in range(vst_ref.shape[0]):
            rows = slice(c * KEY_CHUNK, (c + 1) * KEY_CHUNK)
            vst_ref[c] = vsrc_ref[rows, :].astype(F32).T.astype(BF16)
            vwt_ref[c] = vwsrc_ref[rows, :].astype(F32).T.astype(BF16)

    slopes_all = _alibi_slopes(NSA_HEADS)
    gates_t = jax.nn.sigmoid(gl_ref[...]).T
    feat_lane = lax.broadcasted_iota(jnp.int32, (tq, FEAT_LANES), 1)
    no_sel = jnp.zeros((NSA_R * tq, FEAT_LANES), BF16)
    key_row = lax.broadcasted_iota(jnp.int32, (KEY_CHUNK, tq), 0)
    t_pos = i * tq + lax.broadcasted_iota(jnp.int32, (KEY_CHUNK, tq), 1)
    ones_rows = jnp.ones((16, KEY_CHUNK), BF16)
    win_lo = jnp.maximum(i * tq - (WIN_SIZE - 1), 0) // KEY_CHUNK
    chunks_hi = (i * tq + tq - 1) // KEY_CHUNK + 1

    groups = range(NSA_KV_GROUPS)
    q_win, q_slc, o_cmp = [], [], []
    for g in groups:
        slopes = slopes_all[g * NSA_R:(g + 1) * NSA_R]
        gs = slice(g * hd, (g + 1) * hd)
        q_parts = []
        for r in range(NSA_R):
            qr = q_ref[:, (g * NSA_R + r) * hd:(g * NSA_R + r + 1) * hd].astype(F32) * (hd ** -0.5)
            feats = _lane_table(feat_lane, _slope_pieces(slopes[r]) * 2)
            q_parts.append(jnp.concatenate([qr, feats], axis=1).astype(BF16))
        q_feat = jnp.concatenate(q_parts, axis=0)
        q_aug = jnp.concatenate([q_feat, no_sel], axis=1)

        n_row = lax.broadcasted_iota(jnp.int32, (n_cmp, tq), 0)
        t_cmp = i * tq + lax.broadcasted_iota(jnp.int32, (n_cmp, tq), 1)
        visible = t_cmp >= n_row * CMP_STRIDE + (CMP_BLOCK - 1)
        cfeat_row = lax.broadcasted_iota(jnp.int32, (n_cmp, 2 * FEAT_LANES), 0)
        cfeat_lane = lax.broadcasted_iota(jnp.int32, (n_cmp, 2 * FEAT_LANES), 1)
        kc_feats = _key_feats(cfeat_row * CMP_STRIDE, 0, cfeat_lane)
        kc_aug = jnp.concatenate([kc_ref[:, gs], kc_feats.astype(BF16)], axis=1)
        s = _dot_t(kc_aug, q_aug) + _tile_heads(jnp.where(visible, 0.0, NEG_INF))
        m = jnp.max(s, axis=0, keepdims=True)
        e = jnp.exp(s - m)
        t_one = i * tq + lax.broadcasted_iota(jnp.int32, (1, NSA_R * tq), 1) % tq
        any_visible = t_one >= (CMP_BLOCK - 1)
        p = e * jnp.where(any_visible, 1.0 / jnp.sum(e, axis=0, keepdims=True), 0.0)
        o_cmp.append(jnp.dot(vct_ref[gs, :], p.astype(BF16), preferred_element_type=F32))
        p_sum = p[:, 0:tq]
        for r in range(1, NSA_R):
            p_sum = p_sum + p[:, r * tq:(r + 1) * tq]

        band = p_sum + pltpu.roll(p_sum, 1, axis=0)
        for k in range(1, CMP_PER_SLC):
            band = band + pltpu.roll(p_sum, n_cmp - k, axis=0)
        halves = []
        for half in range(tq // LANES):
            imp_ref[...] = band[:, half * LANES:(half + 1) * LANES]
            halves.append(imp_ref[pl.ds(0, N_SLC, stride=CMP_PER_SLC), :])
        imp = jnp.concatenate(halves, axis=1)
        blk_j = lax.broadcasted_iota(jnp.int32, (N_SLC, tq), 0)
        cur = (i * tq + lax.broadcasted_iota(jnp.int32, (N_SLC, tq), 1)) // SLC_BLOCK
        forced = (blk_j == 0) | (blk_j == cur) | (blk_j == cur - 1)
        v_imp = jnp.where(forced, SEL_FORCE, jnp.where(blk_j > cur, -SEL_FORCE, imp))
        rank = jnp.zeros((N_SLC, tq), F32)
        for other in range(N_SLC):
            row = v_imp[other:other + 1, :]
            ahead = (row > v_imp) | ((row == v_imp) & (blk_j > other))
            rank = rank + jnp.where(ahead, 1.0, 0.0)
        sel_bias = jnp.where(rank < SLC_TOP_N, 0.0, NEG_INF)

        padded = jnp.concatenate([sel_bias, jnp.zeros((LANES - N_SLC, tq), F32)], axis=0)
        sel_t = padded.T[:, 0:FEAT_LANES].astype(BF16)
        q_win.append(q_aug)
        q_slc.append(jnp.concatenate([q_feat, jnp.concatenate([sel_t] * NSA_R, axis=0)], axis=1))

    def attend(q_brs, k_ref, key_cols, vt_ref, lo, hi, masked_from, mask_fn):
        def scores(c, m_run, masked):
            start = pl.multiple_of(c * KEY_CHUNK, KEY_CHUNK)
            scs = [_dot_t(k_ref[pl.ds(start, KEY_CHUNK), g * key_cols:(g + 1) * key_cols], q_brs[g])
                   for g in groups]
            if masked:
                bias = _tile_heads(jnp.where(mask_fn(t_pos - (start + key_row)), 0.0, NEG_INF))
                scs = [sc + bias for sc in scs]
            for g in groups:
                s_ref[g, c] = scs[g]
            return tuple(jnp.maximum(m_run[g], jnp.max(scs[g], axis=0, keepdims=True)) for g in groups)

        m_rows = tuple(jnp.full((1, NSA_R * tq), NEG_INF, F32) for _ in groups)
        m_rows = lax.fori_loop(lo, masked_from, functools.partial(scores, masked=False), m_rows)
        m_rows = lax.fori_loop(masked_from, hi, functools.partial(scores, masked=True), m_rows)
        acc_ref[...] = jnp.zeros(acc_ref.shape, F32)

        def weighted(c, carry):
            for g in groups:
                e = jnp.exp(s_ref[g, c] - m_rows[g]).astype(BF16)
                v_ext = jnp.concatenate([vt_ref[c, g * hd:(g + 1) * hd, :], ones_rows], axis=0)
                acc_ref[g] += jnp.dot(v_ext, e, preferred_element_type=F32)
            return carry

        lax.fori_loop(lo, hi, weighted, 0)
        return [acc_ref[g, 0:hd, :] / acc_ref[g, hd:hd + 1, :] for g in groups]

    o_slc = attend(q_slc, ks_ref, KEY_COLS, vst_ref, 0, chunks_hi, (i * tq) // KEY_CHUNK,
                   lambda dist: dist >= 0)
    o_win = attend(q_win, kw_ref, KEY_COLS, vwt_ref, win_lo, chunks_hi, win_lo,
                   lambda dist: (dist >= 0) & (dist <= WIN_SIZE - 1))

    for g in groups:
        def gate(kind):
            base = g * NSA_R * 3 + kind
            return jnp.concatenate([gates_t[base + 3 * r:base + 3 * r + 1, :] for r in range(NSA_R)], axis=1)

        o = gate(0) * o_cmp[g] + gate(1) * o_slc[g] + gate(2) * o_win[g]
        pairs = []
        for r in range(0, NSA_R, 2):
            two = jnp.concatenate([o[:, r * tq:(r + 1) * tq], o[:, (r + 1) * tq:(r + 2) * tq]], axis=0)
            pairs.append(two.T)
        cs = slice(g * NSA_R * hd, (g + 1) * NSA_R * hd)
        o_ref[:, cs] = (jnp.concatenate(pairs, axis=1) * _silu(z_ref[:, cs].astype(F32))).astype(o_ref.dtype)


def _key_pos_feats(seq):
    assert seq // SLC_BLOCK == N_SLC <= FEAT_LANES
    pos = np.arange(seq)
    feats = np.zeros((seq, FEAT_LANES), np.float32)
    feats[:, 0:SLOPE_PIECES] = ((pos // SLC_BLOCK) * SLC_BLOCK)[:, None]
    feats[:, SLOPE_PIECES:N_FEATS] = (pos % SLC_BLOCK)[:, None]
    onehot = (np.arange(FEAT_LANES)[None, :] == (pos // SLC_BLOCK)[:, None]).astype(np.float32)
    return jnp.asarray(feats, BF16), jnp.asarray(onehot, BF16)


def _nsa_attn(nb, kv_col0, z_col, k_cmp, v_cmp_t, nf, gl_col, batch, seq):
    tq = NSA_TQ
    nq = seq // tq
    feats, onehot = _key_pos_feats(seq)
    kv_blk = kv_col0 // NSA_KV
    seq_cols = lambda col: pl.BlockSpec((seq, NSA_KV), lambda b, i: (b, kv_blk + col))
    const = lambda a: pl.BlockSpec(a.shape, lambda b, i: (0,) * a.ndim)
    per_batch = lambda a: pl.BlockSpec((None,) + a.shape[1:], lambda b, i: (b,) + (0,) * (a.ndim - 1))
    return pl.pallas_call(
        _nsa_kernel,
        grid=(batch, nq),
        in_specs=[pl.BlockSpec((tq, NSA_WIDTH), lambda b, i: (b * nq + i, 0)),
                  per_batch(k_cmp), per_batch(v_cmp_t),
                  seq_cols(0), seq_cols(1), seq_cols(2), seq_cols(3), const(feats), const(onehot),
                  pl.BlockSpec((tq, LANES), lambda b, i: (b * nq + i, gl_col)),
                  pl.BlockSpec((tq, NSA_WIDTH), lambda b, i: (b * nq + i, z_col))],
        out_specs=pl.BlockSpec((tq, NSA_WIDTH), lambda b, i: (b * nq + i, 0)),
        out_shape=jax.ShapeDtypeStruct((batch * seq, NSA_WIDTH), BF16),
        scratch_shapes=[pltpu.VMEM((seq, NSA_KV_GROUPS * KEY_COLS), BF16),
                        pltpu.VMEM((seq // KEY_CHUNK, NSA_KV, KEY_CHUNK), BF16),
                        pltpu.VMEM((seq, NSA_KV_GROUPS * KEY_COLS), BF16),
                        pltpu.VMEM((seq // KEY_CHUNK, NSA_KV, KEY_CHUNK), BF16),
                        pltpu.VMEM((NSA_KV_GROUPS, seq // KEY_CHUNK, KEY_CHUNK, NSA_R * tq), F32),
                        pltpu.VMEM((NSA_KV_GROUPS, NSA_HEAD_DIM + 16, NSA_R * tq), F32),
                        pltpu.VMEM((seq // CMP_STRIDE, LANES), F32)],
        compiler_params=_params("arbitrary", "arbitrary"),
        name="nsa_attn",
    )(nb, k_cmp, v_cmp_t, nb, nb, nb, nb, feats, onehot, nf, nb)


def _nsa_out_kernel(yo_ref, ym_ref, w_ref, x_ref, g_ref, out_ref):
    y = jnp.dot(yo_ref[...], w_ref[0:NSA_WIDTH, :], preferred_element_type=F32)
    y = y + jnp.dot(ym_ref[...], w_ref[NSA_WIDTH:NSA_WIDTH + MEM_WIDTH, :], preferred_element_type=F32)
    x = x_ref[...] + y
    ms = jnp.mean(x * x, axis=-1, keepdims=True)
    out_ref[...] = x * lax.rsqrt(ms + NORM_EPS) * g_ref[...]


def _nsa_out(yo, ym, w, x, g, tm=512):
    m, d = x.shape
    row = lambda width: pl.BlockSpec((tm, width), lambda i: (i, 0))
    return pl.pallas_call(
        _nsa_out_kernel,
        grid=(m // tm,),
        in_specs=[row(NSA_WIDTH), row(MEM_WIDTH), pl.BlockSpec(w.shape, lambda i: (0, 0)), row(d),
                  pl.BlockSpec((1, d), lambda i: (0, 0))],
        out_specs=row(d),
        out_shape=jax.ShapeDtypeStruct((m, d), F32),
        compiler_params=_params("arbitrary"),
        name="nsa_out",
    )(yo, ym, w, x, g.reshape(1, d))


def _cols(w, *ranges):
    return jnp.concatenate([w[:, a:b] for a, b in ranges], axis=1).astype(BF16)


def _hawk_layer(x, mem, batch, seq, norm_g, w_in, conv_w, conv_b, ga_w, ga_b, gx_w, gx_b, lam,
                mem_norm_g, w_mem_kv, w_out):
    xa0, za0 = 0, LRU_WIDTH
    q0 = 2 * LRU_WIDTH
    k0, v0 = q0 + DIL_QKV, q0 + 2 * DIL_QKV
    zb0 = q0 + 3 * DIL_QKV
    qm0 = zb0 + DIL_WIDTH
    zm0 = qm0 + MEM_WIDTH

    def qkv_cols(gi):
        return [(base + gi * DIL_WIDTH, base + (gi + 1) * DIL_WIDTH) for base in (q0, k0, v0)]

    w_nat = _cols(w_in, (xa0, za0), (za0, q0), *qkv_cols(0), (zb0, qm0), (qm0, zm0), (zm0, zm0 + MEM_WIDTH))
    nat_width = w_nat.shape[1] - LRU_WIDTH
    xa, hb = _norm_matmul(x, norm_g, w_nat, [(LRU_WIDTH, F32), (nat_width, BF16)])
    za_col = 0
    qkv0_col = LRU_WIDTH // DIL_WIDTH
    zb_col = (LRU_WIDTH + 3 * DIL_WIDTH) // DIL_WIDTH
    qm_col = (LRU_WIDTH + 4 * DIL_WIDTH) // MEM_WIDTH
    zm_col = qm_col + 1
    qkv = [(hb[None], qkv0_col)]
    for gi in range(1, len(DIL_GROUPS)):
        qkv.append((_norm_matmul(x, norm_g, _cols(w_in, *qkv_cols(gi)), [(3 * DIL_WIDTH, BF16)],
                                 dil=DIL_GROUPS[gi][1]), 0))
    n_mem = mem.shape[0] // batch
    mem_kv, = _norm_matmul(mem, mem_norm_g, w_mem_kv.astype(BF16), [(2 * MEM_WIDTH, BF16)])

    ya = _rglru(xa, hb, za_col, conv_w, conv_b, _pack_block_diag(ga_w), ga_b, _pack_block_diag(gx_w), gx_b,
                lam, batch, seq)
    attn = [_dil_attn(arr, col0, gi, batch, seq) for gi, (arr, col0) in enumerate(qkv)]
    ym = _mem_attn(hb, qm_col, mem_kv, hb, zm_col, batch, seq, n_mem)
    return _hawk_out(ya, [o for o, _ in attn], [l for _, l in attn], hb, zb_col, ym,
                     w_out.astype(BF16), x)


def _nsa_layer(x, mem, batch, seq, norm_g, w_in, pe_k, pe_v, phik_w1, phik_w2, phiv_w1, phiv_w2,
               mem_norm_g, w_mem_kv, w_out, final_g):
    kv0 = NSA_WIDTH
    gl0 = kv0 + 6 * NSA_KV
    z0 = gl0 + 3 * NSA_HEADS
    qm0 = z0 + NSA_WIDTH
    zm0 = qm0 + MEM_WIDTH
    gl_w = jnp.pad(w_in[:, gl0:z0], ((0, 0), (0, LANES - 3 * NSA_HEADS)))
    w_all = jnp.concatenate([gl_w, w_in[:, kv0:kv0 + 2 * NSA_KV], w_in[:, 0:kv0], w_in[:, z0:qm0],
                             w_in[:, kv0 + 2 * NSA_KV:gl0], w_in[:, qm0:zm0 + MEM_WIDTH]], axis=1).astype(BF16)
    f32_width = LANES + 2 * NSA_KV
    nf, nb = _norm_matmul(x, norm_g, w_all, [(f32_width, F32), (w_all.shape[1] - f32_width, BF16)])
    gl_col, kc_col, vc_col = 0, 1, 2
    z_col = 1
    kv_col0 = 2 * NSA_WIDTH
    qm_col = (kv_col0 + 4 * NSA_KV) // MEM_WIDTH
    zm_col = qm_col + 1
    n_mem = mem.shape[0] // batch
    mem_kv, = _norm_matmul(mem, mem_norm_g, w_mem_kv.astype(BF16), [(2 * MEM_WIDTH, BF16)])

    k_cmp, v_cmp_t = _compress(nf, kc_col, vc_col, pe_k, pe_v, phik_w1, phik_w2, phiv_w1, phiv_w2,
                               batch, seq)
    yo = _nsa_attn(nb, kv_col0, z_col, k_cmp, v_cmp_t, nf, gl_col, batch, seq)
    ym = _mem_attn(nb, qm_col, mem_kv, nb, zm_col, batch, seq, n_mem)
    return _nsa_out(yo, ym, w_out.astype(BF16), x, final_g)


def kernel(x, mem, hawk_norm, hawk_w_in, hawk_conv_w, hawk_conv_b, hawk_gate_a_w, hawk_gate_a_b,
           hawk_gate_x_w, hawk_gate_x_b, hawk_lambda, hawk_mem_norm, hawk_w_mem_kv, hawk_w_out,
           nsa_norm, nsa_w_in, nsa_pe_k, nsa_pe_v, nsa_phi_k_w1, nsa_phi_k_w2, nsa_phi_v_w1,
           nsa_phi_v_w2, nsa_mem_norm, nsa_w_mem_kv, nsa_w_out, final_norm):
    batch, seq, d = x.shape
    assert hawk_norm.shape[0] == 1 and nsa_norm.shape[0] == 1, "one layer of each kind"
    assert seq % (ATTN_BLOCK * DIL_GROUPS[-1][1]) == 0
    x2 = x.reshape(batch * seq, d)
    mem2 = mem.reshape(batch * mem.shape[1], d)
    x2 = _hawk_layer(x2, mem2, batch, seq, hawk_norm[0], hawk_w_in[0], hawk_conv_w[0], hawk_conv_b[0],
                     hawk_gate_a_w[0], hawk_gate_a_b[0].reshape(-1), hawk_gate_x_w[0],
                     hawk_gate_x_b[0].reshape(-1), hawk_lambda[0], hawk_mem_norm[0], hawk_w_mem_kv[0],
                     hawk_w_out[0])
    out = _nsa_layer(x2, mem2, batch, seq, nsa_norm[0], nsa_w_in[0], nsa_pe_k[0], nsa_pe_v[0],
                     nsa_phi_k_w1[0], nsa_phi_k_w2[0], nsa_phi_v_w1[0], nsa_phi_v_w2[0],
                     nsa_mem_norm[0], nsa_w_mem_kv[0], nsa_w_out[0], final_norm)
    return out.reshape(batch, seq, d)
```

```python
import functools

import numpy as np
import jax
import jax.numpy as jnp
from jax import lax
from jax.experimental import pallas as pl
from jax.experimental.pallas import tpu as pltpu

F32 = jnp.float32
BF16 = jnp.bfloat16

NORM_EPS = 1e-6
NEG_INF = -1e30
LANES = 128
SUBLANES = 8
ATTN_BLOCK = 128
VMEM_LIMIT = 56 * 1024 * 1024

LRU_WIDTH = 1024
LRU_BLOCKS = 16
LRU_BLOCK_DIM = LRU_WIDTH // LRU_BLOCKS
LRU_PACK = 256
CONV_WIDTH = 4
LRU_C = 8.0

DIL_GROUPS = ((128, 1), (512, 4), (2048, 16))
DIL_HEADS = 4
DIL_HEAD_DIM = 128
DIL_WIDTH = DIL_HEADS * DIL_HEAD_DIM
DIL_QKV = len(DIL_GROUPS) * DIL_WIDTH
LSE_LANES = LANES // DIL_HEADS

MEM_HEADS = 4
MEM_HEAD_DIM = 64
MEM_WIDTH = MEM_HEADS * MEM_HEAD_DIM

NSA_HEADS = 16
NSA_KV_GROUPS = 2
NSA_R = NSA_HEADS // NSA_KV_GROUPS
NSA_HEAD_DIM = 64
NSA_WIDTH = NSA_HEADS * NSA_HEAD_DIM
NSA_KV = NSA_KV_GROUPS * NSA_HEAD_DIM
CMP_BLOCK = 32
CMP_STRIDE = 16
SLC_BLOCK = 64
SLC_TOP_N = 8
WIN_SIZE = 512
PHI_HIDDEN = 256
SEL_FORCE = 1e6
CMP_PER_SLC = SLC_BLOCK // CMP_STRIDE


def _alibi_slopes(n):
    return [float(v) for v in np.exp2(-8.0 * np.arange(1, n + 1) / n).astype(np.float32)]


def _params(*semantics):
    return pltpu.CompilerParams(dimension_semantics=semantics, vmem_limit_bytes=VMEM_LIMIT)


def _silu(z):
    return z * jax.nn.sigmoid(z)


def _dot_t(a, b):
    return lax.dot_general(a, b, (((1,), (1,)), ((), ())), preferred_element_type=F32)


def _rms_norm_rows(x, g):
    ms = jnp.mean(x * x, axis=-1, keepdims=True)
    return (x * lax.rsqrt(ms + NORM_EPS) * g).astype(BF16)


def _norm_matmul_kernel(*refs, dil):
    if dil == 1:
        x_ref, g_ref, w_ref, *o_refs = refs
        xn = _rms_norm_rows(x_ref[...], g_ref[...])
        col = 0
        for o_ref in o_refs:
            width = o_ref.shape[1]
            o_ref[...] = jnp.dot(xn, w_ref[:, col:col + width], preferred_element_type=F32).astype(o_ref.dtype)
            col += width
        return
    if dil % SUBLANES == 0:
        x_ref, g_ref, w_ref, o_ref = refs
        tm, k = x_ref.shape
        x = x_ref[...]
        xn = x * lax.rsqrt(jnp.mean(x * x, axis=-1, keepdims=True) + NORM_EPS) * g_ref[...]
        xn = jnp.swapaxes(xn.reshape(tm // dil, dil, k), 0, 1).reshape(tm, k).astype(BF16)
        res = jnp.dot(xn, w_ref[...], preferred_element_type=F32).astype(o_ref.dtype)
        for c in range(dil):
            o_ref[c] = res[c * (tm // dil):(c + 1) * (tm // dil)]
        return
    *x_refs, g_ref, w_ref, o_ref, xn_ref = refs
    per = x_refs[0].shape[0] // dil
    inv_k = 1.0 / (len(x_refs) * LANES)
    for c in range(dil):
        xs = [x_ref[pl.ds(c, per, stride=dil), :] for x_ref in x_refs]
        ss = xs[0] * xs[0]
        for xj in xs[1:]:
            ss = ss + xj * xj
        r = lax.rsqrt(jnp.sum(ss, axis=-1, keepdims=True) * inv_k + NORM_EPS)
        for j, xj in enumerate(xs):
            cols = slice(j * LANES, (j + 1) * LANES)
            xn_ref[c * per:(c + 1) * per, cols] = (xj * r * g_ref[:, cols]).astype(BF16)
    res = jnp.dot(xn_ref[...], w_ref[...], preferred_element_type=F32).astype(o_ref.dtype)
    for c in range(dil):
        o_ref[c] = res[c * per:(c + 1) * per]


def _norm_matmul(x, g, w, outs, dil=1, tm=512):
    m, k = x.shape
    n = w.shape[1]
    tm = min(tm, m)
    assert m % tm == 0 and k % LANES == 0 and sum(width for width, _ in outs) == n
    resident = [pl.BlockSpec((1, k), lambda i: (0, 0)), pl.BlockSpec((k, n), lambda i: (0, 0))]
    if dil == 1:
        return pl.pallas_call(
            functools.partial(_norm_matmul_kernel, dil=1),
            grid=(m // tm,),
            in_specs=[pl.BlockSpec((tm, k), lambda i: (i, 0))] + resident,
            out_specs=[pl.BlockSpec((tm, width), lambda i: (i, 0)) for width, _ in outs],
            out_shape=[jax.ShapeDtypeStruct((m, width), dtype) for width, dtype in outs],
            compiler_params=_params("arbitrary"),
            name="norm_matmul",
        )(x, g.reshape(1, k), w)
    per = tm // dil
    (_, out_dtype), = outs
    assert tm % dil == 0 and per % 16 == 0
    if dil % SUBLANES == 0:
        x_specs, scratch = [pl.BlockSpec((tm, k), lambda i: (i, 0))], []
    else:
        x_specs = [pl.BlockSpec((tm, LANES), functools.partial(lambda i, j: (i, j), j=j)) for j in range(k // LANES)]
        scratch = [pltpu.VMEM((tm, k), BF16)]
    return pl.pallas_call(
        functools.partial(_norm_matmul_kernel, dil=dil),
        grid=(m // tm,),
        in_specs=x_specs + resident,
        out_specs=pl.BlockSpec((dil, per, n), lambda i: (0, i, 0)),
        out_shape=jax.ShapeDtypeStruct((dil, m // dil, n), out_dtype),
        scratch_shapes=scratch,
        compiler_params=_params("arbitrary"),
        name="norm_matmul_dil",
    )(*([x] * len(x_specs)), g.reshape(1, k), w)


def _rglru_kernel(xa_ref, za_ref, cw_ref, cb_ref, wa_ref, ba_ref, wx_ref, bx_ref, lam_ref,
                  o_ref, xpad_ref, h_ref):
    t = pl.program_id(1)
    tt, width = xa_ref.shape
    halo = 8

    @pl.when(t == 0)
    def _():
        xpad_ref[0:halo, :] = jnp.zeros((halo, width), F32)
        h_ref[...] = jnp.zeros_like(h_ref)

    x = xa_ref[...]
    xpad_ref[halo:halo + tt, :] = x
    cw = cw_ref[...]
    y = cw[CONV_WIDTH - 1:CONV_WIDTH] * x
    for k in range(1, CONV_WIDTH):
        y = y + cw[CONV_WIDTH - 1 - k:CONV_WIDTH - k] * xpad_ref[halo - k:halo - k + tt, :]
    y = y + cb_ref[...]
    xpad_ref[0:halo, :] = x[tt - halo:tt, :]

    yb = y.astype(BF16)
    r_parts, i_parts = [], []
    for p in range(width // LRU_PACK):
        ys = yb[:, p * LRU_PACK:(p + 1) * LRU_PACK]
        r_parts.append(jnp.dot(ys, wa_ref[p], preferred_element_type=F32))
        i_parts.append(jnp.dot(ys, wx_ref[p], preferred_element_type=F32))
    r = jax.nn.sigmoid(jnp.concatenate(r_parts, axis=1) + ba_ref[...])
    gi = jax.nn.sigmoid(jnp.concatenate(i_parts, axis=1) + bx_ref[...])

    nl = -lam_ref[...]
    softplus = jnp.maximum(nl, 0.0) + jnp.log1p(jnp.exp(-jnp.abs(nl)))
    log_a = (-LRU_C) * r * softplus
    a = jnp.exp(log_a)
    z2 = 2.0 * log_a
    u = a * a
    one_minus = jnp.where(u == 1.0, -z2, jnp.where(u < 0.5, 1.0 - u, (1.0 - u) * z2 / jnp.log(u)))
    mult = jnp.sqrt(one_minus)
    b = y * gi * mult
    first = (lax.broadcasted_iota(jnp.int32, (SUBLANES, width), 0) == 0) & (t == 0)
    b = jnp.concatenate([jnp.where(first, (y * gi)[0:SUBLANES], b[0:SUBLANES]), b[SUBLANES:]], axis=0)

    groups = tt // SUBLANES
    a3 = a.reshape(groups, SUBLANES, width)
    b3 = b.reshape(groups, SUBLANES, width)
    sub = lax.broadcasted_iota(jnp.int32, (groups, SUBLANES, width), 1)
    k = 1
    while k < SUBLANES:
        keep = sub >= k
        a_sh = jnp.where(keep, pltpu.roll(a3, k, axis=1), 1.0)
        b_sh = jnp.where(keep, pltpu.roll(b3, k, axis=1), 0.0)
        b3 = a3 * b_sh + b3
        a3 = a3 * a_sh
        k *= 2
    carry = jnp.broadcast_to(h_ref[...], (SUBLANES, width))
    hs = []
    for gidx in range(groups):
        hg = a3[gidx] * carry + b3[gidx]
        hs.append(hg)
        carry = jnp.broadcast_to(hg[SUBLANES - 1:SUBLANES], (SUBLANES, width))
    h = jnp.concatenate(hs, axis=0)
    h_ref[...] = carry[0:1]
    o_ref[...] = (h * _silu(za_ref[...].astype(F32))).astype(o_ref.dtype)


def _rglru(xa, za_src, za_col, conv_w, conv_b, wa, ba, wx, bx, lam, batch, seq, tt=512):
    width = LRU_WIDTH
    nt = seq // tt
    packs = width // LRU_PACK
    vec = pl.BlockSpec((1, width), lambda b, t: (0, 0))
    gate_w = pl.BlockSpec((packs, LRU_PACK, LRU_PACK), lambda b, t: (0, 0, 0))
    return pl.pallas_call(
        _rglru_kernel,
        grid=(batch, nt),
        in_specs=[pl.BlockSpec((tt, width), lambda b, t: (b * nt + t, 0)),
                  pl.BlockSpec((tt, width), lambda b, t: (b * nt + t, za_col)),
                  pl.BlockSpec((CONV_WIDTH, width), lambda b, t: (0, 0)),
                  vec, gate_w, vec, gate_w, vec, vec],
        out_specs=pl.BlockSpec((tt, width), lambda b, t: (b * nt + t, 0)),
        out_shape=jax.ShapeDtypeStruct((batch * seq, width), BF16),
        scratch_shapes=[pltpu.VMEM((tt + 8, width), F32), pltpu.VMEM((1, width), F32)],
        compiler_params=_params("arbitrary", "arbitrary"),
        name="rglru",
    )(xa, za_src, conv_w, conv_b.reshape(1, width), wa, ba.reshape(1, width), wx, bx.reshape(1, width),
      lam.reshape(1, width))


def _pack_block_diag(w):
    per = LRU_PACK // LRU_BLOCK_DIM
    w = w.reshape(LRU_BLOCKS // per, per, LRU_BLOCK_DIM, LRU_BLOCK_DIM)
    eye = jnp.eye(per, dtype=w.dtype)
    packed = w[:, :, :, None, :] * eye[None, :, None, :, None]
    return packed.reshape(LRU_BLOCKS // per, LRU_PACK, LRU_PACK).astype(BF16)


def _dil_attn_kernel(*refs, slopes, pos_scale, max_dist, has_halo, dil, n_cls, n_blk):
    if has_halo:
        q_ref, kh_ref, k_ref, vh_ref, v_ref = refs[:5]
        out_refs = refs[5:]
    else:
        q_ref, k_ref, v_ref = refs[:3]
        out_refs = refs[3:]
    n_out = DIL_HEADS + 1
    dst_refs = out_refs[:n_out]
    stage_refs = out_refs[n_out:] if dil > 1 else dst_refs
    first_super = pl.program_id(1) == 0
    cls0 = pl.program_id(2) * n_cls
    blk = ATTN_BLOCK
    scale = DIL_HEAD_DIM ** -0.5

    def band(width, halo_live):
        row = lax.broadcasted_iota(jnp.int32, (blk, width), 0)
        col = lax.broadcasted_iota(jnp.int32, (blk, width), 1)
        dist = (width - blk) + row - col
        valid = (dist >= 0) & (dist <= max_dist)
        if halo_live is not None:
            valid = valid & ((col >= blk) | halo_live)
        return valid, (dist * pos_scale).astype(F32)

    def scores(cc, jb):
        cur = slice(jb * blk, (jb + 1) * blk)
        if jb > 0:
            valid, distf = band(2 * blk, None)
        elif has_halo:
            valid, distf = band(2 * blk, jnp.logical_not(first_super))
        else:
            valid, distf = band(blk, None)
        out = []
        for h in range(DIL_HEADS):
            hs = slice(h * DIL_HEAD_DIM, (h + 1) * DIL_HEAD_DIM)
            q = q_ref[cc, cur, hs]
            if jb > 0:
                k = k_ref[cc, (jb - 1) * blk:(jb + 1) * blk, hs]
                v = v_ref[cc, (jb - 1) * blk:(jb + 1) * blk, hs]
            elif has_halo:
                k = jnp.concatenate([kh_ref[cc, :, hs], k_ref[cc, cur, hs]], axis=0)
                v = jnp.concatenate([vh_ref[cc, :, hs], v_ref[cc, cur, hs]], axis=0)
            else:
                k, v = k_ref[cc, cur, hs], v_ref[cc, cur, hs]
            s = _dot_t(q, k) * scale - slopes[h] * distf
            out.append((jnp.where(valid, s, NEG_INF), v))
        return out

    def finish(cc, jb, pairs):
        where = (cls0 + cc, slice(jb * blk, (jb + 1) * blk)) if dil > 1 else (slice(jb * blk, (jb + 1) * blk),)
        lses = []
        for h, (s, v) in enumerate(pairs):
            m = jnp.max(s, axis=-1, keepdims=True)
            e = jnp.exp(s - m)
            den = jnp.sum(e, axis=-1, keepdims=True)
            o = jnp.dot(e.astype(BF16), v, preferred_element_type=F32) / den
            stage_refs[h][where] = o.astype(stage_refs[h].dtype)
            lses.append(jnp.broadcast_to(m + jnp.log(den), (blk, LSE_LANES)))
        stage_refs[DIL_HEADS][where] = jnp.concatenate(lses, axis=1)

    pending = None
    for cc in range(n_cls):
        for jb in range(n_blk):
            pairs = scores(cc, jb)
            if pending is not None:
                finish(*pending)
            pending = (cc, jb, pairs)
    finish(*pending)

    if dil > 1:
        @pl.when(pl.program_id(2) == pl.num_programs(2) - 1)
        def _():
            for stage, dst in zip(stage_refs, dst_refs):
                dst[...] = jnp.swapaxes(stage[...], 0, 1).reshape(dst.shape).astype(dst.dtype)


def _dil_attn(qkv, col0, gi, batch, seq, work=4):
    window, dil = DIL_GROUPS[gi]
    sub = seq // dil
    nb = sub // ATTN_BLOCK
    n_blk = min(work, nb)
    n_cls = min(work // n_blk, dil)
    n_super = nb // n_blk
    has_halo = n_super > 1
    span = n_blk * ATTN_BLOCK
    slopes = _alibi_slopes(len(DIL_GROUPS) * DIL_HEADS)[gi * DIL_HEADS:(gi + 1) * DIL_HEADS]
    cur = lambda col: pl.BlockSpec((n_cls, span, DIL_WIDTH), lambda b, i, c: (c, b * n_super + i, col0 + col))
    halo = lambda col: pl.BlockSpec(
        (n_cls, ATTN_BLOCK, DIL_WIDTH),
        lambda b, i, c: (c, jnp.maximum((b * n_super + i) * n_blk - 1, 0), col0 + col))
    if has_halo:
        in_specs = [cur(0), halo(1), cur(1), halo(2), cur(2)]
    else:
        in_specs = [cur(0), cur(1), cur(2)]
    n_out = DIL_HEADS + 1
    *o, lse = pl.pallas_call(
        functools.partial(_dil_attn_kernel, slopes=slopes, pos_scale=dil, max_dist=window // dil,
                          has_halo=has_halo, dil=dil, n_cls=n_cls, n_blk=n_blk),
        grid=(batch, n_super, dil // n_cls),
        in_specs=in_specs,
        out_specs=[pl.BlockSpec((span * dil, LANES), lambda b, i, c: (b * n_super + i, 0))] * n_out,
        out_shape=[jax.ShapeDtypeStruct((batch * seq, LANES), BF16)] * DIL_HEADS
                  + [jax.ShapeDtypeStruct((batch * seq, LANES), F32)],
        scratch_shapes=[pltpu.VMEM((dil, span, LANES), F32)] * (n_out if dil > 1 else 0),
        compiler_params=_params("arbitrary", "arbitrary", "arbitrary"),
        name=f"dil_attn_d{dil}",
    )(*([qkv] * len(in_specs)))
    return o, lse


def _mem_attn_kernel(q_ref, k_ref, v_ref, z_ref, o_ref, ks_ref, vt_ref):
    hd = MEM_HEAD_DIM
    n_mem = k_ref.shape[0]

    @pl.when(pl.program_id(1) == 0)
    def _():
        ks_ref[...] = (k_ref[...].astype(F32) * (hd ** -0.5)).astype(BF16)
        vt = v_ref[...].astype(F32).T.astype(BF16)
        for h in range(MEM_HEADS):
            vt_ref[h, 0:hd, :] = vt[h * hd:(h + 1) * hd, :]
            vt_ref[h, hd:, :] = jnp.ones((vt_ref.shape[1] - hd, n_mem), BF16)

    heads = [slice(h * hd, (h + 1) * hd) for h in range(MEM_HEADS)]
    scores = [_dot_t(ks_ref[:, hs], q_ref[:, hs]) for hs in heads]
    outs = []
    for h, s in enumerate(scores):
        e = jnp.exp(s - jnp.max(s, axis=0, keepdims=True)).astype(BF16)
        acc = jnp.dot(vt_ref[h], e, preferred_element_type=F32)
        outs.append(acc[0:hd, :] / acc[hd:hd + 1, :])
    o = jnp.concatenate(outs, axis=0).T
    o_ref[...] = (o * _silu(z_ref[...].astype(F32))).astype(o_ref.dtype)


def _mem_attn(qsrc, q_col, kv, zsrc, z_col, batch, seq, n_mem, tq=512):
    nq = seq // tq
    return pl.pallas_call(
        _mem_attn_kernel,
        grid=(batch, nq),
        in_specs=[pl.BlockSpec((tq, MEM_WIDTH), lambda b, i: (b * nq + i, q_col)),
                  pl.BlockSpec((n_mem, MEM_WIDTH), lambda b, i: (b, 0)),
                  pl.BlockSpec((n_mem, MEM_WIDTH), lambda b, i: (b, 1)),
                  pl.BlockSpec((tq, MEM_WIDTH), lambda b, i: (b * nq + i, z_col))],
        out_specs=pl.BlockSpec((tq, MEM_WIDTH), lambda b, i: (b * nq + i, 0)),
        out_shape=jax.ShapeDtypeStruct((batch * seq, MEM_WIDTH), BF16),
        scratch_shapes=[pltpu.VMEM((n_mem, MEM_WIDTH), BF16),
                        pltpu.VMEM((MEM_HEADS, MEM_HEAD_DIM + 16, n_mem), BF16)],
        compiler_params=_params("arbitrary", "arbitrary"),
        name="mem_attn",
    )(qsrc, kv, kv, zsrc)


def _hawk_out_kernel(*refs):
    n_groups = len(DIL_GROUPS)
    ya_ref = refs[0]
    o_refs = refs[1:1 + n_groups * DIL_HEADS]
    l_refs = refs[1 + n_groups * DIL_HEADS:1 + n_groups * (DIL_HEADS + 1)]
    zb_ref, ym_ref, w_ref, x_ref, out_ref = refs[1 + n_groups * (DIL_HEADS + 1):]
    a_end = LRU_WIDTH
    b_end = a_end + DIL_WIDTH
    y = jnp.dot(ya_ref[...], w_ref[0:a_end, :], preferred_element_type=F32)
    y = y + jnp.dot(ym_ref[...], w_ref[b_end:b_end + MEM_WIDTH, :], preferred_element_type=F32)
    parts = []
    for h in range(DIL_HEADS):
        ls = [l[:, h * LSE_LANES:h * LSE_LANES + 1] for l in l_refs]
        m = functools.reduce(jnp.maximum, ls)
        ws = [jnp.exp(l - m) for l in ls]
        num = sum(w * o_refs[gi * DIL_HEADS + h][...].astype(F32) for gi, w in enumerate(ws))
        parts.append(num / sum(ws))
    yb = (jnp.concatenate(parts, axis=1) * _silu(zb_ref[...].astype(F32))).astype(BF16)
    y = y + jnp.dot(yb, w_ref[a_end:b_end, :], preferred_element_type=F32)
    out_ref[...] = x_ref[...] + y


def _hawk_out(ya, os_, ls_, zb_src, zb_col, ym, w, x, tm=512):
    m, d = x.shape
    row = lambda width, col=0: pl.BlockSpec((tm, width), lambda i: (i, col))
    heads = [o for group in os_ for o in group]
    return pl.pallas_call(
        _hawk_out_kernel,
        grid=(m // tm,),
        in_specs=[row(LRU_WIDTH)] + [row(DIL_HEAD_DIM)] * len(heads) + [row(LANES)] * len(ls_)
                 + [row(DIL_WIDTH, zb_col), row(MEM_WIDTH),
                    pl.BlockSpec(w.shape, lambda i: (0, 0)), row(d)],
        out_specs=row(d),
        out_shape=jax.ShapeDtypeStruct((m, d), F32),
        compiler_params=_params("arbitrary"),
        name="hawk_out",
    )(ya, *heads, *ls_, zb_src, ym, w, x)


def _compress_kernel(k_ref, v_ref, pe_ref, w1_ref, w2k_ref, w2vt_ref, ko_ref, vto_ref):
    n_blk = k_ref.shape[0] // CMP_STRIDE

    def hidden(which, src_ref):
        x = jnp.concatenate([src_ref[pl.ds(p, n_blk, stride=CMP_STRIDE), :] for p in range(CMP_STRIDE)],
                            axis=1).astype(BF16)
        first = jnp.dot(x, w1_ref[which, 0], preferred_element_type=F32)
        second = jnp.dot(x, w1_ref[which, 1], preferred_element_type=F32)
        pe = (jnp.dot(pe_ref[which, 0], w1_ref[which, 0], preferred_element_type=F32)
              + jnp.dot(pe_ref[which, 1], w1_ref[which, 1], preferred_element_type=F32))
        return _silu(first + pltpu.roll(second, n_blk - 1, axis=0) + pe[0:1, :]).astype(BF16)

    act_k, act_v = hidden(0, k_ref), hidden(1, v_ref)
    part = lambda a, g: a[:, g * PHI_HIDDEN:(g + 1) * PHI_HIDDEN]
    ks = [jnp.dot(part(act_k, g), w2k_ref[...], preferred_element_type=F32) for g in range(NSA_KV_GROUPS)]
    vts = [_dot_t(w2vt_ref[...], part(act_v, g)) for g in range(NSA_KV_GROUPS)]
    ko_ref[...] = jnp.concatenate(ks, axis=1).astype(ko_ref.dtype)
    vto_ref[...] = jnp.concatenate(vts, axis=0).astype(vto_ref.dtype)


def _compress(src, k_col, v_col, pe_k, pe_v, k_w1, k_w2, v_w1, v_w2, batch, seq):
    half = CMP_BLOCK // 2
    assert half == CMP_STRIDE and NSA_KV == LANES
    n_blk = seq // CMP_STRIDE
    hd = NSA_HEAD_DIM
    w1 = jnp.stack([k_w1, v_w1]).reshape(2, 2, half, 1, hd, 1, PHI_HIDDEN)
    eye = jnp.eye(NSA_KV_GROUPS, dtype=w1.dtype)[None, None, None, :, None, :, None]
    w1e = (w1 * eye).reshape(2, 2, half * NSA_KV, NSA_KV_GROUPS * PHI_HIDDEN).astype(BF16)
    pe = jnp.stack([pe_k, pe_v]).reshape(2, 2, half, 1, hd)
    pe = jnp.broadcast_to(pe, (2, 2, half, NSA_KV_GROUPS, hd)).reshape(2, 2, 1, half * NSA_KV)
    pe = jnp.broadcast_to(pe, (2, 2, SUBLANES, half * NSA_KV)).astype(BF16)
    w2k = k_w2.astype(BF16)
    w2vt = v_w2.T.astype(BF16)
    whole = lambda a: pl.BlockSpec(a.shape, lambda b: (0,) * a.ndim)
    return pl.pallas_call(
        _compress_kernel,
        grid=(batch,),
        in_specs=[pl.BlockSpec((seq, LANES), lambda b: (b, k_col)),
                  pl.BlockSpec((seq, LANES), lambda b: (b, v_col)),
                  whole(pe), whole(w1e), whole(w2k), whole(w2vt)],
        out_specs=[pl.BlockSpec((None, n_blk, NSA_KV), lambda b: (b, 0, 0)),
                   pl.BlockSpec((None, NSA_KV, n_blk), lambda b: (b, 0, 0))],
        out_shape=[jax.ShapeDtypeStruct((batch, n_blk, NSA_KV), BF16),
                   jax.ShapeDtypeStruct((batch, NSA_KV, n_blk), BF16)],
        compiler_params=_params("arbitrary"),
        name="compress",
    )(src, src, pe, w1e, w2k, w2vt)


KEY_CHUNK = 256
NSA_TQ = 256
SLOPE_PIECES = 3
N_FEATS = 2 * SLOPE_PIECES
FEAT_LANES = 32
KEY_COLS = NSA_HEAD_DIM + 2 * FEAT_LANES
N_SLC = 32


def _slope_pieces(slope):
    rest = np.float32(slope)
    pieces = []
    for _ in range(SLOPE_PIECES):
        p = np.float32(np.asarray(rest).astype(BF16))
        pieces.append(float(p))
        rest = np.float32(rest - p)
    return pieces


def _lane_table(lane, values):
    out = jnp.zeros(lane.shape, F32)
    for idx, v in enumerate(values):
        out = jnp.where(lane == idx, v, out)
    return out


def _key_feats(pos_hi, pos_lo, lane):
    return jnp.where(lane < SLOPE_PIECES, pos_hi, jnp.where(lane < N_FEATS, pos_lo, 0)).astype(F32)


def _tile_heads(x):
    return jnp.concatenate([x] * NSA_R, axis=1)


def _nsa_kernel(q_ref, kc_ref, vct_ref, ksrc_ref, vsrc_ref, kwsrc_ref, vwsrc_ref, feat_ref, hot_ref,
                gl_ref, z_ref, o_ref, ks_ref, vst_ref, kw_ref, vwt_ref, s_ref, acc_ref, imp_ref):
    i = pl.program_id(1)
    tq = q_ref.shape[0]
    hd = NSA_HEAD_DIM
    n_cmp = kc_ref.shape[0]

    @pl.when(i == 0)
    def _():
        for g in range(NSA_KV_GROUPS):
            gs = slice(g * hd, (g + 1) * hd)
            for dst, src, tail in ((ks_ref, ksrc_ref, hot_ref[...]),
                                   (kw_ref, kwsrc_ref, jnp.zeros(hot_ref.shape, BF16))):
                dst[:, g * KEY_COLS:g * KEY_COLS + hd] = src[:, gs]
                dst[:, g * KEY_COLS + hd:g * KEY_COLS + hd + FEAT_LANES] = feat_ref[...]
                dst[:, g * KEY_COLS + hd + FEAT_LANES:(g + 1) * KEY_COLS] = tail
        for c in range(vst_ref.shape[0]):
            rows = slice(c * KEY_CHUNK, (c + 1) * KEY_CHUNK)
            vst_ref[c] = vsrc_ref[rows, :].astype(F32).T.astype(BF16)
            vwt_ref[c] = vwsrc_ref[rows, :].astype(F32).T.astype(BF16)

    slopes_all = _alibi_slopes(NSA_HEADS)
    gates_t = jax.nn.sigmoid(gl_ref[...]).T
    feat_lane = lax.broadcasted_iota(jnp.int32, (tq, FEAT_LANES), 1)
    no_sel = jnp.zeros((NSA_R * tq, FEAT_LANES), BF16)
    key_row = lax.broadcasted_iota(jnp.int32, (KEY_CHUNK, tq), 0)
    t_pos = i * tq + lax.broadcasted_iota(jnp.int32, (KEY_CHUNK, tq), 1)
    ones_rows = jnp.ones((16, KEY_CHUNK), BF16)
    win_lo = jnp.maximum(i * tq - (WIN_SIZE - 1), 0) // KEY_CHUNK
    chunks_hi = (i * tq + tq - 1) // KEY_CHUNK + 1

    groups = range(NSA_KV_GROUPS)
    q_win, q_slc, o_cmp = [], [], []
    for g in groups:
        slopes = slopes_all[g * NSA_R:(g + 1) * NSA_R]
        gs = slice(g * hd, (g + 1) * hd)
        q_parts = []
        for r in range(NSA_R):
            qr = q_ref[:, (g * NSA_R + r) * hd:(g * NSA_R + r + 1) * hd].astype(F32) * (hd ** -0.5)
            feats = _lane_table(feat_lane, _slope_pieces(slopes[r]) * 2)
            q_parts.append(jnp.concatenate([qr, feats], axis=1).astype(BF16))
        q_feat = jnp.concatenate(q_parts, axis=0)
        q_aug = jnp.concatenate([q_feat, no_sel], axis=1)

        n_row = lax.broadcasted_iota(jnp.int32, (n_cmp, tq), 0)
        t_cmp = i * tq + lax.broadcasted_iota(jnp.int32, (n_cmp, tq), 1)
        visible = t_cmp >= n_row * CMP_STRIDE + (CMP_BLOCK - 1)
        cfeat_row = lax.broadcasted_iota(jnp.int32, (n_cmp, 2 * FEAT_LANES), 0)
        cfeat_lane = lax.broadcasted_iota(jnp.int32, (n_cmp, 2 * FEAT_LANES), 1)
        kc_feats = _key_feats(cfeat_row * CMP_STRIDE, 0, cfeat_lane)
        kc_aug = jnp.concatenate([kc_ref[:, gs], kc_feats.astype(BF16)], axis=1)
        s = _dot_t(kc_aug, q_aug) + _tile_heads(jnp.where(visible, 0.0, NEG_INF))
        m = jnp.max(s, axis=0, keepdims=True)
        e = jnp.exp(s - m)
        t_one = i * tq + lax.broadcasted_iota(jnp.int32, (1, NSA_R * tq), 1) % tq
        any_visible = t_one >= (CMP_BLOCK - 1)
        p = e * jnp.where(any_visible, 1.0 / jnp.sum(e, axis=0, keepdims=True), 0.0)
        o_cmp.append(jnp.dot(vct_ref[gs, :], p.astype(BF16), preferred_element_type=F32))
        p_sum = p[:, 0:tq]
        for r in range(1, NSA_R):
            p_sum = p_sum + p[:, r * tq:(r + 1) * tq]

        band = p_sum + pltpu.roll(p_sum, 1, axis=0)
        for k in range(1, CMP_PER_SLC):
            band = band + pltpu.roll(p_sum, n_cmp - k, axis=0)
        halves = []
        for half in range(tq // LANES):
            imp_ref[...] = band[:, half * LANES:(half + 1) * LANES]
            halves.append(imp_ref[pl.ds(0, N_SLC, stride=CMP_PER_SLC), :])
        imp = jnp.concatenate(halves, axis=1)
        blk_j = lax.broadcasted_iota(jnp.int32, (N_SLC, tq), 0)
        cur = (i * tq + lax.broadcasted_iota(jnp.int32, (N_SLC, tq), 1)) // SLC_BLOCK
        forced = (blk_j == 0) | (blk_j == cur) | (blk_j == cur - 1)
        v_imp = jnp.where(forced, SEL_FORCE, jnp.where(blk_j > cur, -SEL_FORCE, imp))
        rank = jnp.zeros((N_SLC, tq), F32)
        for other in range(N_SLC):
            row = v_imp[other:other + 1, :]
            ahead = (row > v_imp) | ((row == v_imp) & (blk_j > other))
            rank = rank + jnp.where(ahead, 1.0, 0.0)
        sel_bias = jnp.where(rank < SLC_TOP_N, 0.0, NEG_INF)

        padded = jnp.concatenate([sel_bias, jnp.zeros((LANES - N_SLC, tq), F32)], axis=0)
        sel_t = padded.T[:, 0:FEAT_LANES].astype(BF16)
        q_win.append(q_aug)
        q_slc.append(jnp.concatenate([q_feat, jnp.concatenate([sel_t] * NSA_R, axis=0)], axis=1))

    def attend(q_brs, k_ref, key_cols, vt_ref, lo, hi, masked_from, mask_fn):
        def scores(c, m_run, masked):
            start = pl.multiple_of(c * KEY_CHUNK, KEY_CHUNK)
            scs = [_dot_t(k_ref[pl.ds(start, KEY_CHUNK), g * key_cols:(g + 1) * key_cols], q_brs[g])
                   for g in groups]
            if masked:
                bias = _tile_heads(jnp.where(mask_fn(t_pos - (start + key_row)), 0.0, NEG_INF))
                scs = [sc + bias for sc in scs]
            for g in groups:
                s_ref[g, c] = scs[g]
            return tuple(jnp.maximum(m_run[g], jnp.max(scs[g], axis=0, keepdims=True)) for g in groups)

        m_rows = tuple(jnp.full((1, NSA_R * tq), NEG_INF, F32) for _ in groups)
        m_rows = lax.fori_loop(lo, masked_from, functools.partial(scores, masked=False), m_rows)
        m_rows = lax.fori_loop(masked_from, hi, functools.partial(scores, masked=True), m_rows)
        acc_ref[...] = jnp.zeros(acc_ref.shape, F32)

        def weighted(c, carry):
            for g in groups:
                e = jnp.exp(s_ref[g, c] - m_rows[g]).astype(BF16)
                v_ext = jnp.concatenate([vt_ref[c, g * hd:(g + 1) * hd, :], ones_rows], axis=0)
                acc_ref[g] += jnp.dot(v_ext, e, preferred_element_type=F32)
            return carry

        lax.fori_loop(lo, hi, weighted, 0)
        return [acc_ref[g, 0:hd, :] / acc_ref[g, hd:hd + 1, :] for g in groups]

    o_slc = attend(q_slc, ks_ref, KEY_COLS, vst_ref, 0, chunks_hi, (i * tq) // KEY_CHUNK,
                   lambda dist: dist >= 0)
    o_win = attend(q_win, kw_ref, KEY_COLS, vwt_ref, win_lo, chunks_hi, win_lo,
                   lambda dist: (dist >= 0) & (dist <= WIN_SIZE - 1))

    for g in groups:
        def gate(kind):
            base = g * NSA_R * 3 + kind
            return jnp.concatenate([gates_t[base + 3 * r:base + 3 * r + 1, :] for r in range(NSA_R)], axis=1)

        o = gate(0) * o_cmp[g] + gate(1) * o_slc[g] + gate(2) * o_win[g]
        pairs = []
        for r in range(0, NSA_R, 2):
            two = jnp.concatenate([o[:, r * tq:(r + 1) * tq], o[:, (r + 1) * tq:(r + 2) * tq]], axis=0)
            pairs.append(two.T)
        cs = slice(g * NSA_R * hd, (g + 1) * NSA_R * hd)
        o_ref[:, cs] = (jnp.concatenate(pairs, axis=1) * _silu(z_ref[:, cs].astype(F32))).astype(o_ref.dtype)


def _key_pos_feats(seq):
    assert seq // SLC_BLOCK == N_SLC <= FEAT_LANES
    pos = np.arange(seq)
    feats = np.zeros((seq, FEAT_LANES), np.float32)
    feats[:, 0:SLOPE_PIECES] = ((pos // SLC_BLOCK) * SLC_BLOCK)[:, None]
    feats[:, SLOPE_PIECES:N_FEATS] = (pos % SLC_BLOCK)[:, None]
    onehot = (np.arange(FEAT_LANES)[None, :] == (pos // SLC_BLOCK)[:, None]).astype(np.float32)
    return jnp.asarray(feats, BF16), jnp.asarray(onehot, BF16)


def _nsa_attn(nb, kv_col0, z_col, k_cmp, v_cmp_t, nf, gl_col, batch, seq):
    tq = NSA_TQ
    nq = seq // tq
    feats, onehot = _key_pos_feats(seq)
    kv_blk = kv_col0 // NSA_KV
    seq_cols = lambda col: pl.BlockSpec((seq, NSA_KV), lambda b, i: (b, kv_blk + col))
    const = lambda a: pl.BlockSpec(a.shape, lambda b, i: (0,) * a.ndim)
    per_batch = lambda a: pl.BlockSpec((None,) + a.shape[1:], lambda b, i: (b,) + (0,) * (a.ndim - 1))
    return pl.pallas_call(
        _nsa_kernel,
        grid=(batch, nq),
        in_specs=[pl.BlockSpec((tq, NSA_WIDTH), lambda b, i: (b * nq + i, 0)),
                  per_batch(k_cmp), per_batch(v_cmp_t),
                  seq_cols(0), seq_cols(1), seq_cols(2), seq_cols(3), const(feats), const(onehot),
                  pl.BlockSpec((tq, LANES), lambda b, i: (b * nq + i, gl_col)),
                  pl.BlockSpec((tq, NSA_WIDTH), lambda b, i: (b * nq + i, z_col))],
        out_specs=pl.BlockSpec((tq, NSA_WIDTH), lambda b, i: (b * nq + i, 0)),
        out_shape=jax.ShapeDtypeStruct((batch * seq, NSA_WIDTH), BF16),
        scratch_shapes=[pltpu.VMEM((seq, NSA_KV_GROUPS * KEY_COLS), BF16),
                        pltpu.VMEM((seq // KEY_CHUNK, NSA_KV, KEY_CHUNK), BF16),
                        pltpu.VMEM((seq, NSA_KV_GROUPS * KEY_COLS), BF16),
                        pltpu.VMEM((seq // KEY_CHUNK, NSA_KV, KEY_CHUNK), BF16),
                        pltpu.VMEM((NSA_KV_GROUPS, seq // KEY_CHUNK, KEY_CHUNK, NSA_R * tq), F32),
                        pltpu.VMEM((NSA_KV_GROUPS, NSA_HEAD_DIM + 16, NSA_R * tq), F32),
                        pltpu.VMEM((seq // CMP_STRIDE, LANES), F32)],
        compiler_params=_params("arbitrary", "arbitrary"),
        name="nsa_attn",
    )(nb, k_cmp, v_cmp_t, nb, nb, nb, nb, feats, onehot, nf, nb)


def _nsa_out_kernel(yo_ref, ym_ref, w_ref, x_ref, g_ref, out_ref):
    y = jnp.dot(yo_ref[...], w_ref[0:NSA_WIDTH, :], preferred_element_type=F32)
    y = y + jnp.dot(ym_ref[...], w_ref[NSA_WIDTH:NSA_WIDTH + MEM_WIDTH, :], preferred_element_type=F32)
    x = x_ref[...] + y
    ms = jnp.mean(x * x, axis=-1, keepdims=True)
    out_ref[...] = x * lax.rsqrt(ms + NORM_EPS) * g_ref[...]


def _nsa_out(yo, ym, w, x, g, tm=512):
    m, d = x.shape
    row = lambda width: pl.BlockSpec((tm, width), lambda i: (i, 0))
    return pl.pallas_call(
        _nsa_out_kernel,
        grid=(m // tm,),
        in_specs=[row(NSA_WIDTH), row(MEM_WIDTH), pl.BlockSpec(w.shape, lambda i: (0, 0)), row(d),
                  pl.BlockSpec((1, d), lambda i: (0, 0))],
        out_specs=row(d),
        out_shape=jax.ShapeDtypeStruct((m, d), F32),
        compiler_params=_params("arbitrary"),
        name="nsa_out",
    )(yo, ym, w, x, g.reshape(1, d))


def _cols(w, *ranges):
    return jnp.concatenate([w[:, a:b] for a, b in ranges], axis=1).astype(BF16)


def _hawk_layer(x, mem, batch, seq, norm_g, w_in, conv_w, conv_b, ga_w, ga_b, gx_w, gx_b, lam,
                mem_norm_g, w_mem_kv, w_out):
    xa0, za0 = 0, LRU_WIDTH
    q0 = 2 * LRU_WIDTH
    k0, v0 = q0 + DIL_QKV, q0 + 2 * DIL_QKV
    zb0 = q0 + 3 * DIL_QKV
    qm0 = zb0 + DIL_WIDTH
    zm0 = qm0 + MEM_WIDTH

    def qkv_cols(gi):
        return [(base + gi * DIL_WIDTH, base + (gi + 1) * DIL_WIDTH) for base in (q0, k0, v0)]

    w_nat = _cols(w_in, (xa0, za0), (za0, q0), *qkv_cols(0), (zb0, qm0), (qm0, zm0), (zm0, zm0 + MEM_WIDTH))
    nat_width = w_nat.shape[1] - LRU_WIDTH
    xa, hb = _norm_matmul(x, norm_g, w_nat, [(LRU_WIDTH, F32), (nat_width, BF16)])
    za_col = 0
    qkv0_col = LRU_WIDTH // DIL_WIDTH
    zb_col = (LRU_WIDTH + 3 * DIL_WIDTH) // DIL_WIDTH
    qm_col = (LRU_WIDTH + 4 * DIL_WIDTH) // MEM_WIDTH
    zm_col = qm_col + 1
    qkv = [(hb[None], qkv0_col)]
    for gi in range(1, len(DIL_GROUPS)):
        qkv.append((_norm_matmul(x, norm_g, _cols(w_in, *qkv_cols(gi)), [(3 * DIL_WIDTH, BF16)],
                                 dil=DIL_GROUPS[gi][1]), 0))
    n_mem = mem.shape[0] // batch
    mem_kv, = _norm_matmul(mem, mem_norm_g, w_mem_kv.astype(BF16), [(2 * MEM_WIDTH, BF16)])

    ya = _rglru(xa, hb, za_col, conv_w, conv_b, _pack_block_diag(ga_w), ga_b, _pack_block_diag(gx_w), gx_b,
                lam, batch, seq)
    attn = [_dil_attn(arr, col0, gi, batch, seq) for gi, (arr, col0) in enumerate(qkv)]
    ym = _mem_attn(hb, qm_col, mem_kv, hb, zm_col, batch, seq, n_mem)
    return _hawk_out(ya, [o for o, _ in attn], [l for _, l in attn], hb, zb_col, ym,
                     w_out.astype(BF16), x)


def _nsa_layer(x, mem, batch, seq, norm_g, w_in, pe_k, pe_v, phik_w1, phik_w2, phiv_w1, phiv_w2,
               mem_norm_g, w_mem_kv, w_out, final_g):
    kv0 = NSA_WIDTH
    gl0 = kv0 + 6 * NSA_KV
    z0 = gl0 + 3 * NSA_HEADS
    qm0 = z0 + NSA_WIDTH
    zm0 = qm0 + MEM_WIDTH
    gl_w = jnp.pad(w_in[:, gl0:z0], ((0, 0), (0, LANES - 3 * NSA_HEADS)))
    w_all = jnp.concatenate([gl_w, w_in[:, kv0:kv0 + 2 * NSA_KV], w_in[:, 0:kv0], w_in[:, z0:qm0],
                             w_in[:, kv0 + 2 * NSA_KV:gl0], w_in[:, qm0:zm0 + MEM_WIDTH]], axis=1).astype(BF16)
    f32_width = LANES + 2 * NSA_KV
    nf, nb = _norm_matmul(x, norm_g, w_all, [(f32_width, F32), (w_all.shape[1] - f32_width, BF16)])
    gl_col, kc_col, vc_col = 0, 1, 2
    z_col = 1
    kv_col0 = 2 * NSA_WIDTH
    qm_col = (kv_col0 + 4 * NSA_KV) // MEM_WIDTH
    zm_col = qm_col + 1
    n_mem = mem.shape[0] // batch
    mem_kv, = _norm_matmul(mem, mem_norm_g, w_mem_kv.astype(BF16), [(2 * MEM_WIDTH, BF16)])

    k_cmp, v_cmp_t = _compress(nf, kc_col, vc_col, pe_k, pe_v, phik_w1, phik_w2, phiv_w1, phiv_w2,
                               batch, seq)
    yo = _nsa_attn(nb, kv_col0, z_col, k_cmp, v_cmp_t, nf, gl_col, batch, seq)
    ym = _mem_attn(nb, qm_col, mem_kv, nb, zm_col, batch, seq, n_mem)
    return _nsa_out(yo, ym, w_out.astype(BF16), x, final_g)


def kernel(x, mem, hawk_norm, hawk_w_in, hawk_conv_w, hawk_conv_b, hawk_gate_a_w, hawk_gate_a_b,
           hawk_gate_x_w, hawk_gate_x_b, hawk_lambda, hawk_mem_norm, hawk_w_mem_kv, hawk_w_out,
           nsa_norm, nsa_w_in, nsa_pe_k, nsa_pe_v, nsa_phi_k_w1, nsa_phi_k_w2, nsa_phi_v_w1,
           nsa_phi_v_w2, nsa_mem_norm, nsa_w_mem_kv, nsa_w_out, final_norm):
    batch, seq, d = x.shape
    assert hawk_norm.shape[0] == 1 and nsa_norm.shape[0] == 1, "one layer of each kind"
    assert seq % (ATTN_BLOCK * DIL_GROUPS[-1][1]) == 0
    x2 = x.reshape(batch * seq, d)
    mem2 = mem.reshape(batch * mem.shape[1], d)
    x2 = _hawk_layer(x2, mem2, batch, seq, hawk_norm[0], hawk_w_in[0], hawk_conv_w[0], hawk_conv_b[0],
                     hawk_gate_a_w[0], hawk_gate_a_b[0].reshape(-1), hawk_gate_x_w[0],
                     hawk_gate_x_b[0].reshape(-1), hawk_lambda[0], hawk_mem_norm[0], hawk_w_mem_kv[0],
                     hawk_w_out[0])
    out = _nsa_layer(x2, mem2, batch, seq, nsa_norm[0], nsa_w_in[0], nsa_pe_k[0], nsa_pe_v[0],
                     nsa_phi_k_w1[0], nsa_phi_k_w2[0], nsa_phi_v_w1[0], nsa_phi_v_w2[0],
                     nsa_mem_norm[0], nsa_w_mem_kv[0], nsa_w_out[0], final_norm)
    return out.reshape(batch, seq, d)
```

```python
import functools

import numpy as np
import jax
import jax.numpy as jnp
from jax import lax
from jax.experimental import pallas as pl
from jax.experimental.pallas import tpu as pltpu

F32 = jnp.float32
BF16 = jnp.bfloat16

NORM_EPS = 1e-6
NEG_INF = -1e30
LANES = 128
SUBLANES = 8
ATTN_BLOCK = 128
VMEM_LIMIT = 56 * 1024 * 1024

LRU_WIDTH = 1024
LRU_BLOCKS = 16
LRU_BLOCK_DIM = LRU_WIDTH // LRU_BLOCKS
LRU_PACK = 256
CONV_WIDTH = 4
LRU_C = 8.0

DIL_GROUPS = ((128, 1), (512, 4), (2048, 16))
DIL_HEADS = 4
DIL_HEAD_DIM = 128
DIL_WIDTH = DIL_HEADS * DIL_HEAD_DIM
DIL_QKV = len(DIL_GROUPS) * DIL_WIDTH
LSE_LANES = LANES // DIL_HEADS

MEM_HEADS = 4
MEM_HEAD_DIM = 64
MEM_WIDTH = MEM_HEADS * MEM_HEAD_DIM

NSA_HEADS = 16
NSA_KV_GROUPS = 2
NSA_R = NSA_HEADS // NSA_KV_GROUPS
NSA_HEAD_DIM = 64
NSA_WIDTH = NSA_HEADS * NSA_HEAD_DIM
NSA_KV = NSA_KV_GROUPS * NSA_HEAD_DIM
CMP_BLOCK = 32
CMP_STRIDE = 16
SLC_BLOCK = 64
SLC_TOP_N = 8
WIN_SIZE = 512
PHI_HIDDEN = 256
SEL_FORCE = 1e6
CMP_PER_SLC = SLC_BLOCK // CMP_STRIDE


def _alibi_slopes(n):
    return [float(v) for v in np.exp2(-8.0 * np.arange(1, n + 1) / n).astype(np.float32)]


def _params(*semantics):
    return pltpu.CompilerParams(dimension_semantics=semantics, vmem_limit_bytes=VMEM_LIMIT)


def _silu(z):
    return z * jax.nn.sigmoid(z)


def _dot_t(a, b):
    return lax.dot_general(a, b, (((1,), (1,)), ((), ())), preferred_element_type=F32)


def _rms_norm_rows(x, g):
    ms = jnp.mean(x * x, axis=-1, keepdims=True)
    return (x * lax.rsqrt(ms + NORM_EPS) * g).astype(BF16)


def _norm_matmul_kernel(*refs, dil):
    if dil == 1:
        x_ref, g_ref, w_ref, *o_refs = refs
        xn = _rms_norm_rows(x_ref[...], g_ref[...])
        col = 0
        for o_ref in o_refs:
            width = o_ref.shape[1]
            o_ref[...] = jnp.dot(xn, w_ref[:, col:col + width], preferred_element_type=F32).astype(o_ref.dtype)
            col += width
        return
    if dil % SUBLANES == 0:
        x_ref, g_ref, w_ref, o_ref = refs
        tm, k = x_ref.shape
        x = x_ref[...]
        xn = x * lax.rsqrt(jnp.mean(x * x, axis=-1, keepdims=True) + NORM_EPS) * g_ref[...]
        xn = jnp.swapaxes(xn.reshape(tm // dil, dil, k), 0, 1).reshape(tm, k).astype(BF16)
        res = jnp.dot(xn, w_ref[...], preferred_element_type=F32).astype(o_ref.dtype)
        for c in range(dil):
            o_ref[c] = res[c * (tm // dil):(c + 1) * (tm // dil)]
        return
    *x_refs, g_ref, w_ref, o_ref, xn_ref = refs
    per = x_refs[0].shape[0] // dil
    inv_k = 1.0 / (len(x_refs) * LANES)
    for c in range(dil):
        xs = [x_ref[pl.ds(c, per, stride=dil), :] for x_ref in x_refs]
        ss = xs[0] * xs[0]
        for xj in xs[1:]:
            ss = ss + xj * xj
        r = lax.rsqrt(jnp.sum(ss, axis=-1, keepdims=True) * inv_k + NORM_EPS)
        for j, xj in enumerate(xs):
            cols = slice(j * LANES, (j + 1) * LANES)
            xn_ref[c * per:(c + 1) * per, cols] = (xj * r * g_ref[:, cols]).astype(BF16)
    res = jnp.dot(xn_ref[...], w_ref[...], preferred_element_type=F32).astype(o_ref.dtype)
    for c in range(dil):
        o_ref[c] = res[c * per:(c + 1) * per]


def _norm_matmul(x, g, w, outs, dil=1, tm=512):
    m, k = x.shape
    n = w.shape[1]
    tm = min(tm, m)
    assert m % tm == 0 and k % LANES == 0 and sum(width for width, _ in outs) == n
    resident = [pl.BlockSpec((1, k), lambda i: (0, 0)), pl.BlockSpec((k, n), lambda i: (0, 0))]
    if dil == 1:
        return pl.pallas_call(
            functools.partial(_norm_matmul_kernel, dil=1),
            grid=(m // tm,),
            in_specs=[pl.BlockSpec((tm, k), lambda i: (i, 0))] + resident,
            out_specs=[pl.BlockSpec((tm, width), lambda i: (i, 0)) for width, _ in outs],
            out_shape=[jax.ShapeDtypeStruct((m, width), dtype) for width, dtype in outs],
            compiler_params=_params("arbitrary"),
            name="norm_matmul",
        )(x, g.reshape(1, k), w)
    per = tm // dil
    (_, out_dtype), = outs
    assert tm % dil == 0 and per % 16 == 0
    if dil % SUBLANES == 0:
        x_specs, scratch = [pl.BlockSpec((tm, k), lambda i: (i, 0))], []
    else:
        x_specs = [pl.BlockSpec((tm, LANES), functools.partial(lambda i, j: (i, j), j=j)) for j in range(k // LANES)]
        scratch = [pltpu.VMEM((tm, k), BF16)]
    return pl.pallas_call(
        functools.partial(_norm_matmul_kernel, dil=dil),
        grid=(m // tm,),
        in_specs=x_specs + resident,
        out_specs=pl.BlockSpec((dil, per, n), lambda i: (0, i, 0)),
        out_shape=jax.ShapeDtypeStruct((dil, m // dil, n), out_dtype),
        scratch_shapes=scratch,
        compiler_params=_params("arbitrary"),
        name="norm_matmul_dil",
    )(*([x] * len(x_specs)), g.reshape(1, k), w)


def _rglru_kernel(xa_ref, za_ref, cw_ref, cb_ref, wa_ref, ba_ref, wx_ref, bx_ref, lam_ref,
                  o_ref, xpad_ref, h_ref):
    t = pl.program_id(1)
    tt, width = xa_ref.shape
    halo = 8

    @pl.when(t == 0)
    def _():
        xpad_ref[0:halo, :] = jnp.zeros((halo, width), F32)
        h_ref[...] = jnp.zeros_like(h_ref)

    x = xa_ref[...]
    xpad_ref[halo:halo + tt, :] = x
    cw = cw_ref[...]
    y = cw[CONV_WIDTH - 1:CONV_WIDTH] * x
    for k in range(1, CONV_WIDTH):
        y = y + cw[CONV_WIDTH - 1 - k:CONV_WIDTH - k] * xpad_ref[halo - k:halo - k + tt, :]
    y = y + cb_ref[...]
    xpad_ref[0:halo, :] = x[tt - halo:tt, :]

    yb = y.astype(BF16)
    r_parts, i_parts = [], []
    for p in range(width // LRU_PACK):
        ys = yb[:, p * LRU_PACK:(p + 1) * LRU_PACK]
        r_parts.append(jnp.dot(ys, wa_ref[p], preferred_element_type=F32))
        i_parts.append(jnp.dot(ys, wx_ref[p], preferred_element_type=F32))
    r = jax.nn.sigmoid(jnp.concatenate(r_parts, axis=1) + ba_ref[...])
    gi = jax.nn.sigmoid(jnp.concatenate(i_parts, axis=1) + bx_ref[...])

    nl = -lam_ref[...]
    softplus = jnp.maximum(nl, 0.0) + jnp.log1p(jnp.exp(-jnp.abs(nl)))
    log_a = (-LRU_C) * r * softplus
    a = jnp.exp(log_a)
    z2 = 2.0 * log_a
    u = a * a
    one_minus = jnp.where(u == 1.0, -z2, jnp.where(u < 0.5, 1.0 - u, (1.0 - u) * z2 / jnp.log(u)))
    mult = jnp.sqrt(one_minus)
    b = y * gi * mult
    first = (lax.broadcasted_iota(jnp.int32, (SUBLANES, width), 0) == 0) & (t == 0)
    b = jnp.concatenate([jnp.where(first, (y * gi)[0:SUBLANES], b[0:SUBLANES]), b[SUBLANES:]], axis=0)

    groups = tt // SUBLANES
    a3 = a.reshape(groups, SUBLANES, width)
    b3 = b.reshape(groups, SUBLANES, width)
    sub = lax.broadcasted_iota(jnp.int32, (groups, SUBLANES, width), 1)
    k = 1
    while k < SUBLANES:
        keep = sub >= k
        a_sh = jnp.where(keep, pltpu.roll(a3, k, axis=1), 1.0)
        b_sh = jnp.where(keep, pltpu.roll(b3, k, axis=1), 0.0)
        b3 = a3 * b_sh + b3
        a3 = a3 * a_sh
        k *= 2
    carry = jnp.broadcast_to(h_ref[...], (SUBLANES, width))
    hs = []
    for gidx in range(groups):
        hg = a3[gidx] * carry + b3[gidx]
        hs.append(hg)
        carry = jnp.broadcast_to(hg[SUBLANES - 1:SUBLANES], (SUBLANES, width))
    h = jnp.concatenate(hs, axis=0)
    h_ref[...] = carry[0:1]
    o_ref[...] = (h * _silu(za_ref[...].astype(F32))).astype(o_ref.dtype)


def _rglru(xa, za_src, za_col, conv_w, conv_b, wa, ba, wx, bx, lam, batch, seq, tt=512):
    width = LRU_WIDTH
    nt = seq // tt
    packs = width // LRU_PACK
    vec = pl.BlockSpec((1, width), lambda b, t: (0, 0))
    gate_w = pl.BlockSpec((packs, LRU_PACK, LRU_PACK), lambda b, t: (0, 0, 0))
    return pl.pallas_call(
        _rglru_kernel,
        grid=(batch, nt),
        in_specs=[pl.BlockSpec((tt, width), lambda b, t: (b * nt + t, 0)),
                  pl.BlockSpec((tt, width), lambda b, t: (b * nt + t, za_col)),
                  pl.BlockSpec((CONV_WIDTH, width), lambda b, t: (0, 0)),
                  vec, gate_w, vec, gate_w, vec, vec],
        out_specs=pl.BlockSpec((tt, width), lambda b, t: (b * nt + t, 0)),
        out_shape=jax.ShapeDtypeStruct((batch * seq, width), BF16),
        scratch_shapes=[pltpu.VMEM((tt + 8, width), F32), pltpu.VMEM((1, width), F32)],
        compiler_params=_params("arbitrary", "arbitrary"),
        name="rglru",
    )(xa, za_src, conv_w, conv_b.reshape(1, width), wa, ba.reshape(1, width), wx, bx.reshape(1, width),
      lam.reshape(1, width))


def _pack_block_diag(w):
    per = LRU_PACK // LRU_BLOCK_DIM
    w = w.reshape(LRU_BLOCKS // per, per, LRU_BLOCK_DIM, LRU_BLOCK_DIM)
    eye = jnp.eye(per, dtype=w.dtype)
    packed = w[:, :, :, None, :] * eye[None, :, None, :, None]
    return packed.reshape(LRU_BLOCKS // per, LRU_PACK, LRU_PACK).astype(BF16)


def _dil_attn_kernel(*refs, slopes, pos_scale, max_dist, has_halo, dil, n_cls, n_blk):
    if has_halo:
        q_ref, kh_ref, k_ref, vh_ref, v_ref = refs[:5]
        out_refs = refs[5:]
    else:
        q_ref, k_ref, v_ref = refs[:3]
        out_refs = refs[3:]
    n_out = DIL_HEADS + 1
    dst_refs = out_refs[:n_out]
    stage_refs = out_refs[n_out:] if dil > 1 else dst_refs
    first_super = pl.program_id(1) == 0
    cls0 = pl.program_id(2) * n_cls
    blk = ATTN_BLOCK
    scale = DIL_HEAD_DIM ** -0.5

    def band(width, halo_live):
        row = lax.broadcasted_iota(jnp.int32, (blk, width), 0)
        col = lax.broadcasted_iota(jnp.int32, (blk, width), 1)
        dist = (width - blk) + row - col
        valid = (dist >= 0) & (dist <= max_dist)
        if halo_live is not None:
            valid = valid & ((col >= blk) | halo_live)
        distf = (dist * pos_scale).astype(F32)
        return [jnp.where(valid, -slope * distf, NEG_INF) for slope in slopes]

    bias_inner = band(2 * blk, None) if n_blk > 1 else None
    bias_first = band(2 * blk, jnp.logical_not(first_super)) if has_halo else band(blk, None)

    def scores(cc, jb):
        cur = slice(jb * blk, (jb + 1) * blk)
        bias = bias_inner if jb > 0 else bias_first
        out = []
        for h in range(DIL_HEADS):
            hs = slice(h * DIL_HEAD_DIM, (h + 1) * DIL_HEAD_DIM)
            q = q_ref[cc, cur, hs]
            if jb > 0:
                k = k_ref[cc, (jb - 1) * blk:(jb + 1) * blk, hs]
                v = v_ref[cc, (jb - 1) * blk:(jb + 1) * blk, hs]
            elif has_halo:
                k = jnp.concatenate([kh_ref[cc, :, hs], k_ref[cc, cur, hs]], axis=0)
                v = jnp.concatenate([vh_ref[cc, :, hs], v_ref[cc, cur, hs]], axis=0)
            else:
                k, v = k_ref[cc, cur, hs], v_ref[cc, cur, hs]
            out.append((_dot_t(q, k) * scale + bias[h], v))
        return out

    def finish(cc, jb, pairs):
        where = (cls0 + cc, slice(jb * blk, (jb + 1) * blk)) if dil > 1 else (slice(jb * blk, (jb + 1) * blk),)
        lses = []
        for h, (s, v) in enumerate(pairs):
            m = jnp.max(s, axis=-1, keepdims=True)
            e = jnp.exp(s - m)
            den = jnp.sum(e, axis=-1, keepdims=True)
            o = jnp.dot(e.astype(BF16), v, preferred_element_type=F32) / den
            stage_refs[h][where] = o.astype(stage_refs[h].dtype)
            lses.append(jnp.broadcast_to(m + jnp.log(den), (blk, LSE_LANES)))
        stage_refs[DIL_HEADS][where] = jnp.concatenate(lses, axis=1)

    pending = None
    for cc in range(n_cls):
        for jb in range(n_blk):
            pairs = scores(cc, jb)
            if pending is not None:
                finish(*pending)
            pending = (cc, jb, pairs)
    finish(*pending)

    if dil > 1:
        @pl.when(pl.program_id(2) == pl.num_programs(2) - 1)
        def _():
            for stage, dst in zip(stage_refs, dst_refs):
                dst[...] = jnp.swapaxes(stage[...], 0, 1).reshape(dst.shape).astype(dst.dtype)


def _dil_attn(qkv, col0, gi, batch, seq, work=4):
    window, dil = DIL_GROUPS[gi]
    sub = seq // dil
    nb = sub // ATTN_BLOCK
    n_blk = min(work, nb)
    n_cls = min(work // n_blk, dil)
    n_super = nb // n_blk
    has_halo = n_super > 1
    span = n_blk * ATTN_BLOCK
    slopes = _alibi_slopes(len(DIL_GROUPS) * DIL_HEADS)[gi * DIL_HEADS:(gi + 1) * DIL_HEADS]
    cur = lambda col: pl.BlockSpec((n_cls, span, DIL_WIDTH), lambda b, i, c: (c, b * n_super + i, col0 + col))
    halo = lambda col: pl.BlockSpec(
        (n_cls, ATTN_BLOCK, DIL_WIDTH),
        lambda b, i, c: (c, jnp.maximum((b * n_super + i) * n_blk - 1, 0), col0 + col))
    if has_halo:
        in_specs = [cur(0), halo(1), cur(1), halo(2), cur(2)]
    else:
        in_specs = [cur(0), cur(1), cur(2)]
    n_out = DIL_HEADS + 1
    *o, lse = pl.pallas_call(
        functools.partial(_dil_attn_kernel, slopes=slopes, pos_scale=dil, max_dist=window // dil,
                          has_halo=has_halo, dil=dil, n_cls=n_cls, n_blk=n_blk),
        grid=(batch, n_super, dil // n_cls),
        in_specs=in_specs,
        out_specs=[pl.BlockSpec((span * dil, LANES), lambda b, i, c: (b * n_super + i, 0))] * n_out,
        out_shape=[jax.ShapeDtypeStruct((batch * seq, LANES), BF16)] * DIL_HEADS
                  + [jax.ShapeDtypeStruct((batch * seq, LANES), F32)],
        scratch_shapes=[pltpu.VMEM((dil, span, LANES), F32)] * (n_out if dil > 1 else 0),
        compiler_params=_params("arbitrary", "arbitrary", "arbitrary"),
        name=f"dil_attn_d{dil}",
    )(*([qkv] * len(in_specs)))
    return o, lse


def _mem_attn_kernel(q_ref, k_ref, v_ref, z_ref, o_ref, ks_ref, vt_ref):
    hd = MEM_HEAD_DIM
    n_mem = k_ref.shape[0]

    @pl.when(pl.program_id(1) == 0)
    def _():
        ks_ref[...] = (k_ref[...].astype(F32) * (hd ** -0.5)).astype(BF16)
        vt = v_ref[...].astype(F32).T.astype(BF16)
        for h in range(MEM_HEADS):
            vt_ref[h, 0:hd, :] = vt[h * hd:(h + 1) * hd, :]
            vt_ref[h, hd:, :] = jnp.ones((vt_ref.shape[1] - hd, n_mem), BF16)

    heads = [slice(h * hd, (h + 1) * hd) for h in range(MEM_HEADS)]
    scores = [_dot_t(ks_ref[:, hs], q_ref[:, hs]) for hs in heads]
    outs = []
    for h, s in enumerate(scores):
        e = jnp.exp(s - jnp.max(s, axis=0, keepdims=True)).astype(BF16)
        acc = jnp.dot(vt_ref[h], e, preferred_element_type=F32)
        outs.append(acc[0:hd, :] / acc[hd:hd + 1, :])
    o = jnp.concatenate(outs, axis=0).T
    o_ref[...] = (o * _silu(z_ref[...].astype(F32))).astype(o_ref.dtype)


def _mem_attn(qsrc, q_col, kv, zsrc, z_col, batch, seq, n_mem, tq=512):
    nq = seq // tq
    return pl.pallas_call(
        _mem_attn_kernel,
        grid=(batch, nq),
        in_specs=[pl.BlockSpec((tq, MEM_WIDTH), lambda b, i: (b * nq + i, q_col)),
                  pl.BlockSpec((n_mem, MEM_WIDTH), lambda b, i: (b, 0)),
                  pl.BlockSpec((n_mem, MEM_WIDTH), lambda b, i: (b, 1)),
                  pl.BlockSpec((tq, MEM_WIDTH), lambda b, i: (b * nq + i, z_col))],
        out_specs=pl.BlockSpec((tq, MEM_WIDTH), lambda b, i: (b * nq + i, 0)),
        out_shape=jax.ShapeDtypeStruct((batch * seq, MEM_WIDTH), BF16),
        scratch_shapes=[pltpu.VMEM((n_mem, MEM_WIDTH), BF16),
                        pltpu.VMEM((MEM_HEADS, MEM_HEAD_DIM + 16, n_mem), BF16)],
        compiler_params=_params("arbitrary", "arbitrary"),
        name="mem_attn",
    )(qsrc, kv, kv, zsrc)


def _hawk_out_kernel(*refs):
    n_groups = len(DIL_GROUPS)
    ya_ref = refs[0]
    o_refs = refs[1:1 + n_groups * DIL_HEADS]
    l_refs = refs[1 + n_groups * DIL_HEADS:1 + n_groups * (DIL_HEADS + 1)]
    zb_ref, ym_ref, w_ref, x_ref, out_ref = refs[1 + n_groups * (DIL_HEADS + 1):]
    a_end = LRU_WIDTH
    b_end = a_end + DIL_WIDTH
    y = jnp.dot(ya_ref[...], w_ref[0:a_end, :], preferred_element_type=F32)
    y = y + jnp.dot(ym_ref[...], w_ref[b_end:b_end + MEM_WIDTH, :], preferred_element_type=F32)
    parts = []
    for h in range(DIL_HEADS):
        ls = [l[:, h * LSE_LANES:h * LSE_LANES + 1] for l in l_refs]
        m = functools.reduce(jnp.maximum, ls)
        ws = [jnp.exp(l - m) for l in ls]
        num = sum(w * o_refs[gi * DIL_HEADS + h][...].astype(F32) for gi, w in enumerate(ws))
        parts.append(num / sum(ws))
    yb = (jnp.concatenate(parts, axis=1) * _silu(zb_ref[...].astype(F32))).astype(BF16)
    y = y + jnp.dot(yb, w_ref[a_end:b_end, :], preferred_element_type=F32)
    out_ref[...] = x_ref[...] + y


def _hawk_out(ya, os_, ls_, zb_src, zb_col, ym, w, x, tm=512):
    m, d = x.shape
    row = lambda width, col=0: pl.BlockSpec((tm, width), lambda i: (i, col))
    heads = [o for group in os_ for o in group]
    return pl.pallas_call(
        _hawk_out_kernel,
        grid=(m // tm,),
        in_specs=[row(LRU_WIDTH)] + [row(DIL_HEAD_DIM)] * len(heads) + [row(LANES)] * len(ls_)
                 + [row(DIL_WIDTH, zb_col), row(MEM_WIDTH),
                    pl.BlockSpec(w.shape, lambda i: (0, 0)), row(d)],
        out_specs=row(d),
        out_shape=jax.ShapeDtypeStruct((m, d), F32),
        compiler_params=_params("arbitrary"),
        name="hawk_out",
    )(ya, *heads, *ls_, zb_src, ym, w, x)


def _compress_kernel(k_ref, v_ref, pe_ref, w1_ref, w2k_ref, w2vt_ref, ko_ref, vto_ref):
    n_blk = k_ref.shape[0] // CMP_STRIDE

    def hidden(which, src_ref):
        x = jnp.concatenate([src_ref[pl.ds(p, n_blk, stride=CMP_STRIDE), :] for p in range(CMP_STRIDE)],
                            axis=1).astype(BF16)
        first = jnp.dot(x, w1_ref[which, 0], preferred_element_type=F32)
        second = jnp.dot(x, w1_ref[which, 1], preferred_element_type=F32)
        pe = (jnp.dot(pe_ref[which, 0], w1_ref[which, 0], preferred_element_type=F32)
              + jnp.dot(pe_ref[which, 1], w1_ref[which, 1], preferred_element_type=F32))
        return _silu(first + pltpu.roll(second, n_blk - 1, axis=0) + pe[0:1, :]).astype(BF16)

    act_k, act_v = hidden(0, k_ref), hidden(1, v_ref)
    part = lambda a, g: a[:, g * PHI_HIDDEN:(g + 1) * PHI_HIDDEN]
    ks = [jnp.dot(part(act_k, g), w2k_ref[...], preferred_element_type=F32) for g in range(NSA_KV_GROUPS)]
    vts = [_dot_t(w2vt_ref[...], part(act_v, g)) for g in range(NSA_KV_GROUPS)]
    ko_ref[...] = jnp.concatenate(ks, axis=1).astype(ko_ref.dtype)
    vto_ref[...] = jnp.concatenate(vts, axis=0).astype(vto_ref.dtype)


def _compress(src, k_col, v_col, pe_k, pe_v, k_w1, k_w2, v_w1, v_w2, batch, seq):
    half = CMP_BLOCK // 2
    assert half == CMP_STRIDE and NSA_KV == LANES
    n_blk = seq // CMP_STRIDE
    hd = NSA_HEAD_DIM
    w1 = jnp.stack([k_w1, v_w1]).reshape(2, 2, half, hd, PHI_HIDDEN).astype(BF16)
    zero = jnp.zeros_like(w1)
    per_group = [jnp.concatenate([w1 if g == col else zero for col in range(NSA_KV_GROUPS)], axis=-1)
                 for g in range(NSA_KV_GROUPS)]
    w1e = jnp.stack(per_group, axis=3).reshape(2, 2, half * NSA_KV, NSA_KV_GROUPS * PHI_HIDDEN)
    pe = jnp.stack([pe_k, pe_v]).reshape(2, 2, half, 1, hd)
    pe = jnp.broadcast_to(pe, (2, 2, half, NSA_KV_GROUPS, hd)).reshape(2, 2, 1, half * NSA_KV)
    pe = jnp.broadcast_to(pe, (2, 2, SUBLANES, half * NSA_KV)).astype(BF16)
    w2k = k_w2.astype(BF16)
    w2vt = v_w2.T.astype(BF16)
    whole = lambda a: pl.BlockSpec(a.shape, lambda b: (0,) * a.ndim)
    return pl.pallas_call(
        _compress_kernel,
        grid=(batch,),
        in_specs=[pl.BlockSpec((seq, LANES), lambda b: (b, k_col)),
                  pl.BlockSpec((seq, LANES), lambda b: (b, v_col)),
                  whole(pe), whole(w1e), whole(w2k), whole(w2vt)],
        out_specs=[pl.BlockSpec((None, n_blk, NSA_KV), lambda b: (b, 0, 0)),
                   pl.BlockSpec((None, NSA_KV, n_blk), lambda b: (b, 0, 0))],
        out_shape=[jax.ShapeDtypeStruct((batch, n_blk, NSA_KV), BF16),
                   jax.ShapeDtypeStruct((batch, NSA_KV, n_blk), BF16)],
        compiler_params=_params("arbitrary"),
        name="compress",
    )(src, src, pe, w1e, w2k, w2vt)


KEY_CHUNK = 256
NSA_TQ = 256
SLOPE_PIECES = 3
N_FEATS = 2 * SLOPE_PIECES
FEAT_LANES = 32
KEY_COLS = NSA_HEAD_DIM + 2 * FEAT_LANES
N_SLC = 32


def _slope_pieces(slope):
    rest = np.float32(slope)
    pieces = []
    for _ in range(SLOPE_PIECES):
        p = np.float32(np.asarray(rest).astype(BF16))
        pieces.append(float(p))
        rest = np.float32(rest - p)
    return pieces


def _lane_table(lane, values):
    out = jnp.zeros(lane.shape, F32)
    for idx, v in enumerate(values):
        out = jnp.where(lane == idx, v, out)
    return out


def _key_feats(pos_hi, pos_lo, lane):
    return jnp.where(lane < SLOPE_PIECES, pos_hi, jnp.where(lane < N_FEATS, pos_lo, 0)).astype(F32)


def _tile_heads(x):
    return jnp.concatenate([x] * NSA_R, axis=1)


def _nsa_kernel(q_ref, kc_ref, vct_ref, ksrc_ref, vsrc_ref, kwsrc_ref, vwsrc_ref, feat_ref, hot_ref,
                gl_ref, z_ref, ym_ref, w_ref, x_ref, fin_ref, o_ref,
                ks_ref, vst_ref, kw_ref, vwt_ref, s_ref, acc_ref, imp_ref):
    i = pl.program_id(1)
    tq = q_ref.shape[0]
    hd = NSA_HEAD_DIM
    n_cmp = kc_ref.shape[0]

    @pl.when(i == 0)
    def _():
        for g in range(NSA_KV_GROUPS):
            gs = slice(g * hd, (g + 1) * hd)
            for dst, src, tail in ((ks_ref, ksrc_ref, hot_ref[...]),
                                   (kw_ref, kwsrc_ref, jnp.zeros(hot_ref.shape, BF16))):
                dst[:, g * KEY_COLS:g * KEY_COLS + hd] = src[:, gs]
                dst[:, g * KEY_COLS + hd:g * KEY_COLS + hd + FEAT_LANES] = feat_ref[...]
                dst[:, g * KEY_COLS + hd + FEAT_LANES:(g + 1) * KEY_COLS] = tail
        for c in range(vst_ref.shape[0]):
            rows = slice(c * KEY_CHUNK, (c + 1) * KEY_CHUNK)
            vst_ref[c] = vsrc_ref[rows, :].astype(F32).T.astype(BF16)
            vwt_ref[c] = vwsrc_ref[rows, :].astype(F32).T.astype(BF16)

    slopes_all = _alibi_slopes(NSA_HEADS)
    gates_t = jax.nn.sigmoid(gl_ref[...]).T
    feat_lane = lax.broadcasted_iota(jnp.int32, (tq, FEAT_LANES), 1)
    no_sel = jnp.zeros((NSA_R * tq, FEAT_LANES), BF16)
    key_row = lax.broadcasted_iota(jnp.int32, (KEY_CHUNK, tq), 0)
    t_pos = i * tq + lax.broadcasted_iota(jnp.int32, (KEY_CHUNK, tq), 1)
    ones_rows = jnp.ones((16, KEY_CHUNK), BF16)
    win_lo = jnp.maximum(i * tq - (WIN_SIZE - 1), 0) // KEY_CHUNK
    chunks_hi = (i * tq + tq - 1) // KEY_CHUNK + 1

    groups = range(NSA_KV_GROUPS)
    q_win, q_slc, o_cmp = [], [], []
    for g in groups:
        slopes = slopes_all[g * NSA_R:(g + 1) * NSA_R]
        gs = slice(g * hd, (g + 1) * hd)
        q_parts = []
        for r in range(NSA_R):
            qr = q_ref[:, (g * NSA_R + r) * hd:(g * NSA_R + r + 1) * hd]
            feats = _lane_table(feat_lane, _slope_pieces(slopes[r]) * 2).astype(BF16)
            q_parts.append(jnp.concatenate([qr, feats], axis=1))
        q_feat = jnp.concatenate(q_parts, axis=0)
        q_aug = jnp.concatenate([q_feat, no_sel], axis=1)

        n_row = lax.broadcasted_iota(jnp.int32, (n_cmp, tq), 0)
        t_cmp = i * tq + lax.broadcasted_iota(jnp.int32, (n_cmp, tq), 1)
        visible = t_cmp >= n_row * CMP_STRIDE + (CMP_BLOCK - 1)
        cfeat_row = lax.broadcasted_iota(jnp.int32, (n_cmp, 2 * FEAT_LANES), 0)
        cfeat_lane = lax.broadcasted_iota(jnp.int32, (n_cmp, 2 * FEAT_LANES), 1)
        kc_feats = _key_feats(cfeat_row * CMP_STRIDE, 0, cfeat_lane)
        kc_aug = jnp.concatenate([kc_ref[:, gs], kc_feats.astype(BF16)], axis=1)
        s = _dot_t(kc_aug, q_aug) + _tile_heads(jnp.where(visible, 0.0, NEG_INF))
        m = jnp.max(s, axis=0, keepdims=True)
        e = jnp.exp(s - m)
        t_one = i * tq + lax.broadcasted_iota(jnp.int32, (1, NSA_R * tq), 1) % tq
        any_visible = t_one >= (CMP_BLOCK - 1)
        p = e * jnp.where(any_visible, 1.0 / jnp.sum(e, axis=0, keepdims=True), 0.0)
        o_cmp.append(jnp.dot(vct_ref[gs, :], p.astype(BF16), preferred_element_type=F32))
        p_sum = p[:, 0:tq]
        for r in range(1, NSA_R):
            p_sum = p_sum + p[:, r * tq:(r + 1) * tq]

        band = p_sum + pltpu.roll(p_sum, 1, axis=0)
        for k in range(1, CMP_PER_SLC):
            band = band + pltpu.roll(p_sum, n_cmp - k, axis=0)
        halves = []
        for half in range(tq // LANES):
            imp_ref[...] = band[:, half * LANES:(half + 1) * LANES]
            halves.append(imp_ref[pl.ds(0, N_SLC, stride=CMP_PER_SLC), :])
        imp = jnp.concatenate(halves, axis=1)
        blk_j = lax.broadcasted_iota(jnp.int32, (N_SLC, tq), 0)
        cur = (i * tq + lax.broadcasted_iota(jnp.int32, (N_SLC, tq), 1)) // SLC_BLOCK
        forced = (blk_j == 0) | (blk_j == cur) | (blk_j == cur - 1)
        v_imp = jnp.where(forced, SEL_FORCE, jnp.where(blk_j > cur, -SEL_FORCE, imp))
        rank = jnp.zeros((N_SLC, tq), F32)
        for other in range(N_SLC):
            row = v_imp[other:other + 1, :]
            ahead = (row > v_imp) | ((row == v_imp) & (blk_j > other))
            rank = rank + jnp.where(ahead, 1.0, 0.0)
        sel_bias = jnp.where(rank < SLC_TOP_N, 0.0, NEG_INF)

        padded = jnp.concatenate([sel_bias, jnp.zeros((LANES - N_SLC, tq), F32)], axis=0)
        sel_t = padded.T[:, 0:FEAT_LANES].astype(BF16)
        q_win.append(q_aug)
        q_slc.append(jnp.concatenate([q_feat, jnp.concatenate([sel_t] * NSA_R, axis=0)], axis=1))

    def attend(q_brs, k_ref, key_cols, vt_ref, lo, hi, masked_from, mask_fn):
        def scores(c, m_run, masked):
            start = pl.multiple_of(c * KEY_CHUNK, KEY_CHUNK)
            scs = [_dot_t(k_ref[pl.ds(start, KEY_CHUNK), g * key_cols:(g + 1) * key_cols], q_brs[g])
                   for g in groups]
            if masked:
                bias = _tile_heads(jnp.where(mask_fn(t_pos - (start + key_row)), 0.0, NEG_INF))
                scs = [sc + bias for sc in scs]
            for g in groups:
                s_ref[g, c] = scs[g]
            return tuple(jnp.maximum(m_run[g], jnp.max(scs[g], axis=0, keepdims=True)) for g in groups)

        m_rows = tuple(jnp.full((1, NSA_R * tq), NEG_INF, F32) for _ in groups)
        m_rows = lax.fori_loop(lo, masked_from, functools.partial(scores, masked=False), m_rows)
        m_rows = lax.fori_loop(masked_from, hi, functools.partial(scores, masked=True), m_rows)
        acc_ref[...] = jnp.zeros(acc_ref.shape, F32)

        def weighted(c, carry):
            for g in groups:
                e = jnp.exp(s_ref[g, c] - m_rows[g]).astype(BF16)
                v_ext = jnp.concatenate([vt_ref[c, g * hd:(g + 1) * hd, :], ones_rows], axis=0)
                acc_ref[g] += jnp.dot(v_ext, e, preferred_element_type=F32)
            return carry

        lax.fori_loop(lo, hi, weighted, 0)
        return [acc_ref[g, 0:hd, :] / acc_ref[g, hd:hd + 1, :] for g in groups]

    o_slc = attend(q_slc, ks_ref, KEY_COLS, vst_ref, 0, chunks_hi, (i * tq) // KEY_CHUNK,
                   lambda dist: dist >= 0)
    o_win = attend(q_win, kw_ref, KEY_COLS, vwt_ref, win_lo, chunks_hi, win_lo,
                   lambda dist: (dist >= 0) & (dist <= WIN_SIZE - 1))

    y = jnp.dot(ym_ref[...], w_ref[NSA_WIDTH:NSA_WIDTH + MEM_WIDTH, :], preferred_element_type=F32)
    for g in groups:
        def gate(kind):
            base = g * NSA_R * 3 + kind
            return jnp.concatenate([gates_t[base + 3 * r:base + 3 * r + 1, :] for r in range(NSA_R)], axis=1)

        o = gate(0) * o_cmp[g] + gate(1) * o_slc[g] + gate(2) * o_win[g]
        pairs = []
        for r in range(0, NSA_R, 2):
            two = jnp.concatenate([o[:, r * tq:(r + 1) * tq], o[:, (r + 1) * tq:(r + 2) * tq]], axis=0)
            pairs.append(two.T)
        cs = slice(g * NSA_R * hd, (g + 1) * NSA_R * hd)
        yo = (jnp.concatenate(pairs, axis=1) * _silu(z_ref[:, cs].astype(F32))).astype(BF16)
        y = y + jnp.dot(yo, w_ref[cs, :], preferred_element_type=F32)

    x = x_ref[...] + y
    ms = jnp.mean(x * x, axis=-1, keepdims=True)
    o_ref[...] = x * lax.rsqrt(ms + NORM_EPS) * fin_ref[...]


def _key_pos_feats(seq):
    assert seq // SLC_BLOCK == N_SLC <= FEAT_LANES
    pos = np.arange(seq)
    feats = np.zeros((seq, FEAT_LANES), np.float32)
    feats[:, 0:SLOPE_PIECES] = ((pos // SLC_BLOCK) * SLC_BLOCK)[:, None]
    feats[:, SLOPE_PIECES:N_FEATS] = (pos % SLC_BLOCK)[:, None]
    onehot = (np.arange(FEAT_LANES)[None, :] == (pos // SLC_BLOCK)[:, None]).astype(np.float32)
    return jnp.asarray(feats, BF16), jnp.asarray(onehot, BF16)


def _nsa_attn(nb, kv_col0, z_col, k_cmp, v_cmp_t, nf, gl_col, ym, w_out, x, final_g, batch, seq):
    tq = NSA_TQ
    nq = seq // tq
    d = x.shape[1]
    feats, onehot = _key_pos_feats(seq)
    kv_blk = kv_col0 // NSA_KV
    seq_cols = lambda col: pl.BlockSpec((seq, NSA_KV), lambda b, i: (b, kv_blk + col))
    const = lambda a: pl.BlockSpec(a.shape, lambda b, i: (0,) * a.ndim)
    per_batch = lambda a: pl.BlockSpec((None,) + a.shape[1:], lambda b, i: (b,) + (0,) * (a.ndim - 1))
    rows = lambda width, col=0: pl.BlockSpec((tq, width), lambda b, i: (b * nq + i, col))
    fin = final_g.reshape(1, d)
    return pl.pallas_call(
        _nsa_kernel,
        grid=(batch, nq),
        in_specs=[rows(NSA_WIDTH), per_batch(k_cmp), per_batch(v_cmp_t),
                  seq_cols(0), seq_cols(1), seq_cols(2), seq_cols(3), const(feats), const(onehot),
                  rows(LANES, gl_col), rows(NSA_WIDTH, z_col), rows(MEM_WIDTH), const(w_out), rows(d),
                  const(fin)],
        out_specs=rows(d),
        out_shape=jax.ShapeDtypeStruct((batch * seq, d), F32),
        scratch_shapes=[pltpu.VMEM((seq, NSA_KV_GROUPS * KEY_COLS), BF16),
                        pltpu.VMEM((seq // KEY_CHUNK, NSA_KV, KEY_CHUNK), BF16),
                        pltpu.VMEM((seq, NSA_KV_GROUPS * KEY_COLS), BF16),
                        pltpu.VMEM((seq // KEY_CHUNK, NSA_KV, KEY_CHUNK), BF16),
                        pltpu.VMEM((NSA_KV_GROUPS, seq // KEY_CHUNK, KEY_CHUNK, NSA_R * tq), F32),
                        pltpu.VMEM((NSA_KV_GROUPS, NSA_HEAD_DIM + 16, NSA_R * tq), F32),
                        pltpu.VMEM((seq // CMP_STRIDE, LANES), F32)],
        compiler_params=_params("arbitrary", "arbitrary"),
        name="nsa_attn",
    )(nb, k_cmp, v_cmp_t, nb, nb, nb, nb, feats, onehot, nf, nb, ym, w_out, x, fin)


def _cols(w, *ranges):
    return jnp.concatenate([w[:, a:b] for a, b in ranges], axis=1).astype(BF16)


def _hawk_layer(x, mem, batch, seq, norm_g, w_in, conv_w, conv_b, ga_w, ga_b, gx_w, gx_b, lam,
                mem_norm_g, w_mem_kv, w_out):
    xa0, za0 = 0, LRU_WIDTH
    q0 = 2 * LRU_WIDTH
    k0, v0 = q0 + DIL_QKV, q0 + 2 * DIL_QKV
    zb0 = q0 + 3 * DIL_QKV
    qm0 = zb0 + DIL_WIDTH
    zm0 = qm0 + MEM_WIDTH

    def qkv_cols(gi):
        return [(base + gi * DIL_WIDTH, base + (gi + 1) * DIL_WIDTH) for base in (q0, k0, v0)]

    w_nat = _cols(w_in, (xa0, za0), (za0, q0), *qkv_cols(0), (zb0, qm0), (qm0, zm0), (zm0, zm0 + MEM_WIDTH))
    nat_width = w_nat.shape[1] - LRU_WIDTH
    xa, hb = _norm_matmul(x, norm_g, w_nat, [(LRU_WIDTH, F32), (nat_width, BF16)])
    za_col = 0
    qkv0_col = LRU_WIDTH // DIL_WIDTH
    zb_col = (LRU_WIDTH + 3 * DIL_WIDTH) // DIL_WIDTH
    qm_col = (LRU_WIDTH + 4 * DIL_WIDTH) // MEM_WIDTH
    zm_col = qm_col + 1
    qkv = [(hb[None], qkv0_col)]
    for gi in range(1, len(DIL_GROUPS)):
        qkv.append((_norm_matmul(x, norm_g, _cols(w_in, *qkv_cols(gi)), [(3 * DIL_WIDTH, BF16)],
                                 dil=DIL_GROUPS[gi][1]), 0))
    n_mem = mem.shape[0] // batch
    mem_kv, = _norm_matmul(mem, mem_norm_g, w_mem_kv.astype(BF16), [(2 * MEM_WIDTH, BF16)])

    ya = _rglru(xa, hb, za_col, conv_w, conv_b, _pack_block_diag(ga_w), ga_b, _pack_block_diag(gx_w), gx_b,
                lam, batch, seq)
    attn = [_dil_attn(arr, col0, gi, batch, seq) for gi, (arr, col0) in enumerate(qkv)]
    ym = _mem_attn(hb, qm_col, mem_kv, hb, zm_col, batch, seq, n_mem)
    return _hawk_out(ya, [o for o, _ in attn], [l for _, l in attn], hb, zb_col, ym,
                     w_out.astype(BF16), x)


def _nsa_layer(x, mem, batch, seq, norm_g, w_in, pe_k, pe_v, phik_w1, phik_w2, phiv_w1, phiv_w2,
               mem_norm_g, w_mem_kv, w_out, final_g):
    kv0 = NSA_WIDTH
    gl0 = kv0 + 6 * NSA_KV
    z0 = gl0 + 3 * NSA_HEADS
    qm0 = z0 + NSA_WIDTH
    zm0 = qm0 + MEM_WIDTH
    gl_w = jnp.pad(w_in[:, gl0:z0], ((0, 0), (0, LANES - 3 * NSA_HEADS)))
    q_w = w_in[:, 0:kv0] * (NSA_HEAD_DIM ** -0.5)
    w_all = jnp.concatenate([gl_w, w_in[:, kv0:kv0 + 2 * NSA_KV], q_w, w_in[:, z0:qm0],
                             w_in[:, kv0 + 2 * NSA_KV:gl0], w_in[:, qm0:zm0 + MEM_WIDTH]], axis=1).astype(BF16)
    f32_width = LANES + 2 * NSA_KV
    nf, nb = _norm_matmul(x, norm_g, w_all, [(f32_width, F32), (w_all.shape[1] - f32_width, BF16)])
    gl_col, kc_col, vc_col = 0, 1, 2
    z_col = 1
    kv_col0 = 2 * NSA_WIDTH
    qm_col = (kv_col0 + 4 * NSA_KV) // MEM_WIDTH
    zm_col = qm_col + 1
    n_mem = mem.shape[0] // batch
    mem_kv, = _norm_matmul(mem, mem_norm_g, w_mem_kv.astype(BF16), [(2 * MEM_WIDTH, BF16)])

    k_cmp, v_cmp_t = _compress(nf, kc_col, vc_col, pe_k, pe_v, phik_w1, phik_w2, phiv_w1, phiv_w2,
                               batch, seq)
    ym = _mem_attn(nb, qm_col, mem_kv, nb, zm_col, batch, seq, n_mem)
    return _nsa_attn(nb, kv_col0, z_col, k_cmp, v_cmp_t, nf, gl_col, ym, w_out.astype(BF16), x, final_g,
                     batch, seq)


def kernel(x, mem, hawk_norm, hawk_w_in, hawk_conv_w, hawk_conv_b, hawk_gate_a_w, hawk_gate_a_b,
           hawk_gate_x_w, hawk_gate_x_b, hawk_lambda, hawk_mem_norm, hawk_w_mem_kv, hawk_w_out,
           nsa_norm, nsa_w_in, nsa_pe_k, nsa_pe_v, nsa_phi_k_w1, nsa_phi_k_w2, nsa_phi_v_w1,
           nsa_phi_v_w2, nsa_mem_norm, nsa_w_mem_kv, nsa_w_out, final_norm):
    batch, seq, d = x.shape
    assert hawk_norm.shape[0] == 1 and nsa_norm.shape[0] == 1, "one layer of each kind"
    assert seq % (ATTN_BLOCK * DIL_GROUPS[-1][1]) == 0
    x2 = x.reshape(batch * seq, d)
    mem2 = mem.reshape(batch * mem.shape[1], d)
    x2 = _hawk_layer(x2, mem2, batch, seq, hawk_norm[0], hawk_w_in[0], hawk_conv_w[0], hawk_conv_b[0],
                     hawk_gate_a_w[0], hawk_gate_a_b[0].reshape(-1), hawk_gate_x_w[0],
                     hawk_gate_x_b[0].reshape(-1), hawk_lambda[0], hawk_mem_norm[0], hawk_w_mem_kv[0],
                     hawk_w_out[0])
    out = _nsa_layer(x2, mem2, batch, seq, nsa_norm[0], nsa_w_in[0], nsa_pe_k[0], nsa_pe_v[0],
                     nsa_phi_k_w1[0], nsa_phi_k_w2[0], nsa_phi_v_w1[0], nsa_phi_v_w2[0],
                     nsa_mem_norm[0], nsa_w_mem_kv[0], nsa_w_out[0], final_norm)
    return out.reshape(batch, seq, d)
```

```python
import functools

import numpy as np
import jax
import jax.numpy as jnp
from jax import lax
from jax.experimental import pallas as pl
from jax.experimental.pallas import tpu as pltpu

F32 = jnp.float32
BF16 = jnp.bfloat16

NORM_EPS = 1e-6
NEG_INF = -1e30
LANES = 128
SUBLANES = 8
ATTN_BLOCK = 128
VMEM_LIMIT = 56 * 1024 * 1024

LRU_WIDTH = 1024
LRU_BLOCKS = 16
LRU_BLOCK_DIM = LRU_WIDTH // LRU_BLOCKS
LRU_PACK = 256
CONV_WIDTH = 4
LRU_C = 8.0

DIL_GROUPS = ((128, 1), (512, 4), (2048, 16))
DIL_HEADS = 4
DIL_HEAD_DIM = 128
DIL_WIDTH = DIL_HEADS * DIL_HEAD_DIM
DIL_QKV = len(DIL_GROUPS) * DIL_WIDTH
LSE_LANES = LANES // DIL_HEADS

MEM_HEADS = 4
MEM_HEAD_DIM = 64
MEM_WIDTH = MEM_HEADS * MEM_HEAD_DIM

NSA_HEADS = 16
NSA_KV_GROUPS = 2
NSA_R = NSA_HEADS // NSA_KV_GROUPS
NSA_HEAD_DIM = 64
NSA_WIDTH = NSA_HEADS * NSA_HEAD_DIM
NSA_KV = NSA_KV_GROUPS * NSA_HEAD_DIM
CMP_BLOCK = 32
CMP_STRIDE = 16
SLC_BLOCK = 64
SLC_TOP_N = 8
WIN_SIZE = 512
PHI_HIDDEN = 256
SEL_FORCE = 1e6
CMP_PER_SLC = SLC_BLOCK // CMP_STRIDE


def _alibi_slopes(n):
    return [float(v) for v in np.exp2(-8.0 * np.arange(1, n + 1) / n).astype(np.float32)]


def _params(*semantics):
    return pltpu.CompilerParams(dimension_semantics=semantics, vmem_limit_bytes=VMEM_LIMIT)


def _silu(z):
    return z * jax.nn.sigmoid(z)


def _dot_t(a, b):
    return lax.dot_general(a, b, (((1,), (1,)), ((), ())), preferred_element_type=F32)


def _rms_norm_rows(x, g):
    ms = jnp.mean(x * x, axis=-1, keepdims=True)
    return (x * lax.rsqrt(ms + NORM_EPS) * g).astype(BF16)


def _norm_matmul_kernel(*refs, dil, pieces):
    def project(xn, out_ref, w_refs, by_class=False):
        col = 0
        for w_ref in w_refs:
            width = w_ref.shape[1]
            res = jnp.dot(xn, w_ref[...], preferred_element_type=F32).astype(out_ref.dtype)
            if by_class:
                per = xn.shape[0] // dil
                for c in range(dil):
                    out_ref[c, :, col:col + width] = res[c * per:(c + 1) * per]
            else:
                out_ref[:, col:col + width] = res
            col += width

    if dil == 1:
        x_ref, g_ref = refs[:2]
        w_refs, o_refs = refs[2:2 + sum(pieces)], refs[2 + sum(pieces):]
        xn = _rms_norm_rows(x_ref[...], g_ref[...])
        first = 0
        for o_ref, n_pieces in zip(o_refs, pieces):
            project(xn, o_ref, w_refs[first:first + n_pieces])
            first += n_pieces
        return
    if dil % SUBLANES == 0:
        x_ref, g_ref, *w_refs, o_ref = refs
        tm, k = x_ref.shape
        x = x_ref[...]
        xn = x * lax.rsqrt(jnp.mean(x * x, axis=-1, keepdims=True) + NORM_EPS) * g_ref[...]
        xn = jnp.swapaxes(xn.reshape(tm // dil, dil, k), 0, 1).reshape(tm, k).astype(BF16)
        project(xn, o_ref, w_refs, by_class=True)
        return
    n_w = sum(pieces)
    x_refs, g_ref, w_refs = refs[:-(n_w + 3)], refs[-(n_w + 3)], refs[-(n_w + 2):-2]
    o_ref, xn_ref = refs[-2:]
    per = x_refs[0].shape[0] // dil
    inv_k = 1.0 / (len(x_refs) * LANES)
    for c in range(dil):
        xs = [x_ref[pl.ds(c, per, stride=dil), :] for x_ref in x_refs]
        ss = xs[0] * xs[0]
        for xj in xs[1:]:
            ss = ss + xj * xj
        r = lax.rsqrt(jnp.sum(ss, axis=-1, keepdims=True) * inv_k + NORM_EPS)
        for j, xj in enumerate(xs):
            cols = slice(j * LANES, (j + 1) * LANES)
            xn_ref[c * per:(c + 1) * per, cols] = (xj * r * g_ref[:, cols]).astype(BF16)
    project(xn_ref[...], o_ref, w_refs, by_class=True)


def _norm_matmul(x, g, w, outs, dil=1, tm=512):
    m, k = x.shape
    tm = min(tm, m)
    assert m % tm == 0 and k % LANES == 0
    pieces = [len(cols) for _, cols in outs]
    widths = [sum(width for _, width in cols) for _, cols in outs]
    w_specs = []
    for _, cols in outs:
        for col, width in cols:
            assert col % width == 0 and width % LANES == 0
            w_specs.append(pl.BlockSpec((k, width), functools.partial(lambda i, j: (0, j), j=col // width)))
    resident = [pl.BlockSpec((1, k), lambda i: (0, 0))] + w_specs
    operands = (g.reshape(1, k),) + (w,) * len(w_specs)
    kernel = functools.partial(_norm_matmul_kernel, dil=dil, pieces=pieces)
    if dil == 1:
        return pl.pallas_call(
            kernel,
            grid=(m // tm,),
            in_specs=[pl.BlockSpec((tm, k), lambda i: (i, 0))] + resident,
            out_specs=[pl.BlockSpec((tm, width), lambda i: (i, 0)) for width in widths],
            out_shape=[jax.ShapeDtypeStruct((m, width), dtype) for width, (dtype, _) in zip(widths, outs)],
            compiler_params=_params("arbitrary"),
            name="norm_matmul",
        )(x, *operands)
    per = tm // dil
    (out_dtype, _), = outs
    n, = widths
    assert tm % dil == 0 and per % 16 == 0
    if dil % SUBLANES == 0:
        x_specs, scratch = [pl.BlockSpec((tm, k), lambda i: (i, 0))], []
    else:
        x_specs = [pl.BlockSpec((tm, LANES), functools.partial(lambda i, j: (i, j), j=j)) for j in range(k // LANES)]
        scratch = [pltpu.VMEM((tm, k), BF16)]
    return pl.pallas_call(
        kernel,
        grid=(m // tm,),
        in_specs=x_specs + resident,
        out_specs=pl.BlockSpec((dil, per, n), lambda i: (0, i, 0)),
        out_shape=jax.ShapeDtypeStruct((dil, m // dil, n), out_dtype),
        scratch_shapes=scratch,
        compiler_params=_params("arbitrary"),
        name="norm_matmul_dil",
    )(*([x] * len(x_specs)), *operands)


def _rglru_kernel(xa_ref, za_ref, cw_ref, cb_ref, wa_ref, ba_ref, wx_ref, bx_ref, lam_ref,
                  o_ref, xpad_ref, h_ref):
    t = pl.program_id(1)
    tt, width = xa_ref.shape
    halo = 8

    @pl.when(t == 0)
    def _():
        xpad_ref[0:halo, :] = jnp.zeros((halo, width), F32)
        h_ref[...] = jnp.zeros_like(h_ref)

    x = xa_ref[...]
    xpad_ref[halo:halo + tt, :] = x
    cw = cw_ref[...]
    y = cw[CONV_WIDTH - 1:CONV_WIDTH] * x
    for k in range(1, CONV_WIDTH):
        y = y + cw[CONV_WIDTH - 1 - k:CONV_WIDTH - k] * xpad_ref[halo - k:halo - k + tt, :]
    y = y + cb_ref[...]
    xpad_ref[0:halo, :] = x[tt - halo:tt, :]

    yb = y.astype(BF16)
    r_parts, i_parts = [], []
    for p in range(width // LRU_PACK):
        ys = yb[:, p * LRU_PACK:(p + 1) * LRU_PACK]
        r_parts.append(jnp.dot(ys, wa_ref[p], preferred_element_type=F32))
        i_parts.append(jnp.dot(ys, wx_ref[p], preferred_element_type=F32))
    r = jax.nn.sigmoid(jnp.concatenate(r_parts, axis=1) + ba_ref[...])
    gi = jax.nn.sigmoid(jnp.concatenate(i_parts, axis=1) + bx_ref[...])

    nl = -lam_ref[...]
    softplus = jnp.maximum(nl, 0.0) + jnp.log1p(jnp.exp(-jnp.abs(nl)))
    log_a = (-LRU_C) * r * softplus
    a = jnp.exp(log_a)
    z2 = 2.0 * log_a
    u = a * a
    one_minus = jnp.where(u == 1.0, -z2, jnp.where(u < 0.5, 1.0 - u, (1.0 - u) * z2 / jnp.log(u)))
    mult = jnp.sqrt(one_minus)
    b = y * gi * mult
    first = (lax.broadcasted_iota(jnp.int32, (SUBLANES, width), 0) == 0) & (t == 0)
    b = jnp.concatenate([jnp.where(first, (y * gi)[0:SUBLANES], b[0:SUBLANES]), b[SUBLANES:]], axis=0)

    groups = tt // SUBLANES
    a3 = a.reshape(groups, SUBLANES, width)
    b3 = b.reshape(groups, SUBLANES, width)
    sub = lax.broadcasted_iota(jnp.int32, (groups, SUBLANES, width), 1)
    k = 1
    while k < SUBLANES:
        keep = sub >= k
        a_sh = jnp.where(keep, pltpu.roll(a3, k, axis=1), 1.0)
        b_sh = jnp.where(keep, pltpu.roll(b3, k, axis=1), 0.0)
        b3 = a3 * b_sh + b3
        a3 = a3 * a_sh
        k *= 2
    carry = jnp.broadcast_to(h_ref[...], (SUBLANES, width))
    hs = []
    for gidx in range(groups):
        hg = a3[gidx] * carry + b3[gidx]
        hs.append(hg)
        carry = jnp.broadcast_to(hg[SUBLANES - 1:SUBLANES], (SUBLANES, width))
    h = jnp.concatenate(hs, axis=0)
    h_ref[...] = carry[0:1]
    o_ref[...] = (h * _silu(za_ref[...].astype(F32))).astype(o_ref.dtype)


def _rglru(xa, za_src, za_col, conv_w, conv_b, wa, ba, wx, bx, lam, batch, seq, tt=512):
    width = LRU_WIDTH
    nt = seq // tt
    packs = width // LRU_PACK
    vec = pl.BlockSpec((1, width), lambda b, t: (0, 0))
    gate_w = pl.BlockSpec((packs, LRU_PACK, LRU_PACK), lambda b, t: (0, 0, 0))
    return pl.pallas_call(
        _rglru_kernel,
        grid=(batch, nt),
        in_specs=[pl.BlockSpec((tt, width), lambda b, t: (b * nt + t, 0)),
                  pl.BlockSpec((tt, width), lambda b, t: (b * nt + t, za_col)),
                  pl.BlockSpec((CONV_WIDTH, width), lambda b, t: (0, 0)),
                  vec, gate_w, vec, gate_w, vec, vec],
        out_specs=pl.BlockSpec((tt, width), lambda b, t: (b * nt + t, 0)),
        out_shape=jax.ShapeDtypeStruct((batch * seq, width), BF16),
        scratch_shapes=[pltpu.VMEM((tt + 8, width), F32), pltpu.VMEM((1, width), F32)],
        compiler_params=_params("arbitrary", "arbitrary"),
        name="rglru",
    )(xa, za_src, conv_w, conv_b.reshape(1, width), wa, ba.reshape(1, width), wx, bx.reshape(1, width),
      lam.reshape(1, width))


def _pack_block_diag(w):
    per = LRU_PACK // LRU_BLOCK_DIM
    w = w.reshape(LRU_BLOCKS // per, per, LRU_BLOCK_DIM, LRU_BLOCK_DIM)
    eye = jnp.eye(per, dtype=w.dtype)
    packed = w[:, :, :, None, :] * eye[None, :, None, :, None]
    return packed.reshape(LRU_BLOCKS // per, LRU_PACK, LRU_PACK).astype(BF16)


def _dil_attn_kernel(*refs, slopes, pos_scale, max_dist, has_halo, dil, n_cls, n_blk):
    if has_halo:
        q_ref, kh_ref, k_ref, vh_ref, v_ref = refs[:5]
        out_refs = refs[5:]
    else:
        q_ref, k_ref, v_ref = refs[:3]
        out_refs = refs[3:]
    n_out = DIL_HEADS + 1
    dst_refs = out_refs[:n_out]
    stage_refs = out_refs[n_out:] if dil > 1 else dst_refs
    first_super = pl.program_id(1) == 0
    cls0 = pl.program_id(2) * n_cls
    blk = ATTN_BLOCK
    scale = DIL_HEAD_DIM ** -0.5

    def band(width, halo_live):
        row = lax.broadcasted_iota(jnp.int32, (blk, width), 0)
        col = lax.broadcasted_iota(jnp.int32, (blk, width), 1)
        dist = (width - blk) + row - col
        valid = (dist >= 0) & (dist <= max_dist)
        if halo_live is not None:
            valid = valid & ((col >= blk) | halo_live)
        distf = (dist * pos_scale).astype(F32)
        return [jnp.where(valid, -slope * distf, NEG_INF) for slope in slopes]

    bias_inner = band(2 * blk, None) if n_blk > 1 else None
    bias_first = band(2 * blk, jnp.logical_not(first_super)) if has_halo else band(blk, None)

    def scores(cc, jb):
        cur = slice(jb * blk, (jb + 1) * blk)
        bias = bias_inner if jb > 0 else bias_first
        out = []
        for h in range(DIL_HEADS):
            hs = slice(h * DIL_HEAD_DIM, (h + 1) * DIL_HEAD_DIM)
            q = q_ref[cc, cur, hs]
            if jb > 0:
                k = k_ref[cc, (jb - 1) * blk:(jb + 1) * blk, hs]
                v = v_ref[cc, (jb - 1) * blk:(jb + 1) * blk, hs]
            elif has_halo:
                k = jnp.concatenate([kh_ref[cc, :, hs], k_ref[cc, cur, hs]], axis=0)
                v = jnp.concatenate([vh_ref[cc, :, hs], v_ref[cc, cur, hs]], axis=0)
            else:
                k, v = k_ref[cc, cur, hs], v_ref[cc, cur, hs]
            out.append((_dot_t(q, k) * scale + bias[h], v))
        return out

    def finish(cc, jb, pairs):
        where = (cls0 + cc, slice(jb * blk, (jb + 1) * blk)) if dil > 1 else (slice(jb * blk, (jb + 1) * blk),)
        lses = []
        for h, (s, v) in enumerate(pairs):
            m = jnp.max(s, axis=-1, keepdims=True)
            e = jnp.exp(s - m)
            den = jnp.sum(e, axis=-1, keepdims=True)
            o = jnp.dot(e.astype(BF16), v, preferred_element_type=F32) / den
            stage_refs[h][where] = o.astype(stage_refs[h].dtype)
            lses.append(jnp.broadcast_to(m + jnp.log(den), (blk, LSE_LANES)))
        stage_refs[DIL_HEADS][where] = jnp.concatenate(lses, axis=1)

    pending = None
    for cc in range(n_cls):
        for jb in range(n_blk):
            pairs = scores(cc, jb)
            if pending is not None:
                finish(*pending)
            pending = (cc, jb, pairs)
    finish(*pending)

    if dil > 1:
        @pl.when(pl.program_id(2) == pl.num_programs(2) - 1)
        def _():
            for stage, dst in zip(stage_refs, dst_refs):
                dst[...] = jnp.swapaxes(stage[...], 0, 1).reshape(dst.shape).astype(dst.dtype)


def _dil_attn(qkv, col0, gi, batch, seq, work=4):
    window, dil = DIL_GROUPS[gi]
    sub = seq // dil
    nb = sub // ATTN_BLOCK
    n_blk = min(work, nb)
    n_cls = min(work // n_blk, dil)
    n_super = nb // n_blk
    has_halo = n_super > 1
    span = n_blk * ATTN_BLOCK
    slopes = _alibi_slopes(len(DIL_GROUPS) * DIL_HEADS)[gi * DIL_HEADS:(gi + 1) * DIL_HEADS]
    cur = lambda col: pl.BlockSpec((n_cls, span, DIL_WIDTH), lambda b, i, c: (c, b * n_super + i, col0 + col))
    halo = lambda col: pl.BlockSpec(
        (n_cls, ATTN_BLOCK, DIL_WIDTH),
        lambda b, i, c: (c, jnp.maximum((b * n_super + i) * n_blk - 1, 0), col0 + col))
    if has_halo:
        in_specs = [cur(0), halo(1), cur(1), halo(2), cur(2)]
    else:
        in_specs = [cur(0), cur(1), cur(2)]
    n_out = DIL_HEADS + 1
    *o, lse = pl.pallas_call(
        functools.partial(_dil_attn_kernel, slopes=slopes, pos_scale=dil, max_dist=window // dil,
                          has_halo=has_halo, dil=dil, n_cls=n_cls, n_blk=n_blk),
        grid=(batch, n_super, dil // n_cls),
        in_specs=in_specs,
        out_specs=[pl.BlockSpec((span * dil, LANES), lambda b, i, c: (b * n_super + i, 0))] * n_out,
        out_shape=[jax.ShapeDtypeStruct((batch * seq, LANES), BF16)] * DIL_HEADS
                  + [jax.ShapeDtypeStruct((batch * seq, LANES), F32)],
        scratch_shapes=[pltpu.VMEM((dil, span, LANES), F32)] * (n_out if dil > 1 else 0),
        compiler_params=_params("arbitrary", "arbitrary", "arbitrary"),
        name=f"dil_attn_d{dil}",
    )(*([qkv] * len(in_specs)))
    return o, lse


def _mem_attn_kernel(q_ref, k_ref, v_ref, z_ref, o_ref, ks_ref, vt_ref):
    hd = MEM_HEAD_DIM
    n_mem = k_ref.shape[0]

    @pl.when(pl.program_id(1) == 0)
    def _():
        ks_ref[...] = (k_ref[...].astype(F32) * (hd ** -0.5)).astype(BF16)
        vt = v_ref[...].astype(F32).T.astype(BF16)
        for h in range(MEM_HEADS):
            vt_ref[h, 0:hd, :] = vt[h * hd:(h + 1) * hd, :]
            vt_ref[h, hd:, :] = jnp.ones((vt_ref.shape[1] - hd, n_mem), BF16)

    heads = [slice(h * hd, (h + 1) * hd) for h in range(MEM_HEADS)]
    scores = [_dot_t(ks_ref[:, hs], q_ref[:, hs]) for hs in heads]
    outs = []
    for h, s in enumerate(scores):
        e = jnp.exp(s - jnp.max(s, axis=0, keepdims=True)).astype(BF16)
        acc = jnp.dot(vt_ref[h], e, preferred_element_type=F32)
        outs.append(acc[0:hd, :] / acc[hd:hd + 1, :])
    o = jnp.concatenate(outs, axis=0).T
    o_ref[...] = (o * _silu(z_ref[...].astype(F32))).astype(o_ref.dtype)


def _mem_attn(qsrc, q_col, kv, zsrc, z_col, batch, seq, n_mem, tq=512):
    nq = seq // tq
    return pl.pallas_call(
        _mem_attn_kernel,
        grid=(batch, nq),
        in_specs=[pl.BlockSpec((tq, MEM_WIDTH), lambda b, i: (b * nq + i, q_col)),
                  pl.BlockSpec((n_mem, MEM_WIDTH), lambda b, i: (b, 0)),
                  pl.BlockSpec((n_mem, MEM_WIDTH), lambda b, i: (b, 1)),
                  pl.BlockSpec((tq, MEM_WIDTH), lambda b, i: (b * nq + i, z_col))],
        out_specs=pl.BlockSpec((tq, MEM_WIDTH), lambda b, i: (b * nq + i, 0)),
        out_shape=jax.ShapeDtypeStruct((batch * seq, MEM_WIDTH), BF16),
        scratch_shapes=[pltpu.VMEM((n_mem, MEM_WIDTH), BF16),
                        pltpu.VMEM((MEM_HEADS, MEM_HEAD_DIM + 16, n_mem), BF16)],
        compiler_params=_params("arbitrary", "arbitrary"),
        name="mem_attn",
    )(qsrc, kv, kv, zsrc)


def _hawk_out_kernel(*refs):
    n_groups = len(DIL_GROUPS)
    ya_ref = refs[0]
    o_refs = refs[1:1 + n_groups * DIL_HEADS]
    l_refs = refs[1 + n_groups * DIL_HEADS:1 + n_groups * (DIL_HEADS + 1)]
    zb_ref, ym_ref, w_ref, x_ref, out_ref = refs[1 + n_groups * (DIL_HEADS + 1):]
    a_end = LRU_WIDTH
    b_end = a_end + DIL_WIDTH
    y = jnp.dot(ya_ref[...], w_ref[0:a_end, :], preferred_element_type=F32)
    y = y + jnp.dot(ym_ref[...], w_ref[b_end:b_end + MEM_WIDTH, :], preferred_element_type=F32)
    parts = []
    for h in range(DIL_HEADS):
        ls = [l[:, h * LSE_LANES:h * LSE_LANES + 1] for l in l_refs]
        m = functools.reduce(jnp.maximum, ls)
        ws = [jnp.exp(l - m) for l in ls]
        num = sum(w * o_refs[gi * DIL_HEADS + h][...].astype(F32) for gi, w in enumerate(ws))
        parts.append(num / sum(ws))
    yb = (jnp.concatenate(parts, axis=1) * _silu(zb_ref[...].astype(F32))).astype(BF16)
    y = y + jnp.dot(yb, w_ref[a_end:b_end, :], preferred_element_type=F32)
    out_ref[...] = x_ref[...] + y


def _hawk_out(ya, os_, ls_, zb_src, zb_col, ym, w, x, tm=512):
    m, d = x.shape
    row = lambda width, col=0: pl.BlockSpec((tm, width), lambda i: (i, col))
    heads = [o for group in os_ for o in group]
    return pl.pallas_call(
        _hawk_out_kernel,
        grid=(m // tm,),
        in_specs=[row(LRU_WIDTH)] + [row(DIL_HEAD_DIM)] * len(heads) + [row(LANES)] * len(ls_)
                 + [row(DIL_WIDTH, zb_col), row(MEM_WIDTH),
                    pl.BlockSpec(w.shape, lambda i: (0, 0)), row(d)],
        out_specs=row(d),
        out_shape=jax.ShapeDtypeStruct((m, d), F32),
        compiler_params=_params("arbitrary"),
        name="hawk_out",
    )(ya, *heads, *ls_, zb_src, ym, w, x)


def _compress_kernel(k_ref, v_ref, pe_ref, w1_ref, w2k_ref, w2vt_ref, ko_ref, vto_ref):
    n_blk = k_ref.shape[0] // CMP_STRIDE

    def hidden(which, src_ref):
        x = jnp.concatenate([src_ref[pl.ds(p, n_blk, stride=CMP_STRIDE), :] for p in range(CMP_STRIDE)],
                            axis=1).astype(BF16)
        first = jnp.dot(x, w1_ref[which, 0], preferred_element_type=F32)
        second = jnp.dot(x, w1_ref[which, 1], preferred_element_type=F32)
        pe = (jnp.dot(pe_ref[which, 0], w1_ref[which, 0], preferred_element_type=F32)
              + jnp.dot(pe_ref[which, 1], w1_ref[which, 1], preferred_element_type=F32))
        return _silu(first + pltpu.roll(second, n_blk - 1, axis=0) + pe[0:1, :]).astype(BF16)

    act_k, act_v = hidden(0, k_ref), hidden(1, v_ref)
    part = lambda a, g: a[:, g * PHI_HIDDEN:(g + 1) * PHI_HIDDEN]
    ks = [jnp.dot(part(act_k, g), w2k_ref[...], preferred_element_type=F32) for g in range(NSA_KV_GROUPS)]
    vts = [_dot_t(w2vt_ref[...], part(act_v, g)) for g in range(NSA_KV_GROUPS)]
    ko_ref[...] = jnp.concatenate(ks, axis=1).astype(ko_ref.dtype)
    vto_ref[...] = jnp.concatenate(vts, axis=0).astype(vto_ref.dtype)


def _compress(src, k_col, v_col, pe_k, pe_v, k_w1, k_w2, v_w1, v_w2, batch, seq):
    half = CMP_BLOCK // 2
    assert half == CMP_STRIDE and NSA_KV == LANES
    n_blk = seq // CMP_STRIDE
    hd = NSA_HEAD_DIM
    w1 = jnp.stack([k_w1, v_w1]).reshape(2, 2, half, hd, PHI_HIDDEN).astype(BF16)
    zero = jnp.zeros_like(w1)
    per_group = [jnp.concatenate([w1 if g == col else zero for col in range(NSA_KV_GROUPS)], axis=-1)
                 for g in range(NSA_KV_GROUPS)]
    w1e = jnp.stack(per_group, axis=3).reshape(2, 2, half * NSA_KV, NSA_KV_GROUPS * PHI_HIDDEN)
    pe = jnp.stack([pe_k, pe_v]).reshape(2, 2, half, 1, hd)
    pe = jnp.broadcast_to(pe, (2, 2, half, NSA_KV_GROUPS, hd)).reshape(2, 2, 1, half * NSA_KV)
    pe = jnp.broadcast_to(pe, (2, 2, SUBLANES, half * NSA_KV)).astype(BF16)
    w2k = k_w2.astype(BF16)
    w2vt = v_w2.T.astype(BF16)
    whole = lambda a: pl.BlockSpec(a.shape, lambda b: (0,) * a.ndim)
    return pl.pallas_call(
        _compress_kernel,
        grid=(batch,),
        in_specs=[pl.BlockSpec((seq, LANES), lambda b: (b, k_col)),
                  pl.BlockSpec((seq, LANES), lambda b: (b, v_col)),
                  whole(pe), whole(w1e), whole(w2k), whole(w2vt)],
        out_specs=[pl.BlockSpec((None, n_blk, NSA_KV), lambda b: (b, 0, 0)),
                   pl.BlockSpec((None, NSA_KV, n_blk), lambda b: (b, 0, 0))],
        out_shape=[jax.ShapeDtypeStruct((batch, n_blk, NSA_KV), BF16),
                   jax.ShapeDtypeStruct((batch, NSA_KV, n_blk), BF16)],
        compiler_params=_params("arbitrary"),
        name="compress",
    )(src, src, pe, w1e, w2k, w2vt)


KEY_CHUNK = 256
NSA_TQ = 256
SLOPE_PIECES = 3
N_FEATS = 2 * SLOPE_PIECES
FEAT_LANES = 32
KEY_COLS = NSA_HEAD_DIM + 2 * FEAT_LANES
N_SLC = 32


def _slope_pieces(slope):
    rest = np.float32(slope)
    pieces = []
    for _ in range(SLOPE_PIECES):
        p = np.float32(np.asarray(rest).astype(BF16))
        pieces.append(float(p))
        rest = np.float32(rest - p)
    return pieces


def _lane_table(lane, values):
    out = jnp.zeros(lane.shape, F32)
    for idx, v in enumerate(values):
        out = jnp.where(lane == idx, v, out)
    return out


def _key_feats(pos_hi, pos_lo, lane):
    return jnp.where(lane < SLOPE_PIECES, pos_hi, jnp.where(lane < N_FEATS, pos_lo, 0)).astype(F32)


def _tile_heads(x):
    return jnp.concatenate([x] * NSA_R, axis=1)


def _nsa_kernel(q_ref, kc_ref, vct_ref, ksrc_ref, vsrc_ref, kwsrc_ref, vwsrc_ref, feat_ref, hot_ref,
                gl_ref, z_ref, ym_ref, w_ref, x_ref, fin_ref, o_ref,
                ks_ref, vst_ref, kw_ref, vwt_ref, s_ref, acc_ref, imp_ref):
    i = pl.program_id(1)
    tq = q_ref.shape[0]
    hd = NSA_HEAD_DIM
    n_cmp = kc_ref.shape[0]

    @pl.when(i == 0)
    def _():
        for g in range(NSA_KV_GROUPS):
            gs = slice(g * hd, (g + 1) * hd)
            for dst, src, tail in ((ks_ref, ksrc_ref, hot_ref[...]),
                                   (kw_ref, kwsrc_ref, jnp.zeros(hot_ref.shape, BF16))):
                dst[:, g * KEY_COLS:g * KEY_COLS + hd] = src[:, gs]
                dst[:, g * KEY_COLS + hd:g * KEY_COLS + hd + FEAT_LANES] = feat_ref[...]
                dst[:, g * KEY_COLS + hd + FEAT_LANES:(g + 1) * KEY_COLS] = tail
        for c in range(vst_ref.shape[0]):
            rows = slice(c * KEY_CHUNK, (c + 1) * KEY_CHUNK)
            vst_ref[c] = vsrc_ref[rows, :].astype(F32).T.astype(BF16)
            vwt_ref[c] = vwsrc_ref[rows, :].astype(F32).T.astype(BF16)

    slopes_all = _alibi_slopes(NSA_HEADS)
    gates_t = jax.nn.sigmoid(gl_ref[...]).T
    feat_lane = lax.broadcasted_iota(jnp.int32, (tq, FEAT_LANES), 1)
    no_sel = jnp.zeros((NSA_R * tq, FEAT_LANES), BF16)
    key_row = lax.broadcasted_iota(jnp.int32, (KEY_CHUNK, tq), 0)
    t_pos = i * tq + lax.broadcasted_iota(jnp.int32, (KEY_CHUNK, tq), 1)
    ones_rows = jnp.ones((16, KEY_CHUNK), BF16)
    win_lo = jnp.maximum(i * tq - (WIN_SIZE - 1), 0) // KEY_CHUNK
    chunks_hi = (i * tq + tq - 1) // KEY_CHUNK + 1

    groups = range(NSA_KV_GROUPS)
    q_win, q_slc, o_cmp = [], [], []
    for g in groups:
        slopes = slopes_all[g * NSA_R:(g + 1) * NSA_R]
        gs = slice(g * hd, (g + 1) * hd)
        q_parts = []
        for r in range(NSA_R):
            qr = q_ref[:, (g * NSA_R + r) * hd:(g * NSA_R + r + 1) * hd]
            feats = _lane_table(feat_lane, _slope_pieces(slopes[r]) * 2).astype(BF16)
            q_parts.append(jnp.concatenate([qr, feats], axis=1))
        q_feat = jnp.concatenate(q_parts, axis=0)
        q_aug = jnp.concatenate([q_feat, no_sel], axis=1)

        n_row = lax.broadcasted_iota(jnp.int32, (n_cmp, tq), 0)
        t_cmp = i * tq + lax.broadcasted_iota(jnp.int32, (n_cmp, tq), 1)
        visible = t_cmp >= n_row * CMP_STRIDE + (CMP_BLOCK - 1)
        cfeat_row = lax.broadcasted_iota(jnp.int32, (n_cmp, 2 * FEAT_LANES), 0)
        cfeat_lane = lax.broadcasted_iota(jnp.int32, (n_cmp, 2 * FEAT_LANES), 1)
        kc_feats = _key_feats(cfeat_row * CMP_STRIDE, 0, cfeat_lane)
        kc_aug = jnp.concatenate([kc_ref[:, gs], kc_feats.astype(BF16)], axis=1)
        s = _dot_t(kc_aug, q_aug) + _tile_heads(jnp.where(visible, 0.0, NEG_INF))
        m = jnp.max(s, axis=0, keepdims=True)
        e = jnp.exp(s - m)
        t_one = i * tq + lax.broadcasted_iota(jnp.int32, (1, NSA_R * tq), 1) % tq
        any_visible = t_one >= (CMP_BLOCK - 1)
        p = e * jnp.where(any_visible, 1.0 / jnp.sum(e, axis=0, keepdims=True), 0.0)
        o_cmp.append(jnp.dot(vct_ref[gs, :], p.astype(BF16), preferred_element_type=F32))
        p_sum = p[:, 0:tq]
        for r in range(1, NSA_R):
            p_sum = p_sum + p[:, r * tq:(r + 1) * tq]

        band = p_sum + pltpu.roll(p_sum, 1, axis=0)
        for k in range(1, CMP_PER_SLC):
            band = band + pltpu.roll(p_sum, n_cmp - k, axis=0)
        halves = []
        for half in range(tq // LANES):
            imp_ref[...] = band[:, half * LANES:(half + 1) * LANES]
            halves.append(imp_ref[pl.ds(0, N_SLC, stride=CMP_PER_SLC), :])
        imp = jnp.concatenate(halves, axis=1)
        blk_j = lax.broadcasted_iota(jnp.int32, (N_SLC, tq), 0)
        cur = (i * tq + lax.broadcasted_iota(jnp.int32, (N_SLC, tq), 1)) // SLC_BLOCK
        forced = (blk_j == 0) | (blk_j == cur) | (blk_j == cur - 1)
        v_imp = jnp.where(forced, SEL_FORCE, jnp.where(blk_j > cur, -SEL_FORCE, imp))
        rank = jnp.zeros((N_SLC, tq), F32)
        for other in range(N_SLC):
            row = v_imp[other:other + 1, :]
            ahead = (row > v_imp) | ((row == v_imp) & (blk_j > other))
            rank = rank + jnp.where(ahead, 1.0, 0.0)
        sel_bias = jnp.where(rank < SLC_TOP_N, 0.0, NEG_INF)

        padded = jnp.concatenate([sel_bias, jnp.zeros((LANES - N_SLC, tq), F32)], axis=0)
        sel_t = padded.T[:, 0:FEAT_LANES].astype(BF16)
        q_win.append(q_aug)
        q_slc.append(jnp.concatenate([q_feat, jnp.concatenate([sel_t] * NSA_R, axis=0)], axis=1))

    def attend(q_brs, k_ref, key_cols, vt_ref, lo, hi, masked_from, mask_fn):
        def scores(c, m_run, masked):
            start = pl.multiple_of(c * KEY_CHUNK, KEY_CHUNK)
            scs = [_dot_t(k_ref[pl.ds(start, KEY_CHUNK), g * key_cols:(g + 1) * key_cols], q_brs[g])
                   for g in groups]
            if masked:
                bias = _tile_heads(jnp.where(mask_fn(t_pos - (start + key_row)), 0.0, NEG_INF))
                scs = [sc + bias for sc in scs]
            for g in groups:
                s_ref[g, c] = scs[g]
            return tuple(jnp.maximum(m_run[g], jnp.max(scs[g], axis=0, keepdims=True)) for g in groups)

        m_rows = tuple(jnp.full((1, NSA_R * tq), NEG_INF, F32) for _ in groups)
        m_rows = lax.fori_loop(lo, masked_from, functools.partial(scores, masked=False), m_rows)
        m_rows = lax.fori_loop(masked_from, hi, functools.partial(scores, masked=True), m_rows)
        acc_ref[...] = jnp.zeros(acc_ref.shape, F32)

        def weighted(c, carry):
            for g in groups:
                e = jnp.exp(s_ref[g, c] - m_rows[g]).astype(BF16)
                v_ext = jnp.concatenate([vt_ref[c, g * hd:(g + 1) * hd, :], ones_rows], axis=0)
                acc_ref[g] += jnp.dot(v_ext, e, preferred_element_type=F32)
            return carry

        lax.fori_loop(lo, hi, weighted, 0)
        return [acc_ref[g, 0:hd, :] / acc_ref[g, hd:hd + 1, :] for g in groups]

    o_slc = attend(q_slc, ks_ref, KEY_COLS, vst_ref, 0, chunks_hi, (i * tq) // KEY_CHUNK,
                   lambda dist: dist >= 0)
    o_win = attend(q_win, kw_ref, KEY_COLS, vwt_ref, win_lo, chunks_hi, win_lo,
                   lambda dist: (dist >= 0) & (dist <= WIN_SIZE - 1))

    y = jnp.dot(ym_ref[...], w_ref[NSA_WIDTH:NSA_WIDTH + MEM_WIDTH, :], preferred_element_type=F32)
    for g in groups:
        def gate(kind):
            base = g * NSA_R * 3 + kind
            return jnp.concatenate([gates_t[base + 3 * r:base + 3 * r + 1, :] for r in range(NSA_R)], axis=1)

        o = gate(0) * o_cmp[g] + gate(1) * o_slc[g] + gate(2) * o_win[g]
        pairs = []
        for r in range(0, NSA_R, 2):
            two = jnp.concatenate([o[:, r * tq:(r + 1) * tq], o[:, (r + 1) * tq:(r + 2) * tq]], axis=0)
            pairs.append(two.T)
        cs = slice(g * NSA_R * hd, (g + 1) * NSA_R * hd)
        yo = (jnp.concatenate(pairs, axis=1) * _silu(z_ref[:, cs].astype(F32))).astype(BF16)
        y = y + jnp.dot(yo, w_ref[cs, :], preferred_element_type=F32)

    x = x_ref[...] + y
    ms = jnp.mean(x * x, axis=-1, keepdims=True)
    o_ref[...] = x * lax.rsqrt(ms + NORM_EPS) * fin_ref[...]


def _key_pos_feats(seq):
    assert seq // SLC_BLOCK == N_SLC <= FEAT_LANES
    pos = np.arange(seq)
    feats = np.zeros((seq, FEAT_LANES), np.float32)
    feats[:, 0:SLOPE_PIECES] = ((pos // SLC_BLOCK) * SLC_BLOCK)[:, None]
    feats[:, SLOPE_PIECES:N_FEATS] = (pos % SLC_BLOCK)[:, None]
    onehot = (np.arange(FEAT_LANES)[None, :] == (pos // SLC_BLOCK)[:, None]).astype(np.float32)
    return jnp.asarray(feats, BF16), jnp.asarray(onehot, BF16)


def _nsa_attn(nb, kv_col0, z_col, k_cmp, v_cmp_t, nf, gl_col, ym, w_out, x, final_g, batch, seq):
    tq = NSA_TQ
    nq = seq // tq
    d = x.shape[1]
    feats, onehot = _key_pos_feats(seq)
    kv_blk = kv_col0 // NSA_KV
    seq_cols = lambda col: pl.BlockSpec((seq, NSA_KV), lambda b, i: (b, kv_blk + col))
    const = lambda a: pl.BlockSpec(a.shape, lambda b, i: (0,) * a.ndim)
    per_batch = lambda a: pl.BlockSpec((None,) + a.shape[1:], lambda b, i: (b,) + (0,) * (a.ndim - 1))
    rows = lambda width, col=0: pl.BlockSpec((tq, width), lambda b, i: (b * nq + i, col))
    fin = final_g.reshape(1, d)
    return pl.pallas_call(
        _nsa_kernel,
        grid=(batch, nq),
        in_specs=[rows(NSA_WIDTH), per_batch(k_cmp), per_batch(v_cmp_t),
                  seq_cols(0), seq_cols(1), seq_cols(2), seq_cols(3), const(feats), const(onehot),
                  rows(LANES, gl_col), rows(NSA_WIDTH, z_col), rows(MEM_WIDTH), const(w_out), rows(d),
                  const(fin)],
        out_specs=rows(d),
        out_shape=jax.ShapeDtypeStruct((batch * seq, d), F32),
        scratch_shapes=[pltpu.VMEM((seq, NSA_KV_GROUPS * KEY_COLS), BF16),
                        pltpu.VMEM((seq // KEY_CHUNK, NSA_KV, KEY_CHUNK), BF16),
                        pltpu.VMEM((seq, NSA_KV_GROUPS * KEY_COLS), BF16),
                        pltpu.VMEM((seq // KEY_CHUNK, NSA_KV, KEY_CHUNK), BF16),
                        pltpu.VMEM((NSA_KV_GROUPS, seq // KEY_CHUNK, KEY_CHUNK, NSA_R * tq), F32),
                        pltpu.VMEM((NSA_KV_GROUPS, NSA_HEAD_DIM + 16, NSA_R * tq), F32),
                        pltpu.VMEM((seq // CMP_STRIDE, LANES), F32)],
        compiler_params=_params("arbitrary", "arbitrary"),
        name="nsa_attn",
    )(nb, k_cmp, v_cmp_t, nb, nb, nb, nb, feats, onehot, nf, nb, ym, w_out, x, fin)


def _hawk_layer(x, mem, batch, seq, norm_g, w_in, conv_w, conv_b, ga_w, ga_b, gx_w, gx_b, lam,
                mem_norm_g, w_mem_kv, w_out):
    xa0, za0 = 0, LRU_WIDTH
    q0 = 2 * LRU_WIDTH
    k0, v0 = q0 + DIL_QKV, q0 + 2 * DIL_QKV
    zb0 = q0 + 3 * DIL_QKV
    qm0 = zb0 + DIL_WIDTH
    zm0 = qm0 + MEM_WIDTH

    def qkv_cols(gi):
        return [(base + gi * DIL_WIDTH, DIL_WIDTH) for base in (q0, k0, v0)]

    w = w_in.astype(BF16)
    nat_cols = [(za0, LRU_WIDTH), *qkv_cols(0), (zb0, DIL_WIDTH), (qm0, MEM_WIDTH), (zm0, MEM_WIDTH)]
    xa, hb = _norm_matmul(x, norm_g, w, [(F32, [(xa0, LRU_WIDTH)]), (BF16, nat_cols)])
    za_col = 0
    qkv0_col = LRU_WIDTH // DIL_WIDTH
    zb_col = (LRU_WIDTH + 3 * DIL_WIDTH) // DIL_WIDTH
    qm_col = (LRU_WIDTH + 4 * DIL_WIDTH) // MEM_WIDTH
    zm_col = qm_col + 1
    qkv = [(hb[None], qkv0_col)]
    for gi in range(1, len(DIL_GROUPS)):
        qkv.append((_norm_matmul(x, norm_g, w, [(BF16, qkv_cols(gi))], dil=DIL_GROUPS[gi][1]), 0))
    n_mem = mem.shape[0] // batch
    mem_kv, = _norm_matmul(mem, mem_norm_g, w_mem_kv.astype(BF16), [(BF16, [(0, 2 * MEM_WIDTH)])])

    ya = _rglru(xa, hb, za_col, conv_w, conv_b, _pack_block_diag(ga_w), ga_b, _pack_block_diag(gx_w), gx_b,
                lam, batch, seq)
    attn = [_dil_attn(arr, col0, gi, batch, seq) for gi, (arr, col0) in enumerate(qkv)]
    ym = _mem_attn(hb, qm_col, mem_kv, hb, zm_col, batch, seq, n_mem)
    return _hawk_out(ya, [o for o, _ in attn], [l for _, l in attn], hb, zb_col, ym,
                     w_out.astype(BF16), x)


def _nsa_layer(x, mem, batch, seq, norm_g, w_in, pe_k, pe_v, phik_w1, phik_w2, phiv_w1, phiv_w2,
               mem_norm_g, w_mem_kv, w_out, final_g):
    kv0 = NSA_WIDTH
    gl0 = kv0 + 6 * NSA_KV
    z0 = gl0 + 3 * NSA_HEADS
    qm0 = z0 + NSA_WIDTH
    zm0 = qm0 + MEM_WIDTH
    gl_w = jnp.pad(w_in[:, gl0:z0], ((0, 0), (0, LANES - 3 * NSA_HEADS)))
    q_w = w_in[:, 0:kv0] * (NSA_HEAD_DIM ** -0.5)
    w_all = jnp.concatenate([q_w, w_in[:, z0:qm0], w_in[:, kv0 + 2 * NSA_KV:gl0], w_in[:, qm0:zm0 + MEM_WIDTH],
                             gl_w, w_in[:, kv0:kv0 + 2 * NSA_KV]], axis=1).astype(BF16)
    f32_width = LANES + 2 * NSA_KV
    bf16_width = w_all.shape[1] - f32_width
    nb, nf = _norm_matmul(x, norm_g, w_all, [(BF16, [(0, bf16_width)]), (F32, [(bf16_width, f32_width)])])
    gl_col, kc_col, vc_col = 0, 1, 2
    z_col = 1
    kv_col0 = 2 * NSA_WIDTH
    qm_col = (kv_col0 + 4 * NSA_KV) // MEM_WIDTH
    zm_col = qm_col + 1
    n_mem = mem.shape[0] // batch
    mem_kv, = _norm_matmul(mem, mem_norm_g, w_mem_kv.astype(BF16), [(BF16, [(0, 2 * MEM_WIDTH)])])

    k_cmp, v_cmp_t = _compress(nf, kc_col, vc_col, pe_k, pe_v, phik_w1, phik_w2, phiv_w1, phiv_w2,
                               batch, seq)
    ym = _mem_attn(nb, qm_col, mem_kv, nb, zm_col, batch, seq, n_mem)
    return _nsa_attn(nb, kv_col0, z_col, k_cmp, v_cmp_t, nf, gl_col, ym, w_out.astype(BF16), x, final_g,
                     batch, seq)


def kernel(x, mem, hawk_norm, hawk_w_in, hawk_conv_w, hawk_conv_b, hawk_gate_a_w, hawk_gate_a_b,
           hawk_gate_x_w, hawk_gate_x_b, hawk_lambda, hawk_mem_norm, hawk_w_mem_kv, hawk_w_out,
           nsa_norm, nsa_w_in, nsa_pe_k, nsa_pe_v, nsa_phi_k_w1, nsa_phi_k_w2, nsa_phi_v_w1,
           nsa_phi_v_w2, nsa_mem_norm, nsa_w_mem_kv, nsa_w_out, final_norm):
    batch, seq, d = x.shape
    assert hawk_norm.shape[0] == 1 and nsa_norm.shape[0] == 1, "one layer of each kind"
    assert seq % (ATTN_BLOCK * DIL_GROUPS[-1][1]) == 0
    x2 = x.reshape(batch * seq, d)
    mem2 = mem.reshape(batch * mem.shape[1], d)
    x2 = _hawk_layer(x2, mem2, batch, seq, hawk_norm[0], hawk_w_in[0], hawk_conv_w[0], hawk_conv_b[0],
                     hawk_gate_a_w[0], hawk_gate_a_b[0].reshape(-1), hawk_gate_x_w[0],
                     hawk_gate_x_b[0].reshape(-1), hawk_lambda[0], hawk_mem_norm[0], hawk_w_mem_kv[0],
                     hawk_w_out[0])
    out = _nsa_layer(x2, mem2, batch, seq, nsa_norm[0], nsa_w_in[0], nsa_pe_k[0], nsa_pe_v[0],
                     nsa_phi_k_w1[0], nsa_phi_k_w2[0], nsa_phi_v_w1[0], nsa_phi_v_w2[0],
                     nsa_mem_norm[0], nsa_w_mem_kv[0], nsa_w_out[0], final_norm)
    return out.reshape(batch, seq, d)
```

```python
import functools

import numpy as np
import jax
import jax.numpy as jnp
from jax import lax
from jax.experimental import pallas as pl
from jax.experimental.pallas import tpu as pltpu

F32 = jnp.float32
BF16 = jnp.bfloat16

NORM_EPS = 1e-6
NEG_INF = -1e30
LANES = 128
SUBLANES = 8
ATTN_BLOCK = 128
VMEM_LIMIT = 56 * 1024 * 1024

LRU_WIDTH = 1024
LRU_BLOCKS = 16
LRU_BLOCK_DIM = LRU_WIDTH // LRU_BLOCKS
LRU_PACK = 256
CONV_WIDTH = 4
LRU_C = 8.0

DIL_GROUPS = ((128, 1), (512, 4), (2048, 16))
DIL_HEADS = 4
DIL_HEAD_DIM = 128
DIL_WIDTH = DIL_HEADS * DIL_HEAD_DIM
DIL_QKV = len(DIL_GROUPS) * DIL_WIDTH
LSE_LANES = LANES // DIL_HEADS

MEM_HEADS = 4
MEM_HEAD_DIM = 64
MEM_WIDTH = MEM_HEADS * MEM_HEAD_DIM

NSA_HEADS = 16
NSA_KV_GROUPS = 2
NSA_R = NSA_HEADS // NSA_KV_GROUPS
NSA_HEAD_DIM = 64
NSA_WIDTH = NSA_HEADS * NSA_HEAD_DIM
NSA_KV = NSA_KV_GROUPS * NSA_HEAD_DIM
CMP_BLOCK = 32
CMP_STRIDE = 16
SLC_BLOCK = 64
SLC_TOP_N = 8
WIN_SIZE = 512
PHI_HIDDEN = 256
SEL_FORCE = 1e6
CMP_PER_SLC = SLC_BLOCK // CMP_STRIDE


def _alibi_slopes(n):
    return [float(v) for v in np.exp2(-8.0 * np.arange(1, n + 1) / n).astype(np.float32)]


def _params(*semantics):
    return pltpu.CompilerParams(dimension_semantics=semantics, vmem_limit_bytes=VMEM_LIMIT)


def _silu(z):
    return z * jax.nn.sigmoid(z)


def _dot_t(a, b):
    return lax.dot_general(a, b, (((1,), (1,)), ((), ())), preferred_element_type=F32)


def _rms_norm_rows(x, g):
    ms = jnp.mean(x * x, axis=-1, keepdims=True)
    return (x * lax.rsqrt(ms + NORM_EPS) * g).astype(BF16)


def _norm_matmul_kernel(*refs, dil, pieces):
    def project(xn, out_ref, w_refs, by_class=False):
        col = 0
        for w_ref in w_refs:
            width = w_ref.shape[1]
            res = jnp.dot(xn, w_ref[...], preferred_element_type=F32).astype(out_ref.dtype)
            if by_class:
                per = xn.shape[0] // dil
                for c in range(dil):
                    out_ref[c, :, col:col + width] = res[c * per:(c + 1) * per]
            else:
                out_ref[:, col:col + width] = res
            col += width

    if dil == 1:
        x_ref, g_ref = refs[:2]
        w_refs, o_refs = refs[2:2 + sum(pieces)], refs[2 + sum(pieces):]
        xn = _rms_norm_rows(x_ref[...], g_ref[...])
        first = 0
        for o_ref, n_pieces in zip(o_refs, pieces):
            project(xn, o_ref, w_refs[first:first + n_pieces])
            first += n_pieces
        return
    if dil % SUBLANES == 0:
        x_ref, g_ref, *w_refs, o_ref = refs
        tm, k = x_ref.shape
        x = x_ref[...]
        xn = x * lax.rsqrt(jnp.mean(x * x, axis=-1, keepdims=True) + NORM_EPS) * g_ref[...]
        xn = jnp.swapaxes(xn.reshape(tm // dil, dil, k), 0, 1).reshape(tm, k).astype(BF16)
        project(xn, o_ref, w_refs, by_class=True)
        return
    n_w = sum(pieces)
    x_refs, g_ref, w_refs = refs[:-(n_w + 3)], refs[-(n_w + 3)], refs[-(n_w + 2):-2]
    o_ref, xn_ref = refs[-2:]
    per = x_refs[0].shape[0] // dil
    inv_k = 1.0 / (len(x_refs) * LANES)
    for c in range(dil):
        xs = [x_ref[pl.ds(c, per, stride=dil), :] for x_ref in x_refs]
        ss = xs[0] * xs[0]
        for xj in xs[1:]:
            ss = ss + xj * xj
        r = lax.rsqrt(jnp.sum(ss, axis=-1, keepdims=True) * inv_k + NORM_EPS)
        for j, xj in enumerate(xs):
            cols = slice(j * LANES, (j + 1) * LANES)
            xn_ref[c * per:(c + 1) * per, cols] = (xj * r * g_ref[:, cols]).astype(BF16)
    project(xn_ref[...], o_ref, w_refs, by_class=True)


def _norm_matmul(x, g, w, outs, dil=1, tm=512):
    m, k = x.shape
    tm = min(tm, m)
    assert m % tm == 0 and k % LANES == 0
    pieces = [len(cols) for _, cols in outs]
    widths = [sum(width for _, width in cols) for _, cols in outs]
    w_specs = []
    for _, cols in outs:
        for col, width in cols:
            assert col % width == 0 and width % LANES == 0
            w_specs.append(pl.BlockSpec((k, width), functools.partial(lambda i, j: (0, j), j=col // width)))
    resident = [pl.BlockSpec((1, k), lambda i: (0, 0))] + w_specs
    operands = (g.reshape(1, k),) + (w,) * len(w_specs)
    kernel = functools.partial(_norm_matmul_kernel, dil=dil, pieces=pieces)
    if dil == 1:
        return pl.pallas_call(
            kernel,
            grid=(m // tm,),
            in_specs=[pl.BlockSpec((tm, k), lambda i: (i, 0))] + resident,
            out_specs=[pl.BlockSpec((tm, width), lambda i: (i, 0)) for width in widths],
            out_shape=[jax.ShapeDtypeStruct((m, width), dtype) for width, (dtype, _) in zip(widths, outs)],
            compiler_params=_params("arbitrary"),
            name="norm_matmul",
        )(x, *operands)
    per = tm // dil
    (out_dtype, _), = outs
    n, = widths
    assert tm % dil == 0 and per % 16 == 0
    if dil % SUBLANES == 0:
        x_specs, scratch = [pl.BlockSpec((tm, k), lambda i: (i, 0))], []
    else:
        x_specs = [pl.BlockSpec((tm, LANES), functools.partial(lambda i, j: (i, j), j=j)) for j in range(k // LANES)]
        scratch = [pltpu.VMEM((tm, k), BF16)]
    return pl.pallas_call(
        kernel,
        grid=(m // tm,),
        in_specs=x_specs + resident,
        out_specs=pl.BlockSpec((dil, per, n), lambda i: (0, i, 0)),
        out_shape=jax.ShapeDtypeStruct((dil, m // dil, n), out_dtype),
        scratch_shapes=scratch,
        compiler_params=_params("arbitrary"),
        name="norm_matmul_dil",
    )(*([x] * len(x_specs)), *operands)


def _rglru_kernel(xa_ref, za_ref, cw_ref, cb_ref, wa_ref, ba_ref, wx_ref, bx_ref, lam_ref,
                  o_ref, xpad_ref, h_ref):
    t = pl.program_id(1)
    tt, width = xa_ref.shape
    halo = 8

    @pl.when(t == 0)
    def _():
        xpad_ref[0:halo, :] = jnp.zeros((halo, width), F32)
        h_ref[...] = jnp.zeros_like(h_ref)

    x = xa_ref[...]
    xpad_ref[halo:halo + tt, :] = x
    cw = cw_ref[...]
    y = cw[CONV_WIDTH - 1:CONV_WIDTH] * x
    for k in range(1, CONV_WIDTH):
        y = y + cw[CONV_WIDTH - 1 - k:CONV_WIDTH - k] * xpad_ref[halo - k:halo - k + tt, :]
    y = y + cb_ref[...]
    xpad_ref[0:halo, :] = x[tt - halo:tt, :]

    yb = y.astype(BF16)
    r_parts, i_parts = [], []
    for p in range(width // LRU_PACK):
        ys = yb[:, p * LRU_PACK:(p + 1) * LRU_PACK]
        r_parts.append(jnp.dot(ys, wa_ref[p], preferred_element_type=F32))
        i_parts.append(jnp.dot(ys, wx_ref[p], preferred_element_type=F32))
    r = jax.nn.sigmoid(jnp.concatenate(r_parts, axis=1) + ba_ref[...])
    gi = jax.nn.sigmoid(jnp.concatenate(i_parts, axis=1) + bx_ref[...])

    nl = -lam_ref[...]
    softplus = jnp.maximum(nl, 0.0) + jnp.log1p(jnp.exp(-jnp.abs(nl)))
    log_a = (-LRU_C) * r * softplus
    a = jnp.exp(log_a)
    mult = jnp.sqrt(-jnp.tanh(log_a) * (a * a + 1.0))
    b = y * gi * mult
    first = (lax.broadcasted_iota(jnp.int32, (SUBLANES, width), 0) == 0) & (t == 0)
    b = jnp.concatenate([jnp.where(first, (y * gi)[0:SUBLANES], b[0:SUBLANES]), b[SUBLANES:]], axis=0)

    groups = tt // SUBLANES
    a3 = a.reshape(groups, SUBLANES, width)
    b3 = b.reshape(groups, SUBLANES, width)
    sub = lax.broadcasted_iota(jnp.int32, (groups, SUBLANES, width), 1)
    k = 1
    while k < SUBLANES:
        keep = sub >= k
        a_sh = jnp.where(keep, pltpu.roll(a3, k, axis=1), 1.0)
        b_sh = jnp.where(keep, pltpu.roll(b3, k, axis=1), 0.0)
        b3 = a3 * b_sh + b3
        a3 = a3 * a_sh
        k *= 2
    carry = jnp.broadcast_to(h_ref[...], (SUBLANES, width))
    hs = []
    for gidx in range(groups):
        hg = a3[gidx] * carry + b3[gidx]
        hs.append(hg)
        carry = jnp.broadcast_to(hg[SUBLANES - 1:SUBLANES], (SUBLANES, width))
    h = jnp.concatenate(hs, axis=0)
    h_ref[...] = carry[0:1]
    o_ref[...] = (h * _silu(za_ref[...].astype(F32))).astype(o_ref.dtype)


def _rglru(xa, za_src, za_col, conv_w, conv_b, wa, ba, wx, bx, lam, batch, seq, tt=512):
    width = LRU_WIDTH
    nt = seq // tt
    packs = width // LRU_PACK
    vec = pl.BlockSpec((1, width), lambda b, t: (0, 0))
    gate_w = pl.BlockSpec((packs, LRU_PACK, LRU_PACK), lambda b, t: (0, 0, 0))
    return pl.pallas_call(
        _rglru_kernel,
        grid=(batch, nt),
        in_specs=[pl.BlockSpec((tt, width), lambda b, t: (b * nt + t, 0)),
                  pl.BlockSpec((tt, width), lambda b, t: (b * nt + t, za_col)),
                  pl.BlockSpec((CONV_WIDTH, width), lambda b, t: (0, 0)),
                  vec, gate_w, vec, gate_w, vec, vec],
        out_specs=pl.BlockSpec((tt, width), lambda b, t: (b * nt + t, 0)),
        out_shape=jax.ShapeDtypeStruct((batch * seq, width), BF16),
        scratch_shapes=[pltpu.VMEM((tt + 8, width), F32), pltpu.VMEM((1, width), F32)],
        compiler_params=_params("arbitrary", "arbitrary"),
        name="rglru",
    )(xa, za_src, conv_w, conv_b.reshape(1, width), wa, ba.reshape(1, width), wx, bx.reshape(1, width),
      lam.reshape(1, width))


def _pack_block_diag(w):
    per = LRU_PACK // LRU_BLOCK_DIM
    w = w.reshape(LRU_BLOCKS // per, per, LRU_BLOCK_DIM, LRU_BLOCK_DIM)
    eye = jnp.eye(per, dtype=w.dtype)
    packed = w[:, :, :, None, :] * eye[None, :, None, :, None]
    return packed.reshape(LRU_BLOCKS // per, LRU_PACK, LRU_PACK).astype(BF16)


def _dil_attn_kernel(*refs, slopes, pos_scale, max_dist, has_halo, dil, n_cls, n_blk):
    if has_halo:
        q_ref, kh_ref, k_ref, vh_ref, v_ref = refs[:5]
        out_refs = refs[5:]
    else:
        q_ref, k_ref, v_ref = refs[:3]
        out_refs = refs[3:]
    n_out = DIL_HEADS + 1
    dst_refs = out_refs[:n_out]
    stage_refs = out_refs[n_out:] if dil > 1 else dst_refs
    first_super = pl.program_id(1) == 0
    cls0 = pl.program_id(2) * n_cls
    blk = ATTN_BLOCK
    scale = DIL_HEAD_DIM ** -0.5

    def band(width, halo_live):
        row = lax.broadcasted_iota(jnp.int32, (blk, width), 0)
        col = lax.broadcasted_iota(jnp.int32, (blk, width), 1)
        dist = (width - blk) + row - col
        valid = (dist >= 0) & (dist <= max_dist)
        if halo_live is not None:
            valid = valid & ((col >= blk) | halo_live)
        distf = (dist * pos_scale).astype(F32)
        return [jnp.where(valid, -slope * distf, NEG_INF) for slope in slopes]

    bias_inner = band(2 * blk, None) if n_blk > 1 else None
    bias_first = band(2 * blk, jnp.logical_not(first_super)) if has_halo else band(blk, None)

    def scores(cc, jb):
        cur = slice(jb * blk, (jb + 1) * blk)
        bias = bias_inner if jb > 0 else bias_first
        out = []
        for h in range(DIL_HEADS):
            hs = slice(h * DIL_HEAD_DIM, (h + 1) * DIL_HEAD_DIM)
            q = q_ref[cc, cur, hs]
            if jb > 0:
                k = k_ref[cc, (jb - 1) * blk:(jb + 1) * blk, hs]
                v = v_ref[cc, (jb - 1) * blk:(jb + 1) * blk, hs]
            elif has_halo:
                k = jnp.concatenate([kh_ref[cc, :, hs], k_ref[cc, cur, hs]], axis=0)
                v = jnp.concatenate([vh_ref[cc, :, hs], v_ref[cc, cur, hs]], axis=0)
            else:
                k, v = k_ref[cc, cur, hs], v_ref[cc, cur, hs]
            out.append((_dot_t(q, k) * scale + bias[h], v))
        return out

    def finish(cc, jb, pairs):
        where = (cls0 + cc, slice(jb * blk, (jb + 1) * blk)) if dil > 1 else (slice(jb * blk, (jb + 1) * blk),)
        lses = []
        for h, (s, v) in enumerate(pairs):
            m = jnp.max(s, axis=-1, keepdims=True)
            e = jnp.exp(s - m)
            den = jnp.sum(e, axis=-1, keepdims=True)
            o = jnp.dot(e.astype(BF16), v, preferred_element_type=F32) / den
            stage_refs[h][where] = o.astype(stage_refs[h].dtype)
            lses.append(jnp.broadcast_to(m + jnp.log(den), (blk, LSE_LANES)))
        stage_refs[DIL_HEADS][where] = jnp.concatenate(lses, axis=1)

    pending = None
    for cc in range(n_cls):
        for jb in range(n_blk):
            pairs = scores(cc, jb)
            if pending is not None:
                finish(*pending)
            pending = (cc, jb, pairs)
    finish(*pending)

    if dil > 1:
        @pl.when(pl.program_id(2) == pl.num_programs(2) - 1)
        def _():
            for stage, dst in zip(stage_refs, dst_refs):
                dst[...] = jnp.swapaxes(stage[...], 0, 1).reshape(dst.shape).astype(dst.dtype)


def _dil_attn(qkv, col0, gi, batch, seq, work=4):
    window, dil = DIL_GROUPS[gi]
    sub = seq // dil
    nb = sub // ATTN_BLOCK
    n_blk = min(work, nb)
    n_cls = min(work // n_blk, dil)
    n_super = nb // n_blk
    has_halo = n_super > 1
    span = n_blk * ATTN_BLOCK
    slopes = _alibi_slopes(len(DIL_GROUPS) * DIL_HEADS)[gi * DIL_HEADS:(gi + 1) * DIL_HEADS]
    cur = lambda col: pl.BlockSpec((n_cls, span, DIL_WIDTH), lambda b, i, c: (c, b * n_super + i, col0 + col))
    halo = lambda col: pl.BlockSpec(
        (n_cls, ATTN_BLOCK, DIL_WIDTH),
        lambda b, i, c: (c, jnp.maximum((b * n_super + i) * n_blk - 1, 0), col0 + col))
    if has_halo:
        in_specs = [cur(0), halo(1), cur(1), halo(2), cur(2)]
    else:
        in_specs = [cur(0), cur(1), cur(2)]
    n_out = DIL_HEADS + 1
    *o, lse = pl.pallas_call(
        functools.partial(_dil_attn_kernel, slopes=slopes, pos_scale=dil, max_dist=window // dil,
                          has_halo=has_halo, dil=dil, n_cls=n_cls, n_blk=n_blk),
        grid=(batch, n_super, dil // n_cls),
        in_specs=in_specs,
        out_specs=[pl.BlockSpec((span * dil, LANES), lambda b, i, c: (b * n_super + i, 0))] * n_out,
        out_shape=[jax.ShapeDtypeStruct((batch * seq, LANES), BF16)] * DIL_HEADS
                  + [jax.ShapeDtypeStruct((batch * seq, LANES), F32)],
        scratch_shapes=[pltpu.VMEM((dil, span, LANES), F32)] * (n_out if dil > 1 else 0),
        compiler_params=_params("arbitrary", "arbitrary", "arbitrary"),
        name=f"dil_attn_d{dil}",
    )(*([qkv] * len(in_specs)))
    return o, lse


def _mem_attn_kernel(q_ref, k_ref, v_ref, z_ref, o_ref, ks_ref, vt_ref):
    hd = MEM_HEAD_DIM
    n_mem = k_ref.shape[0]

    @pl.when(pl.program_id(1) == 0)
    def _():
        ks_ref[...] = (k_ref[...].astype(F32) * (hd ** -0.5)).astype(BF16)
        vt = v_ref[...].astype(F32).T.astype(BF16)
        for h in range(MEM_HEADS):
            vt_ref[h, 0:hd, :] = vt[h * hd:(h + 1) * hd, :]
            vt_ref[h, hd:, :] = jnp.ones((vt_ref.shape[1] - hd, n_mem), BF16)

    heads = [slice(h * hd, (h + 1) * hd) for h in range(MEM_HEADS)]
    scores = [_dot_t(ks_ref[:, hs], q_ref[:, hs]) for hs in heads]
    outs = []
    for h, s in enumerate(scores):
        e = jnp.exp(s - jnp.max(s, axis=0, keepdims=True)).astype(BF16)
        acc = jnp.dot(vt_ref[h], e, preferred_element_type=F32)
        outs.append(acc[0:hd, :] / acc[hd:hd + 1, :])
    o = jnp.concatenate(outs, axis=0).T
    o_ref[...] = (o * _silu(z_ref[...].astype(F32))).astype(o_ref.dtype)


def _mem_attn(qsrc, q_col, kv, zsrc, z_col, batch, seq, n_mem, tq=512):
    nq = seq // tq
    return pl.pallas_call(
        _mem_attn_kernel,
        grid=(batch, nq),
        in_specs=[pl.BlockSpec((tq, MEM_WIDTH), lambda b, i: (b * nq + i, q_col)),
                  pl.BlockSpec((n_mem, MEM_WIDTH), lambda b, i: (b, 0)),
                  pl.BlockSpec((n_mem, MEM_WIDTH), lambda b, i: (b, 1)),
                  pl.BlockSpec((tq, MEM_WIDTH), lambda b, i: (b * nq + i, z_col))],
        out_specs=pl.BlockSpec((tq, MEM_WIDTH), lambda b, i: (b * nq + i, 0)),
        out_shape=jax.ShapeDtypeStruct((batch * seq, MEM_WIDTH), BF16),
        scratch_shapes=[pltpu.VMEM((n_mem, MEM_WIDTH), BF16),
                        pltpu.VMEM((MEM_HEADS, MEM_HEAD_DIM + 16, n_mem), BF16)],
        compiler_params=_params("arbitrary", "arbitrary"),
        name="mem_attn",
    )(qsrc, kv, kv, zsrc)


def _hawk_out_kernel(*refs):
    n_groups = len(DIL_GROUPS)
    ya_ref = refs[0]
    o_refs = refs[1:1 + n_groups * DIL_HEADS]
    l_refs = refs[1 + n_groups * DIL_HEADS:1 + n_groups * (DIL_HEADS + 1)]
    zb_ref, ym_ref, w_ref, x_ref, out_ref = refs[1 + n_groups * (DIL_HEADS + 1):]
    a_end = LRU_WIDTH
    b_end = a_end + DIL_WIDTH
    y = jnp.dot(ya_ref[...], w_ref[0:a_end, :], preferred_element_type=F32)
    y = y + jnp.dot(ym_ref[...], w_ref[b_end:b_end + MEM_WIDTH, :], preferred_element_type=F32)
    parts = []
    for h in range(DIL_HEADS):
        ls = [l[:, h * LSE_LANES:h * LSE_LANES + 1] for l in l_refs]
        m = functools.reduce(jnp.maximum, ls)
        ws = [jnp.exp(l - m) for l in ls]
        num = sum(w * o_refs[gi * DIL_HEADS + h][...].astype(F32) for gi, w in enumerate(ws))
        parts.append(num / sum(ws))
    yb = (jnp.concatenate(parts, axis=1) * _silu(zb_ref[...].astype(F32))).astype(BF16)
    y = y + jnp.dot(yb, w_ref[a_end:b_end, :], preferred_element_type=F32)
    out_ref[...] = x_ref[...] + y


def _hawk_out(ya, os_, ls_, zb_src, zb_col, ym, w, x, tm=512):
    m, d = x.shape
    row = lambda width, col=0: pl.BlockSpec((tm, width), lambda i: (i, col))
    heads = [o for group in os_ for o in group]
    return pl.pallas_call(
        _hawk_out_kernel,
        grid=(m // tm,),
        in_specs=[row(LRU_WIDTH)] + [row(DIL_HEAD_DIM)] * len(heads) + [row(LANES)] * len(ls_)
                 + [row(DIL_WIDTH, zb_col), row(MEM_WIDTH),
                    pl.BlockSpec(w.shape, lambda i: (0, 0)), row(d)],
        out_specs=row(d),
        out_shape=jax.ShapeDtypeStruct((m, d), F32),
        compiler_params=_params("arbitrary"),
        name="hawk_out",
    )(ya, *heads, *ls_, zb_src, ym, w, x)


def _compress_kernel(k_ref, v_ref, pe_ref, w1_ref, w2k_ref, w2vt_ref, ko_ref, vto_ref):
    n_blk = k_ref.shape[0] // CMP_STRIDE

    def hidden(which, src_ref):
        x = jnp.concatenate([src_ref[pl.ds(p, n_blk, stride=CMP_STRIDE), :] for p in range(CMP_STRIDE)],
                            axis=1).astype(BF16)
        first = jnp.dot(x, w1_ref[which, 0], preferred_element_type=F32)
        second = jnp.dot(x, w1_ref[which, 1], preferred_element_type=F32)
        pe = (jnp.dot(pe_ref[which, 0], w1_ref[which, 0], preferred_element_type=F32)
              + jnp.dot(pe_ref[which, 1], w1_ref[which, 1], preferred_element_type=F32))
        return _silu(first + pltpu.roll(second, n_blk - 1, axis=0) + pe[0:1, :]).astype(BF16)

    act_k, act_v = hidden(0, k_ref), hidden(1, v_ref)
    part = lambda a, g: a[:, g * PHI_HIDDEN:(g + 1) * PHI_HIDDEN]
    ks = [jnp.dot(part(act_k, g), w2k_ref[...], preferred_element_type=F32) for g in range(NSA_KV_GROUPS)]
    vts = [_dot_t(w2vt_ref[...], part(act_v, g)) for g in range(NSA_KV_GROUPS)]
    ko_ref[...] = jnp.concatenate(ks, axis=1).astype(ko_ref.dtype)
    vto_ref[...] = jnp.concatenate(vts, axis=0).astype(vto_ref.dtype)


def _compress(src, k_col, v_col, pe_k, pe_v, k_w1, k_w2, v_w1, v_w2, batch, seq):
    half = CMP_BLOCK // 2
    assert half == CMP_STRIDE and NSA_KV == LANES
    n_blk = seq // CMP_STRIDE
    hd = NSA_HEAD_DIM
    w1 = jnp.stack([k_w1, v_w1]).reshape(2, 2, half, hd, PHI_HIDDEN).astype(BF16)
    zero = jnp.zeros_like(w1)
    per_group = [jnp.concatenate([w1 if g == col else zero for col in range(NSA_KV_GROUPS)], axis=-1)
                 for g in range(NSA_KV_GROUPS)]
    w1e = jnp.stack(per_group, axis=3).reshape(2, 2, half * NSA_KV, NSA_KV_GROUPS * PHI_HIDDEN)
    pe = jnp.stack([pe_k, pe_v]).reshape(2, 2, half, 1, hd)
    pe = jnp.broadcast_to(pe, (2, 2, half, NSA_KV_GROUPS, hd)).reshape(2, 2, 1, half * NSA_KV)
    pe = jnp.broadcast_to(pe, (2, 2, SUBLANES, half * NSA_KV)).astype(BF16)
    w2k = k_w2.astype(BF16)
    w2vt = v_w2.T.astype(BF16)
    whole = lambda a: pl.BlockSpec(a.shape, lambda b: (0,) * a.ndim)
    return pl.pallas_call(
        _compress_kernel,
        grid=(batch,),
        in_specs=[pl.BlockSpec((seq, LANES), lambda b: (b, k_col)),
                  pl.BlockSpec((seq, LANES), lambda b: (b, v_col)),
                  whole(pe), whole(w1e), whole(w2k), whole(w2vt)],
        out_specs=[pl.BlockSpec((None, n_blk, NSA_KV), lambda b: (b, 0, 0)),
                   pl.BlockSpec((None, NSA_KV, n_blk), lambda b: (b, 0, 0))],
        out_shape=[jax.ShapeDtypeStruct((batch, n_blk, NSA_KV), BF16),
                   jax.ShapeDtypeStruct((batch, NSA_KV, n_blk), BF16)],
        compiler_params=_params("arbitrary"),
        name="compress",
    )(src, src, pe, w1e, w2k, w2vt)


KEY_CHUNK = 256
NSA_TQ = 256
SLOPE_PIECES = 3
N_FEATS = 2 * SLOPE_PIECES
FEAT_LANES = 32
KEY_COLS = NSA_HEAD_DIM + 2 * FEAT_LANES
N_SLC = 32


def _slope_pieces(slope):
    rest = np.float32(slope)
    pieces = []
    for _ in range(SLOPE_PIECES):
        p = np.float32(np.asarray(rest).astype(BF16))
        pieces.append(float(p))
        rest = np.float32(rest - p)
    return pieces


def _lane_table(lane, values):
    out = jnp.zeros(lane.shape, F32)
    for idx, v in enumerate(values):
        out = jnp.where(lane == idx, v, out)
    return out


def _key_feats(pos_hi, pos_lo, lane):
    return jnp.where(lane < SLOPE_PIECES, pos_hi, jnp.where(lane < N_FEATS, pos_lo, 0)).astype(F32)


def _tile_heads(x):
    return jnp.concatenate([x] * NSA_R, axis=1)


def _nsa_kernel(q_ref, kc_ref, vct_ref, ksrc_ref, vsrc_ref, kwsrc_ref, vwsrc_ref, feat_ref, hot_ref,
                gl_ref, z_ref, ym_ref, w_ref, x_ref, fin_ref, o_ref,
                ks_ref, vst_ref, kw_ref, vwt_ref, s_ref, acc_ref, imp_ref):
    i = pl.program_id(1)
    tq = q_ref.shape[0]
    hd = NSA_HEAD_DIM
    n_cmp = kc_ref.shape[0]

    @pl.when(i == 0)
    def _():
        for g in range(NSA_KV_GROUPS):
            gs = slice(g * hd, (g + 1) * hd)
            for dst, src, tail in ((ks_ref, ksrc_ref, hot_ref[...]),
                                   (kw_ref, kwsrc_ref, jnp.zeros(hot_ref.shape, BF16))):
                dst[:, g * KEY_COLS:g * KEY_COLS + hd] = src[:, gs]
                dst[:, g * KEY_COLS + hd:g * KEY_COLS + hd + FEAT_LANES] = feat_ref[...]
                dst[:, g * KEY_COLS + hd + FEAT_LANES:(g + 1) * KEY_COLS] = tail
        for c in range(vst_ref.shape[0]):
            rows = slice(c * KEY_CHUNK, (c + 1) * KEY_CHUNK)
            vst_ref[c] = vsrc_ref[rows, :].astype(F32).T.astype(BF16)
            vwt_ref[c] = vwsrc_ref[rows, :].astype(F32).T.astype(BF16)

    slopes_all = _alibi_slopes(NSA_HEADS)
    gates_t = jax.nn.sigmoid(gl_ref[...]).T
    feat_lane = lax.broadcasted_iota(jnp.int32, (tq, FEAT_LANES), 1)
    no_sel = jnp.zeros((NSA_R * tq, FEAT_LANES), BF16)
    key_row = lax.broadcasted_iota(jnp.int32, (KEY_CHUNK, tq), 0)
    t_pos = i * tq + lax.broadcasted_iota(jnp.int32, (KEY_CHUNK, tq), 1)
    ones_rows = jnp.ones((16, KEY_CHUNK), BF16)
    win_lo = jnp.maximum(i * tq - (WIN_SIZE - 1), 0) // KEY_CHUNK
    chunks_hi = (i * tq + tq - 1) // KEY_CHUNK + 1

    groups = range(NSA_KV_GROUPS)
    q_win, q_slc, o_cmp = [], [], []
    for g in groups:
        slopes = slopes_all[g * NSA_R:(g + 1) * NSA_R]
        gs = slice(g * hd, (g + 1) * hd)
        q_parts = []
        for r in range(NSA_R):
            qr = q_ref[:, (g * NSA_R + r) * hd:(g * NSA_R + r + 1) * hd]
            feats = _lane_table(feat_lane, _slope_pieces(slopes[r]) * 2).astype(BF16)
            q_parts.append(jnp.concatenate([qr, feats], axis=1))
        q_feat = jnp.concatenate(q_parts, axis=0)
        q_aug = jnp.concatenate([q_feat, no_sel], axis=1)

        n_row = lax.broadcasted_iota(jnp.int32, (n_cmp, tq), 0)
        t_cmp = i * tq + lax.broadcasted_iota(jnp.int32, (n_cmp, tq), 1)
        visible = t_cmp >= n_row * CMP_STRIDE + (CMP_BLOCK - 1)
        cfeat_row = lax.broadcasted_iota(jnp.int32, (n_cmp, 2 * FEAT_LANES), 0)
        cfeat_lane = lax.broadcasted_iota(jnp.int32, (n_cmp, 2 * FEAT_LANES), 1)
        kc_feats = _key_feats(cfeat_row * CMP_STRIDE, 0, cfeat_lane)
        kc_aug = jnp.concatenate([kc_ref[:, gs], kc_feats.astype(BF16)], axis=1)
        s = _dot_t(kc_aug, q_aug) + _tile_heads(jnp.where(visible, 0.0, NEG_INF))
        m = jnp.max(s, axis=0, keepdims=True)
        e = jnp.exp(s - m)
        t_one = i * tq + lax.broadcasted_iota(jnp.int32, (1, NSA_R * tq), 1) % tq
        any_visible = t_one >= (CMP_BLOCK - 1)
        p = e * jnp.where(any_visible, 1.0 / jnp.sum(e, axis=0, keepdims=True), 0.0)
        o_cmp.append(jnp.dot(vct_ref[gs, :], p.astype(BF16), preferred_element_type=F32))
        p_sum = p[:, 0:tq]
        for r in range(1, NSA_R):
            p_sum = p_sum + p[:, r * tq:(r + 1) * tq]

        band = p_sum + pltpu.roll(p_sum, 1, axis=0)
        for k in range(1, CMP_PER_SLC):
            band = band + pltpu.roll(p_sum, n_cmp - k, axis=0)
        halves = []
        for half in range(tq // LANES):
            imp_ref[...] = band[:, half * LANES:(half + 1) * LANES]
            halves.append(imp_ref[pl.ds(0, N_SLC, stride=CMP_PER_SLC), :])
        imp = jnp.concatenate(halves, axis=1)
        blk_j = lax.broadcasted_iota(jnp.int32, (N_SLC, tq), 0)
        cur = (i * tq + lax.broadcasted_iota(jnp.int32, (N_SLC, tq), 1)) // SLC_BLOCK
        forced = (blk_j == 0) | (blk_j == cur) | (blk_j == cur - 1)
        v_imp = jnp.where(forced, SEL_FORCE, jnp.where(blk_j > cur, -SEL_FORCE, imp))
        rank = jnp.zeros((N_SLC, tq), F32)
        for other in range(N_SLC):
            row = v_imp[other:other + 1, :]
            ahead = (row > v_imp) | ((row == v_imp) & (blk_j > other))
            rank = rank + jnp.where(ahead, 1.0, 0.0)
        sel_bias = jnp.where(rank < SLC_TOP_N, 0.0, NEG_INF)

        padded = jnp.concatenate([sel_bias, jnp.zeros((LANES - N_SLC, tq), F32)], axis=0)
        sel_t = padded.T[:, 0:FEAT_LANES].astype(BF16)
        q_win.append(q_aug)
        q_slc.append(jnp.concatenate([q_feat, jnp.concatenate([sel_t] * NSA_R, axis=0)], axis=1))

    def attend(q_brs, k_ref, key_cols, vt_ref, lo, hi, masked_from, mask_fn):
        def scores(c, m_run, masked):
            start = pl.multiple_of(c * KEY_CHUNK, KEY_CHUNK)
            scs = [_dot_t(k_ref[pl.ds(start, KEY_CHUNK), g * key_cols:(g + 1) * key_cols], q_brs[g])
                   for g in groups]
            if masked:
                bias = _tile_heads(jnp.where(mask_fn(t_pos - (start + key_row)), 0.0, NEG_INF))
                scs = [sc + bias for sc in scs]
            for g in groups:
                s_ref[g, c] = scs[g]
            return tuple(jnp.maximum(m_run[g], jnp.max(scs[g], axis=0, keepdims=True)) for g in groups)

        m_rows = tuple(jnp.full((1, NSA_R * tq), NEG_INF, F32) for _ in groups)
        m_rows = lax.fori_loop(lo, masked_from, functools.partial(scores, masked=False), m_rows)
        m_rows = lax.fori_loop(masked_from, hi, functools.partial(scores, masked=True), m_rows)
        acc_ref[...] = jnp.zeros(acc_ref.shape, F32)

        def weighted(c, carry):
            for g in groups:
                e = jnp.exp(s_ref[g, c] - m_rows[g]).astype(BF16)
                v_ext = jnp.concatenate([vt_ref[c, g * hd:(g + 1) * hd, :], ones_rows], axis=0)
                acc_ref[g] += jnp.dot(v_ext, e, preferred_element_type=F32)
            return carry

        lax.fori_loop(lo, hi, weighted, 0)
        return [acc_ref[g, 0:hd, :] / acc_ref[g, hd:hd + 1, :] for g in groups]

    o_slc = attend(q_slc, ks_ref, KEY_COLS, vst_ref, 0, chunks_hi, (i * tq) // KEY_CHUNK,
                   lambda dist: dist >= 0)
    o_win = attend(q_win, kw_ref, KEY_COLS, vwt_ref, win_lo, chunks_hi, win_lo,
                   lambda dist: (dist >= 0) & (dist <= WIN_SIZE - 1))

    y = jnp.dot(ym_ref[...], w_ref[NSA_WIDTH:NSA_WIDTH + MEM_WIDTH, :], preferred_element_type=F32)
    for g in groups:
        def gate(kind):
            base = g * NSA_R * 3 + kind
            return jnp.concatenate([gates_t[base + 3 * r:base + 3 * r + 1, :] for r in range(NSA_R)], axis=1)

        o = gate(0) * o_cmp[g] + gate(1) * o_slc[g] + gate(2) * o_win[g]
        pairs = []
        for r in range(0, NSA_R, 2):
            two = jnp.concatenate([o[:, r * tq:(r + 1) * tq], o[:, (r + 1) * tq:(r + 2) * tq]], axis=0)
            pairs.append(two.T)
        cs = slice(g * NSA_R * hd, (g + 1) * NSA_R * hd)
        yo = (jnp.concatenate(pairs, axis=1) * _silu(z_ref[:, cs].astype(F32))).astype(BF16)
        y = y + jnp.dot(yo, w_ref[cs, :], preferred_element_type=F32)

    x = x_ref[...] + y
    ms = jnp.mean(x * x, axis=-1, keepdims=True)
    o_ref[...] = x * lax.rsqrt(ms + NORM_EPS) * fin_ref[...]


def _key_pos_feats(seq):
    assert seq // SLC_BLOCK == N_SLC <= FEAT_LANES
    pos = np.arange(seq)
    feats = np.zeros((seq, FEAT_LANES), np.float32)
    feats[:, 0:SLOPE_PIECES] = ((pos // SLC_BLOCK) * SLC_BLOCK)[:, None]
    feats[:, SLOPE_PIECES:N_FEATS] = (pos % SLC_BLOCK)[:, None]
    onehot = (np.arange(FEAT_LANES)[None, :] == (pos // SLC_BLOCK)[:, None]).astype(np.float32)
    return jnp.asarray(feats, BF16), jnp.asarray(onehot, BF16)


def _nsa_attn(nb, kv_col0, z_col, k_cmp, v_cmp_t, nf, gl_col, ym, w_out, x, final_g, batch, seq):
    tq = NSA_TQ
    nq = seq // tq
    d = x.shape[1]
    feats, onehot = _key_pos_feats(seq)
    kv_blk = kv_col0 // NSA_KV
    seq_cols = lambda col: pl.BlockSpec((seq, NSA_KV), lambda b, i: (b, kv_blk + col))
    const = lambda a: pl.BlockSpec(a.shape, lambda b, i: (0,) * a.ndim)
    per_batch = lambda a: pl.BlockSpec((None,) + a.shape[1:], lambda b, i: (b,) + (0,) * (a.ndim - 1))
    rows = lambda width, col=0: pl.BlockSpec((tq, width), lambda b, i: (b * nq + i, col))
    fin = final_g.reshape(1, d)
    return pl.pallas_call(
        _nsa_kernel,
        grid=(batch, nq),
        in_specs=[rows(NSA_WIDTH), per_batch(k_cmp), per_batch(v_cmp_t),
                  seq_cols(0), seq_cols(1), seq_cols(2), seq_cols(3), const(feats), const(onehot),
                  rows(LANES, gl_col), rows(NSA_WIDTH, z_col), rows(MEM_WIDTH), const(w_out), rows(d),
                  const(fin)],
        out_specs=rows(d),
        out_shape=jax.ShapeDtypeStruct((batch * seq, d), F32),
        scratch_shapes=[pltpu.VMEM((seq, NSA_KV_GROUPS * KEY_COLS), BF16),
                        pltpu.VMEM((seq // KEY_CHUNK, NSA_KV, KEY_CHUNK), BF16),
                        pltpu.VMEM((seq, NSA_KV_GROUPS * KEY_COLS), BF16),
                        pltpu.VMEM((seq // KEY_CHUNK, NSA_KV, KEY_CHUNK), BF16),
                        pltpu.VMEM((NSA_KV_GROUPS, seq // KEY_CHUNK, KEY_CHUNK, NSA_R * tq), F32),
                        pltpu.VMEM((NSA_KV_GROUPS, NSA_HEAD_DIM + 16, NSA_R * tq), F32),
                        pltpu.VMEM((seq // CMP_STRIDE, LANES), F32)],
        compiler_params=_params("arbitrary", "arbitrary"),
        name="nsa_attn",
    )(nb, k_cmp, v_cmp_t, nb, nb, nb, nb, feats, onehot, nf, nb, ym, w_out, x, fin)


def _hawk_layer(x, mem, batch, seq, norm_g, w_in, conv_w, conv_b, ga_w, ga_b, gx_w, gx_b, lam,
                mem_norm_g, w_mem_kv, w_out):
    xa0, za0 = 0, LRU_WIDTH
    q0 = 2 * LRU_WIDTH
    k0, v0 = q0 + DIL_QKV, q0 + 2 * DIL_QKV
    zb0 = q0 + 3 * DIL_QKV
    qm0 = zb0 + DIL_WIDTH
    zm0 = qm0 + MEM_WIDTH

    def qkv_cols(gi):
        return [(base + gi * DIL_WIDTH, DIL_WIDTH) for base in (q0, k0, v0)]

    w = w_in.astype(BF16)
    nat_cols = [(za0, LRU_WIDTH), *qkv_cols(0), (zb0, DIL_WIDTH), (qm0, MEM_WIDTH), (zm0, MEM_WIDTH)]
    xa, hb = _norm_matmul(x, norm_g, w, [(F32, [(xa0, LRU_WIDTH)]), (BF16, nat_cols)])
    za_col = 0
    qkv0_col = LRU_WIDTH // DIL_WIDTH
    zb_col = (LRU_WIDTH + 3 * DIL_WIDTH) // DIL_WIDTH
    qm_col = (LRU_WIDTH + 4 * DIL_WIDTH) // MEM_WIDTH
    zm_col = qm_col + 1
    qkv = [(hb[None], qkv0_col)]
    for gi in range(1, len(DIL_GROUPS)):
        qkv.append((_norm_matmul(x, norm_g, w, [(BF16, qkv_cols(gi))], dil=DIL_GROUPS[gi][1]), 0))
    n_mem = mem.shape[0] // batch
    mem_kv, = _norm_matmul(mem, mem_norm_g, w_mem_kv.astype(BF16), [(BF16, [(0, 2 * MEM_WIDTH)])])

    ya = _rglru(xa, hb, za_col, conv_w, conv_b, _pack_block_diag(ga_w), ga_b, _pack_block_diag(gx_w), gx_b,
                lam, batch, seq)
    attn = [_dil_attn(arr, col0, gi, batch, seq) for gi, (arr, col0) in enumerate(qkv)]
    ym = _mem_attn(hb, qm_col, mem_kv, hb, zm_col, batch, seq, n_mem)
    return _hawk_out(ya, [o for o, _ in attn], [l for _, l in attn], hb, zb_col, ym,
                     w_out.astype(BF16), x)


def _nsa_layer(x, mem, batch, seq, norm_g, w_in, pe_k, pe_v, phik_w1, phik_w2, phiv_w1, phiv_w2,
               mem_norm_g, w_mem_kv, w_out, final_g):
    kv0 = NSA_WIDTH
    gl0 = kv0 + 6 * NSA_KV
    z0 = gl0 + 3 * NSA_HEADS
    qm0 = z0 + NSA_WIDTH
    zm0 = qm0 + MEM_WIDTH
    gl_w = jnp.pad(w_in[:, gl0:z0], ((0, 0), (0, LANES - 3 * NSA_HEADS)))
    q_w = w_in[:, 0:kv0] * (NSA_HEAD_DIM ** -0.5)
    w_all = jnp.concatenate([q_w, w_in[:, z0:qm0], w_in[:, kv0 + 2 * NSA_KV:gl0], w_in[:, qm0:zm0 + MEM_WIDTH],
                             gl_w, w_in[:, kv0:kv0 + 2 * NSA_KV]], axis=1).astype(BF16)
    f32_width = LANES + 2 * NSA_KV
    bf16_width = w_all.shape[1] - f32_width
    nb, nf = _norm_matmul(x, norm_g, w_all, [(BF16, [(0, bf16_width)]), (F32, [(bf16_width, f32_width)])])
    gl_col, kc_col, vc_col = 0, 1, 2
    z_col = 1
    kv_col0 = 2 * NSA_WIDTH
    qm_col = (kv_col0 + 4 * NSA_KV) // MEM_WIDTH
    zm_col = qm_col + 1
    n_mem = mem.shape[0] // batch
    mem_kv, = _norm_matmul(mem, mem_norm_g, w_mem_kv.astype(BF16), [(BF16, [(0, 2 * MEM_WIDTH)])])

    k_cmp, v_cmp_t = _compress(nf, kc_col, vc_col, pe_k, pe_v, phik_w1, phik_w2, phiv_w1, phiv_w2,
                               batch, seq)
    ym = _mem_attn(nb, qm_col, mem_kv, nb, zm_col, batch, seq, n_mem)
    return _nsa_attn(nb, kv_col0, z_col, k_cmp, v_cmp_t, nf, gl_col, ym, w_out.astype(BF16), x, final_g,
                     batch, seq)


def kernel(x, mem, hawk_norm, hawk_w_in, hawk_conv_w, hawk_conv_b, hawk_gate_a_w, hawk_gate_a_b,
           hawk_gate_x_w, hawk_gate_x_b, hawk_lambda, hawk_mem_norm, hawk_w_mem_kv, hawk_w_out,
           nsa_norm, nsa_w_in, nsa_pe_k, nsa_pe_v, nsa_phi_k_w1, nsa_phi_k_w2, nsa_phi_v_w1,
           nsa_phi_v_w2, nsa_mem_norm, nsa_w_mem_kv, nsa_w_out, final_norm):
    batch, seq, d = x.shape
    assert hawk_norm.shape[0] == 1 and nsa_norm.shape[0] == 1, "one layer of each kind"
    assert seq % (ATTN_BLOCK * DIL_GROUPS[-1][1]) == 0
    x2 = x.reshape(batch * seq, d)
    mem2 = mem.reshape(batch * mem.shape[1], d)
    x2 = _hawk_layer(x2, mem2, batch, seq, hawk_norm[0], hawk_w_in[0], hawk_conv_w[0], hawk_conv_b[0],
                     hawk_gate_a_w[0], hawk_gate_a_b[0].reshape(-1), hawk_gate_x_w[0],
                     hawk_gate_x_b[0].reshape(-1), hawk_lambda[0], hawk_mem_norm[0], hawk_w_mem_kv[0],
                     hawk_w_out[0])
    out = _nsa_layer(x2, mem2, batch, seq, nsa_norm[0], nsa_w_in[0], nsa_pe_k[0], nsa_pe_v[0],
                     nsa_phi_k_w1[0], nsa_phi_k_w2[0], nsa_phi_v_w1[0], nsa_phi_v_w2[0],
                     nsa_mem_norm[0], nsa_w_mem_kv[0], nsa_w_out[0], final_norm)
    return out.reshape(batch, seq, d)
```

```python
import functools

import numpy as np
import jax
import jax.numpy as jnp
from jax import lax
from jax.experimental import pallas as pl
from jax.experimental.pallas import tpu as pltpu

F32 = jnp.float32
BF16 = jnp.bfloat16

NORM_EPS = 1e-6
NEG_INF = -1e30
LANES = 128
SUBLANES = 8
BF16_SUBLANES = 16
ATTN_BLOCK = 128
VMEM_LIMIT = 56 * 1024 * 1024

LRU_WIDTH = 1024
LRU_BLOCKS = 16
LRU_BLOCK_DIM = LRU_WIDTH // LRU_BLOCKS
LRU_PACK = 256
CONV_WIDTH = 4
LRU_C = 8.0

DIL_GROUPS = ((128, 1), (512, 4), (2048, 16))
DIL_HEADS = 4
DIL_HEAD_DIM = 128
DIL_WIDTH = DIL_HEADS * DIL_HEAD_DIM
DIL_QKV = len(DIL_GROUPS) * DIL_WIDTH
LSE_LANES = LANES // DIL_HEADS

MEM_HEADS = 4
MEM_HEAD_DIM = 64
MEM_WIDTH = MEM_HEADS * MEM_HEAD_DIM

NSA_HEADS = 16
NSA_KV_GROUPS = 2
NSA_R = NSA_HEADS // NSA_KV_GROUPS
NSA_HEAD_DIM = 64
NSA_WIDTH = NSA_HEADS * NSA_HEAD_DIM
NSA_KV = NSA_KV_GROUPS * NSA_HEAD_DIM
CMP_BLOCK = 32
CMP_STRIDE = 16
SLC_BLOCK = 64
SLC_TOP_N = 8
WIN_SIZE = 512
PHI_HIDDEN = 256
SEL_FORCE = 1e6
CMP_PER_SLC = SLC_BLOCK // CMP_STRIDE


def _alibi_slopes(n):
    return [float(v) for v in np.exp2(-8.0 * np.arange(1, n + 1) / n).astype(np.float32)]


def _params(*semantics):
    return pltpu.CompilerParams(dimension_semantics=semantics, vmem_limit_bytes=VMEM_LIMIT)


def _silu(z):
    return z * jax.nn.sigmoid(z)


def _dot_t(a, b):
    return lax.dot_general(a, b, (((1,), (1,)), ((), ())), preferred_element_type=F32)


def _rms_norm_rows(x, g):
    ms = jnp.mean(x * x, axis=-1, keepdims=True)
    return (x * lax.rsqrt(ms + NORM_EPS) * g).astype(BF16)


def _norm_matmul_kernel(*refs, dil, pieces):
    def project(xn, out_ref, w_refs, by_class=False):
        col = 0
        for w_ref in w_refs:
            width = w_ref.shape[1]
            res = jnp.dot(xn, w_ref[...], preferred_element_type=F32).astype(out_ref.dtype)
            if by_class:
                per = xn.shape[0] // dil
                for c in range(dil):
                    out_ref[c, :, col:col + width] = res[c * per:(c + 1) * per]
            else:
                out_ref[:, col:col + width] = res
            col += width

    if dil == 1:
        x_ref, g_ref = refs[:2]
        w_refs, o_refs = refs[2:2 + sum(pieces)], refs[2 + sum(pieces):]
        xn = _rms_norm_rows(x_ref[...], g_ref[...])
        first = 0
        for o_ref, n_pieces in zip(o_refs, pieces):
            project(xn, o_ref, w_refs[first:first + n_pieces])
            first += n_pieces
        return
    x_ref, g_ref, *w_refs, o_ref = refs
    tm, k = x_ref.shape
    x = x_ref[...]
    xn = x * lax.rsqrt(jnp.mean(x * x, axis=-1, keepdims=True) + NORM_EPS) * g_ref[...]
    xn = jnp.swapaxes(xn.reshape(tm // dil, dil, k), 0, 1).reshape(tm, k).astype(BF16)
    project(xn, o_ref, w_refs, by_class=True)


def _norm_matmul(x, g, w, outs, dil=1, tm=512):
    m, k = x.shape
    tm = min(tm, m)
    assert m % tm == 0 and k % LANES == 0
    pieces = [len(cols) for _, cols in outs]
    widths = [sum(width for _, width in cols) for _, cols in outs]
    w_specs = []
    for _, cols in outs:
        for col, width in cols:
            assert col % width == 0 and width % LANES == 0
            w_specs.append(pl.BlockSpec((k, width), functools.partial(lambda i, j: (0, j), j=col // width)))
    resident = [pl.BlockSpec((1, k), lambda i: (0, 0))] + w_specs
    operands = (g.reshape(1, k),) + (w,) * len(w_specs)
    kernel = functools.partial(_norm_matmul_kernel, dil=dil, pieces=pieces)
    if dil == 1:
        return pl.pallas_call(
            kernel,
            grid=(m // tm,),
            in_specs=[pl.BlockSpec((tm, k), lambda i: (i, 0))] + resident,
            out_specs=[pl.BlockSpec((tm, width), lambda i: (i, 0)) for width in widths],
            out_shape=[jax.ShapeDtypeStruct((m, width), dtype) for width, (dtype, _) in zip(widths, outs)],
            compiler_params=_params("arbitrary"),
            name="norm_matmul",
        )(x, *operands)
    per = tm // dil
    (out_dtype, _), = outs
    n, = widths
    assert tm % dil == 0 and per % BF16_SUBLANES == 0
    return pl.pallas_call(
        kernel,
        grid=(m // tm,),
        in_specs=[pl.BlockSpec((tm, k), lambda i: (i, 0))] + resident,
        out_specs=pl.BlockSpec((dil, per, n), lambda i: (0, i, 0)),
        out_shape=jax.ShapeDtypeStruct((dil, m // dil, n), out_dtype),
        compiler_params=_params("arbitrary"),
        name="norm_matmul_dil",
    )(x, *operands)


def _rglru_kernel(xa_ref, za_ref, cw_ref, cb_ref, wa_ref, ba_ref, wx_ref, bx_ref, lam_ref,
                  o_ref, xpad_ref, h_ref):
    t = pl.program_id(1)
    tt, width = xa_ref.shape
    halo = SUBLANES

    @pl.when(t == 0)
    def _():
        xpad_ref[0:halo, :] = jnp.zeros((halo, width), F32)
        h_ref[...] = jnp.zeros_like(h_ref)

    x = xa_ref[...]
    xpad_ref[halo:halo + tt, :] = x
    cw = cw_ref[...]
    y = cw[CONV_WIDTH - 1:CONV_WIDTH] * x
    for k in range(1, CONV_WIDTH):
        y = y + cw[CONV_WIDTH - 1 - k:CONV_WIDTH - k] * xpad_ref[halo - k:halo - k + tt, :]
    y = y + cb_ref[...]
    xpad_ref[0:halo, :] = x[tt - halo:tt, :]

    yb = y.astype(BF16)
    r_parts, i_parts = [], []
    for p in range(width // LRU_PACK):
        ys = yb[:, p * LRU_PACK:(p + 1) * LRU_PACK]
        r_parts.append(jnp.dot(ys, wa_ref[p], preferred_element_type=F32))
        i_parts.append(jnp.dot(ys, wx_ref[p], preferred_element_type=F32))
    r = jax.nn.sigmoid(jnp.concatenate(r_parts, axis=1) + ba_ref[...])
    gi = jax.nn.sigmoid(jnp.concatenate(i_parts, axis=1) + bx_ref[...])

    nl = -lam_ref[...]
    softplus = jnp.maximum(nl, 0.0) + jnp.log1p(jnp.exp(-jnp.abs(nl)))
    log_a = (-LRU_C) * r * softplus
    a = jnp.exp(log_a)
    mult = jnp.sqrt(-jnp.tanh(log_a) * (a * a + 1.0))
    b = y * gi * mult
    first = (lax.broadcasted_iota(jnp.int32, (SUBLANES, width), 0) == 0) & (t == 0)
    b = jnp.concatenate([jnp.where(first, (y * gi)[0:SUBLANES], b[0:SUBLANES]), b[SUBLANES:]], axis=0)

    groups = tt // SUBLANES
    a3 = a.reshape(groups, SUBLANES, width)
    b3 = b.reshape(groups, SUBLANES, width)
    sub = lax.broadcasted_iota(jnp.int32, (groups, SUBLANES, width), 1)
    k = 1
    while k < SUBLANES:
        keep = sub >= k
        a_sh = jnp.where(keep, pltpu.roll(a3, k, axis=1), 1.0)
        b_sh = jnp.where(keep, pltpu.roll(b3, k, axis=1), 0.0)
        b3 = a3 * b_sh + b3
        a3 = a3 * a_sh
        k *= 2
    carry = jnp.broadcast_to(h_ref[...], (SUBLANES, width))
    hs = []
    for gidx in range(groups):
        hg = a3[gidx] * carry + b3[gidx]
        hs.append(hg)
        carry = jnp.broadcast_to(hg[SUBLANES - 1:SUBLANES], (SUBLANES, width))
    h = jnp.concatenate(hs, axis=0)
    h_ref[...] = carry[0:1]
    o_ref[...] = (h * _silu(za_ref[...].astype(F32))).astype(o_ref.dtype)


def _rglru(xa, za_src, za_col, conv_w, conv_b, wa, ba, wx, bx, lam, batch, seq, tt=512):
    width = LRU_WIDTH
    nt = seq // tt
    packs = width // LRU_PACK
    vec = pl.BlockSpec((1, width), lambda b, t: (0, 0))
    gate_w = pl.BlockSpec((packs, LRU_PACK, LRU_PACK), lambda b, t: (0, 0, 0))
    return pl.pallas_call(
        _rglru_kernel,
        grid=(batch, nt),
        in_specs=[pl.BlockSpec((tt, width), lambda b, t: (b * nt + t, 0)),
                  pl.BlockSpec((tt, width), lambda b, t: (b * nt + t, za_col)),
                  pl.BlockSpec((CONV_WIDTH, width), lambda b, t: (0, 0)),
                  vec, gate_w, vec, gate_w, vec, vec],
        out_specs=pl.BlockSpec((tt, width), lambda b, t: (b * nt + t, 0)),
        out_shape=jax.ShapeDtypeStruct((batch * seq, width), BF16),
        scratch_shapes=[pltpu.VMEM((tt + SUBLANES, width), F32), pltpu.VMEM((1, width), F32)],
        compiler_params=_params("arbitrary", "arbitrary"),
        name="rglru",
    )(xa, za_src, conv_w, conv_b.reshape(1, width), wa, ba.reshape(1, width), wx, bx.reshape(1, width),
      lam.reshape(1, width))


def _pack_block_diag(w):
    per = LRU_PACK // LRU_BLOCK_DIM
    w = w.reshape(LRU_BLOCKS // per, per, LRU_BLOCK_DIM, LRU_BLOCK_DIM)
    eye = jnp.eye(per, dtype=w.dtype)
    packed = w[:, :, :, None, :] * eye[None, :, None, :, None]
    return packed.reshape(LRU_BLOCKS // per, LRU_PACK, LRU_PACK).astype(BF16)


def _dil_attn_kernel(*refs, slopes, pos_scale, max_dist, has_halo, dil, n_cls, n_blk):
    if has_halo:
        q_ref, kh_ref, k_ref, vh_ref, v_ref = refs[:5]
        out_refs = refs[5:]
    else:
        q_ref, k_ref, v_ref = refs[:3]
        out_refs = refs[3:]
    n_out = DIL_HEADS + 1
    dst_refs = out_refs[:n_out]
    stage_refs = out_refs[n_out:] if dil > 1 else dst_refs
    first_super = pl.program_id(1) == 0
    cls0 = pl.program_id(2) * n_cls
    blk = ATTN_BLOCK
    scale = DIL_HEAD_DIM ** -0.5

    def band(width, halo_live):
        row = lax.broadcasted_iota(jnp.int32, (blk, width), 0)
        col = lax.broadcasted_iota(jnp.int32, (blk, width), 1)
        dist = (width - blk) + row - col
        valid = (dist >= 0) & (dist <= max_dist)
        if halo_live is not None:
            valid = valid & ((col >= blk) | halo_live)
        distf = (dist * pos_scale).astype(F32)
        return [jnp.where(valid, -slope * distf, NEG_INF) for slope in slopes]

    bias_inner = band(2 * blk, None) if n_blk > 1 else None
    bias_first = band(2 * blk, jnp.logical_not(first_super)) if has_halo else band(blk, None)

    def scores(cc, jb):
        cur = slice(jb * blk, (jb + 1) * blk)
        bias = bias_inner if jb > 0 else bias_first
        out = []
        for h in range(DIL_HEADS):
            hs = slice(h * DIL_HEAD_DIM, (h + 1) * DIL_HEAD_DIM)
            q = q_ref[cc, cur, hs]
            if jb > 0:
                k = k_ref[cc, (jb - 1) * blk:(jb + 1) * blk, hs]
                v = v_ref[cc, (jb - 1) * blk:(jb + 1) * blk, hs]
            elif has_halo:
                k = jnp.concatenate([kh_ref[cc, :, hs], k_ref[cc, cur, hs]], axis=0)
                v = jnp.concatenate([vh_ref[cc, :, hs], v_ref[cc, cur, hs]], axis=0)
            else:
                k, v = k_ref[cc, cur, hs], v_ref[cc, cur, hs]
            out.append((_dot_t(q, k) * scale + bias[h], v))
        return out

    def finish(cc, jb, pairs):
        where = (cls0 + cc, slice(jb * blk, (jb + 1) * blk)) if dil > 1 else (slice(jb * blk, (jb + 1) * blk),)
        lses = []
        for h, (s, v) in enumerate(pairs):
            m = jnp.max(s, axis=-1, keepdims=True)
            e = jnp.exp(s - m)
            den = jnp.sum(e, axis=-1, keepdims=True)
            o = jnp.dot(e.astype(BF16), v, preferred_element_type=F32) / den
            stage_refs[h][where] = o.astype(stage_refs[h].dtype)
            lses.append(jnp.broadcast_to(m + jnp.log(den), (blk, LSE_LANES)))
        stage_refs[DIL_HEADS][where] = jnp.concatenate(lses, axis=1)

    pending = None
    for cc in range(n_cls):
        for jb in range(n_blk):
            pairs = scores(cc, jb)
            if pending is not None:
                finish(*pending)
            pending = (cc, jb, pairs)
    finish(*pending)

    if dil > 1:
        @pl.when(pl.program_id(2) == pl.num_programs(2) - 1)
        def _():
            for stage, dst in zip(stage_refs, dst_refs):
                dst[...] = jnp.swapaxes(stage[...], 0, 1).reshape(dst.shape).astype(dst.dtype)


def _dil_attn(qkv, col0, gi, batch, seq, work=8):
    window, dil = DIL_GROUPS[gi]
    sub = seq // dil
    nb = sub // ATTN_BLOCK
    n_blk = min(work, nb)
    n_cls = min(work // n_blk, dil)
    n_super = nb // n_blk
    has_halo = n_super > 1
    span = n_blk * ATTN_BLOCK
    slopes = _alibi_slopes(len(DIL_GROUPS) * DIL_HEADS)[gi * DIL_HEADS:(gi + 1) * DIL_HEADS]
    cur = lambda col: pl.BlockSpec((n_cls, span, DIL_WIDTH), lambda b, i, c: (c, b * n_super + i, col0 + col))
    halo = lambda col: pl.BlockSpec(
        (n_cls, ATTN_BLOCK, DIL_WIDTH),
        lambda b, i, c: (c, jnp.maximum((b * n_super + i) * n_blk - 1, 0), col0 + col))
    if has_halo:
        in_specs = [cur(0), halo(1), cur(1), halo(2), cur(2)]
    else:
        in_specs = [cur(0), cur(1), cur(2)]
    n_out = DIL_HEADS + 1
    *o, lse = pl.pallas_call(
        functools.partial(_dil_attn_kernel, slopes=slopes, pos_scale=dil, max_dist=window // dil,
                          has_halo=has_halo, dil=dil, n_cls=n_cls, n_blk=n_blk),
        grid=(batch, n_super, dil // n_cls),
        in_specs=in_specs,
        out_specs=[pl.BlockSpec((span * dil, LANES), lambda b, i, c: (b * n_super + i, 0))] * n_out,
        out_shape=[jax.ShapeDtypeStruct((batch * seq, LANES), BF16)] * DIL_HEADS
                  + [jax.ShapeDtypeStruct((batch * seq, LANES), F32)],
        scratch_shapes=[pltpu.VMEM((dil, span, LANES), F32)] * (n_out if dil > 1 else 0),
        compiler_params=_params("arbitrary", "arbitrary", "arbitrary"),
        name=f"dil_attn_d{dil}",
    )(*([qkv] * len(in_specs)))
    return o, lse


def _mem_attn_kernel(q_ref, k_ref, v_ref, z_ref, o_ref, ks_ref, vt_ref):
    hd = MEM_HEAD_DIM
    n_mem = k_ref.shape[0]

    @pl.when(pl.program_id(1) == 0)
    def _():
        ks_ref[...] = (k_ref[...].astype(F32) * (hd ** -0.5)).astype(BF16)
        vt = v_ref[...].astype(F32).T.astype(BF16)
        for h in range(MEM_HEADS):
            vt_ref[h, 0:hd, :] = vt[h * hd:(h + 1) * hd, :]
            vt_ref[h, hd:, :] = jnp.ones((vt_ref.shape[1] - hd, n_mem), BF16)

    heads = [slice(h * hd, (h + 1) * hd) for h in range(MEM_HEADS)]
    scores = [_dot_t(ks_ref[:, hs], q_ref[:, hs]) for hs in heads]
    outs = []
    for h, s in enumerate(scores):
        e = jnp.exp(s - jnp.max(s, axis=0, keepdims=True)).astype(BF16)
        acc = jnp.dot(vt_ref[h], e, preferred_element_type=F32)
        outs.append(acc[0:hd, :] / acc[hd:hd + 1, :])
    o = jnp.concatenate(outs, axis=0).T
    o_ref[...] = (o * _silu(z_ref[...].astype(F32))).astype(o_ref.dtype)


def _mem_attn(qsrc, q_col, kv, zsrc, z_col, batch, seq, n_mem, tq=512):
    nq = seq // tq
    return pl.pallas_call(
        _mem_attn_kernel,
        grid=(batch, nq),
        in_specs=[pl.BlockSpec((tq, MEM_WIDTH), lambda b, i: (b * nq + i, q_col)),
                  pl.BlockSpec((n_mem, MEM_WIDTH), lambda b, i: (b, 0)),
                  pl.BlockSpec((n_mem, MEM_WIDTH), lambda b, i: (b, 1)),
                  pl.BlockSpec((tq, MEM_WIDTH), lambda b, i: (b * nq + i, z_col))],
        out_specs=pl.BlockSpec((tq, MEM_WIDTH), lambda b, i: (b * nq + i, 0)),
        out_shape=jax.ShapeDtypeStruct((batch * seq, MEM_WIDTH), BF16),
        scratch_shapes=[pltpu.VMEM((n_mem, MEM_WIDTH), BF16),
                        pltpu.VMEM((MEM_HEADS, MEM_HEAD_DIM + BF16_SUBLANES, n_mem), BF16)],
        compiler_params=_params("arbitrary", "arbitrary"),
        name="mem_attn",
    )(qsrc, kv, kv, zsrc)


def _hawk_out_kernel(*refs):
    n_groups = len(DIL_GROUPS)
    ya_ref = refs[0]
    o_refs = refs[1:1 + n_groups * DIL_HEADS]
    l_refs = refs[1 + n_groups * DIL_HEADS:1 + n_groups * (DIL_HEADS + 1)]
    zb_ref, ym_ref, w_ref, x_ref, out_ref = refs[1 + n_groups * (DIL_HEADS + 1):]
    a_end = LRU_WIDTH
    b_end = a_end + DIL_WIDTH
    y = jnp.dot(ya_ref[...], w_ref[0:a_end, :], preferred_element_type=F32)
    y = y + jnp.dot(ym_ref[...], w_ref[b_end:b_end + MEM_WIDTH, :], preferred_element_type=F32)
    parts = []
    for h in range(DIL_HEADS):
        ls = [l[:, h * LSE_LANES:h * LSE_LANES + 1] for l in l_refs]
        m = functools.reduce(jnp.maximum, ls)
        ws = [jnp.exp(l - m) for l in ls]
        num = sum(w * o_refs[gi * DIL_HEADS + h][...].astype(F32) for gi, w in enumerate(ws))
        parts.append(num / sum(ws))
    yb = (jnp.concatenate(parts, axis=1) * _silu(zb_ref[...].astype(F32))).astype(BF16)
    y = y + jnp.dot(yb, w_ref[a_end:b_end, :], preferred_element_type=F32)
    out_ref[...] = x_ref[...] + y


def _hawk_out(ya, os_, ls_, zb_src, zb_col, ym, w, x, tm=512):
    m, d = x.shape
    row = lambda width, col=0: pl.BlockSpec((tm, width), lambda i: (i, col))
    heads = [o for group in os_ for o in group]
    return pl.pallas_call(
        _hawk_out_kernel,
        grid=(m // tm,),
        in_specs=[row(LRU_WIDTH)] + [row(DIL_HEAD_DIM)] * len(heads) + [row(LANES)] * len(ls_)
                 + [row(DIL_WIDTH, zb_col), row(MEM_WIDTH),
                    pl.BlockSpec(w.shape, lambda i: (0, 0)), row(d)],
        out_specs=row(d),
        out_shape=jax.ShapeDtypeStruct((m, d), F32),
        compiler_params=_params("arbitrary"),
        name="hawk_out",
    )(ya, *heads, *ls_, zb_src, ym, w, x)


def _compress_kernel(k_ref, v_ref, pe_ref, w1_ref, w2k_ref, w2vt_ref, ko_ref, vto_ref):
    n_blk = k_ref.shape[0] // CMP_STRIDE

    def hidden(which, src_ref):
        x = jnp.concatenate([src_ref[pl.ds(p, n_blk, stride=CMP_STRIDE), :] for p in range(CMP_STRIDE)],
                            axis=1).astype(BF16)
        first = jnp.dot(x, w1_ref[which, 0], preferred_element_type=F32)
        second = jnp.dot(x, w1_ref[which, 1], preferred_element_type=F32)
        pe = (jnp.dot(pe_ref[which, 0], w1_ref[which, 0], preferred_element_type=F32)
              + jnp.dot(pe_ref[which, 1], w1_ref[which, 1], preferred_element_type=F32))
        return _silu(first + pltpu.roll(second, n_blk - 1, axis=0) + pe[0:1, :]).astype(BF16)

    act_k, act_v = hidden(0, k_ref), hidden(1, v_ref)
    part = lambda a, g: a[:, g * PHI_HIDDEN:(g + 1) * PHI_HIDDEN]
    ks = [jnp.dot(part(act_k, g), w2k_ref[...], preferred_element_type=F32) for g in range(NSA_KV_GROUPS)]
    vts = [_dot_t(w2vt_ref[...], part(act_v, g)) for g in range(NSA_KV_GROUPS)]
    ko_ref[...] = jnp.concatenate(ks, axis=1).astype(ko_ref.dtype)
    vto_ref[...] = jnp.concatenate(vts, axis=0).astype(vto_ref.dtype)


def _compress(src, k_col, v_col, pe_k, pe_v, k_w1, k_w2, v_w1, v_w2, batch, seq):
    half = CMP_BLOCK // 2
    assert half == CMP_STRIDE and NSA_KV == LANES
    n_blk = seq // CMP_STRIDE
    hd = NSA_HEAD_DIM
    w1 = jnp.stack([k_w1, v_w1]).reshape(2, 2, half, hd, PHI_HIDDEN).astype(BF16)
    zero = jnp.zeros_like(w1)
    per_group = [jnp.concatenate([w1 if g == col else zero for col in range(NSA_KV_GROUPS)], axis=-1)
                 for g in range(NSA_KV_GROUPS)]
    w1e = jnp.stack(per_group, axis=3).reshape(2, 2, half * NSA_KV, NSA_KV_GROUPS * PHI_HIDDEN)
    pe = jnp.stack([pe_k, pe_v]).reshape(2, 2, half, 1, hd)
    pe = jnp.broadcast_to(pe, (2, 2, half, NSA_KV_GROUPS, hd)).reshape(2, 2, 1, half * NSA_KV)
    pe = jnp.broadcast_to(pe, (2, 2, SUBLANES, half * NSA_KV)).astype(BF16)
    w2k = k_w2.astype(BF16)
    w2vt = v_w2.T.astype(BF16)
    whole = lambda a: pl.BlockSpec(a.shape, lambda b: (0,) * a.ndim)
    return pl.pallas_call(
        _compress_kernel,
        grid=(batch,),
        in_specs=[pl.BlockSpec((seq, LANES), lambda b: (b, k_col)),
                  pl.BlockSpec((seq, LANES), lambda b: (b, v_col)),
                  whole(pe), whole(w1e), whole(w2k), whole(w2vt)],
        out_specs=[pl.BlockSpec((None, n_blk, NSA_KV), lambda b: (b, 0, 0)),
                   pl.BlockSpec((None, NSA_KV, n_blk), lambda b: (b, 0, 0))],
        out_shape=[jax.ShapeDtypeStruct((batch, n_blk, NSA_KV), BF16),
                   jax.ShapeDtypeStruct((batch, NSA_KV, n_blk), BF16)],
        compiler_params=_params("arbitrary"),
        name="compress",
    )(src, src, pe, w1e, w2k, w2vt)


KEY_CHUNK = 256
NSA_TQ = 256
SLOPE_PIECES = 3
N_FEATS = 2 * SLOPE_PIECES
FEAT_LANES = 32
KEY_COLS = NSA_HEAD_DIM + 2 * FEAT_LANES
N_SLC = 32


def _slope_pieces(slope):
    rest = np.float32(slope)
    pieces = []
    for _ in range(SLOPE_PIECES):
        p = np.float32(np.asarray(rest).astype(BF16))
        pieces.append(float(p))
        rest = np.float32(rest - p)
    return pieces


def _lane_table(lane, values):
    out = jnp.zeros(lane.shape, F32)
    for idx, v in enumerate(values):
        out = jnp.where(lane == idx, v, out)
    return out


def _key_feats(pos_hi, pos_lo, lane):
    return jnp.where(lane < SLOPE_PIECES, pos_hi, jnp.where(lane < N_FEATS, pos_lo, 0)).astype(F32)


def _tile_heads(x):
    return jnp.concatenate([x] * NSA_R, axis=1)


def _nsa_kernel(q_ref, kc_ref, vct_ref, ksrc_ref, vsrc_ref, kwsrc_ref, vwsrc_ref, feat_ref, hot_ref,
                gl_ref, z_ref, ym_ref, w_ref, x_ref, fin_ref, o_ref,
                ks_ref, vst_ref, kw_ref, vwt_ref, s_ref, acc_ref, imp_ref):
    i = pl.program_id(1)
    tq = q_ref.shape[0]
    hd = NSA_HEAD_DIM
    n_cmp = kc_ref.shape[0]

    @pl.when(i == 0)
    def _():
        for g in range(NSA_KV_GROUPS):
            gs = slice(g * hd, (g + 1) * hd)
            for dst, src, tail in ((ks_ref, ksrc_ref, hot_ref[...]),
                                   (kw_ref, kwsrc_ref, jnp.zeros(hot_ref.shape, BF16))):
                dst[:, g * KEY_COLS:g * KEY_COLS + hd] = src[:, gs]
                dst[:, g * KEY_COLS + hd:g * KEY_COLS + hd + FEAT_LANES] = feat_ref[...]
                dst[:, g * KEY_COLS + hd + FEAT_LANES:(g + 1) * KEY_COLS] = tail
        for c in range(vst_ref.shape[0]):
            rows = slice(c * KEY_CHUNK, (c + 1) * KEY_CHUNK)
            vst_ref[c] = vsrc_ref[rows, :].astype(F32).T.astype(BF16)
            vwt_ref[c] = vwsrc_ref[rows, :].astype(F32).T.astype(BF16)

    slopes_all = _alibi_slopes(NSA_HEADS)
    gates_t = jax.nn.sigmoid(gl_ref[...]).T
    feat_lane = lax.broadcasted_iota(jnp.int32, (tq, FEAT_LANES), 1)
    no_sel = jnp.zeros((NSA_R * tq, FEAT_LANES), BF16)
    key_row = lax.broadcasted_iota(jnp.int32, (KEY_CHUNK, tq), 0)
    t_pos = i * tq + lax.broadcasted_iota(jnp.int32, (KEY_CHUNK, tq), 1)
    ones_rows = jnp.ones((BF16_SUBLANES, KEY_CHUNK), BF16)
    win_lo = jnp.maximum(i * tq - (WIN_SIZE - 1), 0) // KEY_CHUNK
    chunks_hi = (i * tq + tq - 1) // KEY_CHUNK + 1

    groups = range(NSA_KV_GROUPS)
    q_win, q_slc, o_cmp = [], [], []
    for g in groups:
        slopes = slopes_all[g * NSA_R:(g + 1) * NSA_R]
        gs = slice(g * hd, (g + 1) * hd)
        q_parts = []
        for r in range(NSA_R):
            qr = q_ref[:, (g * NSA_R + r) * hd:(g * NSA_R + r + 1) * hd]
            feats = _lane_table(feat_lane, _slope_pieces(slopes[r]) * 2).astype(BF16)
            q_parts.append(jnp.concatenate([qr, feats], axis=1))
        q_feat = jnp.concatenate(q_parts, axis=0)
        q_aug = jnp.concatenate([q_feat, no_sel], axis=1)

        n_row = lax.broadcasted_iota(jnp.int32, (n_cmp, tq), 0)
        t_cmp = i * tq + lax.broadcasted_iota(jnp.int32, (n_cmp, tq), 1)
        visible = t_cmp >= n_row * CMP_STRIDE + (CMP_BLOCK - 1)
        cfeat_row = lax.broadcasted_iota(jnp.int32, (n_cmp, 2 * FEAT_LANES), 0)
        cfeat_lane = lax.broadcasted_iota(jnp.int32, (n_cmp, 2 * FEAT_LANES), 1)
        kc_feats = _key_feats(cfeat_row * CMP_STRIDE, 0, cfeat_lane)
        kc_aug = jnp.concatenate([kc_ref[:, gs], kc_feats.astype(BF16)], axis=1)
        s = _dot_t(kc_aug, q_aug) + _tile_heads(jnp.where(visible, 0.0, NEG_INF))
        m = jnp.max(s, axis=0, keepdims=True)
        e = jnp.exp(s - m)
        t_one = i * tq + lax.broadcasted_iota(jnp.int32, (1, NSA_R * tq), 1) % tq
        any_visible = t_one >= (CMP_BLOCK - 1)
        p = e * jnp.where(any_visible, 1.0 / jnp.sum(e, axis=0, keepdims=True), 0.0)
        o_cmp.append(jnp.dot(vct_ref[gs, :], p.astype(BF16), preferred_element_type=F32))
        p_sum = p[:, 0:tq]
        for r in range(1, NSA_R):
            p_sum = p_sum + p[:, r * tq:(r + 1) * tq]

        band = p_sum + pltpu.roll(p_sum, 1, axis=0)
        for k in range(1, CMP_PER_SLC):
            band = band + pltpu.roll(p_sum, n_cmp - k, axis=0)
        halves = []
        for half in range(tq // LANES):
            imp_ref[...] = band[:, half * LANES:(half + 1) * LANES]
            halves.append(imp_ref[pl.ds(0, N_SLC, stride=CMP_PER_SLC), :])
        imp = jnp.concatenate(halves, axis=1)
        blk_j = lax.broadcasted_iota(jnp.int32, (N_SLC, tq), 0)
        cur = (i * tq + lax.broadcasted_iota(jnp.int32, (N_SLC, tq), 1)) // SLC_BLOCK
        forced = (blk_j == 0) | (blk_j == cur) | (blk_j == cur - 1)
        v_imp = jnp.where(forced, SEL_FORCE, jnp.where(blk_j > cur, -SEL_FORCE, imp))
        rank = jnp.zeros((N_SLC, tq), F32)
        for other in range(N_SLC):
            row = v_imp[other:other + 1, :]
            ahead = (row > v_imp) | ((row == v_imp) & (blk_j > other))
            rank = rank + jnp.where(ahead, 1.0, 0.0)
        sel_bias = jnp.where(rank < SLC_TOP_N, 0.0, NEG_INF)

        padded = jnp.concatenate([sel_bias, jnp.zeros((LANES - N_SLC, tq), F32)], axis=0)
        sel_t = padded.T[:, 0:FEAT_LANES].astype(BF16)
        q_win.append(q_aug)
        q_slc.append(jnp.concatenate([q_feat, jnp.concatenate([sel_t] * NSA_R, axis=0)], axis=1))

    def attend(q_brs, k_ref, key_cols, vt_ref, lo, hi, masked_from, mask_fn):
        def scores(c, m_run, masked):
            start = pl.multiple_of(c * KEY_CHUNK, KEY_CHUNK)
            scs = [_dot_t(k_ref[pl.ds(start, KEY_CHUNK), g * key_cols:(g + 1) * key_cols], q_brs[g])
                   for g in groups]
            if masked:
                bias = _tile_heads(jnp.where(mask_fn(t_pos - (start + key_row)), 0.0, NEG_INF))
                scs = [sc + bias for sc in scs]
            for g in groups:
                s_ref[g, c] = scs[g]
            return tuple(jnp.maximum(m_run[g], jnp.max(scs[g], axis=0, keepdims=True)) for g in groups)

        m_rows = tuple(jnp.full((1, NSA_R * tq), NEG_INF, F32) for _ in groups)
        m_rows = lax.fori_loop(lo, masked_from, functools.partial(scores, masked=False), m_rows)
        m_rows = lax.fori_loop(masked_from, hi, functools.partial(scores, masked=True), m_rows)
        acc_ref[...] = jnp.zeros(acc_ref.shape, F32)

        def weighted(c, carry):
            for g in groups:
                e = jnp.exp(s_ref[g, c] - m_rows[g]).astype(BF16)
                v_ext = jnp.concatenate([vt_ref[c, g * hd:(g + 1) * hd, :], ones_rows], axis=0)
                acc_ref[g] += jnp.dot(v_ext, e, preferred_element_type=F32)
            return carry

        lax.fori_loop(lo, hi, weighted, 0)
        return [acc_ref[g, 0:hd, :] / acc_ref[g, hd:hd + 1, :] for g in groups]

    o_slc = attend(q_slc, ks_ref, KEY_COLS, vst_ref, 0, chunks_hi, (i * tq) // KEY_CHUNK,
                   lambda dist: dist >= 0)
    o_win = attend(q_win, kw_ref, KEY_COLS, vwt_ref, win_lo, chunks_hi, win_lo,
                   lambda dist: (dist >= 0) & (dist <= WIN_SIZE - 1))

    y = jnp.dot(ym_ref[...], w_ref[NSA_WIDTH:NSA_WIDTH + MEM_WIDTH, :], preferred_element_type=F32)
    for g in groups:
        def gate(kind):
            base = g * NSA_R * 3 + kind
            return jnp.concatenate([gates_t[base + 3 * r:base + 3 * r + 1, :] for r in range(NSA_R)], axis=1)

        o = gate(0) * o_cmp[g] + gate(1) * o_slc[g] + gate(2) * o_win[g]
        pairs = []
        for r in range(0, NSA_R, 2):
            two = jnp.concatenate([o[:, r * tq:(r + 1) * tq], o[:, (r + 1) * tq:(r + 2) * tq]], axis=0)
            pairs.append(two.T)
        cs = slice(g * NSA_R * hd, (g + 1) * NSA_R * hd)
        yo = (jnp.concatenate(pairs, axis=1) * _silu(z_ref[:, cs].astype(F32))).astype(BF16)
        y = y + jnp.dot(yo, w_ref[cs, :], preferred_element_type=F32)

    x = x_ref[...] + y
    ms = jnp.mean(x * x, axis=-1, keepdims=True)
    o_ref[...] = x * lax.rsqrt(ms + NORM_EPS) * fin_ref[...]


def _key_pos_feats(seq):
    assert seq // SLC_BLOCK == N_SLC <= FEAT_LANES
    pos = np.arange(seq)
    feats = np.zeros((seq, FEAT_LANES), np.float32)
    feats[:, 0:SLOPE_PIECES] = ((pos // SLC_BLOCK) * SLC_BLOCK)[:, None]
    feats[:, SLOPE_PIECES:N_FEATS] = (pos % SLC_BLOCK)[:, None]
    onehot = (np.arange(FEAT_LANES)[None, :] == (pos // SLC_BLOCK)[:, None]).astype(np.float32)
    return jnp.asarray(feats, BF16), jnp.asarray(onehot, BF16)


def _nsa_attn(nb, kv_col0, z_col, k_cmp, v_cmp_t, nf, gl_col, ym, w_out, x, final_g, batch, seq):
    tq = NSA_TQ
    nq = seq // tq
    d = x.shape[1]
    feats, onehot = _key_pos_feats(seq)
    kv_blk = kv_col0 // NSA_KV
    seq_cols = lambda col: pl.BlockSpec((seq, NSA_KV), lambda b, i: (b, kv_blk + col))
    const = lambda a: pl.BlockSpec(a.shape, lambda b, i: (0,) * a.ndim)
    per_batch = lambda a: pl.BlockSpec((None,) + a.shape[1:], lambda b, i: (b,) + (0,) * (a.ndim - 1))
    rows = lambda width, col=0: pl.BlockSpec((tq, width), lambda b, i: (b * nq + i, col))
    fin = final_g.reshape(1, d)
    return pl.pallas_call(
        _nsa_kernel,
        grid=(batch, nq),
        in_specs=[rows(NSA_WIDTH), per_batch(k_cmp), per_batch(v_cmp_t),
                  seq_cols(0), seq_cols(1), seq_cols(2), seq_cols(3), const(feats), const(onehot),
                  rows(LANES, gl_col), rows(NSA_WIDTH, z_col), rows(MEM_WIDTH), const(w_out), rows(d),
                  const(fin)],
        out_specs=rows(d),
        out_shape=jax.ShapeDtypeStruct((batch * seq, d), F32),
        scratch_shapes=[pltpu.VMEM((seq, NSA_KV_GROUPS * KEY_COLS), BF16),
                        pltpu.VMEM((seq // KEY_CHUNK, NSA_KV, KEY_CHUNK), BF16),
                        pltpu.VMEM((seq, NSA_KV_GROUPS * KEY_COLS), BF16),
                        pltpu.VMEM((seq // KEY_CHUNK, NSA_KV, KEY_CHUNK), BF16),
                        pltpu.VMEM((NSA_KV_GROUPS, seq // KEY_CHUNK, KEY_CHUNK, NSA_R * tq), F32),
                        pltpu.VMEM((NSA_KV_GROUPS, NSA_HEAD_DIM + BF16_SUBLANES, NSA_R * tq), F32),
                        pltpu.VMEM((seq // CMP_STRIDE, LANES), F32)],
        compiler_params=_params("arbitrary", "arbitrary"),
        name="nsa_attn",
    )(nb, k_cmp, v_cmp_t, nb, nb, nb, nb, feats, onehot, nf, nb, ym, w_out, x, fin)


def _hawk_layer(x, mem, batch, seq, norm_g, w_in, conv_w, conv_b, ga_w, ga_b, gx_w, gx_b, lam,
                mem_norm_g, w_mem_kv, w_out):
    xa0, za0 = 0, LRU_WIDTH
    q0 = 2 * LRU_WIDTH
    k0, v0 = q0 + DIL_QKV, q0 + 2 * DIL_QKV
    zb0 = q0 + 3 * DIL_QKV
    qm0 = zb0 + DIL_WIDTH
    zm0 = qm0 + MEM_WIDTH

    def qkv_cols(gi):
        return [(base + gi * DIL_WIDTH, DIL_WIDTH) for base in (q0, k0, v0)]

    w = w_in.astype(BF16)
    nat_cols = [(za0, LRU_WIDTH), *qkv_cols(0), (zb0, DIL_WIDTH), (qm0, MEM_WIDTH), (zm0, MEM_WIDTH)]
    xa, hb = _norm_matmul(x, norm_g, w, [(F32, [(xa0, LRU_WIDTH)]), (BF16, nat_cols)])
    za_col = 0
    qkv0_col = LRU_WIDTH // DIL_WIDTH
    zb_col = (LRU_WIDTH + 3 * DIL_WIDTH) // DIL_WIDTH
    qm_col = (LRU_WIDTH + 4 * DIL_WIDTH) // MEM_WIDTH
    zm_col = qm_col + 1
    qkv = [(hb[None], qkv0_col)]
    for gi in range(1, len(DIL_GROUPS)):
        qkv.append((_norm_matmul(x, norm_g, w, [(BF16, qkv_cols(gi))], dil=DIL_GROUPS[gi][1]), 0))
    n_mem = mem.shape[0] // batch
    mem_kv, = _norm_matmul(mem, mem_norm_g, w_mem_kv.astype(BF16), [(BF16, [(0, 2 * MEM_WIDTH)])])

    ya = _rglru(xa, hb, za_col, conv_w, conv_b, _pack_block_diag(ga_w), ga_b, _pack_block_diag(gx_w), gx_b,
                lam, batch, seq)
    attn = [_dil_attn(arr, col0, gi, batch, seq) for gi, (arr, col0) in enumerate(qkv)]
    ym = _mem_attn(hb, qm_col, mem_kv, hb, zm_col, batch, seq, n_mem)
    return _hawk_out(ya, [o for o, _ in attn], [l for _, l in attn], hb, zb_col, ym,
                     w_out.astype(BF16), x)


def _nsa_layer(x, mem, batch, seq, norm_g, w_in, pe_k, pe_v, phik_w1, phik_w2, phiv_w1, phiv_w2,
               mem_norm_g, w_mem_kv, w_out, final_g):
    kv0 = NSA_WIDTH
    gl0 = kv0 + 6 * NSA_KV
    z0 = gl0 + 3 * NSA_HEADS
    qm0 = z0 + NSA_WIDTH
    zm0 = qm0 + MEM_WIDTH
    gl_w = jnp.pad(w_in[:, gl0:z0], ((0, 0), (0, LANES - 3 * NSA_HEADS)))
    q_w = w_in[:, 0:kv0] * (NSA_HEAD_DIM ** -0.5)
    w_all = jnp.concatenate([q_w, w_in[:, z0:qm0], w_in[:, kv0 + 2 * NSA_KV:gl0], w_in[:, qm0:zm0 + MEM_WIDTH],
                             gl_w, w_in[:, kv0:kv0 + 2 * NSA_KV]], axis=1).astype(BF16)
    f32_width = LANES + 2 * NSA_KV
    bf16_width = w_all.shape[1] - f32_width
    nb, nf = _norm_matmul(x, norm_g, w_all, [(BF16, [(0, bf16_width)]), (F32, [(bf16_width, f32_width)])])
    gl_col, kc_col, vc_col = 0, 1, 2
    z_col = 1
    kv_col0 = 2 * NSA_WIDTH
    qm_col = (kv_col0 + 4 * NSA_KV) // MEM_WIDTH
    zm_col = qm_col + 1
    n_mem = mem.shape[0] // batch
    mem_kv, = _norm_matmul(mem, mem_norm_g, w_mem_kv.astype(BF16), [(BF16, [(0, 2 * MEM_WIDTH)])])

    k_cmp, v_cmp_t = _compress(nf, kc_col, vc_col, pe_k, pe_v, phik_w1, phik_w2, phiv_w1, phiv_w2,
                               batch, seq)
    ym = _mem_attn(nb, qm_col, mem_kv, nb, zm_col, batch, seq, n_mem)
    return _nsa_attn(nb, kv_col0, z_col, k_cmp, v_cmp_t, nf, gl_col, ym, w_out.astype(BF16), x, final_g,
                     batch, seq)


def kernel(x, mem, hawk_norm, hawk_w_in, hawk_conv_w, hawk_conv_b, hawk_gate_a_w, hawk_gate_a_b,
           hawk_gate_x_w, hawk_gate_x_b, hawk_lambda, hawk_mem_norm, hawk_w_mem_kv, hawk_w_out,
           nsa_norm, nsa_w_in, nsa_pe_k, nsa_pe_v, nsa_phi_k_w1, nsa_phi_k_w2, nsa_phi_v_w1,
           nsa_phi_v_w2, nsa_mem_norm, nsa_w_mem_kv, nsa_w_out, final_norm):
    batch, seq, d = x.shape
    assert hawk_norm.shape[0] == 1 and nsa_norm.shape[0] == 1, "one layer of each kind"
    assert seq % (ATTN_BLOCK * DIL_GROUPS[-1][1]) == 0
    x2 = x.reshape(batch * seq, d)
    mem2 = mem.reshape(batch * mem.shape[1], d)
    x2 = _hawk_layer(x2, mem2, batch, seq, hawk_norm[0], hawk_w_in[0], hawk_conv_w[0], hawk_conv_b[0],
                     hawk_gate_a_w[0], hawk_gate_a_b[0].reshape(-1), hawk_gate_x_w[0],
                     hawk_gate_x_b[0].reshape(-1), hawk_lambda[0], hawk_mem_norm[0], hawk_w_mem_kv[0],
                     hawk_w_out[0])
    out = _nsa_layer(x2, mem2, batch, seq, nsa_norm[0], nsa_w_in[0], nsa_pe_k[0], nsa_pe_v[0],
                     nsa_phi_k_w1[0], nsa_phi_k_w2[0], nsa_phi_v_w1[0], nsa_phi_v_w2[0],
                     nsa_mem_norm[0], nsa_w_mem_kv[0], nsa_w_out[0], final_norm)
    return out.reshape(batch, seq, d)
```

```python
import functools

import numpy as np
import jax
import jax.numpy as jnp
from jax import lax
from jax.experimental import pallas as pl
from jax.experimental.pallas import tpu as pltpu

F32 = jnp.float32
BF16 = jnp.bfloat16

NORM_EPS = 1e-6
NEG_INF = -1e30
LANES = 128
SUBLANES = 8
BF16_SUBLANES = 16
ATTN_BLOCK = 128
VMEM_LIMIT = 56 * 1024 * 1024

LRU_WIDTH = 1024
LRU_BLOCKS = 16
LRU_BLOCK_DIM = LRU_WIDTH // LRU_BLOCKS
LRU_PACK = 256
CONV_WIDTH = 4
LRU_C = 8.0

DIL_GROUPS = ((128, 1), (512, 4), (2048, 16))
DIL_HEADS = 4
DIL_HEAD_DIM = 128
DIL_WIDTH = DIL_HEADS * DIL_HEAD_DIM
DIL_QKV = len(DIL_GROUPS) * DIL_WIDTH
LSE_LANES = LANES // DIL_HEADS

MEM_HEADS = 4
MEM_HEAD_DIM = 64
MEM_WIDTH = MEM_HEADS * MEM_HEAD_DIM

NSA_HEADS = 16
NSA_KV_GROUPS = 2
NSA_R = NSA_HEADS // NSA_KV_GROUPS
NSA_HEAD_DIM = 64
NSA_WIDTH = NSA_HEADS * NSA_HEAD_DIM
NSA_KV = NSA_KV_GROUPS * NSA_HEAD_DIM
CMP_BLOCK = 32
CMP_STRIDE = 16
SLC_BLOCK = 64
SLC_TOP_N = 8
WIN_SIZE = 512
PHI_HIDDEN = 256
SEL_FORCE = 1e6
CMP_PER_SLC = SLC_BLOCK // CMP_STRIDE


def _alibi_slopes(n):
    return [float(v) for v in np.exp2(-8.0 * np.arange(1, n + 1) / n).astype(np.float32)]


def _params(*semantics):
    return pltpu.CompilerParams(dimension_semantics=semantics, vmem_limit_bytes=VMEM_LIMIT)


def _silu(z):
    return z * jax.nn.sigmoid(z)


def _dot_t(a, b):
    return lax.dot_general(a, b, (((1,), (1,)), ((), ())), preferred_element_type=F32)


def _rms_norm_rows(x, g):
    ms = jnp.mean(x * x, axis=-1, keepdims=True)
    return (x * lax.rsqrt(ms + NORM_EPS) * g).astype(BF16)


def _norm_matmul_kernel(*refs, dil, pieces):
    def project(xn, out_ref, w_refs, by_class=False):
        col = 0
        for w_ref in w_refs:
            width = w_ref.shape[1]
            res = jnp.dot(xn, w_ref[...], preferred_element_type=F32).astype(out_ref.dtype)
            if by_class:
                per = xn.shape[0] // dil
                for c in range(dil):
                    out_ref[c, :, col:col + width] = res[c * per:(c + 1) * per]
            else:
                out_ref[:, col:col + width] = res
            col += width

    if dil == 1:
        x_ref, g_ref = refs[:2]
        w_refs, o_refs = refs[2:2 + sum(pieces)], refs[2 + sum(pieces):]
        xn = _rms_norm_rows(x_ref[...], g_ref[...])
        first = 0
        for o_ref, n_pieces in zip(o_refs, pieces):
            project(xn, o_ref, w_refs[first:first + n_pieces])
            first += n_pieces
        return
    x_ref, g_ref, *w_refs, o_ref = refs
    tm, k = x_ref.shape
    x = x_ref[...]
    xn = x * lax.rsqrt(jnp.mean(x * x, axis=-1, keepdims=True) + NORM_EPS) * g_ref[...]
    xn = jnp.swapaxes(xn.reshape(tm // dil, dil, k), 0, 1).reshape(tm, k).astype(BF16)
    project(xn, o_ref, w_refs, by_class=True)


def _norm_matmul(x, g, w, outs, dil=1, tm=512):
    m, k = x.shape
    tm = min(tm, m)
    assert m % tm == 0 and k % LANES == 0
    pieces = [len(cols) for _, cols in outs]
    widths = [sum(width for _, width in cols) for _, cols in outs]
    w_specs = []
    for _, cols in outs:
        for col, width in cols:
            assert col % width == 0 and width % LANES == 0
            w_specs.append(pl.BlockSpec((k, width), functools.partial(lambda i, j: (0, j), j=col // width)))
    resident = [pl.BlockSpec((1, k), lambda i: (0, 0))] + w_specs
    operands = (g.reshape(1, k),) + (w,) * len(w_specs)
    kernel = functools.partial(_norm_matmul_kernel, dil=dil, pieces=pieces)
    if dil == 1:
        return pl.pallas_call(
            kernel,
            grid=(m // tm,),
            in_specs=[pl.BlockSpec((tm, k), lambda i: (i, 0))] + resident,
            out_specs=[pl.BlockSpec((tm, width), lambda i: (i, 0)) for width in widths],
            out_shape=[jax.ShapeDtypeStruct((m, width), dtype) for width, (dtype, _) in zip(widths, outs)],
            compiler_params=_params("arbitrary"),
            name="norm_matmul",
        )(x, *operands)
    per = tm // dil
    (out_dtype, _), = outs
    n, = widths
    assert tm % dil == 0 and per % BF16_SUBLANES == 0
    return pl.pallas_call(
        kernel,
        grid=(m // tm,),
        in_specs=[pl.BlockSpec((tm, k), lambda i: (i, 0))] + resident,
        out_specs=pl.BlockSpec((dil, per, n), lambda i: (0, i, 0)),
        out_shape=jax.ShapeDtypeStruct((dil, m // dil, n), out_dtype),
        compiler_params=_params("arbitrary"),
        name="norm_matmul_dil",
    )(x, *operands)


def _rglru_kernel(xa_ref, za_ref, cw_ref, cb_ref, wa_ref, ba_ref, wx_ref, bx_ref, lam_ref,
                  o_ref, xpad_ref, h_ref):
    t = pl.program_id(1)
    tt, width = xa_ref.shape
    halo = SUBLANES

    @pl.when(t == 0)
    def _():
        xpad_ref[0:halo, :] = jnp.zeros((halo, width), F32)
        h_ref[...] = jnp.zeros_like(h_ref)

    x = xa_ref[...]
    xpad_ref[halo:halo + tt, :] = x
    cw = cw_ref[...]
    y = cw[CONV_WIDTH - 1:CONV_WIDTH] * x
    for k in range(1, CONV_WIDTH):
        y = y + cw[CONV_WIDTH - 1 - k:CONV_WIDTH - k] * xpad_ref[halo - k:halo - k + tt, :]
    y = y + cb_ref[...]
    xpad_ref[0:halo, :] = x[tt - halo:tt, :]

    yb = y.astype(BF16)
    r_parts, i_parts = [], []
    for p in range(width // LRU_PACK):
        ys = yb[:, p * LRU_PACK:(p + 1) * LRU_PACK]
        r_parts.append(jnp.dot(ys, wa_ref[p], preferred_element_type=F32))
        i_parts.append(jnp.dot(ys, wx_ref[p], preferred_element_type=F32))
    r = jax.nn.sigmoid(jnp.concatenate(r_parts, axis=1) + ba_ref[...])
    gi = jax.nn.sigmoid(jnp.concatenate(i_parts, axis=1) + bx_ref[...])

    nl = -lam_ref[...]
    softplus = jnp.maximum(nl, 0.0) + jnp.log1p(jnp.exp(-jnp.abs(nl)))
    log_a = (-LRU_C) * r * softplus
    a = jnp.exp(log_a)
    w = -jnp.tanh(log_a) * (a * a + 1.0)
    mult = jnp.where(w > 0.0, w * lax.rsqrt(w), 0.0)
    b = y * gi * mult
    first = (lax.broadcasted_iota(jnp.int32, (SUBLANES, width), 0) == 0) & (t == 0)
    b = jnp.concatenate([jnp.where(first, (y * gi)[0:SUBLANES], b[0:SUBLANES]), b[SUBLANES:]], axis=0)

    groups = tt // SUBLANES
    a3 = a.reshape(groups, SUBLANES, width)
    b3 = b.reshape(groups, SUBLANES, width)
    sub = lax.broadcasted_iota(jnp.int32, (1, SUBLANES, width), 1)
    k = 1
    while k < SUBLANES:
        keep = sub >= k
        a_sh = jnp.where(keep, pltpu.roll(a3, k, axis=1), 1.0)
        b_sh = jnp.where(keep, pltpu.roll(b3, k, axis=1), 0.0)
        b3 = a3 * b_sh + b3
        a3 = a3 * a_sh
        k *= 2
    carry = jnp.broadcast_to(h_ref[...], (SUBLANES, width))
    hs = []
    for gidx in range(groups):
        hg = a3[gidx] * carry + b3[gidx]
        hs.append(hg)
        carry = jnp.broadcast_to(hg[SUBLANES - 1:SUBLANES], (SUBLANES, width))
    h = jnp.concatenate(hs, axis=0)
    h_ref[...] = carry[0:1]
    o_ref[...] = (h * _silu(za_ref[...].astype(F32))).astype(o_ref.dtype)


def _rglru(xa, za_src, za_col, conv_w, conv_b, wa, ba, wx, bx, lam, batch, seq, tt=1024):
    width = LRU_WIDTH
    nt = seq // tt
    packs = width // LRU_PACK
    vec = pl.BlockSpec((1, width), lambda b, t: (0, 0))
    gate_w = pl.BlockSpec((packs, LRU_PACK, LRU_PACK), lambda b, t: (0, 0, 0))
    return pl.pallas_call(
        _rglru_kernel,
        grid=(batch, nt),
        in_specs=[pl.BlockSpec((tt, width), lambda b, t: (b * nt + t, 0)),
                  pl.BlockSpec((tt, width), lambda b, t: (b * nt + t, za_col)),
                  pl.BlockSpec((CONV_WIDTH, width), lambda b, t: (0, 0)),
                  vec, gate_w, vec, gate_w, vec, vec],
        out_specs=pl.BlockSpec((tt, width), lambda b, t: (b * nt + t, 0)),
        out_shape=jax.ShapeDtypeStruct((batch * seq, width), BF16),
        scratch_shapes=[pltpu.VMEM((tt + SUBLANES, width), F32), pltpu.VMEM((1, width), F32)],
        compiler_params=_params("arbitrary", "arbitrary"),
        name="rglru",
    )(xa, za_src, conv_w, conv_b.reshape(1, width), wa, ba.reshape(1, width), wx, bx.reshape(1, width),
      lam.reshape(1, width))


def _pack_block_diag(w):
    per = LRU_PACK // LRU_BLOCK_DIM
    w = w.reshape(LRU_BLOCKS // per, per, LRU_BLOCK_DIM, LRU_BLOCK_DIM)
    eye = jnp.eye(per, dtype=w.dtype)
    packed = w[:, :, :, None, :] * eye[None, :, None, :, None]
    return packed.reshape(LRU_BLOCKS // per, LRU_PACK, LRU_PACK).astype(BF16)


def _dil_attn_kernel(*refs, slopes, pos_scale, max_dist, has_halo, dil, n_cls, n_blk):
    if has_halo:
        q_ref, kh_ref, k_ref, vh_ref, v_ref = refs[:5]
        out_refs = refs[5:]
    else:
        q_ref, k_ref, v_ref = refs[:3]
        out_refs = refs[3:]
    n_out = DIL_HEADS + 1
    dst_refs = out_refs[:n_out]
    stage_refs = out_refs[n_out:] if dil > 1 else dst_refs
    first_super = pl.program_id(1) == 0
    cls0 = pl.program_id(2) * n_cls
    blk = ATTN_BLOCK
    scale = DIL_HEAD_DIM ** -0.5

    def band(width, halo_live):
        row = lax.broadcasted_iota(jnp.int32, (blk, width), 0)
        col = lax.broadcasted_iota(jnp.int32, (blk, width), 1)
        dist = (width - blk) + row - col
        valid = (dist >= 0) & (dist <= max_dist)
        if halo_live is not None:
            valid = valid & ((col >= blk) | halo_live)
        distf = (dist * pos_scale).astype(F32)
        return [jnp.where(valid, -slope * distf, NEG_INF) for slope in slopes]

    bias_inner = band(2 * blk, None) if n_blk > 1 else None
    bias_first = band(2 * blk, jnp.logical_not(first_super)) if has_halo else band(blk, None)

    def scores(cc, jb):
        cur = slice(jb * blk, (jb + 1) * blk)
        bias = bias_inner if jb > 0 else bias_first
        out = []
        for h in range(DIL_HEADS):
            hs = slice(h * DIL_HEAD_DIM, (h + 1) * DIL_HEAD_DIM)
            q = q_ref[cc, cur, hs]
            if jb > 0:
                k = k_ref[cc, (jb - 1) * blk:(jb + 1) * blk, hs]
                v = v_ref[cc, (jb - 1) * blk:(jb + 1) * blk, hs]
            elif has_halo:
                k = jnp.concatenate([kh_ref[cc, :, hs], k_ref[cc, cur, hs]], axis=0)
                v = jnp.concatenate([vh_ref[cc, :, hs], v_ref[cc, cur, hs]], axis=0)
            else:
                k, v = k_ref[cc, cur, hs], v_ref[cc, cur, hs]
            out.append((_dot_t(q, k) * scale + bias[h], v))
        return out

    def finish(cc, jb, pairs):
        where = (cls0 + cc, slice(jb * blk, (jb + 1) * blk)) if dil > 1 else (slice(jb * blk, (jb + 1) * blk),)
        lses = []
        for h, (s, v) in enumerate(pairs):
            m = jnp.max(s, axis=-1, keepdims=True)
            e = jnp.exp(s - m)
            den = jnp.sum(e, axis=-1, keepdims=True)
            o = jnp.dot(e.astype(BF16), v, preferred_element_type=F32) / den
            stage_refs[h][where] = o.astype(stage_refs[h].dtype)
            lses.append(jnp.broadcast_to(m + jnp.log(den), (blk, LSE_LANES)))
        stage_refs[DIL_HEADS][where] = jnp.concatenate(lses, axis=1)

    pending = None
    for cc in range(n_cls):
        for jb in range(n_blk):
            pairs = scores(cc, jb)
            if pending is not None:
                finish(*pending)
            pending = (cc, jb, pairs)
    finish(*pending)

    if dil > 1:
        @pl.when(pl.program_id(2) == pl.num_programs(2) - 1)
        def _():
            for stage, dst in zip(stage_refs, dst_refs):
                dst[...] = jnp.swapaxes(stage[...], 0, 1).reshape(dst.shape).astype(dst.dtype)


def _dil_attn(qkv, col0, gi, batch, seq, work=8):
    window, dil = DIL_GROUPS[gi]
    sub = seq // dil
    nb = sub // ATTN_BLOCK
    n_blk = min(work, nb)
    n_cls = min(work // n_blk, dil)
    n_super = nb // n_blk
    has_halo = n_super > 1
    span = n_blk * ATTN_BLOCK
    slopes = _alibi_slopes(len(DIL_GROUPS) * DIL_HEADS)[gi * DIL_HEADS:(gi + 1) * DIL_HEADS]
    cur = lambda col: pl.BlockSpec((n_cls, span, DIL_WIDTH), lambda b, i, c: (c, b * n_super + i, col0 + col))
    halo = lambda col: pl.BlockSpec(
        (n_cls, ATTN_BLOCK, DIL_WIDTH),
        lambda b, i, c: (c, jnp.maximum((b * n_super + i) * n_blk - 1, 0), col0 + col))
    if has_halo:
        in_specs = [cur(0), halo(1), cur(1), halo(2), cur(2)]
    else:
        in_specs = [cur(0), cur(1), cur(2)]
    n_out = DIL_HEADS + 1
    *o, lse = pl.pallas_call(
        functools.partial(_dil_attn_kernel, slopes=slopes, pos_scale=dil, max_dist=window // dil,
                          has_halo=has_halo, dil=dil, n_cls=n_cls, n_blk=n_blk),
        grid=(batch, n_super, dil // n_cls),
        in_specs=in_specs,
        out_specs=[pl.BlockSpec((span * dil, LANES), lambda b, i, c: (b * n_super + i, 0))] * n_out,
        out_shape=[jax.ShapeDtypeStruct((batch * seq, LANES), BF16)] * DIL_HEADS
                  + [jax.ShapeDtypeStruct((batch * seq, LANES), F32)],
        scratch_shapes=[pltpu.VMEM((dil, span, LANES), F32)] * (n_out if dil > 1 else 0),
        compiler_params=_params("arbitrary", "arbitrary", "arbitrary"),
        name=f"dil_attn_d{dil}",
    )(*([qkv] * len(in_specs)))
    return o, lse


def _mem_attn_kernel(q_ref, k_ref, v_ref, z_ref, o_ref, ks_ref, vt_ref):
    hd = MEM_HEAD_DIM
    n_mem = k_ref.shape[0]

    @pl.when(pl.program_id(1) == 0)
    def _():
        ks_ref[...] = (k_ref[...].astype(F32) * (hd ** -0.5)).astype(BF16)
        vt = v_ref[...].astype(F32).T.astype(BF16)
        for h in range(MEM_HEADS):
            vt_ref[h, 0:hd, :] = vt[h * hd:(h + 1) * hd, :]
            vt_ref[h, hd:, :] = jnp.ones((vt_ref.shape[1] - hd, n_mem), BF16)

    heads = [slice(h * hd, (h + 1) * hd) for h in range(MEM_HEADS)]
    scores = [_dot_t(ks_ref[:, hs], q_ref[:, hs]) for hs in heads]
    outs = []
    for h, s in enumerate(scores):
        e = jnp.exp(s - jnp.max(s, axis=0, keepdims=True)).astype(BF16)
        acc = jnp.dot(vt_ref[h], e, preferred_element_type=F32)
        outs.append(acc[0:hd, :] / acc[hd:hd + 1, :])
    o = jnp.concatenate(outs, axis=0).T
    o_ref[...] = (o * _silu(z_ref[...].astype(F32))).astype(o_ref.dtype)


def _mem_attn(qsrc, q_col, kv, zsrc, z_col, batch, seq, n_mem, tq=1024):
    nq = seq // tq
    return pl.pallas_call(
        _mem_attn_kernel,
        grid=(batch, nq),
        in_specs=[pl.BlockSpec((tq, MEM_WIDTH), lambda b, i: (b * nq + i, q_col)),
                  pl.BlockSpec((n_mem, MEM_WIDTH), lambda b, i: (b, 0)),
                  pl.BlockSpec((n_mem, MEM_WIDTH), lambda b, i: (b, 1)),
                  pl.BlockSpec((tq, MEM_WIDTH), lambda b, i: (b * nq + i, z_col))],
        out_specs=pl.BlockSpec((tq, MEM_WIDTH), lambda b, i: (b * nq + i, 0)),
        out_shape=jax.ShapeDtypeStruct((batch * seq, MEM_WIDTH), BF16),
        scratch_shapes=[pltpu.VMEM((n_mem, MEM_WIDTH), BF16),
                        pltpu.VMEM((MEM_HEADS, MEM_HEAD_DIM + BF16_SUBLANES, n_mem), BF16)],
        compiler_params=_params("arbitrary", "arbitrary"),
        name="mem_attn",
    )(qsrc, kv, kv, zsrc)


def _hawk_out_kernel(*refs):
    n_groups = len(DIL_GROUPS)
    ya_ref = refs[0]
    o_refs = refs[1:1 + n_groups * DIL_HEADS]
    l_refs = refs[1 + n_groups * DIL_HEADS:1 + n_groups * (DIL_HEADS + 1)]
    zb_ref, ym_ref, w_ref, x_ref, out_ref = refs[1 + n_groups * (DIL_HEADS + 1):]
    a_end = LRU_WIDTH
    b_end = a_end + DIL_WIDTH
    y = jnp.dot(ya_ref[...], w_ref[0:a_end, :], preferred_element_type=F32)
    y = y + jnp.dot(ym_ref[...], w_ref[b_end:b_end + MEM_WIDTH, :], preferred_element_type=F32)
    parts = []
    for h in range(DIL_HEADS):
        ls = [l[:, h * LSE_LANES:h * LSE_LANES + 1] for l in l_refs]
        m = functools.reduce(jnp.maximum, ls)
        ws = [jnp.exp(l - m) for l in ls]
        num = sum(w * o_refs[gi * DIL_HEADS + h][...].astype(F32) for gi, w in enumerate(ws))
        parts.append(num / sum(ws))
    yb = (jnp.concatenate(parts, axis=1) * _silu(zb_ref[...].astype(F32))).astype(BF16)
    y = y + jnp.dot(yb, w_ref[a_end:b_end, :], preferred_element_type=F32)
    out_ref[...] = x_ref[...] + y


def _hawk_out(ya, os_, ls_, zb_src, zb_col, ym, w, x, tm=1024):
    m, d = x.shape
    row = lambda width, col=0: pl.BlockSpec((tm, width), lambda i: (i, col))
    heads = [o for group in os_ for o in group]
    return pl.pallas_call(
        _hawk_out_kernel,
        grid=(m // tm,),
        in_specs=[row(LRU_WIDTH)] + [row(DIL_HEAD_DIM)] * len(heads) + [row(LANES)] * len(ls_)
                 + [row(DIL_WIDTH, zb_col), row(MEM_WIDTH),
                    pl.BlockSpec(w.shape, lambda i: (0, 0)), row(d)],
        out_specs=row(d),
        out_shape=jax.ShapeDtypeStruct((m, d), F32),
        compiler_params=_params("arbitrary"),
        name="hawk_out",
    )(ya, *heads, *ls_, zb_src, ym, w, x)


def _compress_kernel(k_ref, v_ref, pe_ref, w1_ref, w2k_ref, w2vt_ref, ko_ref, vto_ref):
    n_blk = k_ref.shape[0] // CMP_STRIDE

    def hidden(which, src_ref):
        x = jnp.concatenate([src_ref[pl.ds(p, n_blk, stride=CMP_STRIDE), :] for p in range(CMP_STRIDE)],
                            axis=1).astype(BF16)
        first = jnp.dot(x, w1_ref[which, 0], preferred_element_type=F32)
        second = jnp.dot(x, w1_ref[which, 1], preferred_element_type=F32)
        pe = (jnp.dot(pe_ref[which, 0], w1_ref[which, 0], preferred_element_type=F32)
              + jnp.dot(pe_ref[which, 1], w1_ref[which, 1], preferred_element_type=F32))
        return _silu(first + pltpu.roll(second, n_blk - 1, axis=0) + pe[0:1, :]).astype(BF16)

    act_k, act_v = hidden(0, k_ref), hidden(1, v_ref)
    part = lambda a, g: a[:, g * PHI_HIDDEN:(g + 1) * PHI_HIDDEN]
    ks = [jnp.dot(part(act_k, g), w2k_ref[...], preferred_element_type=F32) for g in range(NSA_KV_GROUPS)]
    vts = [_dot_t(w2vt_ref[...], part(act_v, g)) for g in range(NSA_KV_GROUPS)]
    ko_ref[...] = jnp.concatenate(ks, axis=1).astype(ko_ref.dtype)
    vto_ref[...] = jnp.concatenate(vts, axis=0).astype(vto_ref.dtype)


def _compress(src, k_col, v_col, pe_k, pe_v, k_w1, k_w2, v_w1, v_w2, batch, seq):
    half = CMP_BLOCK // 2
    assert half == CMP_STRIDE and NSA_KV == LANES
    n_blk = seq // CMP_STRIDE
    hd = NSA_HEAD_DIM
    w1 = jnp.stack([k_w1, v_w1]).reshape(2, 2, half, hd, PHI_HIDDEN).astype(BF16)
    zero = jnp.zeros_like(w1)
    per_group = [jnp.concatenate([w1 if g == col else zero for col in range(NSA_KV_GROUPS)], axis=-1)
                 for g in range(NSA_KV_GROUPS)]
    w1e = jnp.stack(per_group, axis=3).reshape(2, 2, half * NSA_KV, NSA_KV_GROUPS * PHI_HIDDEN)
    pe = jnp.stack([pe_k, pe_v]).reshape(2, 2, half, 1, hd)
    pe = jnp.broadcast_to(pe, (2, 2, half, NSA_KV_GROUPS, hd)).reshape(2, 2, 1, half * NSA_KV)
    pe = jnp.broadcast_to(pe, (2, 2, SUBLANES, half * NSA_KV)).astype(BF16)
    w2k = k_w2.astype(BF16)
    w2vt = v_w2.T.astype(BF16)
    whole = lambda a: pl.BlockSpec(a.shape, lambda b: (0,) * a.ndim)
    return pl.pallas_call(
        _compress_kernel,
        grid=(batch,),
        in_specs=[pl.BlockSpec((seq, LANES), lambda b: (b, k_col)),
                  pl.BlockSpec((seq, LANES), lambda b: (b, v_col)),
                  whole(pe), whole(w1e), whole(w2k), whole(w2vt)],
        out_specs=[pl.BlockSpec((None, n_blk, NSA_KV), lambda b: (b, 0, 0)),
                   pl.BlockSpec((None, NSA_KV, n_blk), lambda b: (b, 0, 0))],
        out_shape=[jax.ShapeDtypeStruct((batch, n_blk, NSA_KV), BF16),
                   jax.ShapeDtypeStruct((batch, NSA_KV, n_blk), BF16)],
        compiler_params=_params("arbitrary"),
        name="compress",
    )(src, src, pe, w1e, w2k, w2vt)


KEY_CHUNK = 256
NSA_TQ = 256
SLOPE_PIECES = 3
N_FEATS = 2 * SLOPE_PIECES
FEAT_LANES = 32
KEY_COLS = NSA_HEAD_DIM + 2 * FEAT_LANES
N_SLC = 32


def _slope_pieces(slope):
    rest = np.float32(slope)
    pieces = []
    for _ in range(SLOPE_PIECES):
        p = np.float32(np.asarray(rest).astype(BF16))
        pieces.append(float(p))
        rest = np.float32(rest - p)
    return pieces


def _lane_table(lane, values):
    out = jnp.zeros(lane.shape, F32)
    for idx, v in enumerate(values):
        out = jnp.where(lane == idx, v, out)
    return out


def _key_feats(pos_hi, pos_lo, lane):
    return jnp.where(lane < SLOPE_PIECES, pos_hi, jnp.where(lane < N_FEATS, pos_lo, 0)).astype(F32)


def _tile_heads(x):
    return jnp.concatenate([x] * NSA_R, axis=1)


def _nsa_kernel(q_ref, kc_ref, vct_ref, ksrc_ref, vsrc_ref, kwsrc_ref, vwsrc_ref, feat_ref, hot_ref,
                gl_ref, z_ref, ym_ref, w_ref, x_ref, fin_ref, o_ref,
                ks_ref, vst_ref, kw_ref, vwt_ref, s_ref, acc_ref, imp_ref):
    i = pl.program_id(1)
    tq = q_ref.shape[0]
    hd = NSA_HEAD_DIM
    n_cmp = kc_ref.shape[0]

    @pl.when(i == 0)
    def _():
        for g in range(NSA_KV_GROUPS):
            gs = slice(g * hd, (g + 1) * hd)
            for dst, src, tail in ((ks_ref, ksrc_ref, hot_ref[...]),
                                   (kw_ref, kwsrc_ref, jnp.zeros(hot_ref.shape, BF16))):
                dst[:, g * KEY_COLS:g * KEY_COLS + hd] = src[:, gs]
                dst[:, g * KEY_COLS + hd:g * KEY_COLS + hd + FEAT_LANES] = feat_ref[...]
                dst[:, g * KEY_COLS + hd + FEAT_LANES:(g + 1) * KEY_COLS] = tail
        for c in range(vst_ref.shape[0]):
            rows = slice(c * KEY_CHUNK, (c + 1) * KEY_CHUNK)
            vst_ref[c] = vsrc_ref[rows, :].astype(F32).T.astype(BF16)
            vwt_ref[c] = vwsrc_ref[rows, :].astype(F32).T.astype(BF16)

    slopes_all = _alibi_slopes(NSA_HEADS)
    gates_t = jax.nn.sigmoid(gl_ref[...]).T
    feat_lane = lax.broadcasted_iota(jnp.int32, (tq, FEAT_LANES), 1)
    no_sel = jnp.zeros((NSA_R * tq, FEAT_LANES), BF16)
    key_row = lax.broadcasted_iota(jnp.int32, (KEY_CHUNK, tq), 0)
    t_pos = i * tq + lax.broadcasted_iota(jnp.int32, (KEY_CHUNK, tq), 1)
    ones_rows = jnp.ones((BF16_SUBLANES, KEY_CHUNK), BF16)
    win_lo = jnp.maximum(i * tq - (WIN_SIZE - 1), 0) // KEY_CHUNK
    chunks_hi = (i * tq + tq - 1) // KEY_CHUNK + 1

    groups = range(NSA_KV_GROUPS)
    q_win, q_slc, o_cmp = [], [], []
    for g in groups:
        slopes = slopes_all[g * NSA_R:(g + 1) * NSA_R]
        gs = slice(g * hd, (g + 1) * hd)
        q_parts = []
        for r in range(NSA_R):
            qr = q_ref[:, (g * NSA_R + r) * hd:(g * NSA_R + r + 1) * hd]
            feats = _lane_table(feat_lane, _slope_pieces(slopes[r]) * 2).astype(BF16)
            q_parts.append(jnp.concatenate([qr, feats], axis=1))
        q_feat = jnp.concatenate(q_parts, axis=0)
        q_aug = jnp.concatenate([q_feat, no_sel], axis=1)

        n_row = lax.broadcasted_iota(jnp.int32, (n_cmp, tq), 0)
        t_cmp = i * tq + lax.broadcasted_iota(jnp.int32, (n_cmp, tq), 1)
        visible = t_cmp >= n_row * CMP_STRIDE + (CMP_BLOCK - 1)
        cfeat_row = lax.broadcasted_iota(jnp.int32, (n_cmp, 2 * FEAT_LANES), 0)
        cfeat_lane = lax.broadcasted_iota(jnp.int32, (n_cmp, 2 * FEAT_LANES), 1)
        kc_feats = _key_feats(cfeat_row * CMP_STRIDE, 0, cfeat_lane)
        kc_aug = jnp.concatenate([kc_ref[:, gs], kc_feats.astype(BF16)], axis=1)
        s = _dot_t(kc_aug, q_aug) + _tile_heads(jnp.where(visible, 0.0, NEG_INF))
        m = jnp.max(s, axis=0, keepdims=True)
        e = jnp.exp(s - m)
        t_one = i * tq + lax.broadcasted_iota(jnp.int32, (1, NSA_R * tq), 1) % tq
        any_visible = t_one >= (CMP_BLOCK - 1)
        p = e * jnp.where(any_visible, 1.0 / jnp.sum(e, axis=0, keepdims=True), 0.0)
        o_cmp.append(jnp.dot(vct_ref[gs, :], p.astype(BF16), preferred_element_type=F32))
        p_sum = p[:, 0:tq]
        for r in range(1, NSA_R):
            p_sum = p_sum + p[:, r * tq:(r + 1) * tq]

        band = p_sum + pltpu.roll(p_sum, 1, axis=0)
        for k in range(1, CMP_PER_SLC):
            band = band + pltpu.roll(p_sum, n_cmp - k, axis=0)
        halves = []
        for half in range(tq // LANES):
            imp_ref[...] = band[:, half * LANES:(half + 1) * LANES]
            halves.append(imp_ref[pl.ds(0, N_SLC, stride=CMP_PER_SLC), :])
        imp = jnp.concatenate(halves, axis=1)
        blk_j = lax.broadcasted_iota(jnp.int32, (N_SLC, tq), 0)
        cur = (i * tq + lax.broadcasted_iota(jnp.int32, (N_SLC, tq), 1)) // SLC_BLOCK
        forced = (blk_j == 0) | (blk_j == cur) | (blk_j == cur - 1)
        v_imp = jnp.where(forced, SEL_FORCE, jnp.where(blk_j > cur, -SEL_FORCE, imp))
        rank = jnp.zeros((N_SLC, tq), F32)
        for other in range(N_SLC):
            row = v_imp[other:other + 1, :]
            ahead = (row > v_imp) | ((row == v_imp) & (blk_j > other))
            rank = rank + jnp.where(ahead, 1.0, 0.0)
        sel_bias = jnp.where(rank < SLC_TOP_N, 0.0, NEG_INF)

        padded = jnp.concatenate([sel_bias, jnp.zeros((LANES - N_SLC, tq), F32)], axis=0)
        sel_t = padded.T[:, 0:FEAT_LANES].astype(BF16)
        q_win.append(q_aug)
        q_slc.append(jnp.concatenate([q_feat, jnp.concatenate([sel_t] * NSA_R, axis=0)], axis=1))

    def attend(q_brs, k_ref, key_cols, vt_ref, lo, hi, masked_from, mask_fn):
        def scores(c, m_run, masked):
            start = pl.multiple_of(c * KEY_CHUNK, KEY_CHUNK)
            scs = [_dot_t(k_ref[pl.ds(start, KEY_CHUNK), g * key_cols:(g + 1) * key_cols], q_brs[g])
                   for g in groups]
            if masked:
                bias = _tile_heads(jnp.where(mask_fn(t_pos - (start + key_row)), 0.0, NEG_INF))
                scs = [sc + bias for sc in scs]
            for g in groups:
                s_ref[g, c] = scs[g]
            return tuple(jnp.maximum(m_run[g], jnp.max(scs[g], axis=0, keepdims=True)) for g in groups)

        m_rows = tuple(jnp.full((1, NSA_R * tq), NEG_INF, F32) for _ in groups)
        m_rows = lax.fori_loop(lo, masked_from, functools.partial(scores, masked=False), m_rows)
        m_rows = lax.fori_loop(masked_from, hi, functools.partial(scores, masked=True), m_rows)
        acc_ref[...] = jnp.zeros(acc_ref.shape, F32)

        def weighted(c, carry):
            for g in groups:
                e = jnp.exp(s_ref[g, c] - m_rows[g]).astype(BF16)
                v_ext = jnp.concatenate([vt_ref[c, g * hd:(g + 1) * hd, :], ones_rows], axis=0)
                acc_ref[g] += jnp.dot(v_ext, e, preferred_element_type=F32)
            return carry

        lax.fori_loop(lo, hi, weighted, 0)
        return [acc_ref[g, 0:hd, :] / acc_ref[g, hd:hd + 1, :] for g in groups]

    o_slc = attend(q_slc, ks_ref, KEY_COLS, vst_ref, 0, chunks_hi, (i * tq) // KEY_CHUNK,
                   lambda dist: dist >= 0)
    o_win = attend(q_win, kw_ref, KEY_COLS, vwt_ref, win_lo, chunks_hi, win_lo,
                   lambda dist: (dist >= 0) & (dist <= WIN_SIZE - 1))

    y = jnp.dot(ym_ref[...], w_ref[NSA_WIDTH:NSA_WIDTH + MEM_WIDTH, :], preferred_element_type=F32)
    for g in groups:
        def gate(kind):
            base = g * NSA_R * 3 + kind
            return jnp.concatenate([gates_t[base + 3 * r:base + 3 * r + 1, :] for r in range(NSA_R)], axis=1)

        o = gate(0) * o_cmp[g] + gate(1) * o_slc[g] + gate(2) * o_win[g]
        pairs = []
        for r in range(0, NSA_R, 2):
            two = jnp.concatenate([o[:, r * tq:(r + 1) * tq], o[:, (r + 1) * tq:(r + 2) * tq]], axis=0)
            pairs.append(two.T)
        cs = slice(g * NSA_R * hd, (g + 1) * NSA_R * hd)
        yo = (jnp.concatenate(pairs, axis=1) * _silu(z_ref[:, cs].astype(F32))).astype(BF16)
        y = y + jnp.dot(yo, w_ref[cs, :], preferred_element_type=F32)

    x = x_ref[...] + y
    ms = jnp.mean(x * x, axis=-1, keepdims=True)
    o_ref[...] = x * lax.rsqrt(ms + NORM_EPS) * fin_ref[...]


def _key_pos_feats(seq):
    assert seq // SLC_BLOCK == N_SLC <= FEAT_LANES
    pos = np.arange(seq)
    feats = np.zeros((seq, FEAT_LANES), np.float32)
    feats[:, 0:SLOPE_PIECES] = ((pos // SLC_BLOCK) * SLC_BLOCK)[:, None]
    feats[:, SLOPE_PIECES:N_FEATS] = (pos % SLC_BLOCK)[:, None]
    onehot = (np.arange(FEAT_LANES)[None, :] == (pos // SLC_BLOCK)[:, None]).astype(np.float32)
    return jnp.asarray(feats, BF16), jnp.asarray(onehot, BF16)


def _nsa_attn(nb, kv_col0, z_col, k_cmp, v_cmp_t, nf, gl_col, ym, w_out, x, final_g, batch, seq):
    tq = NSA_TQ
    nq = seq // tq
    d = x.shape[1]
    feats, onehot = _key_pos_feats(seq)
    kv_blk = kv_col0 // NSA_KV
    seq_cols = lambda col: pl.BlockSpec((seq, NSA_KV), lambda b, i: (b, kv_blk + col))
    const = lambda a: pl.BlockSpec(a.shape, lambda b, i: (0,) * a.ndim)
    per_batch = lambda a: pl.BlockSpec((None,) + a.shape[1:], lambda b, i: (b,) + (0,) * (a.ndim - 1))
    rows = lambda width, col=0: pl.BlockSpec((tq, width), lambda b, i: (b * nq + i, col))
    fin = final_g.reshape(1, d)
    return pl.pallas_call(
        _nsa_kernel,
        grid=(batch, nq),
        in_specs=[rows(NSA_WIDTH), per_batch(k_cmp), per_batch(v_cmp_t),
                  seq_cols(0), seq_cols(1), seq_cols(2), seq_cols(3), const(feats), const(onehot),
                  rows(LANES, gl_col), rows(NSA_WIDTH, z_col), rows(MEM_WIDTH), const(w_out), rows(d),
                  const(fin)],
        out_specs=rows(d),
        out_shape=jax.ShapeDtypeStruct((batch * seq, d), F32),
        scratch_shapes=[pltpu.VMEM((seq, NSA_KV_GROUPS * KEY_COLS), BF16),
                        pltpu.VMEM((seq // KEY_CHUNK, NSA_KV, KEY_CHUNK), BF16),
                        pltpu.VMEM((seq, NSA_KV_GROUPS * KEY_COLS), BF16),
                        pltpu.VMEM((seq // KEY_CHUNK, NSA_KV, KEY_CHUNK), BF16),
                        pltpu.VMEM((NSA_KV_GROUPS, seq // KEY_CHUNK, KEY_CHUNK, NSA_R * tq), F32),
                        pltpu.VMEM((NSA_KV_GROUPS, NSA_HEAD_DIM + BF16_SUBLANES, NSA_R * tq), F32),
                        pltpu.VMEM((seq // CMP_STRIDE, LANES), F32)],
        compiler_params=_params("arbitrary", "arbitrary"),
        name="nsa_attn",
    )(nb, k_cmp, v_cmp_t, nb, nb, nb, nb, feats, onehot, nf, nb, ym, w_out, x, fin)


def _hawk_layer(x, mem, batch, seq, norm_g, w_in, conv_w, conv_b, ga_w, ga_b, gx_w, gx_b, lam,
                mem_norm_g, w_mem_kv, w_out):
    xa0, za0 = 0, LRU_WIDTH
    q0 = 2 * LRU_WIDTH
    k0, v0 = q0 + DIL_QKV, q0 + 2 * DIL_QKV
    zb0 = q0 + 3 * DIL_QKV
    qm0 = zb0 + DIL_WIDTH
    zm0 = qm0 + MEM_WIDTH

    def qkv_cols(gi):
        return [(base + gi * DIL_WIDTH, DIL_WIDTH) for base in (q0, k0, v0)]

    w = w_in.astype(BF16)
    nat_cols = [(za0, LRU_WIDTH), *qkv_cols(0), (zb0, DIL_WIDTH), (qm0, MEM_WIDTH), (zm0, MEM_WIDTH)]
    xa, hb = _norm_matmul(x, norm_g, w, [(F32, [(xa0, LRU_WIDTH)]), (BF16, nat_cols)])
    za_col = 0
    qkv0_col = LRU_WIDTH // DIL_WIDTH
    zb_col = (LRU_WIDTH + 3 * DIL_WIDTH) // DIL_WIDTH
    qm_col = (LRU_WIDTH + 4 * DIL_WIDTH) // MEM_WIDTH
    zm_col = qm_col + 1
    qkv = [(hb[None], qkv0_col)]
    for gi in range(1, len(DIL_GROUPS)):
        qkv.append((_norm_matmul(x, norm_g, w, [(BF16, qkv_cols(gi))], dil=DIL_GROUPS[gi][1]), 0))
    n_mem = mem.shape[0] // batch
    mem_kv, = _norm_matmul(mem, mem_norm_g, w_mem_kv.astype(BF16), [(BF16, [(0, 2 * MEM_WIDTH)])])

    ya = _rglru(xa, hb, za_col, conv_w, conv_b, _pack_block_diag(ga_w), ga_b, _pack_block_diag(gx_w), gx_b,
                lam, batch, seq)
    attn = [_dil_attn(arr, col0, gi, batch, seq) for gi, (arr, col0) in enumerate(qkv)]
    ym = _mem_attn(hb, qm_col, mem_kv, hb, zm_col, batch, seq, n_mem)
    return _hawk_out(ya, [o for o, _ in attn], [l for _, l in attn], hb, zb_col, ym,
                     w_out.astype(BF16), x)


def _nsa_layer(x, mem, batch, seq, norm_g, w_in, pe_k, pe_v, phik_w1, phik_w2, phiv_w1, phiv_w2,
               mem_norm_g, w_mem_kv, w_out, final_g):
    kv0 = NSA_WIDTH
    gl0 = kv0 + 6 * NSA_KV
    z0 = gl0 + 3 * NSA_HEADS
    qm0 = z0 + NSA_WIDTH
    zm0 = qm0 + MEM_WIDTH
    gl_w = jnp.pad(w_in[:, gl0:z0], ((0, 0), (0, LANES - 3 * NSA_HEADS)))
    q_w = w_in[:, 0:kv0] * (NSA_HEAD_DIM ** -0.5)
    w_all = jnp.concatenate([q_w, w_in[:, z0:qm0], w_in[:, kv0 + 2 * NSA_KV:gl0], w_in[:, qm0:zm0 + MEM_WIDTH],
                             gl_w, w_in[:, kv0:kv0 + 2 * NSA_KV]], axis=1).astype(BF16)
    f32_width = LANES + 2 * NSA_KV
    bf16_width = w_all.shape[1] - f32_width
    nb, nf = _norm_matmul(x, norm_g, w_all, [(BF16, [(0, bf16_width)]), (F32, [(bf16_width, f32_width)])])
    gl_col, kc_col, vc_col = 0, 1, 2
    z_col = 1
    kv_col0 = 2 * NSA_WIDTH
    qm_col = (kv_col0 + 4 * NSA_KV) // MEM_WIDTH
    zm_col = qm_col + 1
    n_mem = mem.shape[0] // batch
    mem_kv, = _norm_matmul(mem, mem_norm_g, w_mem_kv.astype(BF16), [(BF16, [(0, 2 * MEM_WIDTH)])])

    k_cmp, v_cmp_t = _compress(nf, kc_col, vc_col, pe_k, pe_v, phik_w1, phik_w2, phiv_w1, phiv_w2,
                               batch, seq)
    ym = _mem_attn(nb, qm_col, mem_kv, nb, zm_col, batch, seq, n_mem)
    return _nsa_attn(nb, kv_col0, z_col, k_cmp, v_cmp_t, nf, gl_col, ym, w_out.astype(BF16), x, final_g,
                     batch, seq)


def kernel(x, mem, hawk_norm, hawk_w_in, hawk_conv_w, hawk_conv_b, hawk_gate_a_w, hawk_gate_a_b,
           hawk_gate_x_w, hawk_gate_x_b, hawk_lambda, hawk_mem_norm, hawk_w_mem_kv, hawk_w_out,
           nsa_norm, nsa_w_in, nsa_pe_k, nsa_pe_v, nsa_phi_k_w1, nsa_phi_k_w2, nsa_phi_v_w1,
           nsa_phi_v_w2, nsa_mem_norm, nsa_w_mem_kv, nsa_w_out, final_norm):
    batch, seq, d = x.shape
    assert hawk_norm.shape[0] == 1 and nsa_norm.shape[0] == 1, "one layer of each kind"
    assert seq % (ATTN_BLOCK * DIL_GROUPS[-1][1]) == 0
    x2 = x.reshape(batch * seq, d)
    mem2 = mem.reshape(batch * mem.shape[1], d)
    x2 = _hawk_layer(x2, mem2, batch, seq, hawk_norm[0], hawk_w_in[0], hawk_conv_w[0], hawk_conv_b[0],
                     hawk_gate_a_w[0], hawk_gate_a_b[0].reshape(-1), hawk_gate_x_w[0],
                     hawk_gate_x_b[0].reshape(-1), hawk_lambda[0], hawk_mem_norm[0], hawk_w_mem_kv[0],
                     hawk_w_out[0])
    out = _nsa_layer(x2, mem2, batch, seq, nsa_norm[0], nsa_w_in[0], nsa_pe_k[0], nsa_pe_v[0],
                     nsa_phi_k_w1[0], nsa_phi_k_w2[0], nsa_phi_v_w1[0], nsa_phi_v_w2[0],
                     nsa_mem_norm[0], nsa_w_mem_kv[0], nsa_w_out[0], final_norm)
    return out.reshape(batch, seq, d)
```

```python
import functools

import numpy as np
import jax
import jax.numpy as jnp
from jax import lax
from jax.experimental import pallas as pl
from jax.experimental.pallas import tpu as pltpu

F32 = jnp.float32
BF16 = jnp.bfloat16

NORM_EPS = 1e-6
NEG_INF = -1e30
LANES = 128
SUBLANES = 8
BF16_SUBLANES = 16
ATTN_BLOCK = 128
VMEM_LIMIT = 56 * 1024 * 1024

LRU_WIDTH = 1024
LRU_BLOCKS = 16
LRU_BLOCK_DIM = LRU_WIDTH // LRU_BLOCKS
LRU_PACK = 256
CONV_WIDTH = 4
LRU_C = 8.0

DIL_GROUPS = ((128, 1), (512, 4), (2048, 16))
DIL_HEADS = 4
DIL_HEAD_DIM = 128
DIL_WIDTH = DIL_HEADS * DIL_HEAD_DIM
DIL_QKV = len(DIL_GROUPS) * DIL_WIDTH
LSE_LANES = LANES // DIL_HEADS

MEM_HEADS = 4
MEM_HEAD_DIM = 64
MEM_WIDTH = MEM_HEADS * MEM_HEAD_DIM

NSA_HEADS = 16
NSA_KV_GROUPS = 2
NSA_R = NSA_HEADS // NSA_KV_GROUPS
NSA_HEAD_DIM = 64
NSA_WIDTH = NSA_HEADS * NSA_HEAD_DIM
NSA_KV = NSA_KV_GROUPS * NSA_HEAD_DIM
CMP_BLOCK = 32
CMP_STRIDE = 16
SLC_BLOCK = 64
SLC_TOP_N = 8
WIN_SIZE = 512
PHI_HIDDEN = 256
SEL_FORCE = 1e6
CMP_PER_SLC = SLC_BLOCK // CMP_STRIDE


def _alibi_slopes(n):
    return [float(v) for v in np.exp2(-8.0 * np.arange(1, n + 1) / n).astype(np.float32)]


def _params(*semantics):
    return pltpu.CompilerParams(dimension_semantics=semantics, vmem_limit_bytes=VMEM_LIMIT)


def _silu(z):
    return z * jax.nn.sigmoid(z)


def _dot_t(a, b):
    return lax.dot_general(a, b, (((1,), (1,)), ((), ())), preferred_element_type=F32)


def _rms_norm_rows(x, g):
    ms = jnp.mean(x * x, axis=-1, keepdims=True)
    return (x * lax.rsqrt(ms + NORM_EPS) * g).astype(BF16)


def _norm_matmul_kernel(*refs, dil, pieces):
    def project(xn, out_ref, w_refs, by_class=False):
        col = 0
        for w_ref in w_refs:
            width = w_ref.shape[1]
            res = jnp.dot(xn, w_ref[...], preferred_element_type=F32).astype(out_ref.dtype)
            if by_class:
                per = xn.shape[0] // dil
                for c in range(dil):
                    out_ref[c, :, col:col + width] = res[c * per:(c + 1) * per]
            else:
                out_ref[:, col:col + width] = res
            col += width

    if dil == 1:
        x_ref, g_ref = refs[:2]
        w_refs, o_refs = refs[2:2 + sum(pieces)], refs[2 + sum(pieces):]
        xn = _rms_norm_rows(x_ref[...], g_ref[...])
        first = 0
        for o_ref, n_pieces in zip(o_refs, pieces):
            project(xn, o_ref, w_refs[first:first + n_pieces])
            first += n_pieces
        return
    x_ref, g_ref, *w_refs, o_ref = refs
    tm, k = x_ref.shape
    x = x_ref[...]
    xn = x * lax.rsqrt(jnp.mean(x * x, axis=-1, keepdims=True) + NORM_EPS) * g_ref[...]
    xn = jnp.swapaxes(xn.reshape(tm // dil, dil, k), 0, 1).reshape(tm, k).astype(BF16)
    project(xn, o_ref, w_refs, by_class=True)


def _norm_matmul(x, g, w, outs, dil=1, tm=512):
    m, k = x.shape
    tm = min(tm, m)
    assert m % tm == 0 and k % LANES == 0
    pieces = [len(cols) for _, cols in outs]
    widths = [sum(width for _, width in cols) for _, cols in outs]
    w_specs = []
    for _, cols in outs:
        for col, width in cols:
            assert col % width == 0 and width % LANES == 0
            w_specs.append(pl.BlockSpec((k, width), functools.partial(lambda i, j: (0, j), j=col // width)))
    resident = [pl.BlockSpec((1, k), lambda i: (0, 0))] + w_specs
    operands = (g.reshape(1, k),) + (w,) * len(w_specs)
    kernel = functools.partial(_norm_matmul_kernel, dil=dil, pieces=pieces)
    if dil == 1:
        return pl.pallas_call(
            kernel,
            grid=(m // tm,),
            in_specs=[pl.BlockSpec((tm, k), lambda i: (i, 0))] + resident,
            out_specs=[pl.BlockSpec((tm, width), lambda i: (i, 0)) for width in widths],
            out_shape=[jax.ShapeDtypeStruct((m, width), dtype) for width, (dtype, _) in zip(widths, outs)],
            compiler_params=_params("arbitrary"),
            name="norm_matmul",
        )(x, *operands)
    per = tm // dil
    (out_dtype, _), = outs
    n, = widths
    assert tm % dil == 0 and per % BF16_SUBLANES == 0
    return pl.pallas_call(
        kernel,
        grid=(m // tm,),
        in_specs=[pl.BlockSpec((tm, k), lambda i: (i, 0))] + resident,
        out_specs=pl.BlockSpec((dil, per, n), lambda i: (0, i, 0)),
        out_shape=jax.ShapeDtypeStruct((dil, m // dil, n), out_dtype),
        compiler_params=_params("arbitrary"),
        name="norm_matmul_dil",
    )(x, *operands)


def _rglru_kernel(xa_ref, za_ref, cw_ref, cb_ref, wa_ref, ba_ref, wx_ref, bx_ref, lam_ref,
                  o_ref, xpad_ref, h_ref):
    t = pl.program_id(1)
    tt, width = xa_ref.shape
    halo = SUBLANES

    @pl.when(t == 0)
    def _():
        xpad_ref[0:halo, :] = jnp.zeros((halo, width), F32)
        h_ref[...] = jnp.zeros_like(h_ref)

    x = xa_ref[...]
    xpad_ref[halo:halo + tt, :] = x
    cw = cw_ref[...]
    y = cw[CONV_WIDTH - 1:CONV_WIDTH] * x
    for k in range(1, CONV_WIDTH):
        y = y + cw[CONV_WIDTH - 1 - k:CONV_WIDTH - k] * xpad_ref[halo - k:halo - k + tt, :]
    y = y + cb_ref[...]
    xpad_ref[0:halo, :] = x[tt - halo:tt, :]

    yb = y.astype(BF16)
    r_parts, i_parts = [], []
    for p in range(width // LRU_PACK):
        ys = yb[:, p * LRU_PACK:(p + 1) * LRU_PACK]
        r_parts.append(jnp.dot(ys, wa_ref[p], preferred_element_type=F32))
        i_parts.append(jnp.dot(ys, wx_ref[p], preferred_element_type=F32))
    r = jax.nn.sigmoid(jnp.concatenate(r_parts, axis=1) + ba_ref[...])
    gi = jax.nn.sigmoid(jnp.concatenate(i_parts, axis=1) + bx_ref[...])

    nl = -lam_ref[...]
    softplus = jnp.maximum(nl, 0.0) + jnp.log1p(jnp.exp(-jnp.abs(nl)))
    log_a = (-LRU_C) * r * softplus
    a = jnp.exp(log_a)
    w = -jnp.tanh(log_a) * (a * a + 1.0)
    mult = jnp.where(w > 0.0, w * lax.rsqrt(w), 0.0)
    b = y * gi * mult
    first = (lax.broadcasted_iota(jnp.int32, (SUBLANES, width), 0) == 0) & (t == 0)
    b = jnp.concatenate([jnp.where(first, (y * gi)[0:SUBLANES], b[0:SUBLANES]), b[SUBLANES:]], axis=0)

    groups = tt // SUBLANES
    a3 = a.reshape(groups, SUBLANES, width)
    b3 = b.reshape(groups, SUBLANES, width)
    sub = lax.broadcasted_iota(jnp.int32, (1, SUBLANES, width), 1)
    k = 1
    while k < SUBLANES:
        keep = sub >= k
        a_sh = jnp.where(keep, pltpu.roll(a3, k, axis=1), 1.0)
        b_sh = jnp.where(keep, pltpu.roll(b3, k, axis=1), 0.0)
        b3 = a3 * b_sh + b3
        a3 = a3 * a_sh
        k *= 2
    carry = jnp.broadcast_to(h_ref[...], (SUBLANES, width))
    hs = []
    for gidx in range(groups):
        hg = a3[gidx] * carry + b3[gidx]
        hs.append(hg)
        carry = jnp.broadcast_to(hg[SUBLANES - 1:SUBLANES], (SUBLANES, width))
    h = jnp.concatenate(hs, axis=0)
    h_ref[...] = carry[0:1]
    o_ref[...] = (h * _silu(za_ref[...].astype(F32))).astype(o_ref.dtype)


def _rglru(xa, za_src, za_col, conv_w, conv_b, wa, ba, wx, bx, lam, batch, seq, tt=1024):
    width = LRU_WIDTH
    nt = seq // tt
    packs = width // LRU_PACK
    vec = pl.BlockSpec((1, width), lambda b, t: (0, 0))
    gate_w = pl.BlockSpec((packs, LRU_PACK, LRU_PACK), lambda b, t: (0, 0, 0))
    return pl.pallas_call(
        _rglru_kernel,
        grid=(batch, nt),
        in_specs=[pl.BlockSpec((tt, width), lambda b, t: (b * nt + t, 0)),
                  pl.BlockSpec((tt, width), lambda b, t: (b * nt + t, za_col)),
                  pl.BlockSpec((CONV_WIDTH, width), lambda b, t: (0, 0)),
                  vec, gate_w, vec, gate_w, vec, vec],
        out_specs=pl.BlockSpec((tt, width), lambda b, t: (b * nt + t, 0)),
        out_shape=jax.ShapeDtypeStruct((batch * seq, width), BF16),
        scratch_shapes=[pltpu.VMEM((tt + SUBLANES, width), F32), pltpu.VMEM((1, width), F32)],
        compiler_params=_params("arbitrary", "arbitrary"),
        name="rglru",
    )(xa, za_src, conv_w, conv_b.reshape(1, width), wa, ba.reshape(1, width), wx, bx.reshape(1, width),
      lam.reshape(1, width))


def _pack_block_diag(w):
    per = LRU_PACK // LRU_BLOCK_DIM
    w = w.reshape(LRU_BLOCKS // per, per, LRU_BLOCK_DIM, LRU_BLOCK_DIM)
    eye = jnp.eye(per, dtype=w.dtype)
    packed = w[:, :, :, None, :] * eye[None, :, None, :, None]
    return packed.reshape(LRU_BLOCKS // per, LRU_PACK, LRU_PACK).astype(BF16)


def _dil_attn_kernel(*refs, slopes, pos_scale, max_dist, has_halo, dil, n_cls, n_blk):
    if has_halo:
        q_ref, kh_ref, k_ref, vh_ref, v_ref = refs[:5]
        out_refs = refs[5:]
    else:
        q_ref, k_ref, v_ref = refs[:3]
        out_refs = refs[3:]
    n_out = DIL_HEADS + 1
    dst_refs = out_refs[:n_out]
    stage_refs = out_refs[n_out:] if dil > 1 else dst_refs
    first_super = pl.program_id(1) == 0
    cls0 = pl.program_id(2) * n_cls
    blk = ATTN_BLOCK
    scale = DIL_HEAD_DIM ** -0.5

    def band(width, halo_live):
        row = lax.broadcasted_iota(jnp.int32, (blk, width), 0)
        col = lax.broadcasted_iota(jnp.int32, (blk, width), 1)
        dist = (width - blk) + row - col
        valid = (dist >= 0) & (dist <= max_dist)
        if halo_live is not None:
            valid = valid & ((col >= blk) | halo_live)
        distf = (dist * pos_scale).astype(F32)
        return [jnp.where(valid, -slope * distf, NEG_INF) for slope in slopes]

    bias_inner = band(2 * blk, None) if n_blk > 1 else None
    bias_first = band(2 * blk, jnp.logical_not(first_super)) if has_halo else band(blk, None)

    def scores(cc, jb):
        cur = slice(jb * blk, (jb + 1) * blk)
        bias = bias_inner if jb > 0 else bias_first
        out = []
        for h in range(DIL_HEADS):
            hs = slice(h * DIL_HEAD_DIM, (h + 1) * DIL_HEAD_DIM)
            q = q_ref[cc, cur, hs]
            if jb > 0:
                k = k_ref[cc, (jb - 1) * blk:(jb + 1) * blk, hs]
                v = v_ref[cc, (jb - 1) * blk:(jb + 1) * blk, hs]
            elif has_halo:
                k = jnp.concatenate([kh_ref[cc, :, hs], k_ref[cc, cur, hs]], axis=0)
                v = jnp.concatenate([vh_ref[cc, :, hs], v_ref[cc, cur, hs]], axis=0)
            else:
                k, v = k_ref[cc, cur, hs], v_ref[cc, cur, hs]
            out.append((_dot_t(q, k) * scale + bias[h], v))
        return out

    def finish(cc, jb, pairs):
        where = (cls0 + cc, slice(jb * blk, (jb + 1) * blk)) if dil > 1 else (slice(jb * blk, (jb + 1) * blk),)
        lses = []
        for h, (s, v) in enumerate(pairs):
            m = jnp.max(s, axis=-1, keepdims=True)
            e = jnp.exp(s - m)
            den = jnp.sum(e, axis=-1, keepdims=True)
            o = jnp.dot(e.astype(BF16), v, preferred_element_type=F32) / den
            stage_refs[h][where] = o.astype(stage_refs[h].dtype)
            lses.append(jnp.broadcast_to(m + jnp.log(den), (blk, LSE_LANES)))
        stage_refs[DIL_HEADS][where] = jnp.concatenate(lses, axis=1)

    pending = None
    for cc in range(n_cls):
        for jb in range(n_blk):
            pairs = scores(cc, jb)
            if pending is not None:
                finish(*pending)
            pending = (cc, jb, pairs)
    finish(*pending)

    if dil > 1:
        @pl.when(pl.program_id(2) == pl.num_programs(2) - 1)
        def _():
            for stage, dst in zip(stage_refs, dst_refs):
                dst[...] = jnp.swapaxes(stage[...], 0, 1).reshape(dst.shape).astype(dst.dtype)


def _dil_attn(qkv, col0, gi, batch, seq, work=8):
    window, dil = DIL_GROUPS[gi]
    sub = seq // dil
    nb = sub // ATTN_BLOCK
    n_blk = min(work, nb)
    n_cls = min(work // n_blk, dil)
    n_super = nb // n_blk
    has_halo = n_super > 1
    span = n_blk * ATTN_BLOCK
    slopes = _alibi_slopes(len(DIL_GROUPS) * DIL_HEADS)[gi * DIL_HEADS:(gi + 1) * DIL_HEADS]
    cur = lambda col: pl.BlockSpec((n_cls, span, DIL_WIDTH), lambda b, i, c: (c, b * n_super + i, col0 + col))
    halo = lambda col: pl.BlockSpec(
        (n_cls, ATTN_BLOCK, DIL_WIDTH),
        lambda b, i, c: (c, jnp.maximum((b * n_super + i) * n_blk - 1, 0), col0 + col))
    if has_halo:
        in_specs = [cur(0), halo(1), cur(1), halo(2), cur(2)]
    else:
        in_specs = [cur(0), cur(1), cur(2)]
    n_out = DIL_HEADS + 1
    *o, lse = pl.pallas_call(
        functools.partial(_dil_attn_kernel, slopes=slopes, pos_scale=dil, max_dist=window // dil,
                          has_halo=has_halo, dil=dil, n_cls=n_cls, n_blk=n_blk),
        grid=(batch, n_super, dil // n_cls),
        in_specs=in_specs,
        out_specs=[pl.BlockSpec((span * dil, LANES), lambda b, i, c: (b * n_super + i, 0))] * n_out,
        out_shape=[jax.ShapeDtypeStruct((batch * seq, LANES), BF16)] * DIL_HEADS
                  + [jax.ShapeDtypeStruct((batch * seq, LANES), F32)],
        scratch_shapes=[pltpu.VMEM((dil, span, LANES), F32)] * (n_out if dil > 1 else 0),
        compiler_params=_params("arbitrary", "arbitrary", "arbitrary"),
        name=f"dil_attn_d{dil}",
    )(*([qkv] * len(in_specs)))
    return o, lse


def _mem_attn_kernel(q_ref, k_ref, v_ref, z_ref, o_ref, ks_ref, vt_ref):
    hd = MEM_HEAD_DIM
    n_mem = k_ref.shape[0]

    @pl.when(pl.program_id(1) == 0)
    def _():
        ks_ref[...] = (k_ref[...].astype(F32) * (hd ** -0.5)).astype(BF16)
        vt = v_ref[...].astype(F32).T.astype(BF16)
        for h in range(MEM_HEADS):
            vt_ref[h, 0:hd, :] = vt[h * hd:(h + 1) * hd, :]
            vt_ref[h, hd:, :] = jnp.ones((vt_ref.shape[1] - hd, n_mem), BF16)

    heads = [slice(h * hd, (h + 1) * hd) for h in range(MEM_HEADS)]
    scores = [_dot_t(ks_ref[:, hs], q_ref[:, hs]) for hs in heads]
    outs = []
    for h, s in enumerate(scores):
        e = jnp.exp(s - jnp.max(s, axis=0, keepdims=True)).astype(BF16)
        acc = jnp.dot(vt_ref[h], e, preferred_element_type=F32)
        outs.append(acc[0:hd, :] / acc[hd:hd + 1, :])
    o = jnp.concatenate(outs, axis=0).T
    o_ref[...] = (o * _silu(z_ref[...].astype(F32))).astype(o_ref.dtype)


def _mem_attn(qsrc, q_col, kv, zsrc, z_col, batch, seq, n_mem, tq=1024):
    nq = seq // tq
    return pl.pallas_call(
        _mem_attn_kernel,
        grid=(batch, nq),
        in_specs=[pl.BlockSpec((tq, MEM_WIDTH), lambda b, i: (b * nq + i, q_col)),
                  pl.BlockSpec((n_mem, MEM_WIDTH), lambda b, i: (b, 0)),
                  pl.BlockSpec((n_mem, MEM_WIDTH), lambda b, i: (b, 1)),
                  pl.BlockSpec((tq, MEM_WIDTH), lambda b, i: (b * nq + i, z_col))],
        out_specs=pl.BlockSpec((tq, MEM_WIDTH), lambda b, i: (b * nq + i, 0)),
        out_shape=jax.ShapeDtypeStruct((batch * seq, MEM_WIDTH), BF16),
        scratch_shapes=[pltpu.VMEM((n_mem, MEM_WIDTH), BF16),
                        pltpu.VMEM((MEM_HEADS, MEM_HEAD_DIM + BF16_SUBLANES, n_mem), BF16)],
        compiler_params=_params("arbitrary", "arbitrary"),
        name="mem_attn",
    )(qsrc, kv, kv, zsrc)


def _hawk_out_kernel(*refs):
    n_groups = len(DIL_GROUPS)
    ya_ref = refs[0]
    o_refs = refs[1:1 + n_groups * DIL_HEADS]
    l_refs = refs[1 + n_groups * DIL_HEADS:1 + n_groups * (DIL_HEADS + 1)]
    zb_ref, ym_ref, w_ref, x_ref, out_ref = refs[1 + n_groups * (DIL_HEADS + 1):]
    a_end = LRU_WIDTH
    b_end = a_end + DIL_WIDTH
    y = jnp.dot(ya_ref[...], w_ref[0:a_end, :], preferred_element_type=F32)
    y = y + jnp.dot(ym_ref[...], w_ref[b_end:b_end + MEM_WIDTH, :], preferred_element_type=F32)
    parts = []
    for h in range(DIL_HEADS):
        ls = [l[:, h * LSE_LANES:h * LSE_LANES + 1] for l in l_refs]
        m = functools.reduce(jnp.maximum, ls)
        ws = [jnp.exp(l - m) for l in ls]
        num = sum(w * o_refs[gi * DIL_HEADS + h][...].astype(F32) for gi, w in enumerate(ws))
        parts.append(num / sum(ws))
    yb = (jnp.concatenate(parts, axis=1) * _silu(zb_ref[...].astype(F32))).astype(BF16)
    y = y + jnp.dot(yb, w_ref[a_end:b_end, :], preferred_element_type=F32)
    out_ref[...] = x_ref[...] + y


def _hawk_out(ya, os_, ls_, zb_src, zb_col, ym, w, x, tm=1024):
    m, d = x.shape
    row = lambda width, col=0: pl.BlockSpec((tm, width), lambda i: (i, col))
    heads = [o for group in os_ for o in group]
    return pl.pallas_call(
        _hawk_out_kernel,
        grid=(m // tm,),
        in_specs=[row(LRU_WIDTH)] + [row(DIL_HEAD_DIM)] * len(heads) + [row(LANES)] * len(ls_)
                 + [row(DIL_WIDTH, zb_col), row(MEM_WIDTH),
                    pl.BlockSpec(w.shape, lambda i: (0, 0)), row(d)],
        out_specs=row(d),
        out_shape=jax.ShapeDtypeStruct((m, d), F32),
        compiler_params=_params("arbitrary"),
        name="hawk_out",
    )(ya, *heads, *ls_, zb_src, ym, w, x)


def _compress_kernel(k_ref, v_ref, pe_ref, w1_ref, w2k_ref, w2vt_ref, ko_ref, vto_ref):
    n_blk = k_ref.shape[0] // CMP_STRIDE

    def hidden(which, src_ref):
        x = jnp.concatenate([src_ref[pl.ds(p, n_blk, stride=CMP_STRIDE), :] for p in range(CMP_STRIDE)],
                            axis=1).astype(BF16)
        first = jnp.dot(x, w1_ref[which, 0], preferred_element_type=F32)
        second = jnp.dot(x, w1_ref[which, 1], preferred_element_type=F32)
        pe = (jnp.dot(pe_ref[which, 0], w1_ref[which, 0], preferred_element_type=F32)
              + jnp.dot(pe_ref[which, 1], w1_ref[which, 1], preferred_element_type=F32))
        return _silu(first + pltpu.roll(second, n_blk - 1, axis=0) + pe[0:1, :]).astype(BF16)

    act_k, act_v = hidden(0, k_ref), hidden(1, v_ref)
    part = lambda a, g: a[:, g * PHI_HIDDEN:(g + 1) * PHI_HIDDEN]
    ks = [jnp.dot(part(act_k, g), w2k_ref[...], preferred_element_type=F32) for g in range(NSA_KV_GROUPS)]
    vts = [_dot_t(w2vt_ref[...], part(act_v, g)) for g in range(NSA_KV_GROUPS)]
    ko_ref[...] = jnp.concatenate(ks, axis=1).astype(ko_ref.dtype)
    vto_ref[...] = jnp.concatenate(vts, axis=0).astype(vto_ref.dtype)


def _compress(src, k_col, v_col, pe_k, pe_v, k_w1, k_w2, v_w1, v_w2, batch, seq):
    half = CMP_BLOCK // 2
    assert half == CMP_STRIDE and NSA_KV == LANES
    n_blk = seq // CMP_STRIDE
    hd = NSA_HEAD_DIM
    w1 = jnp.stack([k_w1, v_w1]).reshape(2, 2, half, hd, PHI_HIDDEN).astype(BF16)
    zero = jnp.zeros_like(w1)
    per_group = [jnp.concatenate([w1 if g == col else zero for col in range(NSA_KV_GROUPS)], axis=-1)
                 for g in range(NSA_KV_GROUPS)]
    w1e = jnp.stack(per_group, axis=3).reshape(2, 2, half * NSA_KV, NSA_KV_GROUPS * PHI_HIDDEN)
    pe = jnp.stack([pe_k, pe_v]).reshape(2, 2, half, 1, hd)
    pe = jnp.broadcast_to(pe, (2, 2, half, NSA_KV_GROUPS, hd)).reshape(2, 2, 1, half * NSA_KV)
    pe = jnp.broadcast_to(pe, (2, 2, SUBLANES, half * NSA_KV)).astype(BF16)
    w2k = k_w2.astype(BF16)
    w2vt = v_w2.T.astype(BF16)
    whole = lambda a: pl.BlockSpec(a.shape, lambda b: (0,) * a.ndim)
    return pl.pallas_call(
        _compress_kernel,
        grid=(batch,),
        in_specs=[pl.BlockSpec((seq, LANES), lambda b: (b, k_col)),
                  pl.BlockSpec((seq, LANES), lambda b: (b, v_col)),
                  whole(pe), whole(w1e), whole(w2k), whole(w2vt)],
        out_specs=[pl.BlockSpec((None, n_blk, NSA_KV), lambda b: (b, 0, 0)),
                   pl.BlockSpec((None, NSA_KV, n_blk), lambda b: (b, 0, 0))],
        out_shape=[jax.ShapeDtypeStruct((batch, n_blk, NSA_KV), BF16),
                   jax.ShapeDtypeStruct((batch, NSA_KV, n_blk), BF16)],
        compiler_params=_params("arbitrary"),
        name="compress",
    )(src, src, pe, w1e, w2k, w2vt)


KEY_CHUNK = 256
NSA_TQ = 256
SLOPE_PIECES = 3
N_FEATS = 2 * SLOPE_PIECES
FEAT_LANES = 32
KEY_COLS = NSA_HEAD_DIM + 2 * FEAT_LANES
N_SLC = 32


def _slope_pieces(slope):
    rest = np.float32(slope)
    pieces = []
    for _ in range(SLOPE_PIECES):
        p = np.float32(np.asarray(rest).astype(BF16))
        pieces.append(float(p))
        rest = np.float32(rest - p)
    return pieces


def _lane_table(lane, values):
    out = jnp.zeros(lane.shape, F32)
    for idx, v in enumerate(values):
        out = jnp.where(lane == idx, v, out)
    return out


def _key_feats(pos_hi, pos_lo, lane):
    return jnp.where(lane < SLOPE_PIECES, pos_hi, jnp.where(lane < N_FEATS, pos_lo, 0)).astype(F32)


def _tile_heads(x):
    return jnp.concatenate([x] * NSA_R, axis=1)


def _chunk_loop(lo, hi, body, init):
    n = hi - lo

    def pair(p, carry):
        c = lo + 2 * p
        return body([c, c + 1], carry)

    carry = lax.fori_loop(0, n // 2, pair, init)
    return lax.cond(n % 2 == 1, lambda cr: body([hi - 1], cr), lambda cr: cr, carry)


def _nsa_kernel(q_ref, kc_ref, vct_ref, ksrc_ref, vsrc_ref, kwsrc_ref, vwsrc_ref, feat_ref, hot_ref,
                gl_ref, z_ref, ym_ref, w_ref, x_ref, fin_ref, o_ref,
                ks_ref, vst_ref, kw_ref, vwt_ref, s_ref, acc_ref, imp_ref):
    i = pl.program_id(1)
    tq = q_ref.shape[0]
    hd = NSA_HEAD_DIM
    n_cmp = kc_ref.shape[0]

    @pl.when(i == 0)
    def _():
        for g in range(NSA_KV_GROUPS):
            gs = slice(g * hd, (g + 1) * hd)
            for dst, src, tail in ((ks_ref, ksrc_ref, hot_ref[...]),
                                   (kw_ref, kwsrc_ref, jnp.zeros(hot_ref.shape, BF16))):
                dst[:, g * KEY_COLS:g * KEY_COLS + hd] = src[:, gs]
                dst[:, g * KEY_COLS + hd:g * KEY_COLS + hd + FEAT_LANES] = feat_ref[...]
                dst[:, g * KEY_COLS + hd + FEAT_LANES:(g + 1) * KEY_COLS] = tail
        for c in range(vst_ref.shape[0]):
            rows = slice(c * KEY_CHUNK, (c + 1) * KEY_CHUNK)
            vst_ref[c] = vsrc_ref[rows, :].astype(F32).T.astype(BF16)
            vwt_ref[c] = vwsrc_ref[rows, :].astype(F32).T.astype(BF16)

    slopes_all = _alibi_slopes(NSA_HEADS)
    gates_t = jax.nn.sigmoid(gl_ref[...]).T
    feat_lane = lax.broadcasted_iota(jnp.int32, (tq, FEAT_LANES), 1)
    no_sel = jnp.zeros((NSA_R * tq, FEAT_LANES), BF16)
    key_row = lax.broadcasted_iota(jnp.int32, (KEY_CHUNK, tq), 0)
    t_pos = i * tq + lax.broadcasted_iota(jnp.int32, (KEY_CHUNK, tq), 1)
    ones_rows = jnp.ones((BF16_SUBLANES, KEY_CHUNK), BF16)
    win_lo = jnp.maximum(i * tq - (WIN_SIZE - 1), 0) // KEY_CHUNK
    chunks_hi = (i * tq + tq - 1) // KEY_CHUNK + 1

    groups = range(NSA_KV_GROUPS)
    q_win, q_slc, o_cmp = [], [], []
    for g in groups:
        slopes = slopes_all[g * NSA_R:(g + 1) * NSA_R]
        gs = slice(g * hd, (g + 1) * hd)
        q_parts = []
        for r in range(NSA_R):
            qr = q_ref[:, (g * NSA_R + r) * hd:(g * NSA_R + r + 1) * hd]
            feats = _lane_table(feat_lane, _slope_pieces(slopes[r]) * 2).astype(BF16)
            q_parts.append(jnp.concatenate([qr, feats], axis=1))
        q_feat = jnp.concatenate(q_parts, axis=0)
        q_aug = jnp.concatenate([q_feat, no_sel], axis=1)

        n_row = lax.broadcasted_iota(jnp.int32, (n_cmp, tq), 0)
        t_cmp = i * tq + lax.broadcasted_iota(jnp.int32, (n_cmp, tq), 1)
        visible = t_cmp >= n_row * CMP_STRIDE + (CMP_BLOCK - 1)
        cfeat_row = lax.broadcasted_iota(jnp.int32, (n_cmp, 2 * FEAT_LANES), 0)
        cfeat_lane = lax.broadcasted_iota(jnp.int32, (n_cmp, 2 * FEAT_LANES), 1)
        kc_feats = _key_feats(cfeat_row * CMP_STRIDE, 0, cfeat_lane)
        kc_aug = jnp.concatenate([kc_ref[:, gs], kc_feats.astype(BF16)], axis=1)
        s = _dot_t(kc_aug, q_aug) + _tile_heads(jnp.where(visible, 0.0, NEG_INF))
        m = jnp.max(s, axis=0, keepdims=True)
        e = jnp.exp(s - m)
        t_one = i * tq + lax.broadcasted_iota(jnp.int32, (1, NSA_R * tq), 1) % tq
        any_visible = t_one >= (CMP_BLOCK - 1)
        p = e * jnp.where(any_visible, 1.0 / jnp.sum(e, axis=0, keepdims=True), 0.0)
        o_cmp.append(jnp.dot(vct_ref[gs, :], p.astype(BF16), preferred_element_type=F32))
        p_sum = p[:, 0:tq]
        for r in range(1, NSA_R):
            p_sum = p_sum + p[:, r * tq:(r + 1) * tq]

        band = p_sum + pltpu.roll(p_sum, 1, axis=0)
        for k in range(1, CMP_PER_SLC):
            band = band + pltpu.roll(p_sum, n_cmp - k, axis=0)
        halves = []
        for half in range(tq // LANES):
            imp_ref[...] = band[:, half * LANES:(half + 1) * LANES]
            halves.append(imp_ref[pl.ds(0, N_SLC, stride=CMP_PER_SLC), :])
        imp = jnp.concatenate(halves, axis=1)
        blk_j = lax.broadcasted_iota(jnp.int32, (N_SLC, tq), 0)
        cur = (i * tq + lax.broadcasted_iota(jnp.int32, (N_SLC, tq), 1)) // SLC_BLOCK
        forced = (blk_j == 0) | (blk_j == cur) | (blk_j == cur - 1)
        v_imp = jnp.where(forced, SEL_FORCE, jnp.where(blk_j > cur, -SEL_FORCE, imp))
        rank = jnp.zeros((N_SLC, tq), F32)
        for other in range(N_SLC):
            row = v_imp[other:other + 1, :]
            ahead = (row > v_imp) | ((row == v_imp) & (blk_j > other))
            rank = rank + jnp.where(ahead, 1.0, 0.0)
        sel_bias = jnp.where(rank < SLC_TOP_N, 0.0, NEG_INF)

        padded = jnp.concatenate([sel_bias, jnp.zeros((LANES - N_SLC, tq), F32)], axis=0)
        sel_t = padded.T[:, 0:FEAT_LANES].astype(BF16)
        q_win.append(q_aug)
        q_slc.append(jnp.concatenate([q_feat, jnp.concatenate([sel_t] * NSA_R, axis=0)], axis=1))

    def attend(q_brs, k_ref, key_cols, vt_ref, lo, hi, masked_from, mask_fn):
        def scores(cs, m_run, masked):
            starts = [pl.multiple_of(c * KEY_CHUNK, KEY_CHUNK) for c in cs]
            scs = [[_dot_t(k_ref[pl.ds(start, KEY_CHUNK), g * key_cols:(g + 1) * key_cols], q_brs[g])
                    for g in groups] for start in starts]
            for c, start, sc in zip(cs, starts, scs):
                if masked:
                    bias = _tile_heads(jnp.where(mask_fn(t_pos - (start + key_row)), 0.0, NEG_INF))
                    sc = [x + bias for x in sc]
                for g in groups:
                    s_ref[g, c] = sc[g]
                m_run = tuple(jnp.maximum(m_run[g], jnp.max(sc[g], axis=0, keepdims=True)) for g in groups)
            return m_run

        def weighted(cs, carry):
            for c in cs:
                for g in groups:
                    e = jnp.exp(s_ref[g, c] - m_rows[g]).astype(BF16)
                    v_ext = jnp.concatenate([vt_ref[c, g * hd:(g + 1) * hd, :], ones_rows], axis=0)
                    acc_ref[g] += jnp.dot(v_ext, e, preferred_element_type=F32)
            return carry

        m_rows = tuple(jnp.full((1, NSA_R * tq), NEG_INF, F32) for _ in groups)
        m_rows = _chunk_loop(lo, masked_from, functools.partial(scores, masked=False), m_rows)
        m_rows = _chunk_loop(masked_from, hi, functools.partial(scores, masked=True), m_rows)
        acc_ref[...] = jnp.zeros(acc_ref.shape, F32)
        _chunk_loop(lo, hi, weighted, 0)
        return [acc_ref[g, 0:hd, :] / acc_ref[g, hd:hd + 1, :] for g in groups]

    o_slc = attend(q_slc, ks_ref, KEY_COLS, vst_ref, 0, chunks_hi, (i * tq) // KEY_CHUNK,
                   lambda dist: dist >= 0)
    o_win = attend(q_win, kw_ref, KEY_COLS, vwt_ref, win_lo, chunks_hi, win_lo,
                   lambda dist: (dist >= 0) & (dist <= WIN_SIZE - 1))

    y = jnp.dot(ym_ref[...], w_ref[NSA_WIDTH:NSA_WIDTH + MEM_WIDTH, :], preferred_element_type=F32)
    for g in groups:
        def gate(kind):
            base = g * NSA_R * 3 + kind
            return jnp.concatenate([gates_t[base + 3 * r:base + 3 * r + 1, :] for r in range(NSA_R)], axis=1)

        o = gate(0) * o_cmp[g] + gate(1) * o_slc[g] + gate(2) * o_win[g]
        pairs = []
        for r in range(0, NSA_R, 2):
            two = jnp.concatenate([o[:, r * tq:(r + 1) * tq], o[:, (r + 1) * tq:(r + 2) * tq]], axis=0)
            pairs.append(two.T)
        cs = slice(g * NSA_R * hd, (g + 1) * NSA_R * hd)
        yo = (jnp.concatenate(pairs, axis=1) * _silu(z_ref[:, cs].astype(F32))).astype(BF16)
        y = y + jnp.dot(yo, w_ref[cs, :], preferred_element_type=F32)

    x = x_ref[...] + y
    ms = jnp.mean(x * x, axis=-1, keepdims=True)
    o_ref[...] = x * lax.rsqrt(ms + NORM_EPS) * fin_ref[...]


def _key_pos_feats(seq):
    assert seq // SLC_BLOCK == N_SLC <= FEAT_LANES
    pos = np.arange(seq)
    feats = np.zeros((seq, FEAT_LANES), np.float32)
    feats[:, 0:SLOPE_PIECES] = ((pos // SLC_BLOCK) * SLC_BLOCK)[:, None]
    feats[:, SLOPE_PIECES:N_FEATS] = (pos % SLC_BLOCK)[:, None]
    onehot = (np.arange(FEAT_LANES)[None, :] == (pos // SLC_BLOCK)[:, None]).astype(np.float32)
    return jnp.asarray(feats, BF16), jnp.asarray(onehot, BF16)


def _nsa_attn(nb, kv_col0, z_col, k_cmp, v_cmp_t, nf, gl_col, ym, w_out, x, final_g, batch, seq):
    tq = NSA_TQ
    nq = seq // tq
    d = x.shape[1]
    feats, onehot = _key_pos_feats(seq)
    kv_blk = kv_col0 // NSA_KV
    seq_cols = lambda col: pl.BlockSpec((seq, NSA_KV), lambda b, i: (b, kv_blk + col))
    const = lambda a: pl.BlockSpec(a.shape, lambda b, i: (0,) * a.ndim)
    per_batch = lambda a: pl.BlockSpec((None,) + a.shape[1:], lambda b, i: (b,) + (0,) * (a.ndim - 1))
    rows = lambda width, col=0: pl.BlockSpec((tq, width), lambda b, i: (b * nq + i, col))
    fin = final_g.reshape(1, d)
    return pl.pallas_call(
        _nsa_kernel,
        grid=(batch, nq),
        in_specs=[rows(NSA_WIDTH), per_batch(k_cmp), per_batch(v_cmp_t),
                  seq_cols(0), seq_cols(1), seq_cols(2), seq_cols(3), const(feats), const(onehot),
                  rows(LANES, gl_col), rows(NSA_WIDTH, z_col), rows(MEM_WIDTH), const(w_out), rows(d),
                  const(fin)],
        out_specs=rows(d),
        out_shape=jax.ShapeDtypeStruct((batch * seq, d), F32),
        scratch_shapes=[pltpu.VMEM((seq, NSA_KV_GROUPS * KEY_COLS), BF16),
                        pltpu.VMEM((seq // KEY_CHUNK, NSA_KV, KEY_CHUNK), BF16),
                        pltpu.VMEM((seq, NSA_KV_GROUPS * KEY_COLS), BF16),
                        pltpu.VMEM((seq // KEY_CHUNK, NSA_KV, KEY_CHUNK), BF16),
                        pltpu.VMEM((NSA_KV_GROUPS, seq // KEY_CHUNK, KEY_CHUNK, NSA_R * tq), F32),
                        pltpu.VMEM((NSA_KV_GROUPS, NSA_HEAD_DIM + BF16_SUBLANES, NSA_R * tq), F32),
                        pltpu.VMEM((seq // CMP_STRIDE, LANES), F32)],
        compiler_params=_params("arbitrary", "arbitrary"),
        name="nsa_attn",
    )(nb, k_cmp, v_cmp_t, nb, nb, nb, nb, feats, onehot, nf, nb, ym, w_out, x, fin)


def _hawk_layer(x, mem, batch, seq, norm_g, w_in, conv_w, conv_b, ga_w, ga_b, gx_w, gx_b, lam,
                mem_norm_g, w_mem_kv, w_out):
    xa0, za0 = 0, LRU_WIDTH
    q0 = 2 * LRU_WIDTH
    k0, v0 = q0 + DIL_QKV, q0 + 2 * DIL_QKV
    zb0 = q0 + 3 * DIL_QKV
    qm0 = zb0 + DIL_WIDTH
    zm0 = qm0 + MEM_WIDTH

    def qkv_cols(gi):
        return [(base + gi * DIL_WIDTH, DIL_WIDTH) for base in (q0, k0, v0)]

    w = w_in.astype(BF16)
    nat_cols = [(za0, LRU_WIDTH), *qkv_cols(0), (zb0, DIL_WIDTH), (qm0, MEM_WIDTH), (zm0, MEM_WIDTH)]
    xa, hb = _norm_matmul(x, norm_g, w, [(F32, [(xa0, LRU_WIDTH)]), (BF16, nat_cols)])
    za_col = 0
    qkv0_col = LRU_WIDTH // DIL_WIDTH
    zb_col = (LRU_WIDTH + 3 * DIL_WIDTH) // DIL_WIDTH
    qm_col = (LRU_WIDTH + 4 * DIL_WIDTH) // MEM_WIDTH
    zm_col = qm_col + 1
    qkv = [(hb[None], qkv0_col)]
    for gi in range(1, len(DIL_GROUPS)):
        qkv.append((_norm_matmul(x, norm_g, w, [(BF16, qkv_cols(gi))], dil=DIL_GROUPS[gi][1]), 0))
    n_mem = mem.shape[0] // batch
    mem_kv, = _norm_matmul(mem, mem_norm_g, w_mem_kv.astype(BF16), [(BF16, [(0, 2 * MEM_WIDTH)])])

    ya = _rglru(xa, hb, za_col, conv_w, conv_b, _pack_block_diag(ga_w), ga_b, _pack_block_diag(gx_w), gx_b,
                lam, batch, seq)
    attn = [_dil_attn(arr, col0, gi, batch, seq) for gi, (arr, col0) in enumerate(qkv)]
    ym = _mem_attn(hb, qm_col, mem_kv, hb, zm_col, batch, seq, n_mem)
    return _hawk_out(ya, [o for o, _ in attn], [l for _, l in attn], hb, zb_col, ym,
                     w_out.astype(BF16), x)


def _nsa_layer(x, mem, batch, seq, norm_g, w_in, pe_k, pe_v, phik_w1, phik_w2, phiv_w1, phiv_w2,
               mem_norm_g, w_mem_kv, w_out, final_g):
    kv0 = NSA_WIDTH
    gl0 = kv0 + 6 * NSA_KV
    z0 = gl0 + 3 * NSA_HEADS
    qm0 = z0 + NSA_WIDTH
    zm0 = qm0 + MEM_WIDTH
    gl_w = jnp.pad(w_in[:, gl0:z0], ((0, 0), (0, LANES - 3 * NSA_HEADS)))
    q_w = w_in[:, 0:kv0] * (NSA_HEAD_DIM ** -0.5)
    w_all = jnp.concatenate([q_w, w_in[:, z0:qm0], w_in[:, kv0 + 2 * NSA_KV:gl0], w_in[:, qm0:zm0 + MEM_WIDTH],
                             gl_w, w_in[:, kv0:kv0 + 2 * NSA_KV]], axis=1).astype(BF16)
    f32_width = LANES + 2 * NSA_KV
    bf16_width = w_all.shape[1] - f32_width
    nb, nf = _norm_matmul(x, norm_g, w_all, [(BF16, [(0, bf16_width)]), (F32, [(bf16_width, f32_width)])])
    gl_col, kc_col, vc_col = 0, 1, 2
    z_col = 1
    kv_col0 = 2 * NSA_WIDTH
    qm_col = (kv_col0 + 4 * NSA_KV) // MEM_WIDTH
    zm_col = qm_col + 1
    n_mem = mem.shape[0] // batch
    mem_kv, = _norm_matmul(mem, mem_norm_g, w_mem_kv.astype(BF16), [(BF16, [(0, 2 * MEM_WIDTH)])])

    k_cmp, v_cmp_t = _compress(nf, kc_col, vc_col, pe_k, pe_v, phik_w1, phik_w2, phiv_w1, phiv_w2,
                               batch, seq)
    ym = _mem_attn(nb, qm_col, mem_kv, nb, zm_col, batch, seq, n_mem)
    return _nsa_attn(nb, kv_col0, z_col, k_cmp, v_cmp_t, nf, gl_col, ym, w_out.astype(BF16), x, final_g,
                     batch, seq)


def kernel(x, mem, hawk_norm, hawk_w_in, hawk_conv_w, hawk_conv_b, hawk_gate_a_w, hawk_gate_a_b,
           hawk_gate_x_w, hawk_gate_x_b, hawk_lambda, hawk_mem_norm, hawk_w_mem_kv, hawk_w_out,
           nsa_norm, nsa_w_in, nsa_pe_k, nsa_pe_v, nsa_phi_k_w1, nsa_phi_k_w2, nsa_phi_v_w1,
           nsa_phi_v_w2, nsa_mem_norm, nsa_w_mem_kv, nsa_w_out, final_norm):
    batch, seq, d = x.shape
    assert hawk_norm.shape[0] == 1 and nsa_norm.shape[0] == 1, "one layer of each kind"
    assert seq % (ATTN_BLOCK * DIL_GROUPS[-1][1]) == 0
    x2 = x.reshape(batch * seq, d)
    mem2 = mem.reshape(batch * mem.shape[1], d)
    x2 = _hawk_layer(x2, mem2, batch, seq, hawk_norm[0], hawk_w_in[0], hawk_conv_w[0], hawk_conv_b[0],
                     hawk_gate_a_w[0], hawk_gate_a_b[0].reshape(-1), hawk_gate_x_w[0],
                     hawk_gate_x_b[0].reshape(-1), hawk_lambda[0], hawk_mem_norm[0], hawk_w_mem_kv[0],
                     hawk_w_out[0])
    out = _nsa_layer(x2, mem2, batch, seq, nsa_norm[0], nsa_w_in[0], nsa_pe_k[0], nsa_pe_v[0],
                     nsa_phi_k_w1[0], nsa_phi_k_w2[0], nsa_phi_v_w1[0], nsa_phi_v_w2[0],
                     nsa_mem_norm[0], nsa_w_mem_kv[0], nsa_w_out[0], final_norm)
    return out.reshape(batch, seq, d)
```

```python
import functools

import numpy as np
import jax
import jax.numpy as jnp
from jax import lax
from jax.experimental import pallas as pl
from jax.experimental.pallas import tpu as pltpu

F32 = jnp.float32
BF16 = jnp.bfloat16

NORM_EPS = 1e-6
NEG_INF = -1e30
LOG2_E = 1.4426950408889634
LANES = 128
SUBLANES = 8
BF16_SUBLANES = 16
ATTN_BLOCK = 128
VMEM_LIMIT = 56 * 1024 * 1024

LRU_WIDTH = 1024
LRU_BLOCKS = 16
LRU_BLOCK_DIM = LRU_WIDTH // LRU_BLOCKS
LRU_PACK = 256
CONV_WIDTH = 4
LRU_C = 8.0

DIL_GROUPS = ((128, 1), (512, 4), (2048, 16))
DIL_HEADS = 4
DIL_HEAD_DIM = 128
DIL_WIDTH = DIL_HEADS * DIL_HEAD_DIM
DIL_QKV = len(DIL_GROUPS) * DIL_WIDTH
LSE_LANES = LANES // DIL_HEADS

MEM_HEADS = 4
MEM_HEAD_DIM = 64
MEM_WIDTH = MEM_HEADS * MEM_HEAD_DIM

NSA_HEADS = 16
NSA_KV_GROUPS = 2
NSA_R = NSA_HEADS // NSA_KV_GROUPS
NSA_HEAD_DIM = 64
NSA_WIDTH = NSA_HEADS * NSA_HEAD_DIM
NSA_KV = NSA_KV_GROUPS * NSA_HEAD_DIM
CMP_BLOCK = 32
CMP_STRIDE = 16
SLC_BLOCK = 64
SLC_TOP_N = 8
WIN_SIZE = 512
PHI_HIDDEN = 256
SEL_FORCE = 1e6
CMP_PER_SLC = SLC_BLOCK // CMP_STRIDE


def _alibi_slopes(n):
    return [float(v) for v in np.exp2(-8.0 * np.arange(1, n + 1) / n).astype(np.float32)]


def _params(*semantics):
    return pltpu.CompilerParams(dimension_semantics=semantics, vmem_limit_bytes=VMEM_LIMIT)


def _silu(z):
    return z * jax.nn.sigmoid(z)


def _dot_t(a, b):
    return lax.dot_general(a, b, (((1,), (1,)), ((), ())), preferred_element_type=F32)


def _rms_norm_rows(x, g):
    ms = jnp.mean(x * x, axis=-1, keepdims=True)
    return (x * lax.rsqrt(ms + NORM_EPS) * g).astype(BF16)


def _norm_matmul_kernel(*refs, dil, pieces):
    def project(xn, out_ref, w_refs, by_class=False):
        col = 0
        for w_ref in w_refs:
            width = w_ref.shape[1]
            res = jnp.dot(xn, w_ref[...], preferred_element_type=F32).astype(out_ref.dtype)
            if by_class:
                per = xn.shape[0] // dil
                for c in range(dil):
                    out_ref[c, :, col:col + width] = res[c * per:(c + 1) * per]
            else:
                out_ref[:, col:col + width] = res
            col += width

    if dil == 1:
        x_ref, g_ref = refs[:2]
        w_refs, o_refs = refs[2:2 + sum(pieces)], refs[2 + sum(pieces):]
        xn = _rms_norm_rows(x_ref[...], g_ref[...])
        first = 0
        for o_ref, n_pieces in zip(o_refs, pieces):
            project(xn, o_ref, w_refs[first:first + n_pieces])
            first += n_pieces
        return
    x_ref, g_ref, *w_refs, o_ref = refs
    tm, k = x_ref.shape
    x = x_ref[...]
    xn = x * lax.rsqrt(jnp.mean(x * x, axis=-1, keepdims=True) + NORM_EPS) * g_ref[...]
    xn = jnp.swapaxes(xn.reshape(tm // dil, dil, k), 0, 1).reshape(tm, k).astype(BF16)
    project(xn, o_ref, w_refs, by_class=True)


def _norm_matmul(x, g, w, outs, dil=1, tm=512):
    m, k = x.shape
    tm = min(tm, m)
    assert m % tm == 0 and k % LANES == 0
    pieces = [len(cols) for _, cols in outs]
    widths = [sum(width for _, width in cols) for _, cols in outs]
    w_specs = []
    for _, cols in outs:
        for col, width in cols:
            assert col % width == 0 and width % LANES == 0
            w_specs.append(pl.BlockSpec((k, width), functools.partial(lambda i, j: (0, j), j=col // width)))
    resident = [pl.BlockSpec((1, k), lambda i: (0, 0))] + w_specs
    operands = (g.reshape(1, k),) + (w,) * len(w_specs)
    kernel = functools.partial(_norm_matmul_kernel, dil=dil, pieces=pieces)
    if dil == 1:
        return pl.pallas_call(
            kernel,
            grid=(m // tm,),
            in_specs=[pl.BlockSpec((tm, k), lambda i: (i, 0))] + resident,
            out_specs=[pl.BlockSpec((tm, width), lambda i: (i, 0)) for width in widths],
            out_shape=[jax.ShapeDtypeStruct((m, width), dtype) for width, (dtype, _) in zip(widths, outs)],
            compiler_params=_params("arbitrary"),
            name="norm_matmul",
        )(x, *operands)
    per = tm // dil
    (out_dtype, _), = outs
    n, = widths
    assert tm % dil == 0 and per % BF16_SUBLANES == 0
    return pl.pallas_call(
        kernel,
        grid=(m // tm,),
        in_specs=[pl.BlockSpec((tm, k), lambda i: (i, 0))] + resident,
        out_specs=pl.BlockSpec((dil, per, n), lambda i: (0, i, 0)),
        out_shape=jax.ShapeDtypeStruct((dil, m // dil, n), out_dtype),
        compiler_params=_params("arbitrary"),
        name="norm_matmul_dil",
    )(x, *operands)


def _rglru_kernel(xa_ref, za_ref, cw_ref, cb_ref, wa_ref, ba_ref, wx_ref, bx_ref, lam_ref,
                  o_ref, xpad_ref, h_ref):
    t = pl.program_id(1)
    tt, width = xa_ref.shape
    halo = SUBLANES

    @pl.when(t == 0)
    def _():
        xpad_ref[0:halo, :] = jnp.zeros((halo, width), F32)
        h_ref[...] = jnp.zeros_like(h_ref)

    x = xa_ref[...]
    xpad_ref[halo:halo + tt, :] = x
    cw = cw_ref[...]
    y = cw[CONV_WIDTH - 1:CONV_WIDTH] * x
    for k in range(1, CONV_WIDTH):
        y = y + cw[CONV_WIDTH - 1 - k:CONV_WIDTH - k] * xpad_ref[halo - k:halo - k + tt, :]
    y = y + cb_ref[...]
    xpad_ref[0:halo, :] = x[tt - halo:tt, :]

    yb = y.astype(BF16)
    r_parts, i_parts = [], []
    for p in range(width // LRU_PACK):
        ys = yb[:, p * LRU_PACK:(p + 1) * LRU_PACK]
        r_parts.append(jnp.dot(ys, wa_ref[p], preferred_element_type=F32))
        i_parts.append(jnp.dot(ys, wx_ref[p], preferred_element_type=F32))
    r = jax.nn.sigmoid(jnp.concatenate(r_parts, axis=1) + ba_ref[...])
    gi = jax.nn.sigmoid(jnp.concatenate(i_parts, axis=1) + bx_ref[...])

    nl = -lam_ref[...]
    softplus = jnp.maximum(nl, 0.0) + jnp.log1p(jnp.exp(-jnp.abs(nl)))
    log_a = (-LRU_C) * r * softplus
    a = jnp.exp(log_a)
    w = -jnp.tanh(log_a) * (a * a + 1.0)
    mult = jnp.where(w > 0.0, w * lax.rsqrt(w), 0.0)
    b = y * gi * mult
    first = (lax.broadcasted_iota(jnp.int32, (SUBLANES, width), 0) == 0) & (t == 0)
    b = jnp.concatenate([jnp.where(first, (y * gi)[0:SUBLANES], b[0:SUBLANES]), b[SUBLANES:]], axis=0)

    groups = tt // SUBLANES
    a3 = a.reshape(groups, SUBLANES, width)
    b3 = b.reshape(groups, SUBLANES, width)
    sub = lax.broadcasted_iota(jnp.int32, (1, SUBLANES, width), 1)
    k = 1
    while k < SUBLANES:
        keep = sub >= k
        a_sh = jnp.where(keep, pltpu.roll(a3, k, axis=1), 1.0)
        b_sh = jnp.where(keep, pltpu.roll(b3, k, axis=1), 0.0)
        b3 = a3 * b_sh + b3
        a3 = a3 * a_sh
        k *= 2
    carry = jnp.broadcast_to(h_ref[...], (SUBLANES, width))
    hs = []
    for gidx in range(groups):
        hg = a3[gidx] * carry + b3[gidx]
        hs.append(hg)
        carry = jnp.broadcast_to(hg[SUBLANES - 1:SUBLANES], (SUBLANES, width))
    h = jnp.concatenate(hs, axis=0)
    h_ref[...] = carry[0:1]
    o_ref[...] = (h * _silu(za_ref[...].astype(F32))).astype(o_ref.dtype)


def _rglru(xa, za_src, za_col, conv_w, conv_b, wa, ba, wx, bx, lam, batch, seq, tt=1024):
    width = LRU_WIDTH
    nt = seq // tt
    packs = width // LRU_PACK
    vec = pl.BlockSpec((1, width), lambda b, t: (0, 0))
    gate_w = pl.BlockSpec((packs, LRU_PACK, LRU_PACK), lambda b, t: (0, 0, 0))
    return pl.pallas_call(
        _rglru_kernel,
        grid=(batch, nt),
        in_specs=[pl.BlockSpec((tt, width), lambda b, t: (b * nt + t, 0)),
                  pl.BlockSpec((tt, width), lambda b, t: (b * nt + t, za_col)),
                  pl.BlockSpec((CONV_WIDTH, width), lambda b, t: (0, 0)),
                  vec, gate_w, vec, gate_w, vec, vec],
        out_specs=pl.BlockSpec((tt, width), lambda b, t: (b * nt + t, 0)),
        out_shape=jax.ShapeDtypeStruct((batch * seq, width), BF16),
        scratch_shapes=[pltpu.VMEM((tt + SUBLANES, width), F32), pltpu.VMEM((1, width), F32)],
        compiler_params=_params("arbitrary", "arbitrary"),
        name="rglru",
    )(xa, za_src, conv_w, conv_b.reshape(1, width), wa, ba.reshape(1, width), wx, bx.reshape(1, width),
      lam.reshape(1, width))


def _pack_block_diag(w):
    per = LRU_PACK // LRU_BLOCK_DIM
    w = w.reshape(LRU_BLOCKS // per, per, LRU_BLOCK_DIM, LRU_BLOCK_DIM)
    eye = jnp.eye(per, dtype=w.dtype)
    packed = w[:, :, :, None, :] * eye[None, :, None, :, None]
    return packed.reshape(LRU_BLOCKS // per, LRU_PACK, LRU_PACK).astype(BF16)


def _dil_attn_kernel(*refs, slopes, pos_scale, max_dist, has_halo, dil, n_cls, n_blk):
    if has_halo:
        q_ref, kh_ref, k_ref, vh_ref, v_ref = refs[:5]
        out_refs = refs[5:]
    else:
        q_ref, k_ref, v_ref = refs[:3]
        out_refs = refs[3:]
    n_out = DIL_HEADS + 1
    dst_refs = out_refs[:n_out]
    stage_refs = out_refs[n_out:] if dil > 1 else dst_refs
    first_super = pl.program_id(1) == 0
    cls0 = pl.program_id(2) * n_cls
    blk = ATTN_BLOCK
    scale = DIL_HEAD_DIM ** -0.5

    def band(width, halo_live):
        row = lax.broadcasted_iota(jnp.int32, (blk, width), 0)
        col = lax.broadcasted_iota(jnp.int32, (blk, width), 1)
        dist = (width - blk) + row - col
        valid = (dist >= 0) & (dist <= max_dist)
        if halo_live is not None:
            valid = valid & ((col >= blk) | halo_live)
        distf = (dist * pos_scale).astype(F32)
        return [jnp.where(valid, (-slope / scale) * distf, NEG_INF) for slope in slopes]

    bias_inner = band(2 * blk, None) if n_blk > 1 else None
    bias_first = band(2 * blk, jnp.logical_not(first_super)) if has_halo else band(blk, None)

    def scores(cc, jb):
        cur = slice(jb * blk, (jb + 1) * blk)
        bias = bias_inner if jb > 0 else bias_first
        out = []
        for h in range(DIL_HEADS):
            hs = slice(h * DIL_HEAD_DIM, (h + 1) * DIL_HEAD_DIM)
            q = q_ref[cc, cur, hs]
            if jb > 0:
                k = k_ref[cc, (jb - 1) * blk:(jb + 1) * blk, hs]
                v = v_ref[cc, (jb - 1) * blk:(jb + 1) * blk, hs]
            elif has_halo:
                k = jnp.concatenate([kh_ref[cc, :, hs], k_ref[cc, cur, hs]], axis=0)
                v = jnp.concatenate([vh_ref[cc, :, hs], v_ref[cc, cur, hs]], axis=0)
            else:
                k, v = k_ref[cc, cur, hs], v_ref[cc, cur, hs]
            out.append((_dot_t(q, k) + bias[h], v))
        return out

    def finish(cc, jb, pairs):
        where = (cls0 + cc, slice(jb * blk, (jb + 1) * blk)) if dil > 1 else (slice(jb * blk, (jb + 1) * blk),)
        lses = []
        for h, (s, v) in enumerate(pairs):
            m = jnp.max(s, axis=-1, keepdims=True)
            e = jnp.exp2((s - m) * (scale * LOG2_E))
            den = jnp.sum(e, axis=-1, keepdims=True)
            o = jnp.dot(e.astype(BF16), v, preferred_element_type=F32) / den
            stage_refs[h][where] = o.astype(stage_refs[h].dtype)
            lses.append(jnp.broadcast_to(m * scale + jnp.log(den), (blk, LSE_LANES)))
        stage_refs[DIL_HEADS][where] = jnp.concatenate(lses, axis=1)

    pending = None
    for cc in range(n_cls):
        for jb in range(n_blk):
            pairs = scores(cc, jb)
            if pending is not None:
                finish(*pending)
            pending = (cc, jb, pairs)
    finish(*pending)

    if dil > 1:
        @pl.when(pl.program_id(2) == pl.num_programs(2) - 1)
        def _():
            for stage, dst in zip(stage_refs, dst_refs):
                dst[...] = jnp.swapaxes(stage[...], 0, 1).reshape(dst.shape).astype(dst.dtype)


def _dil_attn(qkv, col0, gi, batch, seq, work=8):
    window, dil = DIL_GROUPS[gi]
    sub = seq // dil
    nb = sub // ATTN_BLOCK
    n_blk = min(work, nb)
    n_cls = min(work // n_blk, dil)
    n_super = nb // n_blk
    has_halo = n_super > 1
    span = n_blk * ATTN_BLOCK
    slopes = _alibi_slopes(len(DIL_GROUPS) * DIL_HEADS)[gi * DIL_HEADS:(gi + 1) * DIL_HEADS]
    cur = lambda col: pl.BlockSpec((n_cls, span, DIL_WIDTH), lambda b, i, c: (c, b * n_super + i, col0 + col))
    halo = lambda col: pl.BlockSpec(
        (n_cls, ATTN_BLOCK, DIL_WIDTH),
        lambda b, i, c: (c, jnp.maximum((b * n_super + i) * n_blk - 1, 0), col0 + col))
    if has_halo:
        in_specs = [cur(0), halo(1), cur(1), halo(2), cur(2)]
    else:
        in_specs = [cur(0), cur(1), cur(2)]
    n_out = DIL_HEADS + 1
    *o, lse = pl.pallas_call(
        functools.partial(_dil_attn_kernel, slopes=slopes, pos_scale=dil, max_dist=window // dil,
                          has_halo=has_halo, dil=dil, n_cls=n_cls, n_blk=n_blk),
        grid=(batch, n_super, dil // n_cls),
        in_specs=in_specs,
        out_specs=[pl.BlockSpec((span * dil, LANES), lambda b, i, c: (b * n_super + i, 0))] * n_out,
        out_shape=[jax.ShapeDtypeStruct((batch * seq, LANES), BF16)] * DIL_HEADS
                  + [jax.ShapeDtypeStruct((batch * seq, LANES), F32)],
        scratch_shapes=[pltpu.VMEM((dil, span, LANES), F32)] * (n_out if dil > 1 else 0),
        compiler_params=_params("arbitrary", "arbitrary", "arbitrary"),
        name=f"dil_attn_d{dil}",
    )(*([qkv] * len(in_specs)))
    return o, lse


def _mem_attn_kernel(q_ref, k_ref, v_ref, z_ref, o_ref, ks_ref, vt_ref):
    hd = MEM_HEAD_DIM
    n_mem = k_ref.shape[0]

    @pl.when(pl.program_id(1) == 0)
    def _():
        ks_ref[...] = (k_ref[...].astype(F32) * (hd ** -0.5)).astype(BF16)
        vt = v_ref[...].astype(F32).T.astype(BF16)
        for h in range(MEM_HEADS):
            vt_ref[h, 0:hd, :] = vt[h * hd:(h + 1) * hd, :]
            vt_ref[h, hd:, :] = jnp.ones((vt_ref.shape[1] - hd, n_mem), BF16)

    heads = [slice(h * hd, (h + 1) * hd) for h in range(MEM_HEADS)]
    scores = [_dot_t(ks_ref[:, hs], q_ref[:, hs]) for hs in heads]
    outs = []
    for h, s in enumerate(scores):
        e = jnp.exp(s - jnp.max(s, axis=0, keepdims=True)).astype(BF16)
        acc = jnp.dot(vt_ref[h], e, preferred_element_type=F32)
        outs.append(acc[0:hd, :] / acc[hd:hd + 1, :])
    o = jnp.concatenate(outs, axis=0).T
    o_ref[...] = (o * _silu(z_ref[...].astype(F32))).astype(o_ref.dtype)


def _mem_attn(qsrc, q_col, kv, zsrc, z_col, batch, seq, n_mem, tq=1024):
    nq = seq // tq
    return pl.pallas_call(
        _mem_attn_kernel,
        grid=(batch, nq),
        in_specs=[pl.BlockSpec((tq, MEM_WIDTH), lambda b, i: (b * nq + i, q_col)),
                  pl.BlockSpec((n_mem, MEM_WIDTH), lambda b, i: (b, 0)),
                  pl.BlockSpec((n_mem, MEM_WIDTH), lambda b, i: (b, 1)),
                  pl.BlockSpec((tq, MEM_WIDTH), lambda b, i: (b * nq + i, z_col))],
        out_specs=pl.BlockSpec((tq, MEM_WIDTH), lambda b, i: (b * nq + i, 0)),
        out_shape=jax.ShapeDtypeStruct((batch * seq, MEM_WIDTH), BF16),
        scratch_shapes=[pltpu.VMEM((n_mem, MEM_WIDTH), BF16),
                        pltpu.VMEM((MEM_HEADS, MEM_HEAD_DIM + BF16_SUBLANES, n_mem), BF16)],
        compiler_params=_params("arbitrary", "arbitrary"),
        name="mem_attn",
    )(qsrc, kv, kv, zsrc)


def _hawk_out_kernel(*refs):
    n_groups = len(DIL_GROUPS)
    ya_ref = refs[0]
    o_refs = refs[1:1 + n_groups * DIL_HEADS]
    l_refs = refs[1 + n_groups * DIL_HEADS:1 + n_groups * (DIL_HEADS + 1)]
    zb_ref, ym_ref, w_ref, x_ref, out_ref = refs[1 + n_groups * (DIL_HEADS + 1):]
    a_end = LRU_WIDTH
    b_end = a_end + DIL_WIDTH
    y = jnp.dot(ya_ref[...], w_ref[0:a_end, :], preferred_element_type=F32)
    y = y + jnp.dot(ym_ref[...], w_ref[b_end:b_end + MEM_WIDTH, :], preferred_element_type=F32)
    parts = []
    for h in range(DIL_HEADS):
        ls = [l[:, h * LSE_LANES:h * LSE_LANES + 1] for l in l_refs]
        m = functools.reduce(jnp.maximum, ls)
        ws = [jnp.exp(l - m) for l in ls]
        num = sum(w * o_refs[gi * DIL_HEADS + h][...].astype(F32) for gi, w in enumerate(ws))
        parts.append(num / sum(ws))
    yb = (jnp.concatenate(parts, axis=1) * _silu(zb_ref[...].astype(F32))).astype(BF16)
    y = y + jnp.dot(yb, w_ref[a_end:b_end, :], preferred_element_type=F32)
    out_ref[...] = x_ref[...] + y


def _hawk_out(ya, os_, ls_, zb_src, zb_col, ym, w, x, tm=1024):
    m, d = x.shape
    row = lambda width, col=0: pl.BlockSpec((tm, width), lambda i: (i, col))
    heads = [o for group in os_ for o in group]
    return pl.pallas_call(
        _hawk_out_kernel,
        grid=(m // tm,),
        in_specs=[row(LRU_WIDTH)] + [row(DIL_HEAD_DIM)] * len(heads) + [row(LANES)] * len(ls_)
                 + [row(DIL_WIDTH, zb_col), row(MEM_WIDTH),
                    pl.BlockSpec(w.shape, lambda i: (0, 0)), row(d)],
        out_specs=row(d),
        out_shape=jax.ShapeDtypeStruct((m, d), F32),
        compiler_params=_params("arbitrary"),
        name="hawk_out",
    )(ya, *heads, *ls_, zb_src, ym, w, x)


def _compress_kernel(k_ref, v_ref, pe_ref, w1_ref, w2k_ref, w2vt_ref, ko_ref, vto_ref):
    n_blk = k_ref.shape[0] // CMP_STRIDE

    def hidden(which, src_ref):
        x = jnp.concatenate([src_ref[pl.ds(p, n_blk, stride=CMP_STRIDE), :] for p in range(CMP_STRIDE)],
                            axis=1).astype(BF16)
        first = jnp.dot(x, w1_ref[which, 0], preferred_element_type=F32)
        second = jnp.dot(x, w1_ref[which, 1], preferred_element_type=F32)
        pe = (jnp.dot(pe_ref[which, 0], w1_ref[which, 0], preferred_element_type=F32)
              + jnp.dot(pe_ref[which, 1], w1_ref[which, 1], preferred_element_type=F32))
        return _silu(first + pltpu.roll(second, n_blk - 1, axis=0) + pe[0:1, :]).astype(BF16)

    act_k, act_v = hidden(0, k_ref), hidden(1, v_ref)
    part = lambda a, g: a[:, g * PHI_HIDDEN:(g + 1) * PHI_HIDDEN]
    ks = [jnp.dot(part(act_k, g), w2k_ref[...], preferred_element_type=F32) for g in range(NSA_KV_GROUPS)]
    vts = [_dot_t(w2vt_ref[...], part(act_v, g)) for g in range(NSA_KV_GROUPS)]
    ko_ref[...] = jnp.concatenate(ks, axis=1).astype(ko_ref.dtype)
    vto_ref[...] = jnp.concatenate(vts, axis=0).astype(vto_ref.dtype)


def _compress(src, k_col, v_col, pe_k, pe_v, k_w1, k_w2, v_w1, v_w2, batch, seq):
    half = CMP_BLOCK // 2
    assert half == CMP_STRIDE and NSA_KV == LANES
    n_blk = seq // CMP_STRIDE
    hd = NSA_HEAD_DIM
    w1 = jnp.stack([k_w1, v_w1]).reshape(2, 2, half, hd, PHI_HIDDEN).astype(BF16)
    zero = jnp.zeros_like(w1)
    per_group = [jnp.concatenate([w1 if g == col else zero for col in range(NSA_KV_GROUPS)], axis=-1)
                 for g in range(NSA_KV_GROUPS)]
    w1e = jnp.stack(per_group, axis=3).reshape(2, 2, half * NSA_KV, NSA_KV_GROUPS * PHI_HIDDEN)
    pe = jnp.stack([pe_k, pe_v]).reshape(2, 2, half, 1, hd)
    pe = jnp.broadcast_to(pe, (2, 2, half, NSA_KV_GROUPS, hd)).reshape(2, 2, 1, half * NSA_KV)
    pe = jnp.broadcast_to(pe, (2, 2, SUBLANES, half * NSA_KV)).astype(BF16)
    w2k = k_w2.astype(BF16)
    w2vt = v_w2.T.astype(BF16)
    whole = lambda a: pl.BlockSpec(a.shape, lambda b: (0,) * a.ndim)
    return pl.pallas_call(
        _compress_kernel,
        grid=(batch,),
        in_specs=[pl.BlockSpec((seq, LANES), lambda b: (b, k_col)),
                  pl.BlockSpec((seq, LANES), lambda b: (b, v_col)),
                  whole(pe), whole(w1e), whole(w2k), whole(w2vt)],
        out_specs=[pl.BlockSpec((None, n_blk, NSA_KV), lambda b: (b, 0, 0)),
                   pl.BlockSpec((None, NSA_KV, n_blk), lambda b: (b, 0, 0))],
        out_shape=[jax.ShapeDtypeStruct((batch, n_blk, NSA_KV), BF16),
                   jax.ShapeDtypeStruct((batch, NSA_KV, n_blk), BF16)],
        compiler_params=_params("arbitrary"),
        name="compress",
    )(src, src, pe, w1e, w2k, w2vt)


KEY_CHUNK = 256
NSA_TQ = 256
SLOPE_PIECES = 3
N_FEATS = 2 * SLOPE_PIECES
FEAT_LANES = 32
KEY_COLS = NSA_HEAD_DIM + 2 * FEAT_LANES
N_SLC = 32


def _slope_pieces(slope):
    rest = np.float32(slope)
    pieces = []
    for _ in range(SLOPE_PIECES):
        p = np.float32(np.asarray(rest).astype(BF16))
        pieces.append(float(p))
        rest = np.float32(rest - p)
    return pieces


def _lane_table(lane, values):
    out = jnp.zeros(lane.shape, F32)
    for idx, v in enumerate(values):
        out = jnp.where(lane == idx, v, out)
    return out


def _key_feats(pos_hi, pos_lo, lane):
    return jnp.where(lane < SLOPE_PIECES, pos_hi, jnp.where(lane < N_FEATS, pos_lo, 0)).astype(F32)


def _tile_heads(x):
    return jnp.concatenate([x] * NSA_R, axis=1)


def _chunk_loop(lo, hi, body, init, widths=(4, 2, 1)):
    carry, start = init, lo
    for idx, w in enumerate(widths):
        count = (hi - start) // w

        def step(p, cr, start=start, w=w):
            first = start + p * w
            return body([first + j for j in range(w)], cr)

        if idx == 0:
            carry = lax.fori_loop(0, count, step, carry)
        else:
            carry = lax.cond(count > 0, functools.partial(step, 0), lambda cr: cr, carry)
        start = start + count * w
    return carry


def _nsa_kernel(q_ref, kc_ref, vct_ref, ksrc_ref, vsrc_ref, kwsrc_ref, vwsrc_ref, feat_ref, hot_ref,
                gl_ref, z_ref, ym_ref, w_ref, x_ref, fin_ref, o_ref,
                ks_ref, vst_ref, kw_ref, vwt_ref, s_ref, acc_ref, imp_ref):
    i = pl.program_id(1)
    tq = q_ref.shape[0]
    hd = NSA_HEAD_DIM
    n_cmp = kc_ref.shape[0]

    @pl.when(i == 0)
    def _():
        for g in range(NSA_KV_GROUPS):
            gs = slice(g * hd, (g + 1) * hd)
            for dst, src, tail in ((ks_ref, ksrc_ref, hot_ref[...]),
                                   (kw_ref, kwsrc_ref, jnp.zeros(hot_ref.shape, BF16))):
                dst[:, g * KEY_COLS:g * KEY_COLS + hd] = src[:, gs]
                dst[:, g * KEY_COLS + hd:g * KEY_COLS + hd + FEAT_LANES] = feat_ref[...]
                dst[:, g * KEY_COLS + hd + FEAT_LANES:(g + 1) * KEY_COLS] = tail
        for c in range(vst_ref.shape[0]):
            rows = slice(c * KEY_CHUNK, (c + 1) * KEY_CHUNK)
            vst_ref[c] = vsrc_ref[rows, :].astype(F32).T.astype(BF16)
            vwt_ref[c] = vwsrc_ref[rows, :].astype(F32).T.astype(BF16)

    slopes_all = _alibi_slopes(NSA_HEADS)
    gates_t = jax.nn.sigmoid(gl_ref[...]).T
    feat_lane = lax.broadcasted_iota(jnp.int32, (tq, FEAT_LANES), 1)
    no_sel = jnp.zeros((NSA_R * tq, FEAT_LANES), BF16)
    key_row = lax.broadcasted_iota(jnp.int32, (KEY_CHUNK, tq), 0)
    t_pos = i * tq + lax.broadcasted_iota(jnp.int32, (KEY_CHUNK, tq), 1)
    ones_rows = jnp.ones((BF16_SUBLANES, KEY_CHUNK), BF16)
    win_lo = jnp.maximum(i * tq - (WIN_SIZE - 1), 0) // KEY_CHUNK
    chunks_hi = (i * tq + tq - 1) // KEY_CHUNK + 1

    groups = range(NSA_KV_GROUPS)
    q_win, q_slc, o_cmp = [], [], []
    for g in groups:
        slopes = slopes_all[g * NSA_R:(g + 1) * NSA_R]
        gs = slice(g * hd, (g + 1) * hd)
        q_parts = []
        for r in range(NSA_R):
            qr = q_ref[:, (g * NSA_R + r) * hd:(g * NSA_R + r + 1) * hd]
            feats = _lane_table(feat_lane, _slope_pieces(slopes[r]) * 2).astype(BF16)
            q_parts.append(jnp.concatenate([qr, feats], axis=1))
        q_feat = jnp.concatenate(q_parts, axis=0)
        q_aug = jnp.concatenate([q_feat, no_sel], axis=1)

        n_row = lax.broadcasted_iota(jnp.int32, (n_cmp, tq), 0)
        t_cmp = i * tq + lax.broadcasted_iota(jnp.int32, (n_cmp, tq), 1)
        visible = t_cmp >= n_row * CMP_STRIDE + (CMP_BLOCK - 1)
        cfeat_row = lax.broadcasted_iota(jnp.int32, (n_cmp, 2 * FEAT_LANES), 0)
        cfeat_lane = lax.broadcasted_iota(jnp.int32, (n_cmp, 2 * FEAT_LANES), 1)
        kc_feats = _key_feats(cfeat_row * CMP_STRIDE, 0, cfeat_lane)
        kc_aug = jnp.concatenate([kc_ref[:, gs], kc_feats.astype(BF16)], axis=1)
        s = _dot_t(kc_aug, q_aug) + _tile_heads(jnp.where(visible, 0.0, NEG_INF))
        m = jnp.max(s, axis=0, keepdims=True)
        e = jnp.exp(s - m)
        t_one = i * tq + lax.broadcasted_iota(jnp.int32, (1, NSA_R * tq), 1) % tq
        any_visible = t_one >= (CMP_BLOCK - 1)
        p = e * jnp.where(any_visible, 1.0 / jnp.sum(e, axis=0, keepdims=True), 0.0)
        o_cmp.append(jnp.dot(vct_ref[gs, :], p.astype(BF16), preferred_element_type=F32))
        p_sum = p[:, 0:tq]
        for r in range(1, NSA_R):
            p_sum = p_sum + p[:, r * tq:(r + 1) * tq]

        band = p_sum + pltpu.roll(p_sum, 1, axis=0)
        for k in range(1, CMP_PER_SLC):
            band = band + pltpu.roll(p_sum, n_cmp - k, axis=0)
        halves = []
        for half in range(tq // LANES):
            imp_ref[...] = band[:, half * LANES:(half + 1) * LANES]
            halves.append(imp_ref[pl.ds(0, N_SLC, stride=CMP_PER_SLC), :])
        imp = jnp.concatenate(halves, axis=1)
        blk_j = lax.broadcasted_iota(jnp.int32, (N_SLC, tq), 0)
        cur = (i * tq + lax.broadcasted_iota(jnp.int32, (N_SLC, tq), 1)) // SLC_BLOCK
        forced = (blk_j == 0) | (blk_j == cur) | (blk_j == cur - 1)
        v_imp = jnp.where(forced, SEL_FORCE, jnp.where(blk_j > cur, -SEL_FORCE, imp))
        rank = jnp.zeros((N_SLC, tq), F32)
        for other in range(N_SLC):
            row = v_imp[other:other + 1, :]
            ahead = (row > v_imp) | ((row == v_imp) & (blk_j > other))
            rank = rank + jnp.where(ahead, 1.0, 0.0)
        sel_bias = jnp.where(rank < SLC_TOP_N, 0.0, NEG_INF)

        padded = jnp.concatenate([sel_bias, jnp.zeros((LANES - N_SLC, tq), F32)], axis=0)
        sel_t = padded.T[:, 0:FEAT_LANES].astype(BF16)
        q_win.append(q_aug)
        q_slc.append(jnp.concatenate([q_feat, jnp.concatenate([sel_t] * NSA_R, axis=0)], axis=1))

    def attend(q_brs, k_ref, key_cols, vt_ref, lo, hi, masked_from, mask_fn):
        def scores(cs, m_run, masked):
            starts = [pl.multiple_of(c * KEY_CHUNK, KEY_CHUNK) for c in cs]
            scs = [[_dot_t(k_ref[pl.ds(start, KEY_CHUNK), g * key_cols:(g + 1) * key_cols], q_brs[g])
                    for g in groups] for start in starts]
            for c, start, sc in zip(cs, starts, scs):
                if masked:
                    bias = _tile_heads(jnp.where(mask_fn(t_pos - (start + key_row)), 0.0, NEG_INF))
                    sc = [x + bias for x in sc]
                for g in groups:
                    s_ref[g, c] = sc[g]
                m_run = tuple(jnp.maximum(m_run[g], jnp.max(sc[g], axis=0, keepdims=True)) for g in groups)
            return m_run

        def weighted(cs, carry):
            for c in cs:
                for g in groups:
                    e = jnp.exp(s_ref[g, c] - m_rows[g]).astype(BF16)
                    v_ext = jnp.concatenate([vt_ref[c, g * hd:(g + 1) * hd, :], ones_rows], axis=0)
                    acc_ref[g] += jnp.dot(v_ext, e, preferred_element_type=F32)
            return carry

        m_rows = tuple(jnp.full((1, NSA_R * tq), NEG_INF, F32) for _ in groups)
        m_rows = _chunk_loop(lo, masked_from, functools.partial(scores, masked=False), m_rows)
        m_rows = _chunk_loop(masked_from, hi, functools.partial(scores, masked=True), m_rows)
        acc_ref[...] = jnp.zeros(acc_ref.shape, F32)
        _chunk_loop(lo, hi, weighted, 0)
        return [acc_ref[g, 0:hd, :] / acc_ref[g, hd:hd + 1, :] for g in groups]

    o_slc = attend(q_slc, ks_ref, KEY_COLS, vst_ref, 0, chunks_hi, (i * tq) // KEY_CHUNK,
                   lambda dist: dist >= 0)
    o_win = attend(q_win, kw_ref, KEY_COLS, vwt_ref, win_lo, chunks_hi, win_lo,
                   lambda dist: (dist >= 0) & (dist <= WIN_SIZE - 1))

    y = jnp.dot(ym_ref[...], w_ref[NSA_WIDTH:NSA_WIDTH + MEM_WIDTH, :], preferred_element_type=F32)
    for g in groups:
        def gate(kind):
            base = g * NSA_R * 3 + kind
            return jnp.concatenate([gates_t[base + 3 * r:base + 3 * r + 1, :] for r in range(NSA_R)], axis=1)

        o = gate(0) * o_cmp[g] + gate(1) * o_slc[g] + gate(2) * o_win[g]
        pairs = []
        for r in range(0, NSA_R, 2):
            two = jnp.concatenate([o[:, r * tq:(r + 1) * tq], o[:, (r + 1) * tq:(r + 2) * tq]], axis=0)
            pairs.append(two.T)
        cs = slice(g * NSA_R * hd, (g + 1) * NSA_R * hd)
        yo = (jnp.concatenate(pairs, axis=1) * _silu(z_ref[:, cs].astype(F32))).astype(BF16)
        y = y + jnp.dot(yo, w_ref[cs, :], preferred_element_type=F32)

    x = x_ref[...] + y
    ms = jnp.mean(x * x, axis=-1, keepdims=True)
    o_ref[...] = x * lax.rsqrt(ms + NORM_EPS) * fin_ref[...]


def _key_pos_feats(seq):
    assert seq // SLC_BLOCK == N_SLC <= FEAT_LANES
    pos = np.arange(seq)
    feats = np.zeros((seq, FEAT_LANES), np.float32)
    feats[:, 0:SLOPE_PIECES] = ((pos // SLC_BLOCK) * SLC_BLOCK)[:, None]
    feats[:, SLOPE_PIECES:N_FEATS] = (pos % SLC_BLOCK)[:, None]
    onehot = (np.arange(FEAT_LANES)[None, :] == (pos // SLC_BLOCK)[:, None]).astype(np.float32)
    return jnp.asarray(feats, BF16), jnp.asarray(onehot, BF16)


def _nsa_attn(nb, kv_col0, z_col, k_cmp, v_cmp_t, nf, gl_col, ym, w_out, x, final_g, batch, seq):
    tq = NSA_TQ
    nq = seq // tq
    d = x.shape[1]
    feats, onehot = _key_pos_feats(seq)
    kv_blk = kv_col0 // NSA_KV
    seq_cols = lambda col: pl.BlockSpec((seq, NSA_KV), lambda b, i: (b, kv_blk + col))
    const = lambda a: pl.BlockSpec(a.shape, lambda b, i: (0,) * a.ndim)
    per_batch = lambda a: pl.BlockSpec((None,) + a.shape[1:], lambda b, i: (b,) + (0,) * (a.ndim - 1))
    rows = lambda width, col=0: pl.BlockSpec((tq, width), lambda b, i: (b * nq + i, col))
    fin = final_g.reshape(1, d)
    return pl.pallas_call(
        _nsa_kernel,
        grid=(batch, nq),
        in_specs=[rows(NSA_WIDTH), per_batch(k_cmp), per_batch(v_cmp_t),
                  seq_cols(0), seq_cols(1), seq_cols(2), seq_cols(3), const(feats), const(onehot),
                  rows(LANES, gl_col), rows(NSA_WIDTH, z_col), rows(MEM_WIDTH), const(w_out), rows(d),
                  const(fin)],
        out_specs=rows(d),
        out_shape=jax.ShapeDtypeStruct((batch * seq, d), F32),
        scratch_shapes=[pltpu.VMEM((seq, NSA_KV_GROUPS * KEY_COLS), BF16),
                        pltpu.VMEM((seq // KEY_CHUNK, NSA_KV, KEY_CHUNK), BF16),
                        pltpu.VMEM((seq, NSA_KV_GROUPS * KEY_COLS), BF16),
                        pltpu.VMEM((seq // KEY_CHUNK, NSA_KV, KEY_CHUNK), BF16),
                        pltpu.VMEM((NSA_KV_GROUPS, seq // KEY_CHUNK, KEY_CHUNK, NSA_R * tq), F32),
                        pltpu.VMEM((NSA_KV_GROUPS, NSA_HEAD_DIM + BF16_SUBLANES, NSA_R * tq), F32),
                        pltpu.VMEM((seq // CMP_STRIDE, LANES), F32)],
        compiler_params=_params("arbitrary", "arbitrary"),
        name="nsa_attn",
    )(nb, k_cmp, v_cmp_t, nb, nb, nb, nb, feats, onehot, nf, nb, ym, w_out, x, fin)


def _hawk_layer(x, mem, batch, seq, norm_g, w_in, conv_w, conv_b, ga_w, ga_b, gx_w, gx_b, lam,
                mem_norm_g, w_mem_kv, w_out):
    xa0, za0 = 0, LRU_WIDTH
    q0 = 2 * LRU_WIDTH
    k0, v0 = q0 + DIL_QKV, q0 + 2 * DIL_QKV
    zb0 = q0 + 3 * DIL_QKV
    qm0 = zb0 + DIL_WIDTH
    zm0 = qm0 + MEM_WIDTH

    def qkv_cols(gi):
        return [(base + gi * DIL_WIDTH, DIL_WIDTH) for base in (q0, k0, v0)]

    w = w_in.astype(BF16)
    nat_cols = [(za0, LRU_WIDTH), *qkv_cols(0), (zb0, DIL_WIDTH), (qm0, MEM_WIDTH), (zm0, MEM_WIDTH)]
    xa, hb = _norm_matmul(x, norm_g, w, [(F32, [(xa0, LRU_WIDTH)]), (BF16, nat_cols)])
    za_col = 0
    qkv0_col = LRU_WIDTH // DIL_WIDTH
    zb_col = (LRU_WIDTH + 3 * DIL_WIDTH) // DIL_WIDTH
    qm_col = (LRU_WIDTH + 4 * DIL_WIDTH) // MEM_WIDTH
    zm_col = qm_col + 1
    qkv = [(hb[None], qkv0_col)]
    for gi in range(1, len(DIL_GROUPS)):
        qkv.append((_norm_matmul(x, norm_g, w, [(BF16, qkv_cols(gi))], dil=DIL_GROUPS[gi][1]), 0))
    n_mem = mem.shape[0] // batch
    mem_kv, = _norm_matmul(mem, mem_norm_g, w_mem_kv.astype(BF16), [(BF16, [(0, 2 * MEM_WIDTH)])])

    ya = _rglru(xa, hb, za_col, conv_w, conv_b, _pack_block_diag(ga_w), ga_b, _pack_block_diag(gx_w), gx_b,
                lam, batch, seq)
    attn = [_dil_attn(arr, col0, gi, batch, seq) for gi, (arr, col0) in enumerate(qkv)]
    ym = _mem_attn(hb, qm_col, mem_kv, hb, zm_col, batch, seq, n_mem)
    return _hawk_out(ya, [o for o, _ in attn], [l for _, l in attn], hb, zb_col, ym,
                     w_out.astype(BF16), x)


def _nsa_layer(x, mem, batch, seq, norm_g, w_in, pe_k, pe_v, phik_w1, phik_w2, phiv_w1, phiv_w2,
               mem_norm_g, w_mem_kv, w_out, final_g):
    kv0 = NSA_WIDTH
    gl0 = kv0 + 6 * NSA_KV
    z0 = gl0 + 3 * NSA_HEADS
    qm0 = z0 + NSA_WIDTH
    zm0 = qm0 + MEM_WIDTH
    gl_w = jnp.pad(w_in[:, gl0:z0], ((0, 0), (0, LANES - 3 * NSA_HEADS)))
    q_w = w_in[:, 0:kv0] * (NSA_HEAD_DIM ** -0.5)
    w_all = jnp.concatenate([q_w, w_in[:, z0:qm0], w_in[:, kv0 + 2 * NSA_KV:gl0], w_in[:, qm0:zm0 + MEM_WIDTH],
                             gl_w, w_in[:, kv0:kv0 + 2 * NSA_KV]], axis=1).astype(BF16)
    f32_width = LANES + 2 * NSA_KV
    bf16_width = w_all.shape[1] - f32_width
    nb, nf = _norm_matmul(x, norm_g, w_all, [(BF16, [(0, bf16_width)]), (F32, [(bf16_width, f32_width)])])
    gl_col, kc_col, vc_col = 0, 1, 2
    z_col = 1
    kv_col0 = 2 * NSA_WIDTH
    qm_col = (kv_col0 + 4 * NSA_KV) // MEM_WIDTH
    zm_col = qm_col + 1
    n_mem = mem.shape[0] // batch
    mem_kv, = _norm_matmul(mem, mem_norm_g, w_mem_kv.astype(BF16), [(BF16, [(0, 2 * MEM_WIDTH)])])

    k_cmp, v_cmp_t = _compress(nf, kc_col, vc_col, pe_k, pe_v, phik_w1, phik_w2, phiv_w1, phiv_w2,
                               batch, seq)
    ym = _mem_attn(nb, qm_col, mem_kv, nb, zm_col, batch, seq, n_mem)
    return _nsa_attn(nb, kv_col0, z_col, k_cmp, v_cmp_t, nf, gl_col, ym, w_out.astype(BF16), x, final_g,
                     batch, seq)


def kernel(x, mem, hawk_norm, hawk_w_in, hawk_conv_w, hawk_conv_b, hawk_gate_a_w, hawk_gate_a_b,
           hawk_gate_x_w, hawk_gate_x_b, hawk_lambda, hawk_mem_norm, hawk_w_mem_kv, hawk_w_out,
           nsa_norm, nsa_w_in, nsa_pe_k, nsa_pe_v, nsa_phi_k_w1, nsa_phi_k_w2, nsa_phi_v_w1,
           nsa_phi_v_w2, nsa_mem_norm, nsa_w_mem_kv, nsa_w_out, final_norm):
    batch, seq, d = x.shape
    assert hawk_norm.shape[0] == 1 and nsa_norm.shape[0] == 1, "one layer of each kind"
    assert seq % (ATTN_BLOCK * DIL_GROUPS[-1][1]) == 0
    x2 = x.reshape(batch * seq, d)
    mem2 = mem.reshape(batch * mem.shape[1], d)
    x2 = _hawk_layer(x2, mem2, batch, seq, hawk_norm[0], hawk_w_in[0], hawk_conv_w[0], hawk_conv_b[0],
                     hawk_gate_a_w[0], hawk_gate_a_b[0].reshape(-1), hawk_gate_x_w[0],
                     hawk_gate_x_b[0].reshape(-1), hawk_lambda[0], hawk_mem_norm[0], hawk_w_mem_kv[0],
                     hawk_w_out[0])
    out = _nsa_layer(x2, mem2, batch, seq, nsa_norm[0], nsa_w_in[0], nsa_pe_k[0], nsa_pe_v[0],
                     nsa_phi_k_w1[0], nsa_phi_k_w2[0], nsa_phi_v_w1[0], nsa_phi_v_w2[0],
                     nsa_mem_norm[0], nsa_w_mem_kv[0], nsa_w_out[0], final_norm)
    return out.reshape(batch, seq, d)
```

```python
import functools

import numpy as np
import jax
import jax.numpy as jnp
from jax import lax
from jax.experimental import pallas as pl
from jax.experimental.pallas import tpu as pltpu

F32 = jnp.float32
BF16 = jnp.bfloat16

NORM_EPS = 1e-6
NEG_INF = -1e30
LOG2_E = 1.4426950408889634
LANES = 128
SUBLANES = 8
BF16_SUBLANES = 16
ATTN_BLOCK = 128
VMEM_LIMIT = 56 * 1024 * 1024

LRU_WIDTH = 1024
LRU_BLOCKS = 16
LRU_BLOCK_DIM = LRU_WIDTH // LRU_BLOCKS
LRU_PACK = 256
CONV_WIDTH = 4
LRU_C = 8.0

DIL_GROUPS = ((128, 1), (512, 4), (2048, 16))
DIL_HEADS = 4
DIL_HEAD_DIM = 128
DIL_WIDTH = DIL_HEADS * DIL_HEAD_DIM
DIL_QKV = len(DIL_GROUPS) * DIL_WIDTH
LSE_LANES = LANES // DIL_HEADS

MEM_HEADS = 4
MEM_HEAD_DIM = 64
MEM_WIDTH = MEM_HEADS * MEM_HEAD_DIM

NSA_HEADS = 16
NSA_KV_GROUPS = 2
NSA_R = NSA_HEADS // NSA_KV_GROUPS
NSA_HEAD_DIM = 64
NSA_WIDTH = NSA_HEADS * NSA_HEAD_DIM
NSA_KV = NSA_KV_GROUPS * NSA_HEAD_DIM
CMP_BLOCK = 32
CMP_STRIDE = 16
SLC_BLOCK = 64
SLC_TOP_N = 8
WIN_SIZE = 512
PHI_HIDDEN = 256
SEL_FORCE = 1e6
CMP_PER_SLC = SLC_BLOCK // CMP_STRIDE


def _alibi_slopes(n):
    return [float(v) for v in np.exp2(-8.0 * np.arange(1, n + 1) / n).astype(np.float32)]


def _params(*semantics):
    return pltpu.CompilerParams(dimension_semantics=semantics, vmem_limit_bytes=VMEM_LIMIT)


def _silu(z):
    return z * jax.nn.sigmoid(z)


def _dot_t(a, b):
    return lax.dot_general(a, b, (((1,), (1,)), ((), ())), preferred_element_type=F32)


def _rms_norm_rows(x, g):
    ms = jnp.mean(x * x, axis=-1, keepdims=True)
    return (x * lax.rsqrt(ms + NORM_EPS) * g).astype(BF16)


def _norm_matmul_kernel(*refs, dil, pieces):
    def project(xn, out_ref, w_refs, by_class=False):
        col = 0
        for w_ref in w_refs:
            width = w_ref.shape[1]
            res = jnp.dot(xn, w_ref[...], preferred_element_type=F32).astype(out_ref.dtype)
            if by_class:
                per = xn.shape[0] // dil
                for c in range(dil):
                    out_ref[c, :, col:col + width] = res[c * per:(c + 1) * per]
            else:
                out_ref[:, col:col + width] = res
            col += width

    if dil == 1:
        x_ref, g_ref = refs[:2]
        w_refs, o_refs = refs[2:2 + sum(pieces)], refs[2 + sum(pieces):]
        xn = _rms_norm_rows(x_ref[...], g_ref[...])
        first = 0
        for o_ref, n_pieces in zip(o_refs, pieces):
            project(xn, o_ref, w_refs[first:first + n_pieces])
            first += n_pieces
        return
    x_ref, g_ref, *w_refs, o_ref = refs
    tm, k = x_ref.shape
    x = x_ref[...]
    xn = x * lax.rsqrt(jnp.mean(x * x, axis=-1, keepdims=True) + NORM_EPS) * g_ref[...]
    xn = jnp.swapaxes(xn.reshape(tm // dil, dil, k), 0, 1).reshape(tm, k).astype(BF16)
    project(xn, o_ref, w_refs, by_class=True)


def _norm_matmul(x, g, w, outs, dil=1, tm=512):
    m, k = x.shape
    tm = min(tm, m)
    assert m % tm == 0 and k % LANES == 0
    pieces = [len(cols) for _, cols in outs]
    widths = [sum(width for _, width in cols) for _, cols in outs]
    w_specs = []
    for _, cols in outs:
        for col, width in cols:
            assert col % width == 0 and width % LANES == 0
            w_specs.append(pl.BlockSpec((k, width), functools.partial(lambda i, j: (0, j), j=col // width)))
    resident = [pl.BlockSpec((1, k), lambda i: (0, 0))] + w_specs
    operands = (g.reshape(1, k),) + (w,) * len(w_specs)
    kernel = functools.partial(_norm_matmul_kernel, dil=dil, pieces=pieces)
    if dil == 1:
        return pl.pallas_call(
            kernel,
            grid=(m // tm,),
            in_specs=[pl.BlockSpec((tm, k), lambda i: (i, 0))] + resident,
            out_specs=[pl.BlockSpec((tm, width), lambda i: (i, 0)) for width in widths],
            out_shape=[jax.ShapeDtypeStruct((m, width), dtype) for width, (dtype, _) in zip(widths, outs)],
            compiler_params=_params("arbitrary"),
            name="norm_matmul",
        )(x, *operands)
    per = tm // dil
    (out_dtype, _), = outs
    n, = widths
    assert tm % dil == 0 and per % BF16_SUBLANES == 0
    return pl.pallas_call(
        kernel,
        grid=(m // tm,),
        in_specs=[pl.BlockSpec((tm, k), lambda i: (i, 0))] + resident,
        out_specs=pl.BlockSpec((dil, per, n), lambda i: (0, i, 0)),
        out_shape=jax.ShapeDtypeStruct((dil, m // dil, n), out_dtype),
        compiler_params=_params("arbitrary"),
        name="norm_matmul_dil",
    )(x, *operands)


def _rglru_kernel(xa_ref, za_ref, cw_ref, cb_ref, wa_ref, ba_ref, wx_ref, bx_ref, lam_ref,
                  o_ref, xpad_ref, h_ref):
    t = pl.program_id(1)
    tt, width = xa_ref.shape
    halo = SUBLANES

    @pl.when(t == 0)
    def _():
        xpad_ref[0:halo, :] = jnp.zeros((halo, width), F32)
        h_ref[...] = jnp.zeros_like(h_ref)

    x = xa_ref[...]
    xpad_ref[halo:halo + tt, :] = x
    cw = cw_ref[...]
    y = cw[CONV_WIDTH - 1:CONV_WIDTH] * x
    for k in range(1, CONV_WIDTH):
        y = y + cw[CONV_WIDTH - 1 - k:CONV_WIDTH - k] * xpad_ref[halo - k:halo - k + tt, :]
    y = y + cb_ref[...]
    xpad_ref[0:halo, :] = x[tt - halo:tt, :]

    yb = y.astype(BF16)
    r_parts, i_parts = [], []
    for p in range(width // LRU_PACK):
        ys = yb[:, p * LRU_PACK:(p + 1) * LRU_PACK]
        r_parts.append(jnp.dot(ys, wa_ref[p], preferred_element_type=F32))
        i_parts.append(jnp.dot(ys, wx_ref[p], preferred_element_type=F32))
    r = jax.nn.sigmoid(jnp.concatenate(r_parts, axis=1) + ba_ref[...])
    gi = jax.nn.sigmoid(jnp.concatenate(i_parts, axis=1) + bx_ref[...])

    nl = -lam_ref[...]
    softplus = jnp.maximum(nl, 0.0) + jnp.log1p(jnp.exp(-jnp.abs(nl)))
    log_a = (-LRU_C) * r * softplus
    a = jnp.exp(log_a)
    w = -jnp.tanh(log_a) * (a * a + 1.0)
    mult = jnp.where(w > 0.0, w * lax.rsqrt(w), 0.0)
    b = y * gi * mult
    first = (lax.broadcasted_iota(jnp.int32, (SUBLANES, width), 0) == 0) & (t == 0)
    b = jnp.concatenate([jnp.where(first, (y * gi)[0:SUBLANES], b[0:SUBLANES]), b[SUBLANES:]], axis=0)

    groups = tt // SUBLANES
    a3 = a.reshape(groups, SUBLANES, width)
    b3 = b.reshape(groups, SUBLANES, width)
    sub = lax.broadcasted_iota(jnp.int32, (1, SUBLANES, width), 1)
    k = 1
    while k < SUBLANES:
        keep = sub >= k
        a_sh = jnp.where(keep, pltpu.roll(a3, k, axis=1), 1.0)
        b_sh = jnp.where(keep, pltpu.roll(b3, k, axis=1), 0.0)
        b3 = a3 * b_sh + b3
        a3 = a3 * a_sh
        k *= 2
    carry = jnp.broadcast_to(h_ref[...], (SUBLANES, width))
    hs = []
    for gidx in range(groups):
        hg = a3[gidx] * carry + b3[gidx]
        hs.append(hg)
        carry = jnp.broadcast_to(hg[SUBLANES - 1:SUBLANES], (SUBLANES, width))
    h = jnp.concatenate(hs, axis=0)
    h_ref[...] = carry[0:1]
    o_ref[...] = (h * _silu(za_ref[...].astype(F32))).astype(o_ref.dtype)


def _rglru(xa, za_src, za_col, conv_w, conv_b, wa, ba, wx, bx, lam, batch, seq, tt=1024):
    width = LRU_WIDTH
    nt = seq // tt
    packs = width // LRU_PACK
    vec = pl.BlockSpec((1, width), lambda b, t: (0, 0))
    gate_w = pl.BlockSpec((packs, LRU_PACK, LRU_PACK), lambda b, t: (0, 0, 0))
    return pl.pallas_call(
        _rglru_kernel,
        grid=(batch, nt),
        in_specs=[pl.BlockSpec((tt, width), lambda b, t: (b * nt + t, 0)),
                  pl.BlockSpec((tt, width), lambda b, t: (b * nt + t, za_col)),
                  pl.BlockSpec((CONV_WIDTH, width), lambda b, t: (0, 0)),
                  vec, gate_w, vec, gate_w, vec, vec],
        out_specs=pl.BlockSpec((tt, width), lambda b, t: (b * nt + t, 0)),
        out_shape=jax.ShapeDtypeStruct((batch * seq, width), BF16),
        scratch_shapes=[pltpu.VMEM((tt + SUBLANES, width), F32), pltpu.VMEM((1, width), F32)],
        compiler_params=_params("arbitrary", "arbitrary"),
        name="rglru",
    )(xa, za_src, conv_w, conv_b.reshape(1, width), wa, ba.reshape(1, width), wx, bx.reshape(1, width),
      lam.reshape(1, width))


def _pack_block_diag(w):
    per = LRU_PACK // LRU_BLOCK_DIM
    w = w.reshape(LRU_BLOCKS // per, per, LRU_BLOCK_DIM, LRU_BLOCK_DIM)
    eye = jnp.eye(per, dtype=w.dtype)
    packed = w[:, :, :, None, :] * eye[None, :, None, :, None]
    return packed.reshape(LRU_BLOCKS // per, LRU_PACK, LRU_PACK).astype(BF16)


def _dil_attn_kernel(*refs, slopes, pos_scale, max_dist, has_halo, dil, n_cls, n_blk):
    if has_halo:
        q_ref, kh_ref, k_ref, vh_ref, v_ref = refs[:5]
        out_refs = refs[5:]
    else:
        q_ref, k_ref, v_ref = refs[:3]
        out_refs = refs[3:]
    n_out = DIL_HEADS + 1
    dst_refs = out_refs[:n_out]
    stage_refs = out_refs[n_out:] if dil > 1 else dst_refs
    first_super = pl.program_id(1) == 0
    cls0 = pl.program_id(2) * n_cls
    blk = ATTN_BLOCK
    scale = DIL_HEAD_DIM ** -0.5

    def band(width, halo_live):
        row = lax.broadcasted_iota(jnp.int32, (blk, width), 0)
        col = lax.broadcasted_iota(jnp.int32, (blk, width), 1)
        dist = (width - blk) + row - col
        valid = (dist >= 0) & (dist <= max_dist)
        if halo_live is not None:
            valid = valid & ((col >= blk) | halo_live)
        distf = (dist * pos_scale).astype(F32)
        return [jnp.where(valid, (-slope / scale) * distf, NEG_INF) for slope in slopes]

    bias_inner = band(2 * blk, None) if n_blk > 1 else None
    bias_first = band(2 * blk, jnp.logical_not(first_super)) if has_halo else band(blk, None)

    def scores(cc, jb):
        cur = slice(jb * blk, (jb + 1) * blk)
        bias = bias_inner if jb > 0 else bias_first
        out = []
        for h in range(DIL_HEADS):
            hs = slice(h * DIL_HEAD_DIM, (h + 1) * DIL_HEAD_DIM)
            q = q_ref[cc, cur, hs]
            if jb > 0:
                k = k_ref[cc, (jb - 1) * blk:(jb + 1) * blk, hs]
                v = v_ref[cc, (jb - 1) * blk:(jb + 1) * blk, hs]
            elif has_halo:
                k = jnp.concatenate([kh_ref[cc, :, hs], k_ref[cc, cur, hs]], axis=0)
                v = jnp.concatenate([vh_ref[cc, :, hs], v_ref[cc, cur, hs]], axis=0)
            else:
                k, v = k_ref[cc, cur, hs], v_ref[cc, cur, hs]
            out.append((_dot_t(q, k) + bias[h], v))
        return out

    def finish(cc, jb, pairs):
        where = (cls0 + cc, slice(jb * blk, (jb + 1) * blk)) if dil > 1 else (slice(jb * blk, (jb + 1) * blk),)
        lses = []
        for h, (s, v) in enumerate(pairs):
            m = jnp.max(s, axis=-1, keepdims=True)
            e = jnp.exp2((s - m) * (scale * LOG2_E))
            den = jnp.sum(e, axis=-1, keepdims=True)
            o = jnp.dot(e.astype(BF16), v, preferred_element_type=F32) / den
            stage_refs[h][where] = o.astype(stage_refs[h].dtype)
            lses.append(jnp.broadcast_to(m * scale + jnp.log(den), (blk, LSE_LANES)))
        stage_refs[DIL_HEADS][where] = jnp.concatenate(lses, axis=1)

    pending = None
    for cc in range(n_cls):
        for jb in range(n_blk):
            pairs = scores(cc, jb)
            if pending is not None:
                finish(*pending)
            pending = (cc, jb, pairs)
    finish(*pending)

    if dil > 1:
        @pl.when(pl.program_id(2) == pl.num_programs(2) - 1)
        def _():
            for stage, dst in zip(stage_refs, dst_refs):
                dst[...] = jnp.swapaxes(stage[...], 0, 1).reshape(dst.shape).astype(dst.dtype)


def _dil_attn(qkv, col0, gi, batch, seq, work=8):
    window, dil = DIL_GROUPS[gi]
    sub = seq // dil
    nb = sub // ATTN_BLOCK
    n_blk = min(work, nb)
    n_cls = min(work // n_blk, dil)
    n_super = nb // n_blk
    has_halo = n_super > 1
    span = n_blk * ATTN_BLOCK
    slopes = _alibi_slopes(len(DIL_GROUPS) * DIL_HEADS)[gi * DIL_HEADS:(gi + 1) * DIL_HEADS]
    cur = lambda col: pl.BlockSpec((n_cls, span, DIL_WIDTH), lambda b, i, c: (c, b * n_super + i, col0 + col))
    halo = lambda col: pl.BlockSpec(
        (n_cls, ATTN_BLOCK, DIL_WIDTH),
        lambda b, i, c: (c, jnp.maximum((b * n_super + i) * n_blk - 1, 0), col0 + col))
    if has_halo:
        in_specs = [cur(0), halo(1), cur(1), halo(2), cur(2)]
    else:
        in_specs = [cur(0), cur(1), cur(2)]
    n_out = DIL_HEADS + 1
    *o, lse = pl.pallas_call(
        functools.partial(_dil_attn_kernel, slopes=slopes, pos_scale=dil, max_dist=window // dil,
                          has_halo=has_halo, dil=dil, n_cls=n_cls, n_blk=n_blk),
        grid=(batch, n_super, dil // n_cls),
        in_specs=in_specs,
        out_specs=[pl.BlockSpec((span * dil, LANES), lambda b, i, c: (b * n_super + i, 0))] * n_out,
        out_shape=[jax.ShapeDtypeStruct((batch * seq, LANES), BF16)] * DIL_HEADS
                  + [jax.ShapeDtypeStruct((batch * seq, LANES), F32)],
        scratch_shapes=[pltpu.VMEM((dil, span, LANES), F32)] * (n_out if dil > 1 else 0),
        compiler_params=_params("arbitrary", "arbitrary", "arbitrary"),
        name=f"dil_attn_d{dil}",
    )(*([qkv] * len(in_specs)))
    return o, lse


def _mem_attn_kernel(q_ref, k_ref, v_ref, z_ref, o_ref, ks_ref, vt_ref):
    hd = MEM_HEAD_DIM
    n_mem = k_ref.shape[0]

    @pl.when(pl.program_id(1) == 0)
    def _():
        ks_ref[...] = (k_ref[...].astype(F32) * (hd ** -0.5)).astype(BF16)
        vt = v_ref[...].astype(F32).T.astype(BF16)
        for h in range(MEM_HEADS):
            vt_ref[h, 0:hd, :] = vt[h * hd:(h + 1) * hd, :]
            vt_ref[h, hd:, :] = jnp.ones((vt_ref.shape[1] - hd, n_mem), BF16)

    heads = [slice(h * hd, (h + 1) * hd) for h in range(MEM_HEADS)]
    scores = [_dot_t(ks_ref[:, hs], q_ref[:, hs]) for hs in heads]
    outs = []
    for h, s in enumerate(scores):
        e = jnp.exp(s - jnp.max(s, axis=0, keepdims=True)).astype(BF16)
        acc = jnp.dot(vt_ref[h], e, preferred_element_type=F32)
        outs.append(acc[0:hd, :] / acc[hd:hd + 1, :])
    o = jnp.concatenate(outs, axis=0).T
    o_ref[...] = (o * _silu(z_ref[...].astype(F32))).astype(o_ref.dtype)


def _mem_attn(qsrc, q_col, kv, zsrc, z_col, batch, seq, n_mem, tq=1024):
    nq = seq // tq
    return pl.pallas_call(
        _mem_attn_kernel,
        grid=(batch, nq),
        in_specs=[pl.BlockSpec((tq, MEM_WIDTH), lambda b, i: (b * nq + i, q_col)),
                  pl.BlockSpec((n_mem, MEM_WIDTH), lambda b, i: (b, 0)),
                  pl.BlockSpec((n_mem, MEM_WIDTH), lambda b, i: (b, 1)),
                  pl.BlockSpec((tq, MEM_WIDTH), lambda b, i: (b * nq + i, z_col))],
        out_specs=pl.BlockSpec((tq, MEM_WIDTH), lambda b, i: (b * nq + i, 0)),
        out_shape=jax.ShapeDtypeStruct((batch * seq, MEM_WIDTH), BF16),
        scratch_shapes=[pltpu.VMEM((n_mem, MEM_WIDTH), BF16),
                        pltpu.VMEM((MEM_HEADS, MEM_HEAD_DIM + BF16_SUBLANES, n_mem), BF16)],
        compiler_params=_params("arbitrary", "arbitrary"),
        name="mem_attn",
    )(qsrc, kv, kv, zsrc)


def _hawk_out_kernel(*refs):
    n_groups = len(DIL_GROUPS)
    ya_ref = refs[0]
    o_refs = refs[1:1 + n_groups * DIL_HEADS]
    l_refs = refs[1 + n_groups * DIL_HEADS:1 + n_groups * (DIL_HEADS + 1)]
    zb_ref, ym_ref, w_ref, x_ref, out_ref = refs[1 + n_groups * (DIL_HEADS + 1):]
    a_end = LRU_WIDTH
    b_end = a_end + DIL_WIDTH
    y = jnp.dot(ya_ref[...], w_ref[0:a_end, :], preferred_element_type=F32)
    y = y + jnp.dot(ym_ref[...], w_ref[b_end:b_end + MEM_WIDTH, :], preferred_element_type=F32)
    parts = []
    for h in range(DIL_HEADS):
        ls = [l[:, h * LSE_LANES:h * LSE_LANES + 1] for l in l_refs]
        m = functools.reduce(jnp.maximum, ls)
        ws = [jnp.exp(l - m) for l in ls]
        num = sum(w * o_refs[gi * DIL_HEADS + h][...].astype(F32) for gi, w in enumerate(ws))
        parts.append(num / sum(ws))
    yb = (jnp.concatenate(parts, axis=1) * _silu(zb_ref[...].astype(F32))).astype(BF16)
    y = y + jnp.dot(yb, w_ref[a_end:b_end, :], preferred_element_type=F32)
    out_ref[...] = x_ref[...] + y


def _hawk_out(ya, os_, ls_, zb_src, zb_col, ym, w, x, tm=1024):
    m, d = x.shape
    row = lambda width, col=0: pl.BlockSpec((tm, width), lambda i: (i, col))
    heads = [o for group in os_ for o in group]
    return pl.pallas_call(
        _hawk_out_kernel,
        grid=(m // tm,),
        in_specs=[row(LRU_WIDTH)] + [row(DIL_HEAD_DIM)] * len(heads) + [row(LANES)] * len(ls_)
                 + [row(DIL_WIDTH, zb_col), row(MEM_WIDTH),
                    pl.BlockSpec(w.shape, lambda i: (0, 0)), row(d)],
        out_specs=row(d),
        out_shape=jax.ShapeDtypeStruct((m, d), F32),
        compiler_params=_params("arbitrary"),
        name="hawk_out",
    )(ya, *heads, *ls_, zb_src, ym, w, x)


def _compress_kernel(k_ref, v_ref, pe_ref, w1_ref, w2k_ref, w2vt_ref, ko_ref, vto_ref):
    n_blk = k_ref.shape[0] // CMP_STRIDE

    def hidden(which, src_ref):
        x = jnp.concatenate([src_ref[pl.ds(p, n_blk, stride=CMP_STRIDE), :] for p in range(CMP_STRIDE)],
                            axis=1).astype(BF16)
        first = jnp.dot(x, w1_ref[which, 0], preferred_element_type=F32)
        second = jnp.dot(x, w1_ref[which, 1], preferred_element_type=F32)
        pe = (jnp.dot(pe_ref[which, 0], w1_ref[which, 0], preferred_element_type=F32)
              + jnp.dot(pe_ref[which, 1], w1_ref[which, 1], preferred_element_type=F32))
        return _silu(first + pltpu.roll(second, n_blk - 1, axis=0) + pe[0:1, :]).astype(BF16)

    act_k, act_v = hidden(0, k_ref), hidden(1, v_ref)
    part = lambda a, g: a[:, g * PHI_HIDDEN:(g + 1) * PHI_HIDDEN]
    ks = [jnp.dot(part(act_k, g), w2k_ref[...], preferred_element_type=F32) for g in range(NSA_KV_GROUPS)]
    vts = [_dot_t(w2vt_ref[...], part(act_v, g)) for g in range(NSA_KV_GROUPS)]
    ko_ref[...] = jnp.concatenate(ks, axis=1).astype(ko_ref.dtype)
    vto_ref[...] = jnp.concatenate(vts, axis=0).astype(vto_ref.dtype)


def _compress(src, k_col, v_col, pe_k, pe_v, k_w1, k_w2, v_w1, v_w2, batch, seq):
    half = CMP_BLOCK // 2
    assert half == CMP_STRIDE and NSA_KV == LANES
    n_blk = seq // CMP_STRIDE
    hd = NSA_HEAD_DIM
    w1 = jnp.stack([k_w1, v_w1]).reshape(2, 2, half, hd, PHI_HIDDEN).astype(BF16)
    zero = jnp.zeros_like(w1)
    per_group = [jnp.concatenate([w1 if g == col else zero for col in range(NSA_KV_GROUPS)], axis=-1)
                 for g in range(NSA_KV_GROUPS)]
    w1e = jnp.stack(per_group, axis=3).reshape(2, 2, half * NSA_KV, NSA_KV_GROUPS * PHI_HIDDEN)
    pe = jnp.stack([pe_k, pe_v]).reshape(2, 2, half, 1, hd)
    pe = jnp.broadcast_to(pe, (2, 2, half, NSA_KV_GROUPS, hd)).reshape(2, 2, 1, half * NSA_KV)
    pe = jnp.broadcast_to(pe, (2, 2, SUBLANES, half * NSA_KV)).astype(BF16)
    w2k = k_w2.astype(BF16)
    w2vt = v_w2.T.astype(BF16)
    whole = lambda a: pl.BlockSpec(a.shape, lambda b: (0,) * a.ndim)
    return pl.pallas_call(
        _compress_kernel,
        grid=(batch,),
        in_specs=[pl.BlockSpec((seq, LANES), lambda b: (b, k_col)),
                  pl.BlockSpec((seq, LANES), lambda b: (b, v_col)),
                  whole(pe), whole(w1e), whole(w2k), whole(w2vt)],
        out_specs=[pl.BlockSpec((None, n_blk, NSA_KV), lambda b: (b, 0, 0)),
                   pl.BlockSpec((None, NSA_KV, n_blk), lambda b: (b, 0, 0))],
        out_shape=[jax.ShapeDtypeStruct((batch, n_blk, NSA_KV), BF16),
                   jax.ShapeDtypeStruct((batch, NSA_KV, n_blk), BF16)],
        compiler_params=_params("arbitrary"),
        name="compress",
    )(src, src, pe, w1e, w2k, w2vt)


KEY_CHUNK = 256
NSA_TQ = 256
SLOPE_PIECES = 3
N_FEATS = 2 * SLOPE_PIECES
FEAT_LANES = 32
KEY_COLS = NSA_HEAD_DIM + 2 * FEAT_LANES
N_SLC = 32


def _slope_pieces(slope):
    rest = np.float32(slope)
    pieces = []
    for _ in range(SLOPE_PIECES):
        p = np.float32(np.asarray(rest).astype(BF16))
        pieces.append(float(p))
        rest = np.float32(rest - p)
    return pieces


def _lane_table(lane, values):
    out = jnp.zeros(lane.shape, F32)
    for idx, v in enumerate(values):
        out = jnp.where(lane == idx, v, out)
    return out


def _key_feats(pos_hi, pos_lo, lane):
    return jnp.where(lane < SLOPE_PIECES, pos_hi, jnp.where(lane < N_FEATS, pos_lo, 0)).astype(F32)


def _tile_heads(x):
    return jnp.concatenate([x] * NSA_R, axis=1)


def _chunk_loop(lo, hi, body, init, widths=(4, 2, 1)):
    carry, start = init, lo
    for idx, w in enumerate(widths):
        count = (hi - start) // w

        def step(p, cr, start=start, w=w):
            first = start + p * w
            return body([first + j for j in range(w)], cr)

        if idx == 0:
            carry = lax.fori_loop(0, count, step, carry)
        else:
            carry = lax.cond(count > 0, functools.partial(step, 0), lambda cr: cr, carry)
        start = start + count * w
    return carry


def _nsa_kernel(q_ref, kc_ref, vct_ref, ksrc_ref, vsrc_ref, kwsrc_ref, vwsrc_ref, feat_ref, hot_ref,
                gl_ref, z_ref, ym_ref, w_ref, x_ref, fin_ref, o_ref,
                ks_ref, vst_ref, kw_ref, vwt_ref, s_ref, acc_ref, imp_ref):
    i = pl.program_id(1)
    tq = q_ref.shape[0]
    hd = NSA_HEAD_DIM
    n_cmp = kc_ref.shape[0]

    @pl.when(i == 0)
    def _():
        for g in range(NSA_KV_GROUPS):
            gs = slice(g * hd, (g + 1) * hd)
            for dst, src, tail in ((ks_ref, ksrc_ref, hot_ref[...]),
                                   (kw_ref, kwsrc_ref, jnp.zeros(hot_ref.shape, BF16))):
                dst[:, g * KEY_COLS:g * KEY_COLS + hd] = src[:, gs]
                dst[:, g * KEY_COLS + hd:g * KEY_COLS + hd + FEAT_LANES] = feat_ref[...]
                dst[:, g * KEY_COLS + hd + FEAT_LANES:(g + 1) * KEY_COLS] = tail
        for c in range(vst_ref.shape[0]):
            rows = slice(c * KEY_CHUNK, (c + 1) * KEY_CHUNK)
            vst_ref[c] = vsrc_ref[rows, :].astype(F32).T.astype(BF16)
            vwt_ref[c] = vwsrc_ref[rows, :].astype(F32).T.astype(BF16)

    slopes_all = _alibi_slopes(NSA_HEADS)
    gates_t = jax.nn.sigmoid(gl_ref[...]).T
    feat_lane = lax.broadcasted_iota(jnp.int32, (tq, FEAT_LANES), 1)
    no_sel = jnp.zeros((NSA_R * tq, FEAT_LANES), BF16)
    key_row = lax.broadcasted_iota(jnp.int32, (KEY_CHUNK, tq), 0)
    t_pos = i * tq + lax.broadcasted_iota(jnp.int32, (KEY_CHUNK, tq), 1)
    ones_rows = jnp.ones((BF16_SUBLANES, KEY_CHUNK), BF16)
    win_lo = jnp.maximum(i * tq - (WIN_SIZE - 1), 0) // KEY_CHUNK
    chunks_hi = (i * tq + tq - 1) // KEY_CHUNK + 1

    groups = range(NSA_KV_GROUPS)
    q_win, q_slc, o_cmp = [], [], []
    for g in groups:
        slopes = slopes_all[g * NSA_R:(g + 1) * NSA_R]
        gs = slice(g * hd, (g + 1) * hd)
        q_parts = []
        for r in range(NSA_R):
            qr = q_ref[:, (g * NSA_R + r) * hd:(g * NSA_R + r + 1) * hd]
            feats = _lane_table(feat_lane, _slope_pieces(slopes[r]) * 2).astype(BF16)
            q_parts.append(jnp.concatenate([qr, feats], axis=1))
        q_feat = jnp.concatenate(q_parts, axis=0)
        q_aug = jnp.concatenate([q_feat, no_sel], axis=1)

        n_row = lax.broadcasted_iota(jnp.int32, (n_cmp, tq), 0)
        t_cmp = i * tq + lax.broadcasted_iota(jnp.int32, (n_cmp, tq), 1)
        visible = t_cmp >= n_row * CMP_STRIDE + (CMP_BLOCK - 1)
        cfeat_row = lax.broadcasted_iota(jnp.int32, (n_cmp, 2 * FEAT_LANES), 0)
        cfeat_lane = lax.broadcasted_iota(jnp.int32, (n_cmp, 2 * FEAT_LANES), 1)
        kc_feats = _key_feats(cfeat_row * CMP_STRIDE, 0, cfeat_lane)
        kc_aug = jnp.concatenate([kc_ref[:, gs], kc_feats.astype(BF16)], axis=1)
        s = _dot_t(kc_aug, q_aug) + _tile_heads(jnp.where(visible, 0.0, NEG_INF))
        m = jnp.max(s, axis=0, keepdims=True)
        e = jnp.exp(s - m)
        t_one = i * tq + lax.broadcasted_iota(jnp.int32, (1, NSA_R * tq), 1) % tq
        any_visible = t_one >= (CMP_BLOCK - 1)
        p = e * jnp.where(any_visible, 1.0 / jnp.sum(e, axis=0, keepdims=True), 0.0)
        o_cmp.append(jnp.dot(vct_ref[gs, :], p.astype(BF16), preferred_element_type=F32))
        p_sum = p[:, 0:tq]
        for r in range(1, NSA_R):
            p_sum = p_sum + p[:, r * tq:(r + 1) * tq]

        band = p_sum + pltpu.roll(p_sum, 1, axis=0)
        for k in range(1, CMP_PER_SLC):
            band = band + pltpu.roll(p_sum, n_cmp - k, axis=0)
        halves = []
        for half in range(tq // LANES):
            imp_ref[...] = band[:, half * LANES:(half + 1) * LANES]
            halves.append(imp_ref[pl.ds(0, N_SLC, stride=CMP_PER_SLC), :])
        imp = jnp.concatenate(halves, axis=1)
        blk_j = lax.broadcasted_iota(jnp.int32, (N_SLC, tq), 0)
        cur = (i * tq + lax.broadcasted_iota(jnp.int32, (N_SLC, tq), 1)) // SLC_BLOCK
        forced = (blk_j == 0) | (blk_j == cur) | (blk_j == cur - 1)
        v_imp = jnp.where(forced, SEL_FORCE, jnp.where(blk_j > cur, -SEL_FORCE, imp))
        rank = jnp.zeros((N_SLC, tq), F32)
        for other in range(N_SLC):
            row = v_imp[other:other + 1, :]
            ahead = (row > v_imp) | ((row == v_imp) & (blk_j > other))
            rank = rank + jnp.where(ahead, 1.0, 0.0)
        sel_bias = jnp.where(rank < SLC_TOP_N, 0.0, NEG_INF)

        padded = jnp.concatenate([sel_bias, jnp.zeros((LANES - N_SLC, tq), F32)], axis=0)
        sel_t = padded.T[:, 0:FEAT_LANES].astype(BF16)
        q_win.append(q_aug)
        q_slc.append(jnp.concatenate([q_feat, jnp.concatenate([sel_t] * NSA_R, axis=0)], axis=1))

    def attend(q_brs, k_ref, key_cols, vt_ref, lo, hi, masked_from, mask_fn):
        def scores(cs, m_run, masked):
            starts = [pl.multiple_of(c * KEY_CHUNK, KEY_CHUNK) for c in cs]
            scs = [[_dot_t(k_ref[pl.ds(start, KEY_CHUNK), g * key_cols:(g + 1) * key_cols], q_brs[g])
                    for g in groups] for start in starts]
            for c, start, sc in zip(cs, starts, scs):
                if masked:
                    bias = _tile_heads(jnp.where(mask_fn(t_pos - (start + key_row)), 0.0, NEG_INF))
                    sc = [x + bias for x in sc]
                for g in groups:
                    s_ref[g, c] = sc[g]
                m_run = tuple(jnp.maximum(m_run[g], jnp.max(sc[g], axis=0, keepdims=True)) for g in groups)
            return m_run

        def weighted(cs, carry):
            for c in cs:
                for g in groups:
                    e = jnp.exp(s_ref[g, c] - m_rows[g]).astype(BF16)
                    v_ext = jnp.concatenate([vt_ref[c, g * hd:(g + 1) * hd, :], ones_rows], axis=0)
                    acc_ref[g] += jnp.dot(v_ext, e, preferred_element_type=F32)
            return carry

        m_rows = tuple(jnp.full((1, NSA_R * tq), NEG_INF, F32) for _ in groups)
        m_rows = _chunk_loop(lo, masked_from, functools.partial(scores, masked=False), m_rows)
        m_rows = _chunk_loop(masked_from, hi, functools.partial(scores, masked=True), m_rows)
        acc_ref[...] = jnp.zeros(acc_ref.shape, F32)
        _chunk_loop(lo, hi, weighted, 0)
        return [acc_ref[g, 0:hd, :] / acc_ref[g, hd:hd + 1, :] for g in groups]

    o_slc = attend(q_slc, ks_ref, KEY_COLS, vst_ref, 0, chunks_hi, (i * tq) // KEY_CHUNK,
                   lambda dist: dist >= 0)
    o_win = attend(q_win, kw_ref, KEY_COLS, vwt_ref, win_lo, chunks_hi, win_lo,
                   lambda dist: (dist >= 0) & (dist <= WIN_SIZE - 1))

    y = jnp.dot(ym_ref[...], w_ref[NSA_WIDTH:NSA_WIDTH + MEM_WIDTH, :], preferred_element_type=F32)
    for g in groups:
        def gate(kind):
            base = g * NSA_R * 3 + kind
            return jnp.concatenate([gates_t[base + 3 * r:base + 3 * r + 1, :] for r in range(NSA_R)], axis=1)

        o = gate(0) * o_cmp[g] + gate(1) * o_slc[g] + gate(2) * o_win[g]
        pairs = []
        for r in range(0, NSA_R, 2):
            two = jnp.concatenate([o[:, r * tq:(r + 1) * tq], o[:, (r + 1) * tq:(r + 2) * tq]], axis=0)
            pairs.append(two.T)
        cs = slice(g * NSA_R * hd, (g + 1) * NSA_R * hd)
        yo = (jnp.concatenate(pairs, axis=1) * _silu(z_ref[:, cs].astype(F32))).astype(BF16)
        y = y + jnp.dot(yo, w_ref[cs, :], preferred_element_type=F32)

    x = x_ref[...] + y
    ms = jnp.mean(x * x, axis=-1, keepdims=True)
    o_ref[...] = x * lax.rsqrt(ms + NORM_EPS) * fin_ref[...]


def _key_pos_feats(seq):
    assert seq // SLC_BLOCK == N_SLC <= FEAT_LANES
    pos = np.arange(seq)
    feats = np.zeros((seq, FEAT_LANES), np.float32)
    feats[:, 0:SLOPE_PIECES] = ((pos // SLC_BLOCK) * SLC_BLOCK)[:, None]
    feats[:, SLOPE_PIECES:N_FEATS] = (pos % SLC_BLOCK)[:, None]
    onehot = (np.arange(FEAT_LANES)[None, :] == (pos // SLC_BLOCK)[:, None]).astype(np.float32)
    return jnp.asarray(feats, BF16), jnp.asarray(onehot, BF16)


def _nsa_attn(nb, kv_col0, z_col, k_cmp, v_cmp_t, nf, gl_col, ym, w_out, x, final_g, batch, seq):
    tq = NSA_TQ
    nq = seq // tq
    d = x.shape[1]
    feats, onehot = _key_pos_feats(seq)
    kv_blk = kv_col0 // NSA_KV
    seq_cols = lambda col: pl.BlockSpec((seq, NSA_KV), lambda b, i: (b, kv_blk + col))
    const = lambda a: pl.BlockSpec(a.shape, lambda b, i: (0,) * a.ndim)
    per_batch = lambda a: pl.BlockSpec((None,) + a.shape[1:], lambda b, i: (b,) + (0,) * (a.ndim - 1))
    rows = lambda width, col=0: pl.BlockSpec((tq, width), lambda b, i: (b * nq + i, col))
    fin = final_g.reshape(1, d)
    return pl.pallas_call(
        _nsa_kernel,
        grid=(batch, nq),
        in_specs=[rows(NSA_WIDTH), per_batch(k_cmp), per_batch(v_cmp_t),
                  seq_cols(0), seq_cols(1), seq_cols(2), seq_cols(3), const(feats), const(onehot),
                  rows(LANES, gl_col), rows(NSA_WIDTH, z_col), rows(MEM_WIDTH), const(w_out), rows(d),
                  const(fin)],
        out_specs=rows(d),
        out_shape=jax.ShapeDtypeStruct((batch * seq, d), F32),
        scratch_shapes=[pltpu.VMEM((seq, NSA_KV_GROUPS * KEY_COLS), BF16),
                        pltpu.VMEM((seq // KEY_CHUNK, NSA_KV, KEY_CHUNK), BF16),
                        pltpu.VMEM((seq, NSA_KV_GROUPS * KEY_COLS), BF16),
                        pltpu.VMEM((seq // KEY_CHUNK, NSA_KV, KEY_CHUNK), BF16),
                        pltpu.VMEM((NSA_KV_GROUPS, seq // KEY_CHUNK, KEY_CHUNK, NSA_R * tq), F32),
                        pltpu.VMEM((NSA_KV_GROUPS, NSA_HEAD_DIM + BF16_SUBLANES, NSA_R * tq), F32),
                        pltpu.VMEM((seq // CMP_STRIDE, LANES), F32)],
        compiler_params=_params("arbitrary", "arbitrary"),
        name="nsa_attn",
    )(nb, k_cmp, v_cmp_t, nb, nb, nb, nb, feats, onehot, nf, nb, ym, w_out, x, fin)


def _hawk_layer(x, mem, batch, seq, norm_g, w_in, conv_w, conv_b, ga_w, ga_b, gx_w, gx_b, lam,
                mem_norm_g, w_mem_kv, w_out):
    xa0, za0 = 0, LRU_WIDTH
    q0 = 2 * LRU_WIDTH
    k0, v0 = q0 + DIL_QKV, q0 + 2 * DIL_QKV
    zb0 = q0 + 3 * DIL_QKV
    qm0 = zb0 + DIL_WIDTH
    zm0 = qm0 + MEM_WIDTH

    def qkv_cols(gi):
        return [(base + gi * DIL_WIDTH, DIL_WIDTH) for base in (q0, k0, v0)]

    w = w_in.astype(BF16)
    nat_cols = [(za0, LRU_WIDTH), *qkv_cols(0), (zb0, DIL_WIDTH), (qm0, MEM_WIDTH), (zm0, MEM_WIDTH)]
    xa, hb = _norm_matmul(x, norm_g, w, [(F32, [(xa0, LRU_WIDTH)]), (BF16, nat_cols)])
    za_col = 0
    qkv0_col = LRU_WIDTH // DIL_WIDTH
    zb_col = (LRU_WIDTH + 3 * DIL_WIDTH) // DIL_WIDTH
    qm_col = (LRU_WIDTH + 4 * DIL_WIDTH) // MEM_WIDTH
    zm_col = qm_col + 1
    qkv = [(hb[None], qkv0_col)]
    for gi in range(1, len(DIL_GROUPS)):
        qkv.append((_norm_matmul(x, norm_g, w, [(BF16, qkv_cols(gi))], dil=DIL_GROUPS[gi][1]), 0))
    n_mem = mem.shape[0] // batch
    mem_kv, = _norm_matmul(mem, mem_norm_g, w_mem_kv.astype(BF16), [(BF16, [(0, 2 * MEM_WIDTH)])])

    ya = _rglru(xa, hb, za_col, conv_w, conv_b, _pack_block_diag(ga_w), ga_b, _pack_block_diag(gx_w), gx_b,
                lam, batch, seq)
    attn = [_dil_attn(arr, col0, gi, batch, seq) for gi, (arr, col0) in enumerate(qkv)]
    ym = _mem_attn(hb, qm_col, mem_kv, hb, zm_col, batch, seq, n_mem)
    return _hawk_out(ya, [o for o, _ in attn], [l for _, l in attn], hb, zb_col, ym,
                     w_out.astype(BF16), x)


def _nsa_layer(x, mem, batch, seq, norm_g, w_in, pe_k, pe_v, phik_w1, phik_w2, phiv_w1, phiv_w2,
               mem_norm_g, w_mem_kv, w_out, final_g):
    kv0 = NSA_WIDTH
    gl0 = kv0 + 6 * NSA_KV
    z0 = gl0 + 3 * NSA_HEADS
    qm0 = z0 + NSA_WIDTH
    zm0 = qm0 + MEM_WIDTH
    wb = w_in.astype(BF16)
    gl_w = jnp.pad(wb[:, gl0:z0], ((0, 0), (0, LANES - 3 * NSA_HEADS)))
    q_w = wb[:, 0:kv0] * jnp.asarray(NSA_HEAD_DIM ** -0.5, BF16)
    w_all = jnp.concatenate([q_w, wb[:, z0:qm0], wb[:, kv0 + 2 * NSA_KV:gl0], wb[:, qm0:zm0 + MEM_WIDTH],
                             gl_w, wb[:, kv0:kv0 + 2 * NSA_KV]], axis=1)
    f32_width = LANES + 2 * NSA_KV
    bf16_width = w_all.shape[1] - f32_width
    nb, nf = _norm_matmul(x, norm_g, w_all, [(BF16, [(0, bf16_width)]), (F32, [(bf16_width, f32_width)])])
    gl_col, kc_col, vc_col = 0, 1, 2
    z_col = 1
    kv_col0 = 2 * NSA_WIDTH
    qm_col = (kv_col0 + 4 * NSA_KV) // MEM_WIDTH
    zm_col = qm_col + 1
    n_mem = mem.shape[0] // batch
    mem_kv, = _norm_matmul(mem, mem_norm_g, w_mem_kv.astype(BF16), [(BF16, [(0, 2 * MEM_WIDTH)])])

    k_cmp, v_cmp_t = _compress(nf, kc_col, vc_col, pe_k, pe_v, phik_w1, phik_w2, phiv_w1, phiv_w2,
                               batch, seq)
    ym = _mem_attn(nb, qm_col, mem_kv, nb, zm_col, batch, seq, n_mem)
    return _nsa_attn(nb, kv_col0, z_col, k_cmp, v_cmp_t, nf, gl_col, ym, w_out.astype(BF16), x, final_g,
                     batch, seq)


def kernel(x, mem, hawk_norm, hawk_w_in, hawk_conv_w, hawk_conv_b, hawk_gate_a_w, hawk_gate_a_b,
           hawk_gate_x_w, hawk_gate_x_b, hawk_lambda, hawk_mem_norm, hawk_w_mem_kv, hawk_w_out,
           nsa_norm, nsa_w_in, nsa_pe_k, nsa_pe_v, nsa_phi_k_w1, nsa_phi_k_w2, nsa_phi_v_w1,
           nsa_phi_v_w2, nsa_mem_norm, nsa_w_mem_kv, nsa_w_out, final_norm):
    batch, seq, d = x.shape
    assert hawk_norm.shape[0] == 1 and nsa_norm.shape[0] == 1, "one layer of each kind"
    assert seq % (ATTN_BLOCK * DIL_GROUPS[-1][1]) == 0
    x2 = x.reshape(batch * seq, d)
    mem2 = mem.reshape(batch * mem.shape[1], d)
    x2 = _hawk_layer(x2, mem2, batch, seq, hawk_norm[0], hawk_w_in[0], hawk_conv_w[0], hawk_conv_b[0],
                     hawk_gate_a_w[0], hawk_gate_a_b[0].reshape(-1), hawk_gate_x_w[0],
                     hawk_gate_x_b[0].reshape(-1), hawk_lambda[0], hawk_mem_norm[0], hawk_w_mem_kv[0],
                     hawk_w_out[0])
    out = _nsa_layer(x2, mem2, batch, seq, nsa_norm[0], nsa_w_in[0], nsa_pe_k[0], nsa_pe_v[0],
                     nsa_phi_k_w1[0], nsa_phi_k_w2[0], nsa_phi_v_w1[0], nsa_phi_v_w2[0],
                     nsa_mem_norm[0], nsa_w_mem_kv[0], nsa_w_out[0], final_norm)
    return out.reshape(batch, seq, d)
```

```python
import functools

import numpy as np
import jax
import jax.numpy as jnp
from jax import lax
from jax.experimental import pallas as pl
from jax.experimental.pallas import tpu as pltpu

F32 = jnp.float32
BF16 = jnp.bfloat16

NORM_EPS = 1e-6
NEG_INF = -1e30
LOG2_E = 1.4426950408889634
LANES = 128
SUBLANES = 8
BF16_SUBLANES = 16
ATTN_BLOCK = 128
VMEM_LIMIT = 56 * 1024 * 1024

LRU_WIDTH = 1024
LRU_BLOCKS = 16
LRU_BLOCK_DIM = LRU_WIDTH // LRU_BLOCKS
LRU_PACK = 256
CONV_WIDTH = 4
LRU_C = 8.0

DIL_GROUPS = ((128, 1), (512, 4), (2048, 16))
DIL_HEADS = 4
DIL_HEAD_DIM = 128
DIL_WIDTH = DIL_HEADS * DIL_HEAD_DIM
DIL_QKV = len(DIL_GROUPS) * DIL_WIDTH
LSE_LANES = LANES // DIL_HEADS

MEM_HEADS = 4
MEM_HEAD_DIM = 64
MEM_WIDTH = MEM_HEADS * MEM_HEAD_DIM

NSA_HEADS = 16
NSA_KV_GROUPS = 2
NSA_R = NSA_HEADS // NSA_KV_GROUPS
NSA_HEAD_DIM = 64
NSA_WIDTH = NSA_HEADS * NSA_HEAD_DIM
NSA_KV = NSA_KV_GROUPS * NSA_HEAD_DIM
CMP_BLOCK = 32
CMP_STRIDE = 16
SLC_BLOCK = 64
SLC_TOP_N = 8
WIN_SIZE = 512
PHI_HIDDEN = 256
SEL_FORCE = 1e6
CMP_PER_SLC = SLC_BLOCK // CMP_STRIDE


def _alibi_slopes(n):
    return [float(v) for v in np.exp2(-8.0 * np.arange(1, n + 1) / n).astype(np.float32)]


def _params(*semantics):
    return pltpu.CompilerParams(dimension_semantics=semantics, vmem_limit_bytes=VMEM_LIMIT)


def _silu(z):
    return z * jax.nn.sigmoid(z)


def _dot_t(a, b):
    return lax.dot_general(a, b, (((1,), (1,)), ((), ())), preferred_element_type=F32)


def _rms_norm_rows(x, g):
    ms = jnp.mean(x * x, axis=-1, keepdims=True)
    return (x * lax.rsqrt(ms + NORM_EPS) * g).astype(BF16)


def _norm_matmul_kernel(*refs, dil, pieces):
    def project(xn, out_ref, w_refs, by_class=False):
        col = 0
        for w_ref in w_refs:
            width = w_ref.shape[1]
            res = jnp.dot(xn, w_ref[...], preferred_element_type=F32).astype(out_ref.dtype)
            if by_class:
                per = xn.shape[0] // dil
                for c in range(dil):
                    out_ref[c, :, col:col + width] = res[c * per:(c + 1) * per]
            else:
                out_ref[:, col:col + width] = res
            col += width

    if dil == 1:
        x_ref, g_ref = refs[:2]
        w_refs, o_refs = refs[2:2 + sum(pieces)], refs[2 + sum(pieces):]
        xn = _rms_norm_rows(x_ref[...], g_ref[...])
        first = 0
        for o_ref, n_pieces in zip(o_refs, pieces):
            project(xn, o_ref, w_refs[first:first + n_pieces])
            first += n_pieces
        return
    x_ref, g_ref, *w_refs, o_ref = refs
    tm, k = x_ref.shape
    x = x_ref[...]
    xn = x * lax.rsqrt(jnp.mean(x * x, axis=-1, keepdims=True) + NORM_EPS) * g_ref[...]
    xn = jnp.swapaxes(xn.reshape(tm // dil, dil, k), 0, 1).reshape(tm, k).astype(BF16)
    project(xn, o_ref, w_refs, by_class=True)


def _norm_matmul(x, g, w, outs, dil=1, tm=1024):
    m, k = x.shape
    tm = min(tm, m)
    assert m % tm == 0 and k % LANES == 0
    pieces = [len(cols) for _, cols in outs]
    widths = [sum(width for _, width in cols) for _, cols in outs]
    w_specs = []
    for _, cols in outs:
        for col, width in cols:
            assert col % width == 0 and width % LANES == 0
            w_specs.append(pl.BlockSpec((k, width), functools.partial(lambda i, j: (0, j), j=col // width)))
    resident = [pl.BlockSpec((1, k), lambda i: (0, 0))] + w_specs
    operands = (g.reshape(1, k),) + (w,) * len(w_specs)
    kernel = functools.partial(_norm_matmul_kernel, dil=dil, pieces=pieces)
    if dil == 1:
        return pl.pallas_call(
            kernel,
            grid=(m // tm,),
            in_specs=[pl.BlockSpec((tm, k), lambda i: (i, 0))] + resident,
            out_specs=[pl.BlockSpec((tm, width), lambda i: (i, 0)) for width in widths],
            out_shape=[jax.ShapeDtypeStruct((m, width), dtype) for width, (dtype, _) in zip(widths, outs)],
            compiler_params=_params("arbitrary"),
            name="norm_matmul",
        )(x, *operands)
    per = tm // dil
    (out_dtype, _), = outs
    n, = widths
    assert tm % dil == 0 and per % BF16_SUBLANES == 0
    return pl.pallas_call(
        kernel,
        grid=(m // tm,),
        in_specs=[pl.BlockSpec((tm, k), lambda i: (i, 0))] + resident,
        out_specs=pl.BlockSpec((dil, per, n), lambda i: (0, i, 0)),
        out_shape=jax.ShapeDtypeStruct((dil, m // dil, n), out_dtype),
        compiler_params=_params("arbitrary"),
        name="norm_matmul_dil",
    )(x, *operands)


def _rglru_kernel(xa_ref, za_ref, cw_ref, cb_ref, wa_ref, ba_ref, wx_ref, bx_ref, lam_ref,
                  o_ref, xpad_ref, h_ref):
    t = pl.program_id(1)
    tt, width = xa_ref.shape
    halo = SUBLANES

    @pl.when(t == 0)
    def _():
        xpad_ref[0:halo, :] = jnp.zeros((halo, width), F32)
        h_ref[...] = jnp.zeros_like(h_ref)

    x = xa_ref[...]
    xpad_ref[halo:halo + tt, :] = x
    cw = cw_ref[...]
    y = cw[CONV_WIDTH - 1:CONV_WIDTH] * x
    for k in range(1, CONV_WIDTH):
        y = y + cw[CONV_WIDTH - 1 - k:CONV_WIDTH - k] * xpad_ref[halo - k:halo - k + tt, :]
    y = y + cb_ref[...]
    xpad_ref[0:halo, :] = x[tt - halo:tt, :]

    yb = y.astype(BF16)
    r_parts, i_parts = [], []
    for p in range(width // LRU_PACK):
        ys = yb[:, p * LRU_PACK:(p + 1) * LRU_PACK]
        r_parts.append(jnp.dot(ys, wa_ref[p], preferred_element_type=F32))
        i_parts.append(jnp.dot(ys, wx_ref[p], preferred_element_type=F32))
    r = jax.nn.sigmoid(jnp.concatenate(r_parts, axis=1) + ba_ref[...])
    gi = jax.nn.sigmoid(jnp.concatenate(i_parts, axis=1) + bx_ref[...])

    nl = -lam_ref[...]
    softplus = jnp.maximum(nl, 0.0) + jnp.log1p(jnp.exp(-jnp.abs(nl)))
    log_a = (-LRU_C) * r * softplus
    a = jnp.exp(log_a)
    w = -jnp.tanh(log_a) * (a * a + 1.0)
    mult = jnp.where(w > 0.0, w * lax.rsqrt(w), 0.0)
    b = y * gi * mult
    first = (lax.broadcasted_iota(jnp.int32, (SUBLANES, width), 0) == 0) & (t == 0)
    b = jnp.concatenate([jnp.where(first, (y * gi)[0:SUBLANES], b[0:SUBLANES]), b[SUBLANES:]], axis=0)

    groups = tt // SUBLANES
    a3 = a.reshape(groups, SUBLANES, width)
    b3 = b.reshape(groups, SUBLANES, width)
    sub = lax.broadcasted_iota(jnp.int32, (1, SUBLANES, width), 1)
    k = 1
    while k < SUBLANES:
        keep = sub >= k
        a_sh = jnp.where(keep, pltpu.roll(a3, k, axis=1), 1.0)
        b_sh = jnp.where(keep, pltpu.roll(b3, k, axis=1), 0.0)
        b3 = a3 * b_sh + b3
        a3 = a3 * a_sh
        k *= 2
    carry = jnp.broadcast_to(h_ref[...], (SUBLANES, width))
    hs = []
    for gidx in range(groups):
        hg = a3[gidx] * carry + b3[gidx]
        hs.append(hg)
        carry = jnp.broadcast_to(hg[SUBLANES - 1:SUBLANES], (SUBLANES, width))
    h = jnp.concatenate(hs, axis=0)
    h_ref[...] = carry[0:1]
    o_ref[...] = (h * _silu(za_ref[...].astype(F32))).astype(o_ref.dtype)


def _rglru(xa, za_src, za_col, conv_w, conv_b, wa, ba, wx, bx, lam, batch, seq, tt=1024):
    width = LRU_WIDTH
    nt = seq // tt
    packs = width // LRU_PACK
    vec = pl.BlockSpec((1, width), lambda b, t: (0, 0))
    gate_w = pl.BlockSpec((packs, LRU_PACK, LRU_PACK), lambda b, t: (0, 0, 0))
    return pl.pallas_call(
        _rglru_kernel,
        grid=(batch, nt),
        in_specs=[pl.BlockSpec((tt, width), lambda b, t: (b * nt + t, 0)),
                  pl.BlockSpec((tt, width), lambda b, t: (b * nt + t, za_col)),
                  pl.BlockSpec((CONV_WIDTH, width), lambda b, t: (0, 0)),
                  vec, gate_w, vec, gate_w, vec, vec],
        out_specs=pl.BlockSpec((tt, width), lambda b, t: (b * nt + t, 0)),
        out_shape=jax.ShapeDtypeStruct((batch * seq, width), BF16),
        scratch_shapes=[pltpu.VMEM((tt + SUBLANES, width), F32), pltpu.VMEM((1, width), F32)],
        compiler_params=_params("arbitrary", "arbitrary"),
        name="rglru",
    )(xa, za_src, conv_w, conv_b.reshape(1, width), wa, ba.reshape(1, width), wx, bx.reshape(1, width),
      lam.reshape(1, width))


def _pack_block_diag(w):
    per = LRU_PACK // LRU_BLOCK_DIM
    w = w.reshape(LRU_BLOCKS // per, per, LRU_BLOCK_DIM, LRU_BLOCK_DIM)
    eye = jnp.eye(per, dtype=w.dtype)
    packed = w[:, :, :, None, :] * eye[None, :, None, :, None]
    return packed.reshape(LRU_BLOCKS // per, LRU_PACK, LRU_PACK).astype(BF16)


def _dil_attn_kernel(*refs, slopes, pos_scale, max_dist, has_halo, dil, n_cls, n_blk):
    if has_halo:
        q_ref, kh_ref, k_ref, vh_ref, v_ref = refs[:5]
        out_refs = refs[5:]
    else:
        q_ref, k_ref, v_ref = refs[:3]
        out_refs = refs[3:]
    n_out = DIL_HEADS + 1
    dst_refs = out_refs[:n_out]
    stage_refs = out_refs[n_out:] if dil > 1 else dst_refs
    first_super = pl.program_id(1) == 0
    cls0 = pl.program_id(2) * n_cls
    blk = ATTN_BLOCK
    scale = DIL_HEAD_DIM ** -0.5

    def band(width, halo_live):
        row = lax.broadcasted_iota(jnp.int32, (blk, width), 0)
        col = lax.broadcasted_iota(jnp.int32, (blk, width), 1)
        dist = (width - blk) + row - col
        valid = (dist >= 0) & (dist <= max_dist)
        if halo_live is not None:
            valid = valid & ((col >= blk) | halo_live)
        distf = (dist * pos_scale).astype(F32)
        return [jnp.where(valid, (-slope / scale) * distf, NEG_INF) for slope in slopes]

    bias_inner = band(2 * blk, None) if n_blk > 1 else None
    bias_first = band(2 * blk, jnp.logical_not(first_super)) if has_halo else band(blk, None)

    def scores(cc, jb):
        cur = slice(jb * blk, (jb + 1) * blk)
        bias = bias_inner if jb > 0 else bias_first
        out = []
        for h in range(DIL_HEADS):
            hs = slice(h * DIL_HEAD_DIM, (h + 1) * DIL_HEAD_DIM)
            q = q_ref[cc, cur, hs]
            if jb > 0:
                k = k_ref[cc, (jb - 1) * blk:(jb + 1) * blk, hs]
                v = v_ref[cc, (jb - 1) * blk:(jb + 1) * blk, hs]
            elif has_halo:
                k = jnp.concatenate([kh_ref[cc, :, hs], k_ref[cc, cur, hs]], axis=0)
                v = jnp.concatenate([vh_ref[cc, :, hs], v_ref[cc, cur, hs]], axis=0)
            else:
                k, v = k_ref[cc, cur, hs], v_ref[cc, cur, hs]
            out.append((_dot_t(q, k) + bias[h], v))
        return out

    def finish(cc, jb, pairs):
        where = (cls0 + cc, slice(jb * blk, (jb + 1) * blk)) if dil > 1 else (slice(jb * blk, (jb + 1) * blk),)
        lses = []
        for h, (s, v) in enumerate(pairs):
            m = jnp.max(s, axis=-1, keepdims=True)
            e = jnp.exp2((s - m) * (scale * LOG2_E))
            den = jnp.sum(e, axis=-1, keepdims=True)
            o = jnp.dot(e.astype(BF16), v, preferred_element_type=F32) / den
            stage_refs[h][where] = o.astype(stage_refs[h].dtype)
            lses.append(jnp.broadcast_to(m * scale + jnp.log(den), (blk, LSE_LANES)))
        stage_refs[DIL_HEADS][where] = jnp.concatenate(lses, axis=1)

    pending = None
    for cc in range(n_cls):
        for jb in range(n_blk):
            pairs = scores(cc, jb)
            if pending is not None:
                finish(*pending)
            pending = (cc, jb, pairs)
    finish(*pending)

    if dil > 1:
        @pl.when(pl.program_id(2) == pl.num_programs(2) - 1)
        def _():
            for stage, dst in zip(stage_refs, dst_refs):
                dst[...] = jnp.swapaxes(stage[...], 0, 1).reshape(dst.shape).astype(dst.dtype)


def _dil_attn(qkv, col0, gi, batch, seq, work=8):
    window, dil = DIL_GROUPS[gi]
    sub = seq // dil
    nb = sub // ATTN_BLOCK
    n_blk = min(work, nb)
    n_cls = min(work // n_blk, dil)
    n_super = nb // n_blk
    has_halo = n_super > 1
    span = n_blk * ATTN_BLOCK
    slopes = _alibi_slopes(len(DIL_GROUPS) * DIL_HEADS)[gi * DIL_HEADS:(gi + 1) * DIL_HEADS]
    cur = lambda col: pl.BlockSpec((n_cls, span, DIL_WIDTH), lambda b, i, c: (c, b * n_super + i, col0 + col))
    halo = lambda col: pl.BlockSpec(
        (n_cls, ATTN_BLOCK, DIL_WIDTH),
        lambda b, i, c: (c, jnp.maximum((b * n_super + i) * n_blk - 1, 0), col0 + col))
    if has_halo:
        in_specs = [cur(0), halo(1), cur(1), halo(2), cur(2)]
    else:
        in_specs = [cur(0), cur(1), cur(2)]
    n_out = DIL_HEADS + 1
    *o, lse = pl.pallas_call(
        functools.partial(_dil_attn_kernel, slopes=slopes, pos_scale=dil, max_dist=window // dil,
                          has_halo=has_halo, dil=dil, n_cls=n_cls, n_blk=n_blk),
        grid=(batch, n_super, dil // n_cls),
        in_specs=in_specs,
        out_specs=[pl.BlockSpec((span * dil, LANES), lambda b, i, c: (b * n_super + i, 0))] * n_out,
        out_shape=[jax.ShapeDtypeStruct((batch * seq, LANES), BF16)] * DIL_HEADS
                  + [jax.ShapeDtypeStruct((batch * seq, LANES), F32)],
        scratch_shapes=[pltpu.VMEM((dil, span, LANES), F32)] * (n_out if dil > 1 else 0),
        compiler_params=_params("arbitrary", "arbitrary", "arbitrary"),
        name=f"dil_attn_d{dil}",
    )(*([qkv] * len(in_specs)))
    return o, lse


def _mem_attn_kernel(q_ref, k_ref, v_ref, z_ref, o_ref, ks_ref, vt_ref):
    hd = MEM_HEAD_DIM
    n_mem = k_ref.shape[0]

    @pl.when(pl.program_id(1) == 0)
    def _():
        ks_ref[...] = (k_ref[...].astype(F32) * (hd ** -0.5)).astype(BF16)
        vt = v_ref[...].astype(F32).T.astype(BF16)
        for h in range(MEM_HEADS):
            vt_ref[h, 0:hd, :] = vt[h * hd:(h + 1) * hd, :]
            vt_ref[h, hd:, :] = jnp.ones((vt_ref.shape[1] - hd, n_mem), BF16)

    heads = [slice(h * hd, (h + 1) * hd) for h in range(MEM_HEADS)]
    scores = [_dot_t(ks_ref[:, hs], q_ref[:, hs]) for hs in heads]
    outs = []
    for h, s in enumerate(scores):
        e = jnp.exp(s - jnp.max(s, axis=0, keepdims=True)).astype(BF16)
        acc = jnp.dot(vt_ref[h], e, preferred_element_type=F32)
        outs.append(acc[0:hd, :] / acc[hd:hd + 1, :])
    o = jnp.concatenate(outs, axis=0).T
    o_ref[...] = (o * _silu(z_ref[...].astype(F32))).astype(o_ref.dtype)


def _mem_attn(qsrc, q_col, kv, zsrc, z_col, batch, seq, n_mem, tq=1024):
    nq = seq // tq
    return pl.pallas_call(
        _mem_attn_kernel,
        grid=(batch, nq),
        in_specs=[pl.BlockSpec((tq, MEM_WIDTH), lambda b, i: (b * nq + i, q_col)),
                  pl.BlockSpec((n_mem, MEM_WIDTH), lambda b, i: (b, 0)),
                  pl.BlockSpec((n_mem, MEM_WIDTH), lambda b, i: (b, 1)),
                  pl.BlockSpec((tq, MEM_WIDTH), lambda b, i: (b * nq + i, z_col))],
        out_specs=pl.BlockSpec((tq, MEM_WIDTH), lambda b, i: (b * nq + i, 0)),
        out_shape=jax.ShapeDtypeStruct((batch * seq, MEM_WIDTH), BF16),
        scratch_shapes=[pltpu.VMEM((n_mem, MEM_WIDTH), BF16),
                        pltpu.VMEM((MEM_HEADS, MEM_HEAD_DIM + BF16_SUBLANES, n_mem), BF16)],
        compiler_params=_params("arbitrary", "arbitrary"),
        name="mem_attn",
    )(qsrc, kv, kv, zsrc)


def _hawk_out_kernel(*refs):
    n_groups = len(DIL_GROUPS)
    ya_ref = refs[0]
    o_refs = refs[1:1 + n_groups * DIL_HEADS]
    l_refs = refs[1 + n_groups * DIL_HEADS:1 + n_groups * (DIL_HEADS + 1)]
    zb_ref, ym_ref, w_ref, x_ref, out_ref = refs[1 + n_groups * (DIL_HEADS + 1):]
    a_end = LRU_WIDTH
    b_end = a_end + DIL_WIDTH
    y = jnp.dot(ya_ref[...], w_ref[0:a_end, :], preferred_element_type=F32)
    y = y + jnp.dot(ym_ref[...], w_ref[b_end:b_end + MEM_WIDTH, :], preferred_element_type=F32)
    parts = []
    for h in range(DIL_HEADS):
        ls = [l[:, h * LSE_LANES:h * LSE_LANES + 1] for l in l_refs]
        m = functools.reduce(jnp.maximum, ls)
        ws = [jnp.exp(l - m) for l in ls]
        num = sum(w * o_refs[gi * DIL_HEADS + h][...].astype(F32) for gi, w in enumerate(ws))
        parts.append(num / sum(ws))
    yb = (jnp.concatenate(parts, axis=1) * _silu(zb_ref[...].astype(F32))).astype(BF16)
    y = y + jnp.dot(yb, w_ref[a_end:b_end, :], preferred_element_type=F32)
    out_ref[...] = x_ref[...] + y


def _hawk_out(ya, os_, ls_, zb_src, zb_col, ym, w, x, tm=1024):
    m, d = x.shape
    row = lambda width, col=0: pl.BlockSpec((tm, width), lambda i: (i, col))
    heads = [o for group in os_ for o in group]
    return pl.pallas_call(
        _hawk_out_kernel,
        grid=(m // tm,),
        in_specs=[row(LRU_WIDTH)] + [row(DIL_HEAD_DIM)] * len(heads) + [row(LANES)] * len(ls_)
                 + [row(DIL_WIDTH, zb_col), row(MEM_WIDTH),
                    pl.BlockSpec(w.shape, lambda i: (0, 0)), row(d)],
        out_specs=row(d),
        out_shape=jax.ShapeDtypeStruct((m, d), F32),
        compiler_params=_params("arbitrary"),
        name="hawk_out",
    )(ya, *heads, *ls_, zb_src, ym, w, x)


def _compress_kernel(k_ref, v_ref, pe_ref, w1_ref, w2k_ref, w2vt_ref, ko_ref, vto_ref):
    n_blk = k_ref.shape[0] // CMP_STRIDE

    def hidden(which, src_ref):
        x = jnp.concatenate([src_ref[pl.ds(p, n_blk, stride=CMP_STRIDE), :] for p in range(CMP_STRIDE)],
                            axis=1).astype(BF16)
        first = jnp.dot(x, w1_ref[which, 0], preferred_element_type=F32)
        second = jnp.dot(x, w1_ref[which, 1], preferred_element_type=F32)
        pe = (jnp.dot(pe_ref[which, 0], w1_ref[which, 0], preferred_element_type=F32)
              + jnp.dot(pe_ref[which, 1], w1_ref[which, 1], preferred_element_type=F32))
        return _silu(first + pltpu.roll(second, n_blk - 1, axis=0) + pe[0:1, :]).astype(BF16)

    act_k, act_v = hidden(0, k_ref), hidden(1, v_ref)
    part = lambda a, g: a[:, g * PHI_HIDDEN:(g + 1) * PHI_HIDDEN]
    ks = [jnp.dot(part(act_k, g), w2k_ref[...], preferred_element_type=F32) for g in range(NSA_KV_GROUPS)]
    vts = [_dot_t(w2vt_ref[...], part(act_v, g)) for g in range(NSA_KV_GROUPS)]
    ko_ref[...] = jnp.concatenate(ks, axis=1).astype(ko_ref.dtype)
    vto_ref[...] = jnp.concatenate(vts, axis=0).astype(vto_ref.dtype)


def _compress(src, k_col, v_col, pe_k, pe_v, k_w1, k_w2, v_w1, v_w2, batch, seq):
    half = CMP_BLOCK // 2
    assert half == CMP_STRIDE and NSA_KV == LANES
    n_blk = seq // CMP_STRIDE
    hd = NSA_HEAD_DIM
    w1 = jnp.stack([k_w1, v_w1]).reshape(2, 2, half, hd, PHI_HIDDEN).astype(BF16)
    zero = jnp.zeros_like(w1)
    per_group = [jnp.concatenate([w1 if g == col else zero for col in range(NSA_KV_GROUPS)], axis=-1)
                 for g in range(NSA_KV_GROUPS)]
    w1e = jnp.stack(per_group, axis=3).reshape(2, 2, half * NSA_KV, NSA_KV_GROUPS * PHI_HIDDEN)
    pe = jnp.stack([pe_k, pe_v]).reshape(2, 2, half, 1, hd)
    pe = jnp.broadcast_to(pe, (2, 2, half, NSA_KV_GROUPS, hd)).reshape(2, 2, 1, half * NSA_KV)
    pe = jnp.broadcast_to(pe, (2, 2, SUBLANES, half * NSA_KV)).astype(BF16)
    w2k = k_w2.astype(BF16)
    w2vt = v_w2.T.astype(BF16)
    whole = lambda a: pl.BlockSpec(a.shape, lambda b: (0,) * a.ndim)
    return pl.pallas_call(
        _compress_kernel,
        grid=(batch,),
        in_specs=[pl.BlockSpec((seq, LANES), lambda b: (b, k_col)),
                  pl.BlockSpec((seq, LANES), lambda b: (b, v_col)),
                  whole(pe), whole(w1e), whole(w2k), whole(w2vt)],
        out_specs=[pl.BlockSpec((None, n_blk, NSA_KV), lambda b: (b, 0, 0)),
                   pl.BlockSpec((None, NSA_KV, n_blk), lambda b: (b, 0, 0))],
        out_shape=[jax.ShapeDtypeStruct((batch, n_blk, NSA_KV), BF16),
                   jax.ShapeDtypeStruct((batch, NSA_KV, n_blk), BF16)],
        compiler_params=_params("arbitrary"),
        name="compress",
    )(src, src, pe, w1e, w2k, w2vt)


KEY_CHUNK = 256
NSA_TQ = 256
SLOPE_PIECES = 3
N_FEATS = 2 * SLOPE_PIECES
FEAT_LANES = 32
KEY_COLS = NSA_HEAD_DIM + 2 * FEAT_LANES
N_SLC = 32


def _slope_pieces(slope):
    rest = np.float32(slope)
    pieces = []
    for _ in range(SLOPE_PIECES):
        p = np.float32(np.asarray(rest).astype(BF16))
        pieces.append(float(p))
        rest = np.float32(rest - p)
    return pieces


def _lane_table(lane, values):
    out = jnp.zeros(lane.shape, F32)
    for idx, v in enumerate(values):
        out = jnp.where(lane == idx, v, out)
    return out


def _key_feats(pos_hi, pos_lo, lane):
    return jnp.where(lane < SLOPE_PIECES, pos_hi, jnp.where(lane < N_FEATS, pos_lo, 0)).astype(F32)


def _tile_heads(x):
    return jnp.concatenate([x] * NSA_R, axis=1)


def _chunk_loop(lo, hi, body, init, widths=(4, 2, 1)):
    carry, start = init, lo
    for idx, w in enumerate(widths):
        count = (hi - start) // w

        def step(p, cr, start=start, w=w):
            first = start + p * w
            return body([first + j for j in range(w)], cr)

        if idx == 0:
            carry = lax.fori_loop(0, count, step, carry)
        else:
            carry = lax.cond(count > 0, functools.partial(step, 0), lambda cr: cr, carry)
        start = start + count * w
    return carry


def _nsa_kernel(q_ref, kc_ref, vct_ref, ksrc_ref, vsrc_ref, kwsrc_ref, vwsrc_ref, feat_ref, hot_ref,
                gl_ref, z_ref, ym_ref, w_ref, x_ref, fin_ref, o_ref,
                ks_ref, vst_ref, kw_ref, vwt_ref, s_ref, acc_ref, imp_ref):
    i = pl.program_id(1)
    tq = q_ref.shape[0]
    hd = NSA_HEAD_DIM
    n_cmp = kc_ref.shape[0]

    @pl.when(i == 0)
    def _():
        for g in range(NSA_KV_GROUPS):
            gs = slice(g * hd, (g + 1) * hd)
            for dst, src, tail in ((ks_ref, ksrc_ref, hot_ref[...]),
                                   (kw_ref, kwsrc_ref, jnp.zeros(hot_ref.shape, BF16))):
                dst[:, g * KEY_COLS:g * KEY_COLS + hd] = src[:, gs]
                dst[:, g * KEY_COLS + hd:g * KEY_COLS + hd + FEAT_LANES] = feat_ref[...]
                dst[:, g * KEY_COLS + hd + FEAT_LANES:(g + 1) * KEY_COLS] = tail
        for c in range(vst_ref.shape[0]):
            rows = slice(c * KEY_CHUNK, (c + 1) * KEY_CHUNK)
            vst_ref[c] = vsrc_ref[rows, :].astype(F32).T.astype(BF16)
            vwt_ref[c] = vwsrc_ref[rows, :].astype(F32).T.astype(BF16)

    slopes_all = _alibi_slopes(NSA_HEADS)
    gates_t = jax.nn.sigmoid(gl_ref[...]).T
    feat_lane = lax.broadcasted_iota(jnp.int32, (tq, FEAT_LANES), 1)
    no_sel = jnp.zeros((NSA_R * tq, FEAT_LANES), BF16)
    key_row = lax.broadcasted_iota(jnp.int32, (KEY_CHUNK, tq), 0)
    t_pos = i * tq + lax.broadcasted_iota(jnp.int32, (KEY_CHUNK, tq), 1)
    ones_rows = jnp.ones((BF16_SUBLANES, KEY_CHUNK), BF16)
    win_lo = jnp.maximum(i * tq - (WIN_SIZE - 1), 0) // KEY_CHUNK
    chunks_hi = (i * tq + tq - 1) // KEY_CHUNK + 1

    groups = range(NSA_KV_GROUPS)
    q_win, q_slc, o_cmp = [], [], []
    for g in groups:
        slopes = slopes_all[g * NSA_R:(g + 1) * NSA_R]
        gs = slice(g * hd, (g + 1) * hd)
        q_parts = []
        for r in range(NSA_R):
            qr = q_ref[:, (g * NSA_R + r) * hd:(g * NSA_R + r + 1) * hd]
            feats = _lane_table(feat_lane, _slope_pieces(slopes[r]) * 2).astype(BF16)
            q_parts.append(jnp.concatenate([qr, feats], axis=1))
        q_feat = jnp.concatenate(q_parts, axis=0)
        q_aug = jnp.concatenate([q_feat, no_sel], axis=1)

        n_row = lax.broadcasted_iota(jnp.int32, (n_cmp, tq), 0)
        t_cmp = i * tq + lax.broadcasted_iota(jnp.int32, (n_cmp, tq), 1)
        visible = t_cmp >= n_row * CMP_STRIDE + (CMP_BLOCK - 1)
        cfeat_row = lax.broadcasted_iota(jnp.int32, (n_cmp, 2 * FEAT_LANES), 0)
        cfeat_lane = lax.broadcasted_iota(jnp.int32, (n_cmp, 2 * FEAT_LANES), 1)
        kc_feats = _key_feats(cfeat_row * CMP_STRIDE, 0, cfeat_lane)
        kc_aug = jnp.concatenate([kc_ref[:, gs], kc_feats.astype(BF16)], axis=1)
        s = _dot_t(kc_aug, q_aug) + _tile_heads(jnp.where(visible, 0.0, NEG_INF))
        m = jnp.max(s, axis=0, keepdims=True)
        e = jnp.exp(s - m)
        t_one = i * tq + lax.broadcasted_iota(jnp.int32, (1, NSA_R * tq), 1) % tq
        any_visible = t_one >= (CMP_BLOCK - 1)
        p = e * jnp.where(any_visible, 1.0 / jnp.sum(e, axis=0, keepdims=True), 0.0)
        o_cmp.append(jnp.dot(vct_ref[gs, :], p.astype(BF16), preferred_element_type=F32))
        p_sum = p[:, 0:tq]
        for r in range(1, NSA_R):
            p_sum = p_sum + p[:, r * tq:(r + 1) * tq]

        band = p_sum + pltpu.roll(p_sum, 1, axis=0)
        for k in range(1, CMP_PER_SLC):
            band = band + pltpu.roll(p_sum, n_cmp - k, axis=0)
        halves = []
        for half in range(tq // LANES):
            imp_ref[...] = band[:, half * LANES:(half + 1) * LANES]
            halves.append(imp_ref[pl.ds(0, N_SLC, stride=CMP_PER_SLC), :])
        imp = jnp.concatenate(halves, axis=1)
        blk_j = lax.broadcasted_iota(jnp.int32, (N_SLC, tq), 0)
        cur = (i * tq + lax.broadcasted_iota(jnp.int32, (N_SLC, tq), 1)) // SLC_BLOCK
        forced = (blk_j == 0) | (blk_j == cur) | (blk_j == cur - 1)
        v_imp = jnp.where(forced, SEL_FORCE, jnp.where(blk_j > cur, -SEL_FORCE, imp))
        rank = jnp.zeros((N_SLC, tq), F32)
        for other in range(N_SLC):
            row = v_imp[other:other + 1, :]
            ahead = (row > v_imp) | ((row == v_imp) & (blk_j > other))
            rank = rank + jnp.where(ahead, 1.0, 0.0)
        sel_bias = jnp.where(rank < SLC_TOP_N, 0.0, NEG_INF)

        padded = jnp.concatenate([sel_bias, jnp.zeros((LANES - N_SLC, tq), F32)], axis=0)
        sel_t = padded.T[:, 0:FEAT_LANES].astype(BF16)
        q_win.append(q_aug)
        q_slc.append(jnp.concatenate([q_feat, jnp.concatenate([sel_t] * NSA_R, axis=0)], axis=1))

    def attend(q_brs, k_ref, key_cols, vt_ref, lo, hi, masked_from, mask_fn):
        def scores(cs, m_run, masked):
            starts = [pl.multiple_of(c * KEY_CHUNK, KEY_CHUNK) for c in cs]
            scs = [[_dot_t(k_ref[pl.ds(start, KEY_CHUNK), g * key_cols:(g + 1) * key_cols], q_brs[g])
                    for g in groups] for start in starts]
            for c, start, sc in zip(cs, starts, scs):
                if masked:
                    bias = _tile_heads(jnp.where(mask_fn(t_pos - (start + key_row)), 0.0, NEG_INF))
                    sc = [x + bias for x in sc]
                for g in groups:
                    s_ref[g, c] = sc[g]
                m_run = tuple(jnp.maximum(m_run[g], jnp.max(sc[g], axis=0, keepdims=True)) for g in groups)
            return m_run

        def weighted(cs, carry):
            for c in cs:
                for g in groups:
                    e = jnp.exp(s_ref[g, c] - m_rows[g]).astype(BF16)
                    v_ext = jnp.concatenate([vt_ref[c, g * hd:(g + 1) * hd, :], ones_rows], axis=0)
                    acc_ref[g] += jnp.dot(v_ext, e, preferred_element_type=F32)
            return carry

        m_rows = tuple(jnp.full((1, NSA_R * tq), NEG_INF, F32) for _ in groups)
        m_rows = _chunk_loop(lo, masked_from, functools.partial(scores, masked=False), m_rows)
        m_rows = _chunk_loop(masked_from, hi, functools.partial(scores, masked=True), m_rows)
        acc_ref[...] = jnp.zeros(acc_ref.shape, F32)
        _chunk_loop(lo, hi, weighted, 0)
        return [acc_ref[g, 0:hd, :] / acc_ref[g, hd:hd + 1, :] for g in groups]

    o_slc = attend(q_slc, ks_ref, KEY_COLS, vst_ref, 0, chunks_hi, (i * tq) // KEY_CHUNK,
                   lambda dist: dist >= 0)
    o_win = attend(q_win, kw_ref, KEY_COLS, vwt_ref, win_lo, chunks_hi, win_lo,
                   lambda dist: (dist >= 0) & (dist <= WIN_SIZE - 1))

    y = jnp.dot(ym_ref[...], w_ref[NSA_WIDTH:NSA_WIDTH + MEM_WIDTH, :], preferred_element_type=F32)
    for g in groups:
        def gate(kind):
            base = g * NSA_R * 3 + kind
            return jnp.concatenate([gates_t[base + 3 * r:base + 3 * r + 1, :] for r in range(NSA_R)], axis=1)

        o = gate(0) * o_cmp[g] + gate(1) * o_slc[g] + gate(2) * o_win[g]
        pairs = []
        for r in range(0, NSA_R, 2):
            two = jnp.concatenate([o[:, r * tq:(r + 1) * tq], o[:, (r + 1) * tq:(r + 2) * tq]], axis=0)
            pairs.append(two.T)
        cs = slice(g * NSA_R * hd, (g + 1) * NSA_R * hd)
        yo = (jnp.concatenate(pairs, axis=1) * _silu(z_ref[:, cs].astype(F32))).astype(BF16)
        y = y + jnp.dot(yo, w_ref[cs, :], preferred_element_type=F32)

    x = x_ref[...] + y
    ms = jnp.mean(x * x, axis=-1, keepdims=True)
    o_ref[...] = x * lax.rsqrt(ms + NORM_EPS) * fin_ref[...]


def _key_pos_feats(seq):
    assert seq // SLC_BLOCK == N_SLC <= FEAT_LANES
    pos = np.arange(seq)
    feats = np.zeros((seq, FEAT_LANES), np.float32)
    feats[:, 0:SLOPE_PIECES] = ((pos // SLC_BLOCK) * SLC_BLOCK)[:, None]
    feats[:, SLOPE_PIECES:N_FEATS] = (pos % SLC_BLOCK)[:, None]
    onehot = (np.arange(FEAT_LANES)[None, :] == (pos // SLC_BLOCK)[:, None]).astype(np.float32)
    return jnp.asarray(feats, BF16), jnp.asarray(onehot, BF16)


def _nsa_attn(nb, kv_col0, z_col, k_cmp, v_cmp_t, nf, gl_col, ym, w_out, x, final_g, batch, seq):
    tq = NSA_TQ
    nq = seq // tq
    d = x.shape[1]
    feats, onehot = _key_pos_feats(seq)
    kv_blk = kv_col0 // NSA_KV
    seq_cols = lambda col: pl.BlockSpec((seq, NSA_KV), lambda b, i: (b, kv_blk + col))
    const = lambda a: pl.BlockSpec(a.shape, lambda b, i: (0,) * a.ndim)
    per_batch = lambda a: pl.BlockSpec((None,) + a.shape[1:], lambda b, i: (b,) + (0,) * (a.ndim - 1))
    rows = lambda width, col=0: pl.BlockSpec((tq, width), lambda b, i: (b * nq + i, col))
    fin = final_g.reshape(1, d)
    return pl.pallas_call(
        _nsa_kernel,
        grid=(batch, nq),
        in_specs=[rows(NSA_WIDTH), per_batch(k_cmp), per_batch(v_cmp_t),
                  seq_cols(0), seq_cols(1), seq_cols(2), seq_cols(3), const(feats), const(onehot),
                  rows(LANES, gl_col), rows(NSA_WIDTH, z_col), rows(MEM_WIDTH), const(w_out), rows(d),
                  const(fin)],
        out_specs=rows(d),
        out_shape=jax.ShapeDtypeStruct((batch * seq, d), F32),
        scratch_shapes=[pltpu.VMEM((seq, NSA_KV_GROUPS * KEY_COLS), BF16),
                        pltpu.VMEM((seq // KEY_CHUNK, NSA_KV, KEY_CHUNK), BF16),
                        pltpu.VMEM((seq, NSA_KV_GROUPS * KEY_COLS), BF16),
                        pltpu.VMEM((seq // KEY_CHUNK, NSA_KV, KEY_CHUNK), BF16),
                        pltpu.VMEM((NSA_KV_GROUPS, seq // KEY_CHUNK, KEY_CHUNK, NSA_R * tq), F32),
                        pltpu.VMEM((NSA_KV_GROUPS, NSA_HEAD_DIM + BF16_SUBLANES, NSA_R * tq), F32),
                        pltpu.VMEM((seq // CMP_STRIDE, LANES), F32)],
        compiler_params=_params("arbitrary", "arbitrary"),
        name="nsa_attn",
    )(nb, k_cmp, v_cmp_t, nb, nb, nb, nb, feats, onehot, nf, nb, ym, w_out, x, fin)


def _hawk_layer(x, mem, batch, seq, norm_g, w_in, conv_w, conv_b, ga_w, ga_b, gx_w, gx_b, lam,
                mem_norm_g, w_mem_kv, w_out):
    xa0, za0 = 0, LRU_WIDTH
    q0 = 2 * LRU_WIDTH
    k0, v0 = q0 + DIL_QKV, q0 + 2 * DIL_QKV
    zb0 = q0 + 3 * DIL_QKV
    qm0 = zb0 + DIL_WIDTH
    zm0 = qm0 + MEM_WIDTH

    def qkv_cols(gi):
        return [(base + gi * DIL_WIDTH, DIL_WIDTH) for base in (q0, k0, v0)]

    w = w_in.astype(BF16)
    nat_cols = [(za0, LRU_WIDTH), *qkv_cols(0), (zb0, DIL_WIDTH), (qm0, MEM_WIDTH), (zm0, MEM_WIDTH)]
    xa, hb = _norm_matmul(x, norm_g, w, [(F32, [(xa0, LRU_WIDTH)]), (BF16, nat_cols)])
    za_col = 0
    qkv0_col = LRU_WIDTH // DIL_WIDTH
    zb_col = (LRU_WIDTH + 3 * DIL_WIDTH) // DIL_WIDTH
    qm_col = (LRU_WIDTH + 4 * DIL_WIDTH) // MEM_WIDTH
    zm_col = qm_col + 1
    qkv = [(hb[None], qkv0_col)]
    for gi in range(1, len(DIL_GROUPS)):
        qkv.append((_norm_matmul(x, norm_g, w, [(BF16, qkv_cols(gi))], dil=DIL_GROUPS[gi][1]), 0))
    n_mem = mem.shape[0] // batch
    mem_kv, = _norm_matmul(mem, mem_norm_g, w_mem_kv.astype(BF16), [(BF16, [(0, 2 * MEM_WIDTH)])])

    ya = _rglru(xa, hb, za_col, conv_w, conv_b, _pack_block_diag(ga_w), ga_b, _pack_block_diag(gx_w), gx_b,
                lam, batch, seq)
    attn = [_dil_attn(arr, col0, gi, batch, seq) for gi, (arr, col0) in enumerate(qkv)]
    ym = _mem_attn(hb, qm_col, mem_kv, hb, zm_col, batch, seq, n_mem)
    return _hawk_out(ya, [o for o, _ in attn], [l for _, l in attn], hb, zb_col, ym,
                     w_out.astype(BF16), x)


def _nsa_layer(x, mem, batch, seq, norm_g, w_in, pe_k, pe_v, phik_w1, phik_w2, phiv_w1, phiv_w2,
               mem_norm_g, w_mem_kv, w_out, final_g):
    kv0 = NSA_WIDTH
    gl0 = kv0 + 6 * NSA_KV
    z0 = gl0 + 3 * NSA_HEADS
    qm0 = z0 + NSA_WIDTH
    zm0 = qm0 + MEM_WIDTH
    wb = w_in.astype(BF16)
    gl_w = jnp.pad(wb[:, gl0:z0], ((0, 0), (0, LANES - 3 * NSA_HEADS)))
    q_w = wb[:, 0:kv0] * jnp.asarray(NSA_HEAD_DIM ** -0.5, BF16)
    w_all = jnp.concatenate([q_w, wb[:, z0:qm0], wb[:, kv0 + 2 * NSA_KV:gl0], wb[:, qm0:zm0 + MEM_WIDTH],
                             gl_w, wb[:, kv0:kv0 + 2 * NSA_KV]], axis=1)
    f32_width = LANES + 2 * NSA_KV
    bf16_width = w_all.shape[1] - f32_width
    nb, nf = _norm_matmul(x, norm_g, w_all, [(BF16, [(0, bf16_width)]), (F32, [(bf16_width, f32_width)])])
    gl_col, kc_col, vc_col = 0, 1, 2
    z_col = 1
    kv_col0 = 2 * NSA_WIDTH
    qm_col = (kv_col0 + 4 * NSA_KV) // MEM_WIDTH
    zm_col = qm_col + 1
    n_mem = mem.shape[0] // batch
    mem_kv, = _norm_matmul(mem, mem_norm_g, w_mem_kv.astype(BF16), [(BF16, [(0, 2 * MEM_WIDTH)])])

    k_cmp, v_cmp_t = _compress(nf, kc_col, vc_col, pe_k, pe_v, phik_w1, phik_w2, phiv_w1, phiv_w2,
                               batch, seq)
    ym = _mem_attn(nb, qm_col, mem_kv, nb, zm_col, batch, seq, n_mem)
    return _nsa_attn(nb, kv_col0, z_col, k_cmp, v_cmp_t, nf, gl_col, ym, w_out.astype(BF16), x, final_g,
                     batch, seq)


def kernel(x, mem, hawk_norm, hawk_w_in, hawk_conv_w, hawk_conv_b, hawk_gate_a_w, hawk_gate_a_b,
           hawk_gate_x_w, hawk_gate_x_b, hawk_lambda, hawk_mem_norm, hawk_w_mem_kv, hawk_w_out,
           nsa_norm, nsa_w_in, nsa_pe_k, nsa_pe_v, nsa_phi_k_w1, nsa_phi_k_w2, nsa_phi_v_w1,
           nsa_phi_v_w2, nsa_mem_norm, nsa_w_mem_kv, nsa_w_out, final_norm):
    batch, seq, d = x.shape
    assert hawk_norm.shape[0] == 1 and nsa_norm.shape[0] == 1, "one layer of each kind"
    assert seq % (ATTN_BLOCK * DIL_GROUPS[-1][1]) == 0
    x2 = x.reshape(batch * seq, d)
    mem2 = mem.reshape(batch * mem.shape[1], d)
    x2 = _hawk_layer(x2, mem2, batch, seq, hawk_norm[0], hawk_w_in[0], hawk_conv_w[0], hawk_conv_b[0],
                     hawk_gate_a_w[0], hawk_gate_a_b[0].reshape(-1), hawk_gate_x_w[0],
                     hawk_gate_x_b[0].reshape(-1), hawk_lambda[0], hawk_mem_norm[0], hawk_w_mem_kv[0],
                     hawk_w_out[0])
    out = _nsa_layer(x2, mem2, batch, seq, nsa_norm[0], nsa_w_in[0], nsa_pe_k[0], nsa_pe_v[0],
                     nsa_phi_k_w1[0], nsa_phi_k_w2[0], nsa_phi_v_w1[0], nsa_phi_v_w2[0],
                     nsa_mem_norm[0], nsa_w_mem_kv[0], nsa_w_out[0], final_norm)
    return out.reshape(batch, seq, d)
```

```python
import functools

import numpy as np
import jax
import jax.numpy as jnp
from jax import lax
from jax.experimental import pallas as pl
from jax.experimental.pallas import tpu as pltpu

F32 = jnp.float32
BF16 = jnp.bfloat16

NORM_EPS = 1e-6
NEG_INF = -1e30
LOG2_E = 1.4426950408889634
LANES = 128
SUBLANES = 8
BF16_SUBLANES = 16
ATTN_BLOCK = 128
VMEM_LIMIT = 56 * 1024 * 1024

LRU_WIDTH = 1024
LRU_BLOCKS = 16
LRU_BLOCK_DIM = LRU_WIDTH // LRU_BLOCKS
LRU_PACK = 256
CONV_WIDTH = 4
LRU_C = 8.0

DIL_GROUPS = ((128, 1), (512, 4), (2048, 16))
DIL_HEADS = 4
DIL_HEAD_DIM = 128
DIL_WIDTH = DIL_HEADS * DIL_HEAD_DIM
DIL_QKV = len(DIL_GROUPS) * DIL_WIDTH
LSE_LANES = LANES // DIL_HEADS

MEM_HEADS = 4
MEM_HEAD_DIM = 64
MEM_WIDTH = MEM_HEADS * MEM_HEAD_DIM

NSA_HEADS = 16
NSA_KV_GROUPS = 2
NSA_R = NSA_HEADS // NSA_KV_GROUPS
NSA_HEAD_DIM = 64
NSA_WIDTH = NSA_HEADS * NSA_HEAD_DIM
NSA_KV = NSA_KV_GROUPS * NSA_HEAD_DIM
CMP_BLOCK = 32
CMP_STRIDE = 16
SLC_BLOCK = 64
SLC_TOP_N = 8
WIN_SIZE = 512
PHI_HIDDEN = 256
SEL_FORCE = 1e6
CMP_PER_SLC = SLC_BLOCK // CMP_STRIDE


def _alibi_slopes(n):
    return [float(v) for v in np.exp2(-8.0 * np.arange(1, n + 1) / n).astype(np.float32)]


def _params(*semantics):
    return pltpu.CompilerParams(dimension_semantics=semantics, vmem_limit_bytes=VMEM_LIMIT)


def _silu(z):
    return z * jax.nn.sigmoid(z)


def _dot_t(a, b):
    return lax.dot_general(a, b, (((1,), (1,)), ((), ())), preferred_element_type=F32)


def _rms_norm_rows(x, g):
    ms = jnp.mean(x * x, axis=-1, keepdims=True)
    return (x * lax.rsqrt(ms + NORM_EPS) * g).astype(BF16)


def _norm_matmul_kernel(*refs, dil, pieces):
    def project(xn, out_ref, w_refs, by_class=False):
        col = 0
        for w_ref in w_refs:
            width = w_ref.shape[1]
            res = jnp.dot(xn, w_ref[...], preferred_element_type=F32).astype(out_ref.dtype)
            if by_class:
                per = xn.shape[0] // dil
                for c in range(dil):
                    out_ref[c, :, col:col + width] = res[c * per:(c + 1) * per]
            else:
                out_ref[:, col:col + width] = res
            col += width

    if dil == 1:
        x_ref, g_ref = refs[:2]
        w_refs, o_refs = refs[2:2 + sum(pieces)], refs[2 + sum(pieces):]
        xn = _rms_norm_rows(x_ref[...], g_ref[...])
        first = 0
        for o_ref, n_pieces in zip(o_refs, pieces):
            project(xn, o_ref, w_refs[first:first + n_pieces])
            first += n_pieces
        return
    x_ref, g_ref, *w_refs, o_ref = refs
    tm, k = x_ref.shape
    x = x_ref[...]
    xn = x * lax.rsqrt(jnp.mean(x * x, axis=-1, keepdims=True) + NORM_EPS) * g_ref[...]
    xn = jnp.swapaxes(xn.reshape(tm // dil, dil, k), 0, 1).reshape(tm, k).astype(BF16)
    project(xn, o_ref, w_refs, by_class=True)


def _norm_matmul(x, g, w, outs, dil=1, tm=1024):
    m, k = x.shape
    tm = min(tm, m)
    assert m % tm == 0 and k % LANES == 0
    pieces = [len(cols) for _, cols in outs]
    widths = [sum(width for _, width in cols) for _, cols in outs]
    w_specs = []
    for _, cols in outs:
        for col, width in cols:
            assert col % width == 0 and width % LANES == 0
            w_specs.append(pl.BlockSpec((k, width), functools.partial(lambda i, j: (0, j), j=col // width)))
    resident = [pl.BlockSpec((1, k), lambda i: (0, 0))] + w_specs
    operands = (g.reshape(1, k),) + (w,) * len(w_specs)
    kernel = functools.partial(_norm_matmul_kernel, dil=dil, pieces=pieces)
    if dil == 1:
        return pl.pallas_call(
            kernel,
            grid=(m // tm,),
            in_specs=[pl.BlockSpec((tm, k), lambda i: (i, 0))] + resident,
            out_specs=[pl.BlockSpec((tm, width), lambda i: (i, 0)) for width in widths],
            out_shape=[jax.ShapeDtypeStruct((m, width), dtype) for width, (dtype, _) in zip(widths, outs)],
            compiler_params=_params("arbitrary"),
            name="norm_matmul",
        )(x, *operands)
    per = tm // dil
    (out_dtype, _), = outs
    n, = widths
    assert tm % dil == 0 and per % BF16_SUBLANES == 0
    return pl.pallas_call(
        kernel,
        grid=(m // tm,),
        in_specs=[pl.BlockSpec((tm, k), lambda i: (i, 0))] + resident,
        out_specs=pl.BlockSpec((dil, per, n), lambda i: (0, i, 0)),
        out_shape=jax.ShapeDtypeStruct((dil, m // dil, n), out_dtype),
        compiler_params=_params("arbitrary"),
        name="norm_matmul_dil",
    )(x, *operands)


def _rglru_kernel(xa_ref, za_ref, cw_ref, cb_ref, wa_ref, ba_ref, wx_ref, bx_ref, lam_ref,
                  o_ref, xpad_ref, h_ref):
    t = pl.program_id(1)
    tt, width = xa_ref.shape
    halo = SUBLANES

    @pl.when(t == 0)
    def _():
        xpad_ref[0:halo, :] = jnp.zeros((halo, width), F32)
        h_ref[...] = jnp.zeros_like(h_ref)

    x = xa_ref[...]
    xpad_ref[halo:halo + tt, :] = x
    cw = cw_ref[...]
    y = cw[CONV_WIDTH - 1:CONV_WIDTH] * x
    for k in range(1, CONV_WIDTH):
        y = y + cw[CONV_WIDTH - 1 - k:CONV_WIDTH - k] * xpad_ref[halo - k:halo - k + tt, :]
    y = y + cb_ref[...]
    xpad_ref[0:halo, :] = x[tt - halo:tt, :]

    yb = y.astype(BF16)
    r_parts, i_parts = [], []
    for p in range(width // LRU_PACK):
        ys = yb[:, p * LRU_PACK:(p + 1) * LRU_PACK]
        r_parts.append(jnp.dot(ys, wa_ref[p], preferred_element_type=F32))
        i_parts.append(jnp.dot(ys, wx_ref[p], preferred_element_type=F32))
    r = jax.nn.sigmoid(jnp.concatenate(r_parts, axis=1) + ba_ref[...])
    gi = jax.nn.sigmoid(jnp.concatenate(i_parts, axis=1) + bx_ref[...])

    nl = -lam_ref[...]
    softplus = jnp.maximum(nl, 0.0) + jnp.log1p(jnp.exp(-jnp.abs(nl)))
    log_a = (-LRU_C) * r * softplus
    a = jnp.exp(log_a)
    w = -jnp.tanh(log_a) * (a * a + 1.0)
    mult = jnp.where(w > 0.0, w * lax.rsqrt(w), 0.0)
    b = y * gi * mult
    first = (lax.broadcasted_iota(jnp.int32, (SUBLANES, width), 0) == 0) & (t == 0)
    b = jnp.concatenate([jnp.where(first, (y * gi)[0:SUBLANES], b[0:SUBLANES]), b[SUBLANES:]], axis=0)

    blocks = width // LANES

    def time_major(v):
        cols = jnp.stack([v[:, j * LANES:(j + 1) * LANES] for j in range(blocks)], axis=0)
        return jnp.swapaxes(cols, 0, 1)

    a_t, b_t = time_major(a), time_major(b)
    state = h_ref[...].reshape(blocks, LANES)
    steps = []
    for step in range(tt):
        state = a_t[step] * state + b_t[step]
        steps.append(state)
    h_ref[...] = state.reshape(1, width)
    h_cols = jnp.swapaxes(jnp.stack(steps, axis=0), 0, 1)
    h = jnp.concatenate([h_cols[j] for j in range(blocks)], axis=1)
    o_ref[...] = (h * _silu(za_ref[...].astype(F32))).astype(o_ref.dtype)


def _rglru(xa, za_src, za_col, conv_w, conv_b, wa, ba, wx, bx, lam, batch, seq, tt=1024):
    width = LRU_WIDTH
    nt = seq // tt
    packs = width // LRU_PACK
    vec = pl.BlockSpec((1, width), lambda b, t: (0, 0))
    gate_w = pl.BlockSpec((packs, LRU_PACK, LRU_PACK), lambda b, t: (0, 0, 0))
    return pl.pallas_call(
        _rglru_kernel,
        grid=(batch, nt),
        in_specs=[pl.BlockSpec((tt, width), lambda b, t: (b * nt + t, 0)),
                  pl.BlockSpec((tt, width), lambda b, t: (b * nt + t, za_col)),
                  pl.BlockSpec((CONV_WIDTH, width), lambda b, t: (0, 0)),
                  vec, gate_w, vec, gate_w, vec, vec],
        out_specs=pl.BlockSpec((tt, width), lambda b, t: (b * nt + t, 0)),
        out_shape=jax.ShapeDtypeStruct((batch * seq, width), BF16),
        scratch_shapes=[pltpu.VMEM((tt + SUBLANES, width), F32), pltpu.VMEM((1, width), F32)],
        compiler_params=_params("arbitrary", "arbitrary"),
        name="rglru",
    )(xa, za_src, conv_w, conv_b.reshape(1, width), wa, ba.reshape(1, width), wx, bx.reshape(1, width),
      lam.reshape(1, width))


def _pack_block_diag(w):
    per = LRU_PACK // LRU_BLOCK_DIM
    w = w.reshape(LRU_BLOCKS // per, per, LRU_BLOCK_DIM, LRU_BLOCK_DIM)
    eye = jnp.eye(per, dtype=w.dtype)
    packed = w[:, :, :, None, :] * eye[None, :, None, :, None]
    return packed.reshape(LRU_BLOCKS // per, LRU_PACK, LRU_PACK).astype(BF16)


def _dil_attn_kernel(*refs, slopes, pos_scale, max_dist, has_halo, dil, n_cls, n_blk):
    if has_halo:
        q_ref, kh_ref, k_ref, vh_ref, v_ref = refs[:5]
        out_refs = refs[5:]
    else:
        q_ref, k_ref, v_ref = refs[:3]
        out_refs = refs[3:]
    n_out = DIL_HEADS + 1
    dst_refs = out_refs[:n_out]
    stage_refs = out_refs[n_out:] if dil > 1 else dst_refs
    first_super = pl.program_id(1) == 0
    cls0 = pl.program_id(2) * n_cls
    blk = ATTN_BLOCK
    scale = DIL_HEAD_DIM ** -0.5

    def band(width, halo_live):
        row = lax.broadcasted_iota(jnp.int32, (blk, width), 0)
        col = lax.broadcasted_iota(jnp.int32, (blk, width), 1)
        dist = (width - blk) + row - col
        valid = (dist >= 0) & (dist <= max_dist)
        if halo_live is not None:
            valid = valid & ((col >= blk) | halo_live)
        distf = (dist * pos_scale).astype(F32)
        return [jnp.where(valid, (-slope / scale) * distf, NEG_INF) for slope in slopes]

    bias_inner = band(2 * blk, None) if n_blk > 1 else None
    bias_first = band(2 * blk, jnp.logical_not(first_super)) if has_halo else band(blk, None)

    def scores(cc, jb):
        cur = slice(jb * blk, (jb + 1) * blk)
        bias = bias_inner if jb > 0 else bias_first
        out = []
        for h in range(DIL_HEADS):
            hs = slice(h * DIL_HEAD_DIM, (h + 1) * DIL_HEAD_DIM)
            q = q_ref[cc, cur, hs]
            if jb > 0:
                k = k_ref[cc, (jb - 1) * blk:(jb + 1) * blk, hs]
                v = v_ref[cc, (jb - 1) * blk:(jb + 1) * blk, hs]
            elif has_halo:
                k = jnp.concatenate([kh_ref[cc, :, hs], k_ref[cc, cur, hs]], axis=0)
                v = jnp.concatenate([vh_ref[cc, :, hs], v_ref[cc, cur, hs]], axis=0)
            else:
                k, v = k_ref[cc, cur, hs], v_ref[cc, cur, hs]
            out.append((_dot_t(q, k) + bias[h], v))
        return out

    def finish(cc, jb, pairs):
        where = (cls0 + cc, slice(jb * blk, (jb + 1) * blk)) if dil > 1 else (slice(jb * blk, (jb + 1) * blk),)
        lses = []
        for h, (s, v) in enumerate(pairs):
            m = jnp.max(s, axis=-1, keepdims=True)
            e = jnp.exp2((s - m) * (scale * LOG2_E))
            den = jnp.sum(e, axis=-1, keepdims=True)
            o = jnp.dot(e.astype(BF16), v, preferred_element_type=F32) / den
            stage_refs[h][where] = o.astype(stage_refs[h].dtype)
            lses.append(jnp.broadcast_to(m * scale + jnp.log(den), (blk, LSE_LANES)))
        stage_refs[DIL_HEADS][where] = jnp.concatenate(lses, axis=1)

    pending = None
    for cc in range(n_cls):
        for jb in range(n_blk):
            pairs = scores(cc, jb)
            if pending is not None:
                finish(*pending)
            pending = (cc, jb, pairs)
    finish(*pending)

    if dil > 1:
        @pl.when(pl.program_id(2) == pl.num_programs(2) - 1)
        def _():
            for stage, dst in zip(stage_refs, dst_refs):
                dst[...] = jnp.swapaxes(stage[...], 0, 1).reshape(dst.shape).astype(dst.dtype)


def _dil_attn(qkv, col0, gi, batch, seq, work=8):
    window, dil = DIL_GROUPS[gi]
    sub = seq // dil
    nb = sub // ATTN_BLOCK
    n_blk = min(work, nb)
    n_cls = min(work // n_blk, dil)
    n_super = nb // n_blk
    has_halo = n_super > 1
    span = n_blk * ATTN_BLOCK
    slopes = _alibi_slopes(len(DIL_GROUPS) * DIL_HEADS)[gi * DIL_HEADS:(gi + 1) * DIL_HEADS]
    cur = lambda col: pl.BlockSpec((n_cls, span, DIL_WIDTH), lambda b, i, c: (c, b * n_super + i, col0 + col))
    halo = lambda col: pl.BlockSpec(
        (n_cls, ATTN_BLOCK, DIL_WIDTH),
        lambda b, i, c: (c, jnp.maximum((b * n_super + i) * n_blk - 1, 0), col0 + col))
    if has_halo:
        in_specs = [cur(0), halo(1), cur(1), halo(2), cur(2)]
    else:
        in_specs = [cur(0), cur(1), cur(2)]
    n_out = DIL_HEADS + 1
    *o, lse = pl.pallas_call(
        functools.partial(_dil_attn_kernel, slopes=slopes, pos_scale=dil, max_dist=window // dil,
                          has_halo=has_halo, dil=dil, n_cls=n_cls, n_blk=n_blk),
        grid=(batch, n_super, dil // n_cls),
        in_specs=in_specs,
        out_specs=[pl.BlockSpec((span * dil, LANES), lambda b, i, c: (b * n_super + i, 0))] * n_out,
        out_shape=[jax.ShapeDtypeStruct((batch * seq, LANES), BF16)] * DIL_HEADS
                  + [jax.ShapeDtypeStruct((batch * seq, LANES), F32)],
        scratch_shapes=[pltpu.VMEM((dil, span, LANES), F32)] * (n_out if dil > 1 else 0),
        compiler_params=_params("arbitrary", "arbitrary", "arbitrary"),
        name=f"dil_attn_d{dil}",
    )(*([qkv] * len(in_specs)))
    return o, lse


def _mem_attn_kernel(q_ref, k_ref, v_ref, z_ref, o_ref, ks_ref, vt_ref):
    hd = MEM_HEAD_DIM
    n_mem = k_ref.shape[0]

    @pl.when(pl.program_id(1) == 0)
    def _():
        ks_ref[...] = (k_ref[...].astype(F32) * (hd ** -0.5)).astype(BF16)
        vt = v_ref[...].astype(F32).T.astype(BF16)
        for h in range(MEM_HEADS):
            vt_ref[h, 0:hd, :] = vt[h * hd:(h + 1) * hd, :]
            vt_ref[h, hd:, :] = jnp.ones((vt_ref.shape[1] - hd, n_mem), BF16)

    heads = [slice(h * hd, (h + 1) * hd) for h in range(MEM_HEADS)]
    scores = [_dot_t(ks_ref[:, hs], q_ref[:, hs]) for hs in heads]
    outs = []
    for h, s in enumerate(scores):
        e = jnp.exp(s - jnp.max(s, axis=0, keepdims=True)).astype(BF16)
        acc = jnp.dot(vt_ref[h], e, preferred_element_type=F32)
        outs.append(acc[0:hd, :] / acc[hd:hd + 1, :])
    o = jnp.concatenate(outs, axis=0).T
    o_ref[...] = (o * _silu(z_ref[...].astype(F32))).astype(o_ref.dtype)


def _mem_attn(qsrc, q_col, kv, zsrc, z_col, batch, seq, n_mem, tq=1024):
    nq = seq // tq
    return pl.pallas_call(
        _mem_attn_kernel,
        grid=(batch, nq),
        in_specs=[pl.BlockSpec((tq, MEM_WIDTH), lambda b, i: (b * nq + i, q_col)),
                  pl.BlockSpec((n_mem, MEM_WIDTH), lambda b, i: (b, 0)),
                  pl.BlockSpec((n_mem, MEM_WIDTH), lambda b, i: (b, 1)),
                  pl.BlockSpec((tq, MEM_WIDTH), lambda b, i: (b * nq + i, z_col))],
        out_specs=pl.BlockSpec((tq, MEM_WIDTH), lambda b, i: (b * nq + i, 0)),
        out_shape=jax.ShapeDtypeStruct((batch * seq, MEM_WIDTH), BF16),
        scratch_shapes=[pltpu.VMEM((n_mem, MEM_WIDTH), BF16),
                        pltpu.VMEM((MEM_HEADS, MEM_HEAD_DIM + BF16_SUBLANES, n_mem), BF16)],
        compiler_params=_params("arbitrary", "arbitrary"),
        name="mem_attn",
    )(qsrc, kv, kv, zsrc)


def _hawk_out_kernel(*refs):
    n_groups = len(DIL_GROUPS)
    ya_ref = refs[0]
    o_refs = refs[1:1 + n_groups * DIL_HEADS]
    l_refs = refs[1 + n_groups * DIL_HEADS:1 + n_groups * (DIL_HEADS + 1)]
    zb_ref, ym_ref, w_ref, x_ref, out_ref = refs[1 + n_groups * (DIL_HEADS + 1):]
    a_end = LRU_WIDTH
    b_end = a_end + DIL_WIDTH
    y = jnp.dot(ya_ref[...], w_ref[0:a_end, :], preferred_element_type=F32)
    y = y + jnp.dot(ym_ref[...], w_ref[b_end:b_end + MEM_WIDTH, :], preferred_element_type=F32)
    parts = []
    for h in range(DIL_HEADS):
        ls = [l[:, h * LSE_LANES:h * LSE_LANES + 1] for l in l_refs]
        m = functools.reduce(jnp.maximum, ls)
        ws = [jnp.exp(l - m) for l in ls]
        num = sum(w * o_refs[gi * DIL_HEADS + h][...].astype(F32) for gi, w in enumerate(ws))
        parts.append(num / sum(ws))
    yb = (jnp.concatenate(parts, axis=1) * _silu(zb_ref[...].astype(F32))).astype(BF16)
    y = y + jnp.dot(yb, w_ref[a_end:b_end, :], preferred_element_type=F32)
    out_ref[...] = x_ref[...] + y


def _hawk_out(ya, os_, ls_, zb_src, zb_col, ym, w, x, tm=1024):
    m, d = x.shape
    row = lambda width, col=0: pl.BlockSpec((tm, width), lambda i: (i, col))
    heads = [o for group in os_ for o in group]
    return pl.pallas_call(
        _hawk_out_kernel,
        grid=(m // tm,),
        in_specs=[row(LRU_WIDTH)] + [row(DIL_HEAD_DIM)] * len(heads) + [row(LANES)] * len(ls_)
                 + [row(DIL_WIDTH, zb_col), row(MEM_WIDTH),
                    pl.BlockSpec(w.shape, lambda i: (0, 0)), row(d)],
        out_specs=row(d),
        out_shape=jax.ShapeDtypeStruct((m, d), F32),
        compiler_params=_params("arbitrary"),
        name="hawk_out",
    )(ya, *heads, *ls_, zb_src, ym, w, x)


def _compress_kernel(k_ref, v_ref, pe_ref, w1_ref, w2k_ref, w2vt_ref, ko_ref, vto_ref):
    n_blk = k_ref.shape[0] // CMP_STRIDE

    def hidden(which, src_ref):
        x = jnp.concatenate([src_ref[pl.ds(p, n_blk, stride=CMP_STRIDE), :] for p in range(CMP_STRIDE)],
                            axis=1).astype(BF16)
        first = jnp.dot(x, w1_ref[which, 0], preferred_element_type=F32)
        second = jnp.dot(x, w1_ref[which, 1], preferred_element_type=F32)
        pe = (jnp.dot(pe_ref[which, 0], w1_ref[which, 0], preferred_element_type=F32)
              + jnp.dot(pe_ref[which, 1], w1_ref[which, 1], preferred_element_type=F32))
        return _silu(first + pltpu.roll(second, n_blk - 1, axis=0) + pe[0:1, :]).astype(BF16)

    act_k, act_v = hidden(0, k_ref), hidden(1, v_ref)
    part = lambda a, g: a[:, g * PHI_HIDDEN:(g + 1) * PHI_HIDDEN]
    ks = [jnp.dot(part(act_k, g), w2k_ref[...], preferred_element_type=F32) for g in range(NSA_KV_GROUPS)]
    vts = [_dot_t(w2vt_ref[...], part(act_v, g)) for g in range(NSA_KV_GROUPS)]
    ko_ref[...] = jnp.concatenate(ks, axis=1).astype(ko_ref.dtype)
    vto_ref[...] = jnp.concatenate(vts, axis=0).astype(vto_ref.dtype)


def _compress(src, k_col, v_col, pe_k, pe_v, k_w1, k_w2, v_w1, v_w2, batch, seq):
    half = CMP_BLOCK // 2
    assert half == CMP_STRIDE and NSA_KV == LANES
    n_blk = seq // CMP_STRIDE
    hd = NSA_HEAD_DIM
    w1 = jnp.stack([k_w1, v_w1]).reshape(2, 2, half, hd, PHI_HIDDEN).astype(BF16)
    zero = jnp.zeros_like(w1)
    per_group = [jnp.concatenate([w1 if g == col else zero for col in range(NSA_KV_GROUPS)], axis=-1)
                 for g in range(NSA_KV_GROUPS)]
    w1e = jnp.stack(per_group, axis=3).reshape(2, 2, half * NSA_KV, NSA_KV_GROUPS * PHI_HIDDEN)
    pe = jnp.stack([pe_k, pe_v]).reshape(2, 2, half, 1, hd)
    pe = jnp.broadcast_to(pe, (2, 2, half, NSA_KV_GROUPS, hd)).reshape(2, 2, 1, half * NSA_KV)
    pe = jnp.broadcast_to(pe, (2, 2, SUBLANES, half * NSA_KV)).astype(BF16)
    w2k = k_w2.astype(BF16)
    w2vt = v_w2.T.astype(BF16)
    whole = lambda a: pl.BlockSpec(a.shape, lambda b: (0,) * a.ndim)
    return pl.pallas_call(
        _compress_kernel,
        grid=(batch,),
        in_specs=[pl.BlockSpec((seq, LANES), lambda b: (b, k_col)),
                  pl.BlockSpec((seq, LANES), lambda b: (b, v_col)),
                  whole(pe), whole(w1e), whole(w2k), whole(w2vt)],
        out_specs=[pl.BlockSpec((None, n_blk, NSA_KV), lambda b: (b, 0, 0)),
                   pl.BlockSpec((None, NSA_KV, n_blk), lambda b: (b, 0, 0))],
        out_shape=[jax.ShapeDtypeStruct((batch, n_blk, NSA_KV), BF16),
                   jax.ShapeDtypeStruct((batch, NSA_KV, n_blk), BF16)],
        compiler_params=_params("arbitrary"),
        name="compress",
    )(src, src, pe, w1e, w2k, w2vt)


KEY_CHUNK = 256
NSA_TQ = 256
SLOPE_PIECES = 3
N_FEATS = 2 * SLOPE_PIECES
FEAT_LANES = 32
KEY_COLS = NSA_HEAD_DIM + 2 * FEAT_LANES
N_SLC = 32


def _slope_pieces(slope):
    rest = np.float32(slope)
    pieces = []
    for _ in range(SLOPE_PIECES):
        p = np.float32(np.asarray(rest).astype(BF16))
        pieces.append(float(p))
        rest = np.float32(rest - p)
    return pieces


def _lane_table(lane, values):
    out = jnp.zeros(lane.shape, F32)
    for idx, v in enumerate(values):
        out = jnp.where(lane == idx, v, out)
    return out


def _key_feats(pos_hi, pos_lo, lane):
    return jnp.where(lane < SLOPE_PIECES, pos_hi, jnp.where(lane < N_FEATS, pos_lo, 0)).astype(F32)


def _tile_heads(x):
    return jnp.concatenate([x] * NSA_R, axis=1)


def _chunk_loop(lo, hi, body, init, widths=(4, 2, 1)):
    carry, start = init, lo
    for idx, w in enumerate(widths):
        count = (hi - start) // w

        def step(p, cr, start=start, w=w):
            first = start + p * w
            return body([first + j for j in range(w)], cr)

        if idx == 0:
            carry = lax.fori_loop(0, count, step, carry)
        else:
            carry = lax.cond(count > 0, functools.partial(step, 0), lambda cr: cr, carry)
        start = start + count * w
    return carry


def _nsa_kernel(q_ref, kc_ref, vct_ref, ksrc_ref, vsrc_ref, kwsrc_ref, vwsrc_ref, feat_ref, hot_ref,
                gl_ref, z_ref, ym_ref, w_ref, x_ref, fin_ref, o_ref,
                ks_ref, vst_ref, kw_ref, vwt_ref, s_ref, acc_ref, imp_ref):
    i = pl.program_id(1)
    tq = q_ref.shape[0]
    hd = NSA_HEAD_DIM
    n_cmp = kc_ref.shape[0]

    @pl.when(i == 0)
    def _():
        for g in range(NSA_KV_GROUPS):
            gs = slice(g * hd, (g + 1) * hd)
            for dst, src, tail in ((ks_ref, ksrc_ref, hot_ref[...]),
                                   (kw_ref, kwsrc_ref, jnp.zeros(hot_ref.shape, BF16))):
                dst[:, g * KEY_COLS:g * KEY_COLS + hd] = src[:, gs]
                dst[:, g * KEY_COLS + hd:g * KEY_COLS + hd + FEAT_LANES] = feat_ref[...]
                dst[:, g * KEY_COLS + hd + FEAT_LANES:(g + 1) * KEY_COLS] = tail
        for c in range(vst_ref.shape[0]):
            rows = slice(c * KEY_CHUNK, (c + 1) * KEY_CHUNK)
            vst_ref[c] = vsrc_ref[rows, :].astype(F32).T.astype(BF16)
            vwt_ref[c] = vwsrc_ref[rows, :].astype(F32).T.astype(BF16)

    slopes_all = _alibi_slopes(NSA_HEADS)
    gates_t = jax.nn.sigmoid(gl_ref[...]).T
    feat_lane = lax.broadcasted_iota(jnp.int32, (tq, FEAT_LANES), 1)
    no_sel = jnp.zeros((NSA_R * tq, FEAT_LANES), BF16)
    key_row = lax.broadcasted_iota(jnp.int32, (KEY_CHUNK, tq), 0)
    t_pos = i * tq + lax.broadcasted_iota(jnp.int32, (KEY_CHUNK, tq), 1)
    ones_rows = jnp.ones((BF16_SUBLANES, KEY_CHUNK), BF16)
    win_lo = jnp.maximum(i * tq - (WIN_SIZE - 1), 0) // KEY_CHUNK
    chunks_hi = (i * tq + tq - 1) // KEY_CHUNK + 1

    groups = range(NSA_KV_GROUPS)
    q_win, q_slc, o_cmp = [], [], []
    for g in groups:
        slopes = slopes_all[g * NSA_R:(g + 1) * NSA_R]
        gs = slice(g * hd, (g + 1) * hd)
        q_parts = []
        for r in range(NSA_R):
            qr = q_ref[:, (g * NSA_R + r) * hd:(g * NSA_R + r + 1) * hd]
            feats = _lane_table(feat_lane, _slope_pieces(slopes[r]) * 2).astype(BF16)
            q_parts.append(jnp.concatenate([qr, feats], axis=1))
        q_feat = jnp.concatenate(q_parts, axis=0)
        q_aug = jnp.concatenate([q_feat, no_sel], axis=1)

        n_row = lax.broadcasted_iota(jnp.int32, (n_cmp, tq), 0)
        t_cmp = i * tq + lax.broadcasted_iota(jnp.int32, (n_cmp, tq), 1)
        visible = t_cmp >= n_row * CMP_STRIDE + (CMP_BLOCK - 1)
        cfeat_row = lax.broadcasted_iota(jnp.int32, (n_cmp, 2 * FEAT_LANES), 0)
        cfeat_lane = lax.broadcasted_iota(jnp.int32, (n_cmp, 2 * FEAT_LANES), 1)
        kc_feats = _key_feats(cfeat_row * CMP_STRIDE, 0, cfeat_lane)
        kc_aug = jnp.concatenate([kc_ref[:, gs], kc_feats.astype(BF16)], axis=1)
        s = _dot_t(kc_aug, q_aug) + _tile_heads(jnp.where(visible, 0.0, NEG_INF))
        m = jnp.max(s, axis=0, keepdims=True)
        e = jnp.exp(s - m)
        t_one = i * tq + lax.broadcasted_iota(jnp.int32, (1, NSA_R * tq), 1) % tq
        any_visible = t_one >= (CMP_BLOCK - 1)
        p = e * jnp.where(any_visible, 1.0 / jnp.sum(e, axis=0, keepdims=True), 0.0)
        o_cmp.append(jnp.dot(vct_ref[gs, :], p.astype(BF16), preferred_element_type=F32))
        p_sum = p[:, 0:tq]
        for r in range(1, NSA_R):
            p_sum = p_sum + p[:, r * tq:(r + 1) * tq]

        band = p_sum + pltpu.roll(p_sum, 1, axis=0)
        for k in range(1, CMP_PER_SLC):
            band = band + pltpu.roll(p_sum, n_cmp - k, axis=0)
        halves = []
        for half in range(tq // LANES):
            imp_ref[...] = band[:, half * LANES:(half + 1) * LANES]
            halves.append(imp_ref[pl.ds(0, N_SLC, stride=CMP_PER_SLC), :])
        imp = jnp.concatenate(halves, axis=1)
        blk_j = lax.broadcasted_iota(jnp.int32, (N_SLC, tq), 0)
        cur = (i * tq + lax.broadcasted_iota(jnp.int32, (N_SLC, tq), 1)) // SLC_BLOCK
        forced = (blk_j == 0) | (blk_j == cur) | (blk_j == cur - 1)
        v_imp = jnp.where(forced, SEL_FORCE, jnp.where(blk_j > cur, -SEL_FORCE, imp))
        rank = jnp.zeros((N_SLC, tq), F32)
        for other in range(N_SLC):
            row = v_imp[other:other + 1, :]
            ahead = (row > v_imp) | ((row == v_imp) & (blk_j > other))
            rank = rank + jnp.where(ahead, 1.0, 0.0)
        sel_bias = jnp.where(rank < SLC_TOP_N, 0.0, NEG_INF)

        padded = jnp.concatenate([sel_bias, jnp.zeros((LANES - N_SLC, tq), F32)], axis=0)
        sel_t = padded.T[:, 0:FEAT_LANES].astype(BF16)
        q_win.append(q_aug)
        q_slc.append(jnp.concatenate([q_feat, jnp.concatenate([sel_t] * NSA_R, axis=0)], axis=1))

    def attend(q_brs, k_ref, key_cols, vt_ref, lo, hi, masked_from, mask_fn):
        def scores(cs, m_run, masked):
            starts = [pl.multiple_of(c * KEY_CHUNK, KEY_CHUNK) for c in cs]
            scs = [[_dot_t(k_ref[pl.ds(start, KEY_CHUNK), g * key_cols:(g + 1) * key_cols], q_brs[g])
                    for g in groups] for start in starts]
            for c, start, sc in zip(cs, starts, scs):
                if masked:
                    bias = _tile_heads(jnp.where(mask_fn(t_pos - (start + key_row)), 0.0, NEG_INF))
                    sc = [x + bias for x in sc]
                for g in groups:
                    s_ref[g, c] = sc[g]
                m_run = tuple(jnp.maximum(m_run[g], jnp.max(sc[g], axis=0, keepdims=True)) for g in groups)
            return m_run

        def weighted(cs, carry):
            for c in cs:
                for g in groups:
                    e = jnp.exp(s_ref[g, c] - m_rows[g]).astype(BF16)
                    v_ext = jnp.concatenate([vt_ref[c, g * hd:(g + 1) * hd, :], ones_rows], axis=0)
                    acc_ref[g] += jnp.dot(v_ext, e, preferred_element_type=F32)
            return carry

        m_rows = tuple(jnp.full((1, NSA_R * tq), NEG_INF, F32) for _ in groups)
        m_rows = _chunk_loop(lo, masked_from, functools.partial(scores, masked=False), m_rows)
        m_rows = _chunk_loop(masked_from, hi, functools.partial(scores, masked=True), m_rows)
        acc_ref[...] = jnp.zeros(acc_ref.shape, F32)
        _chunk_loop(lo, hi, weighted, 0)
        return [acc_ref[g, 0:hd, :] / acc_ref[g, hd:hd + 1, :] for g in groups]

    o_slc = attend(q_slc, ks_ref, KEY_COLS, vst_ref, 0, chunks_hi, (i * tq) // KEY_CHUNK,
                   lambda dist: dist >= 0)
    o_win = attend(q_win, kw_ref, KEY_COLS, vwt_ref, win_lo, chunks_hi, win_lo,
                   lambda dist: (dist >= 0) & (dist <= WIN_SIZE - 1))

    y = jnp.dot(ym_ref[...], w_ref[NSA_WIDTH:NSA_WIDTH + MEM_WIDTH, :], preferred_element_type=F32)
    for g in groups:
        def gate(kind):
            base = g * NSA_R * 3 + kind
            return jnp.concatenate([gates_t[base + 3 * r:base + 3 * r + 1, :] for r in range(NSA_R)], axis=1)

        o = gate(0) * o_cmp[g] + gate(1) * o_slc[g] + gate(2) * o_win[g]
        pairs = []
        for r in range(0, NSA_R, 2):
            two = jnp.concatenate([o[:, r * tq:(r + 1) * tq], o[:, (r + 1) * tq:(r + 2) * tq]], axis=0)
            pairs.append(two.T)
        cs = slice(g * NSA_R * hd, (g + 1) * NSA_R * hd)
        yo = (jnp.concatenate(pairs, axis=1) * _silu(z_ref[:, cs].astype(F32))).astype(BF16)
        y = y + jnp.dot(yo, w_ref[cs, :], preferred_element_type=F32)

    x = x_ref[...] + y
    ms = jnp.mean(x * x, axis=-1, keepdims=True)
    o_ref[...] = x * lax.rsqrt(ms + NORM_EPS) * fin_ref[...]


def _key_pos_feats(seq):
    assert seq // SLC_BLOCK == N_SLC <= FEAT_LANES
    pos = np.arange(seq)
    feats = np.zeros((seq, FEAT_LANES), np.float32)
    feats[:, 0:SLOPE_PIECES] = ((pos // SLC_BLOCK) * SLC_BLOCK)[:, None]
    feats[:, SLOPE_PIECES:N_FEATS] = (pos % SLC_BLOCK)[:, None]
    onehot = (np.arange(FEAT_LANES)[None, :] == (pos // SLC_BLOCK)[:, None]).astype(np.float32)
    return jnp.asarray(feats, BF16), jnp.asarray(onehot, BF16)


def _nsa_attn(nb, kv_col0, z_col, k_cmp, v_cmp_t, nf, gl_col, ym, w_out, x, final_g, batch, seq):
    tq = NSA_TQ
    nq = seq // tq
    d = x.shape[1]
    feats, onehot = _key_pos_feats(seq)
    kv_blk = kv_col0 // NSA_KV
    seq_cols = lambda col: pl.BlockSpec((seq, NSA_KV), lambda b, i: (b, kv_blk + col))
    const = lambda a: pl.BlockSpec(a.shape, lambda b, i: (0,) * a.ndim)
    per_batch = lambda a: pl.BlockSpec((None,) + a.shape[1:], lambda b, i: (b,) + (0,) * (a.ndim - 1))
    rows = lambda width, col=0: pl.BlockSpec((tq, width), lambda b, i: (b * nq + i, col))
    fin = final_g.reshape(1, d)
    return pl.pallas_call(
        _nsa_kernel,
        grid=(batch, nq),
        in_specs=[rows(NSA_WIDTH), per_batch(k_cmp), per_batch(v_cmp_t),
                  seq_cols(0), seq_cols(1), seq_cols(2), seq_cols(3), const(feats), const(onehot),
                  rows(LANES, gl_col), rows(NSA_WIDTH, z_col), rows(MEM_WIDTH), const(w_out), rows(d),
                  const(fin)],
        out_specs=rows(d),
        out_shape=jax.ShapeDtypeStruct((batch * seq, d), F32),
        scratch_shapes=[pltpu.VMEM((seq, NSA_KV_GROUPS * KEY_COLS), BF16),
                        pltpu.VMEM((seq // KEY_CHUNK, NSA_KV, KEY_CHUNK), BF16),
                        pltpu.VMEM((seq, NSA_KV_GROUPS * KEY_COLS), BF16),
                        pltpu.VMEM((seq // KEY_CHUNK, NSA_KV, KEY_CHUNK), BF16),
                        pltpu.VMEM((NSA_KV_GROUPS, seq // KEY_CHUNK, KEY_CHUNK, NSA_R * tq), F32),
                        pltpu.VMEM((NSA_KV_GROUPS, NSA_HEAD_DIM + BF16_SUBLANES, NSA_R * tq), F32),
                        pltpu.VMEM((seq // CMP_STRIDE, LANES), F32)],
        compiler_params=_params("arbitrary", "arbitrary"),
        name="nsa_attn",
    )(nb, k_cmp, v_cmp_t, nb, nb, nb, nb, feats, onehot, nf, nb, ym, w_out, x, fin)


def _hawk_layer(x, mem, batch, seq, norm_g, w_in, conv_w, conv_b, ga_w, ga_b, gx_w, gx_b, lam,
                mem_norm_g, w_mem_kv, w_out):
    xa0, za0 = 0, LRU_WIDTH
    q0 = 2 * LRU_WIDTH
    k0, v0 = q0 + DIL_QKV, q0 + 2 * DIL_QKV
    zb0 = q0 + 3 * DIL_QKV
    qm0 = zb0 + DIL_WIDTH
    zm0 = qm0 + MEM_WIDTH

    def qkv_cols(gi):
        return [(base + gi * DIL_WIDTH, DIL_WIDTH) for base in (q0, k0, v0)]

    w = w_in.astype(BF16)
    nat_cols = [(za0, LRU_WIDTH), *qkv_cols(0), (zb0, DIL_WIDTH), (qm0, MEM_WIDTH), (zm0, MEM_WIDTH)]
    xa, hb = _norm_matmul(x, norm_g, w, [(F32, [(xa0, LRU_WIDTH)]), (BF16, nat_cols)])
    za_col = 0
    qkv0_col = LRU_WIDTH // DIL_WIDTH
    zb_col = (LRU_WIDTH + 3 * DIL_WIDTH) // DIL_WIDTH
    qm_col = (LRU_WIDTH + 4 * DIL_WIDTH) // MEM_WIDTH
    zm_col = qm_col + 1
    qkv = [(hb[None], qkv0_col)]
    for gi in range(1, len(DIL_GROUPS)):
        qkv.append((_norm_matmul(x, norm_g, w, [(BF16, qkv_cols(gi))], dil=DIL_GROUPS[gi][1]), 0))
    n_mem = mem.shape[0] // batch
    mem_kv, = _norm_matmul(mem, mem_norm_g, w_mem_kv.astype(BF16), [(BF16, [(0, 2 * MEM_WIDTH)])])

    ya = _rglru(xa, hb, za_col, conv_w, conv_b, _pack_block_diag(ga_w), ga_b, _pack_block_diag(gx_w), gx_b,
                lam, batch, seq)
    attn = [_dil_attn(arr, col0, gi, batch, seq) for gi, (arr, col0) in enumerate(qkv)]
    ym = _mem_attn(hb, qm_col, mem_kv, hb, zm_col, batch, seq, n_mem)
    return _hawk_out(ya, [o for o, _ in attn], [l for _, l in attn], hb, zb_col, ym,
                     w_out.astype(BF16), x)


def _nsa_layer(x, mem, batch, seq, norm_g, w_in, pe_k, pe_v, phik_w1, phik_w2, phiv_w1, phiv_w2,
               mem_norm_g, w_mem_kv, w_out, final_g):
    kv0 = NSA_WIDTH
    gl0 = kv0 + 6 * NSA_KV
    z0 = gl0 + 3 * NSA_HEADS
    qm0 = z0 + NSA_WIDTH
    zm0 = qm0 + MEM_WIDTH
    wb = w_in.astype(BF16)
    gl_w = jnp.pad(wb[:, gl0:z0], ((0, 0), (0, LANES - 3 * NSA_HEADS)))
    q_w = wb[:, 0:kv0] * jnp.asarray(NSA_HEAD_DIM ** -0.5, BF16)
    w_all = jnp.concatenate([q_w, wb[:, z0:qm0], wb[:, kv0 + 2 * NSA_KV:gl0], wb[:, qm0:zm0 + MEM_WIDTH],
                             gl_w, wb[:, kv0:kv0 + 2 * NSA_KV]], axis=1)
    f32_width = LANES + 2 * NSA_KV
    bf16_width = w_all.shape[1] - f32_width
    nb, nf = _norm_matmul(x, norm_g, w_all, [(BF16, [(0, bf16_width)]), (F32, [(bf16_width, f32_width)])])
    gl_col, kc_col, vc_col = 0, 1, 2
    z_col = 1
    kv_col0 = 2 * NSA_WIDTH
    qm_col = (kv_col0 + 4 * NSA_KV) // MEM_WIDTH
    zm_col = qm_col + 1
    n_mem = mem.shape[0] // batch
    mem_kv, = _norm_matmul(mem, mem_norm_g, w_mem_kv.astype(BF16), [(BF16, [(0, 2 * MEM_WIDTH)])])

    k_cmp, v_cmp_t = _compress(nf, kc_col, vc_col, pe_k, pe_v, phik_w1, phik_w2, phiv_w1, phiv_w2,
                               batch, seq)
    ym = _mem_attn(nb, qm_col, mem_kv, nb, zm_col, batch, seq, n_mem)
    return _nsa_attn(nb, kv_col0, z_col, k_cmp, v_cmp_t, nf, gl_col, ym, w_out.astype(BF16), x, final_g,
                     batch, seq)


def kernel(x, mem, hawk_norm, hawk_w_in, hawk_conv_w, hawk_conv_b, hawk_gate_a_w, hawk_gate_a_b,
           hawk_gate_x_w, hawk_gate_x_b, hawk_lambda, hawk_mem_norm, hawk_w_mem_kv, hawk_w_out,
           nsa_norm, nsa_w_in, nsa_pe_k, nsa_pe_v, nsa_phi_k_w1, nsa_phi_k_w2, nsa_phi_v_w1,
           nsa_phi_v_w2, nsa_mem_norm, nsa_w_mem_kv, nsa_w_out, final_norm):
    batch, seq, d = x.shape
    assert hawk_norm.shape[0] == 1 and nsa_norm.shape[0] == 1, "one layer of each kind"
    assert seq % (ATTN_BLOCK * DIL_GROUPS[-1][1]) == 0
    x2 = x.reshape(batch * seq, d)
    mem2 = mem.reshape(batch * mem.shape[1], d)
    x2 = _hawk_layer(x2, mem2, batch, seq, hawk_norm[0], hawk_w_in[0], hawk_conv_w[0], hawk_conv_b[0],
                     hawk_gate_a_w[0], hawk_gate_a_b[0].reshape(-1), hawk_gate_x_w[0],
                     hawk_gate_x_b[0].reshape(-1), hawk_lambda[0], hawk_mem_norm[0], hawk_w_mem_kv[0],
                     hawk_w_out[0])
    out = _nsa_layer(x2, mem2, batch, seq, nsa_norm[0], nsa_w_in[0], nsa_pe_k[0], nsa_pe_v[0],
                     nsa_phi_k_w1[0], nsa_phi_k_w2[0], nsa_phi_v_w1[0], nsa_phi_v_w2[0],
                     nsa_mem_norm[0], nsa_w_mem_kv[0], nsa_w_out[0], final_norm)
    return out.reshape(batch, seq, d)
```

```python
import functools

import numpy as np
import jax
import jax.numpy as jnp
from jax import lax
from jax.experimental import pallas as pl
from jax.experimental.pallas import tpu as pltpu

F32 = jnp.float32
BF16 = jnp.bfloat16

NORM_EPS = 1e-6
NEG_INF = -1e30
LOG2_E = 1.4426950408889634
LANES = 128
SUBLANES = 8
BF16_SUBLANES = 16
ATTN_BLOCK = 128
VMEM_LIMIT = 56 * 1024 * 1024

LRU_WIDTH = 1024
LRU_BLOCKS = 16
LRU_BLOCK_DIM = LRU_WIDTH // LRU_BLOCKS
LRU_PACK = 256
CONV_WIDTH = 4
LRU_C = 8.0

DIL_GROUPS = ((128, 1), (512, 4), (2048, 16))
DIL_HEADS = 4
DIL_HEAD_DIM = 128
DIL_WIDTH = DIL_HEADS * DIL_HEAD_DIM
DIL_QKV = len(DIL_GROUPS) * DIL_WIDTH
LSE_LANES = LANES // DIL_HEADS

MEM_HEADS = 4
MEM_HEAD_DIM = 64
MEM_WIDTH = MEM_HEADS * MEM_HEAD_DIM

NSA_HEADS = 16
NSA_KV_GROUPS = 2
NSA_R = NSA_HEADS // NSA_KV_GROUPS
NSA_HEAD_DIM = 64
NSA_WIDTH = NSA_HEADS * NSA_HEAD_DIM
NSA_KV = NSA_KV_GROUPS * NSA_HEAD_DIM
CMP_BLOCK = 32
CMP_STRIDE = 16
SLC_BLOCK = 64
SLC_TOP_N = 8
WIN_SIZE = 512
PHI_HIDDEN = 256
SEL_FORCE = 1e6
CMP_PER_SLC = SLC_BLOCK // CMP_STRIDE


def _alibi_slopes(n):
    return [float(v) for v in np.exp2(-8.0 * np.arange(1, n + 1) / n).astype(np.float32)]


def _params(*semantics):
    return pltpu.CompilerParams(dimension_semantics=semantics, vmem_limit_bytes=VMEM_LIMIT)


def _silu(z):
    return z * jax.nn.sigmoid(z)


def _dot_t(a, b):
    return lax.dot_general(a, b, (((1,), (1,)), ((), ())), preferred_element_type=F32)


def _rms_norm_rows(x, g):
    ms = jnp.mean(x * x, axis=-1, keepdims=True)
    return (x * lax.rsqrt(ms + NORM_EPS) * g).astype(BF16)


def _norm_matmul_kernel(*refs, dil, pieces):
    def project(xn, out_ref, w_refs, by_class=False):
        col = 0
        for w_ref in w_refs:
            width = w_ref.shape[1]
            res = jnp.dot(xn, w_ref[...], preferred_element_type=F32).astype(out_ref.dtype)
            if by_class:
                per = xn.shape[0] // dil
                for c in range(dil):
                    out_ref[c, :, col:col + width] = res[c * per:(c + 1) * per]
            else:
                out_ref[:, col:col + width] = res
            col += width

    if dil == 1:
        x_ref, g_ref = refs[:2]
        w_refs, o_refs = refs[2:2 + sum(pieces)], refs[2 + sum(pieces):]
        xn = _rms_norm_rows(x_ref[...], g_ref[...])
        first = 0
        for o_ref, n_pieces in zip(o_refs, pieces):
            project(xn, o_ref, w_refs[first:first + n_pieces])
            first += n_pieces
        return
    x_ref, g_ref, *w_refs, o_ref = refs
    tm, k = x_ref.shape
    x = x_ref[...]
    xn = x * lax.rsqrt(jnp.mean(x * x, axis=-1, keepdims=True) + NORM_EPS) * g_ref[...]
    xn = jnp.swapaxes(xn.reshape(tm // dil, dil, k), 0, 1).reshape(tm, k).astype(BF16)
    project(xn, o_ref, w_refs, by_class=True)


def _norm_matmul(x, g, w, outs, dil=1, tm=1024):
    m, k = x.shape
    tm = min(tm, m)
    assert m % tm == 0 and k % LANES == 0
    pieces = [len(cols) for _, cols in outs]
    widths = [sum(width for _, width in cols) for _, cols in outs]
    w_specs = []
    for _, cols in outs:
        for col, width in cols:
            assert col % width == 0 and width % LANES == 0
            w_specs.append(pl.BlockSpec((k, width), functools.partial(lambda i, j: (0, j), j=col // width)))
    resident = [pl.BlockSpec((1, k), lambda i: (0, 0))] + w_specs
    operands = (g.reshape(1, k),) + (w,) * len(w_specs)
    kernel = functools.partial(_norm_matmul_kernel, dil=dil, pieces=pieces)
    if dil == 1:
        return pl.pallas_call(
            kernel,
            grid=(m // tm,),
            in_specs=[pl.BlockSpec((tm, k), lambda i: (i, 0))] + resident,
            out_specs=[pl.BlockSpec((tm, width), lambda i: (i, 0)) for width in widths],
            out_shape=[jax.ShapeDtypeStruct((m, width), dtype) for width, (dtype, _) in zip(widths, outs)],
            compiler_params=_params("arbitrary"),
            name="norm_matmul",
        )(x, *operands)
    per = tm // dil
    (out_dtype, _), = outs
    n, = widths
    assert tm % dil == 0 and per % BF16_SUBLANES == 0
    return pl.pallas_call(
        kernel,
        grid=(m // tm,),
        in_specs=[pl.BlockSpec((tm, k), lambda i: (i, 0))] + resident,
        out_specs=pl.BlockSpec((dil, per, n), lambda i: (0, i, 0)),
        out_shape=jax.ShapeDtypeStruct((dil, m // dil, n), out_dtype),
        compiler_params=_params("arbitrary"),
        name="norm_matmul_dil",
    )(x, *operands)


def _rglru_kernel(xa_ref, za_ref, cw_ref, cb_ref, wa_ref, ba_ref, wx_ref, bx_ref, lam_ref,
                  o_ref, xpad_ref, h_ref):
    t = pl.program_id(1)
    tt, width = xa_ref.shape
    halo = SUBLANES

    @pl.when(t == 0)
    def _():
        xpad_ref[0:halo, :] = jnp.zeros((halo, width), F32)
        h_ref[...] = jnp.zeros_like(h_ref)

    x = xa_ref[...]
    xpad_ref[halo:halo + tt, :] = x
    cw = cw_ref[...]
    y = cw[CONV_WIDTH - 1:CONV_WIDTH] * x
    for k in range(1, CONV_WIDTH):
        y = y + cw[CONV_WIDTH - 1 - k:CONV_WIDTH - k] * xpad_ref[halo - k:halo - k + tt, :]
    y = y + cb_ref[...]
    xpad_ref[0:halo, :] = x[tt - halo:tt, :]

    yb = y.astype(BF16)
    r_parts, i_parts = [], []
    for p in range(width // LRU_PACK):
        ys = yb[:, p * LRU_PACK:(p + 1) * LRU_PACK]
        r_parts.append(jnp.dot(ys, wa_ref[p], preferred_element_type=F32))
        i_parts.append(jnp.dot(ys, wx_ref[p], preferred_element_type=F32))
    r = jax.nn.sigmoid(jnp.concatenate(r_parts, axis=1) + ba_ref[...])
    gi = jax.nn.sigmoid(jnp.concatenate(i_parts, axis=1) + bx_ref[...])

    nl = -lam_ref[...]
    softplus = jnp.maximum(nl, 0.0) + jnp.log1p(jnp.exp(-jnp.abs(nl)))
    log_a = (-LRU_C) * r * softplus
    a = jnp.exp(log_a)
    w = -jnp.tanh(log_a) * (a * a + 1.0)
    mult = jnp.where(w > 0.0, w * lax.rsqrt(w), 0.0)
    b = y * gi * mult
    first = (lax.broadcasted_iota(jnp.int32, (SUBLANES, width), 0) == 0) & (t == 0)
    b = jnp.concatenate([jnp.where(first, (y * gi)[0:SUBLANES], b[0:SUBLANES]), b[SUBLANES:]], axis=0)

    blocks = width // LANES

    def time_major(v):
        cols = jnp.stack([v[:, j * LANES:(j + 1) * LANES] for j in range(blocks)], axis=0)
        return jnp.swapaxes(cols, 0, 1)

    a_t, b_t = time_major(a), time_major(b)
    state = h_ref[...].reshape(blocks, LANES)
    steps = []
    for step in range(tt):
        state = a_t[step] * state + b_t[step]
        steps.append(state)
    h_ref[...] = state.reshape(1, width)
    h_cols = jnp.swapaxes(jnp.stack(steps, axis=0), 0, 1)
    h = jnp.concatenate([h_cols[j] for j in range(blocks)], axis=1)
    o_ref[...] = (h * _silu(za_ref[...].astype(F32))).astype(o_ref.dtype)


def _rglru(xa, za_src, za_col, conv_w, conv_b, wa, ba, wx, bx, lam, batch, seq, tt=1024):
    width = LRU_WIDTH
    nt = seq // tt
    packs = width // LRU_PACK
    vec = pl.BlockSpec((1, width), lambda b, t: (0, 0))
    gate_w = pl.BlockSpec((packs, LRU_PACK, LRU_PACK), lambda b, t: (0, 0, 0))
    return pl.pallas_call(
        _rglru_kernel,
        grid=(batch, nt),
        in_specs=[pl.BlockSpec((tt, width), lambda b, t: (b * nt + t, 0)),
                  pl.BlockSpec((tt, width), lambda b, t: (b * nt + t, za_col)),
                  pl.BlockSpec((CONV_WIDTH, width), lambda b, t: (0, 0)),
                  vec, gate_w, vec, gate_w, vec, vec],
        out_specs=pl.BlockSpec((tt, width), lambda b, t: (b * nt + t, 0)),
        out_shape=jax.ShapeDtypeStruct((batch * seq, width), BF16),
        scratch_shapes=[pltpu.VMEM((tt + SUBLANES, width), F32), pltpu.VMEM((1, width), F32)],
        compiler_params=_params("arbitrary", "arbitrary"),
        name="rglru",
    )(xa, za_src, conv_w, conv_b.reshape(1, width), wa, ba.reshape(1, width), wx, bx.reshape(1, width),
      lam.reshape(1, width))


def _pack_block_diag(w):
    per = LRU_PACK // LRU_BLOCK_DIM
    w = w.reshape(LRU_BLOCKS // per, per, LRU_BLOCK_DIM, LRU_BLOCK_DIM)
    eye = jnp.eye(per, dtype=w.dtype)
    packed = w[:, :, :, None, :] * eye[None, :, None, :, None]
    return packed.reshape(LRU_BLOCKS // per, LRU_PACK, LRU_PACK).astype(BF16)


def _dil_attn_kernel(*refs, slopes, pos_scale, max_dist, has_halo, dil, n_cls, n_blk):
    if has_halo:
        q_ref, kh_ref, k_ref, vh_ref, v_ref = refs[:5]
        out_refs = refs[5:]
    else:
        q_ref, k_ref, v_ref = refs[:3]
        out_refs = refs[3:]
    n_out = DIL_HEADS + 1
    dst_refs = out_refs[:n_out]
    stage_refs = out_refs[n_out:] if dil > 1 else dst_refs
    first_super = pl.program_id(1) == 0
    cls0 = pl.program_id(2) * n_cls
    blk = ATTN_BLOCK
    scale = DIL_HEAD_DIM ** -0.5

    def band(width, halo_live):
        row = lax.broadcasted_iota(jnp.int32, (blk, width), 0)
        col = lax.broadcasted_iota(jnp.int32, (blk, width), 1)
        dist = (width - blk) + row - col
        valid = (dist >= 0) & (dist <= max_dist)
        if halo_live is not None:
            valid = valid & ((col >= blk) | halo_live)
        distf = (dist * pos_scale).astype(F32)
        return [jnp.where(valid, (-slope / scale) * distf, NEG_INF) for slope in slopes]

    bias_inner = band(2 * blk, None) if n_blk > 1 else None
    bias_first = band(2 * blk, jnp.logical_not(first_super)) if has_halo else band(blk, None)

    def scores(cc, jb):
        cur = slice(jb * blk, (jb + 1) * blk)
        bias = bias_inner if jb > 0 else bias_first
        out = []
        for h in range(DIL_HEADS):
            hs = slice(h * DIL_HEAD_DIM, (h + 1) * DIL_HEAD_DIM)
            q = q_ref[cc, cur, hs]
            if jb > 0:
                k = k_ref[cc, (jb - 1) * blk:(jb + 1) * blk, hs]
                v = v_ref[cc, (jb - 1) * blk:(jb + 1) * blk, hs]
            elif has_halo:
                k = jnp.concatenate([kh_ref[cc, :, hs], k_ref[cc, cur, hs]], axis=0)
                v = jnp.concatenate([vh_ref[cc, :, hs], v_ref[cc, cur, hs]], axis=0)
            else:
                k, v = k_ref[cc, cur, hs], v_ref[cc, cur, hs]
            out.append((_dot_t(q, k) + bias[h], v))
        return out

    def finish(cc, jb, pairs):
        where = (cls0 + cc, slice(jb * blk, (jb + 1) * blk)) if dil > 1 else (slice(jb * blk, (jb + 1) * blk),)
        lses = []
        for h, (s, v) in enumerate(pairs):
            m = jnp.max(s, axis=-1, keepdims=True)
            e = jnp.exp2((s - m) * (scale * LOG2_E))
            den = jnp.sum(e, axis=-1, keepdims=True)
            o = jnp.dot(e.astype(BF16), v, preferred_element_type=F32) / den
            stage_refs[h][where] = o.astype(stage_refs[h].dtype)
            lses.append(jnp.broadcast_to(m * scale + jnp.log(den), (blk, LSE_LANES)))
        stage_refs[DIL_HEADS][where] = jnp.concatenate(lses, axis=1)

    pending = None
    for cc in range(n_cls):
        for jb in range(n_blk):
            pairs = scores(cc, jb)
            if pending is not None:
                finish(*pending)
            pending = (cc, jb, pairs)
    finish(*pending)

    if dil > 1:
        @pl.when(pl.program_id(2) == pl.num_programs(2) - 1)
        def _():
            for stage, dst in zip(stage_refs, dst_refs):
                dst[...] = jnp.swapaxes(stage[...], 0, 1).reshape(dst.shape).astype(dst.dtype)


def _dil_attn(qkv, col0, gi, batch, seq, work=8):
    window, dil = DIL_GROUPS[gi]
    sub = seq // dil
    nb = sub // ATTN_BLOCK
    n_blk = min(work, nb)
    n_cls = min(work // n_blk, dil)
    n_super = nb // n_blk
    has_halo = n_super > 1
    span = n_blk * ATTN_BLOCK
    slopes = _alibi_slopes(len(DIL_GROUPS) * DIL_HEADS)[gi * DIL_HEADS:(gi + 1) * DIL_HEADS]
    cur = lambda col: pl.BlockSpec((n_cls, span, DIL_WIDTH), lambda b, i, c: (c, b * n_super + i, col0 + col))
    halo = lambda col: pl.BlockSpec(
        (n_cls, ATTN_BLOCK, DIL_WIDTH),
        lambda b, i, c: (c, jnp.maximum((b * n_super + i) * n_blk - 1, 0), col0 + col))
    if has_halo:
        in_specs = [cur(0), halo(1), cur(1), halo(2), cur(2)]
    else:
        in_specs = [cur(0), cur(1), cur(2)]
    n_out = DIL_HEADS + 1
    *o, lse = pl.pallas_call(
        functools.partial(_dil_attn_kernel, slopes=slopes, pos_scale=dil, max_dist=window // dil,
                          has_halo=has_halo, dil=dil, n_cls=n_cls, n_blk=n_blk),
        grid=(batch, n_super, dil // n_cls),
        in_specs=in_specs,
        out_specs=[pl.BlockSpec((span * dil, LANES), lambda b, i, c: (b * n_super + i, 0))] * n_out,
        out_shape=[jax.ShapeDtypeStruct((batch * seq, LANES), BF16)] * DIL_HEADS
                  + [jax.ShapeDtypeStruct((batch * seq, LANES), F32)],
        scratch_shapes=[pltpu.VMEM((dil, span, LANES), F32)] * (n_out if dil > 1 else 0),
        compiler_params=_params("arbitrary", "arbitrary", "arbitrary"),
        name=f"dil_attn_d{dil}",
    )(*([qkv] * len(in_specs)))
    return o, lse


def _mem_attn_kernel(q_ref, k_ref, v_ref, z_ref, o_ref, ks_ref, vt_ref):
    hd = MEM_HEAD_DIM
    n_mem = k_ref.shape[0]

    @pl.when(pl.program_id(1) == 0)
    def _():
        ks_ref[...] = (k_ref[...].astype(F32) * (hd ** -0.5)).astype(BF16)
        vt = v_ref[...].astype(F32).T.astype(BF16)
        for h in range(MEM_HEADS):
            vt_ref[h, 0:hd, :] = vt[h * hd:(h + 1) * hd, :]
            vt_ref[h, hd:, :] = jnp.ones((vt_ref.shape[1] - hd, n_mem), BF16)

    heads = [slice(h * hd, (h + 1) * hd) for h in range(MEM_HEADS)]
    scores = [_dot_t(ks_ref[:, hs], q_ref[:, hs]) for hs in heads]
    outs = []
    for h, s in enumerate(scores):
        e = jnp.exp(s - jnp.max(s, axis=0, keepdims=True)).astype(BF16)
        acc = jnp.dot(vt_ref[h], e, preferred_element_type=F32)
        outs.append(acc[0:hd, :] / acc[hd:hd + 1, :])
    o = jnp.concatenate(outs, axis=0).T
    o_ref[...] = (o * _silu(z_ref[...].astype(F32))).astype(o_ref.dtype)


def _mem_attn(qsrc, q_col, kv, zsrc, z_col, batch, seq, n_mem, tq=1024):
    nq = seq // tq
    return pl.pallas_call(
        _mem_attn_kernel,
        grid=(batch, nq),
        in_specs=[pl.BlockSpec((tq, MEM_WIDTH), lambda b, i: (b * nq + i, q_col)),
                  pl.BlockSpec((n_mem, MEM_WIDTH), lambda b, i: (b, 0)),
                  pl.BlockSpec((n_mem, MEM_WIDTH), lambda b, i: (b, 1)),
                  pl.BlockSpec((tq, MEM_WIDTH), lambda b, i: (b * nq + i, z_col))],
        out_specs=pl.BlockSpec((tq, MEM_WIDTH), lambda b, i: (b * nq + i, 0)),
        out_shape=jax.ShapeDtypeStruct((batch * seq, MEM_WIDTH), BF16),
        scratch_shapes=[pltpu.VMEM((n_mem, MEM_WIDTH), BF16),
                        pltpu.VMEM((MEM_HEADS, MEM_HEAD_DIM + BF16_SUBLANES, n_mem), BF16)],
        compiler_params=_params("arbitrary", "arbitrary"),
        name="mem_attn",
    )(qsrc, kv, kv, zsrc)


def _hawk_out_kernel(*refs):
    n_groups = len(DIL_GROUPS)
    ya_ref = refs[0]
    o_refs = refs[1:1 + n_groups * DIL_HEADS]
    l_refs = refs[1 + n_groups * DIL_HEADS:1 + n_groups * (DIL_HEADS + 1)]
    zb_ref, ym_ref, w_ref, x_ref, out_ref = refs[1 + n_groups * (DIL_HEADS + 1):]
    a_end = LRU_WIDTH
    b_end = a_end + DIL_WIDTH
    y = jnp.dot(ya_ref[...], w_ref[0:a_end, :], preferred_element_type=F32)
    y = y + jnp.dot(ym_ref[...], w_ref[b_end:b_end + MEM_WIDTH, :], preferred_element_type=F32)
    parts = []
    for h in range(DIL_HEADS):
        ls = [l[:, h * LSE_LANES:h * LSE_LANES + 1] for l in l_refs]
        m = functools.reduce(jnp.maximum, ls)
        ws = [jnp.exp(l - m) for l in ls]
        num = sum(w * o_refs[gi * DIL_HEADS + h][...].astype(F32) for gi, w in enumerate(ws))
        parts.append(num / sum(ws))
    yb = (jnp.concatenate(parts, axis=1) * _silu(zb_ref[...].astype(F32))).astype(BF16)
    y = y + jnp.dot(yb, w_ref[a_end:b_end, :], preferred_element_type=F32)
    out_ref[...] = x_ref[...] + y


def _hawk_out(ya, os_, ls_, zb_src, zb_col, ym, w, x, tm=1024):
    m, d = x.shape
    row = lambda width, col=0: pl.BlockSpec((tm, width), lambda i: (i, col))
    heads = [o for group in os_ for o in group]
    return pl.pallas_call(
        _hawk_out_kernel,
        grid=(m // tm,),
        in_specs=[row(LRU_WIDTH)] + [row(DIL_HEAD_DIM)] * len(heads) + [row(LANES)] * len(ls_)
                 + [row(DIL_WIDTH, zb_col), row(MEM_WIDTH),
                    pl.BlockSpec(w.shape, lambda i: (0, 0)), row(d)],
        out_specs=row(d),
        out_shape=jax.ShapeDtypeStruct((m, d), F32),
        compiler_params=_params("arbitrary"),
        name="hawk_out",
    )(ya, *heads, *ls_, zb_src, ym, w, x)


def _compress_kernel(k_ref, v_ref, pe_ref, w1_ref, w2k_ref, w2vt_ref, ko_ref, vto_ref):
    n_bat, n_blk = ko_ref.shape[0], ko_ref.shape[1]
    seq = n_blk * CMP_STRIDE

    def hidden(which, src_ref):
        x = jnp.concatenate(
            [jnp.concatenate([src_ref[pl.ds(bi * seq + p, n_blk, stride=CMP_STRIDE), :]
                              for p in range(CMP_STRIDE)], axis=1) for bi in range(n_bat)], axis=0).astype(BF16)
        first = jnp.dot(x, w1_ref[which, 0], preferred_element_type=F32)
        second = jnp.dot(x, w1_ref[which, 1], preferred_element_type=F32)
        pe = (jnp.dot(pe_ref[which, 0], w1_ref[which, 0], preferred_element_type=F32)
              + jnp.dot(pe_ref[which, 1], w1_ref[which, 1], preferred_element_type=F32))
        nxt = pltpu.roll(second.reshape(n_bat, n_blk, second.shape[1]), n_blk - 1, axis=1)
        return _silu(first + nxt.reshape(second.shape) + pe[0:1, :]).astype(BF16)

    act_k, act_v = hidden(0, k_ref), hidden(1, v_ref)
    part = lambda a, g: a[:, g * PHI_HIDDEN:(g + 1) * PHI_HIDDEN]
    ks = jnp.concatenate([jnp.dot(part(act_k, g), w2k_ref[...], preferred_element_type=F32)
                          for g in range(NSA_KV_GROUPS)], axis=1)
    vts = jnp.concatenate([_dot_t(w2vt_ref[...], part(act_v, g)) for g in range(NSA_KV_GROUPS)],
                          axis=0)
    for bi in range(n_bat):
        ko_ref[bi] = ks[bi * n_blk:(bi + 1) * n_blk].astype(ko_ref.dtype)
        vto_ref[bi] = vts[:, bi * n_blk:(bi + 1) * n_blk].astype(vto_ref.dtype)


def _compress(src, k_col, v_col, pe_k, pe_v, k_w1, k_w2, v_w1, v_w2, batch, seq, n_bat=4):
    half = CMP_BLOCK // 2
    assert half == CMP_STRIDE and NSA_KV == LANES
    n_blk = seq // CMP_STRIDE
    hd = NSA_HEAD_DIM
    n_bat = min(n_bat, batch)
    assert batch % n_bat == 0
    w1 = jnp.stack([k_w1, v_w1]).reshape(2, 2, half, hd, PHI_HIDDEN).astype(BF16)
    zero = jnp.zeros_like(w1)
    per_group = [jnp.concatenate([w1 if g == col else zero for col in range(NSA_KV_GROUPS)], axis=-1)
                 for g in range(NSA_KV_GROUPS)]
    w1e = jnp.stack(per_group, axis=3).reshape(2, 2, half * NSA_KV, NSA_KV_GROUPS * PHI_HIDDEN)
    pe = jnp.stack([pe_k, pe_v]).reshape(2, 2, half, 1, hd)
    pe = jnp.broadcast_to(pe, (2, 2, half, NSA_KV_GROUPS, hd)).reshape(2, 2, 1, half * NSA_KV)
    pe = jnp.broadcast_to(pe, (2, 2, SUBLANES, half * NSA_KV)).astype(BF16)
    w2k = k_w2.astype(BF16)
    w2vt = v_w2.T.astype(BF16)
    whole = lambda a: pl.BlockSpec(a.shape, lambda b: (0,) * a.ndim)
    return pl.pallas_call(
        _compress_kernel,
        grid=(batch // n_bat,),
        in_specs=[pl.BlockSpec((n_bat * seq, LANES), lambda b: (b, k_col)),
                  pl.BlockSpec((n_bat * seq, LANES), lambda b: (b, v_col)),
                  whole(pe), whole(w1e), whole(w2k), whole(w2vt)],
        out_specs=[pl.BlockSpec((n_bat, n_blk, NSA_KV), lambda b: (b, 0, 0)),
                   pl.BlockSpec((n_bat, NSA_KV, n_blk), lambda b: (b, 0, 0))],
        out_shape=[jax.ShapeDtypeStruct((batch, n_blk, NSA_KV), BF16),
                   jax.ShapeDtypeStruct((batch, NSA_KV, n_blk), BF16)],
        compiler_params=_params("arbitrary"),
        name="compress",
    )(src, src, pe, w1e, w2k, w2vt)


KEY_CHUNK = 256
NSA_TQ = 256
SLOPE_PIECES = 3
N_FEATS = 2 * SLOPE_PIECES
FEAT_LANES = 32
KEY_COLS = NSA_HEAD_DIM + 2 * FEAT_LANES
N_SLC = 32


def _slope_pieces(slope):
    rest = np.float32(slope)
    pieces = []
    for _ in range(SLOPE_PIECES):
        p = np.float32(np.asarray(rest).astype(BF16))
        pieces.append(float(p))
        rest = np.float32(rest - p)
    return pieces


def _lane_table(lane, values):
    out = jnp.zeros(lane.shape, F32)
    for idx, v in enumerate(values):
        out = jnp.where(lane == idx, v, out)
    return out


def _key_feats(pos_hi, pos_lo, lane):
    return jnp.where(lane < SLOPE_PIECES, pos_hi, jnp.where(lane < N_FEATS, pos_lo, 0)).astype(F32)


def _tile_heads(x):
    return jnp.concatenate([x] * NSA_R, axis=1)


def _chunk_loop(lo, hi, body, init, widths=(4, 2, 1)):
    carry, start = init, lo
    for idx, w in enumerate(widths):
        count = (hi - start) // w

        def step(p, cr, start=start, w=w):
            first = start + p * w
            return body([first + j for j in range(w)], cr)

        if idx == 0:
            carry = lax.fori_loop(0, count, step, carry)
        else:
            carry = lax.cond(count > 0, functools.partial(step, 0), lambda cr: cr, carry)
        start = start + count * w
    return carry


def _nsa_kernel(q_ref, kc_ref, vct_ref, ksrc_ref, vsrc_ref, kwsrc_ref, vwsrc_ref, feat_ref, hot_ref,
                gl_ref, z_ref, ym_ref, w_ref, x_ref, fin_ref, o_ref,
                ks_ref, vst_ref, kw_ref, vwt_ref, s_ref, acc_ref, imp_ref):
    i = pl.program_id(1)
    tq = q_ref.shape[0]
    hd = NSA_HEAD_DIM
    n_cmp = kc_ref.shape[0]

    @pl.when(i == 0)
    def _():
        for g in range(NSA_KV_GROUPS):
            gs = slice(g * hd, (g + 1) * hd)
            for dst, src, tail in ((ks_ref, ksrc_ref, hot_ref[...]),
                                   (kw_ref, kwsrc_ref, jnp.zeros(hot_ref.shape, BF16))):
                dst[:, g * KEY_COLS:g * KEY_COLS + hd] = src[:, gs]
                dst[:, g * KEY_COLS + hd:g * KEY_COLS + hd + FEAT_LANES] = feat_ref[...]
                dst[:, g * KEY_COLS + hd + FEAT_LANES:(g + 1) * KEY_COLS] = tail
        for c in range(vst_ref.shape[0]):
            rows = slice(c * KEY_CHUNK, (c + 1) * KEY_CHUNK)
            vst_ref[c] = vsrc_ref[rows, :].astype(F32).T.astype(BF16)
            vwt_ref[c] = vwsrc_ref[rows, :].astype(F32).T.astype(BF16)

    slopes_all = _alibi_slopes(NSA_HEADS)
    gates_t = jax.nn.sigmoid(gl_ref[...]).T
    feat_lane = lax.broadcasted_iota(jnp.int32, (tq, FEAT_LANES), 1)
    no_sel = jnp.zeros((NSA_R * tq, FEAT_LANES), BF16)
    key_row = lax.broadcasted_iota(jnp.int32, (KEY_CHUNK, tq), 0)
    t_pos = i * tq + lax.broadcasted_iota(jnp.int32, (KEY_CHUNK, tq), 1)
    ones_rows = jnp.ones((BF16_SUBLANES, KEY_CHUNK), BF16)
    win_lo = jnp.maximum(i * tq - (WIN_SIZE - 1), 0) // KEY_CHUNK
    chunks_hi = (i * tq + tq - 1) // KEY_CHUNK + 1

    groups = range(NSA_KV_GROUPS)
    q_win, q_slc, o_cmp = [], [], []
    for g in groups:
        slopes = slopes_all[g * NSA_R:(g + 1) * NSA_R]
        gs = slice(g * hd, (g + 1) * hd)
        q_parts = []
        for r in range(NSA_R):
            qr = q_ref[:, (g * NSA_R + r) * hd:(g * NSA_R + r + 1) * hd]
            feats = _lane_table(feat_lane, _slope_pieces(slopes[r]) * 2).astype(BF16)
            q_parts.append(jnp.concatenate([qr, feats], axis=1))
        q_feat = jnp.concatenate(q_parts, axis=0)
        q_aug = jnp.concatenate([q_feat, no_sel], axis=1)

        n_row = lax.broadcasted_iota(jnp.int32, (n_cmp, tq), 0)
        t_cmp = i * tq + lax.broadcasted_iota(jnp.int32, (n_cmp, tq), 1)
        visible = t_cmp >= n_row * CMP_STRIDE + (CMP_BLOCK - 1)
        cfeat_row = lax.broadcasted_iota(jnp.int32, (n_cmp, 2 * FEAT_LANES), 0)
        cfeat_lane = lax.broadcasted_iota(jnp.int32, (n_cmp, 2 * FEAT_LANES), 1)
        kc_feats = _key_feats(cfeat_row * CMP_STRIDE, 0, cfeat_lane)
        kc_aug = jnp.concatenate([kc_ref[:, gs], kc_feats.astype(BF16)], axis=1)
        s = _dot_t(kc_aug, q_aug) + _tile_heads(jnp.where(visible, 0.0, NEG_INF))
        m = jnp.max(s, axis=0, keepdims=True)
        e = jnp.exp(s - m)
        t_one = i * tq + lax.broadcasted_iota(jnp.int32, (1, NSA_R * tq), 1) % tq
        any_visible = t_one >= (CMP_BLOCK - 1)
        p = e * jnp.where(any_visible, 1.0 / jnp.sum(e, axis=0, keepdims=True), 0.0)
        o_cmp.append(jnp.dot(vct_ref[gs, :], p.astype(BF16), preferred_element_type=F32))
        p_sum = p[:, 0:tq]
        for r in range(1, NSA_R):
            p_sum = p_sum + p[:, r * tq:(r + 1) * tq]

        band = p_sum + pltpu.roll(p_sum, 1, axis=0)
        for k in range(1, CMP_PER_SLC):
            band = band + pltpu.roll(p_sum, n_cmp - k, axis=0)
        halves = []
        for half in range(tq // LANES):
            imp_ref[...] = band[:, half * LANES:(half + 1) * LANES]
            halves.append(imp_ref[pl.ds(0, N_SLC, stride=CMP_PER_SLC), :])
        imp = jnp.concatenate(halves, axis=1)
        blk_j = lax.broadcasted_iota(jnp.int32, (N_SLC, tq), 0)
        cur = (i * tq + lax.broadcasted_iota(jnp.int32, (N_SLC, tq), 1)) // SLC_BLOCK
        forced = (blk_j == 0) | (blk_j == cur) | (blk_j == cur - 1)
        v_imp = jnp.where(forced, SEL_FORCE, jnp.where(blk_j > cur, -SEL_FORCE, imp))
        rank = jnp.zeros((N_SLC, tq), F32)
        for other in range(N_SLC):
            row = v_imp[other:other + 1, :]
            ahead = (row > v_imp) | ((row == v_imp) & (blk_j > other))
            rank = rank + jnp.where(ahead, 1.0, 0.0)
        sel_bias = jnp.where(rank < SLC_TOP_N, 0.0, NEG_INF)

        padded = jnp.concatenate([sel_bias, jnp.zeros((LANES - N_SLC, tq), F32)], axis=0)
        sel_t = padded.T[:, 0:FEAT_LANES].astype(BF16)
        q_win.append(q_aug)
        q_slc.append(jnp.concatenate([q_feat, jnp.concatenate([sel_t] * NSA_R, axis=0)], axis=1))

    def attend(q_brs, k_ref, key_cols, vt_ref, lo, hi, masked_from, mask_fn):
        def scores(cs, m_run, masked):
            starts = [pl.multiple_of(c * KEY_CHUNK, KEY_CHUNK) for c in cs]
            scs = [[_dot_t(k_ref[pl.ds(start, KEY_CHUNK), g * key_cols:(g + 1) * key_cols], q_brs[g])
                    for g in groups] for start in starts]
            for c, start, sc in zip(cs, starts, scs):
                if masked:
                    bias = _tile_heads(jnp.where(mask_fn(t_pos - (start + key_row)), 0.0, NEG_INF))
                    sc = [x + bias for x in sc]
                for g in groups:
                    s_ref[g, c] = sc[g]
                m_run = tuple(jnp.maximum(m_run[g], jnp.max(sc[g], axis=0, keepdims=True)) for g in groups)
            return m_run

        def weighted(cs, carry):
            for c in cs:
                for g in groups:
                    e = jnp.exp(s_ref[g, c] - m_rows[g]).astype(BF16)
                    v_ext = jnp.concatenate([vt_ref[c, g * hd:(g + 1) * hd, :], ones_rows], axis=0)
                    acc_ref[g] += jnp.dot(v_ext, e, preferred_element_type=F32)
            return carry

        m_rows = tuple(jnp.full((1, NSA_R * tq), NEG_INF, F32) for _ in groups)
        m_rows = _chunk_loop(lo, masked_from, functools.partial(scores, masked=False), m_rows)
        m_rows = _chunk_loop(masked_from, hi, functools.partial(scores, masked=True), m_rows)
        acc_ref[...] = jnp.zeros(acc_ref.shape, F32)
        _chunk_loop(lo, hi, weighted, 0)
        return [acc_ref[g, 0:hd, :] / acc_ref[g, hd:hd + 1, :] for g in groups]

    o_slc = attend(q_slc, ks_ref, KEY_COLS, vst_ref, 0, chunks_hi, (i * tq) // KEY_CHUNK,
                   lambda dist: dist >= 0)
    o_win = attend(q_win, kw_ref, KEY_COLS, vwt_ref, win_lo, chunks_hi, win_lo,
                   lambda dist: (dist >= 0) & (dist <= WIN_SIZE - 1))

    y = jnp.dot(ym_ref[...], w_ref[NSA_WIDTH:NSA_WIDTH + MEM_WIDTH, :], preferred_element_type=F32)
    for g in groups:
        def gate(kind):
            base = g * NSA_R * 3 + kind
            return jnp.concatenate([gates_t[base + 3 * r:base + 3 * r + 1, :] for r in range(NSA_R)], axis=1)

        o = gate(0) * o_cmp[g] + gate(1) * o_slc[g] + gate(2) * o_win[g]
        pairs = []
        for r in range(0, NSA_R, 2):
            two = jnp.concatenate([o[:, r * tq:(r + 1) * tq], o[:, (r + 1) * tq:(r + 2) * tq]], axis=0)
            pairs.append(two.T)
        cs = slice(g * NSA_R * hd, (g + 1) * NSA_R * hd)
        yo = (jnp.concatenate(pairs, axis=1) * _silu(z_ref[:, cs].astype(F32))).astype(BF16)
        y = y + jnp.dot(yo, w_ref[cs, :], preferred_element_type=F32)

    x = x_ref[...] + y
    ms = jnp.mean(x * x, axis=-1, keepdims=True)
    o_ref[...] = x * lax.rsqrt(ms + NORM_EPS) * fin_ref[...]


def _key_pos_feats(seq):
    assert seq // SLC_BLOCK == N_SLC <= FEAT_LANES
    pos = np.arange(seq)
    feats = np.zeros((seq, FEAT_LANES), np.float32)
    feats[:, 0:SLOPE_PIECES] = ((pos // SLC_BLOCK) * SLC_BLOCK)[:, None]
    feats[:, SLOPE_PIECES:N_FEATS] = (pos % SLC_BLOCK)[:, None]
    onehot = (np.arange(FEAT_LANES)[None, :] == (pos // SLC_BLOCK)[:, None]).astype(np.float32)
    return jnp.asarray(feats, BF16), jnp.asarray(onehot, BF16)


def _nsa_attn(nb, kv_col0, z_col, k_cmp, v_cmp_t, nf, gl_col, ym, w_out, x, final_g, batch, seq):
    tq = NSA_TQ
    nq = seq // tq
    d = x.shape[1]
    feats, onehot = _key_pos_feats(seq)
    kv_blk = kv_col0 // NSA_KV
    seq_cols = lambda col: pl.BlockSpec((seq, NSA_KV), lambda b, i: (b, kv_blk + col))
    const = lambda a: pl.BlockSpec(a.shape, lambda b, i: (0,) * a.ndim)
    per_batch = lambda a: pl.BlockSpec((None,) + a.shape[1:], lambda b, i: (b,) + (0,) * (a.ndim - 1))
    rows = lambda width, col=0: pl.BlockSpec((tq, width), lambda b, i: (b * nq + i, col))
    fin = final_g.reshape(1, d)
    return pl.pallas_call(
        _nsa_kernel,
        grid=(batch, nq),
        in_specs=[rows(NSA_WIDTH), per_batch(k_cmp), per_batch(v_cmp_t),
                  seq_cols(0), seq_cols(1), seq_cols(2), seq_cols(3), const(feats), const(onehot),
                  rows(LANES, gl_col), rows(NSA_WIDTH, z_col), rows(MEM_WIDTH), const(w_out), rows(d),
                  const(fin)],
        out_specs=rows(d),
        out_shape=jax.ShapeDtypeStruct((batch * seq, d), F32),
        scratch_shapes=[pltpu.VMEM((seq, NSA_KV_GROUPS * KEY_COLS), BF16),
                        pltpu.VMEM((seq // KEY_CHUNK, NSA_KV, KEY_CHUNK), BF16),
                        pltpu.VMEM((seq, NSA_KV_GROUPS * KEY_COLS), BF16),
                        pltpu.VMEM((seq // KEY_CHUNK, NSA_KV, KEY_CHUNK), BF16),
                        pltpu.VMEM((NSA_KV_GROUPS, seq // KEY_CHUNK, KEY_CHUNK, NSA_R * tq), F32),
                        pltpu.VMEM((NSA_KV_GROUPS, NSA_HEAD_DIM + BF16_SUBLANES, NSA_R * tq), F32),
                        pltpu.VMEM((seq // CMP_STRIDE, LANES), F32)],
        compiler_params=_params("arbitrary", "arbitrary"),
        name="nsa_attn",
    )(nb, k_cmp, v_cmp_t, nb, nb, nb, nb, feats, onehot, nf, nb, ym, w_out, x, fin)


def _hawk_layer(x, mem, batch, seq, norm_g, w_in, conv_w, conv_b, ga_w, ga_b, gx_w, gx_b, lam,
                mem_norm_g, w_mem_kv, w_out):
    xa0, za0 = 0, LRU_WIDTH
    q0 = 2 * LRU_WIDTH
    k0, v0 = q0 + DIL_QKV, q0 + 2 * DIL_QKV
    zb0 = q0 + 3 * DIL_QKV
    qm0 = zb0 + DIL_WIDTH
    zm0 = qm0 + MEM_WIDTH

    def qkv_cols(gi):
        return [(base + gi * DIL_WIDTH, DIL_WIDTH) for base in (q0, k0, v0)]

    w = w_in.astype(BF16)
    nat_cols = [(za0, LRU_WIDTH), *qkv_cols(0), (zb0, DIL_WIDTH), (qm0, MEM_WIDTH), (zm0, MEM_WIDTH)]
    xa, hb = _norm_matmul(x, norm_g, w, [(F32, [(xa0, LRU_WIDTH)]), (BF16, nat_cols)])
    za_col = 0
    qkv0_col = LRU_WIDTH // DIL_WIDTH
    zb_col = (LRU_WIDTH + 3 * DIL_WIDTH) // DIL_WIDTH
    qm_col = (LRU_WIDTH + 4 * DIL_WIDTH) // MEM_WIDTH
    zm_col = qm_col + 1
    qkv = [(hb[None], qkv0_col)]
    for gi in range(1, len(DIL_GROUPS)):
        qkv.append((_norm_matmul(x, norm_g, w, [(BF16, qkv_cols(gi))], dil=DIL_GROUPS[gi][1]), 0))
    n_mem = mem.shape[0] // batch
    mem_kv, = _norm_matmul(mem, mem_norm_g, w_mem_kv.astype(BF16), [(BF16, [(0, 2 * MEM_WIDTH)])])

    ya = _rglru(xa, hb, za_col, conv_w, conv_b, _pack_block_diag(ga_w), ga_b, _pack_block_diag(gx_w), gx_b,
                lam, batch, seq)
    attn = [_dil_attn(arr, col0, gi, batch, seq) for gi, (arr, col0) in enumerate(qkv)]
    ym = _mem_attn(hb, qm_col, mem_kv, hb, zm_col, batch, seq, n_mem)
    return _hawk_out(ya, [o for o, _ in attn], [l for _, l in attn], hb, zb_col, ym,
                     w_out.astype(BF16), x)


def _nsa_layer(x, mem, batch, seq, norm_g, w_in, pe_k, pe_v, phik_w1, phik_w2, phiv_w1, phiv_w2,
               mem_norm_g, w_mem_kv, w_out, final_g):
    kv0 = NSA_WIDTH
    gl0 = kv0 + 6 * NSA_KV
    z0 = gl0 + 3 * NSA_HEADS
    qm0 = z0 + NSA_WIDTH
    zm0 = qm0 + MEM_WIDTH
    wb = w_in.astype(BF16)
    gl_w = jnp.pad(wb[:, gl0:z0], ((0, 0), (0, LANES - 3 * NSA_HEADS)))
    q_w = wb[:, 0:kv0] * jnp.asarray(NSA_HEAD_DIM ** -0.5, BF16)
    w_all = jnp.concatenate([q_w, wb[:, z0:qm0], wb[:, kv0 + 2 * NSA_KV:gl0], wb[:, qm0:zm0 + MEM_WIDTH],
                             gl_w, wb[:, kv0:kv0 + 2 * NSA_KV]], axis=1)
    f32_width = LANES + 2 * NSA_KV
    bf16_width = w_all.shape[1] - f32_width
    nb, nf = _norm_matmul(x, norm_g, w_all, [(BF16, [(0, bf16_width)]), (F32, [(bf16_width, f32_width)])])
    gl_col, kc_col, vc_col = 0, 1, 2
    z_col = 1
    kv_col0 = 2 * NSA_WIDTH
    qm_col = (kv_col0 + 4 * NSA_KV) // MEM_WIDTH
    zm_col = qm_col + 1
    n_mem = mem.shape[0] // batch
    mem_kv, = _norm_matmul(mem, mem_norm_g, w_mem_kv.astype(BF16), [(BF16, [(0, 2 * MEM_WIDTH)])])

    k_cmp, v_cmp_t = _compress(nf, kc_col, vc_col, pe_k, pe_v, phik_w1, phik_w2, phiv_w1, phiv_w2,
                               batch, seq)
    ym = _mem_attn(nb, qm_col, mem_kv, nb, zm_col, batch, seq, n_mem)
    return _nsa_attn(nb, kv_col0, z_col, k_cmp, v_cmp_t, nf, gl_col, ym, w_out.astype(BF16), x, final_g,
                     batch, seq)


def kernel(x, mem, hawk_norm, hawk_w_in, hawk_conv_w, hawk_conv_b, hawk_gate_a_w, hawk_gate_a_b,
           hawk_gate_x_w, hawk_gate_x_b, hawk_lambda, hawk_mem_norm, hawk_w_mem_kv, hawk_w_out,
           nsa_norm, nsa_w_in, nsa_pe_k, nsa_pe_v, nsa_phi_k_w1, nsa_phi_k_w2, nsa_phi_v_w1,
           nsa_phi_v_w2, nsa_mem_norm, nsa_w_mem_kv, nsa_w_out, final_norm):
    batch, seq, d = x.shape
    assert hawk_norm.shape[0] == 1 and nsa_norm.shape[0] == 1, "one layer of each kind"
    assert seq % (ATTN_BLOCK * DIL_GROUPS[-1][1]) == 0
    x2 = x.reshape(batch * seq, d)
    mem2 = mem.reshape(batch * mem.shape[1], d)
    x2 = _hawk_layer(x2, mem2, batch, seq, hawk_norm[0], hawk_w_in[0], hawk_conv_w[0], hawk_conv_b[0],
                     hawk_gate_a_w[0], hawk_gate_a_b[0].reshape(-1), hawk_gate_x_w[0],
                     hawk_gate_x_b[0].reshape(-1), hawk_lambda[0], hawk_mem_norm[0], hawk_w_mem_kv[0],
                     hawk_w_out[0])
    out = _nsa_layer(x2, mem2, batch, seq, nsa_norm[0], nsa_w_in[0], nsa_pe_k[0], nsa_pe_v[0],
                     nsa_phi_k_w1[0], nsa_phi_k_w2[0], nsa_phi_v_w1[0], nsa_phi_v_w2[0],
                     nsa_mem_norm[0], nsa_w_mem_kv[0], nsa_w_out[0], final_norm)
    return out.reshape(batch, seq, d)
```

```python
import functools

import numpy as np
import jax
import jax.numpy as jnp
from jax import lax
from jax.experimental import pallas as pl
from jax.experimental.pallas import tpu as pltpu

F32 = jnp.float32
BF16 = jnp.bfloat16

NORM_EPS = 1e-6
NEG_INF = -1e30
LOG2_E = 1.4426950408889634
LANES = 128
SUBLANES = 8
BF16_SUBLANES = 16
ATTN_BLOCK = 128
VMEM_LIMIT = 56 * 1024 * 1024

LRU_WIDTH = 1024
LRU_BLOCKS = 16
LRU_BLOCK_DIM = LRU_WIDTH // LRU_BLOCKS
LRU_PACK = 256
CONV_WIDTH = 4
LRU_C = 8.0

DIL_GROUPS = ((128, 1), (512, 4), (2048, 16))
DIL_HEADS = 4
DIL_HEAD_DIM = 128
DIL_WIDTH = DIL_HEADS * DIL_HEAD_DIM
DIL_QKV = len(DIL_GROUPS) * DIL_WIDTH
LSE_LANES = LANES // DIL_HEADS

MEM_HEADS = 4
MEM_HEAD_DIM = 64
MEM_WIDTH = MEM_HEADS * MEM_HEAD_DIM

NSA_HEADS = 16
NSA_KV_GROUPS = 2
NSA_R = NSA_HEADS // NSA_KV_GROUPS
NSA_HEAD_DIM = 64
NSA_WIDTH = NSA_HEADS * NSA_HEAD_DIM
NSA_KV = NSA_KV_GROUPS * NSA_HEAD_DIM
CMP_BLOCK = 32
CMP_STRIDE = 16
SLC_BLOCK = 64
SLC_TOP_N = 8
WIN_SIZE = 512
PHI_HIDDEN = 256
SEL_FORCE = 1e6
CMP_PER_SLC = SLC_BLOCK // CMP_STRIDE


def _alibi_slopes(n):
    return [float(v) for v in np.exp2(-8.0 * np.arange(1, n + 1) / n).astype(np.float32)]


def _params(*semantics):
    return pltpu.CompilerParams(dimension_semantics=semantics, vmem_limit_bytes=VMEM_LIMIT)


def _silu(z):
    return z * jax.nn.sigmoid(z)


def _dot_t(a, b):
    return lax.dot_general(a, b, (((1,), (1,)), ((), ())), preferred_element_type=F32)


def _rms_norm_rows(x, g):
    ms = jnp.mean(x * x, axis=-1, keepdims=True)
    return (x * lax.rsqrt(ms + NORM_EPS) * g).astype(BF16)


def _norm_matmul_kernel(*refs, dil, pieces, row_parts):
    x_ref, g_ref = refs[:2]
    w_refs, o_refs = refs[2:2 + sum(pieces)], refs[2 + sum(pieces):]
    tm, k = x_ref.shape
    rows = tm // row_parts
    per = rows // dil
    for part in range(row_parts):
        x = x_ref[part * rows:(part + 1) * rows, :]
        if dil == 1:
            xn = _rms_norm_rows(x, g_ref[...])
        else:
            xn = x * lax.rsqrt(jnp.mean(x * x, axis=-1, keepdims=True) + NORM_EPS) * g_ref[...]
            xn = jnp.swapaxes(xn.reshape(per, dil, k), 0, 1).reshape(rows, k).astype(BF16)
        first = 0
        for o_ref, n_pieces in zip(o_refs, pieces):
            col = 0
            for w_ref in w_refs[first:first + n_pieces]:
                width = w_ref.shape[1]
                res = jnp.dot(xn, w_ref[...], preferred_element_type=F32).astype(o_ref.dtype)
                if dil == 1:
                    o_ref[part * rows:(part + 1) * rows, col:col + width] = res
                else:
                    for c in range(dil):
                        o_ref[c, part * per:(part + 1) * per, col:col + width] = res[c * per:(c + 1) * per]
                col += width
            first += n_pieces


def _norm_matmul(x, g, w, outs, dil=1, tm=1024, row_parts=8):
    m, k = x.shape
    tm = min(tm, m)
    assert m % tm == 0 and k % LANES == 0
    pieces = [len(cols) for _, cols in outs]
    widths = [sum(width for _, width in cols) for _, cols in outs]
    w_specs = []
    for _, cols in outs:
        for col, width in cols:
            assert col % width == 0 and width % LANES == 0
            w_specs.append(pl.BlockSpec((k, width), functools.partial(lambda i, j: (0, j), j=col // width)))
    resident = [pl.BlockSpec((1, k), lambda i: (0, 0))] + w_specs
    operands = (g.reshape(1, k),) + (w,) * len(w_specs)
    assert tm % (dil * row_parts * SUBLANES) == 0
    kernel = functools.partial(_norm_matmul_kernel, dil=dil, pieces=pieces, row_parts=row_parts)
    if dil == 1:
        return pl.pallas_call(
            kernel,
            grid=(m // tm,),
            in_specs=[pl.BlockSpec((tm, k), lambda i: (i, 0))] + resident,
            out_specs=[pl.BlockSpec((tm, width), lambda i: (i, 0)) for width in widths],
            out_shape=[jax.ShapeDtypeStruct((m, width), dtype) for width, (dtype, _) in zip(widths, outs)],
            compiler_params=_params("arbitrary"),
            name="norm_matmul",
        )(x, *operands)
    per = tm // dil
    (out_dtype, _), = outs
    n, = widths
    assert tm % dil == 0 and per % BF16_SUBLANES == 0
    return pl.pallas_call(
        kernel,
        grid=(m // tm,),
        in_specs=[pl.BlockSpec((tm, k), lambda i: (i, 0))] + resident,
        out_specs=pl.BlockSpec((dil, per, n), lambda i: (0, i, 0)),
        out_shape=jax.ShapeDtypeStruct((dil, m // dil, n), out_dtype),
        compiler_params=_params("arbitrary"),
        name="norm_matmul_dil",
    )(x, *operands)


def _rglru_kernel(xa_ref, za_ref, cw_ref, cb_ref, wa_ref, ba_ref, wx_ref, bx_ref, lam_ref,
                  o_ref, xpad_ref, h_ref):
    t = pl.program_id(1)
    tt, width = xa_ref.shape
    halo = SUBLANES

    @pl.when(t == 0)
    def _():
        xpad_ref[0:halo, :] = jnp.zeros((halo, width), F32)
        h_ref[...] = jnp.zeros_like(h_ref)

    x = xa_ref[...]
    xpad_ref[halo:halo + tt, :] = x
    cw = cw_ref[...]
    y = cw[CONV_WIDTH - 1:CONV_WIDTH] * x
    for k in range(1, CONV_WIDTH):
        y = y + cw[CONV_WIDTH - 1 - k:CONV_WIDTH - k] * xpad_ref[halo - k:halo - k + tt, :]
    y = y + cb_ref[...]
    xpad_ref[0:halo, :] = x[tt - halo:tt, :]

    yb = y.astype(BF16)
    r_parts, i_parts = [], []
    for p in range(width // LRU_PACK):
        ys = yb[:, p * LRU_PACK:(p + 1) * LRU_PACK]
        r_parts.append(jnp.dot(ys, wa_ref[p], preferred_element_type=F32))
        i_parts.append(jnp.dot(ys, wx_ref[p], preferred_element_type=F32))
    r = jax.nn.sigmoid(jnp.concatenate(r_parts, axis=1) + ba_ref[...])
    gi = jax.nn.sigmoid(jnp.concatenate(i_parts, axis=1) + bx_ref[...])

    nl = -lam_ref[...]
    softplus = jnp.maximum(nl, 0.0) + jnp.log1p(jnp.exp(-jnp.abs(nl)))
    log_a = (-LRU_C) * r * softplus
    a = jnp.exp(log_a)
    w = -jnp.tanh(log_a) * (a * a + 1.0)
    mult = jnp.where(w > 0.0, w * lax.rsqrt(w), 0.0)
    b = y * gi * mult
    first = (lax.broadcasted_iota(jnp.int32, (SUBLANES, width), 0) == 0) & (t == 0)
    b = jnp.concatenate([jnp.where(first, (y * gi)[0:SUBLANES], b[0:SUBLANES]), b[SUBLANES:]], axis=0)

    blocks = width // LANES

    def time_major(v):
        cols = jnp.stack([v[:, j * LANES:(j + 1) * LANES] for j in range(blocks)], axis=0)
        return jnp.swapaxes(cols, 0, 1)

    a_t, b_t = time_major(a), time_major(b)
    state = h_ref[...].reshape(blocks, LANES)
    steps = []
    for step in range(tt):
        state = a_t[step] * state + b_t[step]
        steps.append(state)
    h_ref[...] = state.reshape(1, width)
    h_cols = jnp.swapaxes(jnp.stack(steps, axis=0), 0, 1)
    h = jnp.concatenate([h_cols[j] for j in range(blocks)], axis=1)
    o_ref[...] = (h * _silu(za_ref[...].astype(F32))).astype(o_ref.dtype)


def _rglru(xa, za_src, za_col, conv_w, conv_b, wa, ba, wx, bx, lam, batch, seq, tt=1024):
    width = LRU_WIDTH
    nt = seq // tt
    packs = width // LRU_PACK
    vec = pl.BlockSpec((1, width), lambda b, t: (0, 0))
    gate_w = pl.BlockSpec((packs, LRU_PACK, LRU_PACK), lambda b, t: (0, 0, 0))
    return pl.pallas_call(
        _rglru_kernel,
        grid=(batch, nt),
        in_specs=[pl.BlockSpec((tt, width), lambda b, t: (b * nt + t, 0)),
                  pl.BlockSpec((tt, width), lambda b, t: (b * nt + t, za_col)),
                  pl.BlockSpec((CONV_WIDTH, width), lambda b, t: (0, 0)),
                  vec, gate_w, vec, gate_w, vec, vec],
        out_specs=pl.BlockSpec((tt, width), lambda b, t: (b * nt + t, 0)),
        out_shape=jax.ShapeDtypeStruct((batch * seq, width), BF16),
        scratch_shapes=[pltpu.VMEM((tt + SUBLANES, width), F32), pltpu.VMEM((1, width), F32)],
        compiler_params=_params("arbitrary", "arbitrary"),
        name="rglru",
    )(xa, za_src, conv_w, conv_b.reshape(1, width), wa, ba.reshape(1, width), wx, bx.reshape(1, width),
      lam.reshape(1, width))


def _pack_block_diag(w):
    per = LRU_PACK // LRU_BLOCK_DIM
    w = w.reshape(LRU_BLOCKS // per, per, LRU_BLOCK_DIM, LRU_BLOCK_DIM)
    eye = jnp.eye(per, dtype=w.dtype)
    packed = w[:, :, :, None, :] * eye[None, :, None, :, None]
    return packed.reshape(LRU_BLOCKS // per, LRU_PACK, LRU_PACK).astype(BF16)


def _dil_attn_kernel(*refs, slopes, pos_scale, max_dist, has_halo, dil, n_cls, n_blk):
    if has_halo:
        q_ref, kh_ref, k_ref, vh_ref, v_ref = refs[:5]
        out_refs = refs[5:]
    else:
        q_ref, k_ref, v_ref = refs[:3]
        out_refs = refs[3:]
    n_out = DIL_HEADS + 1
    dst_refs = out_refs[:n_out]
    stage_refs = out_refs[n_out:] if dil > 1 else dst_refs
    first_super = pl.program_id(1) == 0
    cls0 = pl.program_id(2) * n_cls
    blk = ATTN_BLOCK
    scale = DIL_HEAD_DIM ** -0.5

    def band(width, halo_live):
        row = lax.broadcasted_iota(jnp.int32, (blk, width), 0)
        col = lax.broadcasted_iota(jnp.int32, (blk, width), 1)
        dist = (width - blk) + row - col
        valid = (dist >= 0) & (dist <= max_dist)
        if halo_live is not None:
            valid = valid & ((col >= blk) | halo_live)
        distf = (dist * pos_scale).astype(F32)
        return [jnp.where(valid, (-slope / scale) * distf, NEG_INF) for slope in slopes]

    bias_inner = band(2 * blk, None) if n_blk > 1 else None
    bias_first = band(2 * blk, jnp.logical_not(first_super)) if has_halo else band(blk, None)

    def scores(cc, jb):
        cur = slice(jb * blk, (jb + 1) * blk)
        bias = bias_inner if jb > 0 else bias_first
        out = []
        for h in range(DIL_HEADS):
            hs = slice(h * DIL_HEAD_DIM, (h + 1) * DIL_HEAD_DIM)
            q = q_ref[cc, cur, hs]
            if jb > 0:
                k = k_ref[cc, (jb - 1) * blk:(jb + 1) * blk, hs]
                v = v_ref[cc, (jb - 1) * blk:(jb + 1) * blk, hs]
            elif has_halo:
                k = jnp.concatenate([kh_ref[cc, :, hs], k_ref[cc, cur, hs]], axis=0)
                v = jnp.concatenate([vh_ref[cc, :, hs], v_ref[cc, cur, hs]], axis=0)
            else:
                k, v = k_ref[cc, cur, hs], v_ref[cc, cur, hs]
            out.append((_dot_t(q, k) + bias[h], v))
        return out

    def finish(cc, jb, pairs):
        where = (cls0 + cc, slice(jb * blk, (jb + 1) * blk)) if dil > 1 else (slice(jb * blk, (jb + 1) * blk),)
        lses = []
        for h, (s, v) in enumerate(pairs):
            m = jnp.max(s, axis=-1, keepdims=True)
            e = jnp.exp2((s - m) * (scale * LOG2_E))
            den = jnp.sum(e, axis=-1, keepdims=True)
            o = jnp.dot(e.astype(BF16), v, preferred_element_type=F32) / den
            stage_refs[h][where] = o.astype(stage_refs[h].dtype)
            lses.append(jnp.broadcast_to(m * scale + jnp.log(den), (blk, LSE_LANES)))
        stage_refs[DIL_HEADS][where] = jnp.concatenate(lses, axis=1)

    pending = None
    for cc in range(n_cls):
        for jb in range(n_blk):
            pairs = scores(cc, jb)
            if pending is not None:
                finish(*pending)
            pending = (cc, jb, pairs)
    finish(*pending)

    if dil > 1:
        @pl.when(pl.program_id(2) == pl.num_programs(2) - 1)
        def _():
            for stage, dst in zip(stage_refs, dst_refs):
                dst[...] = jnp.swapaxes(stage[...], 0, 1).reshape(dst.shape).astype(dst.dtype)


def _dil_attn(qkv, col0, gi, batch, seq, work=8):
    window, dil = DIL_GROUPS[gi]
    sub = seq // dil
    nb = sub // ATTN_BLOCK
    n_blk = min(work, nb)
    n_cls = min(work // n_blk, dil)
    n_super = nb // n_blk
    has_halo = n_super > 1
    span = n_blk * ATTN_BLOCK
    slopes = _alibi_slopes(len(DIL_GROUPS) * DIL_HEADS)[gi * DIL_HEADS:(gi + 1) * DIL_HEADS]
    cur = lambda col: pl.BlockSpec((n_cls, span, DIL_WIDTH), lambda b, i, c: (c, b * n_super + i, col0 + col))
    halo = lambda col: pl.BlockSpec(
        (n_cls, ATTN_BLOCK, DIL_WIDTH),
        lambda b, i, c: (c, jnp.maximum((b * n_super + i) * n_blk - 1, 0), col0 + col))
    if has_halo:
        in_specs = [cur(0), halo(1), cur(1), halo(2), cur(2)]
    else:
        in_specs = [cur(0), cur(1), cur(2)]
    n_out = DIL_HEADS + 1
    *o, lse = pl.pallas_call(
        functools.partial(_dil_attn_kernel, slopes=slopes, pos_scale=dil, max_dist=window // dil,
                          has_halo=has_halo, dil=dil, n_cls=n_cls, n_blk=n_blk),
        grid=(batch, n_super, dil // n_cls),
        in_specs=in_specs,
        out_specs=[pl.BlockSpec((span * dil, LANES), lambda b, i, c: (b * n_super + i, 0))] * n_out,
        out_shape=[jax.ShapeDtypeStruct((batch * seq, LANES), BF16)] * DIL_HEADS
                  + [jax.ShapeDtypeStruct((batch * seq, LANES), F32)],
        scratch_shapes=[pltpu.VMEM((dil, span, LANES), F32)] * (n_out if dil > 1 else 0),
        compiler_params=_params("arbitrary", "arbitrary", "arbitrary"),
        name=f"dil_attn_d{dil}",
    )(*([qkv] * len(in_specs)))
    return o, lse


def _mem_attn_kernel(q_ref, k_ref, v_ref, z_ref, o_ref, ks_ref, vt_ref):
    hd = MEM_HEAD_DIM
    n_mem = k_ref.shape[0]

    @pl.when(pl.program_id(1) == 0)
    def _():
        ks_ref[...] = (k_ref[...].astype(F32) * (hd ** -0.5)).astype(BF16)
        vt = v_ref[...].astype(F32).T.astype(BF16)
        for h in range(MEM_HEADS):
            vt_ref[h, 0:hd, :] = vt[h * hd:(h + 1) * hd, :]
            vt_ref[h, hd:, :] = jnp.ones((vt_ref.shape[1] - hd, n_mem), BF16)

    heads = [slice(h * hd, (h + 1) * hd) for h in range(MEM_HEADS)]
    scores = [_dot_t(ks_ref[:, hs], q_ref[:, hs]) for hs in heads]
    outs = []
    for h, s in enumerate(scores):
        e = jnp.exp(s - jnp.max(s, axis=0, keepdims=True)).astype(BF16)
        acc = jnp.dot(vt_ref[h], e, preferred_element_type=F32)
        outs.append(acc[0:hd, :] / acc[hd:hd + 1, :])
    o = jnp.concatenate(outs, axis=0).T
    o_ref[...] = (o * _silu(z_ref[...].astype(F32))).astype(o_ref.dtype)


def _mem_attn(qsrc, q_col, kv, zsrc, z_col, batch, seq, n_mem, tq=1024):
    nq = seq // tq
    return pl.pallas_call(
        _mem_attn_kernel,
        grid=(batch, nq),
        in_specs=[pl.BlockSpec((tq, MEM_WIDTH), lambda b, i: (b * nq + i, q_col)),
                  pl.BlockSpec((n_mem, MEM_WIDTH), lambda b, i: (b, 0)),
                  pl.BlockSpec((n_mem, MEM_WIDTH), lambda b, i: (b, 1)),
                  pl.BlockSpec((tq, MEM_WIDTH), lambda b, i: (b * nq + i, z_col))],
        out_specs=pl.BlockSpec((tq, MEM_WIDTH), lambda b, i: (b * nq + i, 0)),
        out_shape=jax.ShapeDtypeStruct((batch * seq, MEM_WIDTH), BF16),
        scratch_shapes=[pltpu.VMEM((n_mem, MEM_WIDTH), BF16),
                        pltpu.VMEM((MEM_HEADS, MEM_HEAD_DIM + BF16_SUBLANES, n_mem), BF16)],
        compiler_params=_params("arbitrary", "arbitrary"),
        name="mem_attn",
    )(qsrc, kv, kv, zsrc)


def _hawk_out_kernel(*refs):
    n_groups = len(DIL_GROUPS)
    ya_ref = refs[0]
    o_refs = refs[1:1 + n_groups * DIL_HEADS]
    l_refs = refs[1 + n_groups * DIL_HEADS:1 + n_groups * (DIL_HEADS + 1)]
    zb_ref, ym_ref, w_ref, x_ref, out_ref = refs[1 + n_groups * (DIL_HEADS + 1):]
    a_end = LRU_WIDTH
    b_end = a_end + DIL_WIDTH
    y = jnp.dot(ya_ref[...], w_ref[0:a_end, :], preferred_element_type=F32)
    y = y + jnp.dot(ym_ref[...], w_ref[b_end:b_end + MEM_WIDTH, :], preferred_element_type=F32)
    parts = []
    for h in range(DIL_HEADS):
        ls = [l[:, h * LSE_LANES:h * LSE_LANES + 1] for l in l_refs]
        m = functools.reduce(jnp.maximum, ls)
        ws = [jnp.exp(l - m) for l in ls]
        num = sum(w * o_refs[gi * DIL_HEADS + h][...].astype(F32) for gi, w in enumerate(ws))
        parts.append(num / sum(ws))
    yb = (jnp.concatenate(parts, axis=1) * _silu(zb_ref[...].astype(F32))).astype(BF16)
    y = y + jnp.dot(yb, w_ref[a_end:b_end, :], preferred_element_type=F32)
    out_ref[...] = x_ref[...] + y


def _hawk_out(ya, os_, ls_, zb_src, zb_col, ym, w, x, tm=1024):
    m, d = x.shape
    row = lambda width, col=0: pl.BlockSpec((tm, width), lambda i: (i, col))
    heads = [o for group in os_ for o in group]
    return pl.pallas_call(
        _hawk_out_kernel,
        grid=(m // tm,),
        in_specs=[row(LRU_WIDTH)] + [row(DIL_HEAD_DIM)] * len(heads) + [row(LANES)] * len(ls_)
                 + [row(DIL_WIDTH, zb_col), row(MEM_WIDTH),
                    pl.BlockSpec(w.shape, lambda i: (0, 0)), row(d)],
        out_specs=row(d),
        out_shape=jax.ShapeDtypeStruct((m, d), F32),
        compiler_params=_params("arbitrary"),
        name="hawk_out",
    )(ya, *heads, *ls_, zb_src, ym, w, x)


def _compress_kernel(k_ref, v_ref, pe_ref, w1_ref, w2k_ref, w2vt_ref, ko_ref, vto_ref):
    n_bat, n_blk = ko_ref.shape[0], ko_ref.shape[1]
    seq = n_blk * CMP_STRIDE

    def hidden(which, src_ref):
        x = jnp.concatenate(
            [jnp.concatenate([src_ref[pl.ds(bi * seq + p, n_blk, stride=CMP_STRIDE), :]
                              for p in range(CMP_STRIDE)], axis=1) for bi in range(n_bat)], axis=0).astype(BF16)
        first = jnp.dot(x, w1_ref[which, 0], preferred_element_type=F32)
        second = jnp.dot(x, w1_ref[which, 1], preferred_element_type=F32)
        pe = (jnp.dot(pe_ref[which, 0], w1_ref[which, 0], preferred_element_type=F32)
              + jnp.dot(pe_ref[which, 1], w1_ref[which, 1], preferred_element_type=F32))
        nxt = pltpu.roll(second.reshape(n_bat, n_blk, second.shape[1]), n_blk - 1, axis=1)
        return _silu(first + nxt.reshape(second.shape) + pe[0:1, :]).astype(BF16)

    act_k, act_v = hidden(0, k_ref), hidden(1, v_ref)
    part = lambda a, g: a[:, g * PHI_HIDDEN:(g + 1) * PHI_HIDDEN]
    ks = jnp.concatenate([jnp.dot(part(act_k, g), w2k_ref[...], preferred_element_type=F32)
                          for g in range(NSA_KV_GROUPS)], axis=1)
    vts = jnp.concatenate([_dot_t(w2vt_ref[...], part(act_v, g)) for g in range(NSA_KV_GROUPS)],
                          axis=0)
    for bi in range(n_bat):
        ko_ref[bi] = ks[bi * n_blk:(bi + 1) * n_blk].astype(ko_ref.dtype)
        vto_ref[bi] = vts[:, bi * n_blk:(bi + 1) * n_blk].astype(vto_ref.dtype)


def _compress(src, k_col, v_col, pe_k, pe_v, k_w1, k_w2, v_w1, v_w2, batch, seq, n_bat=4):
    half = CMP_BLOCK // 2
    assert half == CMP_STRIDE and NSA_KV == LANES
    n_blk = seq // CMP_STRIDE
    hd = NSA_HEAD_DIM
    n_bat = min(n_bat, batch)
    assert batch % n_bat == 0
    w1 = jnp.stack([k_w1, v_w1]).reshape(2, 2, half, hd, PHI_HIDDEN).astype(BF16)
    zero = jnp.zeros_like(w1)
    per_group = [jnp.concatenate([w1 if g == col else zero for col in range(NSA_KV_GROUPS)], axis=-1)
                 for g in range(NSA_KV_GROUPS)]
    w1e = jnp.stack(per_group, axis=3).reshape(2, 2, half * NSA_KV, NSA_KV_GROUPS * PHI_HIDDEN)
    pe = jnp.stack([pe_k, pe_v]).reshape(2, 2, half, 1, hd)
    pe = jnp.broadcast_to(pe, (2, 2, half, NSA_KV_GROUPS, hd)).reshape(2, 2, 1, half * NSA_KV)
    pe = jnp.broadcast_to(pe, (2, 2, SUBLANES, half * NSA_KV)).astype(BF16)
    w2k = k_w2.astype(BF16)
    w2vt = v_w2.T.astype(BF16)
    whole = lambda a: pl.BlockSpec(a.shape, lambda b: (0,) * a.ndim)
    return pl.pallas_call(
        _compress_kernel,
        grid=(batch // n_bat,),
        in_specs=[pl.BlockSpec((n_bat * seq, LANES), lambda b: (b, k_col)),
                  pl.BlockSpec((n_bat * seq, LANES), lambda b: (b, v_col)),
                  whole(pe), whole(w1e), whole(w2k), whole(w2vt)],
        out_specs=[pl.BlockSpec((n_bat, n_blk, NSA_KV), lambda b: (b, 0, 0)),
                   pl.BlockSpec((n_bat, NSA_KV, n_blk), lambda b: (b, 0, 0))],
        out_shape=[jax.ShapeDtypeStruct((batch, n_blk, NSA_KV), BF16),
                   jax.ShapeDtypeStruct((batch, NSA_KV, n_blk), BF16)],
        compiler_params=_params("arbitrary"),
        name="compress",
    )(src, src, pe, w1e, w2k, w2vt)


KEY_CHUNK = 256
NSA_TQ = 256
SLOPE_PIECES = 3
N_FEATS = 2 * SLOPE_PIECES
FEAT_LANES = 32
KEY_COLS = NSA_HEAD_DIM + 2 * FEAT_LANES
N_SLC = 32


def _slope_pieces(slope):
    rest = np.float32(slope)
    pieces = []
    for _ in range(SLOPE_PIECES):
        p = np.float32(np.asarray(rest).astype(BF16))
        pieces.append(float(p))
        rest = np.float32(rest - p)
    return pieces


def _lane_table(lane, values):
    out = jnp.zeros(lane.shape, F32)
    for idx, v in enumerate(values):
        out = jnp.where(lane == idx, v, out)
    return out


def _key_feats(pos_hi, pos_lo, lane):
    return jnp.where(lane < SLOPE_PIECES, pos_hi, jnp.where(lane < N_FEATS, pos_lo, 0)).astype(F32)


def _tile_heads(x):
    return jnp.concatenate([x] * NSA_R, axis=1)


def _chunk_loop(lo, hi, body, init, widths=(4, 2, 1)):
    carry, start = init, lo
    for idx, w in enumerate(widths):
        count = (hi - start) // w

        def step(p, cr, start=start, w=w):
            first = start + p * w
            return body([first + j for j in range(w)], cr)

        if idx == 0:
            carry = lax.fori_loop(0, count, step, carry)
        else:
            carry = lax.cond(count > 0, functools.partial(step, 0), lambda cr: cr, carry)
        start = start + count * w
    return carry


def _nsa_kernel(q_ref, kc_ref, vct_ref, ksrc_ref, vsrc_ref, kwsrc_ref, vwsrc_ref, feat_ref, hot_ref,
                gl_ref, z_ref, ym_ref, w_ref, x_ref, fin_ref, o_ref,
                ks_ref, vst_ref, kw_ref, vwt_ref, s_ref, acc_ref, imp_ref):
    i = pl.program_id(1)
    tq = q_ref.shape[0]
    hd = NSA_HEAD_DIM
    n_cmp = kc_ref.shape[0]

    @pl.when(i == 0)
    def _():
        for g in range(NSA_KV_GROUPS):
            gs = slice(g * hd, (g + 1) * hd)
            for dst, src, tail in ((ks_ref, ksrc_ref, hot_ref[...]),
                                   (kw_ref, kwsrc_ref, jnp.zeros(hot_ref.shape, BF16))):
                dst[:, g * KEY_COLS:g * KEY_COLS + hd] = src[:, gs]
                dst[:, g * KEY_COLS + hd:g * KEY_COLS + hd + FEAT_LANES] = feat_ref[...]
                dst[:, g * KEY_COLS + hd + FEAT_LANES:(g + 1) * KEY_COLS] = tail
        for c in range(vst_ref.shape[0]):
            rows = slice(c * KEY_CHUNK, (c + 1) * KEY_CHUNK)
            vst_ref[c] = vsrc_ref[rows, :].astype(F32).T.astype(BF16)
            vwt_ref[c] = vwsrc_ref[rows, :].astype(F32).T.astype(BF16)

    slopes_all = _alibi_slopes(NSA_HEADS)
    gates_t = jax.nn.sigmoid(gl_ref[...]).T
    feat_lane = lax.broadcasted_iota(jnp.int32, (tq, FEAT_LANES), 1)
    no_sel = jnp.zeros((NSA_R * tq, FEAT_LANES), BF16)
    key_row = lax.broadcasted_iota(jnp.int32, (KEY_CHUNK, tq), 0)
    t_pos = i * tq + lax.broadcasted_iota(jnp.int32, (KEY_CHUNK, tq), 1)
    ones_rows = jnp.ones((BF16_SUBLANES, KEY_CHUNK), BF16)
    win_lo = jnp.maximum(i * tq - (WIN_SIZE - 1), 0) // KEY_CHUNK
    chunks_hi = (i * tq + tq - 1) // KEY_CHUNK + 1

    groups = range(NSA_KV_GROUPS)
    q_win, q_slc, o_cmp = [], [], []
    for g in groups:
        slopes = slopes_all[g * NSA_R:(g + 1) * NSA_R]
        gs = slice(g * hd, (g + 1) * hd)
        q_parts = []
        for r in range(NSA_R):
            qr = q_ref[:, (g * NSA_R + r) * hd:(g * NSA_R + r + 1) * hd]
            feats = _lane_table(feat_lane, _slope_pieces(slopes[r]) * 2).astype(BF16)
            q_parts.append(jnp.concatenate([qr, feats], axis=1))
        q_feat = jnp.concatenate(q_parts, axis=0)
        q_aug = jnp.concatenate([q_feat, no_sel], axis=1)

        n_row = lax.broadcasted_iota(jnp.int32, (n_cmp, tq), 0)
        t_cmp = i * tq + lax.broadcasted_iota(jnp.int32, (n_cmp, tq), 1)
        visible = t_cmp >= n_row * CMP_STRIDE + (CMP_BLOCK - 1)
        cfeat_row = lax.broadcasted_iota(jnp.int32, (n_cmp, 2 * FEAT_LANES), 0)
        cfeat_lane = lax.broadcasted_iota(jnp.int32, (n_cmp, 2 * FEAT_LANES), 1)
        kc_feats = _key_feats(cfeat_row * CMP_STRIDE, 0, cfeat_lane)
        kc_aug = jnp.concatenate([kc_ref[:, gs], kc_feats.astype(BF16)], axis=1)
        s = _dot_t(kc_aug, q_aug) + _tile_heads(jnp.where(visible, 0.0, NEG_INF))
        m = jnp.max(s, axis=0, keepdims=True)
        e = jnp.exp(s - m)
        t_one = i * tq + lax.broadcasted_iota(jnp.int32, (1, NSA_R * tq), 1) % tq
        any_visible = t_one >= (CMP_BLOCK - 1)
        p = e * jnp.where(any_visible, 1.0 / jnp.sum(e, axis=0, keepdims=True), 0.0)
        o_cmp.append(jnp.dot(vct_ref[gs, :], p.astype(BF16), preferred_element_type=F32))
        p_sum = p[:, 0:tq]
        for r in range(1, NSA_R):
            p_sum = p_sum + p[:, r * tq:(r + 1) * tq]

        band = p_sum + pltpu.roll(p_sum, 1, axis=0)
        for k in range(1, CMP_PER_SLC):
            band = band + pltpu.roll(p_sum, n_cmp - k, axis=0)
        halves = []
        for half in range(tq // LANES):
            imp_ref[...] = band[:, half * LANES:(half + 1) * LANES]
            halves.append(imp_ref[pl.ds(0, N_SLC, stride=CMP_PER_SLC), :])
        imp = jnp.concatenate(halves, axis=1)
        blk_j = lax.broadcasted_iota(jnp.int32, (N_SLC, tq), 0)
        cur = (i * tq + lax.broadcasted_iota(jnp.int32, (N_SLC, tq), 1)) // SLC_BLOCK
        forced = (blk_j == 0) | (blk_j == cur) | (blk_j == cur - 1)
        v_imp = jnp.where(forced, SEL_FORCE, jnp.where(blk_j > cur, -SEL_FORCE, imp))
        rank = jnp.zeros((N_SLC, tq), F32)
        for other in range(N_SLC):
            row = v_imp[other:other + 1, :]
            ahead = (row > v_imp) | ((row == v_imp) & (blk_j > other))
            rank = rank + jnp.where(ahead, 1.0, 0.0)
        sel_bias = jnp.where(rank < SLC_TOP_N, 0.0, NEG_INF)

        padded = jnp.concatenate([sel_bias, jnp.zeros((LANES - N_SLC, tq), F32)], axis=0)
        sel_t = padded.T[:, 0:FEAT_LANES].astype(BF16)
        q_win.append(q_aug)
        q_slc.append(jnp.concatenate([q_feat, jnp.concatenate([sel_t] * NSA_R, axis=0)], axis=1))

    def attend(q_brs, k_ref, key_cols, vt_ref, lo, hi, masked_from, mask_fn):
        def scores(cs, m_run, masked):
            starts = [pl.multiple_of(c * KEY_CHUNK, KEY_CHUNK) for c in cs]
            scs = [[_dot_t(k_ref[pl.ds(start, KEY_CHUNK), g * key_cols:(g + 1) * key_cols], q_brs[g])
                    for g in groups] for start in starts]
            for c, start, sc in zip(cs, starts, scs):
                if masked:
                    bias = _tile_heads(jnp.where(mask_fn(t_pos - (start + key_row)), 0.0, NEG_INF))
                    sc = [x + bias for x in sc]
                for g in groups:
                    s_ref[g, c] = sc[g]
                m_run = tuple(jnp.maximum(m_run[g], jnp.max(sc[g], axis=0, keepdims=True)) for g in groups)
            return m_run

        def weighted(cs, carry):
            for c in cs:
                for g in groups:
                    e = jnp.exp(s_ref[g, c] - m_rows[g]).astype(BF16)
                    v_ext = jnp.concatenate([vt_ref[c, g * hd:(g + 1) * hd, :], ones_rows], axis=0)
                    acc_ref[g] += jnp.dot(v_ext, e, preferred_element_type=F32)
            return carry

        m_rows = tuple(jnp.full((1, NSA_R * tq), NEG_INF, F32) for _ in groups)
        m_rows = _chunk_loop(lo, masked_from, functools.partial(scores, masked=False), m_rows)
        m_rows = _chunk_loop(masked_from, hi, functools.partial(scores, masked=True), m_rows)
        acc_ref[...] = jnp.zeros(acc_ref.shape, F32)
        _chunk_loop(lo, hi, weighted, 0)
        return [acc_ref[g, 0:hd, :] / acc_ref[g, hd:hd + 1, :] for g in groups]

    o_slc = attend(q_slc, ks_ref, KEY_COLS, vst_ref, 0, chunks_hi, (i * tq) // KEY_CHUNK,
                   lambda dist: dist >= 0)
    o_win = attend(q_win, kw_ref, KEY_COLS, vwt_ref, win_lo, chunks_hi, win_lo,
                   lambda dist: (dist >= 0) & (dist <= WIN_SIZE - 1))

    y = jnp.dot(ym_ref[...], w_ref[NSA_WIDTH:NSA_WIDTH + MEM_WIDTH, :], preferred_element_type=F32)
    for g in groups:
        def gate(kind):
            base = g * NSA_R * 3 + kind
            return jnp.concatenate([gates_t[base + 3 * r:base + 3 * r + 1, :] for r in range(NSA_R)], axis=1)

        o = gate(0) * o_cmp[g] + gate(1) * o_slc[g] + gate(2) * o_win[g]
        pairs = []
        for r in range(0, NSA_R, 2):
            two = jnp.concatenate([o[:, r * tq:(r + 1) * tq], o[:, (r + 1) * tq:(r + 2) * tq]], axis=0)
            pairs.append(two.T)
        cs = slice(g * NSA_R * hd, (g + 1) * NSA_R * hd)
        yo = (jnp.concatenate(pairs, axis=1) * _silu(z_ref[:, cs].astype(F32))).astype(BF16)
        y = y + jnp.dot(yo, w_ref[cs, :], preferred_element_type=F32)

    x = x_ref[...] + y
    ms = jnp.mean(x * x, axis=-1, keepdims=True)
    o_ref[...] = x * lax.rsqrt(ms + NORM_EPS) * fin_ref[...]


def _key_pos_feats(seq):
    assert seq // SLC_BLOCK == N_SLC <= FEAT_LANES
    pos = np.arange(seq)
    feats = np.zeros((seq, FEAT_LANES), np.float32)
    feats[:, 0:SLOPE_PIECES] = ((pos // SLC_BLOCK) * SLC_BLOCK)[:, None]
    feats[:, SLOPE_PIECES:N_FEATS] = (pos % SLC_BLOCK)[:, None]
    onehot = (np.arange(FEAT_LANES)[None, :] == (pos // SLC_BLOCK)[:, None]).astype(np.float32)
    return jnp.asarray(feats, BF16), jnp.asarray(onehot, BF16)


def _nsa_attn(nb, kv_col0, z_col, k_cmp, v_cmp_t, nf, gl_col, ym, w_out, x, final_g, batch, seq):
    tq = NSA_TQ
    nq = seq // tq
    d = x.shape[1]
    feats, onehot = _key_pos_feats(seq)
    kv_blk = kv_col0 // NSA_KV
    seq_cols = lambda col: pl.BlockSpec((seq, NSA_KV), lambda b, i: (b, kv_blk + col))
    const = lambda a: pl.BlockSpec(a.shape, lambda b, i: (0,) * a.ndim)
    per_batch = lambda a: pl.BlockSpec((None,) + a.shape[1:], lambda b, i: (b,) + (0,) * (a.ndim - 1))
    rows = lambda width, col=0: pl.BlockSpec((tq, width), lambda b, i: (b * nq + i, col))
    fin = final_g.reshape(1, d)
    return pl.pallas_call(
        _nsa_kernel,
        grid=(batch, nq),
        in_specs=[rows(NSA_WIDTH), per_batch(k_cmp), per_batch(v_cmp_t),
                  seq_cols(0), seq_cols(1), seq_cols(2), seq_cols(3), const(feats), const(onehot),
                  rows(LANES, gl_col), rows(NSA_WIDTH, z_col), rows(MEM_WIDTH), const(w_out), rows(d),
                  const(fin)],
        out_specs=rows(d),
        out_shape=jax.ShapeDtypeStruct((batch * seq, d), F32),
        scratch_shapes=[pltpu.VMEM((seq, NSA_KV_GROUPS * KEY_COLS), BF16),
                        pltpu.VMEM((seq // KEY_CHUNK, NSA_KV, KEY_CHUNK), BF16),
                        pltpu.VMEM((seq, NSA_KV_GROUPS * KEY_COLS), BF16),
                        pltpu.VMEM((seq // KEY_CHUNK, NSA_KV, KEY_CHUNK), BF16),
                        pltpu.VMEM((NSA_KV_GROUPS, seq // KEY_CHUNK, KEY_CHUNK, NSA_R * tq), F32),
                        pltpu.VMEM((NSA_KV_GROUPS, NSA_HEAD_DIM + BF16_SUBLANES, NSA_R * tq), F32),
                        pltpu.VMEM((seq // CMP_STRIDE, LANES), F32)],
        compiler_params=_params("arbitrary", "arbitrary"),
        name="nsa_attn",
    )(nb, k_cmp, v_cmp_t, nb, nb, nb, nb, feats, onehot, nf, nb, ym, w_out, x, fin)


def _hawk_layer(x, mem, batch, seq, norm_g, w_in, conv_w, conv_b, ga_w, ga_b, gx_w, gx_b, lam,
                mem_norm_g, w_mem_kv, w_out):
    xa0, za0 = 0, LRU_WIDTH
    q0 = 2 * LRU_WIDTH
    k0, v0 = q0 + DIL_QKV, q0 + 2 * DIL_QKV
    zb0 = q0 + 3 * DIL_QKV
    qm0 = zb0 + DIL_WIDTH
    zm0 = qm0 + MEM_WIDTH

    def qkv_cols(gi):
        return [(base + gi * DIL_WIDTH, DIL_WIDTH) for base in (q0, k0, v0)]

    w = w_in.astype(BF16)
    nat_cols = [(za0, LRU_WIDTH), *qkv_cols(0), (zb0, DIL_WIDTH), (qm0, MEM_WIDTH), (zm0, MEM_WIDTH)]
    xa, hb = _norm_matmul(x, norm_g, w, [(F32, [(xa0, LRU_WIDTH)]), (BF16, nat_cols)])
    za_col = 0
    qkv0_col = LRU_WIDTH // DIL_WIDTH
    zb_col = (LRU_WIDTH + 3 * DIL_WIDTH) // DIL_WIDTH
    qm_col = (LRU_WIDTH + 4 * DIL_WIDTH) // MEM_WIDTH
    zm_col = qm_col + 1
    qkv = [(hb[None], qkv0_col)]
    for gi in range(1, len(DIL_GROUPS)):
        qkv.append((_norm_matmul(x, norm_g, w, [(BF16, qkv_cols(gi))], dil=DIL_GROUPS[gi][1]), 0))
    n_mem = mem.shape[0] // batch
    mem_kv, = _norm_matmul(mem, mem_norm_g, w_mem_kv.astype(BF16), [(BF16, [(0, 2 * MEM_WIDTH)])])

    ya = _rglru(xa, hb, za_col, conv_w, conv_b, _pack_block_diag(ga_w), ga_b, _pack_block_diag(gx_w), gx_b,
                lam, batch, seq)
    attn = [_dil_attn(arr, col0, gi, batch, seq) for gi, (arr, col0) in enumerate(qkv)]
    ym = _mem_attn(hb, qm_col, mem_kv, hb, zm_col, batch, seq, n_mem)
    return _hawk_out(ya, [o for o, _ in attn], [l for _, l in attn], hb, zb_col, ym,
                     w_out.astype(BF16), x)


def _nsa_layer(x, mem, batch, seq, norm_g, w_in, pe_k, pe_v, phik_w1, phik_w2, phiv_w1, phiv_w2,
               mem_norm_g, w_mem_kv, w_out, final_g):
    kv0 = NSA_WIDTH
    gl0 = kv0 + 6 * NSA_KV
    z0 = gl0 + 3 * NSA_HEADS
    qm0 = z0 + NSA_WIDTH
    zm0 = qm0 + MEM_WIDTH
    wb = w_in.astype(BF16)
    gl_w = jnp.pad(wb[:, gl0:z0], ((0, 0), (0, LANES - 3 * NSA_HEADS)))
    q_w = wb[:, 0:kv0] * jnp.asarray(NSA_HEAD_DIM ** -0.5, BF16)
    w_all = jnp.concatenate([q_w, wb[:, z0:qm0], wb[:, kv0 + 2 * NSA_KV:gl0], wb[:, qm0:zm0 + MEM_WIDTH],
                             gl_w, wb[:, kv0:kv0 + 2 * NSA_KV]], axis=1)
    f32_width = LANES + 2 * NSA_KV
    bf16_width = w_all.shape[1] - f32_width
    nb, nf = _norm_matmul(x, norm_g, w_all, [(BF16, [(0, bf16_width)]), (F32, [(bf16_width, f32_width)])])
    gl_col, kc_col, vc_col = 0, 1, 2
    z_col = 1
    kv_col0 = 2 * NSA_WIDTH
    qm_col = (kv_col0 + 4 * NSA_KV) // MEM_WIDTH
    zm_col = qm_col + 1
    n_mem = mem.shape[0] // batch
    mem_kv, = _norm_matmul(mem, mem_norm_g, w_mem_kv.astype(BF16), [(BF16, [(0, 2 * MEM_WIDTH)])])

    k_cmp, v_cmp_t = _compress(nf, kc_col, vc_col, pe_k, pe_v, phik_w1, phik_w2, phiv_w1, phiv_w2,
                               batch, seq)
    ym = _mem_attn(nb, qm_col, mem_kv, nb, zm_col, batch, seq, n_mem)
    return _nsa_attn(nb, kv_col0, z_col, k_cmp, v_cmp_t, nf, gl_col, ym, w_out.astype(BF16), x, final_g,
                     batch, seq)


def kernel(x, mem, hawk_norm, hawk_w_in, hawk_conv_w, hawk_conv_b, hawk_gate_a_w, hawk_gate_a_b,
           hawk_gate_x_w, hawk_gate_x_b, hawk_lambda, hawk_mem_norm, hawk_w_mem_kv, hawk_w_out,
           nsa_norm, nsa_w_in, nsa_pe_k, nsa_pe_v, nsa_phi_k_w1, nsa_phi_k_w2, nsa_phi_v_w1,
           nsa_phi_v_w2, nsa_mem_norm, nsa_w_mem_kv, nsa_w_out, final_norm):
    batch, seq, d = x.shape
    assert hawk_norm.shape[0] == 1 and nsa_norm.shape[0] == 1, "one layer of each kind"
    assert seq % (ATTN_BLOCK * DIL_GROUPS[-1][1]) == 0
    x2 = x.reshape(batch * seq, d)
    mem2 = mem.reshape(batch * mem.shape[1], d)
    x2 = _hawk_layer(x2, mem2, batch, seq, hawk_norm[0], hawk_w_in[0], hawk_conv_w[0], hawk_conv_b[0],
                     hawk_gate_a_w[0], hawk_gate_a_b[0].reshape(-1), hawk_gate_x_w[0],
                     hawk_gate_x_b[0].reshape(-1), hawk_lambda[0], hawk_mem_norm[0], hawk_w_mem_kv[0],
                     hawk_w_out[0])
    out = _nsa_layer(x2, mem2, batch, seq, nsa_norm[0], nsa_w_in[0], nsa_pe_k[0], nsa_pe_v[0],
                     nsa_phi_k_w1[0], nsa_phi_k_w2[0], nsa_phi_v_w1[0], nsa_phi_v_w2[0],
                     nsa_mem_norm[0], nsa_w_mem_kv[0], nsa_w_out[0], final_norm)
    return out.reshape(batch, seq, d)
```

```python
import functools

import numpy as np
import jax
import jax.numpy as jnp
from jax import lax
from jax.experimental import pallas as pl
from jax.experimental.pallas import tpu as pltpu

F32 = jnp.float32
BF16 = jnp.bfloat16

NORM_EPS = 1e-6
NEG_INF = -1e30
LOG2_E = 1.4426950408889634
LANES = 128
SUBLANES = 8
BF16_SUBLANES = 16
ATTN_BLOCK = 128
VMEM_LIMIT = 56 * 1024 * 1024

LRU_WIDTH = 1024
LRU_BLOCKS = 16
LRU_BLOCK_DIM = LRU_WIDTH // LRU_BLOCKS
LRU_PACK = 256
CONV_WIDTH = 4
LRU_C = 8.0

DIL_GROUPS = ((128, 1), (512, 4), (2048, 16))
DIL_HEADS = 4
DIL_HEAD_DIM = 128
DIL_WIDTH = DIL_HEADS * DIL_HEAD_DIM
DIL_QKV = len(DIL_GROUPS) * DIL_WIDTH
LSE_LANES = LANES // DIL_HEADS

MEM_HEADS = 4
MEM_HEAD_DIM = 64
MEM_WIDTH = MEM_HEADS * MEM_HEAD_DIM

NSA_HEADS = 16
NSA_KV_GROUPS = 2
NSA_R = NSA_HEADS // NSA_KV_GROUPS
NSA_HEAD_DIM = 64
NSA_WIDTH = NSA_HEADS * NSA_HEAD_DIM
NSA_KV = NSA_KV_GROUPS * NSA_HEAD_DIM
CMP_BLOCK = 32
CMP_STRIDE = 16
SLC_BLOCK = 64
SLC_TOP_N = 8
WIN_SIZE = 512
PHI_HIDDEN = 256
SEL_FORCE = 1e6
CMP_PER_SLC = SLC_BLOCK // CMP_STRIDE


def _alibi_slopes(n):
    return [float(v) for v in np.exp2(-8.0 * np.arange(1, n + 1) / n).astype(np.float32)]


def _params(*semantics):
    return pltpu.CompilerParams(dimension_semantics=semantics, vmem_limit_bytes=VMEM_LIMIT)


def _silu(z):
    return z * jax.nn.sigmoid(z)


def _dot_t(a, b):
    return lax.dot_general(a, b, (((1,), (1,)), ((), ())), preferred_element_type=F32)


def _rms_norm_rows(x, g):
    ms = jnp.mean(x * x, axis=-1, keepdims=True)
    return (x * lax.rsqrt(ms + NORM_EPS) * g).astype(BF16)


def _norm_matmul_kernel(*refs, dil, pieces, row_parts):
    x_ref, g_ref = refs[:2]
    w_refs, o_refs = refs[2:2 + sum(pieces)], refs[2 + sum(pieces):]
    tm, k = x_ref.shape
    rows = tm // row_parts
    per = rows // dil
    for part in range(row_parts):
        x = x_ref[part * rows:(part + 1) * rows, :]
        if dil == 1:
            xn = _rms_norm_rows(x, g_ref[...])
        else:
            xn = x * lax.rsqrt(jnp.mean(x * x, axis=-1, keepdims=True) + NORM_EPS) * g_ref[...]
            xn = jnp.swapaxes(xn.reshape(per, dil, k), 0, 1).reshape(rows, k).astype(BF16)
        first = 0
        for o_ref, n_pieces in zip(o_refs, pieces):
            col = 0
            for w_ref in w_refs[first:first + n_pieces]:
                width = w_ref.shape[1]
                res = jnp.dot(xn, w_ref[...], preferred_element_type=F32).astype(o_ref.dtype)
                if dil == 1:
                    o_ref[part * rows:(part + 1) * rows, col:col + width] = res
                else:
                    for c in range(dil):
                        o_ref[c, part * per:(part + 1) * per, col:col + width] = res[c * per:(c + 1) * per]
                col += width
            first += n_pieces


def _norm_matmul(x, g, w, outs, dil=1, tm=1024, row_parts=8):
    m, k = x.shape
    tm = min(tm, m)
    assert m % tm == 0 and k % LANES == 0
    pieces = [len(cols) for _, cols in outs]
    widths = [sum(width for _, width in cols) for _, cols in outs]
    w_specs = []
    for _, cols in outs:
        for col, width in cols:
            assert col % width == 0 and width % LANES == 0
            w_specs.append(pl.BlockSpec((k, width), functools.partial(lambda i, j: (0, j), j=col // width)))
    resident = [pl.BlockSpec((1, k), lambda i: (0, 0))] + w_specs
    operands = (g.reshape(1, k),) + (w,) * len(w_specs)
    assert tm % (dil * row_parts * SUBLANES) == 0
    kernel = functools.partial(_norm_matmul_kernel, dil=dil, pieces=pieces, row_parts=row_parts)
    if dil == 1:
        return pl.pallas_call(
            kernel,
            grid=(m // tm,),
            in_specs=[pl.BlockSpec((tm, k), lambda i: (i, 0))] + resident,
            out_specs=[pl.BlockSpec((tm, width), lambda i: (i, 0)) for width in widths],
            out_shape=[jax.ShapeDtypeStruct((m, width), dtype) for width, (dtype, _) in zip(widths, outs)],
            compiler_params=_params("arbitrary"),
            name="norm_matmul",
        )(x, *operands)
    per = tm // dil
    (out_dtype, _), = outs
    n, = widths
    assert tm % dil == 0 and per % BF16_SUBLANES == 0
    return pl.pallas_call(
        kernel,
        grid=(m // tm,),
        in_specs=[pl.BlockSpec((tm, k), lambda i: (i, 0))] + resident,
        out_specs=pl.BlockSpec((dil, per, n), lambda i: (0, i, 0)),
        out_shape=jax.ShapeDtypeStruct((dil, m // dil, n), out_dtype),
        compiler_params=_params("arbitrary"),
        name="norm_matmul_dil",
    )(x, *operands)


def _rglru_kernel(xa_ref, za_ref, cw_ref, cb_ref, wa_ref, ba_ref, wx_ref, bx_ref, lam_ref,
                  o_ref, xpad_ref, h_ref):
    t = pl.program_id(1)
    tt, width = xa_ref.shape
    halo = SUBLANES

    @pl.when(t == 0)
    def _():
        xpad_ref[0:halo, :] = jnp.zeros((halo, width), F32)
        h_ref[...] = jnp.zeros_like(h_ref)

    x = xa_ref[...]
    xpad_ref[halo:halo + tt, :] = x
    cw = cw_ref[...]
    y = cw[CONV_WIDTH - 1:CONV_WIDTH] * x
    for k in range(1, CONV_WIDTH):
        y = y + cw[CONV_WIDTH - 1 - k:CONV_WIDTH - k] * xpad_ref[halo - k:halo - k + tt, :]
    y = y + cb_ref[...]
    xpad_ref[0:halo, :] = x[tt - halo:tt, :]

    yb = y.astype(BF16)
    r_parts, i_parts = [], []
    for p in range(width // LRU_PACK):
        ys = yb[:, p * LRU_PACK:(p + 1) * LRU_PACK]
        r_parts.append(jnp.dot(ys, wa_ref[p], preferred_element_type=F32))
        i_parts.append(jnp.dot(ys, wx_ref[p], preferred_element_type=F32))
    r = jax.nn.sigmoid(jnp.concatenate(r_parts, axis=1) + ba_ref[...])
    gi = jax.nn.sigmoid(jnp.concatenate(i_parts, axis=1) + bx_ref[...])

    nl = -lam_ref[...]
    softplus = jnp.maximum(nl, 0.0) + jnp.log1p(jnp.exp(-jnp.abs(nl)))
    log_a = (-LRU_C) * r * softplus
    a = jnp.exp(log_a)
    w = -jnp.tanh(log_a) * (a * a + 1.0)
    mult = jnp.where(w > 0.0, w * lax.rsqrt(w), 0.0)
    b = y * gi * mult
    first = (lax.broadcasted_iota(jnp.int32, (SUBLANES, width), 0) == 0) & (t == 0)
    b = jnp.concatenate([jnp.where(first, (y * gi)[0:SUBLANES], b[0:SUBLANES]), b[SUBLANES:]], axis=0)

    blocks = width // LANES

    def time_major(v):
        cols = jnp.stack([v[:, j * LANES:(j + 1) * LANES] for j in range(blocks)], axis=0)
        return jnp.swapaxes(cols, 0, 1)

    a_t, b_t = time_major(a), time_major(b)
    state = h_ref[...].reshape(blocks, LANES)
    steps = []
    for step in range(tt):
        state = a_t[step] * state + b_t[step]
        steps.append(state)
    h_ref[...] = state.reshape(1, width)
    h_cols = jnp.swapaxes(jnp.stack(steps, axis=0), 0, 1)
    h = jnp.concatenate([h_cols[j] for j in range(blocks)], axis=1)
    o_ref[...] = (h * _silu(za_ref[...].astype(F32))).astype(o_ref.dtype)


def _rglru(xa, za_src, za_col, conv_w, conv_b, wa, ba, wx, bx, lam, batch, seq, tt=1024):
    width = LRU_WIDTH
    nt = seq // tt
    packs = width // LRU_PACK
    vec = pl.BlockSpec((1, width), lambda b, t: (0, 0))
    gate_w = pl.BlockSpec((packs, LRU_PACK, LRU_PACK), lambda b, t: (0, 0, 0))
    return pl.pallas_call(
        _rglru_kernel,
        grid=(batch, nt),
        in_specs=[pl.BlockSpec((tt, width), lambda b, t: (b * nt + t, 0)),
                  pl.BlockSpec((tt, width), lambda b, t: (b * nt + t, za_col)),
                  pl.BlockSpec((CONV_WIDTH, width), lambda b, t: (0, 0)),
                  vec, gate_w, vec, gate_w, vec, vec],
        out_specs=pl.BlockSpec((tt, width), lambda b, t: (b * nt + t, 0)),
        out_shape=jax.ShapeDtypeStruct((batch * seq, width), BF16),
        scratch_shapes=[pltpu.VMEM((tt + SUBLANES, width), F32), pltpu.VMEM((1, width), F32)],
        compiler_params=_params("arbitrary", "arbitrary"),
        name="rglru",
    )(xa, za_src, conv_w, conv_b.reshape(1, width), wa, ba.reshape(1, width), wx, bx.reshape(1, width),
      lam.reshape(1, width))


def _pack_block_diag(w):
    per = LRU_PACK // LRU_BLOCK_DIM
    w = w.reshape(LRU_BLOCKS // per, per, LRU_BLOCK_DIM, LRU_BLOCK_DIM)
    eye = jnp.eye(per, dtype=w.dtype)
    packed = w[:, :, :, None, :] * eye[None, :, None, :, None]
    return packed.reshape(LRU_BLOCKS // per, LRU_PACK, LRU_PACK).astype(BF16)


def _dil_attn_kernel(*refs, slopes, pos_scale, max_dist, has_halo, dil, n_cls, n_blk):
    if has_halo:
        q_ref, kh_ref, k_ref, vh_ref, v_ref = refs[:5]
        out_refs = refs[5:]
    else:
        q_ref, k_ref, v_ref = refs[:3]
        out_refs = refs[3:]
    n_out = DIL_HEADS + 1
    dst_refs = out_refs[:n_out]
    stage_refs = out_refs[n_out:] if dil > 1 else dst_refs
    first_super = pl.program_id(1) == 0
    cls0 = pl.program_id(2) * n_cls
    blk = ATTN_BLOCK
    scale = DIL_HEAD_DIM ** -0.5

    def band(width, halo_live):
        row = lax.broadcasted_iota(jnp.int32, (blk, width), 0)
        col = lax.broadcasted_iota(jnp.int32, (blk, width), 1)
        dist = (width - blk) + row - col
        valid = (dist >= 0) & (dist <= max_dist)
        if halo_live is not None:
            valid = valid & ((col >= blk) | halo_live)
        distf = (dist * pos_scale).astype(F32)
        return [jnp.where(valid, (-slope / scale) * distf, NEG_INF) for slope in slopes]

    bias_inner = band(2 * blk, None) if n_blk > 1 else None
    bias_first = band(2 * blk, jnp.logical_not(first_super)) if has_halo else band(blk, None)

    def scores(cc, jb):
        cur = slice(jb * blk, (jb + 1) * blk)
        bias = bias_inner if jb > 0 else bias_first
        out = []
        for h in range(DIL_HEADS):
            hs = slice(h * DIL_HEAD_DIM, (h + 1) * DIL_HEAD_DIM)
            q = q_ref[cc, cur, hs]
            if jb > 0:
                k = k_ref[cc, (jb - 1) * blk:(jb + 1) * blk, hs]
                v = v_ref[cc, (jb - 1) * blk:(jb + 1) * blk, hs]
            elif has_halo:
                k = jnp.concatenate([kh_ref[cc, :, hs], k_ref[cc, cur, hs]], axis=0)
                v = jnp.concatenate([vh_ref[cc, :, hs], v_ref[cc, cur, hs]], axis=0)
            else:
                k, v = k_ref[cc, cur, hs], v_ref[cc, cur, hs]
            out.append((_dot_t(q, k) + bias[h], v))
        return out

    def finish(cc, jb, pairs):
        where = (cls0 + cc, slice(jb * blk, (jb + 1) * blk)) if dil > 1 else (slice(jb * blk, (jb + 1) * blk),)
        lses = []
        for h, (s, v) in enumerate(pairs):
            m = jnp.max(s, axis=-1, keepdims=True)
            e = jnp.exp2((s - m) * (scale * LOG2_E))
            den = jnp.sum(e, axis=-1, keepdims=True)
            o = jnp.dot(e.astype(BF16), v, preferred_element_type=F32) / den
            stage_refs[h][where] = o.astype(stage_refs[h].dtype)
            lses.append(jnp.broadcast_to(m * scale + jnp.log(den), (blk, LSE_LANES)))
        stage_refs[DIL_HEADS][where] = jnp.concatenate(lses, axis=1)

    pending = None
    for cc in range(n_cls):
        for jb in range(n_blk):
            pairs = scores(cc, jb)
            if pending is not None:
                finish(*pending)
            pending = (cc, jb, pairs)
    finish(*pending)

    if dil > 1:
        @pl.when(pl.program_id(2) == pl.num_programs(2) - 1)
        def _():
            for stage, dst in zip(stage_refs, dst_refs):
                dst[...] = jnp.swapaxes(stage[...], 0, 1).reshape(dst.shape).astype(dst.dtype)


def _dil_attn(qkv, col0, gi, batch, seq, work=8):
    window, dil = DIL_GROUPS[gi]
    sub = seq // dil
    nb = sub // ATTN_BLOCK
    n_blk = min(work, nb)
    n_cls = min(work // n_blk, dil)
    n_super = nb // n_blk
    has_halo = n_super > 1
    span = n_blk * ATTN_BLOCK
    slopes = _alibi_slopes(len(DIL_GROUPS) * DIL_HEADS)[gi * DIL_HEADS:(gi + 1) * DIL_HEADS]
    cur = lambda col: pl.BlockSpec((n_cls, span, DIL_WIDTH), lambda b, i, c: (c, b * n_super + i, col0 + col))
    halo = lambda col: pl.BlockSpec(
        (n_cls, ATTN_BLOCK, DIL_WIDTH),
        lambda b, i, c: (c, jnp.maximum((b * n_super + i) * n_blk - 1, 0), col0 + col))
    if has_halo:
        in_specs = [cur(0), halo(1), cur(1), halo(2), cur(2)]
    else:
        in_specs = [cur(0), cur(1), cur(2)]
    n_out = DIL_HEADS + 1
    *o, lse = pl.pallas_call(
        functools.partial(_dil_attn_kernel, slopes=slopes, pos_scale=dil, max_dist=window // dil,
                          has_halo=has_halo, dil=dil, n_cls=n_cls, n_blk=n_blk),
        grid=(batch, n_super, dil // n_cls),
        in_specs=in_specs,
        out_specs=[pl.BlockSpec((span * dil, LANES), lambda b, i, c: (b * n_super + i, 0))] * n_out,
        out_shape=[jax.ShapeDtypeStruct((batch * seq, LANES), BF16)] * DIL_HEADS
                  + [jax.ShapeDtypeStruct((batch * seq, LANES), F32)],
        scratch_shapes=[pltpu.VMEM((dil, span, LANES), F32)] * (n_out if dil > 1 else 0),
        compiler_params=_params("arbitrary", "arbitrary", "arbitrary"),
        name=f"dil_attn_d{dil}",
    )(*([qkv] * len(in_specs)))
    return o, lse


def _mem_attn_kernel(q_ref, k_ref, v_ref, z_ref, o_ref, ks_ref, vt_ref):
    hd = MEM_HEAD_DIM
    n_mem = k_ref.shape[0]

    @pl.when(pl.program_id(1) == 0)
    def _():
        ks_ref[...] = (k_ref[...].astype(F32) * (hd ** -0.5)).astype(BF16)
        vt = v_ref[...].astype(F32).T.astype(BF16)
        for h in range(MEM_HEADS):
            vt_ref[h, 0:hd, :] = vt[h * hd:(h + 1) * hd, :]
            vt_ref[h, hd:, :] = jnp.ones((vt_ref.shape[1] - hd, n_mem), BF16)

    heads = [slice(h * hd, (h + 1) * hd) for h in range(MEM_HEADS)]
    scores = [_dot_t(ks_ref[:, hs], q_ref[:, hs]) for hs in heads]
    outs = []
    for h, s in enumerate(scores):
        e = jnp.exp(s - jnp.max(s, axis=0, keepdims=True)).astype(BF16)
        acc = jnp.dot(vt_ref[h], e, preferred_element_type=F32)
        outs.append(acc[0:hd, :] / acc[hd:hd + 1, :])
    o = jnp.concatenate(outs, axis=0).T
    o_ref[...] = (o * _silu(z_ref[...].astype(F32))).astype(o_ref.dtype)


def _mem_attn(qsrc, q_col, kv, zsrc, z_col, batch, seq, n_mem, tq=1024):
    nq = seq // tq
    return pl.pallas_call(
        _mem_attn_kernel,
        grid=(batch, nq),
        in_specs=[pl.BlockSpec((tq, MEM_WIDTH), lambda b, i: (b * nq + i, q_col)),
                  pl.BlockSpec((n_mem, MEM_WIDTH), lambda b, i: (b, 0)),
                  pl.BlockSpec((n_mem, MEM_WIDTH), lambda b, i: (b, 1)),
                  pl.BlockSpec((tq, MEM_WIDTH), lambda b, i: (b * nq + i, z_col))],
        out_specs=pl.BlockSpec((tq, MEM_WIDTH), lambda b, i: (b * nq + i, 0)),
        out_shape=jax.ShapeDtypeStruct((batch * seq, MEM_WIDTH), BF16),
        scratch_shapes=[pltpu.VMEM((n_mem, MEM_WIDTH), BF16),
                        pltpu.VMEM((MEM_HEADS, MEM_HEAD_DIM + BF16_SUBLANES, n_mem), BF16)],
        compiler_params=_params("arbitrary", "arbitrary"),
        name="mem_attn",
    )(qsrc, kv, kv, zsrc)


def _hawk_out_kernel(*refs):
    n_groups = len(DIL_GROUPS)
    ya_ref = refs[0]
    o_refs = refs[1:1 + n_groups * DIL_HEADS]
    l_refs = refs[1 + n_groups * DIL_HEADS:1 + n_groups * (DIL_HEADS + 1)]
    zb_ref, ym_ref, w_ref, x_ref, out_ref = refs[1 + n_groups * (DIL_HEADS + 1):]
    a_end = LRU_WIDTH
    b_end = a_end + DIL_WIDTH
    y = jnp.dot(ya_ref[...], w_ref[0:a_end, :], preferred_element_type=F32)
    y = y + jnp.dot(ym_ref[...], w_ref[b_end:b_end + MEM_WIDTH, :], preferred_element_type=F32)
    parts = []
    for h in range(DIL_HEADS):
        ls = [l[:, h * LSE_LANES:h * LSE_LANES + 1] for l in l_refs]
        m = functools.reduce(jnp.maximum, ls)
        ws = [jnp.exp(l - m) for l in ls]
        num = sum(w * o_refs[gi * DIL_HEADS + h][...].astype(F32) for gi, w in enumerate(ws))
        parts.append(num / sum(ws))
    yb = (jnp.concatenate(parts, axis=1) * _silu(zb_ref[...].astype(F32))).astype(BF16)
    y = y + jnp.dot(yb, w_ref[a_end:b_end, :], preferred_element_type=F32)
    out_ref[...] = x_ref[...] + y


def _hawk_out(ya, os_, ls_, zb_src, zb_col, ym, w, x, tm=1024):
    m, d = x.shape
    row = lambda width, col=0: pl.BlockSpec((tm, width), lambda i: (i, col))
    heads = [o for group in os_ for o in group]
    return pl.pallas_call(
        _hawk_out_kernel,
        grid=(m // tm,),
        in_specs=[row(LRU_WIDTH)] + [row(DIL_HEAD_DIM)] * len(heads) + [row(LANES)] * len(ls_)
                 + [row(DIL_WIDTH, zb_col), row(MEM_WIDTH),
                    pl.BlockSpec(w.shape, lambda i: (0, 0)), row(d)],
        out_specs=row(d),
        out_shape=jax.ShapeDtypeStruct((m, d), F32),
        compiler_params=_params("arbitrary"),
        name="hawk_out",
    )(ya, *heads, *ls_, zb_src, ym, w, x)


def _compress_kernel(k_ref, v_ref, pe_ref, w1_ref, w2k_ref, w2vt_ref, ko_ref, vto_ref):
    n_bat, n_blk = ko_ref.shape[0], ko_ref.shape[1]
    seq = n_blk * CMP_STRIDE

    def hidden(which, src_ref):
        x = jnp.concatenate(
            [jnp.concatenate([src_ref[pl.ds(bi * seq + p, n_blk, stride=CMP_STRIDE), :]
                              for p in range(CMP_STRIDE)], axis=1) for bi in range(n_bat)], axis=0).astype(BF16)
        first = jnp.dot(x, w1_ref[which, 0], preferred_element_type=F32)
        second = jnp.dot(x, w1_ref[which, 1], preferred_element_type=F32)
        pe = (jnp.dot(pe_ref[which, 0], w1_ref[which, 0], preferred_element_type=F32)
              + jnp.dot(pe_ref[which, 1], w1_ref[which, 1], preferred_element_type=F32))
        nxt = pltpu.roll(second.reshape(n_bat, n_blk, second.shape[1]), n_blk - 1, axis=1)
        return _silu(first + nxt.reshape(second.shape) + pe[0:1, :]).astype(BF16)

    act_k, act_v = hidden(0, k_ref), hidden(1, v_ref)
    part = lambda a, g: a[:, g * PHI_HIDDEN:(g + 1) * PHI_HIDDEN]
    ks = jnp.concatenate([jnp.dot(part(act_k, g), w2k_ref[...], preferred_element_type=F32)
                          for g in range(NSA_KV_GROUPS)], axis=1)
    vts = jnp.concatenate([_dot_t(w2vt_ref[...], part(act_v, g)) for g in range(NSA_KV_GROUPS)],
                          axis=0)
    for bi in range(n_bat):
        ko_ref[bi] = ks[bi * n_blk:(bi + 1) * n_blk].astype(ko_ref.dtype)
        vto_ref[bi] = vts[:, bi * n_blk:(bi + 1) * n_blk].astype(vto_ref.dtype)


def _compress(src, k_col, v_col, pe_k, pe_v, k_w1, k_w2, v_w1, v_w2, batch, seq, n_bat=4):
    half = CMP_BLOCK // 2
    assert half == CMP_STRIDE and NSA_KV == LANES
    n_blk = seq // CMP_STRIDE
    hd = NSA_HEAD_DIM
    n_bat = min(n_bat, batch)
    assert batch % n_bat == 0
    w1 = jnp.stack([k_w1, v_w1]).reshape(2, 2, half, hd, PHI_HIDDEN).astype(BF16)
    zero = jnp.zeros_like(w1)
    per_group = [jnp.concatenate([w1 if g == col else zero for col in range(NSA_KV_GROUPS)], axis=-1)
                 for g in range(NSA_KV_GROUPS)]
    w1e = jnp.stack(per_group, axis=3).reshape(2, 2, half * NSA_KV, NSA_KV_GROUPS * PHI_HIDDEN)
    pe = jnp.stack([pe_k, pe_v]).reshape(2, 2, half, 1, hd)
    pe = jnp.broadcast_to(pe, (2, 2, half, NSA_KV_GROUPS, hd)).reshape(2, 2, 1, half * NSA_KV)
    pe = jnp.broadcast_to(pe, (2, 2, SUBLANES, half * NSA_KV)).astype(BF16)
    w2k = k_w2.astype(BF16)
    w2vt = v_w2.T.astype(BF16)
    whole = lambda a: pl.BlockSpec(a.shape, lambda b: (0,) * a.ndim)
    return pl.pallas_call(
        _compress_kernel,
        grid=(batch // n_bat,),
        in_specs=[pl.BlockSpec((n_bat * seq, LANES), lambda b: (b, k_col)),
                  pl.BlockSpec((n_bat * seq, LANES), lambda b: (b, v_col)),
                  whole(pe), whole(w1e), whole(w2k), whole(w2vt)],
        out_specs=[pl.BlockSpec((n_bat, n_blk, NSA_KV), lambda b: (b, 0, 0)),
                   pl.BlockSpec((n_bat, NSA_KV, n_blk), lambda b: (b, 0, 0))],
        out_shape=[jax.ShapeDtypeStruct((batch, n_blk, NSA_KV), BF16),
                   jax.ShapeDtypeStruct((batch, NSA_KV, n_blk), BF16)],
        compiler_params=_params("arbitrary"),
        name="compress",
    )(src, src, pe, w1e, w2k, w2vt)


KEY_CHUNK = 256
NSA_TQ = 256
SLOPE_PIECES = 3
N_FEATS = 2 * SLOPE_PIECES
FEAT_LANES = 32
KEY_COLS = NSA_HEAD_DIM + 2 * FEAT_LANES
N_SLC = 32


def _slope_pieces(slope):
    rest = np.float32(slope)
    pieces = []
    for _ in range(SLOPE_PIECES):
        p = np.float32(np.asarray(rest).astype(BF16))
        pieces.append(float(p))
        rest = np.float32(rest - p)
    return pieces


def _lane_table(lane, values):
    out = jnp.zeros(lane.shape, F32)
    for idx, v in enumerate(values):
        out = jnp.where(lane == idx, v, out)
    return out


def _key_feats(pos_hi, pos_lo, lane):
    return jnp.where(lane < SLOPE_PIECES, pos_hi, jnp.where(lane < N_FEATS, pos_lo, 0)).astype(F32)


def _tile_heads(x):
    return jnp.concatenate([x] * NSA_R, axis=1)


def _chunk_loop(lo, hi, body, init, widths=(4, 2, 1)):
    carry, start = init, lo
    for idx, w in enumerate(widths):
        count = (hi - start) // w

        def step(p, cr, start=start, w=w):
            first = start + p * w
            return body([first + j for j in range(w)], cr)

        if idx == 0:
            carry = lax.fori_loop(0, count, step, carry)
        else:
            carry = lax.cond(count > 0, functools.partial(step, 0), lambda cr: cr, carry)
        start = start + count * w
    return carry


def _nsa_kernel(q_ref, kc_ref, vct_ref, ksrc_ref, vsrc_ref, kwsrc_ref, vwsrc_ref, feat_ref, hot_ref,
                gl_ref, z_ref, ym_ref, w_ref, x_ref, fin_ref, o_ref,
                ks_ref, vst_ref, kw_ref, vwt_ref, s_ref, acc_ref, imp_ref):
    i = pl.program_id(1)
    tq = q_ref.shape[0]
    hd = NSA_HEAD_DIM
    n_cmp = kc_ref.shape[0]

    @pl.when(i == 0)
    def _():
        for g in range(NSA_KV_GROUPS):
            gs = slice(g * hd, (g + 1) * hd)
            for dst, src, tail in ((ks_ref, ksrc_ref, hot_ref[...]),
                                   (kw_ref, kwsrc_ref, jnp.zeros(hot_ref.shape, BF16))):
                dst[:, g * KEY_COLS:g * KEY_COLS + hd] = src[:, gs]
                dst[:, g * KEY_COLS + hd:g * KEY_COLS + hd + FEAT_LANES] = feat_ref[...]
                dst[:, g * KEY_COLS + hd + FEAT_LANES:(g + 1) * KEY_COLS] = tail
        for c in range(vst_ref.shape[0]):
            rows = slice(c * KEY_CHUNK, (c + 1) * KEY_CHUNK)
            vst_ref[c] = vsrc_ref[rows, :].astype(F32).T.astype(BF16)
            vwt_ref[c] = vwsrc_ref[rows, :].astype(F32).T.astype(BF16)

    slopes_all = _alibi_slopes(NSA_HEADS)
    gates_t = jax.nn.sigmoid(gl_ref[...]).T
    feat_lane = lax.broadcasted_iota(jnp.int32, (tq, FEAT_LANES), 1)
    no_sel = jnp.zeros((NSA_R * tq, FEAT_LANES), BF16)
    key_row = lax.broadcasted_iota(jnp.int32, (KEY_CHUNK, tq), 0)
    t_pos = i * tq + lax.broadcasted_iota(jnp.int32, (KEY_CHUNK, tq), 1)
    ones_rows = jnp.ones((BF16_SUBLANES, KEY_CHUNK), BF16)
    win_lo = jnp.maximum(i * tq - (WIN_SIZE - 1), 0) // KEY_CHUNK
    chunks_hi = (i * tq + tq - 1) // KEY_CHUNK + 1

    groups = range(NSA_KV_GROUPS)
    q_win, q_slc, o_cmp = [], [], []
    for g in groups:
        slopes = slopes_all[g * NSA_R:(g + 1) * NSA_R]
        gs = slice(g * hd, (g + 1) * hd)
        q_parts = []
        for r in range(NSA_R):
            qr = q_ref[:, (g * NSA_R + r) * hd:(g * NSA_R + r + 1) * hd]
            feats = _lane_table(feat_lane, _slope_pieces(slopes[r]) * 2).astype(BF16)
            q_parts.append(jnp.concatenate([qr, feats], axis=1))
        q_feat = jnp.concatenate(q_parts, axis=0)
        q_aug = jnp.concatenate([q_feat, no_sel], axis=1)

        n_row = lax.broadcasted_iota(jnp.int32, (n_cmp, tq), 0)
        t_cmp = i * tq + lax.broadcasted_iota(jnp.int32, (n_cmp, tq), 1)
        visible = t_cmp >= n_row * CMP_STRIDE + (CMP_BLOCK - 1)
        cfeat_row = lax.broadcasted_iota(jnp.int32, (n_cmp, 2 * FEAT_LANES), 0)
        cfeat_lane = lax.broadcasted_iota(jnp.int32, (n_cmp, 2 * FEAT_LANES), 1)
        kc_feats = _key_feats(cfeat_row * CMP_STRIDE, 0, cfeat_lane)
        kc_aug = jnp.concatenate([kc_ref[:, gs], kc_feats.astype(BF16)], axis=1)
        s = _dot_t(kc_aug, q_aug) + _tile_heads(jnp.where(visible, 0.0, NEG_INF))
        m = jnp.max(s, axis=0, keepdims=True)
        e = jnp.exp(s - m)
        t_one = i * tq + lax.broadcasted_iota(jnp.int32, (1, NSA_R * tq), 1) % tq
        any_visible = t_one >= (CMP_BLOCK - 1)
        p = e * jnp.where(any_visible, 1.0 / jnp.sum(e, axis=0, keepdims=True), 0.0)
        o_cmp.append(jnp.dot(vct_ref[gs, :], p.astype(BF16), preferred_element_type=F32))
        p_sum = p[:, 0:tq]
        for r in range(1, NSA_R):
            p_sum = p_sum + p[:, r * tq:(r + 1) * tq]

        band = p_sum + pltpu.roll(p_sum, 1, axis=0)
        for k in range(1, CMP_PER_SLC):
            band = band + pltpu.roll(p_sum, n_cmp - k, axis=0)
        halves = []
        for half in range(tq // LANES):
            imp_ref[...] = band[:, half * LANES:(half + 1) * LANES]
            halves.append(imp_ref[pl.ds(0, N_SLC, stride=CMP_PER_SLC), :])
        imp = jnp.concatenate(halves, axis=1)
        blk_j = lax.broadcasted_iota(jnp.int32, (N_SLC, tq), 0)
        cur = (i * tq + lax.broadcasted_iota(jnp.int32, (N_SLC, tq), 1)) // SLC_BLOCK
        forced = (blk_j == 0) | (blk_j == cur) | (blk_j == cur - 1)
        v_imp = jnp.where(forced, SEL_FORCE, jnp.where(blk_j > cur, -SEL_FORCE, imp))
        rank = jnp.zeros((N_SLC, tq), F32)
        for other in range(N_SLC):
            row = v_imp[other:other + 1, :]
            ahead = (row > v_imp) | ((row == v_imp) & (blk_j > other))
            rank = rank + jnp.where(ahead, 1.0, 0.0)
        sel_bias = jnp.where(rank < SLC_TOP_N, 0.0, NEG_INF)

        padded = jnp.concatenate([sel_bias, jnp.zeros((LANES - N_SLC, tq), F32)], axis=0)
        sel_t = padded.T[:, 0:FEAT_LANES].astype(BF16)
        q_win.append(q_aug)
        q_slc.append(jnp.concatenate([q_feat, jnp.concatenate([sel_t] * NSA_R, axis=0)], axis=1))

    def attend(q_brs, k_ref, key_cols, vt_ref, lo, hi, masked_from, mask_fn):
        def scores(cs, m_run, masked):
            starts = [pl.multiple_of(c * KEY_CHUNK, KEY_CHUNK) for c in cs]
            scs = [[_dot_t(k_ref[pl.ds(start, KEY_CHUNK), g * key_cols:(g + 1) * key_cols], q_brs[g])
                    for g in groups] for start in starts]
            for c, start, sc in zip(cs, starts, scs):
                if masked:
                    bias = _tile_heads(jnp.where(mask_fn(t_pos - (start + key_row)), 0.0, NEG_INF))
                    sc = [x + bias for x in sc]
                for g in groups:
                    s_ref[g, c] = sc[g]
                m_run = tuple(jnp.maximum(m_run[g], jnp.max(sc[g], axis=0, keepdims=True)) for g in groups)
            return m_run

        def weighted(cs, carry):
            for c in cs:
                for g in groups:
                    e = jnp.exp(s_ref[g, c] - m_rows[g]).astype(BF16)
                    v_ext = jnp.concatenate([vt_ref[c, g * hd:(g + 1) * hd, :], ones_rows], axis=0)
                    acc_ref[g] += jnp.dot(v_ext, e, preferred_element_type=F32)
            return carry

        m_rows = tuple(jnp.full((1, NSA_R * tq), NEG_INF, F32) for _ in groups)
        m_rows = _chunk_loop(lo, masked_from, functools.partial(scores, masked=False), m_rows)
        m_rows = _chunk_loop(masked_from, hi, functools.partial(scores, masked=True), m_rows)
        acc_ref[...] = jnp.zeros(acc_ref.shape, F32)
        _chunk_loop(lo, hi, weighted, 0)
        return [acc_ref[g, 0:hd, :] / acc_ref[g, hd:hd + 1, :] for g in groups]

    o_slc = attend(q_slc, ks_ref, KEY_COLS, vst_ref, 0, chunks_hi, (i * tq) // KEY_CHUNK,
                   lambda dist: dist >= 0)
    o_win = attend(q_win, kw_ref, KEY_COLS, vwt_ref, win_lo, chunks_hi, win_lo,
                   lambda dist: (dist >= 0) & (dist <= WIN_SIZE - 1))

    y = jnp.dot(ym_ref[...], w_ref[NSA_WIDTH:NSA_WIDTH + MEM_WIDTH, :], preferred_element_type=F32)
    for g in groups:
        def gate(kind):
            base = g * NSA_R * 3 + kind
            return jnp.concatenate([gates_t[base + 3 * r:base + 3 * r + 1, :] for r in range(NSA_R)], axis=1)

        o = gate(0) * o_cmp[g] + gate(1) * o_slc[g] + gate(2) * o_win[g]
        pairs = []
        for r in range(0, NSA_R, 2):
            two = jnp.concatenate([o[:, r * tq:(r + 1) * tq], o[:, (r + 1) * tq:(r + 2) * tq]], axis=0)
            pairs.append(two.T)
        cs = slice(g * NSA_R * hd, (g + 1) * NSA_R * hd)
        yo = (jnp.concatenate(pairs, axis=1) * _silu(z_ref[:, cs].astype(F32))).astype(BF16)
        y = y + jnp.dot(yo, w_ref[cs, :], preferred_element_type=F32)

    x = x_ref[...] + y
    ms = jnp.mean(x * x, axis=-1, keepdims=True)
    o_ref[...] = x * lax.rsqrt(ms + NORM_EPS) * fin_ref[...]


def _key_pos_feats(seq):
    assert seq // SLC_BLOCK == N_SLC <= FEAT_LANES
    pos = np.arange(seq)
    feats = np.zeros((seq, FEAT_LANES), np.float32)
    feats[:, 0:SLOPE_PIECES] = ((pos // SLC_BLOCK) * SLC_BLOCK)[:, None]
    feats[:, SLOPE_PIECES:N_FEATS] = (pos % SLC_BLOCK)[:, None]
    onehot = (np.arange(FEAT_LANES)[None, :] == (pos // SLC_BLOCK)[:, None]).astype(np.float32)
    return jnp.asarray(feats, BF16), jnp.asarray(onehot, BF16)


def _nsa_attn(nb, kv_col0, z_col, k_cmp, v_cmp_t, nf, gl_col, ym, w_out, x, final_g, batch, seq):
    tq = NSA_TQ
    nq = seq // tq
    d = x.shape[1]
    feats, onehot = _key_pos_feats(seq)
    kv_blk = kv_col0 // NSA_KV
    seq_cols = lambda col: pl.BlockSpec((seq, NSA_KV), lambda b, i: (b, kv_blk + col))
    const = lambda a: pl.BlockSpec(a.shape, lambda b, i: (0,) * a.ndim)
    per_batch = lambda a: pl.BlockSpec((None,) + a.shape[1:], lambda b, i: (b,) + (0,) * (a.ndim - 1))
    rows = lambda width, col=0: pl.BlockSpec((tq, width), lambda b, i: (b * nq + i, col))
    fin = final_g.reshape(1, d)
    return pl.pallas_call(
        _nsa_kernel,
        grid=(batch, nq),
        in_specs=[rows(NSA_WIDTH), per_batch(k_cmp), per_batch(v_cmp_t),
                  seq_cols(0), seq_cols(1), seq_cols(2), seq_cols(3), const(feats), const(onehot),
                  rows(LANES, gl_col), rows(NSA_WIDTH, z_col), rows(MEM_WIDTH), const(w_out), rows(d),
                  const(fin)],
        out_specs=rows(d),
        out_shape=jax.ShapeDtypeStruct((batch * seq, d), F32),
        scratch_shapes=[pltpu.VMEM((seq, NSA_KV_GROUPS * KEY_COLS), BF16),
                        pltpu.VMEM((seq // KEY_CHUNK, NSA_KV, KEY_CHUNK), BF16),
                        pltpu.VMEM((seq, NSA_KV_GROUPS * KEY_COLS), BF16),
                        pltpu.VMEM((seq // KEY_CHUNK, NSA_KV, KEY_CHUNK), BF16),
                        pltpu.VMEM((NSA_KV_GROUPS, seq // KEY_CHUNK, KEY_CHUNK, NSA_R * tq), F32),
                        pltpu.VMEM((NSA_KV_GROUPS, NSA_HEAD_DIM + BF16_SUBLANES, NSA_R * tq), F32),
                        pltpu.VMEM((seq // CMP_STRIDE, LANES), F32)],
        compiler_params=_params("arbitrary", "arbitrary"),
        name="nsa_attn",
    )(nb, k_cmp, v_cmp_t, nb, nb, nb, nb, feats, onehot, nf, nb, ym, w_out, x, fin)


def _hawk_layer(x, mem, batch, seq, norm_g, w_in, conv_w, conv_b, ga_w, ga_b, gx_w, gx_b, lam,
                mem_norm_g, w_mem_kv, w_out):
    xa0, za0 = 0, LRU_WIDTH
    q0 = 2 * LRU_WIDTH
    k0, v0 = q0 + DIL_QKV, q0 + 2 * DIL_QKV
    zb0 = q0 + 3 * DIL_QKV
    qm0 = zb0 + DIL_WIDTH
    zm0 = qm0 + MEM_WIDTH

    def qkv_cols(gi):
        return [(base + gi * DIL_WIDTH, DIL_WIDTH) for base in (q0, k0, v0)]

    w = w_in.astype(BF16)
    nat_cols = [(za0, LRU_WIDTH), *qkv_cols(0), (zb0, DIL_WIDTH), (qm0, MEM_WIDTH), (zm0, MEM_WIDTH)]
    xa, hb = _norm_matmul(x, norm_g, w, [(F32, [(xa0, LRU_WIDTH)]), (BF16, nat_cols)], row_parts=2)
    za_col = 0
    qkv0_col = LRU_WIDTH // DIL_WIDTH
    zb_col = (LRU_WIDTH + 3 * DIL_WIDTH) // DIL_WIDTH
    qm_col = (LRU_WIDTH + 4 * DIL_WIDTH) // MEM_WIDTH
    zm_col = qm_col + 1
    qkv = [(hb[None], qkv0_col)]
    for gi in range(1, len(DIL_GROUPS)):
        qkv.append((_norm_matmul(x, norm_g, w, [(BF16, qkv_cols(gi))], dil=DIL_GROUPS[gi][1],
                                 row_parts=4 * gi), 0))
    n_mem = mem.shape[0] // batch
    mem_kv, = _norm_matmul(mem, mem_norm_g, w_mem_kv.astype(BF16), [(BF16, [(0, 2 * MEM_WIDTH)])], row_parts=1)

    ya = _rglru(xa, hb, za_col, conv_w, conv_b, _pack_block_diag(ga_w), ga_b, _pack_block_diag(gx_w), gx_b,
                lam, batch, seq)
    attn = [_dil_attn(arr, col0, gi, batch, seq) for gi, (arr, col0) in enumerate(qkv)]
    ym = _mem_attn(hb, qm_col, mem_kv, hb, zm_col, batch, seq, n_mem)
    return _hawk_out(ya, [o for o, _ in attn], [l for _, l in attn], hb, zb_col, ym,
                     w_out.astype(BF16), x)


def _nsa_layer(x, mem, batch, seq, norm_g, w_in, pe_k, pe_v, phik_w1, phik_w2, phiv_w1, phiv_w2,
               mem_norm_g, w_mem_kv, w_out, final_g):
    kv0 = NSA_WIDTH
    gl0 = kv0 + 6 * NSA_KV
    z0 = gl0 + 3 * NSA_HEADS
    qm0 = z0 + NSA_WIDTH
    zm0 = qm0 + MEM_WIDTH
    wb = w_in.astype(BF16)
    gl_w = jnp.pad(wb[:, gl0:z0], ((0, 0), (0, LANES - 3 * NSA_HEADS)))
    q_w = wb[:, 0:kv0] * jnp.asarray(NSA_HEAD_DIM ** -0.5, BF16)
    w_all = jnp.concatenate([q_w, wb[:, z0:qm0], wb[:, kv0 + 2 * NSA_KV:gl0], wb[:, qm0:zm0 + MEM_WIDTH],
                             gl_w, wb[:, kv0:kv0 + 2 * NSA_KV]], axis=1)
    f32_width = LANES + 2 * NSA_KV
    bf16_width = w_all.shape[1] - f32_width
    nb, nf = _norm_matmul(x, norm_g, w_all, [(BF16, [(0, bf16_width)]), (F32, [(bf16_width, f32_width)])],
                          row_parts=4)
    gl_col, kc_col, vc_col = 0, 1, 2
    z_col = 1
    kv_col0 = 2 * NSA_WIDTH
    qm_col = (kv_col0 + 4 * NSA_KV) // MEM_WIDTH
    zm_col = qm_col + 1
    n_mem = mem.shape[0] // batch
    mem_kv, = _norm_matmul(mem, mem_norm_g, w_mem_kv.astype(BF16), [(BF16, [(0, 2 * MEM_WIDTH)])])

    k_cmp, v_cmp_t = _compress(nf, kc_col, vc_col, pe_k, pe_v, phik_w1, phik_w2, phiv_w1, phiv_w2,
                               batch, seq)
    ym = _mem_attn(nb, qm_col, mem_kv, nb, zm_col, batch, seq, n_mem)
    return _nsa_attn(nb, kv_col0, z_col, k_cmp, v_cmp_t, nf, gl_col, ym, w_out.astype(BF16), x, final_g,
                     batch, seq)


def kernel(x, mem, hawk_norm, hawk_w_in, hawk_conv_w, hawk_conv_b, hawk_gate_a_w, hawk_gate_a_b,
           hawk_gate_x_w, hawk_gate_x_b, hawk_lambda, hawk_mem_norm, hawk_w_mem_kv, hawk_w_out,
           nsa_norm, nsa_w_in, nsa_pe_k, nsa_pe_v, nsa_phi_k_w1, nsa_phi_k_w2, nsa_phi_v_w1,
           nsa_phi_v_w2, nsa_mem_norm, nsa_w_mem_kv, nsa_w_out, final_norm):
    batch, seq, d = x.shape
    assert hawk_norm.shape[0] == 1 and nsa_norm.shape[0] == 1, "one layer of each kind"
    assert seq % (ATTN_BLOCK * DIL_GROUPS[-1][1]) == 0
    x2 = x.reshape(batch * seq, d)
    mem2 = mem.reshape(batch * mem.shape[1], d)
    x2 = _hawk_layer(x2, mem2, batch, seq, hawk_norm[0], hawk_w_in[0], hawk_conv_w[0], hawk_conv_b[0],
                     hawk_gate_a_w[0], hawk_gate_a_b[0].reshape(-1), hawk_gate_x_w[0],
                     hawk_gate_x_b[0].reshape(-1), hawk_lambda[0], hawk_mem_norm[0], hawk_w_mem_kv[0],
                     hawk_w_out[0])
    out = _nsa_layer(x2, mem2, batch, seq, nsa_norm[0], nsa_w_in[0], nsa_pe_k[0], nsa_pe_v[0],
                     nsa_phi_k_w1[0], nsa_phi_k_w2[0], nsa_phi_v_w1[0], nsa_phi_v_w2[0],
                     nsa_mem_norm[0], nsa_w_mem_kv[0], nsa_w_out[0], final_norm)
    return out.reshape(batch, seq, d)
```

```python
import functools

import numpy as np
import jax
import jax.numpy as jnp
from jax import lax
from jax.experimental import pallas as pl
from jax.experimental.pallas import tpu as pltpu

F32 = jnp.float32
BF16 = jnp.bfloat16

NORM_EPS = 1e-6
NEG_INF = -1e30
LOG2_E = 1.4426950408889634
LANES = 128
SUBLANES = 8
BF16_SUBLANES = 16
ATTN_BLOCK = 128
VMEM_LIMIT = 56 * 1024 * 1024

LRU_WIDTH = 1024
LRU_BLOCKS = 16
LRU_BLOCK_DIM = LRU_WIDTH // LRU_BLOCKS
LRU_PACK = 256
CONV_WIDTH = 4
LRU_C = 8.0

DIL_GROUPS = ((128, 1), (512, 4), (2048, 16))
DIL_HEADS = 4
DIL_HEAD_DIM = 128
DIL_WIDTH = DIL_HEADS * DIL_HEAD_DIM
DIL_QKV = len(DIL_GROUPS) * DIL_WIDTH
LSE_LANES = LANES // DIL_HEADS
DIL_ROW_PARTS = 4

MEM_HEADS = 4
MEM_HEAD_DIM = 64
MEM_WIDTH = MEM_HEADS * MEM_HEAD_DIM

NSA_HEADS = 16
NSA_KV_GROUPS = 2
NSA_R = NSA_HEADS // NSA_KV_GROUPS
NSA_HEAD_DIM = 64
NSA_WIDTH = NSA_HEADS * NSA_HEAD_DIM
NSA_KV = NSA_KV_GROUPS * NSA_HEAD_DIM
CMP_BLOCK = 32
CMP_STRIDE = 16
SLC_BLOCK = 64
SLC_TOP_N = 8
WIN_SIZE = 512
PHI_HIDDEN = 256
SEL_FORCE = 1e6
CMP_PER_SLC = SLC_BLOCK // CMP_STRIDE


def _alibi_slopes(n):
    return [float(v) for v in np.exp2(-8.0 * np.arange(1, n + 1) / n).astype(np.float32)]


def _params(*semantics):
    return pltpu.CompilerParams(dimension_semantics=semantics, vmem_limit_bytes=VMEM_LIMIT)


def _silu(z):
    return z * jax.nn.sigmoid(z)


def _dot_t(a, b):
    return lax.dot_general(a, b, (((1,), (1,)), ((), ())), preferred_element_type=F32)


def _rms_norm_rows(x, g):
    ms = jnp.mean(x * x, axis=-1, keepdims=True)
    return (x * lax.rsqrt(ms + NORM_EPS) * g).astype(BF16)


def _norm_matmul_kernel(*refs, dil, pieces, row_parts):
    x_ref, g_ref = refs[:2]
    w_refs, o_refs = refs[2:2 + sum(pieces)], refs[2 + sum(pieces):]
    tm, k = x_ref.shape
    rows = tm // row_parts
    per = rows // dil
    for part in range(row_parts):
        x = x_ref[part * rows:(part + 1) * rows, :]
        if dil == 1:
            xn = _rms_norm_rows(x, g_ref[...])
        else:
            xn = x * lax.rsqrt(jnp.mean(x * x, axis=-1, keepdims=True) + NORM_EPS) * g_ref[...]
            xn = jnp.swapaxes(xn.reshape(per, dil, k), 0, 1).reshape(rows, k).astype(BF16)
        first = 0
        for o_ref, n_pieces in zip(o_refs, pieces):
            col = 0
            for w_ref in w_refs[first:first + n_pieces]:
                width = w_ref.shape[1]
                res = jnp.dot(xn, w_ref[...], preferred_element_type=F32).astype(o_ref.dtype)
                if dil == 1:
                    o_ref[part * rows:(part + 1) * rows, col:col + width] = res
                else:
                    for c in range(dil):
                        o_ref[c, part * per:(part + 1) * per, col:col + width] = res[c * per:(c + 1) * per]
                col += width
            first += n_pieces


def _norm_matmul(x, g, w, outs, dil=1, tm=1024):
    m, k = x.shape
    tm = min(tm, m)
    assert m % tm == 0 and k % LANES == 0
    pieces = [len(cols) for _, cols in outs]
    widths = [sum(width for _, width in cols) for _, cols in outs]
    w_specs = []
    for _, cols in outs:
        for col, width in cols:
            assert col % width == 0 and width % LANES == 0
            w_specs.append(pl.BlockSpec((k, width), functools.partial(lambda i, j: (0, j), j=col // width)))
    resident = [pl.BlockSpec((1, k), lambda i: (0, 0))] + w_specs
    operands = (g.reshape(1, k),) + (w,) * len(w_specs)
    row_parts = 1 if dil == 1 else DIL_ROW_PARTS
    assert tm % (dil * row_parts * SUBLANES) == 0
    kernel = functools.partial(_norm_matmul_kernel, dil=dil, pieces=pieces, row_parts=row_parts)
    if dil == 1:
        return pl.pallas_call(
            kernel,
            grid=(m // tm,),
            in_specs=[pl.BlockSpec((tm, k), lambda i: (i, 0))] + resident,
            out_specs=[pl.BlockSpec((tm, width), lambda i: (i, 0)) for width in widths],
            out_shape=[jax.ShapeDtypeStruct((m, width), dtype) for width, (dtype, _) in zip(widths, outs)],
            compiler_params=_params("arbitrary"),
            name="norm_matmul",
        )(x, *operands)
    per = tm // dil
    (out_dtype, _), = outs
    n, = widths
    assert tm % dil == 0 and per % BF16_SUBLANES == 0
    return pl.pallas_call(
        kernel,
        grid=(m // tm,),
        in_specs=[pl.BlockSpec((tm, k), lambda i: (i, 0))] + resident,
        out_specs=pl.BlockSpec((dil, per, n), lambda i: (0, i, 0)),
        out_shape=jax.ShapeDtypeStruct((dil, m // dil, n), out_dtype),
        compiler_params=_params("arbitrary"),
        name="norm_matmul_dil",
    )(x, *operands)


def _rglru_kernel(xa_ref, za_ref, cw_ref, cb_ref, wa_ref, ba_ref, wx_ref, bx_ref, lam_ref,
                  o_ref, xpad_ref, h_ref):
    t = pl.program_id(1)
    tt, width = xa_ref.shape
    halo = SUBLANES

    @pl.when(t == 0)
    def _():
        xpad_ref[0:halo, :] = jnp.zeros((halo, width), F32)
        h_ref[...] = jnp.zeros_like(h_ref)

    x = xa_ref[...]
    xpad_ref[halo:halo + tt, :] = x
    cw = cw_ref[...]
    y = cw[CONV_WIDTH - 1:CONV_WIDTH] * x
    for k in range(1, CONV_WIDTH):
        y = y + cw[CONV_WIDTH - 1 - k:CONV_WIDTH - k] * xpad_ref[halo - k:halo - k + tt, :]
    y = y + cb_ref[...]
    xpad_ref[0:halo, :] = x[tt - halo:tt, :]

    yb = y.astype(BF16)
    r_parts, i_parts = [], []
    for p in range(width // LRU_PACK):
        ys = yb[:, p * LRU_PACK:(p + 1) * LRU_PACK]
        r_parts.append(jnp.dot(ys, wa_ref[p], preferred_element_type=F32))
        i_parts.append(jnp.dot(ys, wx_ref[p], preferred_element_type=F32))
    r = jax.nn.sigmoid(jnp.concatenate(r_parts, axis=1) + ba_ref[...])
    gi = jax.nn.sigmoid(jnp.concatenate(i_parts, axis=1) + bx_ref[...])

    nl = -lam_ref[...]
    softplus = jnp.maximum(nl, 0.0) + jnp.log1p(jnp.exp(-jnp.abs(nl)))
    log_a = (-LRU_C) * r * softplus
    a = jnp.exp(log_a)
    w = -jnp.tanh(log_a) * (a * a + 1.0)
    mult = jnp.where(w > 0.0, w * lax.rsqrt(w), 0.0)
    b = y * gi * mult
    first = (lax.broadcasted_iota(jnp.int32, (SUBLANES, width), 0) == 0) & (t == 0)
    b = jnp.concatenate([jnp.where(first, (y * gi)[0:SUBLANES], b[0:SUBLANES]), b[SUBLANES:]], axis=0)

    blocks = width // LANES

    def time_major(v):
        cols = jnp.stack([v[:, j * LANES:(j + 1) * LANES] for j in range(blocks)], axis=0)
        return jnp.swapaxes(cols, 0, 1)

    a_t, b_t = time_major(a), time_major(b)
    state = h_ref[...].reshape(blocks, LANES)
    steps = []
    for step in range(tt):
        state = a_t[step] * state + b_t[step]
        steps.append(state)
    h_ref[...] = state.reshape(1, width)
    h_cols = jnp.swapaxes(jnp.stack(steps, axis=0), 0, 1)
    h = jnp.concatenate([h_cols[j] for j in range(blocks)], axis=1)
    o_ref[...] = (h * _silu(za_ref[...].astype(F32))).astype(o_ref.dtype)


def _rglru(xa, za_src, za_col, conv_w, conv_b, wa, ba, wx, bx, lam, batch, seq, tt=1024):
    width = LRU_WIDTH
    nt = seq // tt
    packs = width // LRU_PACK
    vec = pl.BlockSpec((1, width), lambda b, t: (0, 0))
    gate_w = pl.BlockSpec((packs, LRU_PACK, LRU_PACK), lambda b, t: (0, 0, 0))
    return pl.pallas_call(
        _rglru_kernel,
        grid=(batch, nt),
        in_specs=[pl.BlockSpec((tt, width), lambda b, t: (b * nt + t, 0)),
                  pl.BlockSpec((tt, width), lambda b, t: (b * nt + t, za_col)),
                  pl.BlockSpec((CONV_WIDTH, width), lambda b, t: (0, 0)),
                  vec, gate_w, vec, gate_w, vec, vec],
        out_specs=pl.BlockSpec((tt, width), lambda b, t: (b * nt + t, 0)),
        out_shape=jax.ShapeDtypeStruct((batch * seq, width), BF16),
        scratch_shapes=[pltpu.VMEM((tt + SUBLANES, width), F32), pltpu.VMEM((1, width), F32)],
        compiler_params=_params("arbitrary", "arbitrary"),
        name="rglru",
    )(xa, za_src, conv_w, conv_b.reshape(1, width), wa, ba.reshape(1, width), wx, bx.reshape(1, width),
      lam.reshape(1, width))


def _pack_block_diag(w):
    per = LRU_PACK // LRU_BLOCK_DIM
    w = w.reshape(LRU_BLOCKS // per, per, LRU_BLOCK_DIM, LRU_BLOCK_DIM)
    eye = jnp.eye(per, dtype=w.dtype)
    packed = w[:, :, :, None, :] * eye[None, :, None, :, None]
    return packed.reshape(LRU_BLOCKS // per, LRU_PACK, LRU_PACK).astype(BF16)


def _dil_attn_kernel(*refs, slopes, pos_scale, max_dist, has_halo, dil, n_cls, n_blk):
    if has_halo:
        q_ref, kh_ref, k_ref, vh_ref, v_ref = refs[:5]
        out_refs = refs[5:]
    else:
        q_ref, k_ref, v_ref = refs[:3]
        out_refs = refs[3:]
    n_out = DIL_HEADS + 1
    dst_refs = out_refs[:n_out]
    stage_refs = out_refs[n_out:] if dil > 1 else dst_refs
    first_super = pl.program_id(1) == 0
    cls0 = pl.program_id(2) * n_cls
    blk = ATTN_BLOCK
    scale = DIL_HEAD_DIM ** -0.5

    def band(width, halo_live):
        row = lax.broadcasted_iota(jnp.int32, (blk, width), 0)
        col = lax.broadcasted_iota(jnp.int32, (blk, width), 1)
        dist = (width - blk) + row - col
        valid = (dist >= 0) & (dist <= max_dist)
        if halo_live is not None:
            valid = valid & ((col >= blk) | halo_live)
        distf = (dist * pos_scale).astype(F32)
        return [jnp.where(valid, (-slope / scale) * distf, NEG_INF) for slope in slopes]

    bias_inner = band(2 * blk, None) if n_blk > 1 else None
    bias_first = band(2 * blk, jnp.logical_not(first_super)) if has_halo else band(blk, None)

    def scores(cc, jb):
        cur = slice(jb * blk, (jb + 1) * blk)
        bias = bias_inner if jb > 0 else bias_first
        out = []
        for h in range(DIL_HEADS):
            hs = slice(h * DIL_HEAD_DIM, (h + 1) * DIL_HEAD_DIM)
            q = q_ref[cc, cur, hs]
            if jb > 0:
                k = k_ref[cc, (jb - 1) * blk:(jb + 1) * blk, hs]
                v = v_ref[cc, (jb - 1) * blk:(jb + 1) * blk, hs]
            elif has_halo:
                k = jnp.concatenate([kh_ref[cc, :, hs], k_ref[cc, cur, hs]], axis=0)
                v = jnp.concatenate([vh_ref[cc, :, hs], v_ref[cc, cur, hs]], axis=0)
            else:
                k, v = k_ref[cc, cur, hs], v_ref[cc, cur, hs]
            out.append((_dot_t(q, k) + bias[h], v))
        return out

    def finish(cc, jb, pairs):
        where = (cls0 + cc, slice(jb * blk, (jb + 1) * blk)) if dil > 1 else (slice(jb * blk, (jb + 1) * blk),)
        lses = []
        for h, (s, v) in enumerate(pairs):
            m = jnp.max(s, axis=-1, keepdims=True)
            e = jnp.exp2((s - m) * (scale * LOG2_E))
            den = jnp.sum(e, axis=-1, keepdims=True)
            o = jnp.dot(e.astype(BF16), v, preferred_element_type=F32) / den
            stage_refs[h][where] = o.astype(stage_refs[h].dtype)
            lses.append(jnp.broadcast_to(m * scale + jnp.log(den), (blk, LSE_LANES)))
        stage_refs[DIL_HEADS][where] = jnp.concatenate(lses, axis=1)

    pending = None
    for cc in range(n_cls):
        for jb in range(n_blk):
            pairs = scores(cc, jb)
            if pending is not None:
                finish(*pending)
            pending = (cc, jb, pairs)
    finish(*pending)

    if dil > 1:
        @pl.when(pl.program_id(2) == pl.num_programs(2) - 1)
        def _():
            for stage, dst in zip(stage_refs, dst_refs):
                dst[...] = jnp.swapaxes(stage[...], 0, 1).reshape(dst.shape).astype(dst.dtype)


def _dil_attn(qkv, col0, gi, batch, seq, work=8):
    window, dil = DIL_GROUPS[gi]
    sub = seq // dil
    nb = sub // ATTN_BLOCK
    n_blk = min(work, nb)
    n_cls = min(work // n_blk, dil)
    n_super = nb // n_blk
    has_halo = n_super > 1
    span = n_blk * ATTN_BLOCK
    slopes = _alibi_slopes(len(DIL_GROUPS) * DIL_HEADS)[gi * DIL_HEADS:(gi + 1) * DIL_HEADS]
    cur = lambda col: pl.BlockSpec((n_cls, span, DIL_WIDTH), lambda b, i, c: (c, b * n_super + i, col0 + col))
    halo = lambda col: pl.BlockSpec(
        (n_cls, ATTN_BLOCK, DIL_WIDTH),
        lambda b, i, c: (c, jnp.maximum((b * n_super + i) * n_blk - 1, 0), col0 + col))
    if has_halo:
        in_specs = [cur(0), halo(1), cur(1), halo(2), cur(2)]
    else:
        in_specs = [cur(0), cur(1), cur(2)]
    n_out = DIL_HEADS + 1
    *o, lse = pl.pallas_call(
        functools.partial(_dil_attn_kernel, slopes=slopes, pos_scale=dil, max_dist=window // dil,
                          has_halo=has_halo, dil=dil, n_cls=n_cls, n_blk=n_blk),
        grid=(batch, n_super, dil // n_cls),
        in_specs=in_specs,
        out_specs=[pl.BlockSpec((span * dil, LANES), lambda b, i, c: (b * n_super + i, 0))] * n_out,
        out_shape=[jax.ShapeDtypeStruct((batch * seq, LANES), BF16)] * DIL_HEADS
                  + [jax.ShapeDtypeStruct((batch * seq, LANES), F32)],
        scratch_shapes=[pltpu.VMEM((dil, span, LANES), F32)] * (n_out if dil > 1 else 0),
        compiler_params=_params("arbitrary", "arbitrary", "arbitrary"),
        name=f"dil_attn_d{dil}",
    )(*([qkv] * len(in_specs)))
    return o, lse


def _mem_attn_kernel(q_ref, k_ref, v_ref, z_ref, o_ref, ks_ref, vt_ref):
    hd = MEM_HEAD_DIM
    n_mem = k_ref.shape[0]

    @pl.when(pl.program_id(1) == 0)
    def _():
        ks_ref[...] = (k_ref[...].astype(F32) * (hd ** -0.5)).astype(BF16)
        vt = v_ref[...].astype(F32).T.astype(BF16)
        for h in range(MEM_HEADS):
            vt_ref[h, 0:hd, :] = vt[h * hd:(h + 1) * hd, :]
            vt_ref[h, hd:, :] = jnp.ones((vt_ref.shape[1] - hd, n_mem), BF16)

    heads = [slice(h * hd, (h + 1) * hd) for h in range(MEM_HEADS)]
    scores = [_dot_t(ks_ref[:, hs], q_ref[:, hs]) for hs in heads]
    outs = []
    for h, s in enumerate(scores):
        e = jnp.exp(s - jnp.max(s, axis=0, keepdims=True)).astype(BF16)
        acc = jnp.dot(vt_ref[h], e, preferred_element_type=F32)
        outs.append(acc[0:hd, :] / acc[hd:hd + 1, :])
    o = jnp.concatenate(outs, axis=0).T
    o_ref[...] = (o * _silu(z_ref[...].astype(F32))).astype(o_ref.dtype)


def _mem_attn(qsrc, q_col, kv, zsrc, z_col, batch, seq, n_mem, tq=1024):
    nq = seq // tq
    return pl.pallas_call(
        _mem_attn_kernel,
        grid=(batch, nq),
        in_specs=[pl.BlockSpec((tq, MEM_WIDTH), lambda b, i: (b * nq + i, q_col)),
                  pl.BlockSpec((n_mem, MEM_WIDTH), lambda b, i: (b, 0)),
                  pl.BlockSpec((n_mem, MEM_WIDTH), lambda b, i: (b, 1)),
                  pl.BlockSpec((tq, MEM_WIDTH), lambda b, i: (b * nq + i, z_col))],
        out_specs=pl.BlockSpec((tq, MEM_WIDTH), lambda b, i: (b * nq + i, 0)),
        out_shape=jax.ShapeDtypeStruct((batch * seq, MEM_WIDTH), BF16),
        scratch_shapes=[pltpu.VMEM((n_mem, MEM_WIDTH), BF16),
                        pltpu.VMEM((MEM_HEADS, MEM_HEAD_DIM + BF16_SUBLANES, n_mem), BF16)],
        compiler_params=_params("arbitrary", "arbitrary"),
        name="mem_attn",
    )(qsrc, kv, kv, zsrc)


def _hawk_out_kernel(*refs):
    n_groups = len(DIL_GROUPS)
    ya_ref = refs[0]
    o_refs = refs[1:1 + n_groups * DIL_HEADS]
    l_refs = refs[1 + n_groups * DIL_HEADS:1 + n_groups * (DIL_HEADS + 1)]
    zb_ref, ym_ref, w_ref, x_ref, out_ref = refs[1 + n_groups * (DIL_HEADS + 1):]
    a_end = LRU_WIDTH
    b_end = a_end + DIL_WIDTH
    y = jnp.dot(ya_ref[...], w_ref[0:a_end, :], preferred_element_type=F32)
    y = y + jnp.dot(ym_ref[...], w_ref[b_end:b_end + MEM_WIDTH, :], preferred_element_type=F32)
    parts = []
    for h in range(DIL_HEADS):
        ls = [l[:, h * LSE_LANES:h * LSE_LANES + 1] for l in l_refs]
        m = functools.reduce(jnp.maximum, ls)
        ws = [jnp.exp(l - m) for l in ls]
        num = sum(w * o_refs[gi * DIL_HEADS + h][...].astype(F32) for gi, w in enumerate(ws))
        parts.append(num / sum(ws))
    yb = (jnp.concatenate(parts, axis=1) * _silu(zb_ref[...].astype(F32))).astype(BF16)
    y = y + jnp.dot(yb, w_ref[a_end:b_end, :], preferred_element_type=F32)
    out_ref[...] = x_ref[...] + y


def _hawk_out(ya, os_, ls_, zb_src, zb_col, ym, w, x, tm=1024):
    m, d = x.shape
    row = lambda width, col=0: pl.BlockSpec((tm, width), lambda i: (i, col))
    heads = [o for group in os_ for o in group]
    return pl.pallas_call(
        _hawk_out_kernel,
        grid=(m // tm,),
        in_specs=[row(LRU_WIDTH)] + [row(DIL_HEAD_DIM)] * len(heads) + [row(LANES)] * len(ls_)
                 + [row(DIL_WIDTH, zb_col), row(MEM_WIDTH),
                    pl.BlockSpec(w.shape, lambda i: (0, 0)), row(d)],
        out_specs=row(d),
        out_shape=jax.ShapeDtypeStruct((m, d), F32),
        compiler_params=_params("arbitrary"),
        name="hawk_out",
    )(ya, *heads, *ls_, zb_src, ym, w, x)


def _compress_kernel(k_ref, v_ref, pe_ref, w1_ref, w2k_ref, w2vt_ref, ko_ref, vto_ref):
    n_bat, n_blk = ko_ref.shape[0], ko_ref.shape[1]
    seq = n_blk * CMP_STRIDE

    def hidden(which, src_ref):
        x = jnp.concatenate(
            [jnp.concatenate([src_ref[pl.ds(bi * seq + p, n_blk, stride=CMP_STRIDE), :]
                              for p in range(CMP_STRIDE)], axis=1) for bi in range(n_bat)], axis=0).astype(BF16)
        first = jnp.dot(x, w1_ref[which, 0], preferred_element_type=F32)
        second = jnp.dot(x, w1_ref[which, 1], preferred_element_type=F32)
        pe = (jnp.dot(pe_ref[which, 0], w1_ref[which, 0], preferred_element_type=F32)
              + jnp.dot(pe_ref[which, 1], w1_ref[which, 1], preferred_element_type=F32))
        nxt = pltpu.roll(second.reshape(n_bat, n_blk, second.shape[1]), n_blk - 1, axis=1)
        return _silu(first + nxt.reshape(second.shape) + pe[0:1, :]).astype(BF16)

    act_k, act_v = hidden(0, k_ref), hidden(1, v_ref)
    part = lambda a, g: a[:, g * PHI_HIDDEN:(g + 1) * PHI_HIDDEN]
    ks = jnp.concatenate([jnp.dot(part(act_k, g), w2k_ref[...], preferred_element_type=F32)
                          for g in range(NSA_KV_GROUPS)], axis=1)
    vts = jnp.concatenate([_dot_t(w2vt_ref[...], part(act_v, g)) for g in range(NSA_KV_GROUPS)],
                          axis=0)
    for bi in range(n_bat):
        ko_ref[bi] = ks[bi * n_blk:(bi + 1) * n_blk].astype(ko_ref.dtype)
        vto_ref[bi] = vts[:, bi * n_blk:(bi + 1) * n_blk].astype(vto_ref.dtype)


def _compress(src, k_col, v_col, pe_k, pe_v, k_w1, k_w2, v_w1, v_w2, batch, seq, n_bat=4):
    half = CMP_BLOCK // 2
    assert half == CMP_STRIDE and NSA_KV == LANES
    n_blk = seq // CMP_STRIDE
    hd = NSA_HEAD_DIM
    n_bat = min(n_bat, batch)
    assert batch % n_bat == 0
    w1 = jnp.stack([k_w1, v_w1]).reshape(2, 2, half, hd, PHI_HIDDEN).astype(BF16)
    zero = jnp.zeros_like(w1)
    per_group = [jnp.concatenate([w1 if g == col else zero for col in range(NSA_KV_GROUPS)], axis=-1)
                 for g in range(NSA_KV_GROUPS)]
    w1e = jnp.stack(per_group, axis=3).reshape(2, 2, half * NSA_KV, NSA_KV_GROUPS * PHI_HIDDEN)
    pe = jnp.stack([pe_k, pe_v]).reshape(2, 2, half, 1, hd)
    pe = jnp.broadcast_to(pe, (2, 2, half, NSA_KV_GROUPS, hd)).reshape(2, 2, 1, half * NSA_KV)
    pe = jnp.broadcast_to(pe, (2, 2, SUBLANES, half * NSA_KV)).astype(BF16)
    w2k = k_w2.astype(BF16)
    w2vt = v_w2.T.astype(BF16)
    whole = lambda a: pl.BlockSpec(a.shape, lambda b: (0,) * a.ndim)
    return pl.pallas_call(
        _compress_kernel,
        grid=(batch // n_bat,),
        in_specs=[pl.BlockSpec((n_bat * seq, LANES), lambda b: (b, k_col)),
                  pl.BlockSpec((n_bat * seq, LANES), lambda b: (b, v_col)),
                  whole(pe), whole(w1e), whole(w2k), whole(w2vt)],
        out_specs=[pl.BlockSpec((n_bat, n_blk, NSA_KV), lambda b: (b, 0, 0)),
                   pl.BlockSpec((n_bat, NSA_KV, n_blk), lambda b: (b, 0, 0))],
        out_shape=[jax.ShapeDtypeStruct((batch, n_blk, NSA_KV), BF16),
                   jax.ShapeDtypeStruct((batch, NSA_KV, n_blk), BF16)],
        compiler_params=_params("arbitrary"),
        name="compress",
    )(src, src, pe, w1e, w2k, w2vt)


KEY_CHUNK = 256
NSA_TQ = 256
SLOPE_PIECES = 3
N_FEATS = 2 * SLOPE_PIECES
FEAT_LANES = 32
KEY_COLS = NSA_HEAD_DIM + 2 * FEAT_LANES
N_SLC = 32


def _slope_pieces(slope):
    rest = np.float32(slope)
    pieces = []
    for _ in range(SLOPE_PIECES):
        p = np.float32(np.asarray(rest).astype(BF16))
        pieces.append(float(p))
        rest = np.float32(rest - p)
    return pieces


def _lane_table(lane, values):
    out = jnp.zeros(lane.shape, F32)
    for idx, v in enumerate(values):
        out = jnp.where(lane == idx, v, out)
    return out


def _key_feats(pos_hi, pos_lo, lane):
    return jnp.where(lane < SLOPE_PIECES, pos_hi, jnp.where(lane < N_FEATS, pos_lo, 0)).astype(F32)


def _tile_heads(x):
    return jnp.concatenate([x] * NSA_R, axis=1)


def _chunk_loop(lo, hi, body, init, widths=(4, 2, 1)):
    carry, start = init, lo
    for idx, w in enumerate(widths):
        count = (hi - start) // w

        def step(p, cr, start=start, w=w):
            first = start + p * w
            return body([first + j for j in range(w)], cr)

        if idx == 0:
            carry = lax.fori_loop(0, count, step, carry)
        else:
            carry = lax.cond(count > 0, functools.partial(step, 0), lambda cr: cr, carry)
        start = start + count * w
    return carry


def _nsa_kernel(q_ref, kc_ref, vct_ref, ksrc_ref, vsrc_ref, kwsrc_ref, vwsrc_ref, feat_ref, hot_ref,
                gl_ref, z_ref, ym_ref, w_ref, x_ref, fin_ref, o_ref,
                ks_ref, vst_ref, kw_ref, vwt_ref, s_ref, acc_ref, imp_ref):
    i = pl.program_id(1)
    tq = q_ref.shape[0]
    hd = NSA_HEAD_DIM
    n_cmp = kc_ref.shape[0]

    @pl.when(i == 0)
    def _():
        for g in range(NSA_KV_GROUPS):
            gs = slice(g * hd, (g + 1) * hd)
            for dst, src, tail in ((ks_ref, ksrc_ref, hot_ref[...]),
                                   (kw_ref, kwsrc_ref, jnp.zeros(hot_ref.shape, BF16))):
                dst[:, g * KEY_COLS:g * KEY_COLS + hd] = src[:, gs]
                dst[:, g * KEY_COLS + hd:g * KEY_COLS + hd + FEAT_LANES] = feat_ref[...]
                dst[:, g * KEY_COLS + hd + FEAT_LANES:(g + 1) * KEY_COLS] = tail
        for c in range(vst_ref.shape[0]):
            rows = slice(c * KEY_CHUNK, (c + 1) * KEY_CHUNK)
            vst_ref[c] = vsrc_ref[rows, :].astype(F32).T.astype(BF16)
            vwt_ref[c] = vwsrc_ref[rows, :].astype(F32).T.astype(BF16)

    slopes_all = _alibi_slopes(NSA_HEADS)
    gates_t = jax.nn.sigmoid(gl_ref[...]).T
    feat_lane = lax.broadcasted_iota(jnp.int32, (tq, FEAT_LANES), 1)
    no_sel = jnp.zeros((NSA_R * tq, FEAT_LANES), BF16)
    key_row = lax.broadcasted_iota(jnp.int32, (KEY_CHUNK, tq), 0)
    t_pos = i * tq + lax.broadcasted_iota(jnp.int32, (KEY_CHUNK, tq), 1)
    ones_rows = jnp.ones((BF16_SUBLANES, KEY_CHUNK), BF16)
    win_lo = jnp.maximum(i * tq - (WIN_SIZE - 1), 0) // KEY_CHUNK
    chunks_hi = (i * tq + tq - 1) // KEY_CHUNK + 1

    groups = range(NSA_KV_GROUPS)
    q_win, q_slc, o_cmp = [], [], []
    for g in groups:
        slopes = slopes_all[g * NSA_R:(g + 1) * NSA_R]
        gs = slice(g * hd, (g + 1) * hd)
        q_parts = []
        for r in range(NSA_R):
            qr = q_ref[:, (g * NSA_R + r) * hd:(g * NSA_R + r + 1) * hd]
            feats = _lane_table(feat_lane, _slope_pieces(slopes[r]) * 2).astype(BF16)
            q_parts.append(jnp.concatenate([qr, feats], axis=1))
        q_feat = jnp.concatenate(q_parts, axis=0)
        q_aug = jnp.concatenate([q_feat, no_sel], axis=1)

        n_row = lax.broadcasted_iota(jnp.int32, (n_cmp, tq), 0)
        t_cmp = i * tq + lax.broadcasted_iota(jnp.int32, (n_cmp, tq), 1)
        visible = t_cmp >= n_row * CMP_STRIDE + (CMP_BLOCK - 1)
        cfeat_row = lax.broadcasted_iota(jnp.int32, (n_cmp, 2 * FEAT_LANES), 0)
        cfeat_lane = lax.broadcasted_iota(jnp.int32, (n_cmp, 2 * FEAT_LANES), 1)
        kc_feats = _key_feats(cfeat_row * CMP_STRIDE, 0, cfeat_lane)
        kc_aug = jnp.concatenate([kc_ref[:, gs], kc_feats.astype(BF16)], axis=1)
        s = _dot_t(kc_aug, q_aug) + _tile_heads(jnp.where(visible, 0.0, NEG_INF))
        m = jnp.max(s, axis=0, keepdims=True)
        e = jnp.exp(s - m)
        t_one = i * tq + lax.broadcasted_iota(jnp.int32, (1, NSA_R * tq), 1) % tq
        any_visible = t_one >= (CMP_BLOCK - 1)
        p = e * jnp.where(any_visible, 1.0 / jnp.sum(e, axis=0, keepdims=True), 0.0)
        o_cmp.append(jnp.dot(vct_ref[gs, :], p.astype(BF16), preferred_element_type=F32))
        p_sum = p[:, 0:tq]
        for r in range(1, NSA_R):
            p_sum = p_sum + p[:, r * tq:(r + 1) * tq]

        band = p_sum + pltpu.roll(p_sum, 1, axis=0)
        for k in range(1, CMP_PER_SLC):
            band = band + pltpu.roll(p_sum, n_cmp - k, axis=0)
        halves = []
        for half in range(tq // LANES):
            imp_ref[...] = band[:, half * LANES:(half + 1) * LANES]
            halves.append(imp_ref[pl.ds(0, N_SLC, stride=CMP_PER_SLC), :])
        imp = jnp.concatenate(halves, axis=1)
        blk_j = lax.broadcasted_iota(jnp.int32, (N_SLC, tq), 0)
        cur = (i * tq + lax.broadcasted_iota(jnp.int32, (N_SLC, tq), 1)) // SLC_BLOCK
        forced = (blk_j == 0) | (blk_j == cur) | (blk_j == cur - 1)
        v_imp = jnp.where(forced, SEL_FORCE, jnp.where(blk_j > cur, -SEL_FORCE, imp))
        rank = jnp.zeros((N_SLC, tq), F32)
        for other in range(N_SLC):
            row = v_imp[other:other + 1, :]
            ahead = (row > v_imp) | ((row == v_imp) & (blk_j > other))
            rank = rank + jnp.where(ahead, 1.0, 0.0)
        sel_bias = jnp.where(rank < SLC_TOP_N, 0.0, NEG_INF)

        padded = jnp.concatenate([sel_bias, jnp.zeros((LANES - N_SLC, tq), F32)], axis=0)
        sel_t = padded.T[:, 0:FEAT_LANES].astype(BF16)
        q_win.append(q_aug)
        q_slc.append(jnp.concatenate([q_feat, jnp.concatenate([sel_t] * NSA_R, axis=0)], axis=1))

    def attend(q_brs, k_ref, key_cols, vt_ref, lo, hi, masked_from, mask_fn):
        def scores(cs, m_run, masked):
            starts = [pl.multiple_of(c * KEY_CHUNK, KEY_CHUNK) for c in cs]
            scs = [[_dot_t(k_ref[pl.ds(start, KEY_CHUNK), g * key_cols:(g + 1) * key_cols], q_brs[g])
                    for g in groups] for start in starts]
            for c, start, sc in zip(cs, starts, scs):
                if masked:
                    bias = _tile_heads(jnp.where(mask_fn(t_pos - (start + key_row)), 0.0, NEG_INF))
                    sc = [x + bias for x in sc]
                for g in groups:
                    s_ref[g, c] = sc[g]
                m_run = tuple(jnp.maximum(m_run[g], jnp.max(sc[g], axis=0, keepdims=True)) for g in groups)
            return m_run

        def weighted(cs, carry):
            for c in cs:
                for g in groups:
                    e = jnp.exp(s_ref[g, c] - m_rows[g]).astype(BF16)
                    v_ext = jnp.concatenate([vt_ref[c, g * hd:(g + 1) * hd, :], ones_rows], axis=0)
                    acc_ref[g] += jnp.dot(v_ext, e, preferred_element_type=F32)
            return carry

        m_rows = tuple(jnp.full((1, NSA_R * tq), NEG_INF, F32) for _ in groups)
        m_rows = _chunk_loop(lo, masked_from, functools.partial(scores, masked=False), m_rows)
        m_rows = _chunk_loop(masked_from, hi, functools.partial(scores, masked=True), m_rows)
        acc_ref[...] = jnp.zeros(acc_ref.shape, F32)
        _chunk_loop(lo, hi, weighted, 0)
        return [acc_ref[g, 0:hd, :] / acc_ref[g, hd:hd + 1, :] for g in groups]

    o_slc = attend(q_slc, ks_ref, KEY_COLS, vst_ref, 0, chunks_hi, (i * tq) // KEY_CHUNK,
                   lambda dist: dist >= 0)
    o_win = attend(q_win, kw_ref, KEY_COLS, vwt_ref, win_lo, chunks_hi, win_lo,
                   lambda dist: (dist >= 0) & (dist <= WIN_SIZE - 1))

    y = jnp.dot(ym_ref[...], w_ref[NSA_WIDTH:NSA_WIDTH + MEM_WIDTH, :], preferred_element_type=F32)
    for g in groups:
        def gate(kind):
            base = g * NSA_R * 3 + kind
            return jnp.concatenate([gates_t[base + 3 * r:base + 3 * r + 1, :] for r in range(NSA_R)], axis=1)

        o = gate(0) * o_cmp[g] + gate(1) * o_slc[g] + gate(2) * o_win[g]
        pairs = []
        for r in range(0, NSA_R, 2):
            two = jnp.concatenate([o[:, r * tq:(r + 1) * tq], o[:, (r + 1) * tq:(r + 2) * tq]], axis=0)
            pairs.append(two.T)
        cs = slice(g * NSA_R * hd, (g + 1) * NSA_R * hd)
        yo = (jnp.concatenate(pairs, axis=1) * _silu(z_ref[:, cs].astype(F32))).astype(BF16)
        y = y + jnp.dot(yo, w_ref[cs, :], preferred_element_type=F32)

    x = x_ref[...] + y
    ms = jnp.mean(x * x, axis=-1, keepdims=True)
    o_ref[...] = x * lax.rsqrt(ms + NORM_EPS) * fin_ref[...]


def _key_pos_feats(seq):
    assert seq // SLC_BLOCK == N_SLC <= FEAT_LANES
    pos = np.arange(seq)
    feats = np.zeros((seq, FEAT_LANES), np.float32)
    feats[:, 0:SLOPE_PIECES] = ((pos // SLC_BLOCK) * SLC_BLOCK)[:, None]
    feats[:, SLOPE_PIECES:N_FEATS] = (pos % SLC_BLOCK)[:, None]
    onehot = (np.arange(FEAT_LANES)[None, :] == (pos // SLC_BLOCK)[:, None]).astype(np.float32)
    return jnp.asarray(feats, BF16), jnp.asarray(onehot, BF16)


def _nsa_attn(nb, kv_col0, z_col, k_cmp, v_cmp_t, nf, gl_col, ym, w_out, x, final_g, batch, seq):
    tq = NSA_TQ
    nq = seq // tq
    d = x.shape[1]
    feats, onehot = _key_pos_feats(seq)
    kv_blk = kv_col0 // NSA_KV
    seq_cols = lambda col: pl.BlockSpec((seq, NSA_KV), lambda b, i: (b, kv_blk + col))
    const = lambda a: pl.BlockSpec(a.shape, lambda b, i: (0,) * a.ndim)
    per_batch = lambda a: pl.BlockSpec((None,) + a.shape[1:], lambda b, i: (b,) + (0,) * (a.ndim - 1))
    rows = lambda width, col=0: pl.BlockSpec((tq, width), lambda b, i: (b * nq + i, col))
    fin = final_g.reshape(1, d)
    return pl.pallas_call(
        _nsa_kernel,
        grid=(batch, nq),
        in_specs=[rows(NSA_WIDTH), per_batch(k_cmp), per_batch(v_cmp_t),
                  seq_cols(0), seq_cols(1), seq_cols(2), seq_cols(3), const(feats), const(onehot),
                  rows(LANES, gl_col), rows(NSA_WIDTH, z_col), rows(MEM_WIDTH), const(w_out), rows(d),
                  const(fin)],
        out_specs=rows(d),
        out_shape=jax.ShapeDtypeStruct((batch * seq, d), F32),
        scratch_shapes=[pltpu.VMEM((seq, NSA_KV_GROUPS * KEY_COLS), BF16),
                        pltpu.VMEM((seq // KEY_CHUNK, NSA_KV, KEY_CHUNK), BF16),
                        pltpu.VMEM((seq, NSA_KV_GROUPS * KEY_COLS), BF16),
                        pltpu.VMEM((seq // KEY_CHUNK, NSA_KV, KEY_CHUNK), BF16),
                        pltpu.VMEM((NSA_KV_GROUPS, seq // KEY_CHUNK, KEY_CHUNK, NSA_R * tq), F32),
                        pltpu.VMEM((NSA_KV_GROUPS, NSA_HEAD_DIM + BF16_SUBLANES, NSA_R * tq), F32),
                        pltpu.VMEM((seq // CMP_STRIDE, LANES), F32)],
        compiler_params=_params("arbitrary", "arbitrary"),
        name="nsa_attn",
    )(nb, k_cmp, v_cmp_t, nb, nb, nb, nb, feats, onehot, nf, nb, ym, w_out, x, fin)


def _hawk_layer(x, mem, batch, seq, norm_g, w_in, conv_w, conv_b, ga_w, ga_b, gx_w, gx_b, lam,
                mem_norm_g, w_mem_kv, w_out):
    xa0, za0 = 0, LRU_WIDTH
    q0 = 2 * LRU_WIDTH
    k0, v0 = q0 + DIL_QKV, q0 + 2 * DIL_QKV
    zb0 = q0 + 3 * DIL_QKV
    qm0 = zb0 + DIL_WIDTH
    zm0 = qm0 + MEM_WIDTH

    def qkv_cols(gi):
        return [(base + gi * DIL_WIDTH, DIL_WIDTH) for base in (q0, k0, v0)]

    w = w_in.astype(BF16)
    nat_cols = [(za0, LRU_WIDTH), *qkv_cols(0), (zb0, DIL_WIDTH), (qm0, MEM_WIDTH), (zm0, MEM_WIDTH)]
    xa, hb = _norm_matmul(x, norm_g, w, [(F32, [(xa0, LRU_WIDTH)]), (BF16, nat_cols)])
    za_col = 0
    qkv0_col = LRU_WIDTH // DIL_WIDTH
    zb_col = (LRU_WIDTH + 3 * DIL_WIDTH) // DIL_WIDTH
    qm_col = (LRU_WIDTH + 4 * DIL_WIDTH) // MEM_WIDTH
    zm_col = qm_col + 1
    qkv = [(hb[None], qkv0_col)]
    for gi in range(1, len(DIL_GROUPS)):
        qkv.append((_norm_matmul(x, norm_g, w, [(BF16, qkv_cols(gi))], dil=DIL_GROUPS[gi][1]), 0))
    n_mem = mem.shape[0] // batch
    mem_kv, = _norm_matmul(mem, mem_norm_g, w_mem_kv.astype(BF16), [(BF16, [(0, 2 * MEM_WIDTH)])])

    ya = _rglru(xa, hb, za_col, conv_w, conv_b, _pack_block_diag(ga_w), ga_b, _pack_block_diag(gx_w), gx_b,
                lam, batch, seq)
    attn = [_dil_attn(arr, col0, gi, batch, seq) for gi, (arr, col0) in enumerate(qkv)]
    ym = _mem_attn(hb, qm_col, mem_kv, hb, zm_col, batch, seq, n_mem)
    return _hawk_out(ya, [o for o, _ in attn], [l for _, l in attn], hb, zb_col, ym,
                     w_out.astype(BF16), x)


def _nsa_layer(x, mem, batch, seq, norm_g, w_in, pe_k, pe_v, phik_w1, phik_w2, phiv_w1, phiv_w2,
               mem_norm_g, w_mem_kv, w_out, final_g):
    kv0 = NSA_WIDTH
    gl0 = kv0 + 6 * NSA_KV
    z0 = gl0 + 3 * NSA_HEADS
    qm0 = z0 + NSA_WIDTH
    zm0 = qm0 + MEM_WIDTH
    wb = w_in.astype(BF16)
    gl_w = jnp.pad(wb[:, gl0:z0], ((0, 0), (0, LANES - 3 * NSA_HEADS)))
    q_w = wb[:, 0:kv0] * jnp.asarray(NSA_HEAD_DIM ** -0.5, BF16)
    w_all = jnp.concatenate([q_w, wb[:, z0:qm0], wb[:, kv0 + 2 * NSA_KV:gl0], wb[:, qm0:zm0 + MEM_WIDTH],
                             gl_w, wb[:, kv0:kv0 + 2 * NSA_KV]], axis=1)
    f32_width = LANES + 2 * NSA_KV
    bf16_width = w_all.shape[1] - f32_width
    nb, nf = _norm_matmul(x, norm_g, w_all, [(BF16, [(0, bf16_width)]), (F32, [(bf16_width, f32_width)])])
    gl_col, kc_col, vc_col = 0, 1, 2
    z_col = 1
    kv_col0 = 2 * NSA_WIDTH
    qm_col = (kv_col0 + 4 * NSA_KV) // MEM_WIDTH
    zm_col = qm_col + 1
    n_mem = mem.shape[0] // batch
    mem_kv, = _norm_matmul(mem, mem_norm_g, w_mem_kv.astype(BF16), [(BF16, [(0, 2 * MEM_WIDTH)])])

    k_cmp, v_cmp_t = _compress(nf, kc_col, vc_col, pe_k, pe_v, phik_w1, phik_w2, phiv_w1, phiv_w2,
                               batch, seq)
    ym = _mem_attn(nb, qm_col, mem_kv, nb, zm_col, batch, seq, n_mem)
    return _nsa_attn(nb, kv_col0, z_col, k_cmp, v_cmp_t, nf, gl_col, ym, w_out.astype(BF16), x, final_g,
                     batch, seq)


def kernel(x, mem, hawk_norm, hawk_w_in, hawk_conv_w, hawk_conv_b, hawk_gate_a_w, hawk_gate_a_b,
           hawk_gate_x_w, hawk_gate_x_b, hawk_lambda, hawk_mem_norm, hawk_w_mem_kv, hawk_w_out,
           nsa_norm, nsa_w_in, nsa_pe_k, nsa_pe_v, nsa_phi_k_w1, nsa_phi_k_w2, nsa_phi_v_w1,
           nsa_phi_v_w2, nsa_mem_norm, nsa_w_mem_kv, nsa_w_out, final_norm):
    batch, seq, d = x.shape
    assert hawk_norm.shape[0] == 1 and nsa_norm.shape[0] == 1, "one layer of each kind"
    assert seq % (ATTN_BLOCK * DIL_GROUPS[-1][1]) == 0
    x2 = x.reshape(batch * seq, d)
    mem2 = mem.reshape(batch * mem.shape[1], d)
    x2 = _hawk_layer(x2, mem2, batch, seq, hawk_norm[0], hawk_w_in[0], hawk_conv_w[0], hawk_conv_b[0],
                     hawk_gate_a_w[0], hawk_gate_a_b[0].reshape(-1), hawk_gate_x_w[0],
                     hawk_gate_x_b[0].reshape(-1), hawk_lambda[0], hawk_mem_norm[0], hawk_w_mem_kv[0],
                     hawk_w_out[0])
    out = _nsa_layer(x2, mem2, batch, seq, nsa_norm[0], nsa_w_in[0], nsa_pe_k[0], nsa_pe_v[0],
                     nsa_phi_k_w1[0], nsa_phi_k_w2[0], nsa_phi_v_w1[0], nsa_phi_v_w2[0],
                     nsa_mem_norm[0], nsa_w_mem_kv[0], nsa_w_out[0], final_norm)
    return out.reshape(batch, seq, d)
```

```python
import functools

import numpy as np
import jax
import jax.numpy as jnp
from jax import lax
from jax.experimental import pallas as pl
from jax.experimental.pallas import tpu as pltpu

F32 = jnp.float32
BF16 = jnp.bfloat16

NORM_EPS = 1e-6
NEG_INF = -1e30
LOG2_E = 1.4426950408889634
LANES = 128
SUBLANES = 8
BF16_SUBLANES = 16
ATTN_BLOCK = 128
VMEM_LIMIT = 56 * 1024 * 1024

LRU_WIDTH = 1024
LRU_BLOCKS = 16
LRU_BLOCK_DIM = LRU_WIDTH // LRU_BLOCKS
LRU_PACK = 256
CONV_WIDTH = 4
LRU_C = 8.0

DIL_GROUPS = ((128, 1), (512, 4), (2048, 16))
DIL_HEADS = 4
DIL_HEAD_DIM = 128
DIL_WIDTH = DIL_HEADS * DIL_HEAD_DIM
DIL_QKV = len(DIL_GROUPS) * DIL_WIDTH
LSE_LANES = LANES // DIL_HEADS
MAX_ROW_PARTS = 8

MEM_HEADS = 4
MEM_HEAD_DIM = 64
MEM_WIDTH = MEM_HEADS * MEM_HEAD_DIM

NSA_HEADS = 16
NSA_KV_GROUPS = 2
NSA_R = NSA_HEADS // NSA_KV_GROUPS
NSA_HEAD_DIM = 64
NSA_WIDTH = NSA_HEADS * NSA_HEAD_DIM
NSA_KV = NSA_KV_GROUPS * NSA_HEAD_DIM
CMP_BLOCK = 32
CMP_STRIDE = 16
SLC_BLOCK = 64
SLC_TOP_N = 8
WIN_SIZE = 512
PHI_HIDDEN = 256
SEL_FORCE = 1e6
CMP_PER_SLC = SLC_BLOCK // CMP_STRIDE


def _alibi_slopes(n):
    return [float(v) for v in np.exp2(-8.0 * np.arange(1, n + 1) / n).astype(np.float32)]


def _params(*semantics):
    return pltpu.CompilerParams(dimension_semantics=semantics, vmem_limit_bytes=VMEM_LIMIT)


def _silu(z):
    return z * jax.nn.sigmoid(z)


def _dot_t(a, b):
    return lax.dot_general(a, b, (((1,), (1,)), ((), ())), preferred_element_type=F32)


def _rms_norm_rows(x, g):
    ms = jnp.mean(x * x, axis=-1, keepdims=True)
    return (x * lax.rsqrt(ms + NORM_EPS) * g).astype(BF16)


def _norm_matmul_kernel(*refs, dil, pieces, row_parts):
    x_ref, g_ref = refs[:2]
    w_refs, o_refs = refs[2:2 + sum(pieces)], refs[2 + sum(pieces):]
    tm, k = x_ref.shape
    rows = tm // row_parts
    per = rows // dil
    for part in range(row_parts):
        x = x_ref[part * rows:(part + 1) * rows, :]
        if dil == 1:
            xn = _rms_norm_rows(x, g_ref[...])
        else:
            xn = x * lax.rsqrt(jnp.mean(x * x, axis=-1, keepdims=True) + NORM_EPS) * g_ref[...]
            xn = jnp.swapaxes(xn.reshape(per, dil, k), 0, 1).reshape(rows, k).astype(BF16)
        first = 0
        for o_ref, n_pieces in zip(o_refs, pieces):
            col = 0
            for w_ref in w_refs[first:first + n_pieces]:
                width = w_ref.shape[1]
                res = jnp.dot(xn, w_ref[...], preferred_element_type=F32).astype(o_ref.dtype)
                if dil == 1:
                    o_ref[part * rows:(part + 1) * rows, col:col + width] = res
                else:
                    for c in range(dil):
                        o_ref[c, part * per:(part + 1) * per, col:col + width] = res[c * per:(c + 1) * per]
                col += width
            first += n_pieces


def _norm_matmul(x, g, w, outs, dil=1, tm=1024):
    m, k = x.shape
    tm = min(tm, m)
    assert m % tm == 0 and k % LANES == 0
    pieces = [len(cols) for _, cols in outs]
    widths = [sum(width for _, width in cols) for _, cols in outs]
    w_specs = []
    for _, cols in outs:
        for col, width in cols:
            assert col % width == 0 and width % LANES == 0
            w_specs.append(pl.BlockSpec((k, width), functools.partial(lambda i, j: (0, j), j=col // width)))
    resident = [pl.BlockSpec((1, k), lambda i: (0, 0))] + w_specs
    operands = (g.reshape(1, k),) + (w,) * len(w_specs)
    row_parts = min(dil, MAX_ROW_PARTS)
    assert tm % (dil * row_parts * SUBLANES) == 0
    kernel = functools.partial(_norm_matmul_kernel, dil=dil, pieces=pieces, row_parts=row_parts)
    if dil == 1:
        return pl.pallas_call(
            kernel,
            grid=(m // tm,),
            in_specs=[pl.BlockSpec((tm, k), lambda i: (i, 0))] + resident,
            out_specs=[pl.BlockSpec((tm, width), lambda i: (i, 0)) for width in widths],
            out_shape=[jax.ShapeDtypeStruct((m, width), dtype) for width, (dtype, _) in zip(widths, outs)],
            compiler_params=_params("arbitrary"),
            name="norm_matmul",
        )(x, *operands)
    per = tm // dil
    (out_dtype, _), = outs
    n, = widths
    assert tm % dil == 0 and per % BF16_SUBLANES == 0
    return pl.pallas_call(
        kernel,
        grid=(m // tm,),
        in_specs=[pl.BlockSpec((tm, k), lambda i: (i, 0))] + resident,
        out_specs=pl.BlockSpec((dil, per, n), lambda i: (0, i, 0)),
        out_shape=jax.ShapeDtypeStruct((dil, m // dil, n), out_dtype),
        compiler_params=_params("arbitrary"),
        name="norm_matmul_dil",
    )(x, *operands)


def _rglru_kernel(xa_ref, za_ref, cw_ref, cb_ref, wa_ref, ba_ref, wx_ref, bx_ref, lam_ref,
                  o_ref, xpad_ref, h_ref):
    t = pl.program_id(1)
    tt, width = xa_ref.shape
    halo = SUBLANES

    @pl.when(t == 0)
    def _():
        xpad_ref[0:halo, :] = jnp.zeros((halo, width), F32)
        h_ref[...] = jnp.zeros_like(h_ref)

    x = xa_ref[...]
    xpad_ref[halo:halo + tt, :] = x
    cw = cw_ref[...]
    y = cw[CONV_WIDTH - 1:CONV_WIDTH] * x
    for k in range(1, CONV_WIDTH):
        y = y + cw[CONV_WIDTH - 1 - k:CONV_WIDTH - k] * xpad_ref[halo - k:halo - k + tt, :]
    y = y + cb_ref[...]
    xpad_ref[0:halo, :] = x[tt - halo:tt, :]

    yb = y.astype(BF16)
    r_parts, i_parts = [], []
    for p in range(width // LRU_PACK):
        ys = yb[:, p * LRU_PACK:(p + 1) * LRU_PACK]
        r_parts.append(jnp.dot(ys, wa_ref[p], preferred_element_type=F32))
        i_parts.append(jnp.dot(ys, wx_ref[p], preferred_element_type=F32))
    r = jax.nn.sigmoid(jnp.concatenate(r_parts, axis=1) + ba_ref[...])
    gi = jax.nn.sigmoid(jnp.concatenate(i_parts, axis=1) + bx_ref[...])

    nl = -lam_ref[...]
    softplus = jnp.maximum(nl, 0.0) + jnp.log1p(jnp.exp(-jnp.abs(nl)))
    log_a = (-LRU_C) * r * softplus
    a = jnp.exp(log_a)
    w = -jnp.tanh(log_a) * (a * a + 1.0)
    mult = jnp.where(w > 0.0, w * lax.rsqrt(w), 0.0)
    b = y * gi * mult
    first = (lax.broadcasted_iota(jnp.int32, (SUBLANES, width), 0) == 0) & (t == 0)
    b = jnp.concatenate([jnp.where(first, (y * gi)[0:SUBLANES], b[0:SUBLANES]), b[SUBLANES:]], axis=0)

    blocks = width // LANES

    def time_major(v):
        cols = jnp.stack([v[:, j * LANES:(j + 1) * LANES] for j in range(blocks)], axis=0)
        return jnp.swapaxes(cols, 0, 1)

    a_t, b_t = time_major(a), time_major(b)
    state = h_ref[...].reshape(blocks, LANES)
    steps = []
    for step in range(tt):
        state = a_t[step] * state + b_t[step]
        steps.append(state)
    h_ref[...] = state.reshape(1, width)
    h_cols = jnp.swapaxes(jnp.stack(steps, axis=0), 0, 1)
    h = jnp.concatenate([h_cols[j] for j in range(blocks)], axis=1)
    o_ref[...] = (h * _silu(za_ref[...].astype(F32))).astype(o_ref.dtype)


def _rglru(xa, za_src, za_col, conv_w, conv_b, wa, ba, wx, bx, lam, batch, seq, tt=1024):
    width = LRU_WIDTH
    nt = seq // tt
    packs = width // LRU_PACK
    vec = pl.BlockSpec((1, width), lambda b, t: (0, 0))
    gate_w = pl.BlockSpec((packs, LRU_PACK, LRU_PACK), lambda b, t: (0, 0, 0))
    return pl.pallas_call(
        _rglru_kernel,
        grid=(batch, nt),
        in_specs=[pl.BlockSpec((tt, width), lambda b, t: (b * nt + t, 0)),
                  pl.BlockSpec((tt, width), lambda b, t: (b * nt + t, za_col)),
                  pl.BlockSpec((CONV_WIDTH, width), lambda b, t: (0, 0)),
                  vec, gate_w, vec, gate_w, vec, vec],
        out_specs=pl.BlockSpec((tt, width), lambda b, t: (b * nt + t, 0)),
        out_shape=jax.ShapeDtypeStruct((batch * seq, width), BF16),
        scratch_shapes=[pltpu.VMEM((tt + SUBLANES, width), F32), pltpu.VMEM((1, width), F32)],
        compiler_params=_params("arbitrary", "arbitrary"),
        name="rglru",
    )(xa, za_src, conv_w, conv_b.reshape(1, width), wa, ba.reshape(1, width), wx, bx.reshape(1, width),
      lam.reshape(1, width))


def _pack_block_diag(w):
    per = LRU_PACK // LRU_BLOCK_DIM
    w = w.reshape(LRU_BLOCKS // per, per, LRU_BLOCK_DIM, LRU_BLOCK_DIM)
    eye = jnp.eye(per, dtype=w.dtype)
    packed = w[:, :, :, None, :] * eye[None, :, None, :, None]
    return packed.reshape(LRU_BLOCKS // per, LRU_PACK, LRU_PACK).astype(BF16)


def _dil_attn_kernel(*refs, slopes, pos_scale, max_dist, has_halo, dil, n_cls, n_blk):
    if has_halo:
        q_ref, kh_ref, k_ref, vh_ref, v_ref = refs[:5]
        out_refs = refs[5:]
    else:
        q_ref, k_ref, v_ref = refs[:3]
        out_refs = refs[3:]
    n_out = DIL_HEADS + 1
    dst_refs = out_refs[:n_out]
    stage_refs = out_refs[n_out:] if dil > 1 else dst_refs
    first_super = pl.program_id(1) == 0
    cls0 = pl.program_id(2) * n_cls
    blk = ATTN_BLOCK
    scale = DIL_HEAD_DIM ** -0.5

    def band(width, halo_live):
        row = lax.broadcasted_iota(jnp.int32, (blk, width), 0)
        col = lax.broadcasted_iota(jnp.int32, (blk, width), 1)
        dist = (width - blk) + row - col
        valid = (dist >= 0) & (dist <= max_dist)
        if halo_live is not None:
            valid = valid & ((col >= blk) | halo_live)
        distf = (dist * pos_scale).astype(F32)
        return [jnp.where(valid, (-slope / scale) * distf, NEG_INF) for slope in slopes]

    bias_inner = band(2 * blk, None) if n_blk > 1 else None
    bias_first = band(2 * blk, jnp.logical_not(first_super)) if has_halo else band(blk, None)

    def scores(cc, jb):
        cur = slice(jb * blk, (jb + 1) * blk)
        bias = bias_inner if jb > 0 else bias_first
        out = []
        for h in range(DIL_HEADS):
            hs = slice(h * DIL_HEAD_DIM, (h + 1) * DIL_HEAD_DIM)
            q = q_ref[cc, cur, hs]
            if jb > 0:
                k = k_ref[cc, (jb - 1) * blk:(jb + 1) * blk, hs]
                v = v_ref[cc, (jb - 1) * blk:(jb + 1) * blk, hs]
            elif has_halo:
                k = jnp.concatenate([kh_ref[cc, :, hs], k_ref[cc, cur, hs]], axis=0)
                v = jnp.concatenate([vh_ref[cc, :, hs], v_ref[cc, cur, hs]], axis=0)
            else:
                k, v = k_ref[cc, cur, hs], v_ref[cc, cur, hs]
            out.append((_dot_t(q, k) + bias[h], v))
        return out

    def finish(cc, jb, pairs):
        where = (cls0 + cc, slice(jb * blk, (jb + 1) * blk)) if dil > 1 else (slice(jb * blk, (jb + 1) * blk),)
        lses = []
        for h, (s, v) in enumerate(pairs):
            m = jnp.max(s, axis=-1, keepdims=True)
            e = jnp.exp2((s - m) * (scale * LOG2_E))
            den = jnp.sum(e, axis=-1, keepdims=True)
            o = jnp.dot(e.astype(BF16), v, preferred_element_type=F32) / den
            stage_refs[h][where] = o.astype(stage_refs[h].dtype)
            lses.append(jnp.broadcast_to(m * scale + jnp.log(den), (blk, LSE_LANES)))
        stage_refs[DIL_HEADS][where] = jnp.concatenate(lses, axis=1)

    pending = None
    for cc in range(n_cls):
        for jb in range(n_blk):
            pairs = scores(cc, jb)
            if pending is not None:
                finish(*pending)
            pending = (cc, jb, pairs)
    finish(*pending)

    if dil > 1:
        @pl.when(pl.program_id(2) == pl.num_programs(2) - 1)
        def _():
            for stage, dst in zip(stage_refs, dst_refs):
                dst[...] = jnp.swapaxes(stage[...], 0, 1).reshape(dst.shape).astype(dst.dtype)


def _dil_attn(qkv, col0, gi, batch, seq, work=8):
    window, dil = DIL_GROUPS[gi]
    sub = seq // dil
    nb = sub // ATTN_BLOCK
    n_blk = min(work, nb)
    n_cls = min(work // n_blk, dil)
    n_super = nb // n_blk
    has_halo = n_super > 1
    span = n_blk * ATTN_BLOCK
    slopes = _alibi_slopes(len(DIL_GROUPS) * DIL_HEADS)[gi * DIL_HEADS:(gi + 1) * DIL_HEADS]
    cur = lambda col: pl.BlockSpec((n_cls, span, DIL_WIDTH), lambda b, i, c: (c, b * n_super + i, col0 + col))
    halo = lambda col: pl.BlockSpec(
        (n_cls, ATTN_BLOCK, DIL_WIDTH),
        lambda b, i, c: (c, jnp.maximum((b * n_super + i) * n_blk - 1, 0), col0 + col))
    if has_halo:
        in_specs = [cur(0), halo(1), cur(1), halo(2), cur(2)]
    else:
        in_specs = [cur(0), cur(1), cur(2)]
    n_out = DIL_HEADS + 1
    *o, lse = pl.pallas_call(
        functools.partial(_dil_attn_kernel, slopes=slopes, pos_scale=dil, max_dist=window // dil,
                          has_halo=has_halo, dil=dil, n_cls=n_cls, n_blk=n_blk),
        grid=(batch, n_super, dil // n_cls),
        in_specs=in_specs,
        out_specs=[pl.BlockSpec((span * dil, LANES), lambda b, i, c: (b * n_super + i, 0))] * n_out,
        out_shape=[jax.ShapeDtypeStruct((batch * seq, LANES), BF16)] * DIL_HEADS
                  + [jax.ShapeDtypeStruct((batch * seq, LANES), F32)],
        scratch_shapes=[pltpu.VMEM((dil, span, LANES), F32)] * (n_out if dil > 1 else 0),
        compiler_params=_params("arbitrary", "arbitrary", "arbitrary"),
        name=f"dil_attn_d{dil}",
    )(*([qkv] * len(in_specs)))
    return o, lse


def _mem_attn_kernel(q_ref, k_ref, v_ref, z_ref, o_ref, ks_ref, vt_ref):
    hd = MEM_HEAD_DIM
    n_mem = k_ref.shape[0]

    @pl.when(pl.program_id(1) == 0)
    def _():
        ks_ref[...] = (k_ref[...].astype(F32) * (hd ** -0.5)).astype(BF16)
        vt = v_ref[...].astype(F32).T.astype(BF16)
        for h in range(MEM_HEADS):
            vt_ref[h, 0:hd, :] = vt[h * hd:(h + 1) * hd, :]
            vt_ref[h, hd:, :] = jnp.ones((vt_ref.shape[1] - hd, n_mem), BF16)

    heads = [slice(h * hd, (h + 1) * hd) for h in range(MEM_HEADS)]
    scores = [_dot_t(ks_ref[:, hs], q_ref[:, hs]) for hs in heads]
    outs = []
    for h, s in enumerate(scores):
        e = jnp.exp(s - jnp.max(s, axis=0, keepdims=True)).astype(BF16)
        acc = jnp.dot(vt_ref[h], e, preferred_element_type=F32)
        outs.append(acc[0:hd, :] / acc[hd:hd + 1, :])
    o = jnp.concatenate(outs, axis=0).T
    o_ref[...] = (o * _silu(z_ref[...].astype(F32))).astype(o_ref.dtype)


def _mem_attn(qsrc, q_col, kv, zsrc, z_col, batch, seq, n_mem, tq=1024):
    nq = seq // tq
    return pl.pallas_call(
        _mem_attn_kernel,
        grid=(batch, nq),
        in_specs=[pl.BlockSpec((tq, MEM_WIDTH), lambda b, i: (b * nq + i, q_col)),
                  pl.BlockSpec((n_mem, MEM_WIDTH), lambda b, i: (b, 0)),
                  pl.BlockSpec((n_mem, MEM_WIDTH), lambda b, i: (b, 1)),
                  pl.BlockSpec((tq, MEM_WIDTH), lambda b, i: (b * nq + i, z_col))],
        out_specs=pl.BlockSpec((tq, MEM_WIDTH), lambda b, i: (b * nq + i, 0)),
        out_shape=jax.ShapeDtypeStruct((batch * seq, MEM_WIDTH), BF16),
        scratch_shapes=[pltpu.VMEM((n_mem, MEM_WIDTH), BF16),
                        pltpu.VMEM((MEM_HEADS, MEM_HEAD_DIM + BF16_SUBLANES, n_mem), BF16)],
        compiler_params=_params("arbitrary", "arbitrary"),
        name="mem_attn",
    )(qsrc, kv, kv, zsrc)


def _hawk_out_kernel(*refs):
    n_groups = len(DIL_GROUPS)
    ya_ref = refs[0]
    o_refs = refs[1:1 + n_groups * DIL_HEADS]
    l_refs = refs[1 + n_groups * DIL_HEADS:1 + n_groups * (DIL_HEADS + 1)]
    zb_ref, ym_ref, w_ref, x_ref, out_ref = refs[1 + n_groups * (DIL_HEADS + 1):]
    a_end = LRU_WIDTH
    b_end = a_end + DIL_WIDTH
    y = jnp.dot(ya_ref[...], w_ref[0:a_end, :], preferred_element_type=F32)
    y = y + jnp.dot(ym_ref[...], w_ref[b_end:b_end + MEM_WIDTH, :], preferred_element_type=F32)
    parts = []
    for h in range(DIL_HEADS):
        ls = [l[:, h * LSE_LANES:h * LSE_LANES + 1] for l in l_refs]
        m = functools.reduce(jnp.maximum, ls)
        ws = [jnp.exp(l - m) for l in ls]
        num = sum(w * o_refs[gi * DIL_HEADS + h][...].astype(F32) for gi, w in enumerate(ws))
        parts.append(num / sum(ws))
    yb = (jnp.concatenate(parts, axis=1) * _silu(zb_ref[...].astype(F32))).astype(BF16)
    y = y + jnp.dot(yb, w_ref[a_end:b_end, :], preferred_element_type=F32)
    out_ref[...] = x_ref[...] + y


def _hawk_out(ya, os_, ls_, zb_src, zb_col, ym, w, x, tm=1024):
    m, d = x.shape
    row = lambda width, col=0: pl.BlockSpec((tm, width), lambda i: (i, col))
    heads = [o for group in os_ for o in group]
    return pl.pallas_call(
        _hawk_out_kernel,
        grid=(m // tm,),
        in_specs=[row(LRU_WIDTH)] + [row(DIL_HEAD_DIM)] * len(heads) + [row(LANES)] * len(ls_)
                 + [row(DIL_WIDTH, zb_col), row(MEM_WIDTH),
                    pl.BlockSpec(w.shape, lambda i: (0, 0)), row(d)],
        out_specs=row(d),
        out_shape=jax.ShapeDtypeStruct((m, d), F32),
        compiler_params=_params("arbitrary"),
        name="hawk_out",
    )(ya, *heads, *ls_, zb_src, ym, w, x)


def _compress_kernel(k_ref, v_ref, pe_ref, w1_ref, w2k_ref, w2vt_ref, ko_ref, vto_ref):
    n_bat, n_blk = ko_ref.shape[0], ko_ref.shape[1]
    seq = n_blk * CMP_STRIDE

    def hidden(which, src_ref):
        x = jnp.concatenate(
            [jnp.concatenate([src_ref[pl.ds(bi * seq + p, n_blk, stride=CMP_STRIDE), :]
                              for p in range(CMP_STRIDE)], axis=1) for bi in range(n_bat)], axis=0).astype(BF16)
        first = jnp.dot(x, w1_ref[which, 0], preferred_element_type=F32)
        second = jnp.dot(x, w1_ref[which, 1], preferred_element_type=F32)
        pe = (jnp.dot(pe_ref[which, 0], w1_ref[which, 0], preferred_element_type=F32)
              + jnp.dot(pe_ref[which, 1], w1_ref[which, 1], preferred_element_type=F32))
        nxt = pltpu.roll(second.reshape(n_bat, n_blk, second.shape[1]), n_blk - 1, axis=1)
        return _silu(first + nxt.reshape(second.shape) + pe[0:1, :]).astype(BF16)

    act_k, act_v = hidden(0, k_ref), hidden(1, v_ref)
    part = lambda a, g: a[:, g * PHI_HIDDEN:(g + 1) * PHI_HIDDEN]
    ks = jnp.concatenate([jnp.dot(part(act_k, g), w2k_ref[...], preferred_element_type=F32)
                          for g in range(NSA_KV_GROUPS)], axis=1)
    vts = jnp.concatenate([_dot_t(w2vt_ref[...], part(act_v, g)) for g in range(NSA_KV_GROUPS)],
                          axis=0)
    for bi in range(n_bat):
        ko_ref[bi] = ks[bi * n_blk:(bi + 1) * n_blk].astype(ko_ref.dtype)
        vto_ref[bi] = vts[:, bi * n_blk:(bi + 1) * n_blk].astype(vto_ref.dtype)


def _compress(src, k_col, v_col, pe_k, pe_v, k_w1, k_w2, v_w1, v_w2, batch, seq, n_bat=4):
    half = CMP_BLOCK // 2
    assert half == CMP_STRIDE and NSA_KV == LANES
    n_blk = seq // CMP_STRIDE
    hd = NSA_HEAD_DIM
    n_bat = min(n_bat, batch)
    assert batch % n_bat == 0
    w1 = jnp.stack([k_w1, v_w1]).reshape(2, 2, half, hd, PHI_HIDDEN).astype(BF16)
    zero = jnp.zeros_like(w1)
    per_group = [jnp.concatenate([w1 if g == col else zero for col in range(NSA_KV_GROUPS)], axis=-1)
                 for g in range(NSA_KV_GROUPS)]
    w1e = jnp.stack(per_group, axis=3).reshape(2, 2, half * NSA_KV, NSA_KV_GROUPS * PHI_HIDDEN)
    pe = jnp.stack([pe_k, pe_v]).reshape(2, 2, half, 1, hd)
    pe = jnp.broadcast_to(pe, (2, 2, half, NSA_KV_GROUPS, hd)).reshape(2, 2, 1, half * NSA_KV)
    pe = jnp.broadcast_to(pe, (2, 2, SUBLANES, half * NSA_KV)).astype(BF16)
    w2k = k_w2.astype(BF16)
    w2vt = v_w2.T.astype(BF16)
    whole = lambda a: pl.BlockSpec(a.shape, lambda b: (0,) * a.ndim)
    return pl.pallas_call(
        _compress_kernel,
        grid=(batch // n_bat,),
        in_specs=[pl.BlockSpec((n_bat * seq, LANES), lambda b: (b, k_col)),
                  pl.BlockSpec((n_bat * seq, LANES), lambda b: (b, v_col)),
                  whole(pe), whole(w1e), whole(w2k), whole(w2vt)],
        out_specs=[pl.BlockSpec((n_bat, n_blk, NSA_KV), lambda b: (b, 0, 0)),
                   pl.BlockSpec((n_bat, NSA_KV, n_blk), lambda b: (b, 0, 0))],
        out_shape=[jax.ShapeDtypeStruct((batch, n_blk, NSA_KV), BF16),
                   jax.ShapeDtypeStruct((batch, NSA_KV, n_blk), BF16)],
        compiler_params=_params("arbitrary"),
        name="compress",
    )(src, src, pe, w1e, w2k, w2vt)


KEY_CHUNK = 256
NSA_TQ = 256
SLOPE_PIECES = 3
N_FEATS = 2 * SLOPE_PIECES
FEAT_LANES = 32
KEY_COLS = NSA_HEAD_DIM + 2 * FEAT_LANES
N_SLC = 32


def _slope_pieces(slope):
    rest = np.float32(slope)
    pieces = []
    for _ in range(SLOPE_PIECES):
        p = np.float32(np.asarray(rest).astype(BF16))
        pieces.append(float(p))
        rest = np.float32(rest - p)
    return pieces


def _lane_table(lane, values):
    out = jnp.zeros(lane.shape, F32)
    for idx, v in enumerate(values):
        out = jnp.where(lane == idx, v, out)
    return out


def _key_feats(pos_hi, pos_lo, lane):
    return jnp.where(lane < SLOPE_PIECES, pos_hi, jnp.where(lane < N_FEATS, pos_lo, 0)).astype(F32)


def _tile_heads(x):
    return jnp.concatenate([x] * NSA_R, axis=1)


def _chunk_loop(lo, hi, body, init, widths=(4, 2, 1)):
    carry, start = init, lo
    for idx, w in enumerate(widths):
        count = (hi - start) // w

        def step(p, cr, start=start, w=w):
            first = start + p * w
            return body([first + j for j in range(w)], cr)

        if idx == 0:
            carry = lax.fori_loop(0, count, step, carry)
        else:
            carry = lax.cond(count > 0, functools.partial(step, 0), lambda cr: cr, carry)
        start = start + count * w
    return carry


def _nsa_kernel(q_ref, kc_ref, vct_ref, ksrc_ref, vsrc_ref, kwsrc_ref, vwsrc_ref, feat_ref, hot_ref,
                gl_ref, z_ref, ym_ref, w_ref, x_ref, fin_ref, o_ref,
                ks_ref, vst_ref, kw_ref, vwt_ref, s_ref, acc_ref, imp_ref):
    i = pl.program_id(1)
    tq = q_ref.shape[0]
    hd = NSA_HEAD_DIM
    n_cmp = kc_ref.shape[0]

    @pl.when(i == 0)
    def _():
        for g in range(NSA_KV_GROUPS):
            gs = slice(g * hd, (g + 1) * hd)
            for dst, src, tail in ((ks_ref, ksrc_ref, hot_ref[...]),
                                   (kw_ref, kwsrc_ref, jnp.zeros(hot_ref.shape, BF16))):
                dst[:, g * KEY_COLS:g * KEY_COLS + hd] = src[:, gs]
                dst[:, g * KEY_COLS + hd:g * KEY_COLS + hd + FEAT_LANES] = feat_ref[...]
                dst[:, g * KEY_COLS + hd + FEAT_LANES:(g + 1) * KEY_COLS] = tail
        for c in range(vst_ref.shape[0]):
            rows = slice(c * KEY_CHUNK, (c + 1) * KEY_CHUNK)
            vst_ref[c] = vsrc_ref[rows, :].astype(F32).T.astype(BF16)
            vwt_ref[c] = vwsrc_ref[rows, :].astype(F32).T.astype(BF16)

    slopes_all = _alibi_slopes(NSA_HEADS)
    gates_t = jax.nn.sigmoid(gl_ref[...]).T
    feat_lane = lax.broadcasted_iota(jnp.int32, (tq, FEAT_LANES), 1)
    no_sel = jnp.zeros((NSA_R * tq, FEAT_LANES), BF16)
    key_row = lax.broadcasted_iota(jnp.int32, (KEY_CHUNK, tq), 0)
    t_pos = i * tq + lax.broadcasted_iota(jnp.int32, (KEY_CHUNK, tq), 1)
    ones_rows = jnp.ones((BF16_SUBLANES, KEY_CHUNK), BF16)
    win_lo = jnp.maximum(i * tq - (WIN_SIZE - 1), 0) // KEY_CHUNK
    chunks_hi = (i * tq + tq - 1) // KEY_CHUNK + 1

    groups = range(NSA_KV_GROUPS)
    q_win, q_slc, o_cmp = [], [], []
    for g in groups:
        slopes = slopes_all[g * NSA_R:(g + 1) * NSA_R]
        gs = slice(g * hd, (g + 1) * hd)
        q_parts = []
        for r in range(NSA_R):
            qr = q_ref[:, (g * NSA_R + r) * hd:(g * NSA_R + r + 1) * hd]
            feats = _lane_table(feat_lane, _slope_pieces(slopes[r]) * 2).astype(BF16)
            q_parts.append(jnp.concatenate([qr, feats], axis=1))
        q_feat = jnp.concatenate(q_parts, axis=0)
        q_aug = jnp.concatenate([q_feat, no_sel], axis=1)

        n_row = lax.broadcasted_iota(jnp.int32, (n_cmp, tq), 0)
        t_cmp = i * tq + lax.broadcasted_iota(jnp.int32, (n_cmp, tq), 1)
        visible = t_cmp >= n_row * CMP_STRIDE + (CMP_BLOCK - 1)
        cfeat_row = lax.broadcasted_iota(jnp.int32, (n_cmp, 2 * FEAT_LANES), 0)
        cfeat_lane = lax.broadcasted_iota(jnp.int32, (n_cmp, 2 * FEAT_LANES), 1)
        kc_feats = _key_feats(cfeat_row * CMP_STRIDE, 0, cfeat_lane)
        kc_aug = jnp.concatenate([kc_ref[:, gs], kc_feats.astype(BF16)], axis=1)
        s = _dot_t(kc_aug, q_aug) + _tile_heads(jnp.where(visible, 0.0, NEG_INF))
        m = jnp.max(s, axis=0, keepdims=True)
        e = jnp.exp(s - m)
        t_one = i * tq + lax.broadcasted_iota(jnp.int32, (1, NSA_R * tq), 1) % tq
        any_visible = t_one >= (CMP_BLOCK - 1)
        p = e * jnp.where(any_visible, 1.0 / jnp.sum(e, axis=0, keepdims=True), 0.0)
        o_cmp.append(jnp.dot(vct_ref[gs, :], p.astype(BF16), preferred_element_type=F32))
        p_sum = p[:, 0:tq]
        for r in range(1, NSA_R):
            p_sum = p_sum + p[:, r * tq:(r + 1) * tq]

        band = p_sum + pltpu.roll(p_sum, 1, axis=0)
        for k in range(1, CMP_PER_SLC):
            band = band + pltpu.roll(p_sum, n_cmp - k, axis=0)
        halves = []
        for half in range(tq // LANES):
            imp_ref[...] = band[:, half * LANES:(half + 1) * LANES]
            halves.append(imp_ref[pl.ds(0, N_SLC, stride=CMP_PER_SLC), :])
        imp = jnp.concatenate(halves, axis=1)
        blk_j = lax.broadcasted_iota(jnp.int32, (N_SLC, tq), 0)
        cur = (i * tq + lax.broadcasted_iota(jnp.int32, (N_SLC, tq), 1)) // SLC_BLOCK
        forced = (blk_j == 0) | (blk_j == cur) | (blk_j == cur - 1)
        v_imp = jnp.where(forced, SEL_FORCE, jnp.where(blk_j > cur, -SEL_FORCE, imp))
        rank = jnp.zeros((N_SLC, tq), F32)
        for other in range(N_SLC):
            row = v_imp[other:other + 1, :]
            ahead = (row > v_imp) | ((row == v_imp) & (blk_j > other))
            rank = rank + jnp.where(ahead, 1.0, 0.0)
        sel_bias = jnp.where(rank < SLC_TOP_N, 0.0, NEG_INF)

        padded = jnp.concatenate([sel_bias, jnp.zeros((LANES - N_SLC, tq), F32)], axis=0)
        sel_t = padded.T[:, 0:FEAT_LANES].astype(BF16)
        q_win.append(q_aug)
        q_slc.append(jnp.concatenate([q_feat, jnp.concatenate([sel_t] * NSA_R, axis=0)], axis=1))

    def attend(q_brs, k_ref, key_cols, vt_ref, lo, hi, masked_from, mask_fn):
        def scores(cs, m_run, masked):
            starts = [pl.multiple_of(c * KEY_CHUNK, KEY_CHUNK) for c in cs]
            scs = [[_dot_t(k_ref[pl.ds(start, KEY_CHUNK), g * key_cols:(g + 1) * key_cols], q_brs[g])
                    for g in groups] for start in starts]
            for c, start, sc in zip(cs, starts, scs):
                if masked:
                    bias = _tile_heads(jnp.where(mask_fn(t_pos - (start + key_row)), 0.0, NEG_INF))
                    sc = [x + bias for x in sc]
                for g in groups:
                    s_ref[g, c] = sc[g]
                m_run = tuple(jnp.maximum(m_run[g], jnp.max(sc[g], axis=0, keepdims=True)) for g in groups)
            return m_run

        def weighted(cs, carry):
            for c in cs:
                for g in groups:
                    e = jnp.exp(s_ref[g, c] - m_rows[g]).astype(BF16)
                    v_ext = jnp.concatenate([vt_ref[c, g * hd:(g + 1) * hd, :], ones_rows], axis=0)
                    acc_ref[g] += jnp.dot(v_ext, e, preferred_element_type=F32)
            return carry

        m_rows = tuple(jnp.full((1, NSA_R * tq), NEG_INF, F32) for _ in groups)
        m_rows = _chunk_loop(lo, masked_from, functools.partial(scores, masked=False), m_rows)
        m_rows = _chunk_loop(masked_from, hi, functools.partial(scores, masked=True), m_rows)
        acc_ref[...] = jnp.zeros(acc_ref.shape, F32)
        _chunk_loop(lo, hi, weighted, 0)
        return [acc_ref[g, 0:hd, :] / acc_ref[g, hd:hd + 1, :] for g in groups]

    o_slc = attend(q_slc, ks_ref, KEY_COLS, vst_ref, 0, chunks_hi, (i * tq) // KEY_CHUNK,
                   lambda dist: dist >= 0)
    o_win = attend(q_win, kw_ref, KEY_COLS, vwt_ref, win_lo, chunks_hi, win_lo,
                   lambda dist: (dist >= 0) & (dist <= WIN_SIZE - 1))

    y = jnp.dot(ym_ref[...], w_ref[NSA_WIDTH:NSA_WIDTH + MEM_WIDTH, :], preferred_element_type=F32)
    for g in groups:
        def gate(kind):
            base = g * NSA_R * 3 + kind
            return jnp.concatenate([gates_t[base + 3 * r:base + 3 * r + 1, :] for r in range(NSA_R)], axis=1)

        o = gate(0) * o_cmp[g] + gate(1) * o_slc[g] + gate(2) * o_win[g]
        pairs = []
        for r in range(0, NSA_R, 2):
            two = jnp.concatenate([o[:, r * tq:(r + 1) * tq], o[:, (r + 1) * tq:(r + 2) * tq]], axis=0)
            pairs.append(two.T)
        cs = slice(g * NSA_R * hd, (g + 1) * NSA_R * hd)
        yo = (jnp.concatenate(pairs, axis=1) * _silu(z_ref[:, cs].astype(F32))).astype(BF16)
        y = y + jnp.dot(yo, w_ref[cs, :], preferred_element_type=F32)

    x = x_ref[...] + y
    ms = jnp.mean(x * x, axis=-1, keepdims=True)
    o_ref[...] = x * lax.rsqrt(ms + NORM_EPS) * fin_ref[...]


def _key_pos_feats(seq):
    assert seq // SLC_BLOCK == N_SLC <= FEAT_LANES
    pos = np.arange(seq)
    feats = np.zeros((seq, FEAT_LANES), np.float32)
    feats[:, 0:SLOPE_PIECES] = ((pos // SLC_BLOCK) * SLC_BLOCK)[:, None]
    feats[:, SLOPE_PIECES:N_FEATS] = (pos % SLC_BLOCK)[:, None]
    onehot = (np.arange(FEAT_LANES)[None, :] == (pos // SLC_BLOCK)[:, None]).astype(np.float32)
    return jnp.asarray(feats, BF16), jnp.asarray(onehot, BF16)


def _nsa_attn(nb, kv_col0, z_col, k_cmp, v_cmp_t, nf, gl_col, ym, w_out, x, final_g, batch, seq):
    tq = NSA_TQ
    nq = seq // tq
    d = x.shape[1]
    feats, onehot = _key_pos_feats(seq)
    kv_blk = kv_col0 // NSA_KV
    seq_cols = lambda col: pl.BlockSpec((seq, NSA_KV), lambda b, i: (b, kv_blk + col))
    const = lambda a: pl.BlockSpec(a.shape, lambda b, i: (0,) * a.ndim)
    per_batch = lambda a: pl.BlockSpec((None,) + a.shape[1:], lambda b, i: (b,) + (0,) * (a.ndim - 1))
    rows = lambda width, col=0: pl.BlockSpec((tq, width), lambda b, i: (b * nq + i, col))
    fin = final_g.reshape(1, d)
    return pl.pallas_call(
        _nsa_kernel,
        grid=(batch, nq),
        in_specs=[rows(NSA_WIDTH), per_batch(k_cmp), per_batch(v_cmp_t),
                  seq_cols(0), seq_cols(1), seq_cols(2), seq_cols(3), const(feats), const(onehot),
                  rows(LANES, gl_col), rows(NSA_WIDTH, z_col), rows(MEM_WIDTH), const(w_out), rows(d),
                  const(fin)],
        out_specs=rows(d),
        out_shape=jax.ShapeDtypeStruct((batch * seq, d), F32),
        scratch_shapes=[pltpu.VMEM((seq, NSA_KV_GROUPS * KEY_COLS), BF16),
                        pltpu.VMEM((seq // KEY_CHUNK, NSA_KV, KEY_CHUNK), BF16),
                        pltpu.VMEM((seq, NSA_KV_GROUPS * KEY_COLS), BF16),
                        pltpu.VMEM((seq // KEY_CHUNK, NSA_KV, KEY_CHUNK), BF16),
                        pltpu.VMEM((NSA_KV_GROUPS, seq // KEY_CHUNK, KEY_CHUNK, NSA_R * tq), F32),
                        pltpu.VMEM((NSA_KV_GROUPS, NSA_HEAD_DIM + BF16_SUBLANES, NSA_R * tq), F32),
                        pltpu.VMEM((seq // CMP_STRIDE, LANES), F32)],
        compiler_params=_params("arbitrary", "arbitrary"),
        name="nsa_attn",
    )(nb, k_cmp, v_cmp_t, nb, nb, nb, nb, feats, onehot, nf, nb, ym, w_out, x, fin)


def _hawk_layer(x, mem, batch, seq, norm_g, w_in, conv_w, conv_b, ga_w, ga_b, gx_w, gx_b, lam,
                mem_norm_g, w_mem_kv, w_out):
    xa0, za0 = 0, LRU_WIDTH
    q0 = 2 * LRU_WIDTH
    k0, v0 = q0 + DIL_QKV, q0 + 2 * DIL_QKV
    zb0 = q0 + 3 * DIL_QKV
    qm0 = zb0 + DIL_WIDTH
    zm0 = qm0 + MEM_WIDTH

    def qkv_cols(gi):
        return [(base + gi * DIL_WIDTH, DIL_WIDTH) for base in (q0, k0, v0)]

    w = w_in.astype(BF16)
    nat_cols = [(za0, LRU_WIDTH), *qkv_cols(0), (zb0, DIL_WIDTH), (qm0, MEM_WIDTH), (zm0, MEM_WIDTH)]
    xa, hb = _norm_matmul(x, norm_g, w, [(F32, [(xa0, LRU_WIDTH)]), (BF16, nat_cols)])
    za_col = 0
    qkv0_col = LRU_WIDTH // DIL_WIDTH
    zb_col = (LRU_WIDTH + 3 * DIL_WIDTH) // DIL_WIDTH
    qm_col = (LRU_WIDTH + 4 * DIL_WIDTH) // MEM_WIDTH
    zm_col = qm_col + 1
    qkv = [(hb[None], qkv0_col)]
    for gi in range(1, len(DIL_GROUPS)):
        qkv.append((_norm_matmul(x, norm_g, w, [(BF16, qkv_cols(gi))], dil=DIL_GROUPS[gi][1]), 0))
    n_mem = mem.shape[0] // batch
    mem_kv, = _norm_matmul(mem, mem_norm_g, w_mem_kv.astype(BF16), [(BF16, [(0, 2 * MEM_WIDTH)])])

    ya = _rglru(xa, hb, za_col, conv_w, conv_b, _pack_block_diag(ga_w), ga_b, _pack_block_diag(gx_w), gx_b,
                lam, batch, seq)
    attn = [_dil_attn(arr, col0, gi, batch, seq) for gi, (arr, col0) in enumerate(qkv)]
    ym = _mem_attn(hb, qm_col, mem_kv, hb, zm_col, batch, seq, n_mem)
    return _hawk_out(ya, [o for o, _ in attn], [l for _, l in attn], hb, zb_col, ym,
                     w_out.astype(BF16), x)


def _nsa_layer(x, mem, batch, seq, norm_g, w_in, pe_k, pe_v, phik_w1, phik_w2, phiv_w1, phiv_w2,
               mem_norm_g, w_mem_kv, w_out, final_g):
    kv0 = NSA_WIDTH
    gl0 = kv0 + 6 * NSA_KV
    z0 = gl0 + 3 * NSA_HEADS
    qm0 = z0 + NSA_WIDTH
    zm0 = qm0 + MEM_WIDTH
    wb = w_in.astype(BF16)
    gl_w = jnp.pad(wb[:, gl0:z0], ((0, 0), (0, LANES - 3 * NSA_HEADS)))
    q_w = wb[:, 0:kv0] * jnp.asarray(NSA_HEAD_DIM ** -0.5, BF16)
    w_all = jnp.concatenate([q_w, wb[:, z0:qm0], wb[:, kv0 + 2 * NSA_KV:gl0], wb[:, qm0:zm0 + MEM_WIDTH],
                             gl_w, wb[:, kv0:kv0 + 2 * NSA_KV]], axis=1)
    f32_width = LANES + 2 * NSA_KV
    bf16_width = w_all.shape[1] - f32_width
    nb, nf = _norm_matmul(x, norm_g, w_all, [(BF16, [(0, bf16_width)]), (F32, [(bf16_width, f32_width)])])
    gl_col, kc_col, vc_col = 0, 1, 2
    z_col = 1
    kv_col0 = 2 * NSA_WIDTH
    qm_col = (kv_col0 + 4 * NSA_KV) // MEM_WIDTH
    zm_col = qm_col + 1
    n_mem = mem.shape[0] // batch
    mem_kv, = _norm_matmul(mem, mem_norm_g, w_mem_kv.astype(BF16), [(BF16, [(0, 2 * MEM_WIDTH)])])

    k_cmp, v_cmp_t = _compress(nf, kc_col, vc_col, pe_k, pe_v, phik_w1, phik_w2, phiv_w1, phiv_w2,
                               batch, seq)
    ym = _mem_attn(nb, qm_col, mem_kv, nb, zm_col, batch, seq, n_mem)
    return _nsa_attn(nb, kv_col0, z_col, k_cmp, v_cmp_t, nf, gl_col, ym, w_out.astype(BF16), x, final_g,
                     batch, seq)


def kernel(x, mem, hawk_norm, hawk_w_in, hawk_conv_w, hawk_conv_b, hawk_gate_a_w, hawk_gate_a_b,
           hawk_gate_x_w, hawk_gate_x_b, hawk_lambda, hawk_mem_norm, hawk_w_mem_kv, hawk_w_out,
           nsa_norm, nsa_w_in, nsa_pe_k, nsa_pe_v, nsa_phi_k_w1, nsa_phi_k_w2, nsa_phi_v_w1,
           nsa_phi_v_w2, nsa_mem_norm, nsa_w_mem_kv, nsa_w_out, final_norm):
    batch, seq, d = x.shape
    assert hawk_norm.shape[0] == 1 and nsa_norm.shape[0] == 1, "one layer of each kind"
    assert seq % (ATTN_BLOCK * DIL_GROUPS[-1][1]) == 0
    x2 = x.reshape(batch * seq, d)
    mem2 = mem.reshape(batch * mem.shape[1], d)
    x2 = _hawk_layer(x2, mem2, batch, seq, hawk_norm[0], hawk_w_in[0], hawk_conv_w[0], hawk_conv_b[0],
                     hawk_gate_a_w[0], hawk_gate_a_b[0].reshape(-1), hawk_gate_x_w[0],
                     hawk_gate_x_b[0].reshape(-1), hawk_lambda[0], hawk_mem_norm[0], hawk_w_mem_kv[0],
                     hawk_w_out[0])
    out = _nsa_layer(x2, mem2, batch, seq, nsa_norm[0], nsa_w_in[0], nsa_pe_k[0], nsa_pe_v[0],
                     nsa_phi_k_w1[0], nsa_phi_k_w2[0], nsa_phi_v_w1[0], nsa_phi_v_w2[0],
                     nsa_mem_norm[0], nsa_w_mem_kv[0], nsa_w_out[0], final_norm)
    return out.reshape(batch, seq, d)
```

```python
import functools

import numpy as np
import jax
import jax.numpy as jnp
from jax import lax
from jax.experimental import pallas as pl
from jax.experimental.pallas import tpu as pltpu

F32 = jnp.float32
BF16 = jnp.bfloat16

NORM_EPS = 1e-6
NEG_INF = -1e30
LOG2_E = 1.4426950408889634
LANES = 128
SUBLANES = 8
BF16_SUBLANES = 16
ATTN_BLOCK = 128
VMEM_LIMIT = 56 * 1024 * 1024

LRU_WIDTH = 1024
LRU_BLOCKS = 16
LRU_BLOCK_DIM = LRU_WIDTH // LRU_BLOCKS
LRU_PACK = 256
CONV_WIDTH = 4
LRU_C = 8.0

DIL_GROUPS = ((128, 1), (512, 4), (2048, 16))
DIL_HEADS = 4
DIL_HEAD_DIM = 128
DIL_WIDTH = DIL_HEADS * DIL_HEAD_DIM
DIL_QKV = len(DIL_GROUPS) * DIL_WIDTH
LSE_LANES = LANES // DIL_HEADS
HAWK_OUT_ROW_PARTS = 4
MAX_ROW_PARTS = 8

MEM_HEADS = 4
MEM_HEAD_DIM = 64
MEM_WIDTH = MEM_HEADS * MEM_HEAD_DIM

NSA_HEADS = 16
NSA_KV_GROUPS = 2
NSA_R = NSA_HEADS // NSA_KV_GROUPS
NSA_HEAD_DIM = 64
NSA_WIDTH = NSA_HEADS * NSA_HEAD_DIM
NSA_KV = NSA_KV_GROUPS * NSA_HEAD_DIM
CMP_BLOCK = 32
CMP_STRIDE = 16
SLC_BLOCK = 64
SLC_TOP_N = 8
WIN_SIZE = 512
PHI_HIDDEN = 256
SEL_FORCE = 1e6
CMP_PER_SLC = SLC_BLOCK // CMP_STRIDE


def _alibi_slopes(n):
    return [float(v) for v in np.exp2(-8.0 * np.arange(1, n + 1) / n).astype(np.float32)]


def _params(*semantics):
    return pltpu.CompilerParams(dimension_semantics=semantics, vmem_limit_bytes=VMEM_LIMIT)


def _silu(z):
    return z * jax.nn.sigmoid(z)


def _dot_t(a, b):
    return lax.dot_general(a, b, (((1,), (1,)), ((), ())), preferred_element_type=F32)


def _rms_norm_rows(x, g):
    ms = jnp.mean(x * x, axis=-1, keepdims=True)
    return (x * lax.rsqrt(ms + NORM_EPS) * g).astype(BF16)


def _norm_matmul_kernel(*refs, dil, pieces, row_parts):
    x_ref, g_ref = refs[:2]
    w_refs, o_refs = refs[2:2 + sum(pieces)], refs[2 + sum(pieces):]
    tm, k = x_ref.shape
    rows = tm // row_parts
    per = rows // dil
    for part in range(row_parts):
        x = x_ref[part * rows:(part + 1) * rows, :]
        if dil == 1:
            xn = _rms_norm_rows(x, g_ref[...])
        else:
            xn = x * lax.rsqrt(jnp.mean(x * x, axis=-1, keepdims=True) + NORM_EPS) * g_ref[...]
            xn = jnp.swapaxes(xn.reshape(per, dil, k), 0, 1).reshape(rows, k).astype(BF16)
        first = 0
        for o_ref, n_pieces in zip(o_refs, pieces):
            col = 0
            for w_ref in w_refs[first:first + n_pieces]:
                width = w_ref.shape[1]
                res = jnp.dot(xn, w_ref[...], preferred_element_type=F32).astype(o_ref.dtype)
                if dil == 1:
                    o_ref[part * rows:(part + 1) * rows, col:col + width] = res
                else:
                    for c in range(dil):
                        o_ref[c, part * per:(part + 1) * per, col:col + width] = res[c * per:(c + 1) * per]
                col += width
            first += n_pieces


def _norm_matmul(x, g, w, outs, dil=1, tm=1024):
    m, k = x.shape
    tm = min(tm, m)
    assert m % tm == 0 and k % LANES == 0
    pieces = [len(cols) for _, cols in outs]
    widths = [sum(width for _, width in cols) for _, cols in outs]
    w_specs = []
    for _, cols in outs:
        for col, width in cols:
            assert col % width == 0 and width % LANES == 0
            w_specs.append(pl.BlockSpec((k, width), functools.partial(lambda i, j: (0, j), j=col // width)))
    resident = [pl.BlockSpec((1, k), lambda i: (0, 0))] + w_specs
    operands = (g.reshape(1, k),) + (w,) * len(w_specs)
    row_parts = min(dil, MAX_ROW_PARTS)
    assert tm % (dil * row_parts * SUBLANES) == 0
    kernel = functools.partial(_norm_matmul_kernel, dil=dil, pieces=pieces, row_parts=row_parts)
    if dil == 1:
        return pl.pallas_call(
            kernel,
            grid=(m // tm,),
            in_specs=[pl.BlockSpec((tm, k), lambda i: (i, 0))] + resident,
            out_specs=[pl.BlockSpec((tm, width), lambda i: (i, 0)) for width in widths],
            out_shape=[jax.ShapeDtypeStruct((m, width), dtype) for width, (dtype, _) in zip(widths, outs)],
            compiler_params=_params("arbitrary"),
            name="norm_matmul",
        )(x, *operands)
    per = tm // dil
    (out_dtype, _), = outs
    n, = widths
    assert tm % dil == 0 and per % BF16_SUBLANES == 0
    return pl.pallas_call(
        kernel,
        grid=(m // tm,),
        in_specs=[pl.BlockSpec((tm, k), lambda i: (i, 0))] + resident,
        out_specs=pl.BlockSpec((dil, per, n), lambda i: (0, i, 0)),
        out_shape=jax.ShapeDtypeStruct((dil, m // dil, n), out_dtype),
        compiler_params=_params("arbitrary"),
        name="norm_matmul_dil",
    )(x, *operands)


def _rglru_kernel(xa_ref, za_ref, cw_ref, cb_ref, wa_ref, ba_ref, wx_ref, bx_ref, lam_ref,
                  o_ref, xpad_ref, h_ref):
    t = pl.program_id(1)
    tt, width = xa_ref.shape
    halo = SUBLANES

    @pl.when(t == 0)
    def _():
        xpad_ref[0:halo, :] = jnp.zeros((halo, width), F32)
        h_ref[...] = jnp.zeros_like(h_ref)

    x = xa_ref[...]
    xpad_ref[halo:halo + tt, :] = x
    cw = cw_ref[...]
    y = cw[CONV_WIDTH - 1:CONV_WIDTH] * x
    for k in range(1, CONV_WIDTH):
        y = y + cw[CONV_WIDTH - 1 - k:CONV_WIDTH - k] * xpad_ref[halo - k:halo - k + tt, :]
    y = y + cb_ref[...]
    xpad_ref[0:halo, :] = x[tt - halo:tt, :]

    yb = y.astype(BF16)
    r_parts, i_parts = [], []
    for p in range(width // LRU_PACK):
        ys = yb[:, p * LRU_PACK:(p + 1) * LRU_PACK]
        r_parts.append(jnp.dot(ys, wa_ref[p], preferred_element_type=F32))
        i_parts.append(jnp.dot(ys, wx_ref[p], preferred_element_type=F32))
    r = jax.nn.sigmoid(jnp.concatenate(r_parts, axis=1) + ba_ref[...])
    gi = jax.nn.sigmoid(jnp.concatenate(i_parts, axis=1) + bx_ref[...])

    nl = -lam_ref[...]
    softplus = jnp.maximum(nl, 0.0) + jnp.log1p(jnp.exp(-jnp.abs(nl)))
    log_a = (-LRU_C) * r * softplus
    a = jnp.exp(log_a)
    w = -jnp.tanh(log_a) * (a * a + 1.0)
    mult = jnp.where(w > 0.0, w * lax.rsqrt(w), 0.0)
    b = y * gi * mult
    first = (lax.broadcasted_iota(jnp.int32, (SUBLANES, width), 0) == 0) & (t == 0)
    b = jnp.concatenate([jnp.where(first, (y * gi)[0:SUBLANES], b[0:SUBLANES]), b[SUBLANES:]], axis=0)

    blocks = width // LANES

    def time_major(v):
        cols = jnp.stack([v[:, j * LANES:(j + 1) * LANES] for j in range(blocks)], axis=0)
        return jnp.swapaxes(cols, 0, 1)

    a_t, b_t = time_major(a), time_major(b)
    state = h_ref[...].reshape(blocks, LANES)
    steps = []
    for step in range(tt):
        state = a_t[step] * state + b_t[step]
        steps.append(state)
    h_ref[...] = state.reshape(1, width)
    h_cols = jnp.swapaxes(jnp.stack(steps, axis=0), 0, 1)
    h = jnp.concatenate([h_cols[j] for j in range(blocks)], axis=1)
    o_ref[...] = (h * _silu(za_ref[...].astype(F32))).astype(o_ref.dtype)


def _rglru(xa, za_src, za_col, conv_w, conv_b, wa, ba, wx, bx, lam, batch, seq, tt=1024):
    width = LRU_WIDTH
    nt = seq // tt
    packs = width // LRU_PACK
    vec = pl.BlockSpec((1, width), lambda b, t: (0, 0))
    gate_w = pl.BlockSpec((packs, LRU_PACK, LRU_PACK), lambda b, t: (0, 0, 0))
    return pl.pallas_call(
        _rglru_kernel,
        grid=(batch, nt),
        in_specs=[pl.BlockSpec((tt, width), lambda b, t: (b * nt + t, 0)),
                  pl.BlockSpec((tt, width), lambda b, t: (b * nt + t, za_col)),
                  pl.BlockSpec((CONV_WIDTH, width), lambda b, t: (0, 0)),
                  vec, gate_w, vec, gate_w, vec, vec],
        out_specs=pl.BlockSpec((tt, width), lambda b, t: (b * nt + t, 0)),
        out_shape=jax.ShapeDtypeStruct((batch * seq, width), BF16),
        scratch_shapes=[pltpu.VMEM((tt + SUBLANES, width), F32), pltpu.VMEM((1, width), F32)],
        compiler_params=_params("arbitrary", "arbitrary"),
        name="rglru",
    )(xa, za_src, conv_w, conv_b.reshape(1, width), wa, ba.reshape(1, width), wx, bx.reshape(1, width),
      lam.reshape(1, width))


def _pack_block_diag(w):
    per = LRU_PACK // LRU_BLOCK_DIM
    w = w.reshape(LRU_BLOCKS // per, per, LRU_BLOCK_DIM, LRU_BLOCK_DIM)
    eye = jnp.eye(per, dtype=w.dtype)
    packed = w[:, :, :, None, :] * eye[None, :, None, :, None]
    return packed.reshape(LRU_BLOCKS // per, LRU_PACK, LRU_PACK).astype(BF16)


def _dil_attn_kernel(*refs, slopes, pos_scale, max_dist, has_halo, dil, n_cls, n_blk):
    if has_halo:
        q_ref, kh_ref, k_ref, vh_ref, v_ref = refs[:5]
        out_refs = refs[5:]
    else:
        q_ref, k_ref, v_ref = refs[:3]
        out_refs = refs[3:]
    n_out = DIL_HEADS + 1
    dst_refs = out_refs[:n_out]
    stage_refs = out_refs[n_out:] if dil > 1 else dst_refs
    first_super = pl.program_id(1) == 0
    cls0 = pl.program_id(2) * n_cls
    blk = ATTN_BLOCK
    scale = DIL_HEAD_DIM ** -0.5

    def band(width, halo_live):
        row = lax.broadcasted_iota(jnp.int32, (blk, width), 0)
        col = lax.broadcasted_iota(jnp.int32, (blk, width), 1)
        dist = (width - blk) + row - col
        valid = (dist >= 0) & (dist <= max_dist)
        if halo_live is not None:
            valid = valid & ((col >= blk) | halo_live)
        distf = (dist * pos_scale).astype(F32)
        return [jnp.where(valid, (-slope / scale) * distf, NEG_INF) for slope in slopes]

    bias_inner = band(2 * blk, None) if n_blk > 1 else None
    bias_first = band(2 * blk, jnp.logical_not(first_super)) if has_halo else band(blk, None)

    def scores(cc, jb):
        cur = slice(jb * blk, (jb + 1) * blk)
        bias = bias_inner if jb > 0 else bias_first
        out = []
        for h in range(DIL_HEADS):
            hs = slice(h * DIL_HEAD_DIM, (h + 1) * DIL_HEAD_DIM)
            q = q_ref[cc, cur, hs]
            if jb > 0:
                k = k_ref[cc, (jb - 1) * blk:(jb + 1) * blk, hs]
                v = v_ref[cc, (jb - 1) * blk:(jb + 1) * blk, hs]
            elif has_halo:
                k = jnp.concatenate([kh_ref[cc, :, hs], k_ref[cc, cur, hs]], axis=0)
                v = jnp.concatenate([vh_ref[cc, :, hs], v_ref[cc, cur, hs]], axis=0)
            else:
                k, v = k_ref[cc, cur, hs], v_ref[cc, cur, hs]
            out.append((_dot_t(q, k) + bias[h], v))
        return out

    def finish(cc, jb, pairs):
        where = (cls0 + cc, slice(jb * blk, (jb + 1) * blk)) if dil > 1 else (slice(jb * blk, (jb + 1) * blk),)
        lses = []
        for h, (s, v) in enumerate(pairs):
            m = jnp.max(s, axis=-1, keepdims=True)
            e = jnp.exp2((s - m) * (scale * LOG2_E))
            den = jnp.sum(e, axis=-1, keepdims=True)
            o = jnp.dot(e.astype(BF16), v, preferred_element_type=F32) / den
            stage_refs[h][where] = o.astype(stage_refs[h].dtype)
            lses.append(jnp.broadcast_to(m * scale + jnp.log(den), (blk, LSE_LANES)))
        stage_refs[DIL_HEADS][where] = jnp.concatenate(lses, axis=1)

    pending = None
    for cc in range(n_cls):
        for jb in range(n_blk):
            pairs = scores(cc, jb)
            if pending is not None:
                finish(*pending)
            pending = (cc, jb, pairs)
    finish(*pending)

    if dil > 1:
        @pl.when(pl.program_id(2) == pl.num_programs(2) - 1)
        def _():
            for stage, dst in zip(stage_refs, dst_refs):
                dst[...] = jnp.swapaxes(stage[...], 0, 1).reshape(dst.shape).astype(dst.dtype)


def _dil_attn(qkv, col0, gi, batch, seq, work=8):
    window, dil = DIL_GROUPS[gi]
    sub = seq // dil
    nb = sub // ATTN_BLOCK
    n_blk = min(work, nb)
    n_cls = min(work // n_blk, dil)
    n_super = nb // n_blk
    has_halo = n_super > 1
    span = n_blk * ATTN_BLOCK
    slopes = _alibi_slopes(len(DIL_GROUPS) * DIL_HEADS)[gi * DIL_HEADS:(gi + 1) * DIL_HEADS]
    cur = lambda col: pl.BlockSpec((n_cls, span, DIL_WIDTH), lambda b, i, c: (c, b * n_super + i, col0 + col))
    halo = lambda col: pl.BlockSpec(
        (n_cls, ATTN_BLOCK, DIL_WIDTH),
        lambda b, i, c: (c, jnp.maximum((b * n_super + i) * n_blk - 1, 0), col0 + col))
    if has_halo:
        in_specs = [cur(0), halo(1), cur(1), halo(2), cur(2)]
    else:
        in_specs = [cur(0), cur(1), cur(2)]
    n_out = DIL_HEADS + 1
    *o, lse = pl.pallas_call(
        functools.partial(_dil_attn_kernel, slopes=slopes, pos_scale=dil, max_dist=window // dil,
                          has_halo=has_halo, dil=dil, n_cls=n_cls, n_blk=n_blk),
        grid=(batch, n_super, dil // n_cls),
        in_specs=in_specs,
        out_specs=[pl.BlockSpec((span * dil, LANES), lambda b, i, c: (b * n_super + i, 0))] * n_out,
        out_shape=[jax.ShapeDtypeStruct((batch * seq, LANES), BF16)] * DIL_HEADS
                  + [jax.ShapeDtypeStruct((batch * seq, LANES), F32)],
        scratch_shapes=[pltpu.VMEM((dil, span, LANES), F32)] * (n_out if dil > 1 else 0),
        compiler_params=_params("arbitrary", "arbitrary", "arbitrary"),
        name=f"dil_attn_d{dil}",
    )(*([qkv] * len(in_specs)))
    return o, lse


def _mem_attn_kernel(q_ref, k_ref, v_ref, z_ref, o_ref, ks_ref, vt_ref):
    hd = MEM_HEAD_DIM
    n_mem = k_ref.shape[0]

    @pl.when(pl.program_id(1) == 0)
    def _():
        ks_ref[...] = (k_ref[...].astype(F32) * (hd ** -0.5)).astype(BF16)
        vt = v_ref[...].astype(F32).T.astype(BF16)
        for h in range(MEM_HEADS):
            vt_ref[h, 0:hd, :] = vt[h * hd:(h + 1) * hd, :]
            vt_ref[h, hd:, :] = jnp.ones((vt_ref.shape[1] - hd, n_mem), BF16)

    heads = [slice(h * hd, (h + 1) * hd) for h in range(MEM_HEADS)]
    scores = [_dot_t(ks_ref[:, hs], q_ref[:, hs]) for hs in heads]
    outs = []
    for h, s in enumerate(scores):
        e = jnp.exp(s - jnp.max(s, axis=0, keepdims=True)).astype(BF16)
        acc = jnp.dot(vt_ref[h], e, preferred_element_type=F32)
        outs.append(acc[0:hd, :] / acc[hd:hd + 1, :])
    o = jnp.concatenate(outs, axis=0).T
    o_ref[...] = (o * _silu(z_ref[...].astype(F32))).astype(o_ref.dtype)


def _mem_attn(qsrc, q_col, kv, zsrc, z_col, batch, seq, n_mem, tq=1024):
    nq = seq // tq
    return pl.pallas_call(
        _mem_attn_kernel,
        grid=(batch, nq),
        in_specs=[pl.BlockSpec((tq, MEM_WIDTH), lambda b, i: (b * nq + i, q_col)),
                  pl.BlockSpec((n_mem, MEM_WIDTH), lambda b, i: (b, 0)),
                  pl.BlockSpec((n_mem, MEM_WIDTH), lambda b, i: (b, 1)),
                  pl.BlockSpec((tq, MEM_WIDTH), lambda b, i: (b * nq + i, z_col))],
        out_specs=pl.BlockSpec((tq, MEM_WIDTH), lambda b, i: (b * nq + i, 0)),
        out_shape=jax.ShapeDtypeStruct((batch * seq, MEM_WIDTH), BF16),
        scratch_shapes=[pltpu.VMEM((n_mem, MEM_WIDTH), BF16),
                        pltpu.VMEM((MEM_HEADS, MEM_HEAD_DIM + BF16_SUBLANES, n_mem), BF16)],
        compiler_params=_params("arbitrary", "arbitrary"),
        name="mem_attn",
    )(qsrc, kv, kv, zsrc)


def _hawk_out_kernel(*refs):
    n_groups = len(DIL_GROUPS)
    ya_ref = refs[0]
    o_refs = refs[1:1 + n_groups * DIL_HEADS]
    l_refs = refs[1 + n_groups * DIL_HEADS:1 + n_groups * (DIL_HEADS + 1)]
    zb_ref, ym_ref, w_ref, x_ref, out_ref = refs[1 + n_groups * (DIL_HEADS + 1):]
    a_end = LRU_WIDTH
    b_end = a_end + DIL_WIDTH
    rows = x_ref.shape[0] // HAWK_OUT_ROW_PARTS
    for part in range(HAWK_OUT_ROW_PARTS):
        rs = slice(part * rows, (part + 1) * rows)
        y = jnp.dot(ya_ref[rs, :], w_ref[0:a_end, :], preferred_element_type=F32)
        y = y + jnp.dot(ym_ref[rs, :], w_ref[b_end:b_end + MEM_WIDTH, :], preferred_element_type=F32)
        parts = []
        for h in range(DIL_HEADS):
            ls = [l[rs, h * LSE_LANES:h * LSE_LANES + 1] for l in l_refs]
            m = functools.reduce(jnp.maximum, ls)
            ws = [jnp.exp(l - m) for l in ls]
            num = sum(w * o_refs[gi * DIL_HEADS + h][rs, :].astype(F32) for gi, w in enumerate(ws))
            parts.append(num / sum(ws))
        yb = (jnp.concatenate(parts, axis=1) * _silu(zb_ref[rs, :].astype(F32))).astype(BF16)
        y = y + jnp.dot(yb, w_ref[a_end:b_end, :], preferred_element_type=F32)
        out_ref[rs, :] = x_ref[rs, :] + y


def _hawk_out(ya, os_, ls_, zb_src, zb_col, ym, w, x, tm=1024):
    m, d = x.shape
    row = lambda width, col=0: pl.BlockSpec((tm, width), lambda i: (i, col))
    heads = [o for group in os_ for o in group]
    return pl.pallas_call(
        _hawk_out_kernel,
        grid=(m // tm,),
        in_specs=[row(LRU_WIDTH)] + [row(DIL_HEAD_DIM)] * len(heads) + [row(LANES)] * len(ls_)
                 + [row(DIL_WIDTH, zb_col), row(MEM_WIDTH),
                    pl.BlockSpec(w.shape, lambda i: (0, 0)), row(d)],
        out_specs=row(d),
        out_shape=jax.ShapeDtypeStruct((m, d), F32),
        compiler_params=_params("arbitrary"),
        name="hawk_out",
    )(ya, *heads, *ls_, zb_src, ym, w, x)


def _compress_kernel(k_ref, v_ref, pe_ref, w1_ref, w2k_ref, w2vt_ref, ko_ref, vto_ref):
    n_bat, n_blk = ko_ref.shape[0], ko_ref.shape[1]
    seq = n_blk * CMP_STRIDE

    def hidden(which, src_ref):
        x = jnp.concatenate(
            [jnp.concatenate([src_ref[pl.ds(bi * seq + p, n_blk, stride=CMP_STRIDE), :]
                              for p in range(CMP_STRIDE)], axis=1) for bi in range(n_bat)], axis=0).astype(BF16)
        first = jnp.dot(x, w1_ref[which, 0], preferred_element_type=F32)
        second = jnp.dot(x, w1_ref[which, 1], preferred_element_type=F32)
        pe = (jnp.dot(pe_ref[which, 0], w1_ref[which, 0], preferred_element_type=F32)
              + jnp.dot(pe_ref[which, 1], w1_ref[which, 1], preferred_element_type=F32))
        nxt = pltpu.roll(second.reshape(n_bat, n_blk, second.shape[1]), n_blk - 1, axis=1)
        return _silu(first + nxt.reshape(second.shape) + pe[0:1, :]).astype(BF16)

    act_k, act_v = hidden(0, k_ref), hidden(1, v_ref)
    part = lambda a, g: a[:, g * PHI_HIDDEN:(g + 1) * PHI_HIDDEN]
    ks = jnp.concatenate([jnp.dot(part(act_k, g), w2k_ref[...], preferred_element_type=F32)
                          for g in range(NSA_KV_GROUPS)], axis=1)
    vts = jnp.concatenate([_dot_t(w2vt_ref[...], part(act_v, g)) for g in range(NSA_KV_GROUPS)],
                          axis=0)
    for bi in range(n_bat):
        ko_ref[bi] = ks[bi * n_blk:(bi + 1) * n_blk].astype(ko_ref.dtype)
        vto_ref[bi] = vts[:, bi * n_blk:(bi + 1) * n_blk].astype(vto_ref.dtype)


def _compress(src, k_col, v_col, pe_k, pe_v, k_w1, k_w2, v_w1, v_w2, batch, seq, n_bat=4):
    half = CMP_BLOCK // 2
    assert half == CMP_STRIDE and NSA_KV == LANES
    n_blk = seq // CMP_STRIDE
    hd = NSA_HEAD_DIM
    n_bat = min(n_bat, batch)
    assert batch % n_bat == 0
    w1 = jnp.stack([k_w1, v_w1]).reshape(2, 2, half, hd, PHI_HIDDEN).astype(BF16)
    zero = jnp.zeros_like(w1)
    per_group = [jnp.concatenate([w1 if g == col else zero for col in range(NSA_KV_GROUPS)], axis=-1)
                 for g in range(NSA_KV_GROUPS)]
    w1e = jnp.stack(per_group, axis=3).reshape(2, 2, half * NSA_KV, NSA_KV_GROUPS * PHI_HIDDEN)
    pe = jnp.stack([pe_k, pe_v]).reshape(2, 2, half, 1, hd)
    pe = jnp.broadcast_to(pe, (2, 2, half, NSA_KV_GROUPS, hd)).reshape(2, 2, 1, half * NSA_KV)
    pe = jnp.broadcast_to(pe, (2, 2, SUBLANES, half * NSA_KV)).astype(BF16)
    w2k = k_w2.astype(BF16)
    w2vt = v_w2.T.astype(BF16)
    whole = lambda a: pl.BlockSpec(a.shape, lambda b: (0,) * a.ndim)
    return pl.pallas_call(
        _compress_kernel,
        grid=(batch // n_bat,),
        in_specs=[pl.BlockSpec((n_bat * seq, LANES), lambda b: (b, k_col)),
                  pl.BlockSpec((n_bat * seq, LANES), lambda b: (b, v_col)),
                  whole(pe), whole(w1e), whole(w2k), whole(w2vt)],
        out_specs=[pl.BlockSpec((n_bat, n_blk, NSA_KV), lambda b: (b, 0, 0)),
                   pl.BlockSpec((n_bat, NSA_KV, n_blk), lambda b: (b, 0, 0))],
        out_shape=[jax.ShapeDtypeStruct((batch, n_blk, NSA_KV), BF16),
                   jax.ShapeDtypeStruct((batch, NSA_KV, n_blk), BF16)],
        compiler_params=_params("arbitrary"),
        name="compress",
    )(src, src, pe, w1e, w2k, w2vt)


KEY_CHUNK = 256
NSA_TQ = 256
SLOPE_PIECES = 3
N_FEATS = 2 * SLOPE_PIECES
FEAT_LANES = 32
KEY_COLS = NSA_HEAD_DIM + 2 * FEAT_LANES
N_SLC = 32


def _slope_pieces(slope):
    rest = np.float32(slope)
    pieces = []
    for _ in range(SLOPE_PIECES):
        p = np.float32(np.asarray(rest).astype(BF16))
        pieces.append(float(p))
        rest = np.float32(rest - p)
    return pieces


def _lane_table(lane, values):
    out = jnp.zeros(lane.shape, F32)
    for idx, v in enumerate(values):
        out = jnp.where(lane == idx, v, out)
    return out


def _key_feats(pos_hi, pos_lo, lane):
    return jnp.where(lane < SLOPE_PIECES, pos_hi, jnp.where(lane < N_FEATS, pos_lo, 0)).astype(F32)


def _tile_heads(x):
    return jnp.concatenate([x] * NSA_R, axis=1)


def _chunk_loop(lo, hi, body, init, widths=(4, 2, 1)):
    carry, start = init, lo
    for idx, w in enumerate(widths):
        count = (hi - start) // w

        def step(p, cr, start=start, w=w):
            first = start + p * w
            return body([first + j for j in range(w)], cr)

        if idx == 0:
            carry = lax.fori_loop(0, count, step, carry)
        else:
            carry = lax.cond(count > 0, functools.partial(step, 0), lambda cr: cr, carry)
        start = start + count * w
    return carry


def _nsa_kernel(q_ref, kc_ref, vct_ref, ksrc_ref, vsrc_ref, kwsrc_ref, vwsrc_ref, feat_ref, hot_ref,
                gl_ref, z_ref, ym_ref, w_ref, x_ref, fin_ref, o_ref,
                ks_ref, vst_ref, kw_ref, vwt_ref, s_ref, acc_ref, imp_ref):
    i = pl.program_id(1)
    tq = q_ref.shape[0]
    hd = NSA_HEAD_DIM
    n_cmp = kc_ref.shape[0]

    @pl.when(i == 0)
    def _():
        for g in range(NSA_KV_GROUPS):
            gs = slice(g * hd, (g + 1) * hd)
            for dst, src, tail in ((ks_ref, ksrc_ref, hot_ref[...]),
                                   (kw_ref, kwsrc_ref, jnp.zeros(hot_ref.shape, BF16))):
                dst[:, g * KEY_COLS:g * KEY_COLS + hd] = src[:, gs]
                dst[:, g * KEY_COLS + hd:g * KEY_COLS + hd + FEAT_LANES] = feat_ref[...]
                dst[:, g * KEY_COLS + hd + FEAT_LANES:(g + 1) * KEY_COLS] = tail
        for c in range(vst_ref.shape[0]):
            rows = slice(c * KEY_CHUNK, (c + 1) * KEY_CHUNK)
            vst_ref[c] = vsrc_ref[rows, :].astype(F32).T.astype(BF16)
            vwt_ref[c] = vwsrc_ref[rows, :].astype(F32).T.astype(BF16)

    slopes_all = _alibi_slopes(NSA_HEADS)
    gates_t = jax.nn.sigmoid(gl_ref[...]).T
    feat_lane = lax.broadcasted_iota(jnp.int32, (tq, FEAT_LANES), 1)
    no_sel = jnp.zeros((NSA_R * tq, FEAT_LANES), BF16)
    key_row = lax.broadcasted_iota(jnp.int32, (KEY_CHUNK, tq), 0)
    t_pos = i * tq + lax.broadcasted_iota(jnp.int32, (KEY_CHUNK, tq), 1)
    ones_rows = jnp.ones((BF16_SUBLANES, KEY_CHUNK), BF16)
    win_lo = jnp.maximum(i * tq - (WIN_SIZE - 1), 0) // KEY_CHUNK
    chunks_hi = (i * tq + tq - 1) // KEY_CHUNK + 1

    groups = range(NSA_KV_GROUPS)
    q_win, q_slc, o_cmp = [], [], []
    for g in groups:
        slopes = slopes_all[g * NSA_R:(g + 1) * NSA_R]
        gs = slice(g * hd, (g + 1) * hd)
        q_parts = []
        for r in range(NSA_R):
            qr = q_ref[:, (g * NSA_R + r) * hd:(g * NSA_R + r + 1) * hd]
            feats = _lane_table(feat_lane, _slope_pieces(slopes[r]) * 2).astype(BF16)
            q_parts.append(jnp.concatenate([qr, feats], axis=1))
        q_feat = jnp.concatenate(q_parts, axis=0)
        q_aug = jnp.concatenate([q_feat, no_sel], axis=1)

        n_row = lax.broadcasted_iota(jnp.int32, (n_cmp, tq), 0)
        t_cmp = i * tq + lax.broadcasted_iota(jnp.int32, (n_cmp, tq), 1)
        visible = t_cmp >= n_row * CMP_STRIDE + (CMP_BLOCK - 1)
        cfeat_row = lax.broadcasted_iota(jnp.int32, (n_cmp, 2 * FEAT_LANES), 0)
        cfeat_lane = lax.broadcasted_iota(jnp.int32, (n_cmp, 2 * FEAT_LANES), 1)
        kc_feats = _key_feats(cfeat_row * CMP_STRIDE, 0, cfeat_lane)
        kc_aug = jnp.concatenate([kc_ref[:, gs], kc_feats.astype(BF16)], axis=1)
        s = _dot_t(kc_aug, q_aug) + _tile_heads(jnp.where(visible, 0.0, NEG_INF))
        m = jnp.max(s, axis=0, keepdims=True)
        e = jnp.exp(s - m)
        t_one = i * tq + lax.broadcasted_iota(jnp.int32, (1, NSA_R * tq), 1) % tq
        any_visible = t_one >= (CMP_BLOCK - 1)
        p = e * jnp.where(any_visible, 1.0 / jnp.sum(e, axis=0, keepdims=True), 0.0)
        o_cmp.append(jnp.dot(vct_ref[gs, :], p.astype(BF16), preferred_element_type=F32))
        p_sum = p[:, 0:tq]
        for r in range(1, NSA_R):
            p_sum = p_sum + p[:, r * tq:(r + 1) * tq]

        band = p_sum + pltpu.roll(p_sum, 1, axis=0)
        for k in range(1, CMP_PER_SLC):
            band = band + pltpu.roll(p_sum, n_cmp - k, axis=0)
        halves = []
        for half in range(tq // LANES):
            imp_ref[...] = band[:, half * LANES:(half + 1) * LANES]
            halves.append(imp_ref[pl.ds(0, N_SLC, stride=CMP_PER_SLC), :])
        imp = jnp.concatenate(halves, axis=1)
        blk_j = lax.broadcasted_iota(jnp.int32, (N_SLC, tq), 0)
        cur = (i * tq + lax.broadcasted_iota(jnp.int32, (N_SLC, tq), 1)) // SLC_BLOCK
        forced = (blk_j == 0) | (blk_j == cur) | (blk_j == cur - 1)
        v_imp = jnp.where(forced, SEL_FORCE, jnp.where(blk_j > cur, -SEL_FORCE, imp))
        rank = jnp.zeros((N_SLC, tq), F32)
        for other in range(N_SLC):
            row = v_imp[other:other + 1, :]
            ahead = (row > v_imp) | ((row == v_imp) & (blk_j > other))
            rank = rank + jnp.where(ahead, 1.0, 0.0)
        sel_bias = jnp.where(rank < SLC_TOP_N, 0.0, NEG_INF)

        padded = jnp.concatenate([sel_bias, jnp.zeros((LANES - N_SLC, tq), F32)], axis=0)
        sel_t = padded.T[:, 0:FEAT_LANES].astype(BF16)
        q_win.append(q_aug)
        q_slc.append(jnp.concatenate([q_feat, jnp.concatenate([sel_t] * NSA_R, axis=0)], axis=1))

    def attend(q_brs, k_ref, key_cols, vt_ref, lo, hi, masked_from, mask_fn):
        def scores(cs, m_run, masked):
            starts = [pl.multiple_of(c * KEY_CHUNK, KEY_CHUNK) for c in cs]
            scs = [[_dot_t(k_ref[pl.ds(start, KEY_CHUNK), g * key_cols:(g + 1) * key_cols], q_brs[g])
                    for g in groups] for start in starts]
            for c, start, sc in zip(cs, starts, scs):
                if masked:
                    bias = _tile_heads(jnp.where(mask_fn(t_pos - (start + key_row)), 0.0, NEG_INF))
                    sc = [x + bias for x in sc]
                for g in groups:
                    s_ref[g, c] = sc[g]
                m_run = tuple(jnp.maximum(m_run[g], jnp.max(sc[g], axis=0, keepdims=True)) for g in groups)
            return m_run

        def weighted(cs, carry):
            for c in cs:
                for g in groups:
                    e = jnp.exp(s_ref[g, c] - m_rows[g]).astype(BF16)
                    v_ext = jnp.concatenate([vt_ref[c, g * hd:(g + 1) * hd, :], ones_rows], axis=0)
                    acc_ref[g] += jnp.dot(v_ext, e, preferred_element_type=F32)
            return carry

        m_rows = tuple(jnp.full((1, NSA_R * tq), NEG_INF, F32) for _ in groups)
        m_rows = _chunk_loop(lo, masked_from, functools.partial(scores, masked=False), m_rows)
        m_rows = _chunk_loop(masked_from, hi, functools.partial(scores, masked=True), m_rows)
        acc_ref[...] = jnp.zeros(acc_ref.shape, F32)
        _chunk_loop(lo, hi, weighted, 0)
        return [acc_ref[g, 0:hd, :] / acc_ref[g, hd:hd + 1, :] for g in groups]

    o_slc = attend(q_slc, ks_ref, KEY_COLS, vst_ref, 0, chunks_hi, (i * tq) // KEY_CHUNK,
                   lambda dist: dist >= 0)
    o_win = attend(q_win, kw_ref, KEY_COLS, vwt_ref, win_lo, chunks_hi, win_lo,
                   lambda dist: (dist >= 0) & (dist <= WIN_SIZE - 1))

    y = jnp.dot(ym_ref[...], w_ref[NSA_WIDTH:NSA_WIDTH + MEM_WIDTH, :], preferred_element_type=F32)
    for g in groups:
        def gate(kind):
            base = g * NSA_R * 3 + kind
            return jnp.concatenate([gates_t[base + 3 * r:base + 3 * r + 1, :] for r in range(NSA_R)], axis=1)

        o = gate(0) * o_cmp[g] + gate(1) * o_slc[g] + gate(2) * o_win[g]
        pairs = []
        for r in range(0, NSA_R, 2):
            two = jnp.concatenate([o[:, r * tq:(r + 1) * tq], o[:, (r + 1) * tq:(r + 2) * tq]], axis=0)
            pairs.append(two.T)
        cs = slice(g * NSA_R * hd, (g + 1) * NSA_R * hd)
        yo = (jnp.concatenate(pairs, axis=1) * _silu(z_ref[:, cs].astype(F32))).astype(BF16)
        y = y + jnp.dot(yo, w_ref[cs, :], preferred_element_type=F32)

    x = x_ref[...] + y
    ms = jnp.mean(x * x, axis=-1, keepdims=True)
    o_ref[...] = x * lax.rsqrt(ms + NORM_EPS) * fin_ref[...]


def _key_pos_feats(seq):
    assert seq // SLC_BLOCK == N_SLC <= FEAT_LANES
    pos = np.arange(seq)
    feats = np.zeros((seq, FEAT_LANES), np.float32)
    feats[:, 0:SLOPE_PIECES] = ((pos // SLC_BLOCK) * SLC_BLOCK)[:, None]
    feats[:, SLOPE_PIECES:N_FEATS] = (pos % SLC_BLOCK)[:, None]
    onehot = (np.arange(FEAT_LANES)[None, :] == (pos // SLC_BLOCK)[:, None]).astype(np.float32)
    return jnp.asarray(feats, BF16), jnp.asarray(onehot, BF16)


def _nsa_attn(nb, kv_col0, z_col, k_cmp, v_cmp_t, nf, gl_col, ym, w_out, x, final_g, batch, seq):
    tq = NSA_TQ
    nq = seq // tq
    d = x.shape[1]
    feats, onehot = _key_pos_feats(seq)
    kv_blk = kv_col0 // NSA_KV
    seq_cols = lambda col: pl.BlockSpec((seq, NSA_KV), lambda b, i: (b, kv_blk + col))
    const = lambda a: pl.BlockSpec(a.shape, lambda b, i: (0,) * a.ndim)
    per_batch = lambda a: pl.BlockSpec((None,) + a.shape[1:], lambda b, i: (b,) + (0,) * (a.ndim - 1))
    rows = lambda width, col=0: pl.BlockSpec((tq, width), lambda b, i: (b * nq + i, col))
    fin = final_g.reshape(1, d)
    return pl.pallas_call(
        _nsa_kernel,
        grid=(batch, nq),
        in_specs=[rows(NSA_WIDTH), per_batch(k_cmp), per_batch(v_cmp_t),
                  seq_cols(0), seq_cols(1), seq_cols(2), seq_cols(3), const(feats), const(onehot),
                  rows(LANES, gl_col), rows(NSA_WIDTH, z_col), rows(MEM_WIDTH), const(w_out), rows(d),
                  const(fin)],
        out_specs=rows(d),
        out_shape=jax.ShapeDtypeStruct((batch * seq, d), F32),
        scratch_shapes=[pltpu.VMEM((seq, NSA_KV_GROUPS * KEY_COLS), BF16),
                        pltpu.VMEM((seq // KEY_CHUNK, NSA_KV, KEY_CHUNK), BF16),
                        pltpu.VMEM((seq, NSA_KV_GROUPS * KEY_COLS), BF16),
                        pltpu.VMEM((seq // KEY_CHUNK, NSA_KV, KEY_CHUNK), BF16),
                        pltpu.VMEM((NSA_KV_GROUPS, seq // KEY_CHUNK, KEY_CHUNK, NSA_R * tq), F32),
                        pltpu.VMEM((NSA_KV_GROUPS, NSA_HEAD_DIM + BF16_SUBLANES, NSA_R * tq), F32),
                        pltpu.VMEM((seq // CMP_STRIDE, LANES), F32)],
        compiler_params=_params("arbitrary", "arbitrary"),
        name="nsa_attn",
    )(nb, k_cmp, v_cmp_t, nb, nb, nb, nb, feats, onehot, nf, nb, ym, w_out, x, fin)


def _hawk_layer(x, mem, batch, seq, norm_g, w_in, conv_w, conv_b, ga_w, ga_b, gx_w, gx_b, lam,
                mem_norm_g, w_mem_kv, w_out):
    xa0, za0 = 0, LRU_WIDTH
    q0 = 2 * LRU_WIDTH
    k0, v0 = q0 + DIL_QKV, q0 + 2 * DIL_QKV
    zb0 = q0 + 3 * DIL_QKV
    qm0 = zb0 + DIL_WIDTH
    zm0 = qm0 + MEM_WIDTH

    def qkv_cols(gi):
        return [(base + gi * DIL_WIDTH, DIL_WIDTH) for base in (q0, k0, v0)]

    w = w_in.astype(BF16)
    nat_cols = [(za0, LRU_WIDTH), *qkv_cols(0), (zb0, DIL_WIDTH), (qm0, MEM_WIDTH), (zm0, MEM_WIDTH)]
    xa, hb = _norm_matmul(x, norm_g, w, [(F32, [(xa0, LRU_WIDTH)]), (BF16, nat_cols)])
    za_col = 0
    qkv0_col = LRU_WIDTH // DIL_WIDTH
    zb_col = (LRU_WIDTH + 3 * DIL_WIDTH) // DIL_WIDTH
    qm_col = (LRU_WIDTH + 4 * DIL_WIDTH) // MEM_WIDTH
    zm_col = qm_col + 1
    qkv = [(hb[None], qkv0_col)]
    for gi in range(1, len(DIL_GROUPS)):
        qkv.append((_norm_matmul(x, norm_g, w, [(BF16, qkv_cols(gi))], dil=DIL_GROUPS[gi][1]), 0))
    n_mem = mem.shape[0] // batch
    mem_kv, = _norm_matmul(mem, mem_norm_g, w_mem_kv.astype(BF16), [(BF16, [(0, 2 * MEM_WIDTH)])])

    ya = _rglru(xa, hb, za_col, conv_w, conv_b, _pack_block_diag(ga_w), ga_b, _pack_block_diag(gx_w), gx_b,
                lam, batch, seq)
    attn = [_dil_attn(arr, col0, gi, batch, seq) for gi, (arr, col0) in enumerate(qkv)]
    ym = _mem_attn(hb, qm_col, mem_kv, hb, zm_col, batch, seq, n_mem)
    return _hawk_out(ya, [o for o, _ in attn], [l for _, l in attn], hb, zb_col, ym,
                     w_out.astype(BF16), x)


def _nsa_layer(x, mem, batch, seq, norm_g, w_in, pe_k, pe_v, phik_w1, phik_w2, phiv_w1, phiv_w2,
               mem_norm_g, w_mem_kv, w_out, final_g):
    kv0 = NSA_WIDTH
    gl0 = kv0 + 6 * NSA_KV
    z0 = gl0 + 3 * NSA_HEADS
    qm0 = z0 + NSA_WIDTH
    zm0 = qm0 + MEM_WIDTH
    wb = w_in.astype(BF16)
    gl_w = jnp.pad(wb[:, gl0:z0], ((0, 0), (0, LANES - 3 * NSA_HEADS)))
    q_w = wb[:, 0:kv0] * jnp.asarray(NSA_HEAD_DIM ** -0.5, BF16)
    w_all = jnp.concatenate([q_w, wb[:, z0:qm0], wb[:, kv0 + 2 * NSA_KV:gl0], wb[:, qm0:zm0 + MEM_WIDTH],
                             gl_w, wb[:, kv0:kv0 + 2 * NSA_KV]], axis=1)
    f32_width = LANES + 2 * NSA_KV
    bf16_width = w_all.shape[1] - f32_width
    nb, nf = _norm_matmul(x, norm_g, w_all, [(BF16, [(0, bf16_width)]), (F32, [(bf16_width, f32_width)])])
    gl_col, kc_col, vc_col = 0, 1, 2
    z_col = 1
    kv_col0 = 2 * NSA_WIDTH
    qm_col = (kv_col0 + 4 * NSA_KV) // MEM_WIDTH
    zm_col = qm_col + 1
    n_mem = mem.shape[0] // batch
    mem_kv, = _norm_matmul(mem, mem_norm_g, w_mem_kv.astype(BF16), [(BF16, [(0, 2 * MEM_WIDTH)])])

    k_cmp, v_cmp_t = _compress(nf, kc_col, vc_col, pe_k, pe_v, phik_w1, phik_w2, phiv_w1, phiv_w2,
                               batch, seq)
    ym = _mem_attn(nb, qm_col, mem_kv, nb, zm_col, batch, seq, n_mem)
    return _nsa_attn(nb, kv_col0, z_col, k_cmp, v_cmp_t, nf, gl_col, ym, w_out.astype(BF16), x, final_g,
                     batch, seq)


def kernel(x, mem, hawk_norm, hawk_w_in, hawk_conv_w, hawk_conv_b, hawk_gate_a_w, hawk_gate_a_b,
           hawk_gate_x_w, hawk_gate_x_b, hawk_lambda, hawk_mem_norm, hawk_w_mem_kv, hawk_w_out,
           nsa_norm, nsa_w_in, nsa_pe_k, nsa_pe_v, nsa_phi_k_w1, nsa_phi_k_w2, nsa_phi_v_w1,
           nsa_phi_v_w2, nsa_mem_norm, nsa_w_mem_kv, nsa_w_out, final_norm):
    batch, seq, d = x.shape
    assert hawk_norm.shape[0] == 1 and nsa_norm.shape[0] == 1, "one layer of each kind"
    assert seq % (ATTN_BLOCK * DIL_GROUPS[-1][1]) == 0
    x2 = x.reshape(batch * seq, d)
    mem2 = mem.reshape(batch * mem.shape[1], d)
    x2 = _hawk_layer(x2, mem2, batch, seq, hawk_norm[0], hawk_w_in[0], hawk_conv_w[0], hawk_conv_b[0],
                     hawk_gate_a_w[0], hawk_gate_a_b[0].reshape(-1), hawk_gate_x_w[0],
                     hawk_gate_x_b[0].reshape(-1), hawk_lambda[0], hawk_mem_norm[0], hawk_w_mem_kv[0],
                     hawk_w_out[0])
    out = _nsa_layer(x2, mem2, batch, seq, nsa_norm[0], nsa_w_in[0], nsa_pe_k[0], nsa_pe_v[0],
                     nsa_phi_k_w1[0], nsa_phi_k_w2[0], nsa_phi_v_w1[0], nsa_phi_v_w2[0],
                     nsa_mem_norm[0], nsa_w_mem_kv[0], nsa_w_out[0], final_norm)
    return out.reshape(batch, seq, d)
```
